```python
import jax
import jax.numpy as jnp
from jax import lax
import numpy as np

D_MODEL = 2048
BATCH = 8
SEQ = 4096
DEPTH = 2

GRID_W = 64
CTX_LEN = 256
N_EVEN = (DEPTH + 1) // 2
N_ODD = DEPTH // 2
EPS = 1e-6
N_MOD = 6

CHUNK = 128
A_HEADS = 8
A_HEAD_DIM = 128
A_WIDTH = A_HEADS * A_HEAD_DIM
B_WIDTH = 1024
B_CONV = 31
MIX_IN = 2 * A_WIDTH + 2 * B_WIDTH
MIX_OUT = A_WIDTH + B_WIDTH

MLA_HEADS = 16
Q_LORA = 768
KV_LORA = 512
QK_NOPE = 128
QK_ROPE = 64
V_DIM = 128
ROPE_THETA = 10000.0
Q_BLOCK = 128
MLA_IN = Q_LORA + KV_LORA + QK_ROPE

D_FF = 5632
FFN_CONV = 3

kernel_name = 'hybrid_gmlp_conformer_mla_dit_block'


def rmsnorm(x, g):
    xf = x.astype(jnp.float32)
    y = xf * lax.rsqrt(jnp.mean(xf * xf, axis=-1, keepdims=True) + EPS)
    return (y * g.astype(jnp.float32)).astype(x.dtype)


def layernorm(x, g, b):
    xf = x.astype(jnp.float32)
    mu = jnp.mean(xf, axis=-1, keepdims=True)
    var = jnp.mean(jnp.square(xf - mu), axis=-1, keepdims=True)
    y = (xf - mu) * lax.rsqrt(var + EPS)
    return (y * g.astype(jnp.float32) + b.astype(jnp.float32)).astype(x.dtype)


def modulate(h, shift, scale):
    return h * (1 + scale) + shift


def adaln(cvec, w, b):
    m = jax.nn.silu(cvec) @ w + b
    return jnp.split(m, N_MOD, axis=-1)


def depthwise_conv(x, w, b):
    pad = (w.shape[0] - 1) // 2
    y = lax.conv_general_dilated(x, w[:, None, :].astype(x.dtype), window_strides=(1,), padding=[(pad, pad)], dimension_numbers=('NWC', 'WIO', 'NWC'), feature_group_count=x.shape[-1])
    return y + b.astype(x.dtype)


def chunk_gmlp(z, ln_g, ln_b, w_s, b_s):
    z = jax.nn.gelu(z)
    u, v = z[..., :A_WIDTH], z[..., A_WIDTH:]
    v = layernorm(v, ln_g, ln_b)
    bn, L, _ = v.shape
    v = v.reshape(bn, L // CHUNK, CHUNK, A_HEADS, A_HEAD_DIM)
    v = jnp.einsum('hij,bnjhd->bnihd', w_s.astype(v.dtype), v) + b_s.T.astype(v.dtype)[:, :, None]
    return u * v.reshape(bn, L, A_WIDTH)


def conformer_conv(z, conv_w, conv_b, ln_g, ln_b):
    a, g = z[..., :B_WIDTH], z[..., B_WIDTH:]
    h = a * jax.nn.sigmoid(g)
    h = depthwise_conv(h, conv_w, conv_b)
    h = layernorm(h, ln_g, ln_b)
    return jax.nn.silu(h)


def ab_mixer(h, w_in, b_in, a_ln_g, a_ln_b, a_w_s, a_b_s, b_conv_w, b_conv_b, b_ln_g, b_ln_b, w_out):
    z = h @ w_in + b_in
    ya = chunk_gmlp(z[..., :2 * A_WIDTH], a_ln_g, a_ln_b, a_w_s, a_b_s)
    yb = conformer_conv(z[..., 2 * A_WIDTH:], b_conv_w, b_conv_b, b_ln_g, b_ln_b)
    return jnp.concatenate([ya, yb], axis=-1) @ w_out


def conv_ffn(h, w_up, conv_w, conv_b, w_down):
    z = h @ w_up
    g, u = z[..., :D_FF], z[..., D_FF:]
    g = depthwise_conv(g, conv_w, conv_b)
    return (jax.nn.silu(g) * u) @ w_down


def axial_rope(L):
    rows = L // GRID_W
    row = jnp.repeat(jnp.arange(rows, dtype=jnp.float32), GRID_W)
    col = jnp.tile(jnp.arange(GRID_W, dtype=jnp.float32), rows)
    n_freq = QK_ROPE // 4
    inv = ROPE_THETA ** (-jnp.arange(n_freq, dtype=jnp.float32) / n_freq)
    ang = jnp.concatenate([row[:, None] * inv, col[:, None] * inv], axis=-1)
    return jnp.cos(ang), jnp.sin(ang)


def apply_rope(x, cos, sin):
    half = x.shape[-1] // 2
    x1, x2 = x[..., :half], x[..., half:]
    cos = cos.astype(x.dtype)
    sin = sin.astype(x.dtype)
    return jnp.concatenate([x1 * cos - x2 * sin, x2 * cos + x1 * sin], axis=-1)


def mla_queries(cq, q_norm_g, w_uq, cos, sin):
    bn, L, _ = cq.shape
    q = (rmsnorm(cq, q_norm_g) @ w_uq).reshape(bn, L, MLA_HEADS, QK_NOPE + QK_ROPE)
    q_nope, q_pe = q[..., :QK_NOPE], q[..., QK_NOPE:]
    if cos is not None:
        q_pe = apply_rope(q_pe, cos[None, :, None, :], sin[None, :, None, :])
    return q_nope, q_pe


def mla_keys_values(ckv, k_pe, kv_norm_g, w_ukv, cos, sin):
    bn, L, _ = ckv.shape
    kv = (rmsnorm(ckv, kv_norm_g) @ w_ukv).reshape(bn, L, MLA_HEADS, QK_NOPE + V_DIM)
    k_nope, v = kv[..., :QK_NOPE], kv[..., QK_NOPE:]
    if cos is not None:
        k_pe = apply_rope(k_pe, cos[None], sin[None])
    return k_nope, k_pe, v


def attend(q_nope, q_pe, k_nope, k_pe, v):
    bn, lq, _, _ = q_nope.shape
    nb = lq // Q_BLOCK
    qn = q_nope.reshape(bn, nb, Q_BLOCK, MLA_HEADS, QK_NOPE).transpose(1, 0, 2, 3, 4)
    qp = q_pe.reshape(bn, nb, Q_BLOCK, MLA_HEADS, QK_ROPE).transpose(1, 0, 2, 3, 4)
    scale = (QK_NOPE + QK_ROPE) ** -0.5

    def block(args):
        qn_b, qp_b = args
        s = jnp.einsum('bqhd,bkhd->bhqk', qn_b, k_nope) + jnp.einsum('bqhd,bkd->bhqk', qp_b, k_pe)
        p = jax.nn.softmax(s.astype(jnp.float32) * scale, axis=-1).astype(v.dtype)
        return jnp.einsum('bhqk,bkhd->bqhd', p, v)

    o = lax.map(block, (qn, qp))
    return o.transpose(1, 0, 2, 3, 4).reshape(bn, lq, MLA_HEADS * V_DIM)


def mla_mixer(h_lat, h_ctx, w_in, q_norm_g, w_uq, kv_norm_g, w_ukv, w_o, cos, sin, ctx_out):
    z = h_lat @ w_in
    cq, ckv, kpe = z[..., :Q_LORA], z[..., Q_LORA:Q_LORA + KV_LORA], z[..., Q_LORA + KV_LORA:]
    zc = h_ctx @ w_in[:, Q_LORA:]
    ckv_c, kpe_c = zc[..., :KV_LORA], zc[..., KV_LORA:]
    kn_l, kp_l, v_l = mla_keys_values(ckv, kpe, kv_norm_g, w_ukv, cos, sin)
    kn_c, kp_c, v_c = mla_keys_values(ckv_c, kpe_c, kv_norm_g, w_ukv, None, None)
    qn, qp = mla_queries(cq, q_norm_g, w_uq, cos, sin)
    o_lat = attend(qn, qp, jnp.concatenate([kn_l, kn_c], axis=1), jnp.concatenate([kp_l, kp_c], axis=1), jnp.concatenate([v_l, v_c], axis=1)) @ w_o
    o_ctx = None
    if ctx_out:
        qn_c, qp_c = mla_queries(h_ctx @ w_in[:, :Q_LORA], q_norm_g, w_uq, None, None)
        o_ctx = attend(qn_c, qp_c, kn_c, kp_c, v_c) @ w_o
    return o_lat, o_ctx


def _fwd_setup_inputs(seed: int = 0) -> dict:
    key = jax.random.key(seed)
    ks = iter(jax.random.split(key, 40))
    f32 = jnp.float32

    def nrm(shape, scale):
        return jax.random.normal(next(ks), shape, f32) * scale

    def gain(shape):
        return 1.0 + nrm(shape, 0.02)

    return {
        'x': nrm((BATCH, SEQ, D_MODEL), 1.0),
        'c': nrm((BATCH, D_MODEL), 1.0),
        'ctx': nrm((BATCH, CTX_LEN, D_MODEL), 1.0),
        'c_ctx': nrm((D_MODEL,), 1.0),
        'norm1_g': gain((DEPTH, D_MODEL)),
        'norm2_g': gain((DEPTH, D_MODEL)),
        'w_ada': nrm((DEPTH, D_MODEL, N_MOD * D_MODEL), D_MODEL ** -0.5),
        'b_ada': nrm((DEPTH, N_MOD * D_MODEL), 0.02),
        'ab_w_in': nrm((N_EVEN, D_MODEL, MIX_IN), D_MODEL ** -0.5),
        'ab_b_in': nrm((N_EVEN, MIX_IN), 0.02),
        'a_ln_g': gain((N_EVEN, A_WIDTH)),
        'a_ln_b': nrm((N_EVEN, A_WIDTH), 0.02),
        'a_w_s': nrm((N_EVEN, A_HEADS, CHUNK, CHUNK), CHUNK ** -0.5),
        'a_b_s': gain((N_EVEN, A_HEADS, CHUNK)),
        'b_conv_w': nrm((N_EVEN, B_CONV, B_WIDTH), B_CONV ** -0.5),
        'b_conv_b': nrm((N_EVEN, B_WIDTH), 0.02),
        'b_ln_g': gain((N_EVEN, B_WIDTH)),
        'b_ln_b': nrm((N_EVEN, B_WIDTH), 0.02),
        'ab_w_out': nrm((N_EVEN, MIX_OUT, D_MODEL), MIX_OUT ** -0.5),
        'mla_w_in': nrm((N_ODD, D_MODEL, MLA_IN), D_MODEL ** -0.5),
        'mla_q_norm_g': gain((N_ODD, Q_LORA)),
        'mla_w_uq': nrm((N_ODD, Q_LORA, MLA_HEADS * (QK_NOPE + QK_ROPE)), Q_LORA ** -0.5),
        'mla_kv_norm_g': gain((N_ODD, KV_LORA)),
        'mla_w_ukv': nrm((N_ODD, KV_LORA, MLA_HEADS * (QK_NOPE + V_DIM)), KV_LORA ** -0.5),
        'mla_w_o': nrm((N_ODD, MLA_HEADS * V_DIM, D_MODEL), (MLA_HEADS * V_DIM) ** -0.5),
        'ffn_w_up': nrm((DEPTH, D_MODEL, 2 * D_FF), D_MODEL ** -0.5),
        'ffn_conv_w': nrm((DEPTH, FFN_CONV, D_FF), FFN_CONV ** -0.5),
        'ffn_conv_b': nrm((DEPTH, D_FF), 0.02),
        'ffn_w_down': nrm((DEPTH, D_FF, D_MODEL), D_FF ** -0.5),
        'final_norm_g': gain((D_MODEL,)),
    }


def _fwd_reference(x, c, ctx, c_ctx, norm1_g, norm2_g, w_ada, b_ada, ab_w_in, ab_b_in, a_ln_g, a_ln_b, a_w_s, a_b_s, b_conv_w, b_conv_b, b_ln_g, b_ln_b, ab_w_out, mla_w_in, mla_q_norm_g, mla_w_uq, mla_kv_norm_g, mla_w_ukv, mla_w_o, ffn_w_up, ffn_conv_w, ffn_conv_b, ffn_w_down, final_norm_g):
    L = x.shape[1]
    cos, sin = axial_rope(L)
    xl, xc = x, ctx
    for i in range(DEPTH):
        last = i == DEPTH - 1
        even = i % 2 == 0
        j = i // 2
        ctx_in = (not last) or (not even)
        ctx_update = not last
        sh1, sc1, g1, sh2, sc2, g2 = [m[:, None, :] for m in adaln(c, w_ada[i], b_ada[i])]
        if ctx_in:
            csh1, csc1, cg1, csh2, csc2, cg2 = adaln(c_ctx, w_ada[i], b_ada[i])
        hl = modulate(rmsnorm(xl, norm1_g[i]), sh1, sc1)
        if even:
            ab_args = (ab_w_in[j], ab_b_in[j], a_ln_g[j], a_ln_b[j], a_w_s[j], a_b_s[j], b_conv_w[j], b_conv_b[j], b_ln_g[j], b_ln_b[j], ab_w_out[j])
            yl = ab_mixer(hl, *ab_args)
            if ctx_update:
                hc = modulate(rmsnorm(xc, norm1_g[i]), csh1, csc1)
                xc = xc + cg1 * ab_mixer(hc, *ab_args)
        else:
            hc = modulate(rmsnorm(xc, norm1_g[i]), csh1, csc1)
            yl, yc = mla_mixer(hl, hc, mla_w_in[j], mla_q_norm_g[j], mla_w_uq[j], mla_kv_norm_g[j], mla_w_ukv[j], mla_w_o[j], cos, sin, ctx_update)
            if ctx_update:
                xc = xc + cg1 * yc
        xl = xl + g1 * yl
        xl = xl + g2 * conv_ffn(modulate(rmsnorm(xl, norm2_g[i]), sh2, sc2), ffn_w_up[i], ffn_conv_w[i], ffn_conv_b[i], ffn_w_down[i])
        if ctx_update:
            xc = xc + cg2 * conv_ffn(modulate(rmsnorm(xc, norm2_g[i]), csh2, csc2), ffn_w_up[i], ffn_conv_w[i], ffn_conv_b[i], ffn_w_down[i])
    return rmsnorm(xl, final_norm_g)


import jax as _jax
import jax.numpy as _jnp

TWIN_FORMAT = 'train_step'
FWD_PARAMS = ['x', 'c', 'ctx', 'c_ctx', 'norm1_g', 'norm2_g', 'w_ada', 'b_ada', 'ab_w_in', 'ab_b_in', 'a_ln_g', 'a_ln_b', 'a_w_s', 'a_b_s', 'b_conv_w', 'b_conv_b', 'b_ln_g', 'b_ln_b', 'ab_w_out', 'mla_w_in', 'mla_q_norm_g', 'mla_w_uq', 'mla_kv_norm_g', 'mla_w_ukv', 'mla_w_o', 'ffn_w_up', 'ffn_conv_w', 'ffn_conv_b', 'ffn_w_down', 'final_norm_g']
TWIN_WEIGHTS = ['c_ctx', 'norm1_g', 'norm2_g', 'w_ada', 'b_ada', 'ab_w_in', 'ab_b_in', 'a_ln_g', 'a_ln_b', 'a_w_s', 'a_b_s', 'b_conv_w', 'b_conv_b', 'b_ln_g', 'b_ln_b', 'ab_w_out', 'mla_w_in', 'mla_q_norm_g', 'mla_w_uq', 'mla_kv_norm_g', 'mla_w_ukv', 'mla_w_o', 'ffn_w_up', 'ffn_conv_w', 'ffn_conv_b', 'ffn_w_down', 'final_norm_g']
TWIN_DIFF_INPUT = 'x'
TWIN_INPUTS = ['x', 'c', 'ctx', 'c_ctx', 'norm1_g', 'norm2_g', 'w_ada', 'b_ada', 'ab_w_in', 'ab_b_in', 'a_ln_g', 'a_ln_b', 'a_w_s', 'a_b_s', 'b_conv_w', 'b_conv_b', 'b_ln_g', 'b_ln_b', 'ab_w_out', 'mla_w_in', 'mla_q_norm_g', 'mla_w_uq', 'mla_kv_norm_g', 'mla_w_ukv', 'mla_w_o', 'ffn_w_up', 'ffn_conv_w', 'ffn_conv_b', 'ffn_w_down', 'final_norm_g', 'loss_target', 'm_c_ctx', 'm_norm1_g', 'm_norm2_g', 'm_w_ada', 'm_b_ada', 'm_ab_w_in', 'm_ab_b_in', 'm_a_ln_g', 'm_a_ln_b', 'm_a_w_s', 'm_a_b_s', 'm_b_conv_w', 'm_b_conv_b', 'm_b_ln_g', 'm_b_ln_b', 'm_ab_w_out', 'm_mla_w_in', 'm_mla_q_norm_g', 'm_mla_w_uq', 'm_mla_kv_norm_g', 'm_mla_w_ukv', 'm_mla_w_o', 'm_ffn_w_up', 'm_ffn_conv_w', 'm_ffn_conv_b', 'm_ffn_w_down', 'm_final_norm_g', 'v_c_ctx', 'v_norm1_g', 'v_norm2_g', 'v_w_ada', 'v_b_ada', 'v_ab_w_in', 'v_ab_b_in', 'v_a_ln_g', 'v_a_ln_b', 'v_a_w_s', 'v_a_b_s', 'v_b_conv_w', 'v_b_conv_b', 'v_b_ln_g', 'v_b_ln_b', 'v_ab_w_out', 'v_mla_w_in', 'v_mla_q_norm_g', 'v_mla_w_uq', 'v_mla_kv_norm_g', 'v_mla_w_ukv', 'v_mla_w_o', 'v_ffn_w_up', 'v_ffn_conv_w', 'v_ffn_conv_b', 'v_ffn_w_down', 'v_final_norm_g']
TWIN_OUTPUTS = ['loss', 'grad_x', 'grad_c_ctx', 'grad_norm1_g', 'grad_norm2_g', 'grad_w_ada', 'grad_b_ada', 'grad_ab_w_in', 'grad_ab_b_in', 'grad_a_ln_g', 'grad_a_ln_b', 'grad_a_w_s', 'grad_a_b_s', 'grad_b_conv_w', 'grad_b_conv_b', 'grad_b_ln_g', 'grad_b_ln_b', 'grad_ab_w_out', 'grad_mla_w_in', 'grad_mla_q_norm_g', 'grad_mla_w_uq', 'grad_mla_kv_norm_g', 'grad_mla_w_ukv', 'grad_mla_w_o', 'grad_ffn_w_up', 'grad_ffn_conv_w', 'grad_ffn_conv_b', 'grad_ffn_w_down', 'grad_final_norm_g', 'delta_c_ctx', 'delta_norm1_g', 'delta_norm2_g', 'delta_w_ada', 'delta_b_ada', 'delta_ab_w_in', 'delta_ab_b_in', 'delta_a_ln_g', 'delta_a_ln_b', 'delta_a_w_s', 'delta_a_b_s', 'delta_b_conv_w', 'delta_b_conv_b', 'delta_b_ln_g', 'delta_b_ln_b', 'delta_ab_w_out', 'delta_mla_w_in', 'delta_mla_q_norm_g', 'delta_mla_w_uq', 'delta_mla_kv_norm_g', 'delta_mla_w_ukv', 'delta_mla_w_o', 'delta_ffn_w_up', 'delta_ffn_conv_w', 'delta_ffn_conv_b', 'delta_ffn_w_down', 'delta_final_norm_g', 'new_m_c_ctx', 'new_m_norm1_g', 'new_m_norm2_g', 'new_m_w_ada', 'new_m_b_ada', 'new_m_ab_w_in', 'new_m_ab_b_in', 'new_m_a_ln_g', 'new_m_a_ln_b', 'new_m_a_w_s', 'new_m_a_b_s', 'new_m_b_conv_w', 'new_m_b_conv_b', 'new_m_b_ln_g', 'new_m_b_ln_b', 'new_m_ab_w_out', 'new_m_mla_w_in', 'new_m_mla_q_norm_g', 'new_m_mla_w_uq', 'new_m_mla_kv_norm_g', 'new_m_mla_w_ukv', 'new_m_mla_w_o', 'new_m_ffn_w_up', 'new_m_ffn_conv_w', 'new_m_ffn_conv_b', 'new_m_ffn_w_down', 'new_m_final_norm_g', 'new_v_c_ctx', 'new_v_norm1_g', 'new_v_norm2_g', 'new_v_w_ada', 'new_v_b_ada', 'new_v_ab_w_in', 'new_v_ab_b_in', 'new_v_a_ln_g', 'new_v_a_ln_b', 'new_v_a_w_s', 'new_v_a_b_s', 'new_v_b_conv_w', 'new_v_b_conv_b', 'new_v_b_ln_g', 'new_v_b_ln_b', 'new_v_ab_w_out', 'new_v_mla_w_in', 'new_v_mla_q_norm_g', 'new_v_mla_w_uq', 'new_v_mla_kv_norm_g', 'new_v_mla_w_ukv', 'new_v_mla_w_o', 'new_v_ffn_w_up', 'new_v_ffn_conv_w', 'new_v_ffn_conv_b', 'new_v_ffn_w_down', 'new_v_final_norm_g']
TWIN_LEAF_KINDS = {'loss': 'loss', 'grad_x': 'grad_x', 'grad_c_ctx': 'grad_w', 'grad_norm1_g': 'grad_w', 'grad_norm2_g': 'grad_w', 'grad_w_ada': 'grad_w', 'grad_b_ada': 'grad_w', 'grad_ab_w_in': 'grad_w', 'grad_ab_b_in': 'grad_w', 'grad_a_ln_g': 'grad_w', 'grad_a_ln_b': 'grad_w', 'grad_a_w_s': 'grad_w', 'grad_a_b_s': 'grad_w', 'grad_b_conv_w': 'grad_w', 'grad_b_conv_b': 'grad_w', 'grad_b_ln_g': 'grad_w', 'grad_b_ln_b': 'grad_w', 'grad_ab_w_out': 'grad_w', 'grad_mla_w_in': 'grad_w', 'grad_mla_q_norm_g': 'grad_w', 'grad_mla_w_uq': 'grad_w', 'grad_mla_kv_norm_g': 'grad_w', 'grad_mla_w_ukv': 'grad_w', 'grad_mla_w_o': 'grad_w', 'grad_ffn_w_up': 'grad_w', 'grad_ffn_conv_w': 'grad_w', 'grad_ffn_conv_b': 'grad_w', 'grad_ffn_w_down': 'grad_w', 'grad_final_norm_g': 'grad_w', 'delta_c_ctx': 'delta_w', 'delta_norm1_g': 'delta_w', 'delta_norm2_g': 'delta_w', 'delta_w_ada': 'delta_w', 'delta_b_ada': 'delta_w', 'delta_ab_w_in': 'delta_w', 'delta_ab_b_in': 'delta_w', 'delta_a_ln_g': 'delta_w', 'delta_a_ln_b': 'delta_w', 'delta_a_w_s': 'delta_w', 'delta_a_b_s': 'delta_w', 'delta_b_conv_w': 'delta_w', 'delta_b_conv_b': 'delta_w', 'delta_b_ln_g': 'delta_w', 'delta_b_ln_b': 'delta_w', 'delta_ab_w_out': 'delta_w', 'delta_mla_w_in': 'delta_w', 'delta_mla_q_norm_g': 'delta_w', 'delta_mla_w_uq': 'delta_w', 'delta_mla_kv_norm_g': 'delta_w', 'delta_mla_w_ukv': 'delta_w', 'delta_mla_w_o': 'delta_w', 'delta_ffn_w_up': 'delta_w', 'delta_ffn_conv_w': 'delta_w', 'delta_ffn_conv_b': 'delta_w', 'delta_ffn_w_down': 'delta_w', 'delta_final_norm_g': 'delta_w', 'new_m_c_ctx': 'new_m', 'new_m_norm1_g': 'new_m', 'new_m_norm2_g': 'new_m', 'new_m_w_ada': 'new_m', 'new_m_b_ada': 'new_m', 'new_m_ab_w_in': 'new_m', 'new_m_ab_b_in': 'new_m', 'new_m_a_ln_g': 'new_m', 'new_m_a_ln_b': 'new_m', 'new_m_a_w_s': 'new_m', 'new_m_a_b_s': 'new_m', 'new_m_b_conv_w': 'new_m', 'new_m_b_conv_b': 'new_m', 'new_m_b_ln_g': 'new_m', 'new_m_b_ln_b': 'new_m', 'new_m_ab_w_out': 'new_m', 'new_m_mla_w_in': 'new_m', 'new_m_mla_q_norm_g': 'new_m', 'new_m_mla_w_uq': 'new_m', 'new_m_mla_kv_norm_g': 'new_m', 'new_m_mla_w_ukv': 'new_m', 'new_m_mla_w_o': 'new_m', 'new_m_ffn_w_up': 'new_m', 'new_m_ffn_conv_w': 'new_m', 'new_m_ffn_conv_b': 'new_m', 'new_m_ffn_w_down': 'new_m', 'new_m_final_norm_g': 'new_m', 'new_v_c_ctx': 'new_v', 'new_v_norm1_g': 'new_v', 'new_v_norm2_g': 'new_v', 'new_v_w_ada': 'new_v', 'new_v_b_ada': 'new_v', 'new_v_ab_w_in': 'new_v', 'new_v_ab_b_in': 'new_v', 'new_v_a_ln_g': 'new_v', 'new_v_a_ln_b': 'new_v', 'new_v_a_w_s': 'new_v', 'new_v_a_b_s': 'new_v', 'new_v_b_conv_w': 'new_v', 'new_v_b_conv_b': 'new_v', 'new_v_b_ln_g': 'new_v', 'new_v_b_ln_b': 'new_v', 'new_v_ab_w_out': 'new_v', 'new_v_mla_w_in': 'new_v', 'new_v_mla_q_norm_g': 'new_v', 'new_v_mla_w_uq': 'new_v', 'new_v_mla_kv_norm_g': 'new_v', 'new_v_mla_w_ukv': 'new_v', 'new_v_mla_w_o': 'new_v', 'new_v_ffn_w_up': 'new_v', 'new_v_ffn_conv_w': 'new_v', 'new_v_ffn_conv_b': 'new_v', 'new_v_ffn_w_down': 'new_v', 'new_v_final_norm_g': 'new_v'}


def _forward(args):
    return _fwd_reference(*[args[k] for k in FWD_PARAMS])


def _output_shape():
    def fwd():
        inp = _fwd_setup_inputs(0)
        return _fwd_reference(*[inp[k] for k in FWD_PARAMS])
    out = _jax.eval_shape(fwd)
    return out.shape, out.dtype

N_MICROBATCH = 1
ADAM_LR = 0.001
ADAM_B1 = 0.9
ADAM_B2 = 0.999
ADAM_EPS = 1e-08
ADAM_WD = 0.01
ADAM_STEP = 10
PER_EXAMPLE_BATCH_AXIS = {'x': 0, 'c': 0, 'ctx': 0, 'loss_target': 0}
SHARED_INPUTS = []
_WEIGHT_DTYPES = {'c_ctx': _jnp.float32, 'norm1_g': _jnp.float32, 'norm2_g': _jnp.float32, 'w_ada': _jnp.float32, 'b_ada': _jnp.float32, 'ab_w_in': _jnp.float32, 'ab_b_in': _jnp.float32, 'a_ln_g': _jnp.float32, 'a_ln_b': _jnp.float32, 'a_w_s': _jnp.float32, 'a_b_s': _jnp.float32, 'b_conv_w': _jnp.float32, 'b_conv_b': _jnp.float32, 'b_ln_g': _jnp.float32, 'b_ln_b': _jnp.float32, 'ab_w_out': _jnp.float32, 'mla_w_in': _jnp.float32, 'mla_q_norm_g': _jnp.float32, 'mla_w_uq': _jnp.float32, 'mla_kv_norm_g': _jnp.float32, 'mla_w_ukv': _jnp.float32, 'mla_w_o': _jnp.float32, 'ffn_w_up': _jnp.float32, 'ffn_conv_w': _jnp.float32, 'ffn_conv_b': _jnp.float32, 'ffn_w_down': _jnp.float32, 'final_norm_g': _jnp.float32}
MOMENT_SCALE = {'c_ctx': 1.335586e-02, 'norm1_g': 4.204080e-02, 'norm2_g': 5.405673e-02, 'w_ada': 3.516036e-02, 'b_ada': 6.200987e-02, 'ab_w_in': 4.648887e-02, 'ab_b_in': 3.938792e-02, 'a_ln_g': 4.834301e-02, 'a_ln_b': 4.890527e-02, 'a_w_s': 4.510596e-02, 'a_b_s': 4.689438e-02, 'b_conv_w': 3.284951e-02, 'b_conv_b': 5.087329e-02, 'b_ln_g': 3.893688e-02, 'b_ln_b': 3.715594e-02, 'ab_w_out': 5.260459e-02, 'mla_w_in': 2.768492e-02, 'mla_q_norm_g': 9.198718e-03, 'mla_w_uq': 4.586116e-03, 'mla_kv_norm_g': 4.127556e-02, 'mla_w_ukv': 1.565006e-02, 'mla_w_o': 2.228833e-02, 'ffn_w_up': 2.620418e-02, 'ffn_conv_w': 2.684376e-02, 'ffn_conv_b': 1.936782e-02, 'ffn_w_down': 4.304702e-02, 'final_norm_g': 1.623946e+01}


def _to_microbatches(a, axis):
    t = _jnp.moveaxis(a, axis, 0)
    t = t.reshape((N_MICROBATCH, t.shape[0] // N_MICROBATCH) + t.shape[1:])
    return _jnp.moveaxis(t, 1, axis + 1)


def setup_inputs(seed: int = 0) -> dict:
    inp = _fwd_setup_inputs(seed)
    key = _jax.random.fold_in(_jax.random.key(seed), 7919)
    shape, _ = _output_shape()
    out = dict(inp)
    out["loss_target"] = _jax.random.normal(_jax.random.fold_in(key, 0), shape, _jnp.float32)
    for i, name in enumerate(TWIN_WEIGHTS):
        w = inp[name].astype(_jnp.float32)
        if MOMENT_SCALE is None:
            s = _jnp.sqrt(_jnp.mean(_jnp.square(w)) + 1e-30)
        else:
            s = MOMENT_SCALE[name]
        km, kv = _jax.random.split(_jax.random.fold_in(key, i + 1))
        out[name] = w
        out["m_" + name] = s * _jax.random.normal(km, w.shape, _jnp.float32)
        out["v_" + name] = (s * s) * _jax.random.uniform(kv, w.shape, _jnp.float32, 0.5, 1.5)
    if N_MICROBATCH > 1:
        for name, axis in PER_EXAMPLE_BATCH_AXIS.items():
            out[name] = _to_microbatches(out[name], axis)
    return {'x': out['x'], 'c': out['c'], 'ctx': out['ctx'], 'c_ctx': out['c_ctx'], 'norm1_g': out['norm1_g'], 'norm2_g': out['norm2_g'], 'w_ada': out['w_ada'], 'b_ada': out['b_ada'], 'ab_w_in': out['ab_w_in'], 'ab_b_in': out['ab_b_in'], 'a_ln_g': out['a_ln_g'], 'a_ln_b': out['a_ln_b'], 'a_w_s': out['a_w_s'], 'a_b_s': out['a_b_s'], 'b_conv_w': out['b_conv_w'], 'b_conv_b': out['b_conv_b'], 'b_ln_g': out['b_ln_g'], 'b_ln_b': out['b_ln_b'], 'ab_w_out': out['ab_w_out'], 'mla_w_in': out['mla_w_in'], 'mla_q_norm_g': out['mla_q_norm_g'], 'mla_w_uq': out['mla_w_uq'], 'mla_kv_norm_g': out['mla_kv_norm_g'], 'mla_w_ukv': out['mla_w_ukv'], 'mla_w_o': out['mla_w_o'], 'ffn_w_up': out['ffn_w_up'], 'ffn_conv_w': out['ffn_conv_w'], 'ffn_conv_b': out['ffn_conv_b'], 'ffn_w_down': out['ffn_w_down'], 'final_norm_g': out['final_norm_g'], 'loss_target': out['loss_target'], 'm_c_ctx': out['m_c_ctx'], 'm_norm1_g': out['m_norm1_g'], 'm_norm2_g': out['m_norm2_g'], 'm_w_ada': out['m_w_ada'], 'm_b_ada': out['m_b_ada'], 'm_ab_w_in': out['m_ab_w_in'], 'm_ab_b_in': out['m_ab_b_in'], 'm_a_ln_g': out['m_a_ln_g'], 'm_a_ln_b': out['m_a_ln_b'], 'm_a_w_s': out['m_a_w_s'], 'm_a_b_s': out['m_a_b_s'], 'm_b_conv_w': out['m_b_conv_w'], 'm_b_conv_b': out['m_b_conv_b'], 'm_b_ln_g': out['m_b_ln_g'], 'm_b_ln_b': out['m_b_ln_b'], 'm_ab_w_out': out['m_ab_w_out'], 'm_mla_w_in': out['m_mla_w_in'], 'm_mla_q_norm_g': out['m_mla_q_norm_g'], 'm_mla_w_uq': out['m_mla_w_uq'], 'm_mla_kv_norm_g': out['m_mla_kv_norm_g'], 'm_mla_w_ukv': out['m_mla_w_ukv'], 'm_mla_w_o': out['m_mla_w_o'], 'm_ffn_w_up': out['m_ffn_w_up'], 'm_ffn_conv_w': out['m_ffn_conv_w'], 'm_ffn_conv_b': out['m_ffn_conv_b'], 'm_ffn_w_down': out['m_ffn_w_down'], 'm_final_norm_g': out['m_final_norm_g'], 'v_c_ctx': out['v_c_ctx'], 'v_norm1_g': out['v_norm1_g'], 'v_norm2_g': out['v_norm2_g'], 'v_w_ada': out['v_w_ada'], 'v_b_ada': out['v_b_ada'], 'v_ab_w_in': out['v_ab_w_in'], 'v_ab_b_in': out['v_ab_b_in'], 'v_a_ln_g': out['v_a_ln_g'], 'v_a_ln_b': out['v_a_ln_b'], 'v_a_w_s': out['v_a_w_s'], 'v_a_b_s': out['v_a_b_s'], 'v_b_conv_w': out['v_b_conv_w'], 'v_b_conv_b': out['v_b_conv_b'], 'v_b_ln_g': out['v_b_ln_g'], 'v_b_ln_b': out['v_b_ln_b'], 'v_ab_w_out': out['v_ab_w_out'], 'v_mla_w_in': out['v_mla_w_in'], 'v_mla_q_norm_g': out['v_mla_q_norm_g'], 'v_mla_w_uq': out['v_mla_w_uq'], 'v_mla_kv_norm_g': out['v_mla_kv_norm_g'], 'v_mla_w_ukv': out['v_mla_w_ukv'], 'v_mla_w_o': out['v_mla_w_o'], 'v_ffn_w_up': out['v_ffn_w_up'], 'v_ffn_conv_w': out['v_ffn_conv_w'], 'v_ffn_conv_b': out['v_ffn_conv_b'], 'v_ffn_w_down': out['v_ffn_w_down'], 'v_final_norm_g': out['v_final_norm_g']}


def _loss(weights, diff, rest, loss_target):
    with _jax.named_scope("forward"):
        args = {**rest, TWIN_DIFF_INPUT: diff, **{k: w.astype(_WEIGHT_DTYPES[k]) for k, w in weights.items()}}
        y = _forward(args)
    with _jax.named_scope("loss_head"):
        err = _jnp.square(y.astype(_jnp.float32) - loss_target)
        return 0.5 * _jnp.sum(_jnp.mean(err, axis=-1)) if err.ndim else 0.5 * err


def _adamw(w, g, m, v):
    m = ADAM_B1 * m + (1.0 - ADAM_B1) * g
    v = ADAM_B2 * v + (1.0 - ADAM_B2) * _jnp.square(g)
    m_hat = m / (1.0 - ADAM_B1 ** ADAM_STEP)
    v_hat = v / (1.0 - ADAM_B2 ** ADAM_STEP)
    delta = -ADAM_LR * (m_hat / (_jnp.sqrt(v_hat) + ADAM_EPS) + ADAM_WD * w)
    return delta, m, v


def reference(x, c, ctx, c_ctx, norm1_g, norm2_g, w_ada, b_ada, ab_w_in, ab_b_in, a_ln_g, a_ln_b, a_w_s, a_b_s, b_conv_w, b_conv_b, b_ln_g, b_ln_b, ab_w_out, mla_w_in, mla_q_norm_g, mla_w_uq, mla_kv_norm_g, mla_w_ukv, mla_w_o, ffn_w_up, ffn_conv_w, ffn_conv_b, ffn_w_down, final_norm_g, loss_target, m_c_ctx, m_norm1_g, m_norm2_g, m_w_ada, m_b_ada, m_ab_w_in, m_ab_b_in, m_a_ln_g, m_a_ln_b, m_a_w_s, m_a_b_s, m_b_conv_w, m_b_conv_b, m_b_ln_g, m_b_ln_b, m_ab_w_out, m_mla_w_in, m_mla_q_norm_g, m_mla_w_uq, m_mla_kv_norm_g, m_mla_w_ukv, m_mla_w_o, m_ffn_w_up, m_ffn_conv_w, m_ffn_conv_b, m_ffn_w_down, m_final_norm_g, v_c_ctx, v_norm1_g, v_norm2_g, v_w_ada, v_b_ada, v_ab_w_in, v_ab_b_in, v_a_ln_g, v_a_ln_b, v_a_w_s, v_a_b_s, v_b_conv_w, v_b_conv_b, v_b_ln_g, v_b_ln_b, v_ab_w_out, v_mla_w_in, v_mla_q_norm_g, v_mla_w_uq, v_mla_kv_norm_g, v_mla_w_ukv, v_mla_w_o, v_ffn_w_up, v_ffn_conv_w, v_ffn_conv_b, v_ffn_w_down, v_final_norm_g):
    given = dict(x=x, c=c, ctx=ctx, c_ctx=c_ctx, norm1_g=norm1_g, norm2_g=norm2_g, w_ada=w_ada, b_ada=b_ada, ab_w_in=ab_w_in, ab_b_in=ab_b_in, a_ln_g=a_ln_g, a_ln_b=a_ln_b, a_w_s=a_w_s, a_b_s=a_b_s, b_conv_w=b_conv_w, b_conv_b=b_conv_b, b_ln_g=b_ln_g, b_ln_b=b_ln_b, ab_w_out=ab_w_out, mla_w_in=mla_w_in, mla_q_norm_g=mla_q_norm_g, mla_w_uq=mla_w_uq, mla_kv_norm_g=mla_kv_norm_g, mla_w_ukv=mla_w_ukv, mla_w_o=mla_w_o, ffn_w_up=ffn_w_up, ffn_conv_w=ffn_conv_w, ffn_conv_b=ffn_conv_b, ffn_w_down=ffn_w_down, final_norm_g=final_norm_g, loss_target=loss_target, m_c_ctx=m_c_ctx, m_norm1_g=m_norm1_g, m_norm2_g=m_norm2_g, m_w_ada=m_w_ada, m_b_ada=m_b_ada, m_ab_w_in=m_ab_w_in, m_ab_b_in=m_ab_b_in, m_a_ln_g=m_a_ln_g, m_a_ln_b=m_a_ln_b, m_a_w_s=m_a_w_s, m_a_b_s=m_a_b_s, m_b_conv_w=m_b_conv_w, m_b_conv_b=m_b_conv_b, m_b_ln_g=m_b_ln_g, m_b_ln_b=m_b_ln_b, m_ab_w_out=m_ab_w_out, m_mla_w_in=m_mla_w_in, m_mla_q_norm_g=m_mla_q_norm_g, m_mla_w_uq=m_mla_w_uq, m_mla_kv_norm_g=m_mla_kv_norm_g, m_mla_w_ukv=m_mla_w_ukv, m_mla_w_o=m_mla_w_o, m_ffn_w_up=m_ffn_w_up, m_ffn_conv_w=m_ffn_conv_w, m_ffn_conv_b=m_ffn_conv_b, m_ffn_w_down=m_ffn_w_down, m_final_norm_g=m_final_norm_g, v_c_ctx=v_c_ctx, v_norm1_g=v_norm1_g, v_norm2_g=v_norm2_g, v_w_ada=v_w_ada, v_b_ada=v_b_ada, v_ab_w_in=v_ab_w_in, v_ab_b_in=v_ab_b_in, v_a_ln_g=v_a_ln_g, v_a_ln_b=v_a_ln_b, v_a_w_s=v_a_w_s, v_a_b_s=v_a_b_s, v_b_conv_w=v_b_conv_w, v_b_conv_b=v_b_conv_b, v_b_ln_g=v_b_ln_g, v_b_ln_b=v_b_ln_b, v_ab_w_out=v_ab_w_out, v_mla_w_in=v_mla_w_in, v_mla_q_norm_g=v_mla_q_norm_g, v_mla_w_uq=v_mla_w_uq, v_mla_kv_norm_g=v_mla_kv_norm_g, v_mla_w_ukv=v_mla_w_ukv, v_mla_w_o=v_mla_w_o, v_ffn_w_up=v_ffn_w_up, v_ffn_conv_w=v_ffn_conv_w, v_ffn_conv_b=v_ffn_conv_b, v_ffn_w_down=v_ffn_w_down, v_final_norm_g=v_final_norm_g)
    weights = {n: given[n] for n in TWIN_WEIGHTS}
    shared = {n: given[n] for n in SHARED_INPUTS}
    per_example = {n: given[n] for n in ['x', 'c', 'ctx']}
    grad_fn = _jax.value_and_grad(_loss, argnums=(0, 1))

    def one_microbatch(ex, loss_target):
        ex = dict(ex)
        diff = ex.pop(TWIN_DIFF_INPUT)
        return grad_fn(weights, diff, {**shared, **ex}, loss_target)

    if N_MICROBATCH == 1:
        loss, (grad_w, grad_x) = one_microbatch(per_example, given["loss_target"])
    else:
        def body(carry, xs):
            loss_sum, grad_sum = carry
            l_k, (gw_k, gx_k) = one_microbatch(xs[0], xs[1])
            with _jax.named_scope("update"):
                return (loss_sum + l_k, _jax.tree.map(_jnp.add, grad_sum, gw_k)), gx_k

        init = (_jnp.zeros((), _jnp.float32), _jax.tree.map(_jnp.zeros_like, weights))
        (loss, grad_w), grad_x = _jax.lax.scan(body, init, (per_example, given["loss_target"]))
    with _jax.named_scope("update"):
        delta_w, new_m, new_v = {}, {}, {}
        for n in TWIN_WEIGHTS:
            delta_w[n], new_m[n], new_v[n] = _adamw(weights[n], grad_w[n], given["m_" + n], given["v_" + n])
    return (loss, grad_x, *[grad_w[n] for n in TWIN_WEIGHTS], *[delta_w[n] for n in TWIN_WEIGHTS],
            *[new_m[n] for n in TWIN_WEIGHTS], *[new_v[n] for n in TWIN_WEIGHTS])
```

```python
import functools
import math

import jax
import jax.numpy as jnp
from jax import lax
from jax.experimental import pallas as pl
from jax.experimental.pallas import tpu as pltpu

F32 = jnp.float32
BF16 = jnp.bfloat16
MESH = pl.DeviceIdType.MESH
ANY = pl.BlockSpec(memory_space=pl.ANY)

EPS = 1e-6
N_MOD = 6
CHUNK = 128
HEAD = 128
ROPE = 64
QHEAD = 2 * HEAD
GRID_W = 64
ROPE_THETA = 10000.0
B_CONV = 31
FFN_CONV = 3
ADAM_LR, ADAM_B1, ADAM_B2, ADAM_EPS, ADAM_WD, ADAM_STEP = 0.001, 0.9, 0.999, 1e-08, 0.01, 10

V7X_VMEM_LIMIT = 56 * 1024 * 1024
LANES = 128
SUBLANES = 8
CONV_PAD = 16

BS = pl.BlockSpec


def _cp(sem=None, vmem=V7X_VMEM_LIMIT):
    return pltpu.CompilerParams(dimension_semantics=sem, vmem_limit_bytes=vmem)


def _pick(n, prefs):
    for p in prefs:
        if p <= n and n % p == 0:
            return p
    return n


def _row_tile(rows, cols, target_bytes=1 << 20):
    best = None
    for d in range(2 * SUBLANES, rows + 1, 2 * SUBLANES):
        if rows % d == 0 and d * cols * 4 <= target_bytes:
            best = d
    return best if best is not None else rows


def _sigmoid(x):
    return 1.0 / (1.0 + jnp.exp(-x))


def _gelu(x):
    c = math.sqrt(2.0 / math.pi)
    th = jnp.tanh(c * (x + 0.044715 * x * x * x))
    return 0.5 * x * (1.0 + th), th


def _gelu_grad(x, th):
    c = math.sqrt(2.0 / math.pi)
    return 0.5 * (1.0 + th) + 0.5 * x * (1.0 - th * th) * c * (1.0 + 3.0 * 0.044715 * x * x)


_DIMS = {"nn": (((1,), (0,)), ((), ())), "nt": (((1,), (1,)), ((), ())), "tn": (((0,), (0,)), ((), ()))}


def _mm(a, b, mode, out_dtype, name, tm=(512,), tn=(512,), tk=(100000,), bias=None, silu_a=False, rows=None, outer="j"):
    if mode == "nn":
        (M, K), N = a.shape, b.shape[1]
    elif mode == "nt":
        (M, K), N = a.shape, b.shape[0]
    else:
        (K, M), N = a.shape, b.shape[1]
    if rows is not None:
        if mode == "tn":
            K = rows
        else:
            M = rows
    tm, tn, tk = _pick(M, tm), _pick(N, tn), _pick(K, tk)
    gm, gn, gk = M // tm, N // tn, K // tk

    def ij(g0, g1):
        return (g1, g0) if outer == "j" else (g0, g1)

    if mode == "nn":
        a_spec = BS((tm, tk), lambda g0, g1, k: (ij(g0, g1)[0], k))
        b_spec = BS((tk, tn), lambda g0, g1, k: (k, ij(g0, g1)[1]))
    elif mode == "nt":
        a_spec = BS((tm, tk), lambda g0, g1, k: (ij(g0, g1)[0], k))
        b_spec = BS((tn, tk), lambda g0, g1, k: (ij(g0, g1)[1], k))
    else:
        a_spec = BS((tk, tm), lambda g0, g1, k: (k, ij(g0, g1)[0]))
        b_spec = BS((tk, tn), lambda g0, g1, k: (k, ij(g0, g1)[1]))
    in_specs = [a_spec, b_spec]
    operands = [a, b]
    if bias is not None:
        in_specs.append(BS((1, tn), lambda g0, g1, k: (0, ij(g0, g1)[1])))
        operands.append(bias.reshape(1, N))
    o_spec = BS((tm, tn), lambda g0, g1, k: ij(g0, g1))

    def body(*refs):
        a_ref, b_ref = refs[0], refs[1]
        bias_ref = refs[2] if bias is not None else None
        o_ref = refs[3] if bias is not None else refs[2]
        av = a_ref[...]
        if silu_a:
            av = av.astype(F32)
            av = av * _sigmoid(av)
        part = lax.dot_general(av.astype(BF16), b_ref[...].astype(BF16), _DIMS[mode], preferred_element_type=F32)

        def finish(acc):
            if bias_ref is not None:
                acc = acc + bias_ref[...]
            return acc.astype(out_dtype)

        if gk == 1:
            o_ref[...] = finish(part)
        else:
            acc_ref = refs[-1]
            k = pl.program_id(2)

            @pl.when(k == 0)
            def _():
                acc_ref[...] = part

            @pl.when(k > 0)
            def _():
                acc_ref[...] += part

            @pl.when(k == gk - 1)
            def _():
                o_ref[...] = finish(acc_ref[...])

    grid = (gn, gm, gk) if outer == "j" else (gm, gn, gk)
    return pl.pallas_call(
        body, name=name, out_shape=jax.ShapeDtypeStruct((M, N), out_dtype), grid=grid, in_specs=in_specs, out_specs=o_spec,
        scratch_shapes=[pltpu.VMEM((tm, tn), F32)] if gk > 1 else [],
        compiler_params=_cp(("parallel", "parallel", "arbitrary")),
    )(*operands)


def _accum(ref, val, first):
    @pl.when(first)
    def _():
        ref[...] = val

    @pl.when(jnp.logical_not(first))
    def _():
        ref[...] += val


def _norm_mod_fwd(X, gain, sh, sc, tb, nlat, name):
    R, D = X.shape

    def body(x_ref, g_ref, sh_ref, sc_ref, o_ref):
        x = x_ref[...]
        r = lax.rsqrt(jnp.mean(x * x, axis=-1, keepdims=True) + EPS)
        o_ref[...] = ((x * r * g_ref[...]) * (1.0 + sc_ref[0]) + sh_ref[0]).astype(BF16)

    grp = BS((1, 1, D), lambda i: (i // nlat, 0, 0))
    return pl.pallas_call(
        body, name=name, out_shape=jax.ShapeDtypeStruct((R, D), BF16), grid=(R // tb,),
        in_specs=[BS((tb, D), lambda i: (i, 0)), BS((1, D), lambda i: (0, 0)), grp, grp],
        out_specs=BS((tb, D), lambda i: (i, 0)), compiler_params=_cp(("parallel",)),
    )(X, gain.reshape(1, D), sh, sc)


def _norm_mod_bwd(X, gain, sc, dh, dup, tb, nlat, name):
    R, D = X.shape
    G = sc.shape[0]

    def body(x_ref, g_ref, sc_ref, dh_ref, dup_ref, dx_ref, dg_ref, dsh_ref, dsc_ref):
        i = pl.program_id(0)
        x = x_ref[...]
        g = g_ref[...]
        r = lax.rsqrt(jnp.mean(x * x, axis=-1, keepdims=True) + EPS)
        xh = x * r
        dh_ = dh_ref[...].astype(F32)
        t = dh_ * (1.0 + sc_ref[0])
        tg = t * g
        dx_ref[...] = dup_ref[...] + r * (tg - xh * jnp.mean(tg * xh, axis=-1, keepdims=True))
        _accum(dg_ref, jnp.sum(t * xh, axis=0, keepdims=True), i == 0)
        first = i % nlat == 0
        _accum(dsh_ref, jnp.sum(dh_, axis=0, keepdims=True)[None], first)
        _accum(dsc_ref, jnp.sum(dh_ * xh * g, axis=0, keepdims=True)[None], first)

    row = BS((tb, D), lambda i: (i, 0))
    grp = BS((1, 1, D), lambda i: (i // nlat, 0, 0))
    return pl.pallas_call(
        body, name=name,
        out_shape=(jax.ShapeDtypeStruct((R, D), F32), jax.ShapeDtypeStruct((1, D), F32),
                   jax.ShapeDtypeStruct((G, 1, D), F32), jax.ShapeDtypeStruct((G, 1, D), F32)),
        grid=(R // tb,), in_specs=[row, BS((1, D), lambda i: (0, 0)), grp, row, row],
        out_specs=(row, BS((1, D), lambda i: (0, 0)), grp, grp), compiler_params=_cp(("arbitrary",)),
    )(X, gain.reshape(1, D), sc, dh, dup)


def _gate_res_fwd(X, y, gate, tb, nlat, name):
    R, D = X.shape

    def body(x_ref, y_ref, g_ref, o_ref):
        o_ref[...] = x_ref[...] + g_ref[0] * y_ref[...]

    row = BS((tb, D), lambda i: (i, 0))
    return pl.pallas_call(
        body, name=name, out_shape=jax.ShapeDtypeStruct((R, D), F32), grid=(R // tb,),
        in_specs=[row, row, BS((1, 1, D), lambda i: (i // nlat, 0, 0))], out_specs=row, compiler_params=_cp(("parallel",)),
    )(X, y, gate)


def _gate_bwd(dX, y, gate, tb, nlat, name):
    R, D = dX.shape
    G = gate.shape[0]

    def body(dx_ref, y_ref, g_ref, dy_ref, dg_ref):
        i = pl.program_id(0)
        dx = dx_ref[...]
        dy_ref[...] = (g_ref[0] * dx).astype(BF16)
        _accum(dg_ref, jnp.sum(dx * y_ref[...], axis=0, keepdims=True)[None], i % nlat == 0)

    row = BS((tb, D), lambda i: (i, 0))
    grp = BS((1, 1, D), lambda i: (i // nlat, 0, 0))
    return pl.pallas_call(
        body, name=name, out_shape=(jax.ShapeDtypeStruct((R, D), BF16), jax.ShapeDtypeStruct((G, 1, D), F32)),
        grid=(R // tb,), in_specs=[row, row, grp], out_specs=(row, grp), compiler_params=_cp(("arbitrary",)),
    )(dX, y, gate)


def _final_loss(X, gain, target, tb, name):
    R, D = X.shape

    def body(x_ref, g_ref, t_ref, loss_ref, dx_ref, dg_ref):
        i = pl.program_id(0)
        x = x_ref[...]
        g = g_ref[...]
        r = lax.rsqrt(jnp.mean(x * x, axis=-1, keepdims=True) + EPS)
        xh = x * r
        e = xh * g - t_ref[...]
        part = jnp.sum(jnp.sum(e * e, axis=1, keepdims=True), axis=0, keepdims=True) * (0.5 / D)
        _accum(loss_ref, jnp.broadcast_to(part, (1, LANES)), i == 0)
        dy = e * (1.0 / D)
        _accum(dg_ref, jnp.sum(dy * xh, axis=0, keepdims=True), i == 0)
        tg = dy * g
        dx_ref[...] = r * (tg - xh * jnp.mean(tg * xh, axis=-1, keepdims=True))

    row = BS((tb, D), lambda i: (i, 0))
    return pl.pallas_call(
        body, name=name,
        out_shape=(jax.ShapeDtypeStruct((1, LANES), F32), jax.ShapeDtypeStruct((R, D), F32), jax.ShapeDtypeStruct((1, D), F32)),
        grid=(R // tb,), in_specs=[row, BS((1, D), lambda i: (0, 0)), row],
        out_specs=(BS((1, LANES), lambda i: (0, 0)), row, BS((1, D), lambda i: (0, 0))), compiler_params=_cp(("arbitrary",)),
    )(X, gain.reshape(1, D), target)


def _ln_stats(v):
    mu = jnp.mean(v, axis=-1, keepdims=True)
    d = v - mu
    r = lax.rsqrt(jnp.mean(d * d, axis=-1, keepdims=True) + EPS)
    return d * r, r


def _gmlp_fwd(z, ln_g, ln_b, w_s, b_s_full, tb, name):
    R = z.shape[0]
    AW = ln_g.shape[-1]
    AH = w_s.shape[0]

    def body(zu_ref, zv_ref, g_ref, b_ref, ws_ref, bs_ref, o_ref):
        u, _ = _gelu(zu_ref[...].astype(F32))
        v, _ = _gelu(zv_ref[...].astype(F32))
        xh, _ = _ln_stats(v)
        vn = (xh * g_ref[...] + b_ref[...]).astype(BF16)
        for n in range(tb // CHUNK):
            rs = slice(n * CHUNK, (n + 1) * CHUNK)
            for h in range(AH):
                cs = slice(h * CHUNK, (h + 1) * CHUNK)
                v2 = jnp.dot(ws_ref[h].astype(BF16), vn[rs, cs], preferred_element_type=F32) + bs_ref[h]
                o_ref[rs, cs] = (u[rs, cs] * v2).astype(BF16)

    full3 = lambda s: BS(s, lambda i: (0, 0, 0))
    return pl.pallas_call(
        body, name=name, out_shape=jax.ShapeDtypeStruct((R, AW), BF16), grid=(R // tb,),
        in_specs=[BS((tb, AW), lambda i: (i, 0)), BS((tb, AW), lambda i: (i, 1)), BS((1, AW), lambda i: (0, 0)),
                  BS((1, AW), lambda i: (0, 0)), full3((AH, CHUNK, CHUNK)), full3((AH, CHUNK, CHUNK))],
        out_specs=BS((tb, AW), lambda i: (i, 0)), compiler_params=_cp(("parallel",)),
    )(z, z, ln_g.reshape(1, AW), ln_b.reshape(1, AW), w_s, b_s_full)


def _gmlp_bwd(z, dya, ln_g, ln_b, w_s, b_s_full, tb, name):
    R = z.shape[0]
    AW = ln_g.shape[-1]
    AH = w_s.shape[0]

    def body(zu_ref, zv_ref, dy_ref, g_ref, b_ref, ws_ref, bs_ref, dzu_ref, dzv_ref, dg_ref, db_ref, dws_ref, dbs_ref, cs_u_ref, cs_v_ref,
             dvn_scr):
        i = pl.program_id(0)
        first = i == 0
        zu = zu_ref[...].astype(F32)
        zv = zv_ref[...].astype(F32)
        u, thu = _gelu(zu)
        v, thv = _gelu(zv)
        xh, r = _ln_stats(v)
        g = g_ref[...]
        vn = (xh * g + b_ref[...]).astype(BF16)
        dy = dy_ref[...].astype(F32)

        @pl.when(first)
        def _():
            dws_ref[...] = jnp.zeros_like(dws_ref)
            dbs_ref[...] = jnp.zeros_like(dbs_ref)

        for n in range(tb // CHUNK):
            rs = slice(n * CHUNK, (n + 1) * CHUNK)
            for h in range(AH):
                cs = slice(h * CHUNK, (h + 1) * CHUNK)
                w = ws_ref[h].astype(BF16)
                v2 = jnp.dot(w, vn[rs, cs], preferred_element_type=F32) + bs_ref[h]
                dzu_ref[rs, cs] = (dy[rs, cs] * v2 * _gelu_grad(zu[rs, cs], thu[rs, cs])).astype(BF16)
                dv2 = dy[rs, cs] * u[rs, cs]
                dv2b = dv2.astype(BF16)
                dvn_scr[rs, cs] = lax.dot_general(w, dv2b, _DIMS["tn"], preferred_element_type=F32)
                dws_ref[h] += lax.dot_general(dv2b, vn[rs, cs], _DIMS["nt"], preferred_element_type=F32)
                dbs_ref[h] += jnp.sum(dv2, axis=1, keepdims=True)
        dvn = dvn_scr[...]
        _accum(dg_ref, jnp.sum(dvn * xh, axis=0, keepdims=True), first)
        _accum(db_ref, jnp.sum(dvn, axis=0, keepdims=True), first)
        t = dvn * g
        dv = r * (t - jnp.mean(t, axis=-1, keepdims=True) - xh * jnp.mean(t * xh, axis=-1, keepdims=True))
        dzv = dv * _gelu_grad(zv, thv)
        dzv_ref[...] = dzv.astype(BF16)
        _accum(cs_v_ref, jnp.sum(dzv, axis=0, keepdims=True), first)
        _accum(cs_u_ref, jnp.sum(dzu_ref[...].astype(F32), axis=0, keepdims=True), first)

    full3 = lambda s: BS(s, lambda i: (0, 0, 0))
    vec = BS((1, AW), lambda i: (0, 0))
    row = BS((tb, AW), lambda i: (i, 0))
    outs = pl.pallas_call(
        body, name=name,
        out_shape=(jax.ShapeDtypeStruct((R, AW), BF16), jax.ShapeDtypeStruct((R, AW), BF16), jax.ShapeDtypeStruct((1, AW), F32),
                   jax.ShapeDtypeStruct((1, AW), F32), jax.ShapeDtypeStruct((AH, CHUNK, CHUNK), F32),
                   jax.ShapeDtypeStruct((AH, CHUNK, 1), F32), jax.ShapeDtypeStruct((1, AW), F32), jax.ShapeDtypeStruct((1, AW), F32)),
        grid=(R // tb,),
        in_specs=[row, BS((tb, AW), lambda i: (i, 1)), row, vec, vec, full3((AH, CHUNK, CHUNK)), full3((AH, CHUNK, CHUNK))],
        out_specs=(row, row, vec, vec, full3((AH, CHUNK, CHUNK)), full3((AH, CHUNK, 1)), vec, vec),
        scratch_shapes=[pltpu.VMEM((tb, AW), F32)], compiler_params=_cp(("arbitrary",)),
    )(z, z, dya, ln_g.reshape(1, AW), ln_b.reshape(1, AW), w_s, b_s_full)
    return outs


def _segments(R, L):
    return [(0, L)] + ([(L, R - L)] if R > L else [])


def _scr_rows(R, L):
    return R + CONV_PAD * (len(_segments(R, L)) + 1)


def _scr_off(s, start):
    return CONV_PAD * (s + 1) + start


def _zero_pads(scr, R, L):
    segs = _segments(R, L)
    z = jnp.zeros((CONV_PAD, scr.shape[1]), F32)
    for s, (start, n) in enumerate(segs):
        scr[pl.ds(_scr_off(s, start) - CONV_PAD, CONV_PAD), :] = z
    last_s, (last_start, last_n) = len(segs) - 1, segs[-1]
    scr[pl.ds(_scr_off(last_s, last_start) + last_n, CONV_PAD), :] = z


def _for_chunks(R, L, ch, fn):
    for s, (start, n) in enumerate(_segments(R, L)):
        c = min(ch, n)
        off = _scr_off(s, start)

        def step(i, carry, start=start, off=off, c=c):
            r0 = pl.multiple_of(start + i * c, SUBLANES)
            fn(r0, pl.multiple_of(off + i * c, SUBLANES), c)
            return carry

        lax.fori_loop(0, n // c, step, 0)


def _taps(scr, srow, c, w_ref, ntap, flip):
    win = scr[pl.ds(srow - CONV_PAD, c + 2 * CONV_PAD), :]
    n = c + 2 * CONV_PAD
    acc = None
    for k in range(ntap):
        o = k - (ntap - 1) // 2
        if flip:
            o = -o
        sh = win if o == 0 else pltpu.roll(win, (-o) % n, 0)
        term = w_ref[k:k + 1, :] * sh[CONV_PAD:CONV_PAD + c]
        acc = term if acc is None else acc + term
    return acc


def _tap_grads(scr, srow, c, dy, dw_ref, ntap):
    win = scr[pl.ds(srow - CONV_PAD, c + 2 * CONV_PAD), :]
    n = c + 2 * CONV_PAD
    for k in range(ntap):
        o = k - (ntap - 1) // 2
        sh = win if o == 0 else pltpu.roll(win, (-o) % n, 0)
        dw_ref[k:k + 1, :] += jnp.sum(dy * sh[CONV_PAD:CONV_PAD + c], axis=0, keepdims=True)


def _glu_conv_fwd(z, col0, conv_w, conv_b, L, ch, name):
    R = z.shape[0]
    BW = conv_w.shape[1]
    nb, c0 = BW // LANES, col0 // LANES

    def body(a_ref, g_ref, w_ref, b_ref, o_ref, scr):
        _zero_pads(scr, R, L)

        def fill(r0, s0, c):
            a = a_ref[pl.ds(r0, c), :].astype(F32)
            g = g_ref[pl.ds(r0, c), :].astype(F32)
            scr[pl.ds(s0, c), :] = a * _sigmoid(g)

        _for_chunks(R, L, ch, fill)

        def conv(r0, s0, c):
            o_ref[pl.ds(r0, c), :] = _taps(scr, s0, c, w_ref, B_CONV, False) + b_ref[...]

        _for_chunks(R, L, ch, conv)

    return pl.pallas_call(
        body, name=name, out_shape=jax.ShapeDtypeStruct((R, BW), F32), grid=(nb,),
        in_specs=[BS((R, LANES), lambda j: (0, c0 + j)), BS((R, LANES), lambda j: (0, c0 + nb + j)),
                  BS((B_CONV, LANES), lambda j: (0, j)), BS((1, LANES), lambda j: (0, j))],
        out_specs=BS((R, LANES), lambda j: (0, j)), scratch_shapes=[pltpu.VMEM((_scr_rows(R, L), LANES), F32)],
        compiler_params=_cp(("parallel",)),
    )(z, z, conv_w, conv_b.reshape(1, BW))


def _glu_conv_bwd(z, col0, dhc, conv_w, L, ch, name):
    R = z.shape[0]
    BW = conv_w.shape[1]
    nb, c0 = BW // LANES, col0 // LANES

    def body(a_ref, g_ref, dy_ref, w_ref, da_ref, dg_ref, dw_ref, db_ref, csa_ref, csg_ref, scr_h, scr_dy):
        _zero_pads(scr_h, R, L)
        _zero_pads(scr_dy, R, L)
        dw_ref[...] = jnp.zeros_like(dw_ref)
        db_ref[...] = jnp.zeros_like(db_ref)
        csa_ref[...] = jnp.zeros_like(csa_ref)
        csg_ref[...] = jnp.zeros_like(csg_ref)

        def fill(r0, s0, c):
            a = a_ref[pl.ds(r0, c), :].astype(F32)
            g = g_ref[pl.ds(r0, c), :].astype(F32)
            scr_h[pl.ds(s0, c), :] = a * _sigmoid(g)
            scr_dy[pl.ds(s0, c), :] = dy_ref[pl.ds(r0, c), :]

        _for_chunks(R, L, ch, fill)

        def back(r0, s0, c):
            dh = _taps(scr_dy, s0, c, w_ref, B_CONV, True)
            a = a_ref[pl.ds(r0, c), :].astype(F32)
            sg = _sigmoid(g_ref[pl.ds(r0, c), :].astype(F32))
            da = dh * sg
            dg = dh * a * sg * (1.0 - sg)
            da_ref[pl.ds(r0, c), :] = da.astype(BF16)
            dg_ref[pl.ds(r0, c), :] = dg.astype(BF16)
            csa_ref[...] += jnp.sum(da, axis=0, keepdims=True)
            csg_ref[...] += jnp.sum(dg, axis=0, keepdims=True)
            dy = dy_ref[pl.ds(r0, c), :]
            db_ref[...] += jnp.sum(dy, axis=0, keepdims=True)
            _tap_grads(scr_h, s0, c, dy, dw_ref, B_CONV)

        _for_chunks(R, L, ch, back)

    col = BS((R, LANES), lambda j: (0, j))
    vec = BS((1, LANES), lambda j: (0, j))
    nrow = _scr_rows(R, L)
    return pl.pallas_call(
        body, name=name,
        out_shape=(jax.ShapeDtypeStruct((R, BW), BF16), jax.ShapeDtypeStruct((R, BW), BF16), jax.ShapeDtypeStruct((B_CONV, BW), F32),
                   jax.ShapeDtypeStruct((1, BW), F32), jax.ShapeDtypeStruct((1, BW), F32), jax.ShapeDtypeStruct((1, BW), F32)),
        grid=(nb,),
        in_specs=[BS((R, LANES), lambda j: (0, c0 + j)), BS((R, LANES), lambda j: (0, c0 + nb + j)), col,
                  BS((B_CONV, LANES), lambda j: (0, j))],
        out_specs=(col, col, BS((B_CONV, LANES), lambda j: (0, j)), vec, vec, vec),
        scratch_shapes=[pltpu.VMEM((nrow, LANES), F32), pltpu.VMEM((nrow, LANES), F32)], compiler_params=_cp(("parallel",)),
    )(z, z, dhc, conv_w)


def _ln_silu_fwd(hc, ln_g, ln_b, tb, name):
    R, W = hc.shape

    def body(x_ref, g_ref, b_ref, o_ref):
        xh, _ = _ln_stats(x_ref[...])
        y = xh * g_ref[...] + b_ref[...]
        o_ref[...] = (y * _sigmoid(y)).astype(BF16)

    vec = BS((1, W), lambda i: (0, 0))
    row = BS((tb, W), lambda i: (i, 0))
    return pl.pallas_call(
        body, name=name, out_shape=jax.ShapeDtypeStruct((R, W), BF16), grid=(R // tb,), in_specs=[row, vec, vec], out_specs=row,
        compiler_params=_cp(("parallel",)),
    )(hc, ln_g.reshape(1, W), ln_b.reshape(1, W))


def _ln_silu_bwd(hc, dyb, col0, ln_g, ln_b, tb, name):
    R, W = hc.shape
    c0 = col0 // W

    def body(x_ref, dy_ref, g_ref, b_ref, dx_ref, dg_ref, db_ref):
        i = pl.program_id(0)
        xh, r = _ln_stats(x_ref[...])
        g = g_ref[...]
        y = xh * g + b_ref[...]
        s = _sigmoid(y)
        dy = dy_ref[...].astype(F32) * s * (1.0 + y * (1.0 - s))
        _accum(dg_ref, jnp.sum(dy * xh, axis=0, keepdims=True), i == 0)
        _accum(db_ref, jnp.sum(dy, axis=0, keepdims=True), i == 0)
        t = dy * g
        dx_ref[...] = r * (t - jnp.mean(t, axis=-1, keepdims=True) - xh * jnp.mean(t * xh, axis=-1, keepdims=True))

    vec = BS((1, W), lambda i: (0, 0))
    row = BS((tb, W), lambda i: (i, 0))
    return pl.pallas_call(
        body, name=name,
        out_shape=(jax.ShapeDtypeStruct((R, W), F32), jax.ShapeDtypeStruct((1, W), F32), jax.ShapeDtypeStruct((1, W), F32)),
        grid=(R // tb,), in_specs=[row, BS((tb, W), lambda i: (i, c0)), vec, vec], out_specs=(row, vec, vec),
        compiler_params=_cp(("arbitrary",)),
    )(hc, dyb, ln_g.reshape(1, W), ln_b.reshape(1, W))


def _ffn_act_fwd(zf, conv_w, conv_b, L, ch, name):
    R = zf.shape[0]
    DFF = conv_w.shape[1]
    nb = DFF // LANES

    def body(g_ref, u_ref, w_ref, b_ref, o_ref, scr):
        _zero_pads(scr, R, L)

        def fill(r0, s0, c):
            scr[pl.ds(s0, c), :] = g_ref[pl.ds(r0, c), :].astype(F32)

        _for_chunks(R, L, ch, fill)

        def act(r0, s0, c):
            gc = _taps(scr, s0, c, w_ref, FFN_CONV, False) + b_ref[...]
            o_ref[pl.ds(r0, c), :] = (gc * _sigmoid(gc) * u_ref[pl.ds(r0, c), :].astype(F32)).astype(BF16)

        _for_chunks(R, L, ch, act)

    return pl.pallas_call(
        body, name=name, out_shape=jax.ShapeDtypeStruct((R, DFF), BF16), grid=(nb,),
        in_specs=[BS((R, LANES), lambda j: (0, j)), BS((R, LANES), lambda j: (0, nb + j)), BS((FFN_CONV, LANES), lambda j: (0, j)),
                  BS((1, LANES), lambda j: (0, j))],
        out_specs=BS((R, LANES), lambda j: (0, j)), scratch_shapes=[pltpu.VMEM((_scr_rows(R, L), LANES), F32)],
        compiler_params=_cp(("parallel",)),
    )(zf, zf, conv_w, conv_b.reshape(1, DFF))


def _ffn_act_bwd(zf, df, conv_w, conv_b, L, ch, name):
    R = zf.shape[0]
    DFF = conv_w.shape[1]
    nb = DFF // LANES

    def body(g_ref, u_ref, df_ref, w_ref, b_ref, dg_ref, du_ref, dw_ref, db_ref, scr_g, scr_d):
        _zero_pads(scr_g, R, L)
        _zero_pads(scr_d, R, L)
        dw_ref[...] = jnp.zeros_like(dw_ref)
        db_ref[...] = jnp.zeros_like(db_ref)

        def fill(r0, s0, c):
            scr_g[pl.ds(s0, c), :] = g_ref[pl.ds(r0, c), :].astype(F32)

        _for_chunks(R, L, ch, fill)

        def pre(r0, s0, c):
            gc = _taps(scr_g, s0, c, w_ref, FFN_CONV, False) + b_ref[...]
            s = _sigmoid(gc)
            d = df_ref[pl.ds(r0, c), :].astype(F32)
            du_ref[pl.ds(r0, c), :] = (d * gc * s).astype(BF16)
            dgc = d * u_ref[pl.ds(r0, c), :].astype(F32) * s * (1.0 + gc * (1.0 - s))
            scr_d[pl.ds(s0, c), :] = dgc
            db_ref[...] += jnp.sum(dgc, axis=0, keepdims=True)
            _tap_grads(scr_g, s0, c, dgc, dw_ref, FFN_CONV)

        _for_chunks(R, L, ch, pre)

        def back(r0, s0, c):
            dg_ref[pl.ds(r0, c), :] = _taps(scr_d, s0, c, w_ref, FFN_CONV, True).astype(BF16)

        _for_chunks(R, L, ch, back)

    col = BS((R, LANES), lambda j: (0, j))
    vec = BS((1, LANES), lambda j: (0, j))
    nrow = _scr_rows(R, L)
    return pl.pallas_call(
        body, name=name,
        out_shape=(jax.ShapeDtypeStruct((R, DFF), BF16), jax.ShapeDtypeStruct((R, DFF), BF16), jax.ShapeDtypeStruct((FFN_CONV, DFF), F32),
                   jax.ShapeDtypeStruct((1, DFF), F32)),
        grid=(nb,),
        in_specs=[col, BS((R, LANES), lambda j: (0, nb + j)), col, BS((FFN_CONV, LANES), lambda j: (0, j)), vec],
        out_specs=(col, col, BS((FFN_CONV, LANES), lambda j: (0, j)), vec),
        scratch_shapes=[pltpu.VMEM((nrow, LANES), F32), pltpu.VMEM((nrow, LANES), F32)], compiler_params=_cp(("parallel",)),
    )(zf, zf, df, conv_w, conv_b.reshape(1, DFF))


def _rope_tables(L, T):
    rows = L // GRID_W
    row = jnp.repeat(jnp.arange(rows, dtype=F32), GRID_W)
    col = jnp.tile(jnp.arange(GRID_W, dtype=F32), rows)
    n_freq = ROPE // 4
    inv = ROPE_THETA ** (-jnp.arange(n_freq, dtype=F32) / n_freq)
    ang = jnp.concatenate([row[:, None] * inv, col[:, None] * inv], axis=-1)
    cos, sin = jnp.cos(ang), jnp.sin(ang)
    half = ROPE // 2
    zero = jnp.zeros((L, half), F32)
    cos_t = jnp.concatenate([cos, cos, jnp.ones((L, LANES - ROPE), F32)], axis=1)
    sa = jnp.concatenate([zero, sin, zero, zero], axis=1)
    sb = jnp.concatenate([-sin, zero, zero, zero], axis=1)
    pad = T - L
    cos_t = jnp.concatenate([cos_t, jnp.ones((pad, LANES), F32)], axis=0)
    sa = jnp.concatenate([sa, jnp.zeros((pad, LANES), F32)], axis=0)
    sb = jnp.concatenate([sb, jnp.zeros((pad, LANES), F32)], axis=0)
    return cos_t, sa, sb


def _rope(x, cos, sa, sb):
    half = ROPE // 2
    return x * cos + pltpu.roll(x, half, 1) * sa + pltpu.roll(x, LANES - half, 1) * sb


def _rope_t(d, cos, sa, sb):
    half = ROPE // 2
    return d * cos + pltpu.roll(d * sa, LANES - half, 1) + pltpu.roll(d * sb, half, 1)


def _mla_prep_fwd(zm, qg, kvg, tabs, tb, name):
    T, W = zm.shape
    QL, KVL = qg.shape[-1], kvg.shape[-1]

    def body(z_ref, qg_ref, kg_ref, cos_ref, sa_ref, sb_ref, q_ref, k_ref, p_ref):
        cq = z_ref[:, :QL]
        r = lax.rsqrt(jnp.mean(cq * cq, axis=-1, keepdims=True) + EPS)
        q_ref[...] = (cq * r * qg_ref[...]).astype(BF16)
        ck = z_ref[:, QL:QL + KVL]
        r = lax.rsqrt(jnp.mean(ck * ck, axis=-1, keepdims=True) + EPS)
        k_ref[...] = (ck * r * kg_ref[...]).astype(BF16)
        p_ref[...] = _rope(z_ref[:, QL + KVL:], cos_ref[...], sa_ref[...], sb_ref[...]).astype(BF16)

    tab = BS((tb, LANES), lambda i: (i, 0))
    return pl.pallas_call(
        body, name=name,
        out_shape=(jax.ShapeDtypeStruct((T, QL), BF16), jax.ShapeDtypeStruct((T, KVL), BF16), jax.ShapeDtypeStruct((T, LANES), BF16)),
        grid=(T // tb,),
        in_specs=[BS((tb, W), lambda i: (i, 0)), BS((1, QL), lambda i: (0, 0)), BS((1, KVL), lambda i: (0, 0)), tab, tab, tab],
        out_specs=(BS((tb, QL), lambda i: (i, 0)), BS((tb, KVL), lambda i: (i, 0)), tab), compiler_params=_cp(("parallel",)),
    )(zm, qg.reshape(1, QL), kvg.reshape(1, KVL), *tabs)


def _mla_prep_bwd(zm, qg, kvg, tabs, dq, dk, dp, tb, name):
    T, W = zm.shape
    QL, KVL = qg.shape[-1], kvg.shape[-1]

    def body(z_ref, qg_ref, kg_ref, cos_ref, sa_ref, sb_ref, dq_ref, dk_ref, dp_ref, dz_ref, dqg_ref, dkg_ref):
        i = pl.program_id(0)

        def rms_bwd(x, g, dy):
            r = lax.rsqrt(jnp.mean(x * x, axis=-1, keepdims=True) + EPS)
            xh = x * r
            t = dy * g
            return r * (t - xh * jnp.mean(t * xh, axis=-1, keepdims=True)), jnp.sum(dy * xh, axis=0, keepdims=True)

        dcq, dg = rms_bwd(z_ref[:, :QL], qg_ref[...], dq_ref[...].astype(F32))
        dz_ref[:, :QL] = dcq.astype(BF16)
        _accum(dqg_ref, dg, i == 0)
        dck, dg = rms_bwd(z_ref[:, QL:QL + KVL], kg_ref[...], dk_ref[...].astype(F32))
        dz_ref[:, QL:QL + KVL] = dck.astype(BF16)
        _accum(dkg_ref, dg, i == 0)
        dz_ref[:, QL + KVL:] = _rope_t(dp_ref[...], cos_ref[...], sa_ref[...], sb_ref[...]).astype(BF16)

    tab = BS((tb, LANES), lambda i: (i, 0))
    return pl.pallas_call(
        body, name=name,
        out_shape=(jax.ShapeDtypeStruct((T, W), BF16), jax.ShapeDtypeStruct((1, QL), F32), jax.ShapeDtypeStruct((1, KVL), F32)),
        grid=(T // tb,),
        in_specs=[BS((tb, W), lambda i: (i, 0)), BS((1, QL), lambda i: (0, 0)), BS((1, KVL), lambda i: (0, 0)), tab, tab, tab,
                  BS((tb, QL), lambda i: (i, 0)), BS((tb, KVL), lambda i: (i, 0)), tab],
        out_specs=(BS((tb, W), lambda i: (i, 0)), BS((1, QL), lambda i: (0, 0)), BS((1, KVL), lambda i: (0, 0))),
        compiler_params=_cp(("arbitrary",)),
    )(zm, qg.reshape(1, QL), kvg.reshape(1, KVL), *tabs, dq, dk, dp)


def _attn_fwd(q, kv, kpe, tabs, L, tq, name):
    T = kv.shape[0]
    H = kv.shape[1] // QHEAD
    scale = (HEAD + ROPE) ** -0.5

    def body(q_ref, kv_ref, kpe_ref, cos_ref, sa_ref, sb_ref, o_ref, lse_ref, kcat):
        @pl.when(pl.program_id(1) == 0)
        def _():
            kcat[:, :HEAD] = kv_ref[:, :HEAD]
            kcat[:, HEAD:] = kpe_ref[...]

        qp = _rope(q_ref[:, HEAD:].astype(F32), cos_ref[...], sa_ref[...], sb_ref[...]).astype(BF16)
        qc = jnp.concatenate([q_ref[:, :HEAD], qp], axis=1)
        s = lax.dot_general(qc, kcat[...], _DIMS["nt"], preferred_element_type=F32) * scale
        m = jnp.max(s, axis=-1, keepdims=True)
        p = jnp.exp(s - m)
        l = jnp.sum(p, axis=-1, keepdims=True)
        o = jnp.dot(p.astype(BF16), kv_ref[:, HEAD:], preferred_element_type=F32)
        o_ref[...] = (o / l).astype(BF16)
        lse_ref[0] = m + jnp.log(l)

    tab = BS((tq, LANES), lambda h, i: (i, 0))
    return pl.pallas_call(
        body, name=name, out_shape=(jax.ShapeDtypeStruct((L, H * HEAD), BF16), jax.ShapeDtypeStruct((H, L, 1), F32)),
        grid=(H, L // tq),
        in_specs=[BS((tq, QHEAD), lambda h, i: (i, h)), BS((T, QHEAD), lambda h, i: (0, h)), BS((T, LANES), lambda h, i: (0, 0)),
                  tab, tab, tab],
        out_specs=(BS((tq, HEAD), lambda h, i: (i, h)), BS((1, tq, 1), lambda h, i: (h, i, 0))),
        scratch_shapes=[pltpu.VMEM((T, QHEAD), BF16)], compiler_params=_cp(("parallel", "arbitrary")),
    )(q, kv, kpe, *tabs)


def _attn_bwd(q, kv, kpe, tabs, o, lse, do, L, tq, name):
    T = kv.shape[0]
    H = kv.shape[1] // QHEAD
    scale = (HEAD + ROPE) ** -0.5
    nq = L // tq

    def body(q_ref, kv_ref, kpe_ref, cos_ref, sa_ref, sb_ref, o_ref, lse_ref, do_ref, dq_ref, dkv_ref, dkpe_ref, kcat, dk_acc, dv_acc):
        h, i = pl.program_id(0), pl.program_id(1)

        @pl.when(i == 0)
        def _():
            kcat[:, :HEAD] = kv_ref[:, :HEAD]
            kcat[:, HEAD:] = kpe_ref[...]
            dk_acc[...] = jnp.zeros_like(dk_acc)
            dv_acc[...] = jnp.zeros_like(dv_acc)

        cos, sa, sb = cos_ref[...], sa_ref[...], sb_ref[...]
        qp = _rope(q_ref[:, HEAD:].astype(F32), cos, sa, sb).astype(BF16)
        qc = jnp.concatenate([q_ref[:, :HEAD], qp], axis=1)
        s = lax.dot_general(qc, kcat[...], _DIMS["nt"], preferred_element_type=F32) * scale
        p = jnp.exp(s - lse_ref[0])
        dov = do_ref[...]
        delta = jnp.sum(dov.astype(F32) * o_ref[...].astype(F32), axis=-1, keepdims=True)
        dp = lax.dot_general(dov, kv_ref[:, HEAD:], _DIMS["nt"], preferred_element_type=F32)
        ds = (p * (dp - delta) * scale).astype(BF16)
        dqc = jnp.dot(ds, kcat[...], preferred_element_type=F32)
        dq_ref[:, :HEAD] = dqc[:, :HEAD].astype(BF16)
        dq_ref[:, HEAD:] = _rope_t(dqc[:, HEAD:], cos, sa, sb).astype(BF16)
        dk_acc[...] += lax.dot_general(ds, qc, _DIMS["tn"], preferred_element_type=F32)
        dv_acc[...] += lax.dot_general(p.astype(BF16), dov, _DIMS["tn"], preferred_element_type=F32)

        @pl.when(i == nq - 1)
        def _():
            dkv_ref[:, :HEAD] = dk_acc[:, :HEAD].astype(BF16)
            dkv_ref[:, HEAD:] = dv_acc[...].astype(BF16)

            @pl.when(h == 0)
            def _():
                dkpe_ref[...] = dk_acc[:, HEAD:]

            @pl.when(h > 0)
            def _():
                dkpe_ref[...] += dk_acc[:, HEAD:]

    tab = BS((tq, LANES), lambda h, i: (i, 0))
    return pl.pallas_call(
        body, name=name,
        out_shape=(jax.ShapeDtypeStruct((L, H * QHEAD), BF16), jax.ShapeDtypeStruct((T, H * QHEAD), BF16), jax.ShapeDtypeStruct((T, LANES), F32)),
        grid=(H, nq),
        in_specs=[BS((tq, QHEAD), lambda h, i: (i, h)), BS((T, QHEAD), lambda h, i: (0, h)), BS((T, LANES), lambda h, i: (0, 0)),
                  tab, tab, tab, BS((tq, HEAD), lambda h, i: (i, h)), BS((1, tq, 1), lambda h, i: (h, i, 0)),
                  BS((tq, HEAD), lambda h, i: (i, h))],
        out_specs=(BS((tq, QHEAD), lambda h, i: (i, h)), BS((T, QHEAD), lambda h, i: (0, h)), BS((T, LANES), lambda h, i: (0, 0))),
        scratch_shapes=[pltpu.VMEM((T, QHEAD), BF16), pltpu.VMEM((T, QHEAD), F32), pltpu.VMEM((T, HEAD), F32)],
        compiler_params=_cp(("arbitrary", "arbitrary")),
    )(q, kv, kpe, *tabs, o, lse, do)


def _adamw(w, g, m, v, name):
    R, C = w.shape
    tr = _row_tile(R, C)
    c1 = 1.0 / (1.0 - ADAM_B1 ** ADAM_STEP)
    c2 = 1.0 / (1.0 - ADAM_B2 ** ADAM_STEP)

    def body(w_ref, g_ref, m_ref, v_ref, d_ref, nm_ref, nv_ref):
        g_ = g_ref[...]
        nm = ADAM_B1 * m_ref[...] + (1.0 - ADAM_B1) * g_
        nv = ADAM_B2 * v_ref[...] + (1.0 - ADAM_B2) * (g_ * g_)
        nm_ref[...] = nm
        nv_ref[...] = nv
        d_ref[...] = -ADAM_LR * ((nm * c1) / (jnp.sqrt(nv * c2) + ADAM_EPS) + ADAM_WD * w_ref[...])

    blk = BS((tr, C), lambda i: (i, 0))
    sd = jax.ShapeDtypeStruct((R, C), F32)
    return pl.pallas_call(body, name=name, out_shape=(sd, sd, sd), grid=(R // tr,), in_specs=[blk] * 4, out_specs=(blk,) * 3,
                          compiler_params=_cp(("parallel",)))(w, g, m, v)


def _sum_lead(a, out_dtype, name):
    n, R, C = a.shape
    tr = _row_tile(R, C * n, 2 << 20)

    def body(a_ref, o_ref):
        acc = a_ref[0].astype(F32)
        for k in range(1, n):
            acc = acc + a_ref[k].astype(F32)
        o_ref[...] = acc.astype(out_dtype)

    return pl.pallas_call(body, name=name, out_shape=jax.ShapeDtypeStruct((R, C), out_dtype), grid=(R // tr,),
                          in_specs=[BS((n, tr, C), lambda i: (0, i, 0))], out_specs=BS((tr, C), lambda i: (i, 0)),
                          compiler_params=_cp(("parallel",)))(a)


def _cctx_grad(parts, c_ctx, name):
    n, D = parts.shape

    def body(p_ref, c_ref, o_ref):
        d = jnp.sum(p_ref[...], axis=0, keepdims=True)
        c = c_ref[...]
        s = _sigmoid(c)
        o_ref[...] = d * s * (1.0 + c * (1.0 - s))

    return pl.pallas_call(body, name=name, out_shape=jax.ShapeDtypeStruct((1, D), F32))(parts, c_ctx.reshape(1, D))


def _me():
    return lax.axis_index("x"), lax.axis_index("y"), lax.axis_index("c")


def _aligned(v, n):
    return v if isinstance(v, int) else pl.multiple_of(v, n)


def _window(ref, r0, c0, R, C):
    rows = pl.ds(_aligned(r0, SUBLANES), R)
    if C == ref.shape[1]:
        return ref.at[rows, :]
    return ref.at[rows, pl.ds(_aligned(c0, LANES), C)]


def _allgather8(items, name):
    n = len(items)

    def body(*refs):
        srcs, dsts = refs[:n], refs[n:2 * n]
        send_sems, recv_sems, local_sems = refs[2 * n:]
        x, y, c = _me()
        me, sibling = (x, y, c), (x, y, 1 - c)
        chips = [(1 - x, y), (x, 1 - y), (1 - x, 1 - y)]

        def dwin(a, blk):
            (R, C), at = items[a][2], items[a][4]
            return _window(dsts[a], *at(*blk), R, C)

        def swin(a):
            (R, C), at = items[a][2], items[a][1]
            return _window(srcs[a], *at(*me), R, C)

        def copy(a, k, blk, to, src=None):
            return pltpu.make_async_remote_copy(
                src_ref=dwin(a, blk) if src is None else src, dst_ref=dwin(a, blk), send_sem=send_sems.at[7 * a + k],
                recv_sem=recv_sems.at[7 * a + k], device_id=to, device_id_type=MESH)

        mine = [pltpu.make_async_copy(swin(a), dwin(a, me), local_sems.at[a]) for a in range(n)]
        for cp in mine:
            cp.start()
        first = []
        for a in range(n):
            first.append(copy(a, 0, me, sibling, src=swin(a)))
            first += [copy(a, 1 + j, me, (*chip, c), src=swin(a)) for j, chip in enumerate(chips)]
        for cp in first:
            cp.start()
        passed = []
        for j, chip in enumerate(chips):
            for a in range(n):
                copy(a, 1 + j, (*chip, c), me).wait_recv()
                fwd = copy(a, 4 + j, (*chip, c), sibling)
                fwd.start()
                passed.append(fwd)
        for a in range(n):
            copy(a, 0, sibling, me).wait_recv()
            for j, chip in enumerate(chips):
                copy(a, 4 + j, (*chip, 1 - c), me).wait_recv()
        for cp in first + passed:
            cp.wait_send()
        for cp in mine:
            cp.wait()

    outs = pl.pallas_call(
        body, name=name, out_shape=tuple(jax.ShapeDtypeStruct(it[3], it[0].dtype) for it in items),
        in_specs=[ANY] * n, out_specs=tuple([ANY] * n),
        scratch_shapes=[pltpu.SemaphoreType.DMA((7 * n,)), pltpu.SemaphoreType.DMA((7 * n,)), pltpu.SemaphoreType.DMA((n,))],
    )(*[it[0] for it in items])
    return list(outs)


def _rs_to_sibling(grads, wins, name):
    n = len(grads)

    def body(*refs):
        srcs, dsts = refs[:n], refs[n:2 * n]
        send_sems, recv_sems = refs[2 * n:]
        x, y, c = _me()
        sibling = (x, y, 1 - c)
        copies = []
        for a in range(n):
            (R, C), at = wins[a]
            for p in range(4):
                cp = pltpu.make_async_remote_copy(
                    src_ref=_window(srcs[a], *at(p, 1 - c), R, C), dst_ref=dsts[a].at[p], send_sem=send_sems.at[4 * a + p],
                    recv_sem=recv_sems.at[4 * a + p], device_id=sibling, device_id_type=MESH)
                cp.start()
                copies.append(cp)
        for cp in copies:
            cp.wait()

    outs = pl.pallas_call(
        body, name=name, out_shape=tuple(jax.ShapeDtypeStruct((4,) + wins[a][0], grads[a].dtype) for a in range(n)),
        in_specs=[ANY] * n, out_specs=tuple([ANY] * n),
        scratch_shapes=[pltpu.SemaphoreType.DMA((4 * n,)), pltpu.SemaphoreType.DMA((4 * n,))],
    )(*grads)
    return list(outs)


def _rs_add_mine(grad, recv, win, name):
    (R, C), kind = win
    tr = _row_tile(R, C, 1 << 20)
    nr = R // tr
    c_arr = lax.axis_index("c").astype(jnp.int32).reshape(1)

    if kind == "col":
        g_spec = BS((tr, C), lambda p, i, c_ref: (c_ref[0] * nr + i, p))
    else:
        g_spec = BS((tr, C), lambda p, i, c_ref: ((2 * p + c_ref[0]) * nr + i, 0))

    def body(c_ref, g_ref, r_ref, o_ref):
        o_ref[0] = (g_ref[...].astype(F32) + r_ref[0].astype(F32)).astype(BF16)

    blk = BS((1, tr, C), lambda p, i, c_ref: (p, i, 0))
    return pl.pallas_call(
        body, name=name, out_shape=jax.ShapeDtypeStruct((4, R, C), BF16),
        grid_spec=pltpu.PrefetchScalarGridSpec(num_scalar_prefetch=1, grid=(4, nr), in_specs=[g_spec, blk], out_specs=blk),
        compiler_params=_cp(("parallel", "parallel")),
    )(c_arr, grad, recv)


def _rs_across_chips(parts, name):
    n = len(parts)

    def body(*refs):
        srcs, dsts = refs[:n], refs[n:2 * n]
        send_sems, recv_sems, local_sems = refs[2 * n:]
        x, y, c = _me()
        mine_p = 2 * x + y
        chips = [(1 - x, y), (x, 1 - y), (1 - x, 1 - y)]
        local = [pltpu.make_async_copy(srcs[a].at[mine_p], dsts[a].at[mine_p], local_sems.at[a]) for a in range(n)]
        for cp in local:
            cp.start()
        copies = []
        for a in range(n):
            for j, (px, py) in enumerate(chips):
                cp = pltpu.make_async_remote_copy(
                    src_ref=srcs[a].at[2 * px + py], dst_ref=dsts[a].at[mine_p], send_sem=send_sems.at[3 * a + j],
                    recv_sem=recv_sems.at[3 * a + j], device_id=(px, py, c), device_id_type=MESH)
                cp.start()
                copies.append((a, j, cp))
        for a, j, cp in copies:
            px, py = chips[j]
            pltpu.make_async_remote_copy(
                src_ref=srcs[a].at[mine_p], dst_ref=dsts[a].at[2 * px + py], send_sem=send_sems.at[3 * a + j],
                recv_sem=recv_sems.at[3 * a + j], device_id=(px, py, c), device_id_type=MESH).wait_recv()
        for a, j, cp in copies:
            cp.wait_send()
        for cp in local:
            cp.wait()

    outs = pl.pallas_call(
        body, name=name, out_shape=tuple(jax.ShapeDtypeStruct(p.shape, p.dtype) for p in parts),
        in_specs=[ANY] * n, out_specs=tuple([ANY] * n),
        scratch_shapes=[pltpu.SemaphoreType.DMA((3 * n,)), pltpu.SemaphoreType.DMA((3 * n,)), pltpu.SemaphoreType.DMA((n,))],
    )(*parts)
    return list(outs)


def _rs_share(halves, name):
    n = len(halves)

    def body(*refs):
        srcs, dsts = refs[:n], refs[n:2 * n]
        send_sems, recv_sems, local_sems = refs[2 * n:]
        x, y, c = _me()
        sibling = (x, y, 1 - c)
        local, remote = [], []
        for a in range(n):
            R = srcs[a].shape[0]
            mine = dsts[a].at[pl.ds(pl.multiple_of(c * R, SUBLANES), R), :]
            local.append(pltpu.make_async_copy(srcs[a], mine, local_sems.at[a]))
            remote.append(pltpu.make_async_remote_copy(src_ref=srcs[a], dst_ref=mine, send_sem=send_sems.at[a],
                                                       recv_sem=recv_sems.at[a], device_id=sibling, device_id_type=MESH))
        for cp in local + remote:
            cp.start()
        for a in range(n):
            R = srcs[a].shape[0]
            theirs = dsts[a].at[pl.ds(pl.multiple_of((1 - c) * R, SUBLANES), R), :]
            pltpu.make_async_remote_copy(src_ref=srcs[a], dst_ref=theirs, send_sem=send_sems.at[a], recv_sem=recv_sems.at[a],
                                         device_id=sibling, device_id_type=MESH).wait_recv()
        for cp in remote:
            cp.wait_send()
        for cp in local:
            cp.wait()

    outs = pl.pallas_call(
        body, name=name, out_shape=tuple(jax.ShapeDtypeStruct((2 * h.shape[0], h.shape[1]), h.dtype) for h in halves),
        in_specs=[ANY] * n, out_specs=tuple([ANY] * n),
        scratch_shapes=[pltpu.SemaphoreType.DMA((n,)), pltpu.SemaphoreType.DMA((n,)), pltpu.SemaphoreType.DMA((n,))],
    )(*halves)
    return list(outs)


BLOB_ALIGN = SUBLANES * LANES


def _pack(arrs):
    flat = jnp.concatenate([a.reshape(-1).astype(F32) for a in arrs])
    n = flat.shape[0]
    padded = -(-n // BLOB_ALIGN) * BLOB_ALIGN
    return jnp.pad(flat, (0, padded - n)).reshape(padded // LANES, LANES)


def _unpack(flat, shapes):
    out, off = [], 0
    for s in shapes:
        n = math.prod(s)
        out.append(flat[..., off:off + n].reshape(flat.shape[:-1] + tuple(s)))
        off += n
    return out


def _gather_blob(blob, name):
    r = blob.shape[0]
    at = lambda px, py, pc: ((4 * px + 2 * py + pc) * r, 0)
    (out,) = _allgather8([(blob, lambda px, py, pc: (0, 0), (r, LANES), (8 * r, LANES), at)], name)
    return out.reshape(8, r * LANES)


def _conv_ffn_fwd(X, mods, n2g, w_up, conv_w, conv_b, w_down, L, tb, nlat, ch, tag):
    sh2, sc2, g2 = mods[3], mods[4], mods[5]
    h2 = _norm_mod_fwd(X, n2g, sh2, sc2, tb, nlat, tag + "_norm2")
    zf = _mm(h2, w_up, "nn", BF16, tag + "_up", tm=(544, 512), tn=(1408, 512))
    f = _ffn_act_fwd(zf, conv_w, conv_b, L, ch, tag + "_act")
    yf = _mm(f, w_down, "nn", F32, tag + "_down", tm=(544, 512), tn=(512,))
    Xn = _gate_res_fwd(X, yf, g2, tb, nlat, tag + "_res2")
    return Xn, (X, h2, zf, f, yf)


def _conv_ffn_bwd(dXn, saved, mods, n2g, w_up, conv_w, conv_b, w_down, L, tb, nlat, ch, tag):
    X, h2, zf, f, yf = saved
    sc2, g2 = mods[4], mods[5]
    dy, dg2 = _gate_bwd(dXn, yf, g2, tb, nlat, tag + "_dres2")
    df = _mm(dy, w_down, "nt", BF16, tag + "_ddown_x", tm=(544, 512), tn=(1408, 512))
    dw_down = _mm(f, dy, "tn", BF16, tag + "_ddown_w", tm=(1408, 512), tn=(2048,), tk=(544, 512), outer="i")
    dgp, du, dcw, dcb = _ffn_act_bwd(zf, df, conv_w, conv_b, L, ch, tag + "_dact")
    dzf = jnp.concatenate([dgp, du], axis=1)
    dh2 = _mm(dzf, w_up, "nt", BF16, tag + "_dup_x", tm=(1088, 1024, 512), tn=(1024,), tk=(2816, 512))
    dw_up = _mm(h2, dzf, "tn", BF16, tag + "_dup_w", tm=(2048,), tn=(1408, 512), tk=(544, 512))
    dX, dn2g, dsh2, dsc2 = _norm_mod_bwd(X, n2g, sc2, dh2, dXn, tb, nlat, tag + "_dnorm2")
    return dX, dict(n2g=dn2g, sh2=dsh2, sc2=dsc2, g2=dg2, cw=dcw, cb=dcb, w_up=dw_up, w_down=dw_down)


def kernel(x, c, ctx, c_ctx, norm1_g, norm2_g, w_ada, b_ada, ab_w_in, ab_b_in, a_ln_g, a_ln_b, a_w_s, a_b_s, b_conv_w, b_conv_b, b_ln_g, b_ln_b, ab_w_out, mla_w_in, mla_q_norm_g, mla_w_uq, mla_kv_norm_g, mla_w_ukv, mla_w_o, ffn_w_up, ffn_conv_w, ffn_conv_b, ffn_w_down, final_norm_g, loss_target, m_c_ctx, m_norm1_g, m_norm2_g, m_w_ada, m_b_ada, m_ab_w_in, m_ab_b_in, m_a_ln_g, m_a_ln_b, m_a_w_s, m_a_b_s, m_b_conv_w, m_b_conv_b, m_b_ln_g, m_b_ln_b, m_ab_w_out, m_mla_w_in, m_mla_q_norm_g, m_mla_w_uq, m_mla_kv_norm_g, m_mla_w_ukv, m_mla_w_o, m_ffn_w_up, m_ffn_conv_w, m_ffn_conv_b, m_ffn_w_down, m_final_norm_g, v_c_ctx, v_norm1_g, v_norm2_g, v_w_ada, v_b_ada, v_ab_w_in, v_ab_b_in, v_a_ln_g, v_a_ln_b, v_a_w_s, v_a_b_s, v_b_conv_w, v_b_conv_b, v_b_ln_g, v_b_ln_b, v_ab_w_out, v_mla_w_in, v_mla_q_norm_g, v_mla_w_uq, v_mla_kv_norm_g, v_mla_w_ukv, v_mla_w_o, v_ffn_w_up, v_ffn_conv_w, v_ffn_conv_b, v_ffn_w_down, v_final_norm_g):
    W = dict(c_ctx=c_ctx, norm1_g=norm1_g, norm2_g=norm2_g, w_ada=w_ada, b_ada=b_ada, ab_w_in=ab_w_in, ab_b_in=ab_b_in, a_ln_g=a_ln_g,
             a_ln_b=a_ln_b, a_w_s=a_w_s, a_b_s=a_b_s, b_conv_w=b_conv_w, b_conv_b=b_conv_b, b_ln_g=b_ln_g, b_ln_b=b_ln_b,
             ab_w_out=ab_w_out, mla_w_in=mla_w_in, mla_q_norm_g=mla_q_norm_g, mla_w_uq=mla_w_uq, mla_kv_norm_g=mla_kv_norm_g,
             mla_w_ukv=mla_w_ukv, mla_w_o=mla_w_o, ffn_w_up=ffn_w_up, ffn_conv_w=ffn_conv_w, ffn_conv_b=ffn_conv_b,
             ffn_w_down=ffn_w_down, final_norm_g=final_norm_g)
    MOM = dict(c_ctx=m_c_ctx, norm1_g=m_norm1_g, norm2_g=m_norm2_g, w_ada=m_w_ada, b_ada=m_b_ada, ab_w_in=m_ab_w_in, ab_b_in=m_ab_b_in,
               a_ln_g=m_a_ln_g, a_ln_b=m_a_ln_b, a_w_s=m_a_w_s, a_b_s=m_a_b_s, b_conv_w=m_b_conv_w, b_conv_b=m_b_conv_b,
               b_ln_g=m_b_ln_g, b_ln_b=m_b_ln_b, ab_w_out=m_ab_w_out, mla_w_in=m_mla_w_in, mla_q_norm_g=m_mla_q_norm_g,
               mla_w_uq=m_mla_w_uq, mla_kv_norm_g=m_mla_kv_norm_g, mla_w_ukv=m_mla_w_ukv, mla_w_o=m_mla_w_o, ffn_w_up=m_ffn_w_up,
               ffn_conv_w=m_ffn_conv_w, ffn_conv_b=m_ffn_conv_b, ffn_w_down=m_ffn_w_down, final_norm_g=m_final_norm_g)
    VAR = dict(c_ctx=v_c_ctx, norm1_g=v_norm1_g, norm2_g=v_norm2_g, w_ada=v_w_ada, b_ada=v_b_ada, ab_w_in=v_ab_w_in, ab_b_in=v_ab_b_in,
               a_ln_g=v_a_ln_g, a_ln_b=v_a_ln_b, a_w_s=v_a_w_s, a_b_s=v_a_b_s, b_conv_w=v_b_conv_w, b_conv_b=v_b_conv_b,
               b_ln_g=v_b_ln_g, b_ln_b=v_b_ln_b, ab_w_out=v_ab_w_out, mla_w_in=v_mla_w_in, mla_q_norm_g=v_mla_q_norm_g,
               mla_w_uq=v_mla_w_uq, mla_kv_norm_g=v_mla_kv_norm_g, mla_w_ukv=v_mla_w_ukv, mla_w_o=v_mla_w_o, ffn_w_up=v_ffn_w_up,
               ffn_conv_w=v_ffn_conv_w, ffn_conv_b=v_ffn_conv_b, ffn_w_down=v_ffn_w_down, final_norm_g=v_final_norm_g)
    ORDER = list(W.keys())

    L, D = x.shape[1], x.shape[2]
    CT = ctx.shape[1]
    T = L + CT
    AW, BW = a_ln_g.shape[-1], b_ln_g.shape[-1]
    AH = a_w_s.shape[1]
    QL, KVL = 4 * mla_q_norm_g.shape[-1], 4 * mla_kv_norm_g.shape[-1]
    H = 4 * mla_w_o.shape[1] // HEAD
    HS = H // 4
    DFF = ffn_conv_b.shape[-1]
    NA = w_ada.shape[-1]
    tb = 256 if (L % 256 == 0 and CT % 256 == 0) else 128
    nlat = L // tb
    ch = tb
    tq = 256 if L >= 512 else 128
    xi, yi, ci = _me()
    p_me = 2 * xi + yi
    dev = 4 * xi + 2 * yi + ci

    shard_small = [c[0], mla_q_norm_g[0], mla_kv_norm_g[0], b_conv_w[0], ffn_conv_w]
    g0 = _gather_blob(_pack(shard_small), "gather_small")
    c_all, qg_s, kvg_s, bcw_s, fcw_s = _unpack(g0, [a.shape for a in shard_small])
    per_chip = lambda a: a[0::2]
    qg = per_chip(qg_s).reshape(QL)
    kvg = per_chip(kvg_s).reshape(KVL)
    bcw = jnp.concatenate(list(per_chip(bcw_s)), axis=-1)
    fcw = jnp.concatenate(list(per_chip(fcw_s)), axis=-1)
    c16 = jnp.concatenate([c_all, c_ctx[None], jnp.zeros((7, D), F32)], axis=0)

    ms = []
    for i in range(2):
        bias = lax.dynamic_slice(b_ada[i], (p_me * NA,), (NA,))
        ms.append(_mm(c16, w_ada[i], "nn", F32, f"ada{i}", tm=(16,), tn=(512,), bias=bias, silu_a=True))
    ms = jnp.concatenate(ms, axis=0)
    (mods_all,) = _allgather8(
        [(ms, lambda px, py, pc: (pc * 16, 0), (16, NA), (32, 4 * NA), lambda px, py, pc: (pc * 16, (2 * px + py) * NA))], "gather_mods")
    mods_all = mods_all.reshape(2, 16, N_MOD, D)
    mods = []
    for i in range(2):
        lat = lax.dynamic_index_in_dim(mods_all[i], dev, axis=0, keepdims=False)
        both = jnp.stack([lat, mods_all[i, 8]], axis=0)
        mods.append([both[:, k][:, None, :] for k in range(N_MOD)])

    def pad_uq(w):
        w = w.reshape(w.shape[0], HS, HEAD + ROPE)
        return jnp.pad(w, ((0, 0), (0, 0), (0, QHEAD - HEAD - ROPE))).reshape(w.shape[0], HS * QHEAD)

    MI = QL + KVL + LANES
    big = [
        ("ab_w_in", ab_w_in[0], "col"), ("ab_w_out", ab_w_out[0], "row"),
        ("mla_w_in", jnp.pad(mla_w_in[0], ((0, 0), (0, MI - mla_w_in.shape[-1]))), "row"),
        ("mla_w_uq", pad_uq(mla_w_uq[0]), "col"), ("mla_w_ukv", mla_w_ukv[0], "col"), ("mla_w_o", mla_w_o[0], "row"),
        ("ffn_w_up0", ffn_w_up[0], "col"), ("ffn_w_up1", ffn_w_up[1], "col"),
        ("ffn_w_down0", ffn_w_down[0], "row"), ("ffn_w_down1", ffn_w_down[1], "row"),
    ]
    items, wins = [], {}
    for nm, w, kind in big:
        R, C = w.shape[0] // 2, w.shape[1]
        if kind == "col":
            dst_shape = (2 * R, 4 * C)
            at = lambda px, py, pc, R=R, C=C: (pc * R, (2 * px + py) * C)
            wins[nm] = (((R, C), lambda p, pc, R=R, C=C: (pc * R, p * C)), kind)
        else:
            dst_shape = (8 * R, C)
            at = lambda px, py, pc, R=R, C=C: ((4 * px + 2 * py + pc) * R, 0)
            wins[nm] = (((R, C), lambda p, pc, R=R, C=C: ((2 * p + pc) * R, 0)), kind)
        items.append((w.astype(BF16), lambda px, py, pc, R=R: (pc * R, 0), (R, C), dst_shape, at))
    full = dict(zip([b[0] for b in big], _allgather8(items, "gather_weights")))

    X0 = jnp.concatenate([x[0], ctx[0]], axis=0)
    m0, m1 = mods[0], mods[1]
    h1 = _norm_mod_fwd(X0, norm1_g[0], m0[0], m0[1], tb, nlat, "l0_norm1")
    z = _mm(h1, full["ab_w_in"], "nn", BF16, "l0_in", tm=(544, 512), tn=(1024, 512), bias=ab_b_in[0])
    bs_full = jnp.broadcast_to(a_b_s[0][:, :, None], (AH, CHUNK, CHUNK))
    ya = _gmlp_fwd(z, a_ln_g[0], a_ln_b[0], a_w_s[0], bs_full, tb, "l0_gmlp")
    hc = _glu_conv_fwd(z, 2 * AW, bcw, b_conv_b[0], L, ch, "l0_conv")
    yb = _ln_silu_fwd(hc, b_ln_g[0], b_ln_b[0], tb, "l0_lnsilu")
    yab = jnp.concatenate([ya, yb], axis=1)
    y0 = _mm(yab, full["ab_w_out"], "nn", F32, "l0_out", tm=(544, 512), tn=(1024, 512))
    X1 = _gate_res_fwd(X0, y0, m0[2], tb, nlat, "l0_res1")
    X2, ffn0 = _conv_ffn_fwd(X1, m0, norm2_g[0], full["ffn_w_up0"], fcw[0], ffn_conv_b[0], full["ffn_w_down0"], L, tb, nlat, ch, "l0_ffn")

    tabs = _rope_tables(L, T)
    hm = _norm_mod_fwd(X2, norm1_g[1], m1[0], m1[1], tb, nlat, "l1_norm1")
    zm = _mm(hm, full["mla_w_in"], "nn", F32, "l1_in", tm=(544, 512), tn=(MI,))
    cqn, ckvn, kpe = _mla_prep_fwd(zm, qg, kvg, tabs, tb, "l1_prep")
    q = _mm(cqn, full["mla_w_uq"], "nn", BF16, "l1_uq", tm=(512,), tn=(1024, 512), rows=L)
    kvh = _mm(ckvn, full["mla_w_ukv"], "nn", BF16, "l1_ukv", tm=(544, 512), tn=(1024, 512))
    o, lse = _attn_fwd(q, kvh, kpe, tabs, L, tq, "l1_attn")
    yl = _mm(o, full["mla_w_o"], "nn", F32, "l1_o", tm=(512,), tn=(1024, 512))
    m1_lat = [a[:1] for a in m1]
    X2l = X2[:L]
    X3 = _gate_res_fwd(X2l, yl, m1_lat[2], tb, nlat, "l1_res1")
    X4, ffn1 = _conv_ffn_fwd(X3, m1_lat, norm2_g[1], full["ffn_w_up1"], fcw[1], ffn_conv_b[1], full["ffn_w_down1"], L, tb, nlat, ch, "l1_ffn")
    loss_acc, dX4, dfinal = _final_loss(X4, final_norm_g, loss_target[0], tb, "loss")
    loss = lax.psum(loss_acc[0, 0], ("x", "y", "c"))

    dX3, gf1 = _conv_ffn_bwd(dX4, ffn1, m1_lat, norm2_g[1], full["ffn_w_up1"], fcw[1], ffn_conv_b[1], full["ffn_w_down1"], L, tb, nlat, ch, "l1_ffn")
    dyl, dg1_1 = _gate_bwd(dX3, yl, m1_lat[2], tb, nlat, "l1_dres1")
    do = _mm(dyl, full["mla_w_o"], "nt", BF16, "l1_do_x", tm=(512,), tn=(1024, 512))
    dw_o = _mm(o, dyl, "tn", BF16, "l1_do_w", tm=(1024, 512), tn=(2048, 512), tk=(512,), outer="i")
    dq, dkv, dkpe = _attn_bwd(q, kvh, kpe, tabs, o, lse, do, L, tq, "l1_dattn")
    dckvn = _mm(dkv, full["mla_w_ukv"], "nt", BF16, "l1_dukv_x", tm=(544, 512), tn=(KVL,), tk=(2048, 512))
    dw_ukv = _mm(ckvn, dkv, "tn", BF16, "l1_dukv_w", tm=(KVL,), tn=(1024, 512), tk=(544, 512))
    dcqn = _mm(dq, full["mla_w_uq"], "nt", BF16, "l1_duq_x", tm=(512,), tn=(QL,), tk=(2048, 512))
    dw_uq = _mm(cqn, dq, "tn", BF16, "l1_duq_w", tm=(QL,), tn=(1024, 512), tk=(512,), rows=L)
    dcqn = jnp.concatenate([dcqn, jnp.zeros((CT, QL), BF16)], axis=0)
    dzm, dqg, dkvg = _mla_prep_bwd(zm, qg, kvg, tabs, dcqn, dckvn, dkpe, tb, "l1_dprep")
    dhm = _mm(dzm, full["mla_w_in"], "nt", BF16, "l1_din_x", tm=(544, 512), tn=(1024, 512))
    dw_min = _mm(hm, dzm, "tn", BF16, "l1_din_w", tm=(2048,), tn=(MI,), tk=(544, 512))
    dX3p = jnp.concatenate([dX3, jnp.zeros((CT, D), F32)], axis=0)
    dX2, dn1g_1, dsh1_1, dsc1_1 = _norm_mod_bwd(X2, norm1_g[1], m1[1], dhm, dX3p, tb, nlat, "l1_dnorm1")

    dX1, gf0 = _conv_ffn_bwd(dX2, ffn0, m0, norm2_g[0], full["ffn_w_up0"], fcw[0], ffn_conv_b[0], full["ffn_w_down0"], L, tb, nlat, ch, "l0_ffn")
    dy0, dg1_0 = _gate_bwd(dX1, y0, m0[2], tb, nlat, "l0_dres1")
    dyab = _mm(dy0, full["ab_w_out"], "nt", BF16, "l0_dout_x", tm=(544, 512), tn=(1024, 512))
    dw_out = _mm(yab, dy0, "tn", BF16, "l0_dout_w", tm=(1024, 512), tn=(2048, 512), tk=(544, 512), outer="i")
    dzu, dzv, dlnag, dlnab, dws, dbs, csu, csv = _gmlp_bwd(z, dyab, a_ln_g[0], a_ln_b[0], a_w_s[0], bs_full, tb, "l0_dgmlp")
    dhc, dlnbg, dlnbb = _ln_silu_bwd(hc, dyab, AW, b_ln_g[0], b_ln_b[0], tb, "l0_dlnsilu")
    dza, dzg, dbcw, dbcb, csa, csg = _glu_conv_bwd(z, 2 * AW, dhc, bcw, L, ch, "l0_dconv")
    dz = jnp.concatenate([dzu, dzv, dza, dzg], axis=1)
    dbin = jnp.concatenate([csu, csv, csa, csg], axis=1)
    dh1 = _mm(dz, full["ab_w_in"], "nt", BF16, "l0_din_x", tm=(544, 512), tn=(1024, 512), tk=(2048, 512))
    dw_in = _mm(h1, dz, "tn", BF16, "l0_din_w", tm=(2048,), tn=(1024, 512), tk=(544, 512))
    dX0, dn1g_0, dsh1_0, dsc1_0 = _norm_mod_bwd(X0, norm1_g[0], m0[1], dh1, dX1, tb, nlat, "l0_dnorm1")
    grad_x = dX0[:L][None]

    gbig = dict(ab_w_in=dw_in, ab_w_out=dw_out, mla_w_in=dw_min, mla_w_uq=dw_uq, mla_w_ukv=dw_ukv, mla_w_o=dw_o,
                ffn_w_up0=gf0["w_up"], ffn_w_up1=gf1["w_up"], ffn_w_down0=gf0["w_down"], ffn_w_down1=gf1["w_down"])
    names = [b[0] for b in big]
    recv0 = _rs_to_sibling([gbig[nm] for nm in names], [wins[nm][0] for nm in names], "rs_sibling")
    parts = [_rs_add_mine(gbig[nm], recv0[a], (wins[nm][0][0], wins[nm][1]), "rs_add_" + nm) for a, nm in enumerate(names)]
    recv1 = _rs_across_chips(parts, "rs_chips")
    halves = [_sum_lead(r, F32, "rs_sum_" + nm) for r, nm in zip(recv1, names)]
    gshard = dict(zip(names, _rs_share(halves, "rs_share")))

    def grp6(l, sh1, sc1, g1, f):
        G = sh1.shape[0]
        pad = lambda a: jnp.concatenate([a, jnp.zeros((G - a.shape[0],) + a.shape[1:], F32)], axis=0) if a.shape[0] < G else a
        return jnp.concatenate([pad(a) for a in (sh1, sc1, g1, f["sh2"], f["sc2"], f["g2"])], axis=1)

    dm0 = grp6(0, dsh1_0, dsc1_0, dg1_0, gf0)
    dm1 = grp6(1, dsh1_1, dsc1_1, dg1_1, gf1)
    dmods = jnp.stack([dm0, dm1], axis=0)
    small = [
        jnp.concatenate([dn1g_0, dn1g_1], axis=0), jnp.concatenate([gf0["n2g"], gf1["n2g"]], axis=0), dbin, dlnag, dlnab, dws,
        dbs, dbcw, dbcb, dlnbg, dlnbb, dqg, dkvg, jnp.stack([gf0["cw"], gf1["cw"]], axis=0),
        jnp.concatenate([gf0["cb"], gf1["cb"]], axis=0), dfinal, dmods[:, 1],
    ]
    small_shapes = [a.shape for a in small]
    lat_shape = dmods[:, 0].shape
    blob = _pack(small + [dmods[:, 0]])
    gathered = _gather_blob(blob, "gather_grads")
    summed = _sum_lead(gathered.reshape(8, -1, LANES), F32, "sum_grads").reshape(-1)
    (dn1g, dn2g, dbin_s, dlnag_s, dlnab_s, dws_s, dbs_s, dbcw_s, dbcb_s, dlnbg_s, dlnbb_s, dqg_s, dkvg_s, dfcw_s, dfcb_s, dfinal_s,
     dmods_ctx, dmods_lat_sum) = _unpack(summed, small_shapes + [lat_shape])
    n_small = sum(math.prod(s) for s in small_shapes)
    dmods_lat = gathered[:, n_small:n_small + math.prod(lat_shape)].reshape((8,) + lat_shape)

    grad_w_ada, dc_parts = [], []
    for i in range(2):
        dm16 = jnp.concatenate([dmods_lat[:, i].reshape(8, N_MOD * D), dmods_ctx[i].reshape(1, N_MOD * D),
                                jnp.zeros((7, N_MOD * D), F32)], axis=0)
        dm16_s = lax.dynamic_slice(dm16, (0, p_me * NA), (16, NA))
        grad_w_ada.append(_mm(c16, dm16_s, "tn", F32, f"dada{i}_w", tm=(1024, 512), tn=(1024, 512), silu_a=True))
        dc_parts.append(_mm(dm16_s, w_ada[i], "nt", F32, f"dada{i}_c", tm=(16,), tn=(512,), tk=(1024, 512)))
    grad_w_ada = jnp.stack(grad_w_ada, axis=0)
    grad_b_ada = dmods_lat_sum.reshape(2, N_MOD * D) + dmods_ctx.reshape(2, N_MOD * D)
    dc_blob = _pack([dc_parts[0][8] + dc_parts[1][8]])
    dc_all = _gather_blob(dc_blob, "gather_dc")[0::2, :D]
    grad_c_ctx = _cctx_grad(dc_all, c_ctx, "dcctx").reshape(D)

    def my_cols(a, axis, n):
        return lax.dynamic_slice_in_dim(a, p_me * n, n, axis=axis)

    unpad_uq = lambda g: g.reshape(QL, HS, QHEAD)[:, :, :HEAD + ROPE].reshape(QL, HS * (HEAD + ROPE))
    grads = dict(
        c_ctx=grad_c_ctx, norm1_g=dn1g, norm2_g=dn2g, w_ada=grad_w_ada, b_ada=grad_b_ada, ab_w_in=gshard["ab_w_in"][None],
        ab_b_in=dbin_s, a_ln_g=dlnag_s, a_ln_b=dlnab_s, a_w_s=dws_s[None], a_b_s=dbs_s.reshape(1, AH, CHUNK),
        b_conv_w=my_cols(dbcw_s, 1, BW // 4)[None], b_conv_b=dbcb_s, b_ln_g=dlnbg_s, b_ln_b=dlnbb_s,
        ab_w_out=gshard["ab_w_out"][None], mla_w_in=gshard["mla_w_in"][:, :mla_w_in.shape[-1]][None],
        mla_q_norm_g=my_cols(dqg_s, 1, QL // 4), mla_w_uq=unpad_uq(gshard["mla_w_uq"])[None],
        mla_kv_norm_g=my_cols(dkvg_s, 1, KVL // 4), mla_w_ukv=gshard["mla_w_ukv"][None], mla_w_o=gshard["mla_w_o"][None],
        ffn_w_up=jnp.stack([gshard["ffn_w_up0"], gshard["ffn_w_up1"]], axis=0), ffn_conv_w=my_cols(dfcw_s, 2, DFF // 4),
        ffn_conv_b=dfcb_s, ffn_w_down=jnp.stack([gshard["ffn_w_down0"], gshard["ffn_w_down1"]], axis=0),
        final_norm_g=dfinal_s.reshape(D),
    )
    grads = {k: grads[k].reshape(W[k].shape) for k in ORDER}

    BIG = ("w_ada", "ab_w_in", "ab_w_out", "mla_w_in", "mla_w_uq", "mla_w_ukv", "mla_w_o", "ffn_w_up", "ffn_w_down")
    delta, new_m, new_v = {}, {}, {}
    for k in BIG:
        two = lambda a: a.reshape(-1, a.shape[-1])
        d_, m_, v_ = _adamw(two(W[k]), two(grads[k]), two(MOM[k]), two(VAR[k]), "adamw_" + k)
        delta[k], new_m[k], new_v[k] = (a.reshape(W[k].shape) for a in (d_, m_, v_))
    SMALL = [k for k in ORDER if k not in BIG]
    d_, m_, v_ = _adamw(_pack([W[k] for k in SMALL]), _pack([grads[k] for k in SMALL]), _pack([MOM[k] for k in SMALL]),
                        _pack([VAR[k] for k in SMALL]), "adamw_small")
    shapes = [W[k].shape for k in SMALL]
    for k, dd, mm, vv in zip(SMALL, _unpack(d_.reshape(-1), shapes), _unpack(m_.reshape(-1), shapes), _unpack(v_.reshape(-1), shapes)):
        delta[k], new_m[k], new_v[k] = dd, mm, vv

    return (loss, grad_x, *[grads[k] for k in ORDER], *[delta[k] for k in ORDER], *[new_m[k] for k in ORDER],
            *[new_v[k] for k in ORDER])
```

```python
import functools
import math

import jax
import jax.numpy as jnp
from jax import lax
from jax.experimental import pallas as pl
from jax.experimental.pallas import tpu as pltpu

F32 = jnp.float32
BF16 = jnp.bfloat16
MESH = pl.DeviceIdType.MESH
ANY = pl.BlockSpec(memory_space=pl.ANY)

EPS = 1e-6
N_MOD = 6
CHUNK = 128
HEAD = 128
ROPE = 64
QHEAD = 2 * HEAD
GRID_W = 64
ROPE_THETA = 10000.0
B_CONV = 31
FFN_CONV = 3
ADAM_LR, ADAM_B1, ADAM_B2, ADAM_EPS, ADAM_WD, ADAM_STEP = 0.001, 0.9, 0.999, 1e-08, 0.01, 10

V7X_VMEM_LIMIT = 56 * 1024 * 1024
LANES = 128
SUBLANES = 8
CONV_PAD = 16

BS = pl.BlockSpec


def _cp(sem=None, vmem=V7X_VMEM_LIMIT):
    return pltpu.CompilerParams(dimension_semantics=sem, vmem_limit_bytes=vmem)


def _pick(n, prefs):
    for p in prefs:
        if p <= n and n % p == 0:
            return p
    return n


def _row_tile(rows, cols, target_bytes=1 << 20):
    best = None
    for d in range(2 * SUBLANES, rows + 1, 2 * SUBLANES):
        if rows % d == 0 and d * cols * 4 <= target_bytes:
            best = d
    return best if best is not None else rows


def _sigmoid(x):
    return 1.0 / (1.0 + jnp.exp(-x))


def _gelu(x):
    c = math.sqrt(2.0 / math.pi)
    th = jnp.tanh(c * (x + 0.044715 * x * x * x))
    return 0.5 * x * (1.0 + th), th


def _gelu_grad(x, th):
    c = math.sqrt(2.0 / math.pi)
    return 0.5 * (1.0 + th) + 0.5 * x * (1.0 - th * th) * c * (1.0 + 3.0 * 0.044715 * x * x)


_DIMS = {"nn": (((1,), (0,)), ((), ())), "nt": (((1,), (1,)), ((), ())), "tn": (((0,), (0,)), ((), ()))}


def _mm(a, b, mode, out_dtype, name, tm=(512,), tn=(512,), tk=(100000,), bias=None, silu_a=False, rows=None, outer="j"):
    if mode == "nn":
        (M, K), N = a.shape, b.shape[1]
    elif mode == "nt":
        (M, K), N = a.shape, b.shape[0]
    else:
        (K, M), N = a.shape, b.shape[1]
    if rows is not None:
        if mode == "tn":
            K = rows
        else:
            M = rows
    tm, tn, tk = _pick(M, tm), _pick(N, tn), _pick(K, tk)
    gm, gn, gk = M // tm, N // tn, K // tk

    def ij(g0, g1):
        return (g1, g0) if outer == "j" else (g0, g1)

    if mode == "nn":
        a_spec = BS((tm, tk), lambda g0, g1, k: (ij(g0, g1)[0], k))
        b_spec = BS((tk, tn), lambda g0, g1, k: (k, ij(g0, g1)[1]))
    elif mode == "nt":
        a_spec = BS((tm, tk), lambda g0, g1, k: (ij(g0, g1)[0], k))
        b_spec = BS((tn, tk), lambda g0, g1, k: (ij(g0, g1)[1], k))
    else:
        a_spec = BS((tk, tm), lambda g0, g1, k: (k, ij(g0, g1)[0]))
        b_spec = BS((tk, tn), lambda g0, g1, k: (k, ij(g0, g1)[1]))
    in_specs = [a_spec, b_spec]
    operands = [a, b]
    if bias is not None:
        in_specs.append(BS((1, tn), lambda g0, g1, k: (0, ij(g0, g1)[1])))
        operands.append(bias.reshape(1, N))
    o_spec = BS((tm, tn), lambda g0, g1, k: ij(g0, g1))

    def body(*refs):
        a_ref, b_ref = refs[0], refs[1]
        bias_ref = refs[2] if bias is not None else None
        o_ref = refs[3] if bias is not None else refs[2]
        av = a_ref[...]
        if silu_a:
            av = av.astype(F32)
            av = av * _sigmoid(av)
        part = lax.dot_general(av.astype(BF16), b_ref[...].astype(BF16), _DIMS[mode], preferred_element_type=F32)

        def finish(acc):
            if bias_ref is not None:
                acc = acc + bias_ref[...]
            return acc.astype(out_dtype)

        if gk == 1:
            o_ref[...] = finish(part)
        else:
            acc_ref = refs[-1]
            k = pl.program_id(2)

            @pl.when(k == 0)
            def _():
                acc_ref[...] = part

            @pl.when(k > 0)
            def _():
                acc_ref[...] += part

            @pl.when(k == gk - 1)
            def _():
                o_ref[...] = finish(acc_ref[...])

    grid = (gn, gm, gk) if outer == "j" else (gm, gn, gk)
    return pl.pallas_call(
        body, name=name, out_shape=jax.ShapeDtypeStruct((M, N), out_dtype), grid=grid, in_specs=in_specs, out_specs=o_spec,
        scratch_shapes=[pltpu.VMEM((tm, tn), F32)] if gk > 1 else [],
        compiler_params=_cp(("parallel", "parallel", "arbitrary")),
    )(*operands)


def _accum(ref, val, first):
    @pl.when(first)
    def _():
        ref[...] = val

    @pl.when(jnp.logical_not(first))
    def _():
        ref[...] += val


def _norm_mod_fwd(X, gain, sh, sc, tb, nlat, name):
    R, D = X.shape

    def body(x_ref, g_ref, sh_ref, sc_ref, o_ref):
        x = x_ref[...]
        r = lax.rsqrt(jnp.mean(x * x, axis=-1, keepdims=True) + EPS)
        o_ref[...] = ((x * r * g_ref[...]) * (1.0 + sc_ref[0]) + sh_ref[0]).astype(BF16)

    grp = BS((1, 1, D), lambda i: (i // nlat, 0, 0))
    return pl.pallas_call(
        body, name=name, out_shape=jax.ShapeDtypeStruct((R, D), BF16), grid=(R // tb,),
        in_specs=[BS((tb, D), lambda i: (i, 0)), BS((1, D), lambda i: (0, 0)), grp, grp],
        out_specs=BS((tb, D), lambda i: (i, 0)), compiler_params=_cp(("parallel",)),
    )(X, gain.reshape(1, D), sh, sc)


def _norm_mod_bwd(X, gain, sc, dh, dup, tb, nlat, name):
    R, D = X.shape
    G = sc.shape[0]

    def body(x_ref, g_ref, sc_ref, dh_ref, dup_ref, dx_ref, dg_ref, dsh_ref, dsc_ref):
        i = pl.program_id(0)
        x = x_ref[...]
        g = g_ref[...]
        r = lax.rsqrt(jnp.mean(x * x, axis=-1, keepdims=True) + EPS)
        xh = x * r
        dh_ = dh_ref[...].astype(F32)
        t = dh_ * (1.0 + sc_ref[0])
        tg = t * g
        dx_ref[...] = dup_ref[...] + r * (tg - xh * jnp.mean(tg * xh, axis=-1, keepdims=True))
        _accum(dg_ref, jnp.sum(t * xh, axis=0, keepdims=True), i == 0)
        first = i % nlat == 0
        _accum(dsh_ref, jnp.sum(dh_, axis=0, keepdims=True)[None], first)
        _accum(dsc_ref, jnp.sum(dh_ * xh * g, axis=0, keepdims=True)[None], first)

    row = BS((tb, D), lambda i: (i, 0))
    grp = BS((1, 1, D), lambda i: (i // nlat, 0, 0))
    return pl.pallas_call(
        body, name=name,
        out_shape=(jax.ShapeDtypeStruct((R, D), F32), jax.ShapeDtypeStruct((1, D), F32),
                   jax.ShapeDtypeStruct((G, 1, D), F32), jax.ShapeDtypeStruct((G, 1, D), F32)),
        grid=(R // tb,), in_specs=[row, BS((1, D), lambda i: (0, 0)), grp, row, row],
        out_specs=(row, BS((1, D), lambda i: (0, 0)), grp, grp), compiler_params=_cp(("arbitrary",)),
    )(X, gain.reshape(1, D), sc, dh, dup)


def _gate_res_fwd(X, y, gate, tb, nlat, name):
    R, D = X.shape

    def body(x_ref, y_ref, g_ref, o_ref):
        o_ref[...] = x_ref[...] + g_ref[0] * y_ref[...]

    row = BS((tb, D), lambda i: (i, 0))
    return pl.pallas_call(
        body, name=name, out_shape=jax.ShapeDtypeStruct((R, D), F32), grid=(R // tb,),
        in_specs=[row, row, BS((1, 1, D), lambda i: (i // nlat, 0, 0))], out_specs=row, compiler_params=_cp(("parallel",)),
    )(X, y, gate)


def _gate_bwd(dX, y, gate, tb, nlat, name):
    R, D = dX.shape
    G = gate.shape[0]

    def body(dx_ref, y_ref, g_ref, dy_ref, dg_ref):
        i = pl.program_id(0)
        dx = dx_ref[...]
        dy_ref[...] = (g_ref[0] * dx).astype(BF16)
        _accum(dg_ref, jnp.sum(dx * y_ref[...], axis=0, keepdims=True)[None], i % nlat == 0)

    row = BS((tb, D), lambda i: (i, 0))
    grp = BS((1, 1, D), lambda i: (i // nlat, 0, 0))
    return pl.pallas_call(
        body, name=name, out_shape=(jax.ShapeDtypeStruct((R, D), BF16), jax.ShapeDtypeStruct((G, 1, D), F32)),
        grid=(R // tb,), in_specs=[row, row, grp], out_specs=(row, grp), compiler_params=_cp(("arbitrary",)),
    )(dX, y, gate)


def _final_loss(X, gain, target, tb, name):
    R, D = X.shape

    def body(x_ref, g_ref, t_ref, loss_ref, dx_ref, dg_ref):
        i = pl.program_id(0)
        x = x_ref[...]
        g = g_ref[...]
        r = lax.rsqrt(jnp.mean(x * x, axis=-1, keepdims=True) + EPS)
        xh = x * r
        e = xh * g - t_ref[...]
        part = jnp.sum(jnp.sum(e * e, axis=1, keepdims=True), axis=0, keepdims=True) * (0.5 / D)
        _accum(loss_ref, jnp.broadcast_to(part, (1, LANES)), i == 0)
        dy = e * (1.0 / D)
        _accum(dg_ref, jnp.sum(dy * xh, axis=0, keepdims=True), i == 0)
        tg = dy * g
        dx_ref[...] = r * (tg - xh * jnp.mean(tg * xh, axis=-1, keepdims=True))

    row = BS((tb, D), lambda i: (i, 0))
    return pl.pallas_call(
        body, name=name,
        out_shape=(jax.ShapeDtypeStruct((1, LANES), F32), jax.ShapeDtypeStruct((R, D), F32), jax.ShapeDtypeStruct((1, D), F32)),
        grid=(R // tb,), in_specs=[row, BS((1, D), lambda i: (0, 0)), row],
        out_specs=(BS((1, LANES), lambda i: (0, 0)), row, BS((1, D), lambda i: (0, 0))), compiler_params=_cp(("arbitrary",)),
    )(X, gain.reshape(1, D), target)


def _ln_stats(v):
    mu = jnp.mean(v, axis=-1, keepdims=True)
    d = v - mu
    r = lax.rsqrt(jnp.mean(d * d, axis=-1, keepdims=True) + EPS)
    return d * r, r


def _gmlp_fwd(z, ln_g, ln_b, w_s, b_s_full, tb, name):
    R = z.shape[0]
    AW = ln_g.shape[-1]
    AH = w_s.shape[0]

    def body(zu_ref, zv_ref, g_ref, b_ref, ws_ref, bs_ref, o_ref):
        u, _ = _gelu(zu_ref[...].astype(F32))
        v, _ = _gelu(zv_ref[...].astype(F32))
        xh, _ = _ln_stats(v)
        vn = (xh * g_ref[...] + b_ref[...]).astype(BF16)
        for n in range(tb // CHUNK):
            rs = slice(n * CHUNK, (n + 1) * CHUNK)
            for h in range(AH):
                cs = slice(h * CHUNK, (h + 1) * CHUNK)
                v2 = jnp.dot(ws_ref[h].astype(BF16), vn[rs, cs], preferred_element_type=F32) + bs_ref[h]
                o_ref[rs, cs] = (u[rs, cs] * v2).astype(BF16)

    full3 = lambda s: BS(s, lambda i: (0, 0, 0))
    return pl.pallas_call(
        body, name=name, out_shape=jax.ShapeDtypeStruct((R, AW), BF16), grid=(R // tb,),
        in_specs=[BS((tb, AW), lambda i: (i, 0)), BS((tb, AW), lambda i: (i, 1)), BS((1, AW), lambda i: (0, 0)),
                  BS((1, AW), lambda i: (0, 0)), full3((AH, CHUNK, CHUNK)), full3((AH, CHUNK, CHUNK))],
        out_specs=BS((tb, AW), lambda i: (i, 0)), compiler_params=_cp(("parallel",)),
    )(z, z, ln_g.reshape(1, AW), ln_b.reshape(1, AW), w_s, b_s_full)


def _gmlp_bwd(z, dya, ln_g, ln_b, w_s, b_s_full, tb, name):
    R = z.shape[0]
    AW = ln_g.shape[-1]
    AH = w_s.shape[0]

    def body(zu_ref, zv_ref, dy_ref, g_ref, b_ref, ws_ref, bs_ref, dzu_ref, dzv_ref, dg_ref, db_ref, dws_ref, dbs_ref, cs_u_ref, cs_v_ref,
             dvn_scr):
        i = pl.program_id(0)
        first = i == 0
        zu = zu_ref[...].astype(F32)
        zv = zv_ref[...].astype(F32)
        u, thu = _gelu(zu)
        v, thv = _gelu(zv)
        xh, r = _ln_stats(v)
        g = g_ref[...]
        vn = (xh * g + b_ref[...]).astype(BF16)
        dy = dy_ref[...].astype(F32)

        @pl.when(first)
        def _():
            dws_ref[...] = jnp.zeros_like(dws_ref)
            dbs_ref[...] = jnp.zeros_like(dbs_ref)

        for n in range(tb // CHUNK):
            rs = slice(n * CHUNK, (n + 1) * CHUNK)
            for h in range(AH):
                cs = slice(h * CHUNK, (h + 1) * CHUNK)
                w = ws_ref[h].astype(BF16)
                v2 = jnp.dot(w, vn[rs, cs], preferred_element_type=F32) + bs_ref[h]
                dzu_ref[rs, cs] = (dy[rs, cs] * v2 * _gelu_grad(zu[rs, cs], thu[rs, cs])).astype(BF16)
                dv2 = dy[rs, cs] * u[rs, cs]
                dv2b = dv2.astype(BF16)
                dvn_scr[rs, cs] = lax.dot_general(w, dv2b, _DIMS["tn"], preferred_element_type=F32)
                dws_ref[h] += lax.dot_general(dv2b, vn[rs, cs], _DIMS["nt"], preferred_element_type=F32)
                dbs_ref[h] += jnp.sum(dv2, axis=1, keepdims=True)
        dvn = dvn_scr[...]
        _accum(dg_ref, jnp.sum(dvn * xh, axis=0, keepdims=True), first)
        _accum(db_ref, jnp.sum(dvn, axis=0, keepdims=True), first)
        t = dvn * g
        dv = r * (t - jnp.mean(t, axis=-1, keepdims=True) - xh * jnp.mean(t * xh, axis=-1, keepdims=True))
        dzv = dv * _gelu_grad(zv, thv)
        dzv_ref[...] = dzv.astype(BF16)
        _accum(cs_v_ref, jnp.sum(dzv, axis=0, keepdims=True), first)
        _accum(cs_u_ref, jnp.sum(dzu_ref[...].astype(F32), axis=0, keepdims=True), first)

    full3 = lambda s: BS(s, lambda i: (0, 0, 0))
    vec = BS((1, AW), lambda i: (0, 0))
    row = BS((tb, AW), lambda i: (i, 0))
    outs = pl.pallas_call(
        body, name=name,
        out_shape=(jax.ShapeDtypeStruct((R, AW), BF16), jax.ShapeDtypeStruct((R, AW), BF16), jax.ShapeDtypeStruct((1, AW), F32),
                   jax.ShapeDtypeStruct((1, AW), F32), jax.ShapeDtypeStruct((AH, CHUNK, CHUNK), F32),
                   jax.ShapeDtypeStruct((AH, CHUNK, 1), F32), jax.ShapeDtypeStruct((1, AW), F32), jax.ShapeDtypeStruct((1, AW), F32)),
        grid=(R // tb,),
        in_specs=[row, BS((tb, AW), lambda i: (i, 1)), row, vec, vec, full3((AH, CHUNK, CHUNK)), full3((AH, CHUNK, CHUNK))],
        out_specs=(row, row, vec, vec, full3((AH, CHUNK, CHUNK)), full3((AH, CHUNK, 1)), vec, vec),
        scratch_shapes=[pltpu.VMEM((tb, AW), F32)], compiler_params=_cp(("arbitrary",)),
    )(z, z, dya, ln_g.reshape(1, AW), ln_b.reshape(1, AW), w_s, b_s_full)
    return outs


def _segments(R, L):
    return [(0, L)] + ([(L, R - L)] if R > L else [])


def _scr_rows(R, L):
    return R + CONV_PAD * (len(_segments(R, L)) + 1)


def _scr_off(s, start):
    return CONV_PAD * (s + 1) + start


def _zero_pads(scr, R, L):
    segs = _segments(R, L)
    z = jnp.zeros((CONV_PAD, scr.shape[1]), F32)
    for s, (start, n) in enumerate(segs):
        scr[pl.ds(_scr_off(s, start) - CONV_PAD, CONV_PAD), :] = z
    last_s, (last_start, last_n) = len(segs) - 1, segs[-1]
    scr[pl.ds(_scr_off(last_s, last_start) + last_n, CONV_PAD), :] = z


def _for_chunks(R, L, ch, fn):
    for s, (start, n) in enumerate(_segments(R, L)):
        c = min(ch, n)
        off = _scr_off(s, start)

        def step(i, carry, start=start, off=off, c=c):
            r0 = pl.multiple_of(start + i * c, SUBLANES)
            fn(r0, pl.multiple_of(off + i * c, SUBLANES), c)
            return carry

        lax.fori_loop(0, n // c, step, 0)


def _taps(scr, srow, c, w_ref, ntap, flip):
    win = scr[pl.ds(srow - CONV_PAD, c + 2 * CONV_PAD), :]
    n = c + 2 * CONV_PAD
    acc = None
    for k in range(ntap):
        o = k - (ntap - 1) // 2
        if flip:
            o = -o
        sh = win if o == 0 else pltpu.roll(win, (-o) % n, 0)
        term = w_ref[k:k + 1, :] * sh[CONV_PAD:CONV_PAD + c]
        acc = term if acc is None else acc + term
    return acc


def _tap_grads(scr, srow, c, dy, dw_ref, ntap):
    win = scr[pl.ds(srow - CONV_PAD, c + 2 * CONV_PAD), :]
    n = c + 2 * CONV_PAD
    for k in range(ntap):
        o = k - (ntap - 1) // 2
        sh = win if o == 0 else pltpu.roll(win, (-o) % n, 0)
        dw_ref[k:k + 1, :] += jnp.sum(dy * sh[CONV_PAD:CONV_PAD + c], axis=0, keepdims=True)


def _glu_conv_fwd(z, col0, conv_w, conv_b, L, ch, name):
    R = z.shape[0]
    BW = conv_w.shape[1]
    nb, c0 = BW // LANES, col0 // LANES

    def body(a_ref, g_ref, w_ref, b_ref, o_ref, scr):
        _zero_pads(scr, R, L)

        def fill(r0, s0, c):
            a = a_ref[pl.ds(r0, c), :].astype(F32)
            g = g_ref[pl.ds(r0, c), :].astype(F32)
            scr[pl.ds(s0, c), :] = a * _sigmoid(g)

        _for_chunks(R, L, ch, fill)

        def conv(r0, s0, c):
            o_ref[pl.ds(r0, c), :] = _taps(scr, s0, c, w_ref, B_CONV, False) + b_ref[...]

        _for_chunks(R, L, ch, conv)

    return pl.pallas_call(
        body, name=name, out_shape=jax.ShapeDtypeStruct((R, BW), F32), grid=(nb,),
        in_specs=[BS((R, LANES), lambda j: (0, c0 + j)), BS((R, LANES), lambda j: (0, c0 + nb + j)),
                  BS((B_CONV, LANES), lambda j: (0, j)), BS((1, LANES), lambda j: (0, j))],
        out_specs=BS((R, LANES), lambda j: (0, j)), scratch_shapes=[pltpu.VMEM((_scr_rows(R, L), LANES), F32)],
        compiler_params=_cp(("parallel",)),
    )(z, z, conv_w, conv_b.reshape(1, BW))


def _glu_conv_bwd(z, col0, dhc, conv_w, L, ch, name):
    R = z.shape[0]
    BW = conv_w.shape[1]
    nb, c0 = BW // LANES, col0 // LANES

    def body(a_ref, g_ref, dy_ref, w_ref, da_ref, dg_ref, dw_ref, db_ref, csa_ref, csg_ref, scr_h, scr_dy):
        _zero_pads(scr_h, R, L)
        _zero_pads(scr_dy, R, L)
        dw_ref[...] = jnp.zeros_like(dw_ref)
        db_ref[...] = jnp.zeros_like(db_ref)
        csa_ref[...] = jnp.zeros_like(csa_ref)
        csg_ref[...] = jnp.zeros_like(csg_ref)

        def fill(r0, s0, c):
            a = a_ref[pl.ds(r0, c), :].astype(F32)
            g = g_ref[pl.ds(r0, c), :].astype(F32)
            scr_h[pl.ds(s0, c), :] = a * _sigmoid(g)
            scr_dy[pl.ds(s0, c), :] = dy_ref[pl.ds(r0, c), :]

        _for_chunks(R, L, ch, fill)

        def back(r0, s0, c):
            dh = _taps(scr_dy, s0, c, w_ref, B_CONV, True)
            a = a_ref[pl.ds(r0, c), :].astype(F32)
            sg = _sigmoid(g_ref[pl.ds(r0, c), :].astype(F32))
            da = dh * sg
            dg = dh * a * sg * (1.0 - sg)
            da_ref[pl.ds(r0, c), :] = da.astype(BF16)
            dg_ref[pl.ds(r0, c), :] = dg.astype(BF16)
            csa_ref[...] += jnp.sum(da, axis=0, keepdims=True)
            csg_ref[...] += jnp.sum(dg, axis=0, keepdims=True)
            dy = dy_ref[pl.ds(r0, c), :]
            db_ref[...] += jnp.sum(dy, axis=0, keepdims=True)
            _tap_grads(scr_h, s0, c, dy, dw_ref, B_CONV)

        _for_chunks(R, L, ch, back)

    col = BS((R, LANES), lambda j: (0, j))
    vec = BS((1, LANES), lambda j: (0, j))
    nrow = _scr_rows(R, L)
    return pl.pallas_call(
        body, name=name,
        out_shape=(jax.ShapeDtypeStruct((R, BW), BF16), jax.ShapeDtypeStruct((R, BW), BF16), jax.ShapeDtypeStruct((B_CONV, BW), F32),
                   jax.ShapeDtypeStruct((1, BW), F32), jax.ShapeDtypeStruct((1, BW), F32), jax.ShapeDtypeStruct((1, BW), F32)),
        grid=(nb,),
        in_specs=[BS((R, LANES), lambda j: (0, c0 + j)), BS((R, LANES), lambda j: (0, c0 + nb + j)), col,
                  BS((B_CONV, LANES), lambda j: (0, j))],
        out_specs=(col, col, BS((B_CONV, LANES), lambda j: (0, j)), vec, vec, vec),
        scratch_shapes=[pltpu.VMEM((nrow, LANES), F32), pltpu.VMEM((nrow, LANES), F32)], compiler_params=_cp(("parallel",)),
    )(z, z, dhc, conv_w)


def _ln_silu_fwd(hc, ln_g, ln_b, tb, name):
    R, W = hc.shape

    def body(x_ref, g_ref, b_ref, o_ref):
        xh, _ = _ln_stats(x_ref[...])
        y = xh * g_ref[...] + b_ref[...]
        o_ref[...] = (y * _sigmoid(y)).astype(BF16)

    vec = BS((1, W), lambda i: (0, 0))
    row = BS((tb, W), lambda i: (i, 0))
    return pl.pallas_call(
        body, name=name, out_shape=jax.ShapeDtypeStruct((R, W), BF16), grid=(R // tb,), in_specs=[row, vec, vec], out_specs=row,
        compiler_params=_cp(("parallel",)),
    )(hc, ln_g.reshape(1, W), ln_b.reshape(1, W))


def _ln_silu_bwd(hc, dyb, col0, ln_g, ln_b, tb, name):
    R, W = hc.shape
    c0 = col0 // W

    def body(x_ref, dy_ref, g_ref, b_ref, dx_ref, dg_ref, db_ref):
        i = pl.program_id(0)
        xh, r = _ln_stats(x_ref[...])
        g = g_ref[...]
        y = xh * g + b_ref[...]
        s = _sigmoid(y)
        dy = dy_ref[...].astype(F32) * s * (1.0 + y * (1.0 - s))
        _accum(dg_ref, jnp.sum(dy * xh, axis=0, keepdims=True), i == 0)
        _accum(db_ref, jnp.sum(dy, axis=0, keepdims=True), i == 0)
        t = dy * g
        dx_ref[...] = r * (t - jnp.mean(t, axis=-1, keepdims=True) - xh * jnp.mean(t * xh, axis=-1, keepdims=True))

    vec = BS((1, W), lambda i: (0, 0))
    row = BS((tb, W), lambda i: (i, 0))
    return pl.pallas_call(
        body, name=name,
        out_shape=(jax.ShapeDtypeStruct((R, W), F32), jax.ShapeDtypeStruct((1, W), F32), jax.ShapeDtypeStruct((1, W), F32)),
        grid=(R // tb,), in_specs=[row, BS((tb, W), lambda i: (i, c0)), vec, vec], out_specs=(row, vec, vec),
        compiler_params=_cp(("arbitrary",)),
    )(hc, dyb, ln_g.reshape(1, W), ln_b.reshape(1, W))


def _ffn_act_fwd(zf, conv_w, conv_b, L, ch, name):
    R = zf.shape[0]
    DFF = conv_w.shape[1]
    nb = DFF // LANES

    def body(g_ref, u_ref, w_ref, b_ref, o_ref, scr):
        _zero_pads(scr, R, L)

        def fill(r0, s0, c):
            scr[pl.ds(s0, c), :] = g_ref[pl.ds(r0, c), :].astype(F32)

        _for_chunks(R, L, ch, fill)

        def act(r0, s0, c):
            gc = _taps(scr, s0, c, w_ref, FFN_CONV, False) + b_ref[...]
            o_ref[pl.ds(r0, c), :] = (gc * _sigmoid(gc) * u_ref[pl.ds(r0, c), :].astype(F32)).astype(BF16)

        _for_chunks(R, L, ch, act)

    return pl.pallas_call(
        body, name=name, out_shape=jax.ShapeDtypeStruct((R, DFF), BF16), grid=(nb,),
        in_specs=[BS((R, LANES), lambda j: (0, j)), BS((R, LANES), lambda j: (0, nb + j)), BS((FFN_CONV, LANES), lambda j: (0, j)),
                  BS((1, LANES), lambda j: (0, j))],
        out_specs=BS((R, LANES), lambda j: (0, j)), scratch_shapes=[pltpu.VMEM((_scr_rows(R, L), LANES), F32)],
        compiler_params=_cp(("parallel",)),
    )(zf, zf, conv_w, conv_b.reshape(1, DFF))


def _ffn_act_bwd(zf, df, conv_w, conv_b, L, ch, name):
    R = zf.shape[0]
    DFF = conv_w.shape[1]
    nb = DFF // LANES

    def body(g_ref, u_ref, df_ref, w_ref, b_ref, dg_ref, du_ref, dw_ref, db_ref, scr_g, scr_d):
        _zero_pads(scr_g, R, L)
        _zero_pads(scr_d, R, L)
        dw_ref[...] = jnp.zeros_like(dw_ref)
        db_ref[...] = jnp.zeros_like(db_ref)

        def fill(r0, s0, c):
            scr_g[pl.ds(s0, c), :] = g_ref[pl.ds(r0, c), :].astype(F32)

        _for_chunks(R, L, ch, fill)

        def pre(r0, s0, c):
            gc = _taps(scr_g, s0, c, w_ref, FFN_CONV, False) + b_ref[...]
            s = _sigmoid(gc)
            d = df_ref[pl.ds(r0, c), :].astype(F32)
            du_ref[pl.ds(r0, c), :] = (d * gc * s).astype(BF16)
            dgc = d * u_ref[pl.ds(r0, c), :].astype(F32) * s * (1.0 + gc * (1.0 - s))
            scr_d[pl.ds(s0, c), :] = dgc
            db_ref[...] += jnp.sum(dgc, axis=0, keepdims=True)
            _tap_grads(scr_g, s0, c, dgc, dw_ref, FFN_CONV)

        _for_chunks(R, L, ch, pre)

        def back(r0, s0, c):
            dg_ref[pl.ds(r0, c), :] = _taps(scr_d, s0, c, w_ref, FFN_CONV, True).astype(BF16)

        _for_chunks(R, L, ch, back)

    col = BS((R, LANES), lambda j: (0, j))
    vec = BS((1, LANES), lambda j: (0, j))
    nrow = _scr_rows(R, L)
    return pl.pallas_call(
        body, name=name,
        out_shape=(jax.ShapeDtypeStruct((R, DFF), BF16), jax.ShapeDtypeStruct((R, DFF), BF16), jax.ShapeDtypeStruct((FFN_CONV, DFF), F32),
                   jax.ShapeDtypeStruct((1, DFF), F32)),
        grid=(nb,),
        in_specs=[col, BS((R, LANES), lambda j: (0, nb + j)), col, BS((FFN_CONV, LANES), lambda j: (0, j)), vec],
        out_specs=(col, col, BS((FFN_CONV, LANES), lambda j: (0, j)), vec),
        scratch_shapes=[pltpu.VMEM((nrow, LANES), F32), pltpu.VMEM((nrow, LANES), F32)], compiler_params=_cp(("parallel",)),
    )(zf, zf, df, conv_w, conv_b.reshape(1, DFF))


def _rope_tables(L, T):
    rows = L // GRID_W
    row = jnp.repeat(jnp.arange(rows, dtype=F32), GRID_W)
    col = jnp.tile(jnp.arange(GRID_W, dtype=F32), rows)
    n_freq = ROPE // 4
    inv = ROPE_THETA ** (-jnp.arange(n_freq, dtype=F32) / n_freq)
    ang = jnp.concatenate([row[:, None] * inv, col[:, None] * inv], axis=-1)
    cos, sin = jnp.cos(ang), jnp.sin(ang)
    half = ROPE // 2
    zero = jnp.zeros((L, half), F32)
    cos_t = jnp.concatenate([cos, cos, jnp.ones((L, LANES - ROPE), F32)], axis=1)
    sa = jnp.concatenate([zero, sin, zero, zero], axis=1)
    sb = jnp.concatenate([-sin, zero, zero, zero], axis=1)
    pad = T - L
    cos_t = jnp.concatenate([cos_t, jnp.ones((pad, LANES), F32)], axis=0)
    sa = jnp.concatenate([sa, jnp.zeros((pad, LANES), F32)], axis=0)
    sb = jnp.concatenate([sb, jnp.zeros((pad, LANES), F32)], axis=0)
    return cos_t, sa, sb


def _rope(x, cos, sa, sb):
    half = ROPE // 2
    return x * cos + pltpu.roll(x, half, 1) * sa + pltpu.roll(x, LANES - half, 1) * sb


def _rope_t(d, cos, sa, sb):
    half = ROPE // 2
    return d * cos + pltpu.roll(d * sa, LANES - half, 1) + pltpu.roll(d * sb, half, 1)


def _mla_prep_fwd(zm, qg, kvg, tabs, tb, name):
    T, W = zm.shape
    QL, KVL = qg.shape[-1], kvg.shape[-1]

    def body(z_ref, qg_ref, kg_ref, cos_ref, sa_ref, sb_ref, q_ref, k_ref, p_ref):
        cq = z_ref[:, :QL]
        r = lax.rsqrt(jnp.mean(cq * cq, axis=-1, keepdims=True) + EPS)
        q_ref[...] = (cq * r * qg_ref[...]).astype(BF16)
        ck = z_ref[:, QL:QL + KVL]
        r = lax.rsqrt(jnp.mean(ck * ck, axis=-1, keepdims=True) + EPS)
        k_ref[...] = (ck * r * kg_ref[...]).astype(BF16)
        p_ref[...] = _rope(z_ref[:, QL + KVL:], cos_ref[...], sa_ref[...], sb_ref[...]).astype(BF16)

    tab = BS((tb, LANES), lambda i: (i, 0))
    return pl.pallas_call(
        body, name=name,
        out_shape=(jax.ShapeDtypeStruct((T, QL), BF16), jax.ShapeDtypeStruct((T, KVL), BF16), jax.ShapeDtypeStruct((T, LANES), BF16)),
        grid=(T // tb,),
        in_specs=[BS((tb, W), lambda i: (i, 0)), BS((1, QL), lambda i: (0, 0)), BS((1, KVL), lambda i: (0, 0)), tab, tab, tab],
        out_specs=(BS((tb, QL), lambda i: (i, 0)), BS((tb, KVL), lambda i: (i, 0)), tab), compiler_params=_cp(("parallel",)),
    )(zm, qg.reshape(1, QL), kvg.reshape(1, KVL), *tabs)


def _mla_prep_bwd(zm, qg, kvg, tabs, dq, dk, dp, tb, name):
    T, W = zm.shape
    QL, KVL = qg.shape[-1], kvg.shape[-1]

    def body(z_ref, qg_ref, kg_ref, cos_ref, sa_ref, sb_ref, dq_ref, dk_ref, dp_ref, dz_ref, dqg_ref, dkg_ref):
        i = pl.program_id(0)

        def rms_bwd(x, g, dy):
            r = lax.rsqrt(jnp.mean(x * x, axis=-1, keepdims=True) + EPS)
            xh = x * r
            t = dy * g
            return r * (t - xh * jnp.mean(t * xh, axis=-1, keepdims=True)), jnp.sum(dy * xh, axis=0, keepdims=True)

        dcq, dg = rms_bwd(z_ref[:, :QL], qg_ref[...], dq_ref[...].astype(F32))
        dz_ref[:, :QL] = dcq.astype(BF16)
        _accum(dqg_ref, dg, i == 0)
        dck, dg = rms_bwd(z_ref[:, QL:QL + KVL], kg_ref[...], dk_ref[...].astype(F32))
        dz_ref[:, QL:QL + KVL] = dck.astype(BF16)
        _accum(dkg_ref, dg, i == 0)
        dz_ref[:, QL + KVL:] = _rope_t(dp_ref[...], cos_ref[...], sa_ref[...], sb_ref[...]).astype(BF16)

    tab = BS((tb, LANES), lambda i: (i, 0))
    return pl.pallas_call(
        body, name=name,
        out_shape=(jax.ShapeDtypeStruct((T, W), BF16), jax.ShapeDtypeStruct((1, QL), F32), jax.ShapeDtypeStruct((1, KVL), F32)),
        grid=(T // tb,),
        in_specs=[BS((tb, W), lambda i: (i, 0)), BS((1, QL), lambda i: (0, 0)), BS((1, KVL), lambda i: (0, 0)), tab, tab, tab,
                  BS((tb, QL), lambda i: (i, 0)), BS((tb, KVL), lambda i: (i, 0)), tab],
        out_specs=(BS((tb, W), lambda i: (i, 0)), BS((1, QL), lambda i: (0, 0)), BS((1, KVL), lambda i: (0, 0))),
        compiler_params=_cp(("arbitrary",)),
    )(zm, qg.reshape(1, QL), kvg.reshape(1, KVL), *tabs, dq, dk, dp)


def _attn_fwd(q, kv, kpe, tabs, L, tq, name):
    T = kv.shape[0]
    H = kv.shape[1] // QHEAD
    scale = (HEAD + ROPE) ** -0.5

    def body(q_ref, kv_ref, kpe_ref, cos_ref, sa_ref, sb_ref, o_ref, lse_ref, kcat):
        @pl.when(pl.program_id(1) == 0)
        def _():
            kcat[:, :HEAD] = kv_ref[:, :HEAD]
            kcat[:, HEAD:] = kpe_ref[...]

        qp = _rope(q_ref[:, HEAD:].astype(F32), cos_ref[...], sa_ref[...], sb_ref[...]).astype(BF16)
        qc = jnp.concatenate([q_ref[:, :HEAD], qp], axis=1)
        s = lax.dot_general(qc, kcat[...], _DIMS["nt"], preferred_element_type=F32) * scale
        m = jnp.max(s, axis=-1, keepdims=True)
        p = jnp.exp(s - m)
        l = jnp.sum(p, axis=-1, keepdims=True)
        o = jnp.dot(p.astype(BF16), kv_ref[:, HEAD:], preferred_element_type=F32)
        o_ref[...] = (o / l).astype(BF16)
        lse_ref[0] = m + jnp.log(l)

    tab = BS((tq, LANES), lambda h, i: (i, 0))
    return pl.pallas_call(
        body, name=name, out_shape=(jax.ShapeDtypeStruct((L, H * HEAD), BF16), jax.ShapeDtypeStruct((H, L, 1), F32)),
        grid=(H, L // tq),
        in_specs=[BS((tq, QHEAD), lambda h, i: (i, h)), BS((T, QHEAD), lambda h, i: (0, h)), BS((T, LANES), lambda h, i: (0, 0)),
                  tab, tab, tab],
        out_specs=(BS((tq, HEAD), lambda h, i: (i, h)), BS((1, tq, 1), lambda h, i: (h, i, 0))),
        scratch_shapes=[pltpu.VMEM((T, QHEAD), BF16)], compiler_params=_cp(("parallel", "arbitrary")),
    )(q, kv, kpe, *tabs)


def _attn_bwd(q, kv, kpe, tabs, o, lse, do, L, tq, name):
    T = kv.shape[0]
    H = kv.shape[1] // QHEAD
    scale = (HEAD + ROPE) ** -0.5
    nq = L // tq

    def body(q_ref, kv_ref, kpe_ref, cos_ref, sa_ref, sb_ref, o_ref, lse_ref, do_ref, dq_ref, dkv_ref, dkpe_ref, kcat, dk_acc, dv_acc):
        h, i = pl.program_id(0), pl.program_id(1)

        @pl.when(i == 0)
        def _():
            kcat[:, :HEAD] = kv_ref[:, :HEAD]
            kcat[:, HEAD:] = kpe_ref[...]
            dk_acc[...] = jnp.zeros_like(dk_acc)
            dv_acc[...] = jnp.zeros_like(dv_acc)

        cos, sa, sb = cos_ref[...], sa_ref[...], sb_ref[...]
        qp = _rope(q_ref[:, HEAD:].astype(F32), cos, sa, sb).astype(BF16)
        qc = jnp.concatenate([q_ref[:, :HEAD], qp], axis=1)
        s = lax.dot_general(qc, kcat[...], _DIMS["nt"], preferred_element_type=F32) * scale
        p = jnp.exp(s - lse_ref[0])
        dov = do_ref[...]
        delta = jnp.sum(dov.astype(F32) * o_ref[...].astype(F32), axis=-1, keepdims=True)
        dp = lax.dot_general(dov, kv_ref[:, HEAD:], _DIMS["nt"], preferred_element_type=F32)
        ds = (p * (dp - delta) * scale).astype(BF16)
        dqc = jnp.dot(ds, kcat[...], preferred_element_type=F32)
        dq_ref[:, :HEAD] = dqc[:, :HEAD].astype(BF16)
        dq_ref[:, HEAD:] = _rope_t(dqc[:, HEAD:], cos, sa, sb).astype(BF16)
        dk_acc[...] += lax.dot_general(ds, qc, _DIMS["tn"], preferred_element_type=F32)
        dv_acc[...] += lax.dot_general(p.astype(BF16), dov, _DIMS["tn"], preferred_element_type=F32)

        @pl.when(i == nq - 1)
        def _():
            dkv_ref[:, :HEAD] = dk_acc[:, :HEAD].astype(BF16)
            dkv_ref[:, HEAD:] = dv_acc[...].astype(BF16)

            @pl.when(h == 0)
            def _():
                dkpe_ref[...] = dk_acc[:, HEAD:]

            @pl.when(h > 0)
            def _():
                dkpe_ref[...] += dk_acc[:, HEAD:]

    tab = BS((tq, LANES), lambda h, i: (i, 0))
    return pl.pallas_call(
        body, name=name,
        out_shape=(jax.ShapeDtypeStruct((L, H * QHEAD), BF16), jax.ShapeDtypeStruct((T, H * QHEAD), BF16), jax.ShapeDtypeStruct((T, LANES), F32)),
        grid=(H, nq),
        in_specs=[BS((tq, QHEAD), lambda h, i: (i, h)), BS((T, QHEAD), lambda h, i: (0, h)), BS((T, LANES), lambda h, i: (0, 0)),
                  tab, tab, tab, BS((tq, HEAD), lambda h, i: (i, h)), BS((1, tq, 1), lambda h, i: (h, i, 0)),
                  BS((tq, HEAD), lambda h, i: (i, h))],
        out_specs=(BS((tq, QHEAD), lambda h, i: (i, h)), BS((T, QHEAD), lambda h, i: (0, h)), BS((T, LANES), lambda h, i: (0, 0))),
        scratch_shapes=[pltpu.VMEM((T, QHEAD), BF16), pltpu.VMEM((T, QHEAD), F32), pltpu.VMEM((T, HEAD), F32)],
        compiler_params=_cp(("arbitrary", "arbitrary")),
    )(q, kv, kpe, *tabs, o, lse, do)


def _adamw(w, g, m, v, name):
    R, C = w.shape
    tr = _row_tile(R, C)
    c1 = 1.0 / (1.0 - ADAM_B1 ** ADAM_STEP)
    c2 = 1.0 / (1.0 - ADAM_B2 ** ADAM_STEP)

    def body(w_ref, g_ref, m_ref, v_ref, d_ref, nm_ref, nv_ref):
        g_ = g_ref[...]
        nm = ADAM_B1 * m_ref[...] + (1.0 - ADAM_B1) * g_
        nv = ADAM_B2 * v_ref[...] + (1.0 - ADAM_B2) * (g_ * g_)
        nm_ref[...] = nm
        nv_ref[...] = nv
        d_ref[...] = -ADAM_LR * ((nm * c1) / (jnp.sqrt(nv * c2) + ADAM_EPS) + ADAM_WD * w_ref[...])

    blk = BS((tr, C), lambda i: (i, 0))
    sd = jax.ShapeDtypeStruct((R, C), F32)
    return pl.pallas_call(body, name=name, out_shape=(sd, sd, sd), grid=(R // tr,), in_specs=[blk] * 4, out_specs=(blk,) * 3,
                          compiler_params=_cp(("parallel",)))(w, g, m, v)


def _sum_lead(a, out_dtype, name):
    n, R, C = a.shape
    tr = _row_tile(R, C * n, 2 << 20)

    def body(a_ref, o_ref):
        acc = a_ref[0].astype(F32)
        for k in range(1, n):
            acc = acc + a_ref[k].astype(F32)
        o_ref[...] = acc.astype(out_dtype)

    return pl.pallas_call(body, name=name, out_shape=jax.ShapeDtypeStruct((R, C), out_dtype), grid=(R // tr,),
                          in_specs=[BS((n, tr, C), lambda i: (0, i, 0))], out_specs=BS((tr, C), lambda i: (i, 0)),
                          compiler_params=_cp(("parallel",)))(a)


def _cctx_grad(parts, c_ctx, name):
    n, D = parts.shape

    def body(p_ref, c_ref, o_ref):
        d = jnp.sum(p_ref[...], axis=0, keepdims=True)
        c = c_ref[...]
        s = _sigmoid(c)
        o_ref[...] = d * s * (1.0 + c * (1.0 - s))

    return pl.pallas_call(body, name=name, out_shape=jax.ShapeDtypeStruct((1, D), F32))(parts, c_ctx.reshape(1, D))


def _me():
    return lax.axis_index("x"), lax.axis_index("y"), lax.axis_index("c")


def _aligned(v, n):
    return v if isinstance(v, int) else pl.multiple_of(v, n)


def _window(ref, r0, c0, R, C):
    rows = pl.ds(_aligned(r0, SUBLANES), R)
    if C == ref.shape[1]:
        return ref.at[rows, :]
    return ref.at[rows, pl.ds(_aligned(c0, LANES), C)]


def _allgather8(items, name):
    n = len(items)

    def body(*refs):
        srcs, dsts = refs[:n], refs[n:2 * n]
        send_sems, recv_sems, local_sems = refs[2 * n:]
        x, y, c = _me()
        me, sibling = (x, y, c), (x, y, 1 - c)
        chips = [(1 - x, y), (x, 1 - y), (1 - x, 1 - y)]

        def dwin(a, blk):
            (R, C), at = items[a][2], items[a][4]
            return _window(dsts[a], *at(*blk), R, C)

        def swin(a):
            (R, C), at = items[a][2], items[a][1]
            return _window(srcs[a], *at(*me), R, C)

        def copy(a, k, blk, to, src=None):
            return pltpu.make_async_remote_copy(
                src_ref=dwin(a, blk) if src is None else src, dst_ref=dwin(a, blk), send_sem=send_sems.at[7 * a + k],
                recv_sem=recv_sems.at[7 * a + k], device_id=to, device_id_type=MESH)

        mine = [pltpu.make_async_copy(swin(a), dwin(a, me), local_sems.at[a]) for a in range(n)]
        for cp in mine:
            cp.start()
        first = []
        for a in range(n):
            first.append(copy(a, 0, me, sibling, src=swin(a)))
            first += [copy(a, 1 + j, me, (*chip, c), src=swin(a)) for j, chip in enumerate(chips)]
        for cp in first:
            cp.start()
        passed = []
        for j, chip in enumerate(chips):
            for a in range(n):
                copy(a, 1 + j, (*chip, c), me).wait_recv()
                fwd = copy(a, 4 + j, (*chip, c), sibling)
                fwd.start()
                passed.append(fwd)
        for a in range(n):
            copy(a, 0, sibling, me).wait_recv()
            for j, chip in enumerate(chips):
                copy(a, 4 + j, (*chip, 1 - c), me).wait_recv()
        for cp in first + passed:
            cp.wait_send()
        for cp in mine:
            cp.wait()

    outs = pl.pallas_call(
        body, name=name, out_shape=tuple(jax.ShapeDtypeStruct(it[3], it[0].dtype) for it in items),
        in_specs=[ANY] * n, out_specs=tuple([ANY] * n),
        scratch_shapes=[pltpu.SemaphoreType.DMA((7 * n,)), pltpu.SemaphoreType.DMA((7 * n,)), pltpu.SemaphoreType.DMA((n,))],
    )(*[it[0] for it in items])
    return list(outs)


def _cast_place(w, kind, name):
    R2, C = w.shape
    tr = _row_tile(R2, C)
    nr = R2 // tr
    xi, yi, _ = _me()
    p_arr = (2 * xi + yi).astype(jnp.int32).reshape(1)
    if kind == "col":
        shape, o_spec = (R2, 4 * C), BS((tr, C), lambda i, p: (i, p[0]))
    else:
        shape, o_spec = (4 * R2, C), BS((tr, C), lambda i, p: (p[0] * nr + i, 0))

    def body(p_ref, w_ref, o_ref):
        o_ref[...] = w_ref[...].astype(BF16)

    return pl.pallas_call(
        body, name=name, out_shape=jax.ShapeDtypeStruct(shape, BF16),
        grid_spec=pltpu.PrefetchScalarGridSpec(num_scalar_prefetch=1, grid=(nr,), in_specs=[BS((tr, C), lambda i, p: (i, 0))],
                                               out_specs=o_spec),
        compiler_params=_cp(("parallel",)),
    )(p_arr, w)


def _gather_start(bufs, wins, sends, recvs, first_sem, x, y, c):
    chips = [(1 - x, y), (x, 1 - y), (1 - x, 1 - y)]
    started = []
    for a, buf in enumerate(bufs):
        (R, C), at = wins[a]
        mine = _window(buf, *at(2 * x + y, c), R, C)
        for j, (px, py) in enumerate(chips):
            k = first_sem + 6 * a + j
            cp = pltpu.make_async_remote_copy(src_ref=mine, dst_ref=mine, send_sem=sends.at[k], recv_sem=recvs.at[k],
                                              device_id=(px, py, c), device_id_type=MESH)
            cp.start()
            started.append(cp)
    return started


def _gather_finish(bufs, wins, sends, recvs, first_sem, started, x, y, c):
    chips = [(1 - x, y), (x, 1 - y), (1 - x, 1 - y)]
    sibling = (x, y, 1 - c)
    passed = []
    for j, (px, py) in enumerate(chips):
        for a, buf in enumerate(bufs):
            (R, C), at = wins[a]
            landed = _window(buf, *at(2 * px + py, c), R, C)
            k = first_sem + 6 * a + j
            pltpu.make_async_remote_copy(src_ref=landed, dst_ref=landed, send_sem=sends.at[k], recv_sem=recvs.at[k],
                                         device_id=(px, py, c), device_id_type=MESH).wait_recv()
            fwd = pltpu.make_async_remote_copy(src_ref=landed, dst_ref=landed, send_sem=sends.at[k + 3], recv_sem=recvs.at[k + 3],
                                               device_id=sibling, device_id_type=MESH)
            fwd.start()
            passed.append(fwd)
    for j, (px, py) in enumerate(chips):
        for a, buf in enumerate(bufs):
            (R, C), at = wins[a]
            theirs = _window(buf, *at(2 * px + py, 1 - c), R, C)
            k = first_sem + 6 * a + j + 3
            pltpu.make_async_remote_copy(src_ref=theirs, dst_ref=theirs, send_sem=sends.at[k], recv_sem=recvs.at[k],
                                         device_id=sibling, device_id_type=MESH).wait_recv()
    for cp in started + passed:
        cp.wait_send()


def _allgather_inplace(fulls, wins, name):
    n = len(fulls)

    def body(*refs):
        bufs = refs[n:2 * n]
        sends, recvs = refs[2 * n:]
        x, y, c = _me()
        started = _gather_start(bufs, wins, sends, recvs, 0, x, y, c)
        _gather_finish(bufs, wins, sends, recvs, 0, started, x, y, c)

    outs = pl.pallas_call(
        body, name=name, out_shape=tuple(jax.ShapeDtypeStruct(f.shape, f.dtype) for f in fulls),
        in_specs=[ANY] * n, out_specs=tuple([ANY] * n), input_output_aliases={a: a for a in range(n)},
        scratch_shapes=[pltpu.SemaphoreType.DMA((6 * n,)), pltpu.SemaphoreType.DMA((6 * n,))],
    )(*fulls)
    return list(outs)


def _rs_to_sibling(grads, wins, name):
    n = len(grads)

    def body(*refs):
        srcs, dsts = refs[:n], refs[n:2 * n]
        send_sems, recv_sems = refs[2 * n:]
        x, y, c = _me()
        sibling = (x, y, 1 - c)
        copies = []
        for a in range(n):
            (R, C), at = wins[a]
            for p in range(4):
                cp = pltpu.make_async_remote_copy(
                    src_ref=_window(srcs[a], *at(p, 1 - c), R, C), dst_ref=dsts[a].at[p], send_sem=send_sems.at[4 * a + p],
                    recv_sem=recv_sems.at[4 * a + p], device_id=sibling, device_id_type=MESH)
                cp.start()
                copies.append(cp)
        for cp in copies:
            cp.wait()

    outs = pl.pallas_call(
        body, name=name, out_shape=tuple(jax.ShapeDtypeStruct((4,) + wins[a][0], grads[a].dtype) for a in range(n)),
        in_specs=[ANY] * n, out_specs=tuple([ANY] * n),
        scratch_shapes=[pltpu.SemaphoreType.DMA((4 * n,)), pltpu.SemaphoreType.DMA((4 * n,))],
    )(*grads)
    return list(outs)


def _rs_add_mine(grad, recv, win, name):
    (R, C), kind = win
    tr = _row_tile(R, C, 1 << 20)
    nr = R // tr
    c_arr = lax.axis_index("c").astype(jnp.int32).reshape(1)

    if kind == "col":
        g_spec = BS((tr, C), lambda p, i, c_ref: (c_ref[0] * nr + i, p))
    else:
        g_spec = BS((tr, C), lambda p, i, c_ref: ((2 * p + c_ref[0]) * nr + i, 0))

    def body(c_ref, g_ref, r_ref, o_ref):
        o_ref[0] = (g_ref[...].astype(F32) + r_ref[0].astype(F32)).astype(BF16)

    blk = BS((1, tr, C), lambda p, i, c_ref: (p, i, 0))
    return pl.pallas_call(
        body, name=name, out_shape=jax.ShapeDtypeStruct((4, R, C), BF16),
        grid_spec=pltpu.PrefetchScalarGridSpec(num_scalar_prefetch=1, grid=(4, nr), in_specs=[g_spec, blk], out_specs=blk),
        compiler_params=_cp(("parallel", "parallel")),
    )(c_arr, grad, recv)


def _rs_across_chips(parts, name):
    n = len(parts)

    def body(*refs):
        srcs, dsts = refs[:n], refs[n:2 * n]
        send_sems, recv_sems = refs[2 * n:]
        x, y, c = _me()
        mine_p = 2 * x + y
        chips = [(1 - x, y), (x, 1 - y), (1 - x, 1 - y)]
        copies = []
        for a in range(n):
            for j, (px, py) in enumerate(chips):
                cp = pltpu.make_async_remote_copy(
                    src_ref=srcs[a].at[2 * px + py], dst_ref=dsts[a].at[mine_p], send_sem=send_sems.at[3 * a + j],
                    recv_sem=recv_sems.at[3 * a + j], device_id=(px, py, c), device_id_type=MESH)
                cp.start()
                copies.append((a, j, cp))
        for a, j, cp in copies:
            px, py = chips[j]
            pltpu.make_async_remote_copy(
                src_ref=srcs[a].at[mine_p], dst_ref=dsts[a].at[2 * px + py], send_sem=send_sems.at[3 * a + j],
                recv_sem=recv_sems.at[3 * a + j], device_id=(px, py, c), device_id_type=MESH).wait_recv()
        for a, j, cp in copies:
            cp.wait_send()

    outs = pl.pallas_call(
        body, name=name, out_shape=tuple(jax.ShapeDtypeStruct(p.shape, p.dtype) for p in parts),
        in_specs=[ANY] * n, out_specs=tuple([ANY] * n),
        scratch_shapes=[pltpu.SemaphoreType.DMA((3 * n,)), pltpu.SemaphoreType.DMA((3 * n,))],
    )(*parts)
    return list(outs)


def _rs_sum_place(part, recv, name):
    _, R, C = part.shape
    tr = _row_tile(R, 4 * C)
    nr = R // tr
    xi, yi, ci = _me()
    pc = jnp.stack([2 * xi + yi, ci]).astype(jnp.int32)

    def body(s_ref, p_ref, r1_ref, r2_ref, r3_ref, o_ref):
        o_ref[...] = ((p_ref[0].astype(F32) + r1_ref[0].astype(F32)) + r2_ref[0].astype(F32)) + r3_ref[0].astype(F32)

    def other(k):
        return BS((1, tr, C), lambda i, s: (jnp.bitwise_xor(s[0], k), i, 0))

    return pl.pallas_call(
        body, name=name, out_shape=jax.ShapeDtypeStruct((2 * R, C), F32),
        grid_spec=pltpu.PrefetchScalarGridSpec(
            num_scalar_prefetch=1, grid=(nr,), in_specs=[BS((1, tr, C), lambda i, s: (s[0], i, 0)), other(1), other(2), other(3)],
            out_specs=BS((tr, C), lambda i, s: (s[1] * nr + i, 0))),
        compiler_params=_cp(("parallel",)),
    )(pc, part, recv, recv, recv)


def _rs_share(shards, name):
    n = len(shards)

    def body(*refs):
        bufs = refs[n:2 * n]
        send_sems, recv_sems = refs[2 * n:]
        x, y, c = _me()
        sibling = (x, y, 1 - c)
        remote = []
        for a in range(n):
            R = bufs[a].shape[0] // 2
            mine = bufs[a].at[pl.ds(pl.multiple_of(c * R, SUBLANES), R), :]
            cp = pltpu.make_async_remote_copy(src_ref=mine, dst_ref=mine, send_sem=send_sems.at[a], recv_sem=recv_sems.at[a],
                                              device_id=sibling, device_id_type=MESH)
            cp.start()
            remote.append(cp)
        for a in range(n):
            R = bufs[a].shape[0] // 2
            theirs = bufs[a].at[pl.ds(pl.multiple_of((1 - c) * R, SUBLANES), R), :]
            pltpu.make_async_remote_copy(src_ref=theirs, dst_ref=theirs, send_sem=send_sems.at[a], recv_sem=recv_sems.at[a],
                                         device_id=sibling, device_id_type=MESH).wait_recv()
        for cp in remote:
            cp.wait_send()

    outs = pl.pallas_call(
        body, name=name, out_shape=tuple(jax.ShapeDtypeStruct(h.shape, h.dtype) for h in shards),
        in_specs=[ANY] * n, out_specs=tuple([ANY] * n), input_output_aliases={a: a for a in range(n)},
        scratch_shapes=[pltpu.SemaphoreType.DMA((n,)), pltpu.SemaphoreType.DMA((n,))],
    )(*shards)
    return list(outs)


BLOB_ALIGN = SUBLANES * LANES


def _pack(arrs):
    flat = jnp.concatenate([a.reshape(-1).astype(F32) for a in arrs])
    n = flat.shape[0]
    padded = -(-n // BLOB_ALIGN) * BLOB_ALIGN
    return jnp.pad(flat, (0, padded - n)).reshape(padded // LANES, LANES)


def _unpack(flat, shapes):
    out, off = [], 0
    for s in shapes:
        n = math.prod(s)
        out.append(flat[..., off:off + n].reshape(flat.shape[:-1] + tuple(s)))
        off += n
    return out


def _gather_blob(blob, name):
    r = blob.shape[0]
    at = lambda px, py, pc: ((4 * px + 2 * py + pc) * r, 0)
    (out,) = _allgather8([(blob, lambda px, py, pc: (0, 0), (r, LANES), (8 * r, LANES), at)], name)
    return out.reshape(8, r * LANES)


def _conv_ffn_fwd(X, mods, n2g, w_up, conv_w, conv_b, w_down, L, tb, nlat, ch, tag):
    sh2, sc2, g2 = mods[3], mods[4], mods[5]
    h2 = _norm_mod_fwd(X, n2g, sh2, sc2, tb, nlat, tag + "_norm2")
    zf = _mm(h2, w_up, "nn", BF16, tag + "_up", tm=(544, 512), tn=(1408, 512))
    f = _ffn_act_fwd(zf, conv_w, conv_b, L, ch, tag + "_act")
    yf = _mm(f, w_down, "nn", F32, tag + "_down", tm=(544, 512), tn=(512,))
    Xn = _gate_res_fwd(X, yf, g2, tb, nlat, tag + "_res2")
    return Xn, (X, h2, zf, f, yf)


def _conv_ffn_bwd(dXn, saved, mods, n2g, w_up, conv_w, conv_b, w_down, L, tb, nlat, ch, tag):
    X, h2, zf, f, yf = saved
    sc2, g2 = mods[4], mods[5]
    dy, dg2 = _gate_bwd(dXn, yf, g2, tb, nlat, tag + "_dres2")
    df = _mm(dy, w_down, "nt", BF16, tag + "_ddown_x", tm=(544, 512), tn=(1408, 512))
    dw_down = _mm(f, dy, "tn", BF16, tag + "_ddown_w", tm=(1408, 512), tn=(2048,), tk=(544, 512), outer="i")
    dgp, du, dcw, dcb = _ffn_act_bwd(zf, df, conv_w, conv_b, L, ch, tag + "_dact")
    dzf = jnp.concatenate([dgp, du], axis=1)
    dh2 = _mm(dzf, w_up, "nt", BF16, tag + "_dup_x", tm=(1088, 1024, 512), tn=(1024,), tk=(2816, 512))
    dw_up = _mm(h2, dzf, "tn", BF16, tag + "_dup_w", tm=(2048,), tn=(1408, 512), tk=(544, 512))
    dX, dn2g, dsh2, dsc2 = _norm_mod_bwd(X, n2g, sc2, dh2, dXn, tb, nlat, tag + "_dnorm2")
    return dX, dict(n2g=dn2g, sh2=dsh2, sc2=dsc2, g2=dg2, cw=dcw, cb=dcb, w_up=dw_up, w_down=dw_down)


def kernel(x, c, ctx, c_ctx, norm1_g, norm2_g, w_ada, b_ada, ab_w_in, ab_b_in, a_ln_g, a_ln_b, a_w_s, a_b_s, b_conv_w, b_conv_b, b_ln_g, b_ln_b, ab_w_out, mla_w_in, mla_q_norm_g, mla_w_uq, mla_kv_norm_g, mla_w_ukv, mla_w_o, ffn_w_up, ffn_conv_w, ffn_conv_b, ffn_w_down, final_norm_g, loss_target, m_c_ctx, m_norm1_g, m_norm2_g, m_w_ada, m_b_ada, m_ab_w_in, m_ab_b_in, m_a_ln_g, m_a_ln_b, m_a_w_s, m_a_b_s, m_b_conv_w, m_b_conv_b, m_b_ln_g, m_b_ln_b, m_ab_w_out, m_mla_w_in, m_mla_q_norm_g, m_mla_w_uq, m_mla_kv_norm_g, m_mla_w_ukv, m_mla_w_o, m_ffn_w_up, m_ffn_conv_w, m_ffn_conv_b, m_ffn_w_down, m_final_norm_g, v_c_ctx, v_norm1_g, v_norm2_g, v_w_ada, v_b_ada, v_ab_w_in, v_ab_b_in, v_a_ln_g, v_a_ln_b, v_a_w_s, v_a_b_s, v_b_conv_w, v_b_conv_b, v_b_ln_g, v_b_ln_b, v_ab_w_out, v_mla_w_in, v_mla_q_norm_g, v_mla_w_uq, v_mla_kv_norm_g, v_mla_w_ukv, v_mla_w_o, v_ffn_w_up, v_ffn_conv_w, v_ffn_conv_b, v_ffn_w_down, v_final_norm_g):
    W = dict(c_ctx=c_ctx, norm1_g=norm1_g, norm2_g=norm2_g, w_ada=w_ada, b_ada=b_ada, ab_w_in=ab_w_in, ab_b_in=ab_b_in, a_ln_g=a_ln_g,
             a_ln_b=a_ln_b, a_w_s=a_w_s, a_b_s=a_b_s, b_conv_w=b_conv_w, b_conv_b=b_conv_b, b_ln_g=b_ln_g, b_ln_b=b_ln_b,
             ab_w_out=ab_w_out, mla_w_in=mla_w_in, mla_q_norm_g=mla_q_norm_g, mla_w_uq=mla_w_uq, mla_kv_norm_g=mla_kv_norm_g,
             mla_w_ukv=mla_w_ukv, mla_w_o=mla_w_o, ffn_w_up=ffn_w_up, ffn_conv_w=ffn_conv_w, ffn_conv_b=ffn_conv_b,
             ffn_w_down=ffn_w_down, final_norm_g=final_norm_g)
    MOM = dict(c_ctx=m_c_ctx, norm1_g=m_norm1_g, norm2_g=m_norm2_g, w_ada=m_w_ada, b_ada=m_b_ada, ab_w_in=m_ab_w_in, ab_b_in=m_ab_b_in,
               a_ln_g=m_a_ln_g, a_ln_b=m_a_ln_b, a_w_s=m_a_w_s, a_b_s=m_a_b_s, b_conv_w=m_b_conv_w, b_conv_b=m_b_conv_b,
               b_ln_g=m_b_ln_g, b_ln_b=m_b_ln_b, ab_w_out=m_ab_w_out, mla_w_in=m_mla_w_in, mla_q_norm_g=m_mla_q_norm_g,
               mla_w_uq=m_mla_w_uq, mla_kv_norm_g=m_mla_kv_norm_g, mla_w_ukv=m_mla_w_ukv, mla_w_o=m_mla_w_o, ffn_w_up=m_ffn_w_up,
               ffn_conv_w=m_ffn_conv_w, ffn_conv_b=m_ffn_conv_b, ffn_w_down=m_ffn_w_down, final_norm_g=m_final_norm_g)
    VAR = dict(c_ctx=v_c_ctx, norm1_g=v_norm1_g, norm2_g=v_norm2_g, w_ada=v_w_ada, b_ada=v_b_ada, ab_w_in=v_ab_w_in, ab_b_in=v_ab_b_in,
               a_ln_g=v_a_ln_g, a_ln_b=v_a_ln_b, a_w_s=v_a_w_s, a_b_s=v_a_b_s, b_conv_w=v_b_conv_w, b_conv_b=v_b_conv_b,
               b_ln_g=v_b_ln_g, b_ln_b=v_b_ln_b, ab_w_out=v_ab_w_out, mla_w_in=v_mla_w_in, mla_q_norm_g=v_mla_q_norm_g,
               mla_w_uq=v_mla_w_uq, mla_kv_norm_g=v_mla_kv_norm_g, mla_w_ukv=v_mla_w_ukv, mla_w_o=v_mla_w_o, ffn_w_up=v_ffn_w_up,
               ffn_conv_w=v_ffn_conv_w, ffn_conv_b=v_ffn_conv_b, ffn_w_down=v_ffn_w_down, final_norm_g=v_final_norm_g)
    ORDER = list(W.keys())

    L, D = x.shape[1], x.shape[2]
    CT = ctx.shape[1]
    T = L + CT
    AW, BW = a_ln_g.shape[-1], b_ln_g.shape[-1]
    AH = a_w_s.shape[1]
    QL, KVL = 4 * mla_q_norm_g.shape[-1], 4 * mla_kv_norm_g.shape[-1]
    H = 4 * mla_w_o.shape[1] // HEAD
    HS = H // 4
    DFF = ffn_conv_b.shape[-1]
    NA = w_ada.shape[-1]
    tb = 256 if (L % 256 == 0 and CT % 256 == 0) else 128
    nlat = L // tb
    ch = tb
    tq = 256 if L >= 512 else 128
    xi, yi, ci = _me()
    p_me = 2 * xi + yi
    dev = 4 * xi + 2 * yi + ci

    shard_small = [c[0], mla_q_norm_g[0], mla_kv_norm_g[0], b_conv_w[0], ffn_conv_w]
    g0 = _gather_blob(_pack(shard_small), "gather_small")
    c_all, qg_s, kvg_s, bcw_s, fcw_s = _unpack(g0, [a.shape for a in shard_small])
    per_chip = lambda a: a[0::2]
    qg = per_chip(qg_s).reshape(QL)
    kvg = per_chip(kvg_s).reshape(KVL)
    bcw = jnp.concatenate(list(per_chip(bcw_s)), axis=-1)
    fcw = jnp.concatenate(list(per_chip(fcw_s)), axis=-1)
    c16 = jnp.concatenate([c_all, c_ctx[None], jnp.zeros((7, D), F32)], axis=0)

    ms = []
    for i in range(2):
        bias = lax.dynamic_slice(b_ada[i], (p_me * NA,), (NA,))
        ms.append(_mm(c16, w_ada[i], "nn", F32, f"ada{i}", tm=(16,), tn=(512,), bias=bias, silu_a=True))
    ms = jnp.concatenate(ms, axis=0)
    (mods_all,) = _allgather8(
        [(ms, lambda px, py, pc: (pc * 16, 0), (16, NA), (32, 4 * NA), lambda px, py, pc: (pc * 16, (2 * px + py) * NA))], "gather_mods")
    mods_all = mods_all.reshape(2, 16, N_MOD, D)
    mods = []
    for i in range(2):
        lat = lax.dynamic_index_in_dim(mods_all[i], dev, axis=0, keepdims=False)
        both = jnp.stack([lat, mods_all[i, 8]], axis=0)
        mods.append([both[:, k][:, None, :] for k in range(N_MOD)])

    def pad_uq(w):
        w = w.reshape(w.shape[0], HS, HEAD + ROPE)
        return jnp.pad(w, ((0, 0), (0, 0), (0, QHEAD - HEAD - ROPE))).reshape(w.shape[0], HS * QHEAD)

    MI = QL + KVL + LANES
    big = [
        ("ab_w_in", ab_w_in[0], "col"), ("ab_w_out", ab_w_out[0], "row"),
        ("mla_w_in", jnp.pad(mla_w_in[0], ((0, 0), (0, MI - mla_w_in.shape[-1]))), "row"),
        ("mla_w_uq", pad_uq(mla_w_uq[0]), "col"), ("mla_w_ukv", mla_w_ukv[0], "col"), ("mla_w_o", mla_w_o[0], "row"),
        ("ffn_w_up0", ffn_w_up[0], "col"), ("ffn_w_up1", ffn_w_up[1], "col"),
        ("ffn_w_down0", ffn_w_down[0], "row"), ("ffn_w_down1", ffn_w_down[1], "row"),
    ]
    placed, wins = [], {}
    for nm, w, kind in big:
        R, C = w.shape[0] // 2, w.shape[1]
        if kind == "col":
            wins[nm] = (((R, C), lambda p, pc, R=R, C=C: (pc * R, p * C)), kind)
        else:
            wins[nm] = (((R, C), lambda p, pc, R=R, C=C: ((2 * p + pc) * R, 0)), kind)
        placed.append(_cast_place(w, kind, "cast_" + nm))
    names = [b[0] for b in big]
    full = dict(zip(names, _allgather_inplace(placed, [wins[nm][0] for nm in names], "gather_weights")))

    X0 = jnp.concatenate([x[0], ctx[0]], axis=0)
    m0, m1 = mods[0], mods[1]
    h1 = _norm_mod_fwd(X0, norm1_g[0], m0[0], m0[1], tb, nlat, "l0_norm1")
    z = _mm(h1, full["ab_w_in"], "nn", BF16, "l0_in", tm=(544, 512), tn=(1024, 512), bias=ab_b_in[0])
    bs_full = jnp.broadcast_to(a_b_s[0][:, :, None], (AH, CHUNK, CHUNK))
    ya = _gmlp_fwd(z, a_ln_g[0], a_ln_b[0], a_w_s[0], bs_full, tb, "l0_gmlp")
    hc = _glu_conv_fwd(z, 2 * AW, bcw, b_conv_b[0], L, ch, "l0_conv")
    yb = _ln_silu_fwd(hc, b_ln_g[0], b_ln_b[0], tb, "l0_lnsilu")
    yab = jnp.concatenate([ya, yb], axis=1)
    y0 = _mm(yab, full["ab_w_out"], "nn", F32, "l0_out", tm=(544, 512), tn=(1024, 512))
    X1 = _gate_res_fwd(X0, y0, m0[2], tb, nlat, "l0_res1")
    X2, ffn0 = _conv_ffn_fwd(X1, m0, norm2_g[0], full["ffn_w_up0"], fcw[0], ffn_conv_b[0], full["ffn_w_down0"], L, tb, nlat, ch, "l0_ffn")

    tabs = _rope_tables(L, T)
    hm = _norm_mod_fwd(X2, norm1_g[1], m1[0], m1[1], tb, nlat, "l1_norm1")
    zm = _mm(hm, full["mla_w_in"], "nn", F32, "l1_in", tm=(544, 512), tn=(MI,))
    cqn, ckvn, kpe = _mla_prep_fwd(zm, qg, kvg, tabs, tb, "l1_prep")
    q = _mm(cqn, full["mla_w_uq"], "nn", BF16, "l1_uq", tm=(512,), tn=(1024, 512), rows=L)
    kvh = _mm(ckvn, full["mla_w_ukv"], "nn", BF16, "l1_ukv", tm=(544, 512), tn=(1024, 512))
    o, lse = _attn_fwd(q, kvh, kpe, tabs, L, tq, "l1_attn")
    yl = _mm(o, full["mla_w_o"], "nn", F32, "l1_o", tm=(512,), tn=(1024, 512))
    m1_lat = [a[:1] for a in m1]
    X2l = X2[:L]
    X3 = _gate_res_fwd(X2l, yl, m1_lat[2], tb, nlat, "l1_res1")
    X4, ffn1 = _conv_ffn_fwd(X3, m1_lat, norm2_g[1], full["ffn_w_up1"], fcw[1], ffn_conv_b[1], full["ffn_w_down1"], L, tb, nlat, ch, "l1_ffn")
    loss_acc, dX4, dfinal = _final_loss(X4, final_norm_g, loss_target[0], tb, "loss")

    dX3, gf1 = _conv_ffn_bwd(dX4, ffn1, m1_lat, norm2_g[1], full["ffn_w_up1"], fcw[1], ffn_conv_b[1], full["ffn_w_down1"], L, tb, nlat, ch, "l1_ffn")
    dyl, dg1_1 = _gate_bwd(dX3, yl, m1_lat[2], tb, nlat, "l1_dres1")
    do = _mm(dyl, full["mla_w_o"], "nt", BF16, "l1_do_x", tm=(512,), tn=(1024, 512))
    dw_o = _mm(o, dyl, "tn", BF16, "l1_do_w", tm=(1024, 512), tn=(2048, 512), tk=(512,), outer="i")
    dq, dkv, dkpe = _attn_bwd(q, kvh, kpe, tabs, o, lse, do, L, tq, "l1_dattn")
    dckvn = _mm(dkv, full["mla_w_ukv"], "nt", BF16, "l1_dukv_x", tm=(544, 512), tn=(KVL,), tk=(2048, 512))
    dw_ukv = _mm(ckvn, dkv, "tn", BF16, "l1_dukv_w", tm=(KVL,), tn=(1024, 512), tk=(544, 512))
    dcqn = _mm(dq, full["mla_w_uq"], "nt", BF16, "l1_duq_x", tm=(512,), tn=(QL,), tk=(2048, 512))
    dw_uq = _mm(cqn, dq, "tn", BF16, "l1_duq_w", tm=(QL,), tn=(1024, 512), tk=(512,), rows=L)
    dcqn = jnp.concatenate([dcqn, jnp.zeros((CT, QL), BF16)], axis=0)
    dzm, dqg, dkvg = _mla_prep_bwd(zm, qg, kvg, tabs, dcqn, dckvn, dkpe, tb, "l1_dprep")
    dhm = _mm(dzm, full["mla_w_in"], "nt", BF16, "l1_din_x", tm=(544, 512), tn=(1024, 512))
    dw_min = _mm(hm, dzm, "tn", BF16, "l1_din_w", tm=(2048,), tn=(MI,), tk=(544, 512))
    dX3p = jnp.concatenate([dX3, jnp.zeros((CT, D), F32)], axis=0)
    dX2, dn1g_1, dsh1_1, dsc1_1 = _norm_mod_bwd(X2, norm1_g[1], m1[1], dhm, dX3p, tb, nlat, "l1_dnorm1")

    dX1, gf0 = _conv_ffn_bwd(dX2, ffn0, m0, norm2_g[0], full["ffn_w_up0"], fcw[0], ffn_conv_b[0], full["ffn_w_down0"], L, tb, nlat, ch, "l0_ffn")
    dy0, dg1_0 = _gate_bwd(dX1, y0, m0[2], tb, nlat, "l0_dres1")
    dyab = _mm(dy0, full["ab_w_out"], "nt", BF16, "l0_dout_x", tm=(544, 512), tn=(1024, 512))
    dw_out = _mm(yab, dy0, "tn", BF16, "l0_dout_w", tm=(1024, 512), tn=(2048, 512), tk=(544, 512), outer="i")
    dzu, dzv, dlnag, dlnab, dws, dbs, csu, csv = _gmlp_bwd(z, dyab, a_ln_g[0], a_ln_b[0], a_w_s[0], bs_full, tb, "l0_dgmlp")
    dhc, dlnbg, dlnbb = _ln_silu_bwd(hc, dyab, AW, b_ln_g[0], b_ln_b[0], tb, "l0_dlnsilu")
    dza, dzg, dbcw, dbcb, csa, csg = _glu_conv_bwd(z, 2 * AW, dhc, bcw, L, ch, "l0_dconv")
    dz = jnp.concatenate([dzu, dzv, dza, dzg], axis=1)
    dbin = jnp.concatenate([csu, csv, csa, csg], axis=1)
    dh1 = _mm(dz, full["ab_w_in"], "nt", BF16, "l0_din_x", tm=(544, 512), tn=(1024, 512), tk=(2048, 512))
    dw_in = _mm(h1, dz, "tn", BF16, "l0_din_w", tm=(2048,), tn=(1024, 512), tk=(544, 512))
    dX0, dn1g_0, dsh1_0, dsc1_0 = _norm_mod_bwd(X0, norm1_g[0], m0[1], dh1, dX1, tb, nlat, "l0_dnorm1")
    grad_x = dX0[:L][None]

    gbig = dict(ab_w_in=dw_in, ab_w_out=dw_out, mla_w_in=dw_min, mla_w_uq=dw_uq, mla_w_ukv=dw_ukv, mla_w_o=dw_o,
                ffn_w_up0=gf0["w_up"], ffn_w_up1=gf1["w_up"], ffn_w_down0=gf0["w_down"], ffn_w_down1=gf1["w_down"])
    recv0 = _rs_to_sibling([gbig[nm] for nm in names], [wins[nm][0] for nm in names], "rs_sibling")
    parts = [_rs_add_mine(gbig[nm], recv0[a], (wins[nm][0][0], wins[nm][1]), "rs_add_" + nm) for a, nm in enumerate(names)]
    recv1 = _rs_across_chips(parts, "rs_chips")
    halves = [_rs_sum_place(pt, r, "rs_sum_" + nm) for pt, r, nm in zip(parts, recv1, names)]
    gshard = dict(zip(names, _rs_share(halves, "rs_share")))

    def grp6(l, sh1, sc1, g1, f):
        G = sh1.shape[0]
        pad = lambda a: jnp.concatenate([a, jnp.zeros((G - a.shape[0],) + a.shape[1:], F32)], axis=0) if a.shape[0] < G else a
        return jnp.concatenate([pad(a) for a in (sh1, sc1, g1, f["sh2"], f["sc2"], f["g2"])], axis=1)

    dm0 = grp6(0, dsh1_0, dsc1_0, dg1_0, gf0)
    dm1 = grp6(1, dsh1_1, dsc1_1, dg1_1, gf1)
    dmods = jnp.stack([dm0, dm1], axis=0)
    small = [
        jnp.concatenate([dn1g_0, dn1g_1], axis=0), jnp.concatenate([gf0["n2g"], gf1["n2g"]], axis=0), dbin, dlnag, dlnab, dws,
        dbs, dbcw, dbcb, dlnbg, dlnbb, dqg, dkvg, jnp.stack([gf0["cw"], gf1["cw"]], axis=0),
        jnp.concatenate([gf0["cb"], gf1["cb"]], axis=0), dfinal, dmods[:, 1],
    ]
    small_shapes = [a.shape for a in small]
    lat_shape = dmods[:, 0].shape
    blob = _pack(small + [dmods[:, 0], loss_acc[0, :1]])
    gathered = _gather_blob(blob, "gather_grads")
    summed = _sum_lead(gathered.reshape(8, -1, LANES), F32, "sum_grads").reshape(-1)
    (dn1g, dn2g, dbin_s, dlnag_s, dlnab_s, dws_s, dbs_s, dbcw_s, dbcb_s, dlnbg_s, dlnbb_s, dqg_s, dkvg_s, dfcw_s, dfcb_s, dfinal_s,
     dmods_ctx, dmods_lat_sum, loss) = _unpack(summed, small_shapes + [lat_shape, (1,)])
    loss = loss.reshape(())
    n_small = sum(math.prod(s) for s in small_shapes)
    dmods_lat = gathered[:, n_small:n_small + math.prod(lat_shape)].reshape((8,) + lat_shape)

    grad_w_ada, dc_parts = [], []
    for i in range(2):
        dm16 = jnp.concatenate([dmods_lat[:, i].reshape(8, N_MOD * D), dmods_ctx[i].reshape(1, N_MOD * D),
                                jnp.zeros((7, N_MOD * D), F32)], axis=0)
        dm16_s = lax.dynamic_slice(dm16, (0, p_me * NA), (16, NA))
        grad_w_ada.append(_mm(c16, dm16_s, "tn", F32, f"dada{i}_w", tm=(1024, 512), tn=(1024, 512), silu_a=True))
        dc_parts.append(_mm(dm16_s, w_ada[i], "nt", F32, f"dada{i}_c", tm=(16,), tn=(512,), tk=(1024, 512)))
    grad_w_ada = jnp.stack(grad_w_ada, axis=0)
    grad_b_ada = dmods_lat_sum.reshape(2, N_MOD * D) + dmods_ctx.reshape(2, N_MOD * D)
    dc_blob = _pack([dc_parts[0][8] + dc_parts[1][8]])
    dc_all = _gather_blob(dc_blob, "gather_dc")[0::2, :D]
    grad_c_ctx = _cctx_grad(dc_all, c_ctx, "dcctx").reshape(D)

    def my_cols(a, axis, n):
        return lax.dynamic_slice_in_dim(a, p_me * n, n, axis=axis)

    unpad_uq = lambda g: g.reshape(QL, HS, QHEAD)[:, :, :HEAD + ROPE].reshape(QL, HS * (HEAD + ROPE))
    grads = dict(
        c_ctx=grad_c_ctx, norm1_g=dn1g, norm2_g=dn2g, w_ada=grad_w_ada, b_ada=grad_b_ada, ab_w_in=gshard["ab_w_in"][None],
        ab_b_in=dbin_s, a_ln_g=dlnag_s, a_ln_b=dlnab_s, a_w_s=dws_s[None], a_b_s=dbs_s.reshape(1, AH, CHUNK),
        b_conv_w=my_cols(dbcw_s, 1, BW // 4)[None], b_conv_b=dbcb_s, b_ln_g=dlnbg_s, b_ln_b=dlnbb_s,
        ab_w_out=gshard["ab_w_out"][None], mla_w_in=gshard["mla_w_in"][:, :mla_w_in.shape[-1]][None],
        mla_q_norm_g=my_cols(dqg_s, 1, QL // 4), mla_w_uq=unpad_uq(gshard["mla_w_uq"])[None],
        mla_kv_norm_g=my_cols(dkvg_s, 1, KVL // 4), mla_w_ukv=gshard["mla_w_ukv"][None], mla_w_o=gshard["mla_w_o"][None],
        ffn_w_up=jnp.stack([gshard["ffn_w_up0"], gshard["ffn_w_up1"]], axis=0), ffn_conv_w=my_cols(dfcw_s, 2, DFF // 4),
        ffn_conv_b=dfcb_s, ffn_w_down=jnp.stack([gshard["ffn_w_down0"], gshard["ffn_w_down1"]], axis=0),
        final_norm_g=dfinal_s.reshape(D),
    )
    grads = {k: grads[k].reshape(W[k].shape) for k in ORDER}

    BIG = ("w_ada", "ab_w_in", "ab_w_out", "mla_w_in", "mla_w_uq", "mla_w_ukv", "mla_w_o", "ffn_w_up", "ffn_w_down")
    delta, new_m, new_v = {}, {}, {}
    for k in BIG:
        two = lambda a: a.reshape(-1, a.shape[-1])
        d_, m_, v_ = _adamw(two(W[k]), two(grads[k]), two(MOM[k]), two(VAR[k]), "adamw_" + k)
        delta[k], new_m[k], new_v[k] = (a.reshape(W[k].shape) for a in (d_, m_, v_))
    SMALL = [k for k in ORDER if k not in BIG]
    d_, m_, v_ = _adamw(_pack([W[k] for k in SMALL]), _pack([grads[k] for k in SMALL]), _pack([MOM[k] for k in SMALL]),
                        _pack([VAR[k] for k in SMALL]), "adamw_small")
    shapes = [W[k].shape for k in SMALL]
    for k, dd, mm, vv in zip(SMALL, _unpack(d_.reshape(-1), shapes), _unpack(m_.reshape(-1), shapes), _unpack(v_.reshape(-1), shapes)):
        delta[k], new_m[k], new_v[k] = dd, mm, vv

    return (loss, grad_x, *[grads[k] for k in ORDER], *[delta[k] for k in ORDER], *[new_m[k] for k in ORDER],
            *[new_v[k] for k in ORDER])
```

```python
import functools
import math

import jax
import jax.numpy as jnp
from jax import lax
from jax.experimental import pallas as pl
from jax.experimental.pallas import tpu as pltpu

F32 = jnp.float32
BF16 = jnp.bfloat16
MESH = pl.DeviceIdType.MESH
ANY = pl.BlockSpec(memory_space=pl.ANY)

EPS = 1e-6
N_MOD = 6
CHUNK = 128
HEAD = 128
ROPE = 64
QHEAD = 2 * HEAD
GRID_W = 64
ROPE_THETA = 10000.0
B_CONV = 31
FFN_CONV = 3
ADAM_LR, ADAM_B1, ADAM_B2, ADAM_EPS, ADAM_WD, ADAM_STEP = 0.001, 0.9, 0.999, 1e-08, 0.01, 10

V7X_VMEM_LIMIT = 56 * 1024 * 1024
LANES = 128
SUBLANES = 8
CONV_PAD = 16

BS = pl.BlockSpec


def _cp(sem=None, vmem=V7X_VMEM_LIMIT):
    return pltpu.CompilerParams(dimension_semantics=sem, vmem_limit_bytes=vmem)


class _Side:
    def __init__(self, bufs, nsem, start, finish):
        self.bufs, self.nsem, self.start, self.finish = list(bufs), nsem, start, finish


def _call(body, *, name, out_shape, grid, in_specs, out_specs, operands, sem, scratch=(), side=None):
    if side is None:
        return pl.pallas_call(body, name=name, out_shape=out_shape, grid=grid, in_specs=in_specs, out_specs=out_specs,
                              scratch_shapes=list(scratch), compiler_params=_cp(sem))(*operands)
    multi = isinstance(out_shape, (tuple, list))
    outs = tuple(out_shape) if multi else (out_shape,)
    ospecs = tuple(out_specs) if multi else (out_specs,)
    n_in, n_out, n_scr, n_buf = len(in_specs), len(outs), len(scratch), len(side.bufs)

    def body2(*refs):
        o0 = n_in + n_buf
        s0 = o0 + n_out + n_buf
        bufs = refs[o0 + n_out:s0]
        sends, recvs = refs[s0 + n_scr:]
        first = last = None
        for d, g in enumerate(grid):
            pid = pl.program_id(d)
            first = (pid == 0) if first is None else jnp.logical_and(first, pid == 0)
            last = (pid == g - 1) if last is None else jnp.logical_and(last, pid == g - 1)

        @pl.when(first)
        def _():
            side.start(bufs, sends, recvs)

        body(*refs[:n_in], *refs[o0:o0 + n_out], *refs[s0:s0 + n_scr])

        @pl.when(last)
        def _():
            side.finish(bufs, sends, recvs)

    res = pl.pallas_call(
        body2, name=name, out_shape=outs + tuple(jax.ShapeDtypeStruct(b.shape, b.dtype) for b in side.bufs), grid=grid,
        in_specs=list(in_specs) + [ANY] * n_buf, out_specs=ospecs + (ANY,) * n_buf,
        scratch_shapes=list(scratch) + [pltpu.SemaphoreType.DMA((side.nsem,)), pltpu.SemaphoreType.DMA((side.nsem,))],
        input_output_aliases={n_in + i: n_out + i for i in range(n_buf)}, compiler_params=_cp(("arbitrary",) * len(grid)),
    )(*operands, *side.bufs)
    main = res[:n_out]
    return (tuple(main) if multi else main[0]), list(res[n_out:])


def _pick(n, prefs):
    for p in prefs:
        if p <= n and n % p == 0:
            return p
    return n


def _row_tile(rows, cols, target_bytes=1 << 20):
    best = None
    for d in range(2 * SUBLANES, rows + 1, 2 * SUBLANES):
        if rows % d == 0 and d * cols * 4 <= target_bytes:
            best = d
    return best if best is not None else rows


def _sigmoid(x):
    return 1.0 / (1.0 + jnp.exp(-x))


def _gelu(x):
    c = math.sqrt(2.0 / math.pi)
    th = jnp.tanh(c * (x + 0.044715 * x * x * x))
    return 0.5 * x * (1.0 + th), th


def _gelu_grad(x, th):
    c = math.sqrt(2.0 / math.pi)
    return 0.5 * (1.0 + th) + 0.5 * x * (1.0 - th * th) * c * (1.0 + 3.0 * 0.044715 * x * x)


_DIMS = {"nn": (((1,), (0,)), ((), ())), "nt": (((1,), (1,)), ((), ())), "tn": (((0,), (0,)), ((), ()))}


def _mm(a, b, mode, out_dtype, name, tm=(512,), tn=(512,), tk=(100000,), bias=None, silu_a=False, rows=None, outer="j", side=None):
    if mode == "nn":
        (M, K), N = a.shape, b.shape[1]
    elif mode == "nt":
        (M, K), N = a.shape, b.shape[0]
    else:
        (K, M), N = a.shape, b.shape[1]
    if rows is not None:
        if mode == "tn":
            K = rows
        else:
            M = rows
    tm, tn, tk = _pick(M, tm), _pick(N, tn), _pick(K, tk)
    gm, gn, gk = M // tm, N // tn, K // tk

    def ij(g0, g1):
        return (g1, g0) if outer == "j" else (g0, g1)

    if mode == "nn":
        a_spec = BS((tm, tk), lambda g0, g1, k: (ij(g0, g1)[0], k))
        b_spec = BS((tk, tn), lambda g0, g1, k: (k, ij(g0, g1)[1]))
    elif mode == "nt":
        a_spec = BS((tm, tk), lambda g0, g1, k: (ij(g0, g1)[0], k))
        b_spec = BS((tn, tk), lambda g0, g1, k: (ij(g0, g1)[1], k))
    else:
        a_spec = BS((tk, tm), lambda g0, g1, k: (k, ij(g0, g1)[0]))
        b_spec = BS((tk, tn), lambda g0, g1, k: (k, ij(g0, g1)[1]))
    in_specs = [a_spec, b_spec]
    operands = [a, b]
    if bias is not None:
        in_specs.append(BS((1, tn), lambda g0, g1, k: (0, ij(g0, g1)[1])))
        operands.append(bias.reshape(1, N))
    o_spec = BS((tm, tn), lambda g0, g1, k: ij(g0, g1))

    def body(*refs):
        a_ref, b_ref = refs[0], refs[1]
        bias_ref = refs[2] if bias is not None else None
        o_ref = refs[3] if bias is not None else refs[2]
        av = a_ref[...]
        if silu_a:
            av = av.astype(F32)
            av = av * _sigmoid(av)
        part = lax.dot_general(av.astype(BF16), b_ref[...].astype(BF16), _DIMS[mode], preferred_element_type=F32)

        def finish(acc):
            if bias_ref is not None:
                acc = acc + bias_ref[...]
            return acc.astype(out_dtype)

        if gk == 1:
            o_ref[...] = finish(part)
        else:
            acc_ref = refs[-1]
            k = pl.program_id(2)

            @pl.when(k == 0)
            def _():
                acc_ref[...] = part

            @pl.when(k > 0)
            def _():
                acc_ref[...] += part

            @pl.when(k == gk - 1)
            def _():
                o_ref[...] = finish(acc_ref[...])

    grid = (gn, gm, gk) if outer == "j" else (gm, gn, gk)
    return _call(body, name=name, out_shape=jax.ShapeDtypeStruct((M, N), out_dtype), grid=grid, in_specs=in_specs, out_specs=o_spec,
                 scratch=[pltpu.VMEM((tm, tn), F32)] if gk > 1 else [], sem=("parallel", "parallel", "arbitrary"),
                 operands=operands, side=side)


def _accum(ref, val, first):
    @pl.when(first)
    def _():
        ref[...] = val

    @pl.when(jnp.logical_not(first))
    def _():
        ref[...] += val


def _norm_mod_fwd(X, gain, sh, sc, tb, nlat, name):
    R, D = X.shape

    def body(x_ref, g_ref, sh_ref, sc_ref, o_ref):
        x = x_ref[...]
        r = lax.rsqrt(jnp.mean(x * x, axis=-1, keepdims=True) + EPS)
        o_ref[...] = ((x * r * g_ref[...]) * (1.0 + sc_ref[0]) + sh_ref[0]).astype(BF16)

    grp = BS((1, 1, D), lambda i: (i // nlat, 0, 0))
    return pl.pallas_call(
        body, name=name, out_shape=jax.ShapeDtypeStruct((R, D), BF16), grid=(R // tb,),
        in_specs=[BS((tb, D), lambda i: (i, 0)), BS((1, D), lambda i: (0, 0)), grp, grp],
        out_specs=BS((tb, D), lambda i: (i, 0)), compiler_params=_cp(("parallel",)),
    )(X, gain.reshape(1, D), sh, sc)


def _norm_mod_bwd(X, gain, sc, dh, dup, tb, nlat, name):
    R, D = X.shape
    G = sc.shape[0]

    def body(x_ref, g_ref, sc_ref, dh_ref, dup_ref, dx_ref, dg_ref, dsh_ref, dsc_ref):
        i = pl.program_id(0)
        x = x_ref[...]
        g = g_ref[...]
        r = lax.rsqrt(jnp.mean(x * x, axis=-1, keepdims=True) + EPS)
        xh = x * r
        dh_ = dh_ref[...].astype(F32)
        t = dh_ * (1.0 + sc_ref[0])
        tg = t * g
        dx_ref[...] = dup_ref[...] + r * (tg - xh * jnp.mean(tg * xh, axis=-1, keepdims=True))
        _accum(dg_ref, jnp.sum(t * xh, axis=0, keepdims=True), i == 0)
        first = i % nlat == 0
        _accum(dsh_ref, jnp.sum(dh_, axis=0, keepdims=True)[None], first)
        _accum(dsc_ref, jnp.sum(dh_ * xh * g, axis=0, keepdims=True)[None], first)

    row = BS((tb, D), lambda i: (i, 0))
    grp = BS((1, 1, D), lambda i: (i // nlat, 0, 0))
    return pl.pallas_call(
        body, name=name,
        out_shape=(jax.ShapeDtypeStruct((R, D), F32), jax.ShapeDtypeStruct((1, D), F32),
                   jax.ShapeDtypeStruct((G, 1, D), F32), jax.ShapeDtypeStruct((G, 1, D), F32)),
        grid=(R // tb,), in_specs=[row, BS((1, D), lambda i: (0, 0)), grp, row, row],
        out_specs=(row, BS((1, D), lambda i: (0, 0)), grp, grp), compiler_params=_cp(("arbitrary",)),
    )(X, gain.reshape(1, D), sc, dh, dup)


def _gate_res_fwd(X, y, gate, tb, nlat, name):
    R, D = X.shape

    def body(x_ref, y_ref, g_ref, o_ref):
        o_ref[...] = x_ref[...] + g_ref[0] * y_ref[...]

    row = BS((tb, D), lambda i: (i, 0))
    return pl.pallas_call(
        body, name=name, out_shape=jax.ShapeDtypeStruct((R, D), F32), grid=(R // tb,),
        in_specs=[row, row, BS((1, 1, D), lambda i: (i // nlat, 0, 0))], out_specs=row, compiler_params=_cp(("parallel",)),
    )(X, y, gate)


def _gate_bwd(dX, y, gate, tb, nlat, name):
    R, D = dX.shape
    G = gate.shape[0]

    def body(dx_ref, y_ref, g_ref, dy_ref, dg_ref):
        i = pl.program_id(0)
        dx = dx_ref[...]
        dy_ref[...] = (g_ref[0] * dx).astype(BF16)
        _accum(dg_ref, jnp.sum(dx * y_ref[...], axis=0, keepdims=True)[None], i % nlat == 0)

    row = BS((tb, D), lambda i: (i, 0))
    grp = BS((1, 1, D), lambda i: (i // nlat, 0, 0))
    return pl.pallas_call(
        body, name=name, out_shape=(jax.ShapeDtypeStruct((R, D), BF16), jax.ShapeDtypeStruct((G, 1, D), F32)),
        grid=(R // tb,), in_specs=[row, row, grp], out_specs=(row, grp), compiler_params=_cp(("arbitrary",)),
    )(dX, y, gate)


def _final_loss(X, gain, target, tb, name):
    R, D = X.shape

    def body(x_ref, g_ref, t_ref, loss_ref, dx_ref, dg_ref):
        i = pl.program_id(0)
        x = x_ref[...]
        g = g_ref[...]
        r = lax.rsqrt(jnp.mean(x * x, axis=-1, keepdims=True) + EPS)
        xh = x * r
        e = xh * g - t_ref[...]
        part = jnp.sum(jnp.sum(e * e, axis=1, keepdims=True), axis=0, keepdims=True) * (0.5 / D)
        _accum(loss_ref, jnp.broadcast_to(part, (1, LANES)), i == 0)
        dy = e * (1.0 / D)
        _accum(dg_ref, jnp.sum(dy * xh, axis=0, keepdims=True), i == 0)
        tg = dy * g
        dx_ref[...] = r * (tg - xh * jnp.mean(tg * xh, axis=-1, keepdims=True))

    row = BS((tb, D), lambda i: (i, 0))
    return pl.pallas_call(
        body, name=name,
        out_shape=(jax.ShapeDtypeStruct((1, LANES), F32), jax.ShapeDtypeStruct((R, D), F32), jax.ShapeDtypeStruct((1, D), F32)),
        grid=(R // tb,), in_specs=[row, BS((1, D), lambda i: (0, 0)), row],
        out_specs=(BS((1, LANES), lambda i: (0, 0)), row, BS((1, D), lambda i: (0, 0))), compiler_params=_cp(("arbitrary",)),
    )(X, gain.reshape(1, D), target)


def _ln_stats(v):
    mu = jnp.mean(v, axis=-1, keepdims=True)
    d = v - mu
    r = lax.rsqrt(jnp.mean(d * d, axis=-1, keepdims=True) + EPS)
    return d * r, r


def _gmlp_fwd(z, ln_g, ln_b, w_s, b_s_full, tb, name):
    R = z.shape[0]
    AW = ln_g.shape[-1]
    AH = w_s.shape[0]

    def body(zu_ref, zv_ref, g_ref, b_ref, ws_ref, bs_ref, o_ref):
        u, _ = _gelu(zu_ref[...].astype(F32))
        v, _ = _gelu(zv_ref[...].astype(F32))
        xh, _ = _ln_stats(v)
        vn = (xh * g_ref[...] + b_ref[...]).astype(BF16)
        for n in range(tb // CHUNK):
            rs = slice(n * CHUNK, (n + 1) * CHUNK)
            for h in range(AH):
                cs = slice(h * CHUNK, (h + 1) * CHUNK)
                v2 = jnp.dot(ws_ref[h].astype(BF16), vn[rs, cs], preferred_element_type=F32) + bs_ref[h]
                o_ref[rs, cs] = (u[rs, cs] * v2).astype(BF16)

    full3 = lambda s: BS(s, lambda i: (0, 0, 0))
    return pl.pallas_call(
        body, name=name, out_shape=jax.ShapeDtypeStruct((R, AW), BF16), grid=(R // tb,),
        in_specs=[BS((tb, AW), lambda i: (i, 0)), BS((tb, AW), lambda i: (i, 1)), BS((1, AW), lambda i: (0, 0)),
                  BS((1, AW), lambda i: (0, 0)), full3((AH, CHUNK, CHUNK)), full3((AH, CHUNK, CHUNK))],
        out_specs=BS((tb, AW), lambda i: (i, 0)), compiler_params=_cp(("parallel",)),
    )(z, z, ln_g.reshape(1, AW), ln_b.reshape(1, AW), w_s, b_s_full)


def _gmlp_bwd(z, dya, ln_g, ln_b, w_s, b_s_full, tb, name):
    R = z.shape[0]
    AW = ln_g.shape[-1]
    AH = w_s.shape[0]

    def body(zu_ref, zv_ref, dy_ref, g_ref, b_ref, ws_ref, bs_ref, dzu_ref, dzv_ref, dg_ref, db_ref, dws_ref, dbs_ref, cs_u_ref, cs_v_ref,
             dvn_scr):
        i = pl.program_id(0)
        first = i == 0
        zu = zu_ref[...].astype(F32)
        zv = zv_ref[...].astype(F32)
        u, thu = _gelu(zu)
        v, thv = _gelu(zv)
        xh, r = _ln_stats(v)
        g = g_ref[...]
        vn = (xh * g + b_ref[...]).astype(BF16)
        dy = dy_ref[...].astype(F32)

        @pl.when(first)
        def _():
            dws_ref[...] = jnp.zeros_like(dws_ref)
            dbs_ref[...] = jnp.zeros_like(dbs_ref)

        for n in range(tb // CHUNK):
            rs = slice(n * CHUNK, (n + 1) * CHUNK)
            for h in range(AH):
                cs = slice(h * CHUNK, (h + 1) * CHUNK)
                w = ws_ref[h].astype(BF16)
                v2 = jnp.dot(w, vn[rs, cs], preferred_element_type=F32) + bs_ref[h]
                dzu_ref[rs, cs] = (dy[rs, cs] * v2 * _gelu_grad(zu[rs, cs], thu[rs, cs])).astype(BF16)
                dv2 = dy[rs, cs] * u[rs, cs]
                dv2b = dv2.astype(BF16)
                dvn_scr[rs, cs] = lax.dot_general(w, dv2b, _DIMS["tn"], preferred_element_type=F32)
                dws_ref[h] += lax.dot_general(dv2b, vn[rs, cs], _DIMS["nt"], preferred_element_type=F32)
                dbs_ref[h] += jnp.sum(dv2, axis=1, keepdims=True)
        dvn = dvn_scr[...]
        _accum(dg_ref, jnp.sum(dvn * xh, axis=0, keepdims=True), first)
        _accum(db_ref, jnp.sum(dvn, axis=0, keepdims=True), first)
        t = dvn * g
        dv = r * (t - jnp.mean(t, axis=-1, keepdims=True) - xh * jnp.mean(t * xh, axis=-1, keepdims=True))
        dzv = dv * _gelu_grad(zv, thv)
        dzv_ref[...] = dzv.astype(BF16)
        _accum(cs_v_ref, jnp.sum(dzv, axis=0, keepdims=True), first)
        _accum(cs_u_ref, jnp.sum(dzu_ref[...].astype(F32), axis=0, keepdims=True), first)

    full3 = lambda s: BS(s, lambda i: (0, 0, 0))
    vec = BS((1, AW), lambda i: (0, 0))
    row = BS((tb, AW), lambda i: (i, 0))
    outs = pl.pallas_call(
        body, name=name,
        out_shape=(jax.ShapeDtypeStruct((R, AW), BF16), jax.ShapeDtypeStruct((R, AW), BF16), jax.ShapeDtypeStruct((1, AW), F32),
                   jax.ShapeDtypeStruct((1, AW), F32), jax.ShapeDtypeStruct((AH, CHUNK, CHUNK), F32),
                   jax.ShapeDtypeStruct((AH, CHUNK, 1), F32), jax.ShapeDtypeStruct((1, AW), F32), jax.ShapeDtypeStruct((1, AW), F32)),
        grid=(R // tb,),
        in_specs=[row, BS((tb, AW), lambda i: (i, 1)), row, vec, vec, full3((AH, CHUNK, CHUNK)), full3((AH, CHUNK, CHUNK))],
        out_specs=(row, row, vec, vec, full3((AH, CHUNK, CHUNK)), full3((AH, CHUNK, 1)), vec, vec),
        scratch_shapes=[pltpu.VMEM((tb, AW), F32)], compiler_params=_cp(("arbitrary",)),
    )(z, z, dya, ln_g.reshape(1, AW), ln_b.reshape(1, AW), w_s, b_s_full)
    return outs


def _segments(R, L):
    return [(0, L)] + ([(L, R - L)] if R > L else [])


def _scr_rows(R, L):
    return R + CONV_PAD * (len(_segments(R, L)) + 1)


def _scr_off(s, start):
    return CONV_PAD * (s + 1) + start


def _zero_pads(scr, R, L):
    segs = _segments(R, L)
    z = jnp.zeros((CONV_PAD, scr.shape[1]), F32)
    for s, (start, n) in enumerate(segs):
        scr[pl.ds(_scr_off(s, start) - CONV_PAD, CONV_PAD), :] = z
    last_s, (last_start, last_n) = len(segs) - 1, segs[-1]
    scr[pl.ds(_scr_off(last_s, last_start) + last_n, CONV_PAD), :] = z


def _for_chunks(R, L, ch, fn):
    for s, (start, n) in enumerate(_segments(R, L)):
        c = min(ch, n)
        off = _scr_off(s, start)

        def step(i, carry, start=start, off=off, c=c):
            r0 = pl.multiple_of(start + i * c, SUBLANES)
            fn(r0, pl.multiple_of(off + i * c, SUBLANES), c)
            return carry

        lax.fori_loop(0, n // c, step, 0)


def _taps(scr, srow, c, w_ref, ntap, flip):
    win = scr[pl.ds(srow - CONV_PAD, c + 2 * CONV_PAD), :]
    n = c + 2 * CONV_PAD
    acc = None
    for k in range(ntap):
        o = k - (ntap - 1) // 2
        if flip:
            o = -o
        sh = win if o == 0 else pltpu.roll(win, (-o) % n, 0)
        term = w_ref[k:k + 1, :] * sh[CONV_PAD:CONV_PAD + c]
        acc = term if acc is None else acc + term
    return acc


def _tap_grads(scr, srow, c, dy, dw_ref, ntap):
    win = scr[pl.ds(srow - CONV_PAD, c + 2 * CONV_PAD), :]
    n = c + 2 * CONV_PAD
    for k in range(ntap):
        o = k - (ntap - 1) // 2
        sh = win if o == 0 else pltpu.roll(win, (-o) % n, 0)
        dw_ref[k:k + 1, :] += jnp.sum(dy * sh[CONV_PAD:CONV_PAD + c], axis=0, keepdims=True)


def _glu_conv_fwd(z, col0, conv_w, conv_b, L, ch, name):
    R = z.shape[0]
    BW = conv_w.shape[1]
    nb, c0 = BW // LANES, col0 // LANES

    def body(a_ref, g_ref, w_ref, b_ref, o_ref, scr):
        _zero_pads(scr, R, L)

        def fill(r0, s0, c):
            a = a_ref[pl.ds(r0, c), :].astype(F32)
            g = g_ref[pl.ds(r0, c), :].astype(F32)
            scr[pl.ds(s0, c), :] = a * _sigmoid(g)

        _for_chunks(R, L, ch, fill)

        def conv(r0, s0, c):
            o_ref[pl.ds(r0, c), :] = _taps(scr, s0, c, w_ref, B_CONV, False) + b_ref[...]

        _for_chunks(R, L, ch, conv)

    return pl.pallas_call(
        body, name=name, out_shape=jax.ShapeDtypeStruct((R, BW), F32), grid=(nb,),
        in_specs=[BS((R, LANES), lambda j: (0, c0 + j)), BS((R, LANES), lambda j: (0, c0 + nb + j)),
                  BS((B_CONV, LANES), lambda j: (0, j)), BS((1, LANES), lambda j: (0, j))],
        out_specs=BS((R, LANES), lambda j: (0, j)), scratch_shapes=[pltpu.VMEM((_scr_rows(R, L), LANES), F32)],
        compiler_params=_cp(("parallel",)),
    )(z, z, conv_w, conv_b.reshape(1, BW))


def _glu_conv_bwd(z, col0, dhc, conv_w, L, ch, name):
    R = z.shape[0]
    BW = conv_w.shape[1]
    nb, c0 = BW // LANES, col0 // LANES

    def body(a_ref, g_ref, dy_ref, w_ref, da_ref, dg_ref, dw_ref, db_ref, csa_ref, csg_ref, scr_h, scr_dy):
        _zero_pads(scr_h, R, L)
        _zero_pads(scr_dy, R, L)
        dw_ref[...] = jnp.zeros_like(dw_ref)
        db_ref[...] = jnp.zeros_like(db_ref)
        csa_ref[...] = jnp.zeros_like(csa_ref)
        csg_ref[...] = jnp.zeros_like(csg_ref)

        def fill(r0, s0, c):
            a = a_ref[pl.ds(r0, c), :].astype(F32)
            g = g_ref[pl.ds(r0, c), :].astype(F32)
            scr_h[pl.ds(s0, c), :] = a * _sigmoid(g)
            scr_dy[pl.ds(s0, c), :] = dy_ref[pl.ds(r0, c), :]

        _for_chunks(R, L, ch, fill)

        def back(r0, s0, c):
            dh = _taps(scr_dy, s0, c, w_ref, B_CONV, True)
            a = a_ref[pl.ds(r0, c), :].astype(F32)
            sg = _sigmoid(g_ref[pl.ds(r0, c), :].astype(F32))
            da = dh * sg
            dg = dh * a * sg * (1.0 - sg)
            da_ref[pl.ds(r0, c), :] = da.astype(BF16)
            dg_ref[pl.ds(r0, c), :] = dg.astype(BF16)
            csa_ref[...] += jnp.sum(da, axis=0, keepdims=True)
            csg_ref[...] += jnp.sum(dg, axis=0, keepdims=True)
            dy = dy_ref[pl.ds(r0, c), :]
            db_ref[...] += jnp.sum(dy, axis=0, keepdims=True)
            _tap_grads(scr_h, s0, c, dy, dw_ref, B_CONV)

        _for_chunks(R, L, ch, back)

    col = BS((R, LANES), lambda j: (0, j))
    vec = BS((1, LANES), lambda j: (0, j))
    nrow = _scr_rows(R, L)
    return pl.pallas_call(
        body, name=name,
        out_shape=(jax.ShapeDtypeStruct((R, BW), BF16), jax.ShapeDtypeStruct((R, BW), BF16), jax.ShapeDtypeStruct((B_CONV, BW), F32),
                   jax.ShapeDtypeStruct((1, BW), F32), jax.ShapeDtypeStruct((1, BW), F32), jax.ShapeDtypeStruct((1, BW), F32)),
        grid=(nb,),
        in_specs=[BS((R, LANES), lambda j: (0, c0 + j)), BS((R, LANES), lambda j: (0, c0 + nb + j)), col,
                  BS((B_CONV, LANES), lambda j: (0, j))],
        out_specs=(col, col, BS((B_CONV, LANES), lambda j: (0, j)), vec, vec, vec),
        scratch_shapes=[pltpu.VMEM((nrow, LANES), F32), pltpu.VMEM((nrow, LANES), F32)], compiler_params=_cp(("parallel",)),
    )(z, z, dhc, conv_w)


def _ln_silu_fwd(hc, ln_g, ln_b, tb, name):
    R, W = hc.shape

    def body(x_ref, g_ref, b_ref, o_ref):
        xh, _ = _ln_stats(x_ref[...])
        y = xh * g_ref[...] + b_ref[...]
        o_ref[...] = (y * _sigmoid(y)).astype(BF16)

    vec = BS((1, W), lambda i: (0, 0))
    row = BS((tb, W), lambda i: (i, 0))
    return pl.pallas_call(
        body, name=name, out_shape=jax.ShapeDtypeStruct((R, W), BF16), grid=(R // tb,), in_specs=[row, vec, vec], out_specs=row,
        compiler_params=_cp(("parallel",)),
    )(hc, ln_g.reshape(1, W), ln_b.reshape(1, W))


def _ln_silu_bwd(hc, dyb, col0, ln_g, ln_b, tb, name):
    R, W = hc.shape
    c0 = col0 // W

    def body(x_ref, dy_ref, g_ref, b_ref, dx_ref, dg_ref, db_ref):
        i = pl.program_id(0)
        xh, r = _ln_stats(x_ref[...])
        g = g_ref[...]
        y = xh * g + b_ref[...]
        s = _sigmoid(y)
        dy = dy_ref[...].astype(F32) * s * (1.0 + y * (1.0 - s))
        _accum(dg_ref, jnp.sum(dy * xh, axis=0, keepdims=True), i == 0)
        _accum(db_ref, jnp.sum(dy, axis=0, keepdims=True), i == 0)
        t = dy * g
        dx_ref[...] = r * (t - jnp.mean(t, axis=-1, keepdims=True) - xh * jnp.mean(t * xh, axis=-1, keepdims=True))

    vec = BS((1, W), lambda i: (0, 0))
    row = BS((tb, W), lambda i: (i, 0))
    return pl.pallas_call(
        body, name=name,
        out_shape=(jax.ShapeDtypeStruct((R, W), F32), jax.ShapeDtypeStruct((1, W), F32), jax.ShapeDtypeStruct((1, W), F32)),
        grid=(R // tb,), in_specs=[row, BS((tb, W), lambda i: (i, c0)), vec, vec], out_specs=(row, vec, vec),
        compiler_params=_cp(("arbitrary",)),
    )(hc, dyb, ln_g.reshape(1, W), ln_b.reshape(1, W))


def _ffn_act_fwd(zf, conv_w, conv_b, L, ch, name):
    R = zf.shape[0]
    DFF = conv_w.shape[1]
    nb = DFF // LANES

    def body(g_ref, u_ref, w_ref, b_ref, o_ref, scr):
        _zero_pads(scr, R, L)

        def fill(r0, s0, c):
            scr[pl.ds(s0, c), :] = g_ref[pl.ds(r0, c), :].astype(F32)

        _for_chunks(R, L, ch, fill)

        def act(r0, s0, c):
            gc = _taps(scr, s0, c, w_ref, FFN_CONV, False) + b_ref[...]
            o_ref[pl.ds(r0, c), :] = (gc * _sigmoid(gc) * u_ref[pl.ds(r0, c), :].astype(F32)).astype(BF16)

        _for_chunks(R, L, ch, act)

    return pl.pallas_call(
        body, name=name, out_shape=jax.ShapeDtypeStruct((R, DFF), BF16), grid=(nb,),
        in_specs=[BS((R, LANES), lambda j: (0, j)), BS((R, LANES), lambda j: (0, nb + j)), BS((FFN_CONV, LANES), lambda j: (0, j)),
                  BS((1, LANES), lambda j: (0, j))],
        out_specs=BS((R, LANES), lambda j: (0, j)), scratch_shapes=[pltpu.VMEM((_scr_rows(R, L), LANES), F32)],
        compiler_params=_cp(("parallel",)),
    )(zf, zf, conv_w, conv_b.reshape(1, DFF))


def _ffn_act_bwd(zf, df, conv_w, conv_b, L, ch, name):
    R = zf.shape[0]
    DFF = conv_w.shape[1]
    nb = DFF // LANES

    def body(g_ref, u_ref, df_ref, w_ref, b_ref, dg_ref, du_ref, dw_ref, db_ref, scr_g, scr_d):
        _zero_pads(scr_g, R, L)
        _zero_pads(scr_d, R, L)
        dw_ref[...] = jnp.zeros_like(dw_ref)
        db_ref[...] = jnp.zeros_like(db_ref)

        def fill(r0, s0, c):
            scr_g[pl.ds(s0, c), :] = g_ref[pl.ds(r0, c), :].astype(F32)

        _for_chunks(R, L, ch, fill)

        def pre(r0, s0, c):
            gc = _taps(scr_g, s0, c, w_ref, FFN_CONV, False) + b_ref[...]
            s = _sigmoid(gc)
            d = df_ref[pl.ds(r0, c), :].astype(F32)
            du_ref[pl.ds(r0, c), :] = (d * gc * s).astype(BF16)
            dgc = d * u_ref[pl.ds(r0, c), :].astype(F32) * s * (1.0 + gc * (1.0 - s))
            scr_d[pl.ds(s0, c), :] = dgc
            db_ref[...] += jnp.sum(dgc, axis=0, keepdims=True)
            _tap_grads(scr_g, s0, c, dgc, dw_ref, FFN_CONV)

        _for_chunks(R, L, ch, pre)

        def back(r0, s0, c):
            dg_ref[pl.ds(r0, c), :] = _taps(scr_d, s0, c, w_ref, FFN_CONV, True).astype(BF16)

        _for_chunks(R, L, ch, back)

    col = BS((R, LANES), lambda j: (0, j))
    vec = BS((1, LANES), lambda j: (0, j))
    nrow = _scr_rows(R, L)
    return pl.pallas_call(
        body, name=name,
        out_shape=(jax.ShapeDtypeStruct((R, DFF), BF16), jax.ShapeDtypeStruct((R, DFF), BF16), jax.ShapeDtypeStruct((FFN_CONV, DFF), F32),
                   jax.ShapeDtypeStruct((1, DFF), F32)),
        grid=(nb,),
        in_specs=[col, BS((R, LANES), lambda j: (0, nb + j)), col, BS((FFN_CONV, LANES), lambda j: (0, j)), vec],
        out_specs=(col, col, BS((FFN_CONV, LANES), lambda j: (0, j)), vec),
        scratch_shapes=[pltpu.VMEM((nrow, LANES), F32), pltpu.VMEM((nrow, LANES), F32)], compiler_params=_cp(("parallel",)),
    )(zf, zf, df, conv_w, conv_b.reshape(1, DFF))


def _rope_tables(L, T):
    rows = L // GRID_W
    row = jnp.repeat(jnp.arange(rows, dtype=F32), GRID_W)
    col = jnp.tile(jnp.arange(GRID_W, dtype=F32), rows)
    n_freq = ROPE // 4
    inv = ROPE_THETA ** (-jnp.arange(n_freq, dtype=F32) / n_freq)
    ang = jnp.concatenate([row[:, None] * inv, col[:, None] * inv], axis=-1)
    cos, sin = jnp.cos(ang), jnp.sin(ang)
    half = ROPE // 2
    zero = jnp.zeros((L, half), F32)
    cos_t = jnp.concatenate([cos, cos, jnp.ones((L, LANES - ROPE), F32)], axis=1)
    sa = jnp.concatenate([zero, sin, zero, zero], axis=1)
    sb = jnp.concatenate([-sin, zero, zero, zero], axis=1)
    pad = T - L
    cos_t = jnp.concatenate([cos_t, jnp.ones((pad, LANES), F32)], axis=0)
    sa = jnp.concatenate([sa, jnp.zeros((pad, LANES), F32)], axis=0)
    sb = jnp.concatenate([sb, jnp.zeros((pad, LANES), F32)], axis=0)
    return cos_t, sa, sb


def _rope(x, cos, sa, sb):
    half = ROPE // 2
    return x * cos + pltpu.roll(x, half, 1) * sa + pltpu.roll(x, LANES - half, 1) * sb


def _rope_t(d, cos, sa, sb):
    half = ROPE // 2
    return d * cos + pltpu.roll(d * sa, LANES - half, 1) + pltpu.roll(d * sb, half, 1)


def _mla_prep_fwd(zm, qg, kvg, tabs, tb, name):
    T, W = zm.shape
    QL, KVL = qg.shape[-1], kvg.shape[-1]

    def body(z_ref, qg_ref, kg_ref, cos_ref, sa_ref, sb_ref, q_ref, k_ref, p_ref):
        cq = z_ref[:, :QL]
        r = lax.rsqrt(jnp.mean(cq * cq, axis=-1, keepdims=True) + EPS)
        q_ref[...] = (cq * r * qg_ref[...]).astype(BF16)
        ck = z_ref[:, QL:QL + KVL]
        r = lax.rsqrt(jnp.mean(ck * ck, axis=-1, keepdims=True) + EPS)
        k_ref[...] = (ck * r * kg_ref[...]).astype(BF16)
        p_ref[...] = _rope(z_ref[:, QL + KVL:], cos_ref[...], sa_ref[...], sb_ref[...]).astype(BF16)

    tab = BS((tb, LANES), lambda i: (i, 0))
    return pl.pallas_call(
        body, name=name,
        out_shape=(jax.ShapeDtypeStruct((T, QL), BF16), jax.ShapeDtypeStruct((T, KVL), BF16), jax.ShapeDtypeStruct((T, LANES), BF16)),
        grid=(T // tb,),
        in_specs=[BS((tb, W), lambda i: (i, 0)), BS((1, QL), lambda i: (0, 0)), BS((1, KVL), lambda i: (0, 0)), tab, tab, tab],
        out_specs=(BS((tb, QL), lambda i: (i, 0)), BS((tb, KVL), lambda i: (i, 0)), tab), compiler_params=_cp(("parallel",)),
    )(zm, qg.reshape(1, QL), kvg.reshape(1, KVL), *tabs)


def _mla_prep_bwd(zm, qg, kvg, tabs, dq, dk, dp, tb, name):
    T, W = zm.shape
    QL, KVL = qg.shape[-1], kvg.shape[-1]

    def body(z_ref, qg_ref, kg_ref, cos_ref, sa_ref, sb_ref, dq_ref, dk_ref, dp_ref, dz_ref, dqg_ref, dkg_ref):
        i = pl.program_id(0)

        def rms_bwd(x, g, dy):
            r = lax.rsqrt(jnp.mean(x * x, axis=-1, keepdims=True) + EPS)
            xh = x * r
            t = dy * g
            return r * (t - xh * jnp.mean(t * xh, axis=-1, keepdims=True)), jnp.sum(dy * xh, axis=0, keepdims=True)

        dcq, dg = rms_bwd(z_ref[:, :QL], qg_ref[...], dq_ref[...].astype(F32))
        dz_ref[:, :QL] = dcq.astype(BF16)
        _accum(dqg_ref, dg, i == 0)
        dck, dg = rms_bwd(z_ref[:, QL:QL + KVL], kg_ref[...], dk_ref[...].astype(F32))
        dz_ref[:, QL:QL + KVL] = dck.astype(BF16)
        _accum(dkg_ref, dg, i == 0)
        dz_ref[:, QL + KVL:] = _rope_t(dp_ref[...], cos_ref[...], sa_ref[...], sb_ref[...]).astype(BF16)

    tab = BS((tb, LANES), lambda i: (i, 0))
    return pl.pallas_call(
        body, name=name,
        out_shape=(jax.ShapeDtypeStruct((T, W), BF16), jax.ShapeDtypeStruct((1, QL), F32), jax.ShapeDtypeStruct((1, KVL), F32)),
        grid=(T // tb,),
        in_specs=[BS((tb, W), lambda i: (i, 0)), BS((1, QL), lambda i: (0, 0)), BS((1, KVL), lambda i: (0, 0)), tab, tab, tab,
                  BS((tb, QL), lambda i: (i, 0)), BS((tb, KVL), lambda i: (i, 0)), tab],
        out_specs=(BS((tb, W), lambda i: (i, 0)), BS((1, QL), lambda i: (0, 0)), BS((1, KVL), lambda i: (0, 0))),
        compiler_params=_cp(("arbitrary",)),
    )(zm, qg.reshape(1, QL), kvg.reshape(1, KVL), *tabs, dq, dk, dp)


def _attn_fwd(q, kv, kpe, tabs, L, tq, name, side=None):
    T = kv.shape[0]
    H = kv.shape[1] // QHEAD
    scale = (HEAD + ROPE) ** -0.5

    def body(q_ref, kv_ref, kpe_ref, cos_ref, sa_ref, sb_ref, o_ref, lse_ref, kcat):
        @pl.when(pl.program_id(1) == 0)
        def _():
            kcat[:, :HEAD] = kv_ref[:, :HEAD]
            kcat[:, HEAD:] = kpe_ref[...]

        qp = _rope(q_ref[:, HEAD:].astype(F32), cos_ref[...], sa_ref[...], sb_ref[...]).astype(BF16)
        qc = jnp.concatenate([q_ref[:, :HEAD], qp], axis=1)
        s = lax.dot_general(qc, kcat[...], _DIMS["nt"], preferred_element_type=F32) * scale
        m = jnp.max(s, axis=-1, keepdims=True)
        p = jnp.exp(s - m)
        l = jnp.sum(p, axis=-1, keepdims=True)
        o = jnp.dot(p.astype(BF16), kv_ref[:, HEAD:], preferred_element_type=F32)
        o_ref[...] = (o / l).astype(BF16)
        lse_ref[0] = m + jnp.log(l)

    tab = BS((tq, LANES), lambda h, i: (i, 0))
    return _call(
        body, name=name, out_shape=(jax.ShapeDtypeStruct((L, H * HEAD), BF16), jax.ShapeDtypeStruct((H, L, 1), F32)),
        grid=(H, L // tq),
        in_specs=[BS((tq, QHEAD), lambda h, i: (i, h)), BS((T, QHEAD), lambda h, i: (0, h)), BS((T, LANES), lambda h, i: (0, 0)),
                  tab, tab, tab],
        out_specs=(BS((tq, HEAD), lambda h, i: (i, h)), BS((1, tq, 1), lambda h, i: (h, i, 0))),
        scratch=[pltpu.VMEM((T, QHEAD), BF16)], sem=("parallel", "arbitrary"), operands=(q, kv, kpe, *tabs), side=side)


def _attn_bwd(q, kv, kpe, tabs, o, lse, do, L, tq, name, side=None):
    T = kv.shape[0]
    H = kv.shape[1] // QHEAD
    scale = (HEAD + ROPE) ** -0.5
    nq = L // tq

    def body(q_ref, kv_ref, kpe_ref, cos_ref, sa_ref, sb_ref, o_ref, lse_ref, do_ref, dq_ref, dkv_ref, dkpe_ref, kcat, dk_acc, dv_acc):
        h, i = pl.program_id(0), pl.program_id(1)

        @pl.when(i == 0)
        def _():
            kcat[:, :HEAD] = kv_ref[:, :HEAD]
            kcat[:, HEAD:] = kpe_ref[...]
            dk_acc[...] = jnp.zeros_like(dk_acc)
            dv_acc[...] = jnp.zeros_like(dv_acc)

        cos, sa, sb = cos_ref[...], sa_ref[...], sb_ref[...]
        qp = _rope(q_ref[:, HEAD:].astype(F32), cos, sa, sb).astype(BF16)
        qc = jnp.concatenate([q_ref[:, :HEAD], qp], axis=1)
        s = lax.dot_general(qc, kcat[...], _DIMS["nt"], preferred_element_type=F32) * scale
        p = jnp.exp(s - lse_ref[0])
        dov = do_ref[...]
        delta = jnp.sum(dov.astype(F32) * o_ref[...].astype(F32), axis=-1, keepdims=True)
        dp = lax.dot_general(dov, kv_ref[:, HEAD:], _DIMS["nt"], preferred_element_type=F32)
        ds = (p * (dp - delta) * scale).astype(BF16)
        dqc = jnp.dot(ds, kcat[...], preferred_element_type=F32)
        dq_ref[:, :HEAD] = dqc[:, :HEAD].astype(BF16)
        dq_ref[:, HEAD:] = _rope_t(dqc[:, HEAD:], cos, sa, sb).astype(BF16)
        dk_acc[...] += lax.dot_general(ds, qc, _DIMS["tn"], preferred_element_type=F32)
        dv_acc[...] += lax.dot_general(p.astype(BF16), dov, _DIMS["tn"], preferred_element_type=F32)

        @pl.when(i == nq - 1)
        def _():
            dkv_ref[:, :HEAD] = dk_acc[:, :HEAD].astype(BF16)
            dkv_ref[:, HEAD:] = dv_acc[...].astype(BF16)

            @pl.when(h == 0)
            def _():
                dkpe_ref[...] = dk_acc[:, HEAD:]

            @pl.when(h > 0)
            def _():
                dkpe_ref[...] += dk_acc[:, HEAD:]

    tab = BS((tq, LANES), lambda h, i: (i, 0))
    return _call(
        body, name=name,
        out_shape=(jax.ShapeDtypeStruct((L, H * QHEAD), BF16), jax.ShapeDtypeStruct((T, H * QHEAD), BF16), jax.ShapeDtypeStruct((T, LANES), F32)),
        grid=(H, nq),
        in_specs=[BS((tq, QHEAD), lambda h, i: (i, h)), BS((T, QHEAD), lambda h, i: (0, h)), BS((T, LANES), lambda h, i: (0, 0)),
                  tab, tab, tab, BS((tq, HEAD), lambda h, i: (i, h)), BS((1, tq, 1), lambda h, i: (h, i, 0)),
                  BS((tq, HEAD), lambda h, i: (i, h))],
        out_specs=(BS((tq, QHEAD), lambda h, i: (i, h)), BS((T, QHEAD), lambda h, i: (0, h)), BS((T, LANES), lambda h, i: (0, 0))),
        scratch=[pltpu.VMEM((T, QHEAD), BF16), pltpu.VMEM((T, QHEAD), F32), pltpu.VMEM((T, HEAD), F32)],
        sem=("arbitrary", "arbitrary"), operands=(q, kv, kpe, *tabs, o, lse, do), side=side)


def _adamw(w, g, m, v, name):
    R, C = w.shape
    tr = _row_tile(R, C)
    c1 = 1.0 / (1.0 - ADAM_B1 ** ADAM_STEP)
    c2 = 1.0 / (1.0 - ADAM_B2 ** ADAM_STEP)

    def body(w_ref, g_ref, m_ref, v_ref, d_ref, nm_ref, nv_ref):
        g_ = g_ref[...]
        nm = ADAM_B1 * m_ref[...] + (1.0 - ADAM_B1) * g_
        nv = ADAM_B2 * v_ref[...] + (1.0 - ADAM_B2) * (g_ * g_)
        nm_ref[...] = nm
        nv_ref[...] = nv
        d_ref[...] = -ADAM_LR * ((nm * c1) / (jnp.sqrt(nv * c2) + ADAM_EPS) + ADAM_WD * w_ref[...])

    blk = BS((tr, C), lambda i: (i, 0))
    sd = jax.ShapeDtypeStruct((R, C), F32)
    return pl.pallas_call(body, name=name, out_shape=(sd, sd, sd), grid=(R // tr,), in_specs=[blk] * 4, out_specs=(blk,) * 3,
                          compiler_params=_cp(("parallel",)))(w, g, m, v)


def _sum_lead(a, out_dtype, name):
    n, R, C = a.shape
    tr = _row_tile(R, C * n, 2 << 20)

    def body(a_ref, o_ref):
        acc = a_ref[0].astype(F32)
        for k in range(1, n):
            acc = acc + a_ref[k].astype(F32)
        o_ref[...] = acc.astype(out_dtype)

    return pl.pallas_call(body, name=name, out_shape=jax.ShapeDtypeStruct((R, C), out_dtype), grid=(R // tr,),
                          in_specs=[BS((n, tr, C), lambda i: (0, i, 0))], out_specs=BS((tr, C), lambda i: (i, 0)),
                          compiler_params=_cp(("parallel",)))(a)


def _cctx_grad(parts, c_ctx, name):
    n, D = parts.shape

    def body(p_ref, c_ref, o_ref):
        d = jnp.sum(p_ref[...], axis=0, keepdims=True)
        c = c_ref[...]
        s = _sigmoid(c)
        o_ref[...] = d * s * (1.0 + c * (1.0 - s))

    return pl.pallas_call(body, name=name, out_shape=jax.ShapeDtypeStruct((1, D), F32))(parts, c_ctx.reshape(1, D))


def _me():
    return lax.axis_index("x"), lax.axis_index("y"), lax.axis_index("c")


def _aligned(v, n):
    return v if isinstance(v, int) else pl.multiple_of(v, n)


def _window(ref, r0, c0, R, C):
    rows = pl.ds(_aligned(r0, SUBLANES), R)
    if C == ref.shape[1]:
        return ref.at[rows, :]
    return ref.at[rows, pl.ds(_aligned(c0, LANES), C)]


def _allgather8(items, name):
    n = len(items)

    def body(*refs):
        srcs, dsts = refs[:n], refs[n:2 * n]
        send_sems, recv_sems, local_sems = refs[2 * n:]
        x, y, c = _me()
        me, sibling = (x, y, c), (x, y, 1 - c)
        chips = [(1 - x, y), (x, 1 - y), (1 - x, 1 - y)]

        def dwin(a, blk):
            (R, C), at = items[a][2], items[a][4]
            return _window(dsts[a], *at(*blk), R, C)

        def swin(a):
            (R, C), at = items[a][2], items[a][1]
            return _window(srcs[a], *at(*me), R, C)

        def copy(a, k, blk, to, src=None):
            return pltpu.make_async_remote_copy(
                src_ref=dwin(a, blk) if src is None else src, dst_ref=dwin(a, blk), send_sem=send_sems.at[7 * a + k],
                recv_sem=recv_sems.at[7 * a + k], device_id=to, device_id_type=MESH)

        mine = [pltpu.make_async_copy(swin(a), dwin(a, me), local_sems.at[a]) for a in range(n)]
        for cp in mine:
            cp.start()
        first = []
        for a in range(n):
            first.append(copy(a, 0, me, sibling, src=swin(a)))
            first += [copy(a, 1 + j, me, (*chip, c), src=swin(a)) for j, chip in enumerate(chips)]
        for cp in first:
            cp.start()
        passed = []
        for j, chip in enumerate(chips):
            for a in range(n):
                copy(a, 1 + j, (*chip, c), me).wait_recv()
                fwd = copy(a, 4 + j, (*chip, c), sibling)
                fwd.start()
                passed.append(fwd)
        for a in range(n):
            copy(a, 0, sibling, me).wait_recv()
            for j, chip in enumerate(chips):
                copy(a, 4 + j, (*chip, 1 - c), me).wait_recv()
        for cp in first + passed:
            cp.wait_send()
        for cp in mine:
            cp.wait()

    outs = pl.pallas_call(
        body, name=name, out_shape=tuple(jax.ShapeDtypeStruct(it[3], it[0].dtype) for it in items),
        in_specs=[ANY] * n, out_specs=tuple([ANY] * n),
        scratch_shapes=[pltpu.SemaphoreType.DMA((7 * n,)), pltpu.SemaphoreType.DMA((7 * n,)), pltpu.SemaphoreType.DMA((n,))],
    )(*[it[0] for it in items])
    return list(outs)


def _cast_place(w, kind, name):
    R2, C = w.shape
    tr = _row_tile(R2, C)
    nr = R2 // tr
    xi, yi, _ = _me()
    p_arr = (2 * xi + yi).astype(jnp.int32).reshape(1)
    if kind == "col":
        shape, o_spec = (R2, 4 * C), BS((tr, C), lambda i, p: (i, p[0]))
    else:
        shape, o_spec = (4 * R2, C), BS((tr, C), lambda i, p: (p[0] * nr + i, 0))

    def body(p_ref, w_ref, o_ref):
        o_ref[...] = w_ref[...].astype(BF16)

    return pl.pallas_call(
        body, name=name, out_shape=jax.ShapeDtypeStruct(shape, BF16),
        grid_spec=pltpu.PrefetchScalarGridSpec(num_scalar_prefetch=1, grid=(nr,), in_specs=[BS((tr, C), lambda i, p: (i, 0))],
                                               out_specs=o_spec),
        compiler_params=_cp(("parallel",)),
    )(p_arr, w)


def _remote(src, dst, sends, recvs, k, to):
    return pltpu.make_async_remote_copy(src_ref=src, dst_ref=dst, send_sem=sends.at[k], recv_sem=recvs.at[k], device_id=to,
                                        device_id_type=MESH)


def _gather_side(bufs, pieces):
    def win(b, piece, p, pc):
        bi, (R, C), at, k, i = piece
        r0, c0 = at(p, pc)
        return _window(b[bi], r0 + i * (R // k), c0, R // k, C)

    def chips_of(x, y):
        return [(1 - x, y), (x, 1 - y), (1 - x, 1 - y)]

    def outgoing(b, sends, recvs):
        x, y, c = _me()
        cps = []
        for m, piece in enumerate(pieces):
            mine = win(b, piece, 2 * x + y, c)
            cps += [_remote(mine, mine, sends, recvs, 6 * m + j, (px, py, c)) for j, (px, py) in enumerate(chips_of(x, y))]
        return cps

    def start(b, sends, recvs):
        for cp in outgoing(b, sends, recvs):
            cp.start()

    def finish(b, sends, recvs):
        x, y, c = _me()
        sibling = (x, y, 1 - c)
        passed = []
        for j, (px, py) in enumerate(chips_of(x, y)):
            for m, piece in enumerate(pieces):
                landed = win(b, piece, 2 * px + py, c)
                _remote(landed, landed, sends, recvs, 6 * m + j, (px, py, c)).wait_recv()
                fwd = _remote(landed, landed, sends, recvs, 6 * m + 3 + j, sibling)
                fwd.start()
                passed.append(fwd)
        for j, (px, py) in enumerate(chips_of(x, y)):
            for m, piece in enumerate(pieces):
                theirs = win(b, piece, 2 * px + py, 1 - c)
                _remote(theirs, theirs, sends, recvs, 6 * m + 3 + j, sibling).wait_recv()
        for cp in outgoing(b, sends, recvs) + passed:
            cp.wait_send()

    return _Side(bufs, 6 * len(pieces), start, finish)


_RELS = [(dx, dy, dc) for dx in (0, 1) for dy in (0, 1) for dc in (0, 1)][1:]


def _rs_side(bufs, pieces):
    def flip(v, d):
        return 1 - v if d else v

    def copies(b, sends, recvs):
        x, y, c = _me()
        dev = 4 * x + 2 * y + c
        cps, lands = [], []
        for m, (gi, ri, (R, C), at, k, i) in enumerate(pieces):
            rows = R // k
            for t, (dx, dy, dc) in enumerate(_RELS):
                tx, ty, tc = flip(x, dx), flip(y, dy), flip(c, dc)
                r0, c0 = at(2 * tx + ty, tc)
                src = _window(b[gi], r0 + i * rows, c0, rows, C)
                cps.append(_remote(src, b[ri].at[dev, pl.ds(i * rows, rows), :], sends, recvs, 7 * m + t, (tx, ty, tc)))
                theirs = b[ri].at[4 * tx + 2 * ty + tc, pl.ds(i * rows, rows), :]
                lands.append(_remote(theirs, theirs, sends, recvs, 7 * m + t, (tx, ty, tc)))
        return cps, lands

    def start(b, sends, recvs):
        for cp in copies(b, sends, recvs)[0]:
            cp.start()

    def finish(b, sends, recvs):
        cps, lands = copies(b, sends, recvs)
        for cp in lands:
            cp.wait_recv()
        for cp in cps:
            cp.wait_send()

    return _Side(bufs, 7 * len(pieces), start, finish)


def _comm_only(side, name):
    n = len(side.bufs)

    def body(*refs):
        bufs, (sends, recvs) = refs[n:2 * n], refs[2 * n:]
        side.start(bufs, sends, recvs)
        side.finish(bufs, sends, recvs)

    outs = pl.pallas_call(
        body, name=name, out_shape=tuple(jax.ShapeDtypeStruct(b.shape, b.dtype) for b in side.bufs),
        in_specs=[ANY] * n, out_specs=tuple([ANY] * n), input_output_aliases={a: a for a in range(n)},
        scratch_shapes=[pltpu.SemaphoreType.DMA((side.nsem,)), pltpu.SemaphoreType.DMA((side.nsem,))],
    )(*side.bufs)
    return list(outs)


def _rs_sum8(grad, recv, win, kind, name):
    R, C = win
    tr = _row_tile(R, 4 * C)
    nr = R // tr
    xi, yi, ci = _me()
    s_arr = jnp.stack([4 * xi + 2 * yi + ci, ci]).astype(jnp.int32)

    if kind == "col":
        g_spec = BS((tr, C), lambda i, s: (s[1] * nr + i, s[0] // 2))
    else:
        g_spec = BS((tr, C), lambda i, s: (s[0] * nr + i, 0))

    def body(s_ref, g_ref, *refs):
        acc = g_ref[...].astype(F32)
        for r_ref in refs[:7]:
            acc = acc + r_ref[0].astype(F32)
        refs[7][...] = acc

    def other(t):
        return BS((1, tr, C), lambda i, s: (jnp.bitwise_xor(s[0], t), i, 0))

    return pl.pallas_call(
        body, name=name, out_shape=jax.ShapeDtypeStruct((2 * R, C), F32),
        grid_spec=pltpu.PrefetchScalarGridSpec(
            num_scalar_prefetch=1, grid=(nr,), in_specs=[g_spec] + [other(t) for t in range(1, 8)],
            out_specs=BS((tr, C), lambda i, s: (s[1] * nr + i, 0))),
        compiler_params=_cp(("parallel",)),
    )(s_arr, grad, *([recv] * 7))


def _rs_share(shards, name):
    n = len(shards)

    def body(*refs):
        bufs = refs[n:2 * n]
        send_sems, recv_sems = refs[2 * n:]
        x, y, c = _me()
        sibling = (x, y, 1 - c)
        remote = []
        for a in range(n):
            R = bufs[a].shape[0] // 2
            mine = bufs[a].at[pl.ds(pl.multiple_of(c * R, SUBLANES), R), :]
            cp = pltpu.make_async_remote_copy(src_ref=mine, dst_ref=mine, send_sem=send_sems.at[a], recv_sem=recv_sems.at[a],
                                              device_id=sibling, device_id_type=MESH)
            cp.start()
            remote.append(cp)
        for a in range(n):
            R = bufs[a].shape[0] // 2
            theirs = bufs[a].at[pl.ds(pl.multiple_of((1 - c) * R, SUBLANES), R), :]
            pltpu.make_async_remote_copy(src_ref=theirs, dst_ref=theirs, send_sem=send_sems.at[a], recv_sem=recv_sems.at[a],
                                         device_id=sibling, device_id_type=MESH).wait_recv()
        for cp in remote:
            cp.wait_send()

    outs = pl.pallas_call(
        body, name=name, out_shape=tuple(jax.ShapeDtypeStruct(h.shape, h.dtype) for h in shards),
        in_specs=[ANY] * n, out_specs=tuple([ANY] * n), input_output_aliases={a: a for a in range(n)},
        scratch_shapes=[pltpu.SemaphoreType.DMA((n,)), pltpu.SemaphoreType.DMA((n,))],
    )(*shards)
    return list(outs)


BLOB_ALIGN = SUBLANES * LANES


def _pack(arrs):
    flat = jnp.concatenate([a.reshape(-1).astype(F32) for a in arrs])
    n = flat.shape[0]
    padded = -(-n // BLOB_ALIGN) * BLOB_ALIGN
    return jnp.pad(flat, (0, padded - n)).reshape(padded // LANES, LANES)


def _unpack(flat, shapes):
    out, off = [], 0
    for s in shapes:
        n = math.prod(s)
        out.append(flat[..., off:off + n].reshape(flat.shape[:-1] + tuple(s)))
        off += n
    return out


def _gather_blob(blob, name):
    r = blob.shape[0]
    at = lambda px, py, pc: ((4 * px + 2 * py + pc) * r, 0)
    (out,) = _allgather8([(blob, lambda px, py, pc: (0, 0), (r, LANES), (8 * r, LANES), at)], name)
    return out.reshape(8, r * LANES)


def _conv_ffn_fwd(mm, full, layer, X, mods, n2g, conv_w, conv_b, L, tb, nlat, ch, tag):
    sh2, sc2, g2 = mods[3], mods[4], mods[5]
    h2 = _norm_mod_fwd(X, n2g, sh2, sc2, tb, nlat, tag + "_norm2")
    zf = mm(h2, full[f"ffn_w_up{layer}"], "nn", BF16, tag + "_up", tm=(544, 512), tn=(1408, 512))
    f = _ffn_act_fwd(zf, conv_w, conv_b, L, ch, tag + "_act")
    yf = mm(f, full[f"ffn_w_down{layer}"], "nn", F32, tag + "_down", tm=(544, 512), tn=(512,))
    Xn = _gate_res_fwd(X, yf, g2, tb, nlat, tag + "_res2")
    return Xn, (X, h2, zf, f, yf)


def _conv_ffn_bwd(mm, full, gbuf, layer, dXn, saved, mods, n2g, conv_w, conv_b, L, tb, nlat, ch, tag):
    X, h2, zf, f, yf = saved
    sc2, g2 = mods[4], mods[5]
    w_up, w_down = full[f"ffn_w_up{layer}"], full[f"ffn_w_down{layer}"]
    dy, dg2 = _gate_bwd(dXn, yf, g2, tb, nlat, tag + "_dres2")
    df = mm(dy, w_down, "nt", BF16, tag + "_ddown_x", tm=(544, 512), tn=(1408, 512))
    gbuf[f"ffn_w_down{layer}"] = mm(f, dy, "tn", BF16, tag + "_ddown_w", tm=(1408, 512), tn=(2048,), tk=(544, 512), outer="i")
    dgp, du, dcw, dcb = _ffn_act_bwd(zf, df, conv_w, conv_b, L, ch, tag + "_dact")
    dzf = jnp.concatenate([dgp, du], axis=1)
    dh2 = mm(dzf, w_up, "nt", BF16, tag + "_dup_x", tm=(1088, 1024, 512), tn=(1024,), tk=(2816, 512))
    gbuf[f"ffn_w_up{layer}"] = mm(h2, dzf, "tn", BF16, tag + "_dup_w", tm=(2048,), tn=(1408, 512), tk=(544, 512))
    dX, dn2g, dsh2, dsc2 = _norm_mod_bwd(X, n2g, sc2, dh2, dXn, tb, nlat, tag + "_dnorm2")
    return dX, dict(n2g=dn2g, sh2=dsh2, sc2=dsc2, g2=dg2, cw=dcw, cb=dcb)


def kernel(x, c, ctx, c_ctx, norm1_g, norm2_g, w_ada, b_ada, ab_w_in, ab_b_in, a_ln_g, a_ln_b, a_w_s, a_b_s, b_conv_w, b_conv_b, b_ln_g, b_ln_b, ab_w_out, mla_w_in, mla_q_norm_g, mla_w_uq, mla_kv_norm_g, mla_w_ukv, mla_w_o, ffn_w_up, ffn_conv_w, ffn_conv_b, ffn_w_down, final_norm_g, loss_target, m_c_ctx, m_norm1_g, m_norm2_g, m_w_ada, m_b_ada, m_ab_w_in, m_ab_b_in, m_a_ln_g, m_a_ln_b, m_a_w_s, m_a_b_s, m_b_conv_w, m_b_conv_b, m_b_ln_g, m_b_ln_b, m_ab_w_out, m_mla_w_in, m_mla_q_norm_g, m_mla_w_uq, m_mla_kv_norm_g, m_mla_w_ukv, m_mla_w_o, m_ffn_w_up, m_ffn_conv_w, m_ffn_conv_b, m_ffn_w_down, m_final_norm_g, v_c_ctx, v_norm1_g, v_norm2_g, v_w_ada, v_b_ada, v_ab_w_in, v_ab_b_in, v_a_ln_g, v_a_ln_b, v_a_w_s, v_a_b_s, v_b_conv_w, v_b_conv_b, v_b_ln_g, v_b_ln_b, v_ab_w_out, v_mla_w_in, v_mla_q_norm_g, v_mla_w_uq, v_mla_kv_norm_g, v_mla_w_ukv, v_mla_w_o, v_ffn_w_up, v_ffn_conv_w, v_ffn_conv_b, v_ffn_w_down, v_final_norm_g):
    W = dict(c_ctx=c_ctx, norm1_g=norm1_g, norm2_g=norm2_g, w_ada=w_ada, b_ada=b_ada, ab_w_in=ab_w_in, ab_b_in=ab_b_in, a_ln_g=a_ln_g,
             a_ln_b=a_ln_b, a_w_s=a_w_s, a_b_s=a_b_s, b_conv_w=b_conv_w, b_conv_b=b_conv_b, b_ln_g=b_ln_g, b_ln_b=b_ln_b,
             ab_w_out=ab_w_out, mla_w_in=mla_w_in, mla_q_norm_g=mla_q_norm_g, mla_w_uq=mla_w_uq, mla_kv_norm_g=mla_kv_norm_g,
             mla_w_ukv=mla_w_ukv, mla_w_o=mla_w_o, ffn_w_up=ffn_w_up, ffn_conv_w=ffn_conv_w, ffn_conv_b=ffn_conv_b,
             ffn_w_down=ffn_w_down, final_norm_g=final_norm_g)
    MOM = dict(c_ctx=m_c_ctx, norm1_g=m_norm1_g, norm2_g=m_norm2_g, w_ada=m_w_ada, b_ada=m_b_ada, ab_w_in=m_ab_w_in, ab_b_in=m_ab_b_in,
               a_ln_g=m_a_ln_g, a_ln_b=m_a_ln_b, a_w_s=m_a_w_s, a_b_s=m_a_b_s, b_conv_w=m_b_conv_w, b_conv_b=m_b_conv_b,
               b_ln_g=m_b_ln_g, b_ln_b=m_b_ln_b, ab_w_out=m_ab_w_out, mla_w_in=m_mla_w_in, mla_q_norm_g=m_mla_q_norm_g,
               mla_w_uq=m_mla_w_uq, mla_kv_norm_g=m_mla_kv_norm_g, mla_w_ukv=m_mla_w_ukv, mla_w_o=m_mla_w_o, ffn_w_up=m_ffn_w_up,
               ffn_conv_w=m_ffn_conv_w, ffn_conv_b=m_ffn_conv_b, ffn_w_down=m_ffn_w_down, final_norm_g=m_final_norm_g)
    VAR = dict(c_ctx=v_c_ctx, norm1_g=v_norm1_g, norm2_g=v_norm2_g, w_ada=v_w_ada, b_ada=v_b_ada, ab_w_in=v_ab_w_in, ab_b_in=v_ab_b_in,
               a_ln_g=v_a_ln_g, a_ln_b=v_a_ln_b, a_w_s=v_a_w_s, a_b_s=v_a_b_s, b_conv_w=v_b_conv_w, b_conv_b=v_b_conv_b,
               b_ln_g=v_b_ln_g, b_ln_b=v_b_ln_b, ab_w_out=v_ab_w_out, mla_w_in=v_mla_w_in, mla_q_norm_g=v_mla_q_norm_g,
               mla_w_uq=v_mla_w_uq, mla_kv_norm_g=v_mla_kv_norm_g, mla_w_ukv=v_mla_w_ukv, mla_w_o=v_mla_w_o, ffn_w_up=v_ffn_w_up,
               ffn_conv_w=v_ffn_conv_w, ffn_conv_b=v_ffn_conv_b, ffn_w_down=v_ffn_w_down, final_norm_g=v_final_norm_g)
    ORDER = list(W.keys())

    L, D = x.shape[1], x.shape[2]
    CT = ctx.shape[1]
    T = L + CT
    AW, BW = a_ln_g.shape[-1], b_ln_g.shape[-1]
    AH = a_w_s.shape[1]
    QL, KVL = 4 * mla_q_norm_g.shape[-1], 4 * mla_kv_norm_g.shape[-1]
    H = 4 * mla_w_o.shape[1] // HEAD
    HS = H // 4
    DFF = ffn_conv_b.shape[-1]
    NA = w_ada.shape[-1]
    tb = 256 if (L % 256 == 0 and CT % 256 == 0) else 128
    nlat = L // tb
    ch = tb
    tq = 256 if L >= 512 else 128
    xi, yi, ci = _me()
    p_me = 2 * xi + yi
    dev = 4 * xi + 2 * yi + ci

    shard_small = [c[0], mla_q_norm_g[0], mla_kv_norm_g[0], b_conv_w[0], ffn_conv_w]
    g0 = _gather_blob(_pack(shard_small), "gather_small")
    c_all, qg_s, kvg_s, bcw_s, fcw_s = _unpack(g0, [a.shape for a in shard_small])
    per_chip = lambda a: a[0::2]
    qg = per_chip(qg_s).reshape(QL)
    kvg = per_chip(kvg_s).reshape(KVL)
    bcw = jnp.concatenate(list(per_chip(bcw_s)), axis=-1)
    fcw = jnp.concatenate(list(per_chip(fcw_s)), axis=-1)
    c16 = jnp.concatenate([c_all, c_ctx[None], jnp.zeros((7, D), F32)], axis=0)

    ms = []
    for i in range(2):
        bias = lax.dynamic_slice(b_ada[i], (p_me * NA,), (NA,))
        ms.append(_mm(c16, w_ada[i], "nn", F32, f"ada{i}", tm=(16,), tn=(512,), bias=bias, silu_a=True))
    ms = jnp.concatenate(ms, axis=0)
    (mods_all,) = _allgather8(
        [(ms, lambda px, py, pc: (pc * 16, 0), (16, NA), (32, 4 * NA), lambda px, py, pc: (pc * 16, (2 * px + py) * NA))], "gather_mods")
    mods_all = mods_all.reshape(2, 16, N_MOD, D)
    mods = []
    for i in range(2):
        lat = lax.dynamic_index_in_dim(mods_all[i], dev, axis=0, keepdims=False)
        both = jnp.stack([lat, mods_all[i, 8]], axis=0)
        mods.append([both[:, k][:, None, :] for k in range(N_MOD)])

    def pad_uq(w):
        w = w.reshape(w.shape[0], HS, HEAD + ROPE)
        return jnp.pad(w, ((0, 0), (0, 0), (0, QHEAD - HEAD - ROPE))).reshape(w.shape[0], HS * QHEAD)

    MI = QL + KVL + LANES
    big = [
        ("ab_w_in", ab_w_in[0], "col"), ("ab_w_out", ab_w_out[0], "row"),
        ("mla_w_in", jnp.pad(mla_w_in[0], ((0, 0), (0, MI - mla_w_in.shape[-1]))), "row"),
        ("mla_w_uq", pad_uq(mla_w_uq[0]), "col"), ("mla_w_ukv", mla_w_ukv[0], "col"), ("mla_w_o", mla_w_o[0], "row"),
        ("ffn_w_up0", ffn_w_up[0], "col"), ("ffn_w_up1", ffn_w_up[1], "col"),
        ("ffn_w_down0", ffn_w_down[0], "row"), ("ffn_w_down1", ffn_w_down[1], "row"),
    ]
    full, wins = {}, {}
    for nm, w, kind in big:
        R, C = w.shape[0] // 2, w.shape[1]
        if kind == "col":
            wins[nm] = ((R, C), lambda p, pc, R=R, C=C: (pc * R, p * C), kind)
        else:
            wins[nm] = ((R, C), lambda p, pc, R=R, C=C: ((2 * p + pc) * R, 0), kind)
        full[nm] = _cast_place(w, kind, "cast_" + nm)
    names = [b[0] for b in big]

    gather_plan = {
        "l0_in": [("ffn_w_up0", 2, 0)], "l0_out": [("ffn_w_up0", 2, 1)],
        "l0_ffn_up": [("ffn_w_down0", 1, 0), ("mla_w_in", 1, 0), ("mla_w_uq", 1, 0)],
        "l0_ffn_down": [("mla_w_ukv", 1, 0), ("mla_w_o", 1, 0)],
        "l1_attn": [("ffn_w_up1", 1, 0), ("ffn_w_down1", 1, 0)],
    }
    reduce_plan = {
        "l1_ffn_dup_x": [("ffn_w_down1", 1, 0)], "l1_dattn": [("ffn_w_up1", 1, 0), ("mla_w_o", 1, 0)],
        "l0_ffn_ddown_x": [("mla_w_ukv", 1, 0), ("mla_w_uq", 1, 0)], "l0_ffn_ddown_w": [("mla_w_in", 1, 0)],
        "l0_ffn_dup_x": [("ffn_w_down0", 2, 0)], "l0_ffn_dup_w": [("ffn_w_down0", 2, 1)],
        "l0_dout_x": [("ffn_w_up0", 4, 0)], "l0_dout_w": [("ffn_w_up0", 4, 1)], "l0_din_x": [("ffn_w_up0", 4, 2)],
        "l0_din_w": [("ffn_w_up0", 4, 3)],
    }
    gbuf, rbuf = {}, {}

    def gather_side(req):
        nms = list(dict.fromkeys(nm for nm, _, _ in req))
        side = _gather_side([full[nm] for nm in nms], [(nms.index(nm), wins[nm][0], wins[nm][1], k, i) for nm, k, i in req])

        def commit(bufs):
            full.update(zip(nms, bufs))

        return side, commit

    def reduce_side(req):
        nms = list(dict.fromkeys(nm for nm, _, _ in req))
        for nm in nms:
            if nm not in rbuf:
                rbuf[nm] = lax.empty((8,) + wins[nm][0], BF16)
        n = len(nms)
        side = _rs_side([gbuf[nm] for nm in nms] + [rbuf[nm] for nm in nms],
                        [(nms.index(nm), n + nms.index(nm), wins[nm][0], wins[nm][1], k, i) for nm, k, i in req])

        def commit(bufs):
            gbuf.update(zip(nms, bufs[:n]))
            rbuf.update(zip(nms, bufs[n:]))

        return side, commit

    def side_of(name):
        if name in gather_plan:
            return gather_side(gather_plan[name])
        if name in reduce_plan:
            return reduce_side(reduce_plan[name])
        return None, None

    def carried(fn, name_at):
        def run(*args, **kw):
            side, commit = side_of(args[name_at])
            if side is None:
                return fn(*args, **kw)
            out, bufs = fn(*args, side=side, **kw)
            commit(bufs)
            return out
        return run

    mm, attn_fwd, attn_bwd = carried(_mm, 4), carried(_attn_fwd, 6), carried(_attn_bwd, 9)

    side, commit = gather_side([("ab_w_in", 1, 0), ("ab_w_out", 1, 0)])
    commit(_comm_only(side, "gather_first"))

    X0 = jnp.concatenate([x[0], ctx[0]], axis=0)
    m0, m1 = mods[0], mods[1]
    h1 = _norm_mod_fwd(X0, norm1_g[0], m0[0], m0[1], tb, nlat, "l0_norm1")
    z = mm(h1, full["ab_w_in"], "nn", BF16, "l0_in", tm=(544, 512), tn=(1024, 512), bias=ab_b_in[0])
    bs_full = jnp.broadcast_to(a_b_s[0][:, :, None], (AH, CHUNK, CHUNK))
    ya = _gmlp_fwd(z, a_ln_g[0], a_ln_b[0], a_w_s[0], bs_full, tb, "l0_gmlp")
    hc = _glu_conv_fwd(z, 2 * AW, bcw, b_conv_b[0], L, ch, "l0_conv")
    yb = _ln_silu_fwd(hc, b_ln_g[0], b_ln_b[0], tb, "l0_lnsilu")
    yab = jnp.concatenate([ya, yb], axis=1)
    y0 = mm(yab, full["ab_w_out"], "nn", F32, "l0_out", tm=(544, 512), tn=(1024, 512))
    X1 = _gate_res_fwd(X0, y0, m0[2], tb, nlat, "l0_res1")
    X2, ffn0 = _conv_ffn_fwd(mm, full, 0, X1, m0, norm2_g[0], fcw[0], ffn_conv_b[0], L, tb, nlat, ch, "l0_ffn")

    tabs = _rope_tables(L, T)
    hm = _norm_mod_fwd(X2, norm1_g[1], m1[0], m1[1], tb, nlat, "l1_norm1")
    zm = mm(hm, full["mla_w_in"], "nn", F32, "l1_in", tm=(544, 512), tn=(MI,))
    cqn, ckvn, kpe = _mla_prep_fwd(zm, qg, kvg, tabs, tb, "l1_prep")
    q = mm(cqn, full["mla_w_uq"], "nn", BF16, "l1_uq", tm=(512,), tn=(1024, 512), rows=L)
    kvh = mm(ckvn, full["mla_w_ukv"], "nn", BF16, "l1_ukv", tm=(544, 512), tn=(1024, 512))
    o, lse = attn_fwd(q, kvh, kpe, tabs, L, tq, "l1_attn")
    yl = mm(o, full["mla_w_o"], "nn", F32, "l1_o", tm=(512,), tn=(1024, 512))
    m1_lat = [a[:1] for a in m1]
    X2l = X2[:L]
    X3 = _gate_res_fwd(X2l, yl, m1_lat[2], tb, nlat, "l1_res1")
    X4, ffn1 = _conv_ffn_fwd(mm, full, 1, X3, m1_lat, norm2_g[1], fcw[1], ffn_conv_b[1], L, tb, nlat, ch, "l1_ffn")
    loss_acc, dX4, dfinal = _final_loss(X4, final_norm_g, loss_target[0], tb, "loss")

    dX3, gf1 = _conv_ffn_bwd(mm, full, gbuf, 1, dX4, ffn1, m1_lat, norm2_g[1], fcw[1], ffn_conv_b[1], L, tb, nlat, ch, "l1_ffn")
    dyl, dg1_1 = _gate_bwd(dX3, yl, m1_lat[2], tb, nlat, "l1_dres1")
    do = mm(dyl, full["mla_w_o"], "nt", BF16, "l1_do_x", tm=(512,), tn=(1024, 512))
    gbuf["mla_w_o"] = mm(o, dyl, "tn", BF16, "l1_do_w", tm=(1024, 512), tn=(2048, 512), tk=(512,), outer="i")
    dq, dkv, dkpe = attn_bwd(q, kvh, kpe, tabs, o, lse, do, L, tq, "l1_dattn")
    dckvn = mm(dkv, full["mla_w_ukv"], "nt", BF16, "l1_dukv_x", tm=(544, 512), tn=(KVL,), tk=(2048, 512))
    gbuf["mla_w_ukv"] = mm(ckvn, dkv, "tn", BF16, "l1_dukv_w", tm=(KVL,), tn=(1024, 512), tk=(544, 512))
    dcqn = mm(dq, full["mla_w_uq"], "nt", BF16, "l1_duq_x", tm=(512,), tn=(QL,), tk=(2048, 512))
    gbuf["mla_w_uq"] = mm(cqn, dq, "tn", BF16, "l1_duq_w", tm=(QL,), tn=(1024, 512), tk=(512,), rows=L)
    dcqn = jnp.concatenate([dcqn, jnp.zeros((CT, QL), BF16)], axis=0)
    dzm, dqg, dkvg = _mla_prep_bwd(zm, qg, kvg, tabs, dcqn, dckvn, dkpe, tb, "l1_dprep")
    dhm = mm(dzm, full["mla_w_in"], "nt", BF16, "l1_din_x", tm=(544, 512), tn=(1024, 512))
    gbuf["mla_w_in"] = mm(hm, dzm, "tn", BF16, "l1_din_w", tm=(2048,), tn=(MI,), tk=(544, 512))
    dX3p = jnp.concatenate([dX3, jnp.zeros((CT, D), F32)], axis=0)
    dX2, dn1g_1, dsh1_1, dsc1_1 = _norm_mod_bwd(X2, norm1_g[1], m1[1], dhm, dX3p, tb, nlat, "l1_dnorm1")

    dX1, gf0 = _conv_ffn_bwd(mm, full, gbuf, 0, dX2, ffn0, m0, norm2_g[0], fcw[0], ffn_conv_b[0], L, tb, nlat, ch, "l0_ffn")
    dy0, dg1_0 = _gate_bwd(dX1, y0, m0[2], tb, nlat, "l0_dres1")
    dyab = mm(dy0, full["ab_w_out"], "nt", BF16, "l0_dout_x", tm=(544, 512), tn=(1024, 512))
    gbuf["ab_w_out"] = mm(yab, dy0, "tn", BF16, "l0_dout_w", tm=(1024, 512), tn=(2048, 512), tk=(544, 512), outer="i")
    dzu, dzv, dlnag, dlnab, dws, dbs, csu, csv = _gmlp_bwd(z, dyab, a_ln_g[0], a_ln_b[0], a_w_s[0], bs_full, tb, "l0_dgmlp")
    dhc, dlnbg, dlnbb = _ln_silu_bwd(hc, dyab, AW, b_ln_g[0], b_ln_b[0], tb, "l0_dlnsilu")
    dza, dzg, dbcw, dbcb, csa, csg = _glu_conv_bwd(z, 2 * AW, dhc, bcw, L, ch, "l0_dconv")
    dz = jnp.concatenate([dzu, dzv, dza, dzg], axis=1)
    dbin = jnp.concatenate([csu, csv, csa, csg], axis=1)
    dh1 = mm(dz, full["ab_w_in"], "nt", BF16, "l0_din_x", tm=(544, 512), tn=(1024, 512), tk=(2048, 512))
    gbuf["ab_w_in"] = mm(h1, dz, "tn", BF16, "l0_din_w", tm=(2048,), tn=(1024, 512), tk=(544, 512))
    dX0, dn1g_0, dsh1_0, dsc1_0 = _norm_mod_bwd(X0, norm1_g[0], m0[1], dh1, dX1, tb, nlat, "l0_dnorm1")
    grad_x = dX0[:L][None]

    side, commit = reduce_side([("ab_w_out", 1, 0), ("ab_w_in", 1, 0)])
    commit(_comm_only(side, "rs_last"))
    halves = [_rs_sum8(gbuf[nm], rbuf[nm], wins[nm][0], wins[nm][2], "rs_sum_" + nm) for nm in names]
    gshard = dict(zip(names, _rs_share(halves, "rs_share")))

    def grp6(l, sh1, sc1, g1, f):
        G = sh1.shape[0]
        pad = lambda a: jnp.concatenate([a, jnp.zeros((G - a.shape[0],) + a.shape[1:], F32)], axis=0) if a.shape[0] < G else a
        return jnp.concatenate([pad(a) for a in (sh1, sc1, g1, f["sh2"], f["sc2"], f["g2"])], axis=1)

    dm0 = grp6(0, dsh1_0, dsc1_0, dg1_0, gf0)
    dm1 = grp6(1, dsh1_1, dsc1_1, dg1_1, gf1)
    dmods = jnp.stack([dm0, dm1], axis=0)
    small = [
        jnp.concatenate([dn1g_0, dn1g_1], axis=0), jnp.concatenate([gf0["n2g"], gf1["n2g"]], axis=0), dbin, dlnag, dlnab, dws,
        dbs, dbcw, dbcb, dlnbg, dlnbb, dqg, dkvg, jnp.stack([gf0["cw"], gf1["cw"]], axis=0),
        jnp.concatenate([gf0["cb"], gf1["cb"]], axis=0), dfinal, dmods[:, 1],
    ]
    small_shapes = [a.shape for a in small]
    lat_shape = dmods[:, 0].shape
    blob = _pack(small + [dmods[:, 0], loss_acc[0, :1]])
    gathered = _gather_blob(blob, "gather_grads")
    summed = _sum_lead(gathered.reshape(8, -1, LANES), F32, "sum_grads").reshape(-1)
    (dn1g, dn2g, dbin_s, dlnag_s, dlnab_s, dws_s, dbs_s, dbcw_s, dbcb_s, dlnbg_s, dlnbb_s, dqg_s, dkvg_s, dfcw_s, dfcb_s, dfinal_s,
     dmods_ctx, dmods_lat_sum, loss) = _unpack(summed, small_shapes + [lat_shape, (1,)])
    loss = loss.reshape(())
    n_small = sum(math.prod(s) for s in small_shapes)
    dmods_lat = gathered[:, n_small:n_small + math.prod(lat_shape)].reshape((8,) + lat_shape)

    grad_w_ada, dc_parts = [], []
    for i in range(2):
        dm16 = jnp.concatenate([dmods_lat[:, i].reshape(8, N_MOD * D), dmods_ctx[i].reshape(1, N_MOD * D),
                                jnp.zeros((7, N_MOD * D), F32)], axis=0)
        dm16_s = lax.dynamic_slice(dm16, (0, p_me * NA), (16, NA))
        grad_w_ada.append(_mm(c16, dm16_s, "tn", F32, f"dada{i}_w", tm=(1024, 512), tn=(1024, 512), silu_a=True))
        dc_parts.append(_mm(dm16_s, w_ada[i], "nt", F32, f"dada{i}_c", tm=(16,), tn=(512,), tk=(1024, 512)))
    grad_w_ada = jnp.stack(grad_w_ada, axis=0)
    grad_b_ada = dmods_lat_sum.reshape(2, N_MOD * D) + dmods_ctx.reshape(2, N_MOD * D)
    dc_blob = _pack([dc_parts[0][8] + dc_parts[1][8]])
    dc_all = _gather_blob(dc_blob, "gather_dc")[0::2, :D]
    grad_c_ctx = _cctx_grad(dc_all, c_ctx, "dcctx").reshape(D)

    def my_cols(a, axis, n):
        return lax.dynamic_slice_in_dim(a, p_me * n, n, axis=axis)

    unpad_uq = lambda g: g.reshape(QL, HS, QHEAD)[:, :, :HEAD + ROPE].reshape(QL, HS * (HEAD + ROPE))
    grads = dict(
        c_ctx=grad_c_ctx, norm1_g=dn1g, norm2_g=dn2g, w_ada=grad_w_ada, b_ada=grad_b_ada, ab_w_in=gshard["ab_w_in"][None],
        ab_b_in=dbin_s, a_ln_g=dlnag_s, a_ln_b=dlnab_s, a_w_s=dws_s[None], a_b_s=dbs_s.reshape(1, AH, CHUNK),
        b_conv_w=my_cols(dbcw_s, 1, BW // 4)[None], b_conv_b=dbcb_s, b_ln_g=dlnbg_s, b_ln_b=dlnbb_s,
        ab_w_out=gshard["ab_w_out"][None], mla_w_in=gshard["mla_w_in"][:, :mla_w_in.shape[-1]][None],
        mla_q_norm_g=my_cols(dqg_s, 1, QL // 4), mla_w_uq=unpad_uq(gshard["mla_w_uq"])[None],
        mla_kv_norm_g=my_cols(dkvg_s, 1, KVL // 4), mla_w_ukv=gshard["mla_w_ukv"][None], mla_w_o=gshard["mla_w_o"][None],
        ffn_w_up=jnp.stack([gshard["ffn_w_up0"], gshard["ffn_w_up1"]], axis=0), ffn_conv_w=my_cols(dfcw_s, 2, DFF // 4),
        ffn_conv_b=dfcb_s, ffn_w_down=jnp.stack([gshard["ffn_w_down0"], gshard["ffn_w_down1"]], axis=0),
        final_norm_g=dfinal_s.reshape(D),
    )
    grads = {k: grads[k].reshape(W[k].shape) for k in ORDER}

    BIG = ("w_ada", "ab_w_in", "ab_w_out", "mla_w_in", "mla_w_uq", "mla_w_ukv", "mla_w_o", "ffn_w_up", "ffn_w_down")
    delta, new_m, new_v = {}, {}, {}
    for k in BIG:
        two = lambda a: a.reshape(-1, a.shape[-1])
        d_, m_, v_ = _adamw(two(W[k]), two(grads[k]), two(MOM[k]), two(VAR[k]), "adamw_" + k)
        delta[k], new_m[k], new_v[k] = (a.reshape(W[k].shape) for a in (d_, m_, v_))
    SMALL = [k for k in ORDER if k not in BIG]
    d_, m_, v_ = _adamw(_pack([W[k] for k in SMALL]), _pack([grads[k] for k in SMALL]), _pack([MOM[k] for k in SMALL]),
                        _pack([VAR[k] for k in SMALL]), "adamw_small")
    shapes = [W[k].shape for k in SMALL]
    for k, dd, mm, vv in zip(SMALL, _unpack(d_.reshape(-1), shapes), _unpack(m_.reshape(-1), shapes), _unpack(v_.reshape(-1), shapes)):
        delta[k], new_m[k], new_v[k] = dd, mm, vv

    return (loss, grad_x, *[grads[k] for k in ORDER], *[delta[k] for k in ORDER], *[new_m[k] for k in ORDER],
            *[new_v[k] for k in ORDER])
```

```python
import functools
import math

import jax
import jax.numpy as jnp
from jax import lax
from jax.experimental import pallas as pl
from jax.experimental.pallas import tpu as pltpu

F32 = jnp.float32
BF16 = jnp.bfloat16
MESH = pl.DeviceIdType.MESH
ANY = pl.BlockSpec(memory_space=pl.ANY)

EPS = 1e-6
N_MOD = 6
CHUNK = 128
HEAD = 128
ROPE = 64
QHEAD = 2 * HEAD
GRID_W = 64
ROPE_THETA = 10000.0
B_CONV = 31
FFN_CONV = 3
ADAM_LR, ADAM_B1, ADAM_B2, ADAM_EPS, ADAM_WD, ADAM_STEP = 0.001, 0.9, 0.999, 1e-08, 0.01, 10

V7X_VMEM_LIMIT = 56 * 1024 * 1024
LANES = 128
SUBLANES = 8
CONV_PAD = 16

BS = pl.BlockSpec


def _cp(sem=None, vmem=V7X_VMEM_LIMIT):
    return pltpu.CompilerParams(dimension_semantics=sem, vmem_limit_bytes=vmem)


class _Side:
    def __init__(self, bufs, nsem, start, finish):
        self.bufs, self.nsem, self.start, self.finish = list(bufs), nsem, start, finish


def _call(body, *, name, out_shape, grid, in_specs, out_specs, operands, sem, scratch=(), side=None):
    if side is None:
        return pl.pallas_call(body, name=name, out_shape=out_shape, grid=grid, in_specs=in_specs, out_specs=out_specs,
                              scratch_shapes=list(scratch), compiler_params=_cp(sem))(*operands)
    multi = isinstance(out_shape, (tuple, list))
    outs = tuple(out_shape) if multi else (out_shape,)
    ospecs = tuple(out_specs) if multi else (out_specs,)
    n_in, n_out, n_scr, n_buf = len(in_specs), len(outs), len(scratch), len(side.bufs)

    def body2(*refs):
        o0 = n_in + n_buf
        s0 = o0 + n_out + n_buf
        bufs = refs[o0 + n_out:s0]
        sends, recvs = refs[s0 + n_scr:]
        first = last = None
        for d, g in enumerate(grid):
            pid = pl.program_id(d)
            first = (pid == 0) if first is None else jnp.logical_and(first, pid == 0)
            last = (pid == g - 1) if last is None else jnp.logical_and(last, pid == g - 1)

        @pl.when(first)
        def _():
            side.start(bufs, sends, recvs)

        body(*refs[:n_in], *refs[o0:o0 + n_out], *refs[s0:s0 + n_scr])

        @pl.when(last)
        def _():
            side.finish(bufs, sends, recvs)

    res = pl.pallas_call(
        body2, name=name, out_shape=outs + tuple(jax.ShapeDtypeStruct(b.shape, b.dtype) for b in side.bufs), grid=grid,
        in_specs=list(in_specs) + [ANY] * n_buf, out_specs=ospecs + (ANY,) * n_buf,
        scratch_shapes=list(scratch) + [pltpu.SemaphoreType.DMA((side.nsem,)), pltpu.SemaphoreType.DMA((side.nsem,))],
        input_output_aliases={n_in + i: n_out + i for i in range(n_buf)}, compiler_params=_cp(("arbitrary",) * len(grid)),
    )(*operands, *side.bufs)
    main = res[:n_out]
    return (tuple(main) if multi else main[0]), list(res[n_out:])


def _pick(n, prefs):
    for p in prefs:
        if p <= n and n % p == 0:
            return p
    return n


def _row_tile(rows, cols, target_bytes=1 << 20):
    best = None
    for d in range(2 * SUBLANES, rows + 1, 2 * SUBLANES):
        if rows % d == 0 and d * cols * 4 <= target_bytes:
            best = d
    return best if best is not None else rows


def _sigmoid(x):
    return 1.0 / (1.0 + jnp.exp(-x))


def _gelu(x):
    c = math.sqrt(2.0 / math.pi)
    th = jnp.tanh(c * (x + 0.044715 * x * x * x))
    return 0.5 * x * (1.0 + th), th


def _gelu_grad(x, th):
    c = math.sqrt(2.0 / math.pi)
    return 0.5 * (1.0 + th) + 0.5 * x * (1.0 - th * th) * c * (1.0 + 3.0 * 0.044715 * x * x)


_DIMS = {"nn": (((1,), (0,)), ((), ())), "nt": (((1,), (1,)), ((), ())), "tn": (((0,), (0,)), ((), ()))}


def _mm(a, b, mode, out_dtype, name, tm=(512,), tn=(512,), tk=(100000,), bias=None, silu_a=False, rows=None, outer="j", side=None,
        b_lead=None):
    bshape = b.shape[-2:]
    if mode == "nn":
        (M, K), N = a.shape, bshape[1]
    elif mode == "nt":
        (M, K), N = a.shape, bshape[0]
    else:
        (K, M), N = a.shape, bshape[1]
    if rows is not None:
        if mode == "tn":
            K = rows
        else:
            M = rows
    tm, tn, tk = _pick(M, tm), _pick(N, tn), _pick(K, tk)
    gm, gn, gk = M // tm, N // tn, K // tk

    def ij(g0, g1):
        return (g1, g0) if outer == "j" else (g0, g1)

    if mode == "nn":
        a_spec = BS((tm, tk), lambda g0, g1, k: (ij(g0, g1)[0], k))
        b_spec = BS((tk, tn), lambda g0, g1, k: (k, ij(g0, g1)[1]))
    elif mode == "nt":
        a_spec = BS((tm, tk), lambda g0, g1, k: (ij(g0, g1)[0], k))
        b_spec = BS((tn, tk), lambda g0, g1, k: (ij(g0, g1)[1], k))
    else:
        a_spec = BS((tk, tm), lambda g0, g1, k: (k, ij(g0, g1)[0]))
        b_spec = BS((tk, tn), lambda g0, g1, k: (k, ij(g0, g1)[1]))
    if b_lead is not None:
        blk, at = b_spec.block_shape, b_spec.index_map
        b_spec = BS((None,) + tuple(blk), lambda g0, g1, k: (b_lead,) + tuple(at(g0, g1, k)))
    in_specs = [a_spec, b_spec]
    operands = [a, b]
    if bias is not None:
        in_specs.append(BS((1, tn), lambda g0, g1, k: (0, ij(g0, g1)[1])))
        operands.append(bias.reshape(1, N))
    o_spec = BS((tm, tn), lambda g0, g1, k: ij(g0, g1))

    def body(*refs):
        a_ref, b_ref = refs[0], refs[1]
        bias_ref = refs[2] if bias is not None else None
        o_ref = refs[3] if bias is not None else refs[2]
        av = a_ref[...]
        if silu_a:
            av = av.astype(F32)
            av = av * _sigmoid(av)
        part = lax.dot_general(av.astype(BF16), b_ref[...].astype(BF16), _DIMS[mode], preferred_element_type=F32)

        def finish(acc):
            if bias_ref is not None:
                acc = acc + bias_ref[...]
            return acc.astype(out_dtype)

        if gk == 1:
            o_ref[...] = finish(part)
        else:
            acc_ref = refs[-1]
            k = pl.program_id(2)

            @pl.when(k == 0)
            def _():
                acc_ref[...] = part

            @pl.when(k > 0)
            def _():
                acc_ref[...] += part

            @pl.when(k == gk - 1)
            def _():
                o_ref[...] = finish(acc_ref[...])

    grid = (gn, gm, gk) if outer == "j" else (gm, gn, gk)
    return _call(body, name=name, out_shape=jax.ShapeDtypeStruct((M, N), out_dtype), grid=grid, in_specs=in_specs, out_specs=o_spec,
                 scratch=[pltpu.VMEM((tm, tn), F32)] if gk > 1 else [], sem=("parallel", "parallel", "arbitrary"),
                 operands=operands, side=side)


def _accum(ref, val, first):
    @pl.when(first)
    def _():
        ref[...] = val

    @pl.when(jnp.logical_not(first))
    def _():
        ref[...] += val


def _norm_mod_fwd(X, gain, sh, sc, tb, nlat, name):
    R, D = X.shape

    def body(x_ref, g_ref, sh_ref, sc_ref, o_ref):
        x = x_ref[...]
        r = lax.rsqrt(jnp.mean(x * x, axis=-1, keepdims=True) + EPS)
        o_ref[...] = ((x * r * g_ref[...]) * (1.0 + sc_ref[0]) + sh_ref[0]).astype(BF16)

    grp = BS((1, 1, D), lambda i: (i // nlat, 0, 0))
    return pl.pallas_call(
        body, name=name, out_shape=jax.ShapeDtypeStruct((R, D), BF16), grid=(R // tb,),
        in_specs=[BS((tb, D), lambda i: (i, 0)), BS((1, D), lambda i: (0, 0)), grp, grp],
        out_specs=BS((tb, D), lambda i: (i, 0)), compiler_params=_cp(("parallel",)),
    )(X, gain.reshape(1, D), sh, sc)


def _norm_mod_bwd(X, gain, sc, dh, dup, tb, nlat, name):
    R, D = X.shape
    G = sc.shape[0]
    n_up = dup.shape[0] // tb

    def body(x_ref, g_ref, sc_ref, dh_ref, dup_ref, dx_ref, dg_ref, dsh_ref, dsc_ref):
        i = pl.program_id(0)
        x = x_ref[...]
        g = g_ref[...]
        r = lax.rsqrt(jnp.mean(x * x, axis=-1, keepdims=True) + EPS)
        xh = x * r
        dh_ = dh_ref[...].astype(F32)
        t = dh_ * (1.0 + sc_ref[0])
        tg = t * g
        dup_ = dup_ref[...] if n_up == R // tb else jnp.where(i < n_up, dup_ref[...], 0.0)
        dx_ref[...] = dup_ + r * (tg - xh * jnp.mean(tg * xh, axis=-1, keepdims=True))
        _accum(dg_ref, jnp.sum(t * xh, axis=0, keepdims=True), i == 0)
        first = i % nlat == 0
        _accum(dsh_ref, jnp.sum(dh_, axis=0, keepdims=True)[None], first)
        _accum(dsc_ref, jnp.sum(dh_ * xh * g, axis=0, keepdims=True)[None], first)

    row = BS((tb, D), lambda i: (i, 0))
    grp = BS((1, 1, D), lambda i: (i // nlat, 0, 0))
    return pl.pallas_call(
        body, name=name,
        out_shape=(jax.ShapeDtypeStruct((R, D), F32), jax.ShapeDtypeStruct((1, D), F32),
                   jax.ShapeDtypeStruct((G, 1, D), F32), jax.ShapeDtypeStruct((G, 1, D), F32)),
        grid=(R // tb,), in_specs=[row, BS((1, D), lambda i: (0, 0)), grp, row, BS((tb, D), lambda i: (jnp.minimum(i, n_up - 1), 0))],
        out_specs=(row, BS((1, D), lambda i: (0, 0)), grp, grp), compiler_params=_cp(("arbitrary",)),
    )(X, gain.reshape(1, D), sc, dh, dup)


def _gate_res_fwd(X, y, gate, tb, nlat, name):
    R, D = y.shape

    def body(x_ref, y_ref, g_ref, o_ref):
        o_ref[...] = x_ref[...] + g_ref[0] * y_ref[...]

    row = BS((tb, D), lambda i: (i, 0))
    return pl.pallas_call(
        body, name=name, out_shape=jax.ShapeDtypeStruct((R, D), F32), grid=(R // tb,),
        in_specs=[row, row, BS((1, 1, D), lambda i: (i // nlat, 0, 0))], out_specs=row, compiler_params=_cp(("parallel",)),
    )(X, y, gate)


def _gate_bwd(dX, y, gate, tb, nlat, name):
    R, D = dX.shape
    G = gate.shape[0]

    def body(dx_ref, y_ref, g_ref, dy_ref, dg_ref):
        i = pl.program_id(0)
        dx = dx_ref[...]
        dy_ref[...] = (g_ref[0] * dx).astype(BF16)
        _accum(dg_ref, jnp.sum(dx * y_ref[...], axis=0, keepdims=True)[None], i % nlat == 0)

    row = BS((tb, D), lambda i: (i, 0))
    grp = BS((1, 1, D), lambda i: (i // nlat, 0, 0))
    return pl.pallas_call(
        body, name=name, out_shape=(jax.ShapeDtypeStruct((R, D), BF16), jax.ShapeDtypeStruct((G, 1, D), F32)),
        grid=(R // tb,), in_specs=[row, row, grp], out_specs=(row, grp), compiler_params=_cp(("arbitrary",)),
    )(dX, y, gate)


def _final_loss(X, gain, target, tb, name):
    R, D = X.shape

    def body(x_ref, g_ref, t_ref, loss_ref, dx_ref, dg_ref):
        i = pl.program_id(0)
        x = x_ref[...]
        g = g_ref[...]
        r = lax.rsqrt(jnp.mean(x * x, axis=-1, keepdims=True) + EPS)
        xh = x * r
        e = xh * g - t_ref[...]
        part = jnp.sum(jnp.sum(e * e, axis=1, keepdims=True), axis=0, keepdims=True) * (0.5 / D)
        _accum(loss_ref, jnp.broadcast_to(part, (1, LANES)), i == 0)
        dy = e * (1.0 / D)
        _accum(dg_ref, jnp.sum(dy * xh, axis=0, keepdims=True), i == 0)
        tg = dy * g
        dx_ref[...] = r * (tg - xh * jnp.mean(tg * xh, axis=-1, keepdims=True))

    row = BS((tb, D), lambda i: (i, 0))
    return pl.pallas_call(
        body, name=name,
        out_shape=(jax.ShapeDtypeStruct((1, LANES), F32), jax.ShapeDtypeStruct((R, D), F32), jax.ShapeDtypeStruct((1, D), F32)),
        grid=(R // tb,), in_specs=[row, BS((1, D), lambda i: (0, 0)), row],
        out_specs=(BS((1, LANES), lambda i: (0, 0)), row, BS((1, D), lambda i: (0, 0))), compiler_params=_cp(("arbitrary",)),
    )(X, gain.reshape(1, D), target)


def _ln_stats(v):
    mu = jnp.mean(v, axis=-1, keepdims=True)
    d = v - mu
    r = lax.rsqrt(jnp.mean(d * d, axis=-1, keepdims=True) + EPS)
    return d * r, r


def _gmlp_fwd(z, ln_g, ln_b, w_s, b_s_full, tb, name):
    R = z.shape[0]
    AW = ln_g.shape[-1]
    AH = w_s.shape[0]

    def body(zu_ref, zv_ref, g_ref, b_ref, ws_ref, bs_ref, o_ref):
        u, _ = _gelu(zu_ref[...].astype(F32))
        v, _ = _gelu(zv_ref[...].astype(F32))
        xh, _ = _ln_stats(v)
        vn = (xh * g_ref[...] + b_ref[...]).astype(BF16)
        for n in range(tb // CHUNK):
            rs = slice(n * CHUNK, (n + 1) * CHUNK)
            for h in range(AH):
                cs = slice(h * CHUNK, (h + 1) * CHUNK)
                v2 = jnp.dot(ws_ref[h].astype(BF16), vn[rs, cs], preferred_element_type=F32) + bs_ref[h]
                o_ref[rs, cs] = (u[rs, cs] * v2).astype(BF16)

    full3 = lambda s: BS(s, lambda i: (0, 0, 0))
    return pl.pallas_call(
        body, name=name, out_shape=jax.ShapeDtypeStruct((R, AW), BF16), grid=(R // tb,),
        in_specs=[BS((tb, AW), lambda i: (i, 0)), BS((tb, AW), lambda i: (i, 1)), BS((1, AW), lambda i: (0, 0)),
                  BS((1, AW), lambda i: (0, 0)), full3((AH, CHUNK, CHUNK)), full3((AH, CHUNK, CHUNK))],
        out_specs=BS((tb, AW), lambda i: (i, 0)), compiler_params=_cp(("parallel",)),
    )(z, z, ln_g.reshape(1, AW), ln_b.reshape(1, AW), w_s, b_s_full)


def _gmlp_bwd(z, dya, ln_g, ln_b, w_s, b_s_full, tb, name):
    R = z.shape[0]
    AW = ln_g.shape[-1]
    AH = w_s.shape[0]

    def body(zu_ref, zv_ref, dy_ref, g_ref, b_ref, ws_ref, bs_ref, dzu_ref, dzv_ref, dg_ref, db_ref, dws_ref, dbs_ref, cs_u_ref, cs_v_ref,
             dvn_scr):
        i = pl.program_id(0)
        first = i == 0
        zu = zu_ref[...].astype(F32)
        zv = zv_ref[...].astype(F32)
        u, thu = _gelu(zu)
        v, thv = _gelu(zv)
        xh, r = _ln_stats(v)
        g = g_ref[...]
        vn = (xh * g + b_ref[...]).astype(BF16)
        dy = dy_ref[...].astype(F32)

        @pl.when(first)
        def _():
            dws_ref[...] = jnp.zeros_like(dws_ref)
            dbs_ref[...] = jnp.zeros_like(dbs_ref)

        for n in range(tb // CHUNK):
            rs = slice(n * CHUNK, (n + 1) * CHUNK)
            for h in range(AH):
                cs = slice(h * CHUNK, (h + 1) * CHUNK)
                w = ws_ref[h].astype(BF16)
                v2 = jnp.dot(w, vn[rs, cs], preferred_element_type=F32) + bs_ref[h]
                dzu_ref[rs, cs] = (dy[rs, cs] * v2 * _gelu_grad(zu[rs, cs], thu[rs, cs])).astype(BF16)
                dv2 = dy[rs, cs] * u[rs, cs]
                dv2b = dv2.astype(BF16)
                dvn_scr[rs, cs] = lax.dot_general(w, dv2b, _DIMS["tn"], preferred_element_type=F32)
                dws_ref[h] += lax.dot_general(dv2b, vn[rs, cs], _DIMS["nt"], preferred_element_type=F32)
                dbs_ref[h] += jnp.sum(dv2, axis=1, keepdims=True)
        dvn = dvn_scr[...]
        _accum(dg_ref, jnp.sum(dvn * xh, axis=0, keepdims=True), first)
        _accum(db_ref, jnp.sum(dvn, axis=0, keepdims=True), first)
        t = dvn * g
        dv = r * (t - jnp.mean(t, axis=-1, keepdims=True) - xh * jnp.mean(t * xh, axis=-1, keepdims=True))
        dzv = dv * _gelu_grad(zv, thv)
        dzv_ref[...] = dzv.astype(BF16)
        _accum(cs_v_ref, jnp.sum(dzv, axis=0, keepdims=True), first)
        _accum(cs_u_ref, jnp.sum(dzu_ref[...].astype(F32), axis=0, keepdims=True), first)

    full3 = lambda s: BS(s, lambda i: (0, 0, 0))
    vec = BS((1, AW), lambda i: (0, 0))
    row = BS((tb, AW), lambda i: (i, 0))
    outs = pl.pallas_call(
        body, name=name,
        out_shape=(jax.ShapeDtypeStruct((R, AW), BF16), jax.ShapeDtypeStruct((R, AW), BF16), jax.ShapeDtypeStruct((1, AW), F32),
                   jax.ShapeDtypeStruct((1, AW), F32), jax.ShapeDtypeStruct((AH, CHUNK, CHUNK), F32),
                   jax.ShapeDtypeStruct((AH, CHUNK, 1), F32), jax.ShapeDtypeStruct((1, AW), F32), jax.ShapeDtypeStruct((1, AW), F32)),
        grid=(R // tb,),
        in_specs=[row, BS((tb, AW), lambda i: (i, 1)), row, vec, vec, full3((AH, CHUNK, CHUNK)), full3((AH, CHUNK, CHUNK))],
        out_specs=(row, row, vec, vec, full3((AH, CHUNK, CHUNK)), full3((AH, CHUNK, 1)), vec, vec),
        scratch_shapes=[pltpu.VMEM((tb, AW), F32)], compiler_params=_cp(("arbitrary",)),
    )(z, z, dya, ln_g.reshape(1, AW), ln_b.reshape(1, AW), w_s, b_s_full)
    return outs


def _segments(R, L):
    return [(0, L)] + ([(L, R - L)] if R > L else [])


def _scr_rows(R, L):
    return R + CONV_PAD * (len(_segments(R, L)) + 1)


def _scr_off(s, start):
    return CONV_PAD * (s + 1) + start


def _zero_pads(scr, R, L):
    segs = _segments(R, L)
    z = jnp.zeros((CONV_PAD, scr.shape[1]), F32)
    for s, (start, n) in enumerate(segs):
        scr[pl.ds(_scr_off(s, start) - CONV_PAD, CONV_PAD), :] = z
    last_s, (last_start, last_n) = len(segs) - 1, segs[-1]
    scr[pl.ds(_scr_off(last_s, last_start) + last_n, CONV_PAD), :] = z


def _for_chunks(R, L, ch, fn):
    for s, (start, n) in enumerate(_segments(R, L)):
        c = min(ch, n)
        off = _scr_off(s, start)

        def step(i, carry, start=start, off=off, c=c):
            r0 = pl.multiple_of(start + i * c, SUBLANES)
            fn(r0, pl.multiple_of(off + i * c, SUBLANES), c)
            return carry

        lax.fori_loop(0, n // c, step, 0)


def _taps(scr, srow, c, w_ref, ntap, flip):
    win = scr[pl.ds(srow - CONV_PAD, c + 2 * CONV_PAD), :]
    n = c + 2 * CONV_PAD
    acc = None
    for k in range(ntap):
        o = k - (ntap - 1) // 2
        if flip:
            o = -o
        sh = win if o == 0 else pltpu.roll(win, (-o) % n, 0)
        term = w_ref[k:k + 1, :] * sh[CONV_PAD:CONV_PAD + c]
        acc = term if acc is None else acc + term
    return acc


def _tap_grads(scr, srow, c, dy, dw_ref, ntap):
    win = scr[pl.ds(srow - CONV_PAD, c + 2 * CONV_PAD), :]
    n = c + 2 * CONV_PAD
    for k in range(ntap):
        o = k - (ntap - 1) // 2
        sh = win if o == 0 else pltpu.roll(win, (-o) % n, 0)
        dw_ref[k:k + 1, :] += jnp.sum(dy * sh[CONV_PAD:CONV_PAD + c], axis=0, keepdims=True)


def _glu_conv_fwd(z, col0, conv_w, conv_b, L, ch, name):
    R = z.shape[0]
    BW = conv_w.shape[1]
    nb, c0 = BW // LANES, col0 // LANES

    def body(a_ref, g_ref, w_ref, b_ref, o_ref, scr):
        _zero_pads(scr, R, L)

        def fill(r0, s0, c):
            a = a_ref[pl.ds(r0, c), :].astype(F32)
            g = g_ref[pl.ds(r0, c), :].astype(F32)
            scr[pl.ds(s0, c), :] = a * _sigmoid(g)

        _for_chunks(R, L, ch, fill)

        def conv(r0, s0, c):
            o_ref[pl.ds(r0, c), :] = _taps(scr, s0, c, w_ref, B_CONV, False) + b_ref[...]

        _for_chunks(R, L, ch, conv)

    return pl.pallas_call(
        body, name=name, out_shape=jax.ShapeDtypeStruct((R, BW), F32), grid=(nb,),
        in_specs=[BS((R, LANES), lambda j: (0, c0 + j)), BS((R, LANES), lambda j: (0, c0 + nb + j)),
                  BS((B_CONV, LANES), lambda j: (0, j)), BS((1, LANES), lambda j: (0, j))],
        out_specs=BS((R, LANES), lambda j: (0, j)), scratch_shapes=[pltpu.VMEM((_scr_rows(R, L), LANES), F32)],
        compiler_params=_cp(("parallel",)),
    )(z, z, conv_w, conv_b.reshape(1, BW))


def _glu_conv_bwd(z, col0, dhc, conv_w, L, ch, name):
    R = z.shape[0]
    BW = conv_w.shape[1]
    nb, c0 = BW // LANES, col0 // LANES

    def body(a_ref, g_ref, dy_ref, w_ref, da_ref, dg_ref, dw_ref, db_ref, csa_ref, csg_ref, scr_h, scr_dy):
        _zero_pads(scr_h, R, L)
        _zero_pads(scr_dy, R, L)
        dw_ref[...] = jnp.zeros_like(dw_ref)
        db_ref[...] = jnp.zeros_like(db_ref)
        csa_ref[...] = jnp.zeros_like(csa_ref)
        csg_ref[...] = jnp.zeros_like(csg_ref)

        def fill(r0, s0, c):
            a = a_ref[pl.ds(r0, c), :].astype(F32)
            g = g_ref[pl.ds(r0, c), :].astype(F32)
            scr_h[pl.ds(s0, c), :] = a * _sigmoid(g)
            scr_dy[pl.ds(s0, c), :] = dy_ref[pl.ds(r0, c), :]

        _for_chunks(R, L, ch, fill)

        def back(r0, s0, c):
            dh = _taps(scr_dy, s0, c, w_ref, B_CONV, True)
            a = a_ref[pl.ds(r0, c), :].astype(F32)
            sg = _sigmoid(g_ref[pl.ds(r0, c), :].astype(F32))
            da = dh * sg
            dg = dh * a * sg * (1.0 - sg)
            da_ref[pl.ds(r0, c), :] = da.astype(BF16)
            dg_ref[pl.ds(r0, c), :] = dg.astype(BF16)
            csa_ref[...] += jnp.sum(da, axis=0, keepdims=True)
            csg_ref[...] += jnp.sum(dg, axis=0, keepdims=True)
            dy = dy_ref[pl.ds(r0, c), :]
            db_ref[...] += jnp.sum(dy, axis=0, keepdims=True)
            _tap_grads(scr_h, s0, c, dy, dw_ref, B_CONV)

        _for_chunks(R, L, ch, back)

    col = BS((R, LANES), lambda j: (0, j))
    vec = BS((1, LANES), lambda j: (0, j))
    nrow = _scr_rows(R, L)
    return pl.pallas_call(
        body, name=name,
        out_shape=(jax.ShapeDtypeStruct((R, BW), BF16), jax.ShapeDtypeStruct((R, BW), BF16), jax.ShapeDtypeStruct((B_CONV, BW), F32),
                   jax.ShapeDtypeStruct((1, BW), F32), jax.ShapeDtypeStruct((1, BW), F32), jax.ShapeDtypeStruct((1, BW), F32)),
        grid=(nb,),
        in_specs=[BS((R, LANES), lambda j: (0, c0 + j)), BS((R, LANES), lambda j: (0, c0 + nb + j)), col,
                  BS((B_CONV, LANES), lambda j: (0, j))],
        out_specs=(col, col, BS((B_CONV, LANES), lambda j: (0, j)), vec, vec, vec),
        scratch_shapes=[pltpu.VMEM((nrow, LANES), F32), pltpu.VMEM((nrow, LANES), F32)], compiler_params=_cp(("parallel",)),
    )(z, z, dhc, conv_w)


def _ln_silu_fwd(hc, ln_g, ln_b, tb, name):
    R, W = hc.shape

    def body(x_ref, g_ref, b_ref, o_ref):
        xh, _ = _ln_stats(x_ref[...])
        y = xh * g_ref[...] + b_ref[...]
        o_ref[...] = (y * _sigmoid(y)).astype(BF16)

    vec = BS((1, W), lambda i: (0, 0))
    row = BS((tb, W), lambda i: (i, 0))
    return pl.pallas_call(
        body, name=name, out_shape=jax.ShapeDtypeStruct((R, W), BF16), grid=(R // tb,), in_specs=[row, vec, vec], out_specs=row,
        compiler_params=_cp(("parallel",)),
    )(hc, ln_g.reshape(1, W), ln_b.reshape(1, W))


def _ln_silu_bwd(hc, dyb, col0, ln_g, ln_b, tb, name):
    R, W = hc.shape
    c0 = col0 // W

    def body(x_ref, dy_ref, g_ref, b_ref, dx_ref, dg_ref, db_ref):
        i = pl.program_id(0)
        xh, r = _ln_stats(x_ref[...])
        g = g_ref[...]
        y = xh * g + b_ref[...]
        s = _sigmoid(y)
        dy = dy_ref[...].astype(F32) * s * (1.0 + y * (1.0 - s))
        _accum(dg_ref, jnp.sum(dy * xh, axis=0, keepdims=True), i == 0)
        _accum(db_ref, jnp.sum(dy, axis=0, keepdims=True), i == 0)
        t = dy * g
        dx_ref[...] = r * (t - jnp.mean(t, axis=-1, keepdims=True) - xh * jnp.mean(t * xh, axis=-1, keepdims=True))

    vec = BS((1, W), lambda i: (0, 0))
    row = BS((tb, W), lambda i: (i, 0))
    return pl.pallas_call(
        body, name=name,
        out_shape=(jax.ShapeDtypeStruct((R, W), F32), jax.ShapeDtypeStruct((1, W), F32), jax.ShapeDtypeStruct((1, W), F32)),
        grid=(R // tb,), in_specs=[row, BS((tb, W), lambda i: (i, c0)), vec, vec], out_specs=(row, vec, vec),
        compiler_params=_cp(("arbitrary",)),
    )(hc, dyb, ln_g.reshape(1, W), ln_b.reshape(1, W))


def _ffn_act_fwd(zf, conv_w, conv_b, L, ch, name):
    R = zf.shape[0]
    DFF = conv_w.shape[1]
    nb = DFF // LANES

    def body(g_ref, u_ref, w_ref, b_ref, o_ref, scr):
        _zero_pads(scr, R, L)

        def fill(r0, s0, c):
            scr[pl.ds(s0, c), :] = g_ref[pl.ds(r0, c), :].astype(F32)

        _for_chunks(R, L, ch, fill)

        def act(r0, s0, c):
            gc = _taps(scr, s0, c, w_ref, FFN_CONV, False) + b_ref[...]
            o_ref[pl.ds(r0, c), :] = (gc * _sigmoid(gc) * u_ref[pl.ds(r0, c), :].astype(F32)).astype(BF16)

        _for_chunks(R, L, ch, act)

    return pl.pallas_call(
        body, name=name, out_shape=jax.ShapeDtypeStruct((R, DFF), BF16), grid=(nb,),
        in_specs=[BS((R, LANES), lambda j: (0, j)), BS((R, LANES), lambda j: (0, nb + j)), BS((FFN_CONV, LANES), lambda j: (0, j)),
                  BS((1, LANES), lambda j: (0, j))],
        out_specs=BS((R, LANES), lambda j: (0, j)), scratch_shapes=[pltpu.VMEM((_scr_rows(R, L), LANES), F32)],
        compiler_params=_cp(("parallel",)),
    )(zf, zf, conv_w, conv_b.reshape(1, DFF))


def _ffn_act_bwd(zf, df, conv_w, conv_b, L, ch, name):
    R = zf.shape[0]
    DFF = conv_w.shape[1]
    nb = DFF // LANES

    def body(g_ref, u_ref, df_ref, w_ref, b_ref, dg_ref, du_ref, dw_ref, db_ref, scr_g, scr_d):
        _zero_pads(scr_g, R, L)
        _zero_pads(scr_d, R, L)
        dw_ref[...] = jnp.zeros_like(dw_ref)
        db_ref[...] = jnp.zeros_like(db_ref)

        def fill(r0, s0, c):
            scr_g[pl.ds(s0, c), :] = g_ref[pl.ds(r0, c), :].astype(F32)

        _for_chunks(R, L, ch, fill)

        def pre(r0, s0, c):
            gc = _taps(scr_g, s0, c, w_ref, FFN_CONV, False) + b_ref[...]
            s = _sigmoid(gc)
            d = df_ref[pl.ds(r0, c), :].astype(F32)
            du_ref[pl.ds(r0, c), :] = (d * gc * s).astype(BF16)
            dgc = d * u_ref[pl.ds(r0, c), :].astype(F32) * s * (1.0 + gc * (1.0 - s))
            scr_d[pl.ds(s0, c), :] = dgc
            db_ref[...] += jnp.sum(dgc, axis=0, keepdims=True)
            _tap_grads(scr_g, s0, c, dgc, dw_ref, FFN_CONV)

        _for_chunks(R, L, ch, pre)

        def back(r0, s0, c):
            dg_ref[pl.ds(r0, c), :] = _taps(scr_d, s0, c, w_ref, FFN_CONV, True).astype(BF16)

        _for_chunks(R, L, ch, back)

    col = BS((R, LANES), lambda j: (0, j))
    vec = BS((1, LANES), lambda j: (0, j))
    nrow = _scr_rows(R, L)
    return pl.pallas_call(
        body, name=name,
        out_shape=(jax.ShapeDtypeStruct((R, DFF), BF16), jax.ShapeDtypeStruct((R, DFF), BF16), jax.ShapeDtypeStruct((FFN_CONV, DFF), F32),
                   jax.ShapeDtypeStruct((1, DFF), F32)),
        grid=(nb,),
        in_specs=[col, BS((R, LANES), lambda j: (0, nb + j)), col, BS((FFN_CONV, LANES), lambda j: (0, j)), vec],
        out_specs=(col, col, BS((FFN_CONV, LANES), lambda j: (0, j)), vec),
        scratch_shapes=[pltpu.VMEM((nrow, LANES), F32), pltpu.VMEM((nrow, LANES), F32)], compiler_params=_cp(("parallel",)),
    )(zf, zf, df, conv_w, conv_b.reshape(1, DFF))


def _rope_tables(L, T):
    rows = L // GRID_W
    row = jnp.repeat(jnp.arange(rows, dtype=F32), GRID_W)
    col = jnp.tile(jnp.arange(GRID_W, dtype=F32), rows)
    n_freq = ROPE // 4
    inv = ROPE_THETA ** (-jnp.arange(n_freq, dtype=F32) / n_freq)
    ang = jnp.concatenate([row[:, None] * inv, col[:, None] * inv], axis=-1)
    cos, sin = jnp.cos(ang), jnp.sin(ang)
    half = ROPE // 2
    zero = jnp.zeros((L, half), F32)
    cos_t = jnp.concatenate([cos, cos, jnp.ones((L, LANES - ROPE), F32)], axis=1)
    sa = jnp.concatenate([zero, sin, zero, zero], axis=1)
    sb = jnp.concatenate([-sin, zero, zero, zero], axis=1)
    pad = T - L
    cos_t = jnp.concatenate([cos_t, jnp.ones((pad, LANES), F32)], axis=0)
    sa = jnp.concatenate([sa, jnp.zeros((pad, LANES), F32)], axis=0)
    sb = jnp.concatenate([sb, jnp.zeros((pad, LANES), F32)], axis=0)
    return cos_t, sa, sb


def _rope(x, cos, sa, sb):
    half = ROPE // 2
    return x * cos + pltpu.roll(x, half, 1) * sa + pltpu.roll(x, LANES - half, 1) * sb


def _rope_t(d, cos, sa, sb):
    half = ROPE // 2
    return d * cos + pltpu.roll(d * sa, LANES - half, 1) + pltpu.roll(d * sb, half, 1)


def _mla_prep_fwd(zm, qg, kvg, tabs, tb, name):
    T, W = zm.shape
    QL, KVL = qg.shape[-1], kvg.shape[-1]

    def body(z_ref, qg_ref, kg_ref, cos_ref, sa_ref, sb_ref, q_ref, k_ref, p_ref):
        cq = z_ref[:, :QL]
        r = lax.rsqrt(jnp.mean(cq * cq, axis=-1, keepdims=True) + EPS)
        q_ref[...] = (cq * r * qg_ref[...]).astype(BF16)
        ck = z_ref[:, QL:QL + KVL]
        r = lax.rsqrt(jnp.mean(ck * ck, axis=-1, keepdims=True) + EPS)
        k_ref[...] = (ck * r * kg_ref[...]).astype(BF16)
        p_ref[...] = _rope(z_ref[:, QL + KVL:], cos_ref[...], sa_ref[...], sb_ref[...]).astype(BF16)

    tab = BS((tb, LANES), lambda i: (i, 0))
    return pl.pallas_call(
        body, name=name,
        out_shape=(jax.ShapeDtypeStruct((T, QL), BF16), jax.ShapeDtypeStruct((T, KVL), BF16), jax.ShapeDtypeStruct((T, LANES), BF16)),
        grid=(T // tb,),
        in_specs=[BS((tb, W), lambda i: (i, 0)), BS((1, QL), lambda i: (0, 0)), BS((1, KVL), lambda i: (0, 0)), tab, tab, tab],
        out_specs=(BS((tb, QL), lambda i: (i, 0)), BS((tb, KVL), lambda i: (i, 0)), tab), compiler_params=_cp(("parallel",)),
    )(zm, qg.reshape(1, QL), kvg.reshape(1, KVL), *tabs)


def _mla_prep_bwd(zm, qg, kvg, tabs, dq, dk, dp, tb, name):
    T, W = zm.shape
    QL, KVL = qg.shape[-1], kvg.shape[-1]

    def body(z_ref, qg_ref, kg_ref, cos_ref, sa_ref, sb_ref, dq_ref, dk_ref, dp_ref, dz_ref, dqg_ref, dkg_ref):
        i = pl.program_id(0)

        def rms_bwd(x, g, dy):
            r = lax.rsqrt(jnp.mean(x * x, axis=-1, keepdims=True) + EPS)
            xh = x * r
            t = dy * g
            return r * (t - xh * jnp.mean(t * xh, axis=-1, keepdims=True)), jnp.sum(dy * xh, axis=0, keepdims=True)

        dcq, dg = rms_bwd(z_ref[:, :QL], qg_ref[...], dq_ref[...].astype(F32))
        dz_ref[:, :QL] = dcq.astype(BF16)
        _accum(dqg_ref, dg, i == 0)
        dck, dg = rms_bwd(z_ref[:, QL:QL + KVL], kg_ref[...], dk_ref[...].astype(F32))
        dz_ref[:, QL:QL + KVL] = dck.astype(BF16)
        _accum(dkg_ref, dg, i == 0)
        dz_ref[:, QL + KVL:] = _rope_t(dp_ref[...], cos_ref[...], sa_ref[...], sb_ref[...]).astype(BF16)

    tab = BS((tb, LANES), lambda i: (i, 0))
    return pl.pallas_call(
        body, name=name,
        out_shape=(jax.ShapeDtypeStruct((T, W), BF16), jax.ShapeDtypeStruct((1, QL), F32), jax.ShapeDtypeStruct((1, KVL), F32)),
        grid=(T // tb,),
        in_specs=[BS((tb, W), lambda i: (i, 0)), BS((1, QL), lambda i: (0, 0)), BS((1, KVL), lambda i: (0, 0)), tab, tab, tab,
                  BS((tb, QL), lambda i: (i, 0)), BS((tb, KVL), lambda i: (i, 0)), tab],
        out_specs=(BS((tb, W), lambda i: (i, 0)), BS((1, QL), lambda i: (0, 0)), BS((1, KVL), lambda i: (0, 0))),
        compiler_params=_cp(("arbitrary",)),
    )(zm, qg.reshape(1, QL), kvg.reshape(1, KVL), *tabs, dq, dk, dp)


def _attn_fwd(q, kv, kpe, tabs, L, tq, name, side=None):
    T = kv.shape[0]
    H = kv.shape[1] // QHEAD
    scale = (HEAD + ROPE) ** -0.5

    def body(q_ref, kv_ref, kpe_ref, cos_ref, sa_ref, sb_ref, o_ref, lse_ref, kcat):
        @pl.when(pl.program_id(1) == 0)
        def _():
            kcat[:, :HEAD] = kv_ref[:, :HEAD]
            kcat[:, HEAD:] = kpe_ref[...]

        sub = tq // 2
        for r in range(2):
            rs = slice(r * sub, (r + 1) * sub)
            qp = _rope(q_ref[rs, HEAD:].astype(F32), cos_ref[rs, :], sa_ref[rs, :], sb_ref[rs, :]).astype(BF16)
            qc = jnp.concatenate([q_ref[rs, :HEAD], qp], axis=1)
            s = lax.dot_general(qc, kcat[...], _DIMS["nt"], preferred_element_type=F32)
            m = jnp.max(s, axis=-1, keepdims=True)
            p = jnp.exp2((s - m) * (scale * math.log2(math.e)))
            l = jnp.sum(p, axis=-1, keepdims=True)
            o = jnp.dot(p.astype(BF16), kv_ref[:, HEAD:], preferred_element_type=F32)
            o_ref[rs, :] = (o / l).astype(BF16)
            lse_ref[0, rs, :] = m * scale + jnp.log(l)

    tab = BS((tq, LANES), lambda h, i: (i, 0))
    return _call(
        body, name=name, out_shape=(jax.ShapeDtypeStruct((L, H * HEAD), BF16), jax.ShapeDtypeStruct((H, L, 1), F32)),
        grid=(H, L // tq),
        in_specs=[BS((tq, QHEAD), lambda h, i: (i, h)), BS((T, QHEAD), lambda h, i: (0, h)), BS((T, LANES), lambda h, i: (0, 0)),
                  tab, tab, tab],
        out_specs=(BS((tq, HEAD), lambda h, i: (i, h)), BS((1, tq, 1), lambda h, i: (h, i, 0))),
        scratch=[pltpu.VMEM((T, QHEAD), BF16)], sem=("parallel", "arbitrary"), operands=(q, kv, kpe, *tabs), side=side)


def _attn_bwd(q, kv, kpe, tabs, o, lse, do, L, tq, name, side=None):
    T = kv.shape[0]
    H = kv.shape[1] // QHEAD
    scale = (HEAD + ROPE) ** -0.5
    nq = L // tq

    def body(q_ref, kv_ref, kpe_ref, cos_ref, sa_ref, sb_ref, o_ref, lse_ref, do_ref, dq_ref, dkv_ref, dkpe_ref, kcat, dk_acc, dv_acc):
        h, i = pl.program_id(0), pl.program_id(1)

        @pl.when(i == 0)
        def _():
            kcat[:, :HEAD] = kv_ref[:, :HEAD]
            kcat[:, HEAD:] = kpe_ref[...]
            dk_acc[...] = jnp.zeros_like(dk_acc)
            dv_acc[...] = jnp.zeros_like(dv_acc)

        cos, sa, sb = cos_ref[...], sa_ref[...], sb_ref[...]
        qp = _rope(q_ref[:, HEAD:].astype(F32), cos, sa, sb).astype(BF16)
        qc = jnp.concatenate([q_ref[:, :HEAD], qp], axis=1)
        s = lax.dot_general(qc, kcat[...], _DIMS["nt"], preferred_element_type=F32) * scale
        p = jnp.exp(s - lse_ref[0])
        dov = do_ref[...]
        delta = jnp.sum(dov.astype(F32) * o_ref[...].astype(F32), axis=-1, keepdims=True)
        dp = lax.dot_general(dov, kv_ref[:, HEAD:], _DIMS["nt"], preferred_element_type=F32)
        ds = (p * (dp - delta) * scale).astype(BF16)
        dqc = jnp.dot(ds, kcat[...], preferred_element_type=F32)
        dq_ref[:, :HEAD] = dqc[:, :HEAD].astype(BF16)
        dq_ref[:, HEAD:] = _rope_t(dqc[:, HEAD:], cos, sa, sb).astype(BF16)
        dk_acc[...] += lax.dot_general(ds, qc, _DIMS["tn"], preferred_element_type=F32)
        dv_acc[...] += lax.dot_general(p.astype(BF16), dov, _DIMS["tn"], preferred_element_type=F32)

        @pl.when(i == nq - 1)
        def _():
            dkv_ref[:, :HEAD] = dk_acc[:, :HEAD].astype(BF16)
            dkv_ref[:, HEAD:] = dv_acc[...].astype(BF16)

            @pl.when(h == 0)
            def _():
                dkpe_ref[...] = dk_acc[:, HEAD:]

            @pl.when(h > 0)
            def _():
                dkpe_ref[...] += dk_acc[:, HEAD:]

    tab = BS((tq, LANES), lambda h, i: (i, 0))
    return _call(
        body, name=name,
        out_shape=(jax.ShapeDtypeStruct((L, H * QHEAD), BF16), jax.ShapeDtypeStruct((T, H * QHEAD), BF16), jax.ShapeDtypeStruct((T, LANES), F32)),
        grid=(H, nq),
        in_specs=[BS((tq, QHEAD), lambda h, i: (i, h)), BS((T, QHEAD), lambda h, i: (0, h)), BS((T, LANES), lambda h, i: (0, 0)),
                  tab, tab, tab, BS((tq, HEAD), lambda h, i: (i, h)), BS((1, tq, 1), lambda h, i: (h, i, 0)),
                  BS((tq, HEAD), lambda h, i: (i, h))],
        out_specs=(BS((tq, QHEAD), lambda h, i: (i, h)), BS((T, QHEAD), lambda h, i: (0, h)), BS((T, LANES), lambda h, i: (0, 0))),
        scratch=[pltpu.VMEM((T, QHEAD), BF16), pltpu.VMEM((T, QHEAD), F32), pltpu.VMEM((T, HEAD), F32)],
        sem=("arbitrary", "arbitrary"), operands=(q, kv, kpe, *tabs, o, lse, do), side=side)


def _adamw(w, g, m, v, name):
    R, C = w.shape
    tr = _row_tile(R, C)
    c1 = 1.0 / (1.0 - ADAM_B1 ** ADAM_STEP)
    c2 = 1.0 / (1.0 - ADAM_B2 ** ADAM_STEP)

    def body(w_ref, g_ref, m_ref, v_ref, d_ref, nm_ref, nv_ref):
        g_ = g_ref[...]
        nm = ADAM_B1 * m_ref[...] + (1.0 - ADAM_B1) * g_
        nv = ADAM_B2 * v_ref[...] + (1.0 - ADAM_B2) * (g_ * g_)
        nm_ref[...] = nm
        nv_ref[...] = nv
        d_ref[...] = -ADAM_LR * ((nm * c1) / (jnp.sqrt(nv * c2) + ADAM_EPS) + ADAM_WD * w_ref[...])

    blk = BS((tr, C), lambda i: (i, 0))
    sd = jax.ShapeDtypeStruct((R, C), F32)
    return pl.pallas_call(body, name=name, out_shape=(sd, sd, sd), grid=(R // tr,), in_specs=[blk] * 4, out_specs=(blk,) * 3,
                          compiler_params=_cp(("parallel",)))(w, g, m, v)


def _sum_lead(a, out_dtype, name):
    n, R, C = a.shape
    tr = _row_tile(R, C * n, 2 << 20)

    def body(a_ref, o_ref):
        acc = a_ref[0].astype(F32)
        for k in range(1, n):
            acc = acc + a_ref[k].astype(F32)
        o_ref[...] = acc.astype(out_dtype)

    return pl.pallas_call(body, name=name, out_shape=jax.ShapeDtypeStruct((R, C), out_dtype), grid=(R // tr,),
                          in_specs=[BS((n, tr, C), lambda i: (0, i, 0))], out_specs=BS((tr, C), lambda i: (i, 0)),
                          compiler_params=_cp(("parallel",)))(a)


def _cctx_grad(parts, c_ctx, name):
    n, D = parts.shape

    def body(p_ref, c_ref, o_ref):
        d = jnp.sum(p_ref[...], axis=0, keepdims=True)
        c = c_ref[...]
        s = _sigmoid(c)
        o_ref[...] = d * s * (1.0 + c * (1.0 - s))

    return pl.pallas_call(body, name=name, out_shape=jax.ShapeDtypeStruct((1, D), F32))(parts, c_ctx.reshape(1, D))


def _me():
    return lax.axis_index("x"), lax.axis_index("y"), lax.axis_index("c")


def _aligned(v, n):
    return v if isinstance(v, int) else pl.multiple_of(v, n)


def _window(ref, r0, c0, R, C):
    rows = pl.ds(_aligned(r0, SUBLANES), R)
    if C == ref.shape[1]:
        return ref.at[rows, :]
    return ref.at[rows, pl.ds(_aligned(c0, LANES), C)]


def _allgather8(items, name):
    n = len(items)

    def body(*refs):
        srcs, dsts = refs[:n], refs[n:2 * n]
        send_sems, recv_sems, local_sems = refs[2 * n:]
        x, y, c = _me()
        me, sibling = (x, y, c), (x, y, 1 - c)
        chips = [(1 - x, y), (x, 1 - y), (1 - x, 1 - y)]

        def dwin(a, blk):
            (R, C), at = items[a][2], items[a][4]
            return _window(dsts[a], *at(*blk), R, C)

        def swin(a):
            (R, C), at = items[a][2], items[a][1]
            return _window(srcs[a], *at(*me), R, C)

        def copy(a, k, blk, to, src=None):
            return pltpu.make_async_remote_copy(
                src_ref=dwin(a, blk) if src is None else src, dst_ref=dwin(a, blk), send_sem=send_sems.at[7 * a + k],
                recv_sem=recv_sems.at[7 * a + k], device_id=to, device_id_type=MESH)

        mine = [pltpu.make_async_copy(swin(a), dwin(a, me), local_sems.at[a]) for a in range(n)]
        for cp in mine:
            cp.start()
        first = []
        for a in range(n):
            first.append(copy(a, 0, me, sibling, src=swin(a)))
            first += [copy(a, 1 + j, me, (*chip, c), src=swin(a)) for j, chip in enumerate(chips)]
        for cp in first:
            cp.start()
        passed = []
        for j, chip in enumerate(chips):
            for a in range(n):
                copy(a, 1 + j, (*chip, c), me).wait_recv()
                fwd = copy(a, 4 + j, (*chip, c), sibling)
                fwd.start()
                passed.append(fwd)
        for a in range(n):
            copy(a, 0, sibling, me).wait_recv()
            for j, chip in enumerate(chips):
                copy(a, 4 + j, (*chip, 1 - c), me).wait_recv()
        for cp in first + passed:
            cp.wait_send()
        for cp in mine:
            cp.wait()

    outs = pl.pallas_call(
        body, name=name, out_shape=tuple(jax.ShapeDtypeStruct(it[3], it[0].dtype) for it in items),
        in_specs=[ANY] * n, out_specs=tuple([ANY] * n),
        scratch_shapes=[pltpu.SemaphoreType.DMA((7 * n,)), pltpu.SemaphoreType.DMA((7 * n,)), pltpu.SemaphoreType.DMA((n,))],
    )(*[it[0] for it in items])
    return list(outs)


def _cast_place(w, kind, name, lead=None):
    R2, C = w.shape[-2:]
    tr = _row_tile(R2, C)
    nr = R2 // tr
    xi, yi, _ = _me()
    p_arr = (2 * xi + yi).astype(jnp.int32).reshape(1)
    if kind == "col":
        shape, o_spec = (R2, 4 * C), BS((tr, C), lambda i, p: (i, p[0]))
    else:
        shape, o_spec = (4 * R2, C), BS((tr, C), lambda i, p: (p[0] * nr + i, 0))

    def body(p_ref, w_ref, o_ref):
        o_ref[...] = w_ref[...].astype(BF16)

    return pl.pallas_call(
        body, name=name, out_shape=jax.ShapeDtypeStruct(shape, BF16),
        grid_spec=pltpu.PrefetchScalarGridSpec(
            num_scalar_prefetch=1, grid=(nr,), out_specs=o_spec,
            in_specs=[BS((tr, C), lambda i, p: (i, 0)) if lead is None else BS((None, tr, C), lambda i, p: (lead, i, 0))]),
        compiler_params=_cp(("parallel",)),
    )(p_arr, w)


def _remote(src, dst, sends, recvs, k, to):
    return pltpu.make_async_remote_copy(src_ref=src, dst_ref=dst, send_sem=sends.at[k], recv_sem=recvs.at[k], device_id=to,
                                        device_id_type=MESH)


def _gather_side(bufs, pieces):
    def win(b, piece, p, pc):
        bi, (R, C), at, k, i = piece
        r0, c0 = at(p, pc)
        return _window(b[bi], r0 + i * (R // k), c0, R // k, C)

    def chips_of(x, y):
        return [(1 - x, y), (x, 1 - y), (1 - x, 1 - y)]

    def outgoing(b, sends, recvs):
        x, y, c = _me()
        cps = []
        for m, piece in enumerate(pieces):
            mine = win(b, piece, 2 * x + y, c)
            cps += [_remote(mine, mine, sends, recvs, 6 * m + j, (px, py, c)) for j, (px, py) in enumerate(chips_of(x, y))]
        return cps

    def start(b, sends, recvs):
        for cp in outgoing(b, sends, recvs):
            cp.start()

    def finish(b, sends, recvs):
        x, y, c = _me()
        sibling = (x, y, 1 - c)
        passed = []
        for j, (px, py) in enumerate(chips_of(x, y)):
            for m, piece in enumerate(pieces):
                landed = win(b, piece, 2 * px + py, c)
                _remote(landed, landed, sends, recvs, 6 * m + j, (px, py, c)).wait_recv()
                fwd = _remote(landed, landed, sends, recvs, 6 * m + 3 + j, sibling)
                fwd.start()
                passed.append(fwd)
        for j, (px, py) in enumerate(chips_of(x, y)):
            for m, piece in enumerate(pieces):
                theirs = win(b, piece, 2 * px + py, 1 - c)
                _remote(theirs, theirs, sends, recvs, 6 * m + 3 + j, sibling).wait_recv()
        for cp in outgoing(b, sends, recvs) + passed:
            cp.wait_send()

    return _Side(bufs, 6 * len(pieces), start, finish)


_RELS = [(dx, dy, dc) for dx in (0, 1) for dy in (0, 1) for dc in (0, 1)][1:]


def _rs_side(bufs, pieces):
    def flip(v, d):
        return 1 - v if d else v

    def copies(b, sends, recvs):
        x, y, c = _me()
        dev = 4 * x + 2 * y + c
        cps, lands = [], []
        for m, (gi, ri, (R, C), at, k, i) in enumerate(pieces):
            rows = R // k
            for t, (dx, dy, dc) in enumerate(_RELS):
                tx, ty, tc = flip(x, dx), flip(y, dy), flip(c, dc)
                r0, c0 = at(2 * tx + ty, tc)
                src = _window(b[gi], r0 + i * rows, c0, rows, C)
                cps.append(_remote(src, b[ri].at[dev, pl.ds(i * rows, rows), :], sends, recvs, 7 * m + t, (tx, ty, tc)))
                theirs = b[ri].at[4 * tx + 2 * ty + tc, pl.ds(i * rows, rows), :]
                lands.append(_remote(theirs, theirs, sends, recvs, 7 * m + t, (tx, ty, tc)))
        return cps, lands

    def start(b, sends, recvs):
        for cp in copies(b, sends, recvs)[0]:
            cp.start()

    def finish(b, sends, recvs):
        cps, lands = copies(b, sends, recvs)
        for cp in lands:
            cp.wait_recv()
        for cp in cps:
            cp.wait_send()

    return _Side(bufs, 7 * len(pieces), start, finish)


def _comm_only(side, name):
    n = len(side.bufs)

    def body(*refs):
        bufs, (sends, recvs) = refs[n:2 * n], refs[2 * n:]
        side.start(bufs, sends, recvs)
        side.finish(bufs, sends, recvs)

    outs = pl.pallas_call(
        body, name=name, out_shape=tuple(jax.ShapeDtypeStruct(b.shape, b.dtype) for b in side.bufs),
        in_specs=[ANY] * n, out_specs=tuple([ANY] * n), input_output_aliases={a: a for a in range(n)},
        scratch_shapes=[pltpu.SemaphoreType.DMA((side.nsem,)), pltpu.SemaphoreType.DMA((side.nsem,))],
    )(*side.bufs)
    return list(outs)


def _rs_sum8(grad, recv, win, kind, name):
    R, C = win
    tr = _row_tile(R, 4 * C)
    nr = R // tr
    xi, yi, ci = _me()
    s_arr = jnp.stack([4 * xi + 2 * yi + ci, ci]).astype(jnp.int32)

    if kind == "col":
        g_spec = BS((tr, C), lambda i, s: (s[1] * nr + i, s[0] // 2))
    else:
        g_spec = BS((tr, C), lambda i, s: (s[0] * nr + i, 0))

    def body(s_ref, g_ref, *refs):
        acc = g_ref[...].astype(F32)
        for r_ref in refs[:7]:
            acc = acc + r_ref[0].astype(F32)
        refs[7][...] = acc

    def other(t):
        return BS((1, tr, C), lambda i, s: (jnp.bitwise_xor(s[0], t), i, 0))

    return pl.pallas_call(
        body, name=name, out_shape=jax.ShapeDtypeStruct((2 * R, C), F32),
        grid_spec=pltpu.PrefetchScalarGridSpec(
            num_scalar_prefetch=1, grid=(nr,), in_specs=[g_spec] + [other(t) for t in range(1, 8)],
            out_specs=BS((tr, C), lambda i, s: (s[1] * nr + i, 0))),
        compiler_params=_cp(("parallel",)),
    )(s_arr, grad, *([recv] * 7))


def _rs_share(shards, name):
    n = len(shards)

    def body(*refs):
        bufs = refs[n:2 * n]
        send_sems, recv_sems = refs[2 * n:]
        x, y, c = _me()
        sibling = (x, y, 1 - c)
        remote = []
        for a in range(n):
            R = bufs[a].shape[0] // 2
            mine = bufs[a].at[pl.ds(pl.multiple_of(c * R, SUBLANES), R), :]
            cp = pltpu.make_async_remote_copy(src_ref=mine, dst_ref=mine, send_sem=send_sems.at[a], recv_sem=recv_sems.at[a],
                                              device_id=sibling, device_id_type=MESH)
            cp.start()
            remote.append(cp)
        for a in range(n):
            R = bufs[a].shape[0] // 2
            theirs = bufs[a].at[pl.ds(pl.multiple_of((1 - c) * R, SUBLANES), R), :]
            pltpu.make_async_remote_copy(src_ref=theirs, dst_ref=theirs, send_sem=send_sems.at[a], recv_sem=recv_sems.at[a],
                                         device_id=sibling, device_id_type=MESH).wait_recv()
        for cp in remote:
            cp.wait_send()

    outs = pl.pallas_call(
        body, name=name, out_shape=tuple(jax.ShapeDtypeStruct(h.shape, h.dtype) for h in shards),
        in_specs=[ANY] * n, out_specs=tuple([ANY] * n), input_output_aliases={a: a for a in range(n)},
        scratch_shapes=[pltpu.SemaphoreType.DMA((n,)), pltpu.SemaphoreType.DMA((n,))],
    )(*shards)
    return list(outs)


BLOB_ALIGN = SUBLANES * LANES


def _pack(arrs):
    flat = jnp.concatenate([a.reshape(-1).astype(F32) for a in arrs])
    n = flat.shape[0]
    padded = -(-n // BLOB_ALIGN) * BLOB_ALIGN
    return jnp.pad(flat, (0, padded - n)).reshape(padded // LANES, LANES)


def _unpack(flat, shapes):
    out, off = [], 0
    for s in shapes:
        n = math.prod(s)
        out.append(flat[..., off:off + n].reshape(flat.shape[:-1] + tuple(s)))
        off += n
    return out


def _gather_blob(blob, name):
    r = blob.shape[0]
    at = lambda px, py, pc: ((4 * px + 2 * py + pc) * r, 0)
    (out,) = _allgather8([(blob, lambda px, py, pc: (0, 0), (r, LANES), (8 * r, LANES), at)], name)
    return out.reshape(8, r * LANES)


def _conv_ffn_fwd(mm, full, layer, X, mods, n2g, conv_w, conv_b, L, tb, nlat, ch, tag):
    sh2, sc2, g2 = mods[3], mods[4], mods[5]
    h2 = _norm_mod_fwd(X, n2g, sh2, sc2, tb, nlat, tag + "_norm2")
    zf = mm(h2, full[f"ffn_w_up{layer}"], "nn", BF16, tag + "_up", tm=(544, 512), tn=(1408, 512))
    f = _ffn_act_fwd(zf, conv_w, conv_b, L, ch, tag + "_act")
    yf = mm(f, full[f"ffn_w_down{layer}"], "nn", F32, tag + "_down", tm=(544, 512), tn=(512,))
    Xn = _gate_res_fwd(X, yf, g2, tb, nlat, tag + "_res2")
    return Xn, (X, h2, zf, f, yf)


def _conv_ffn_bwd(mm, full, gbuf, layer, dXn, saved, mods, n2g, conv_w, conv_b, L, tb, nlat, ch, tag):
    X, h2, zf, f, yf = saved
    sc2, g2 = mods[4], mods[5]
    w_up, w_down = full[f"ffn_w_up{layer}"], full[f"ffn_w_down{layer}"]
    dy, dg2 = _gate_bwd(dXn, yf, g2, tb, nlat, tag + "_dres2")
    df = mm(dy, w_down, "nt", BF16, tag + "_ddown_x", tm=(544, 512), tn=(1408, 512))
    gbuf[f"ffn_w_down{layer}"] = mm(f, dy, "tn", BF16, tag + "_ddown_w", tm=(512,), tn=(1024, 512))
    dgp, du, dcw, dcb = _ffn_act_bwd(zf, df, conv_w, conv_b, L, ch, tag + "_dact")
    dzf = jnp.concatenate([dgp, du], axis=1)
    dh2 = mm(dzf, w_up, "nt", BF16, tag + "_dup_x", tm=(1088, 1024, 512), tn=(1024,), tk=(2816, 512))
    gbuf[f"ffn_w_up{layer}"] = mm(h2, dzf, "tn", BF16, tag + "_dup_w", tm=(512,), tn=(1024, 512))
    dX, dn2g, dsh2, dsc2 = _norm_mod_bwd(X, n2g, sc2, dh2, dXn, tb, nlat, tag + "_dnorm2")
    return dX, dict(n2g=dn2g, sh2=dsh2, sc2=dsc2, g2=dg2, cw=dcw, cb=dcb)


def kernel(x, c, ctx, c_ctx, norm1_g, norm2_g, w_ada, b_ada, ab_w_in, ab_b_in, a_ln_g, a_ln_b, a_w_s, a_b_s, b_conv_w, b_conv_b, b_ln_g, b_ln_b, ab_w_out, mla_w_in, mla_q_norm_g, mla_w_uq, mla_kv_norm_g, mla_w_ukv, mla_w_o, ffn_w_up, ffn_conv_w, ffn_conv_b, ffn_w_down, final_norm_g, loss_target, m_c_ctx, m_norm1_g, m_norm2_g, m_w_ada, m_b_ada, m_ab_w_in, m_ab_b_in, m_a_ln_g, m_a_ln_b, m_a_w_s, m_a_b_s, m_b_conv_w, m_b_conv_b, m_b_ln_g, m_b_ln_b, m_ab_w_out, m_mla_w_in, m_mla_q_norm_g, m_mla_w_uq, m_mla_kv_norm_g, m_mla_w_ukv, m_mla_w_o, m_ffn_w_up, m_ffn_conv_w, m_ffn_conv_b, m_ffn_w_down, m_final_norm_g, v_c_ctx, v_norm1_g, v_norm2_g, v_w_ada, v_b_ada, v_ab_w_in, v_ab_b_in, v_a_ln_g, v_a_ln_b, v_a_w_s, v_a_b_s, v_b_conv_w, v_b_conv_b, v_b_ln_g, v_b_ln_b, v_ab_w_out, v_mla_w_in, v_mla_q_norm_g, v_mla_w_uq, v_mla_kv_norm_g, v_mla_w_ukv, v_mla_w_o, v_ffn_w_up, v_ffn_conv_w, v_ffn_conv_b, v_ffn_w_down, v_final_norm_g):
    W = dict(c_ctx=c_ctx, norm1_g=norm1_g, norm2_g=norm2_g, w_ada=w_ada, b_ada=b_ada, ab_w_in=ab_w_in, ab_b_in=ab_b_in, a_ln_g=a_ln_g,
             a_ln_b=a_ln_b, a_w_s=a_w_s, a_b_s=a_b_s, b_conv_w=b_conv_w, b_conv_b=b_conv_b, b_ln_g=b_ln_g, b_ln_b=b_ln_b,
             ab_w_out=ab_w_out, mla_w_in=mla_w_in, mla_q_norm_g=mla_q_norm_g, mla_w_uq=mla_w_uq, mla_kv_norm_g=mla_kv_norm_g,
             mla_w_ukv=mla_w_ukv, mla_w_o=mla_w_o, ffn_w_up=ffn_w_up, ffn_conv_w=ffn_conv_w, ffn_conv_b=ffn_conv_b,
             ffn_w_down=ffn_w_down, final_norm_g=final_norm_g)
    MOM = dict(c_ctx=m_c_ctx, norm1_g=m_norm1_g, norm2_g=m_norm2_g, w_ada=m_w_ada, b_ada=m_b_ada, ab_w_in=m_ab_w_in, ab_b_in=m_ab_b_in,
               a_ln_g=m_a_ln_g, a_ln_b=m_a_ln_b, a_w_s=m_a_w_s, a_b_s=m_a_b_s, b_conv_w=m_b_conv_w, b_conv_b=m_b_conv_b,
               b_ln_g=m_b_ln_g, b_ln_b=m_b_ln_b, ab_w_out=m_ab_w_out, mla_w_in=m_mla_w_in, mla_q_norm_g=m_mla_q_norm_g,
               mla_w_uq=m_mla_w_uq, mla_kv_norm_g=m_mla_kv_norm_g, mla_w_ukv=m_mla_w_ukv, mla_w_o=m_mla_w_o, ffn_w_up=m_ffn_w_up,
               ffn_conv_w=m_ffn_conv_w, ffn_conv_b=m_ffn_conv_b, ffn_w_down=m_ffn_w_down, final_norm_g=m_final_norm_g)
    VAR = dict(c_ctx=v_c_ctx, norm1_g=v_norm1_g, norm2_g=v_norm2_g, w_ada=v_w_ada, b_ada=v_b_ada, ab_w_in=v_ab_w_in, ab_b_in=v_ab_b_in,
               a_ln_g=v_a_ln_g, a_ln_b=v_a_ln_b, a_w_s=v_a_w_s, a_b_s=v_a_b_s, b_conv_w=v_b_conv_w, b_conv_b=v_b_conv_b,
               b_ln_g=v_b_ln_g, b_ln_b=v_b_ln_b, ab_w_out=v_ab_w_out, mla_w_in=v_mla_w_in, mla_q_norm_g=v_mla_q_norm_g,
               mla_w_uq=v_mla_w_uq, mla_kv_norm_g=v_mla_kv_norm_g, mla_w_ukv=v_mla_w_ukv, mla_w_o=v_mla_w_o, ffn_w_up=v_ffn_w_up,
               ffn_conv_w=v_ffn_conv_w, ffn_conv_b=v_ffn_conv_b, ffn_w_down=v_ffn_w_down, final_norm_g=v_final_norm_g)
    ORDER = list(W.keys())

    L, D = x.shape[1], x.shape[2]
    CT = ctx.shape[1]
    T = L + CT
    AW, BW = a_ln_g.shape[-1], b_ln_g.shape[-1]
    AH = a_w_s.shape[1]
    QL, KVL = 4 * mla_q_norm_g.shape[-1], 4 * mla_kv_norm_g.shape[-1]
    H = 4 * mla_w_o.shape[1] // HEAD
    HS = H // 4
    DFF = ffn_conv_b.shape[-1]
    NA = w_ada.shape[-1]
    tb = 256 if (L % 256 == 0 and CT % 256 == 0) else 128
    nlat = L // tb
    ch = tb
    tq = 256 if L >= 512 else 128
    xi, yi, ci = _me()
    p_me = 2 * xi + yi
    dev = 4 * xi + 2 * yi + ci

    shard_small = [c[0], mla_q_norm_g[0], mla_kv_norm_g[0], b_conv_w[0], ffn_conv_w]
    g0 = _gather_blob(_pack(shard_small), "gather_small")
    c_all, qg_s, kvg_s, bcw_s, fcw_s = _unpack(g0, [a.shape for a in shard_small])
    per_chip = lambda a: a[0::2]
    qg = per_chip(qg_s).reshape(QL)
    kvg = per_chip(kvg_s).reshape(KVL)
    bcw = jnp.concatenate(list(per_chip(bcw_s)), axis=-1)
    fcw = jnp.concatenate(list(per_chip(fcw_s)), axis=-1)
    c16 = jnp.concatenate([c_all, c_ctx[None], jnp.zeros((7, D), F32)], axis=0)

    ms = []
    for i in range(2):
        bias = lax.dynamic_slice(b_ada[i], (p_me * NA,), (NA,))
        ms.append(_mm(c16, w_ada, "nn", F32, f"ada{i}", tm=(16,), tn=(512,), bias=bias, silu_a=True, b_lead=i))
    ms = jnp.concatenate(ms, axis=0)
    (mods_all,) = _allgather8(
        [(ms, lambda px, py, pc: (pc * 16, 0), (16, NA), (32, 4 * NA), lambda px, py, pc: (pc * 16, (2 * px + py) * NA))], "gather_mods")
    mods_all = mods_all.reshape(2, 16, N_MOD, D)
    mods = []
    for i in range(2):
        lat = lax.dynamic_index_in_dim(mods_all[i], dev, axis=0, keepdims=False)
        both = jnp.stack([lat, mods_all[i, 8]], axis=0)
        mods.append([both[:, k][:, None, :] for k in range(N_MOD)])

    def pad_uq(w):
        w = w.reshape(w.shape[0], HS, HEAD + ROPE)
        return jnp.pad(w, ((0, 0), (0, 0), (0, QHEAD - HEAD - ROPE))).reshape(w.shape[0], HS * QHEAD)

    MI = QL + KVL + LANES
    big = [
        ("ab_w_in", ab_w_in, 0, "col"), ("ab_w_out", ab_w_out, 0, "row"),
        ("mla_w_in", jnp.pad(mla_w_in[0], ((0, 0), (0, MI - mla_w_in.shape[-1]))), None, "row"),
        ("mla_w_uq", pad_uq(mla_w_uq[0]), None, "col"), ("mla_w_ukv", mla_w_ukv, 0, "col"), ("mla_w_o", mla_w_o, 0, "row"),
        ("ffn_w_up0", ffn_w_up, 0, "col"), ("ffn_w_up1", ffn_w_up, 1, "col"),
        ("ffn_w_down0", ffn_w_down, 0, "row"), ("ffn_w_down1", ffn_w_down, 1, "row"),
    ]
    full, wins = {}, {}
    for nm, w, lead, kind in big:
        R, C = w.shape[-2] // 2, w.shape[-1]
        if kind == "col":
            wins[nm] = ((R, C), lambda p, pc, R=R, C=C: (pc * R, p * C), kind)
        else:
            wins[nm] = ((R, C), lambda p, pc, R=R, C=C: ((2 * p + pc) * R, 0), kind)
        full[nm] = _cast_place(w, kind, "cast_" + nm, lead)
    names = [b[0] for b in big]

    gather_plan = {
        "l0_in": [("ffn_w_up0", 2, 0)], "l0_out": [("ffn_w_up0", 2, 1)],
        "l0_ffn_up": [("ffn_w_down0", 1, 0), ("mla_w_in", 1, 0), ("mla_w_uq", 1, 0)],
        "l0_ffn_down": [("mla_w_ukv", 1, 0), ("mla_w_o", 1, 0)],
        "l1_attn": [("ffn_w_up1", 1, 0), ("ffn_w_down1", 1, 0)],
    }
    reduce_plan = {
        "l1_ffn_dup_x": [("ffn_w_down1", 1, 0)], "l1_dattn": [("ffn_w_up1", 1, 0), ("mla_w_o", 1, 0)],
        "l0_ffn_ddown_x": [("mla_w_ukv", 1, 0), ("mla_w_uq", 1, 0)], "l0_ffn_ddown_w": [("mla_w_in", 1, 0)],
        "l0_ffn_dup_x": [("ffn_w_down0", 2, 0)], "l0_ffn_dup_w": [("ffn_w_down0", 2, 1)],
        "l0_dout_x": [("ffn_w_up0", 4, 0)], "l0_dout_w": [("ffn_w_up0", 4, 1)], "l0_din_x": [("ffn_w_up0", 4, 2)],
        "l0_din_w": [("ffn_w_up0", 4, 3)],
    }
    gbuf, rbuf = {}, {}

    def gather_side(req):
        nms = list(dict.fromkeys(nm for nm, _, _ in req))
        side = _gather_side([full[nm] for nm in nms], [(nms.index(nm), wins[nm][0], wins[nm][1], k, i) for nm, k, i in req])

        def commit(bufs):
            full.update(zip(nms, bufs))

        return side, commit

    def reduce_side(req):
        nms = list(dict.fromkeys(nm for nm, _, _ in req))
        for nm in nms:
            if nm not in rbuf:
                rbuf[nm] = lax.empty((8,) + wins[nm][0], BF16)
        n = len(nms)
        side = _rs_side([gbuf[nm] for nm in nms] + [rbuf[nm] for nm in nms],
                        [(nms.index(nm), n + nms.index(nm), wins[nm][0], wins[nm][1], k, i) for nm, k, i in req])

        def commit(bufs):
            gbuf.update(zip(nms, bufs[:n]))
            rbuf.update(zip(nms, bufs[n:]))

        return side, commit

    def side_of(name):
        if name in gather_plan:
            return gather_side(gather_plan[name])
        if name in reduce_plan:
            return reduce_side(reduce_plan[name])
        return None, None

    def carried(fn, name_at):
        def run(*args, **kw):
            side, commit = side_of(args[name_at])
            if side is None:
                return fn(*args, **kw)
            out, bufs = fn(*args, side=side, **kw)
            commit(bufs)
            return out
        return run

    mm, attn_fwd, attn_bwd = carried(_mm, 4), carried(_attn_fwd, 6), carried(_attn_bwd, 9)

    side, commit = gather_side([("ab_w_in", 1, 0), ("ab_w_out", 1, 0)])
    commit(_comm_only(side, "gather_first"))

    X0 = jnp.concatenate([x[0], ctx[0]], axis=0)
    m0, m1 = mods[0], mods[1]
    h1 = _norm_mod_fwd(X0, norm1_g[0], m0[0], m0[1], tb, nlat, "l0_norm1")
    z = mm(h1, full["ab_w_in"], "nn", BF16, "l0_in", tm=(544, 512), tn=(1024, 512), bias=ab_b_in[0])
    bs_full = jnp.broadcast_to(a_b_s[0][:, :, None], (AH, CHUNK, CHUNK))
    ya = _gmlp_fwd(z, a_ln_g[0], a_ln_b[0], a_w_s[0], bs_full, tb, "l0_gmlp")
    hc = _glu_conv_fwd(z, 2 * AW, bcw, b_conv_b[0], L, ch, "l0_conv")
    yb = _ln_silu_fwd(hc, b_ln_g[0], b_ln_b[0], tb, "l0_lnsilu")
    yab = jnp.concatenate([ya, yb], axis=1)
    y0 = mm(yab, full["ab_w_out"], "nn", F32, "l0_out", tm=(544, 512), tn=(1024, 512))
    X1 = _gate_res_fwd(X0, y0, m0[2], tb, nlat, "l0_res1")
    X2, ffn0 = _conv_ffn_fwd(mm, full, 0, X1, m0, norm2_g[0], fcw[0], ffn_conv_b[0], L, tb, nlat, ch, "l0_ffn")

    tabs = _rope_tables(L, T)
    hm = _norm_mod_fwd(X2, norm1_g[1], m1[0], m1[1], tb, nlat, "l1_norm1")
    zm = mm(hm, full["mla_w_in"], "nn", F32, "l1_in", tm=(544, 512), tn=(MI,))
    cqn, ckvn, kpe = _mla_prep_fwd(zm, qg, kvg, tabs, tb, "l1_prep")
    q = mm(cqn, full["mla_w_uq"], "nn", BF16, "l1_uq", tm=(512,), tn=(1024, 512), rows=L)
    kvh = mm(ckvn, full["mla_w_ukv"], "nn", BF16, "l1_ukv", tm=(544, 512), tn=(1024, 512))
    o, lse = attn_fwd(q, kvh, kpe, tabs, L, 2 * tq if L % (2 * tq) == 0 and L > 2 * tq else tq, "l1_attn")
    yl = mm(o, full["mla_w_o"], "nn", F32, "l1_o", tm=(512,), tn=(1024, 512))
    m1_lat = [a[:1] for a in m1]
    X3 = _gate_res_fwd(X2, yl, m1_lat[2], tb, nlat, "l1_res1")
    X4, ffn1 = _conv_ffn_fwd(mm, full, 1, X3, m1_lat, norm2_g[1], fcw[1], ffn_conv_b[1], L, tb, nlat, ch, "l1_ffn")
    loss_acc, dX4, dfinal = _final_loss(X4, final_norm_g, loss_target[0], tb, "loss")

    dX3, gf1 = _conv_ffn_bwd(mm, full, gbuf, 1, dX4, ffn1, m1_lat, norm2_g[1], fcw[1], ffn_conv_b[1], L, tb, nlat, ch, "l1_ffn")
    dyl, dg1_1 = _gate_bwd(dX3, yl, m1_lat[2], tb, nlat, "l1_dres1")
    do = mm(dyl, full["mla_w_o"], "nt", BF16, "l1_do_x", tm=(512,), tn=(1024, 512))
    gbuf["mla_w_o"] = mm(o, dyl, "tn", BF16, "l1_do_w", tm=(512,), tn=(1024, 512))
    dq, dkv, dkpe = attn_bwd(q, kvh, kpe, tabs, o, lse, do, L, tq, "l1_dattn")
    dckvn = mm(dkv, full["mla_w_ukv"], "nt", BF16, "l1_dukv_x", tm=(544, 512), tn=(KVL,), tk=(2048, 512))
    gbuf["mla_w_ukv"] = mm(ckvn, dkv, "tn", BF16, "l1_dukv_w", tm=(512,), tn=(1024, 512))
    dcqn = mm(dq, full["mla_w_uq"], "nt", BF16, "l1_duq_x", tm=(512,), tn=(QL,), tk=(2048, 512))
    gbuf["mla_w_uq"] = mm(cqn, dq, "tn", BF16, "l1_duq_w", tm=(QL,), tn=(1024, 512), rows=L)
    dcqn = jnp.concatenate([dcqn, jnp.zeros((CT, QL), BF16)], axis=0)
    dzm, dqg, dkvg = _mla_prep_bwd(zm, qg, kvg, tabs, dcqn, dckvn, dkpe, tb, "l1_dprep")
    dhm = mm(dzm, full["mla_w_in"], "nt", BF16, "l1_din_x", tm=(544, 512), tn=(1024, 512))
    gbuf["mla_w_in"] = mm(hm, dzm, "tn", BF16, "l1_din_w", tm=(512,), tn=(MI,))
    dX2, dn1g_1, dsh1_1, dsc1_1 = _norm_mod_bwd(X2, norm1_g[1], m1[1], dhm, dX3, tb, nlat, "l1_dnorm1")

    dX1, gf0 = _conv_ffn_bwd(mm, full, gbuf, 0, dX2, ffn0, m0, norm2_g[0], fcw[0], ffn_conv_b[0], L, tb, nlat, ch, "l0_ffn")
    dy0, dg1_0 = _gate_bwd(dX1, y0, m0[2], tb, nlat, "l0_dres1")
    dyab = mm(dy0, full["ab_w_out"], "nt", BF16, "l0_dout_x", tm=(544, 512), tn=(1024, 512))
    gbuf["ab_w_out"] = mm(yab, dy0, "tn", BF16, "l0_dout_w", tm=(512,), tn=(1024, 512))
    dzu, dzv, dlnag, dlnab, dws, dbs, csu, csv = _gmlp_bwd(z, dyab, a_ln_g[0], a_ln_b[0], a_w_s[0], bs_full, tb, "l0_dgmlp")
    dhc, dlnbg, dlnbb = _ln_silu_bwd(hc, dyab, AW, b_ln_g[0], b_ln_b[0], tb, "l0_dlnsilu")
    dza, dzg, dbcw, dbcb, csa, csg = _glu_conv_bwd(z, 2 * AW, dhc, bcw, L, ch, "l0_dconv")
    dz = jnp.concatenate([dzu, dzv, dza, dzg], axis=1)
    dbin = jnp.concatenate([csu, csv, csa, csg], axis=1)
    dh1 = mm(dz, full["ab_w_in"], "nt", BF16, "l0_din_x", tm=(544, 512), tn=(1024, 512), tk=(2048, 512))
    gbuf["ab_w_in"] = mm(h1, dz, "tn", BF16, "l0_din_w", tm=(512,), tn=(1024, 512))
    dX0, dn1g_0, dsh1_0, dsc1_0 = _norm_mod_bwd(X0, norm1_g[0], m0[1], dh1, dX1, tb, nlat, "l0_dnorm1")
    grad_x = dX0[:L][None]

    side, commit = reduce_side([("ab_w_out", 1, 0), ("ab_w_in", 1, 0)])
    commit(_comm_only(side, "rs_last"))
    halves = [_rs_sum8(gbuf[nm], rbuf[nm], wins[nm][0], wins[nm][2], "rs_sum_" + nm) for nm in names]
    gshard = dict(zip(names, _rs_share(halves, "rs_share")))

    def grp6(l, sh1, sc1, g1, f):
        G = sh1.shape[0]
        pad = lambda a: jnp.concatenate([a, jnp.zeros((G - a.shape[0],) + a.shape[1:], F32)], axis=0) if a.shape[0] < G else a
        return jnp.concatenate([pad(a) for a in (sh1, sc1, g1, f["sh2"], f["sc2"], f["g2"])], axis=1)

    dm0 = grp6(0, dsh1_0, dsc1_0, dg1_0, gf0)
    dm1 = grp6(1, dsh1_1, dsc1_1, dg1_1, gf1)
    dmods = jnp.stack([dm0, dm1], axis=0)
    small = [
        jnp.concatenate([dn1g_0, dn1g_1], axis=0), jnp.concatenate([gf0["n2g"], gf1["n2g"]], axis=0), dbin, dlnag, dlnab, dws,
        dbs, dbcw, dbcb, dlnbg, dlnbb, dqg, dkvg, jnp.stack([gf0["cw"], gf1["cw"]], axis=0),
        jnp.concatenate([gf0["cb"], gf1["cb"]], axis=0), dfinal, dmods[:, 1],
    ]
    small_shapes = [a.shape for a in small]
    lat_shape = dmods[:, 0].shape
    blob = _pack(small + [dmods[:, 0], loss_acc[0, :1]])
    gathered = _gather_blob(blob, "gather_grads")
    summed = _sum_lead(gathered.reshape(8, -1, LANES), F32, "sum_grads").reshape(-1)
    (dn1g, dn2g, dbin_s, dlnag_s, dlnab_s, dws_s, dbs_s, dbcw_s, dbcb_s, dlnbg_s, dlnbb_s, dqg_s, dkvg_s, dfcw_s, dfcb_s, dfinal_s,
     dmods_ctx, dmods_lat_sum, loss) = _unpack(summed, small_shapes + [lat_shape, (1,)])
    loss = loss.reshape(())
    n_small = sum(math.prod(s) for s in small_shapes)
    dmods_lat = gathered[:, n_small:n_small + math.prod(lat_shape)].reshape((8,) + lat_shape)

    grad_w_ada, dc_parts = [], []
    for i in range(2):
        dm16 = jnp.concatenate([dmods_lat[:, i].reshape(8, N_MOD * D), dmods_ctx[i].reshape(1, N_MOD * D),
                                jnp.zeros((7, N_MOD * D), F32)], axis=0)
        dm16_s = lax.dynamic_slice(dm16, (0, p_me * NA), (16, NA))
        grad_w_ada.append(_mm(c16, dm16_s, "tn", F32, f"dada{i}_w", tm=(1024, 512), tn=(1024, 512), silu_a=True))
        dc_parts.append(_mm(dm16_s, w_ada, "nt", F32, f"dada{i}_c", tm=(16,), tn=(512,), tk=(1024, 512), b_lead=i))
    grad_w_ada = jnp.stack(grad_w_ada, axis=0)
    grad_b_ada = dmods_lat_sum.reshape(2, N_MOD * D) + dmods_ctx.reshape(2, N_MOD * D)
    dc_blob = _pack([dc_parts[0][8] + dc_parts[1][8]])
    dc_all = _gather_blob(dc_blob, "gather_dc")[0::2, :D]
    grad_c_ctx = _cctx_grad(dc_all, c_ctx, "dcctx").reshape(D)

    def my_cols(a, axis, n):
        return lax.dynamic_slice_in_dim(a, p_me * n, n, axis=axis)

    unpad_uq = lambda g: g.reshape(QL, HS, QHEAD)[:, :, :HEAD + ROPE].reshape(QL, HS * (HEAD + ROPE))
    grads = dict(
        c_ctx=grad_c_ctx, norm1_g=dn1g, norm2_g=dn2g, w_ada=grad_w_ada, b_ada=grad_b_ada, ab_w_in=gshard["ab_w_in"][None],
        ab_b_in=dbin_s, a_ln_g=dlnag_s, a_ln_b=dlnab_s, a_w_s=dws_s[None], a_b_s=dbs_s.reshape(1, AH, CHUNK),
        b_conv_w=my_cols(dbcw_s, 1, BW // 4)[None], b_conv_b=dbcb_s, b_ln_g=dlnbg_s, b_ln_b=dlnbb_s,
        ab_w_out=gshard["ab_w_out"][None], mla_w_in=gshard["mla_w_in"][:, :mla_w_in.shape[-1]][None],
        mla_q_norm_g=my_cols(dqg_s, 1, QL // 4), mla_w_uq=unpad_uq(gshard["mla_w_uq"])[None],
        mla_kv_norm_g=my_cols(dkvg_s, 1, KVL // 4), mla_w_ukv=gshard["mla_w_ukv"][None], mla_w_o=gshard["mla_w_o"][None],
        ffn_w_up=jnp.stack([gshard["ffn_w_up0"], gshard["ffn_w_up1"]], axis=0), ffn_conv_w=my_cols(dfcw_s, 2, DFF // 4),
        ffn_conv_b=dfcb_s, ffn_w_down=jnp.stack([gshard["ffn_w_down0"], gshard["ffn_w_down1"]], axis=0),
        final_norm_g=dfinal_s.reshape(D),
    )
    grads = {k: grads[k].reshape(W[k].shape) for k in ORDER}

    BIG = ("w_ada", "ab_w_in", "ab_w_out", "mla_w_in", "mla_w_uq", "mla_w_ukv", "mla_w_o", "ffn_w_up", "ffn_w_down")
    delta, new_m, new_v = {}, {}, {}
    for k in BIG:
        two = lambda a: a.reshape(-1, a.shape[-1])
        d_, m_, v_ = _adamw(two(W[k]), two(grads[k]), two(MOM[k]), two(VAR[k]), "adamw_" + k)
        delta[k], new_m[k], new_v[k] = (a.reshape(W[k].shape) for a in (d_, m_, v_))
    SMALL = [k for k in ORDER if k not in BIG]
    d_, m_, v_ = _adamw(_pack([W[k] for k in SMALL]), _pack([grads[k] for k in SMALL]), _pack([MOM[k] for k in SMALL]),
                        _pack([VAR[k] for k in SMALL]), "adamw_small")
    shapes = [W[k].shape for k in SMALL]
    for k, dd, mm, vv in zip(SMALL, _unpack(d_.reshape(-1), shapes), _unpack(m_.reshape(-1), shapes), _unpack(v_.reshape(-1), shapes)):
        delta[k], new_m[k], new_v[k] = dd, mm, vv

    return (loss, grad_x, *[grads[k] for k in ORDER], *[delta[k] for k in ORDER], *[new_m[k] for k in ORDER],
            *[new_v[k] for k in ORDER])
```

```python
import functools
import math

import jax
import jax.numpy as jnp
from jax import lax
from jax.experimental import pallas as pl
from jax.experimental.pallas import tpu as pltpu

F32 = jnp.float32
BF16 = jnp.bfloat16
MESH = pl.DeviceIdType.MESH
ANY = pl.BlockSpec(memory_space=pl.ANY)

EPS = 1e-6
N_MOD = 6
CHUNK = 128
HEAD = 128
ROPE = 64
QHEAD = 2 * HEAD
GRID_W = 64
ROPE_THETA = 10000.0
B_CONV = 31
FFN_CONV = 3
ADAM_LR, ADAM_B1, ADAM_B2, ADAM_EPS, ADAM_WD, ADAM_STEP = 0.001, 0.9, 0.999, 1e-08, 0.01, 10

V7X_VMEM_LIMIT = 56 * 1024 * 1024
LANES = 128
SUBLANES = 8
CONV_PAD = 16

BS = pl.BlockSpec


def _cp(sem=None, vmem=V7X_VMEM_LIMIT):
    return pltpu.CompilerParams(dimension_semantics=sem, vmem_limit_bytes=vmem)


class _Side:
    def __init__(self, bufs, nsem, start, finish):
        self.bufs, self.nsem, self.start, self.finish = list(bufs), nsem, start, finish


_carry_plan = {}


def _call(body, *, name, out_shape, grid, in_specs, out_specs, operands, sem, scratch=()):
    if name not in _carry_plan:
        return pl.pallas_call(body, name=name, out_shape=out_shape, grid=grid, in_specs=in_specs, out_specs=out_specs,
                              scratch_shapes=list(scratch), compiler_params=_cp(sem))(*operands)
    side, commit = _carry_plan[name]()
    multi = isinstance(out_shape, (tuple, list))
    outs = tuple(out_shape) if multi else (out_shape,)
    ospecs = tuple(out_specs) if multi else (out_specs,)
    n_in, n_out, n_scr, n_buf = len(in_specs), len(outs), len(scratch), len(side.bufs)

    def body2(*refs):
        o0 = n_in + n_buf
        s0 = o0 + n_out + n_buf
        bufs = refs[o0 + n_out:s0]
        sends, recvs = refs[s0 + n_scr:]
        first = last = None
        for d, g in enumerate(grid):
            pid = pl.program_id(d)
            first = (pid == 0) if first is None else jnp.logical_and(first, pid == 0)
            last = (pid == g - 1) if last is None else jnp.logical_and(last, pid == g - 1)

        @pl.when(first)
        def _():
            side.start(bufs, sends, recvs)

        body(*refs[:n_in], *refs[o0:o0 + n_out], *refs[s0:s0 + n_scr])

        @pl.when(last)
        def _():
            side.finish(bufs, sends, recvs)

    res = pl.pallas_call(
        body2, name=name, out_shape=outs + tuple(jax.ShapeDtypeStruct(b.shape, b.dtype) for b in side.bufs), grid=grid,
        in_specs=list(in_specs) + [ANY] * n_buf, out_specs=ospecs + (ANY,) * n_buf,
        scratch_shapes=list(scratch) + [pltpu.SemaphoreType.DMA((side.nsem,)), pltpu.SemaphoreType.DMA((side.nsem,))],
        input_output_aliases={n_in + i: n_out + i for i in range(n_buf)}, compiler_params=_cp(("arbitrary",) * len(grid)),
    )(*operands, *side.bufs)
    commit(list(res[n_out:]))
    return tuple(res[:n_out]) if multi else res[0]


def _pcs(body, *, name, out_shape, grid, in_specs, out_specs, compiler_params, scratch_shapes=()):
    def run(*operands):
        return _call(body, name=name, out_shape=out_shape, grid=grid, in_specs=in_specs, out_specs=out_specs, operands=operands,
                     sem=compiler_params.dimension_semantics, scratch=scratch_shapes)
    return run


def _pick(n, prefs):
    for p in prefs:
        if p <= n and n % p == 0:
            return p
    return n


def _row_tile(rows, cols, target_bytes=1 << 20):
    best = None
    for d in range(2 * SUBLANES, rows + 1, 2 * SUBLANES):
        if rows % d == 0 and d * cols * 4 <= target_bytes:
            best = d
    return best if best is not None else rows


def _sigmoid(x):
    return 1.0 / (1.0 + jnp.exp(-x))


def _gelu(x):
    c = math.sqrt(2.0 / math.pi)
    th = jnp.tanh(c * (x + 0.044715 * x * x * x))
    return 0.5 * x * (1.0 + th), th


def _gelu_grad(x, th):
    c = math.sqrt(2.0 / math.pi)
    return 0.5 * (1.0 + th) + 0.5 * x * (1.0 - th * th) * c * (1.0 + 3.0 * 0.044715 * x * x)


_DIMS = {"nn": (((1,), (0,)), ((), ())), "nt": (((1,), (1,)), ((), ())), "tn": (((0,), (0,)), ((), ()))}


def _mm(a, b, mode, out_dtype, name, tm=(512,), tn=(512,), tk=(100000,), bias=None, silu_a=False, rows=None, outer="j", b_lead=None):
    bshape = b.shape[-2:]
    if mode == "nn":
        (M, K), N = a.shape, bshape[1]
    elif mode == "nt":
        (M, K), N = a.shape, bshape[0]
    else:
        (K, M), N = a.shape, bshape[1]
    if rows is not None:
        if mode == "tn":
            K = rows
        else:
            M = rows
    tm, tn, tk = _pick(M, tm), _pick(N, tn), _pick(K, tk)
    gm, gn, gk = M // tm, N // tn, K // tk

    def ij(g0, g1):
        return (g1, g0) if outer == "j" else (g0, g1)

    if mode == "nn":
        a_spec = BS((tm, tk), lambda g0, g1, k: (ij(g0, g1)[0], k))
        b_spec = BS((tk, tn), lambda g0, g1, k: (k, ij(g0, g1)[1]))
    elif mode == "nt":
        a_spec = BS((tm, tk), lambda g0, g1, k: (ij(g0, g1)[0], k))
        b_spec = BS((tn, tk), lambda g0, g1, k: (ij(g0, g1)[1], k))
    else:
        a_spec = BS((tk, tm), lambda g0, g1, k: (k, ij(g0, g1)[0]))
        b_spec = BS((tk, tn), lambda g0, g1, k: (k, ij(g0, g1)[1]))
    if b_lead is not None:
        blk, at = b_spec.block_shape, b_spec.index_map
        b_spec = BS((None,) + tuple(blk), lambda g0, g1, k: (b_lead,) + tuple(at(g0, g1, k)))
    in_specs = [a_spec, b_spec]
    operands = [a, b]
    if bias is not None:
        in_specs.append(BS((1, tn), lambda g0, g1, k: (0, ij(g0, g1)[1])))
        operands.append(bias.reshape(1, N))
    o_spec = BS((tm, tn), lambda g0, g1, k: ij(g0, g1))

    def body(*refs):
        a_ref, b_ref = refs[0], refs[1]
        bias_ref = refs[2] if bias is not None else None
        o_ref = refs[3] if bias is not None else refs[2]
        av = a_ref[...]
        if silu_a:
            av = av.astype(F32)
            av = av * _sigmoid(av)
        part = lax.dot_general(av.astype(BF16), b_ref[...].astype(BF16), _DIMS[mode], preferred_element_type=F32)

        def finish(acc):
            if bias_ref is not None:
                acc = acc + bias_ref[...]
            return acc.astype(out_dtype)

        if gk == 1:
            o_ref[...] = finish(part)
        else:
            acc_ref = refs[-1]
            k = pl.program_id(2)

            @pl.when(k == 0)
            def _():
                acc_ref[...] = part

            @pl.when(k > 0)
            def _():
                acc_ref[...] += part

            @pl.when(k == gk - 1)
            def _():
                o_ref[...] = finish(acc_ref[...])

    grid = (gn, gm, gk) if outer == "j" else (gm, gn, gk)
    return _call(body, name=name, out_shape=jax.ShapeDtypeStruct((M, N), out_dtype), grid=grid, in_specs=in_specs, out_specs=o_spec,
                 scratch=[pltpu.VMEM((tm, tn), F32)] if gk > 1 else [], sem=("parallel", "parallel", "arbitrary"),
                 operands=operands)


def _accum(ref, val, first):
    @pl.when(first)
    def _():
        ref[...] = val

    @pl.when(jnp.logical_not(first))
    def _():
        ref[...] += val


def _norm_mod_fwd(X, gain, sh, sc, tb, nlat, name):
    R, D = X.shape

    def body(x_ref, g_ref, sh_ref, sc_ref, o_ref):
        x = x_ref[...]
        r = lax.rsqrt(jnp.mean(x * x, axis=-1, keepdims=True) + EPS)
        o_ref[...] = ((x * r * g_ref[...]) * (1.0 + sc_ref[0]) + sh_ref[0]).astype(BF16)

    grp = BS((1, 1, D), lambda i: (i // nlat, 0, 0))
    return _pcs(
        body, name=name, out_shape=jax.ShapeDtypeStruct((R, D), BF16), grid=(R // tb,),
        in_specs=[BS((tb, D), lambda i: (i, 0)), BS((1, D), lambda i: (0, 0)), grp, grp],
        out_specs=BS((tb, D), lambda i: (i, 0)), compiler_params=_cp(("parallel",)),
    )(X, gain.reshape(1, D), sh, sc)


def _norm_mod_bwd(X, gain, sc, dh, dup, tb, nlat, name):
    R, D = X.shape
    G = sc.shape[0]
    n_up = dup.shape[0] // tb

    def body(x_ref, g_ref, sc_ref, dh_ref, dup_ref, dx_ref, dg_ref, dsh_ref, dsc_ref):
        i = pl.program_id(0)
        x = x_ref[...]
        g = g_ref[...]
        r = lax.rsqrt(jnp.mean(x * x, axis=-1, keepdims=True) + EPS)
        xh = x * r
        dh_ = dh_ref[...].astype(F32)
        t = dh_ * (1.0 + sc_ref[0])
        tg = t * g
        dup_ = dup_ref[...] if n_up == R // tb else jnp.where(i < n_up, dup_ref[...], 0.0)
        dx_ref[...] = dup_ + r * (tg - xh * jnp.mean(tg * xh, axis=-1, keepdims=True))
        _accum(dg_ref, jnp.sum(t * xh, axis=0, keepdims=True), i == 0)
        first = i % nlat == 0
        _accum(dsh_ref, jnp.sum(dh_, axis=0, keepdims=True)[None], first)
        _accum(dsc_ref, jnp.sum(dh_ * xh * g, axis=0, keepdims=True)[None], first)

    row = BS((tb, D), lambda i: (i, 0))
    grp = BS((1, 1, D), lambda i: (i // nlat, 0, 0))
    return _pcs(
        body, name=name,
        out_shape=(jax.ShapeDtypeStruct((R, D), F32), jax.ShapeDtypeStruct((1, D), F32),
                   jax.ShapeDtypeStruct((G, 1, D), F32), jax.ShapeDtypeStruct((G, 1, D), F32)),
        grid=(R // tb,), in_specs=[row, BS((1, D), lambda i: (0, 0)), grp, row, BS((tb, D), lambda i: (jnp.minimum(i, n_up - 1), 0))],
        out_specs=(row, BS((1, D), lambda i: (0, 0)), grp, grp), compiler_params=_cp(("arbitrary",)),
    )(X, gain.reshape(1, D), sc, dh, dup)


def _gate_res_fwd(X, y, gate, tb, nlat, name):
    R, D = y.shape

    def body(x_ref, y_ref, g_ref, o_ref):
        o_ref[...] = x_ref[...] + g_ref[0] * y_ref[...]

    row = BS((tb, D), lambda i: (i, 0))
    return _pcs(
        body, name=name, out_shape=jax.ShapeDtypeStruct((R, D), F32), grid=(R // tb,),
        in_specs=[row, row, BS((1, 1, D), lambda i: (i // nlat, 0, 0))], out_specs=row, compiler_params=_cp(("parallel",)),
    )(X, y, gate)


def _gate_bwd(dX, y, gate, tb, nlat, name):
    R, D = dX.shape
    G = gate.shape[0]

    def body(dx_ref, y_ref, g_ref, dy_ref, dg_ref):
        i = pl.program_id(0)
        dx = dx_ref[...]
        dy_ref[...] = (g_ref[0] * dx).astype(BF16)
        _accum(dg_ref, jnp.sum(dx * y_ref[...], axis=0, keepdims=True)[None], i % nlat == 0)

    row = BS((tb, D), lambda i: (i, 0))
    grp = BS((1, 1, D), lambda i: (i // nlat, 0, 0))
    return _pcs(
        body, name=name, out_shape=(jax.ShapeDtypeStruct((R, D), BF16), jax.ShapeDtypeStruct((G, 1, D), F32)),
        grid=(R // tb,), in_specs=[row, row, grp], out_specs=(row, grp), compiler_params=_cp(("arbitrary",)),
    )(dX, y, gate)


def _final_loss(X, gain, target, tb, name):
    R, D = X.shape

    def body(x_ref, g_ref, t_ref, loss_ref, dx_ref, dg_ref):
        i = pl.program_id(0)
        x = x_ref[...]
        g = g_ref[...]
        r = lax.rsqrt(jnp.mean(x * x, axis=-1, keepdims=True) + EPS)
        xh = x * r
        e = xh * g - t_ref[...]
        part = jnp.sum(jnp.sum(e * e, axis=1, keepdims=True), axis=0, keepdims=True) * (0.5 / D)
        _accum(loss_ref, jnp.broadcast_to(part, (1, LANES)), i == 0)
        dy = e * (1.0 / D)
        _accum(dg_ref, jnp.sum(dy * xh, axis=0, keepdims=True), i == 0)
        tg = dy * g
        dx_ref[...] = r * (tg - xh * jnp.mean(tg * xh, axis=-1, keepdims=True))

    row = BS((tb, D), lambda i: (i, 0))
    return _pcs(
        body, name=name,
        out_shape=(jax.ShapeDtypeStruct((1, LANES), F32), jax.ShapeDtypeStruct((R, D), F32), jax.ShapeDtypeStruct((1, D), F32)),
        grid=(R // tb,), in_specs=[row, BS((1, D), lambda i: (0, 0)), row],
        out_specs=(BS((1, LANES), lambda i: (0, 0)), row, BS((1, D), lambda i: (0, 0))), compiler_params=_cp(("arbitrary",)),
    )(X, gain.reshape(1, D), target)


def _ln_stats(v):
    mu = jnp.mean(v, axis=-1, keepdims=True)
    d = v - mu
    r = lax.rsqrt(jnp.mean(d * d, axis=-1, keepdims=True) + EPS)
    return d * r, r


def _gmlp_fwd(z, ln_g, ln_b, w_s, b_s_full, tb, name):
    R = z.shape[0]
    AW = ln_g.shape[-1]
    AH = w_s.shape[0]

    def body(zu_ref, zv_ref, g_ref, b_ref, ws_ref, bs_ref, o_ref):
        u, _ = _gelu(zu_ref[...].astype(F32))
        v, _ = _gelu(zv_ref[...].astype(F32))
        xh, _ = _ln_stats(v)
        vn = (xh * g_ref[...] + b_ref[...]).astype(BF16)
        for n in range(tb // CHUNK):
            rs = slice(n * CHUNK, (n + 1) * CHUNK)
            for h in range(AH):
                cs = slice(h * CHUNK, (h + 1) * CHUNK)
                v2 = jnp.dot(ws_ref[h].astype(BF16), vn[rs, cs], preferred_element_type=F32) + bs_ref[h]
                o_ref[rs, cs] = (u[rs, cs] * v2).astype(BF16)

    full3 = lambda s: BS(s, lambda i: (0, 0, 0))
    return _pcs(
        body, name=name, out_shape=jax.ShapeDtypeStruct((R, AW), BF16), grid=(R // tb,),
        in_specs=[BS((tb, AW), lambda i: (i, 0)), BS((tb, AW), lambda i: (i, 1)), BS((1, AW), lambda i: (0, 0)),
                  BS((1, AW), lambda i: (0, 0)), full3((AH, CHUNK, CHUNK)), full3((AH, CHUNK, CHUNK))],
        out_specs=BS((tb, AW), lambda i: (i, 0)), compiler_params=_cp(("parallel",)),
    )(z, z, ln_g.reshape(1, AW), ln_b.reshape(1, AW), w_s, b_s_full)


def _gmlp_bwd(z, dya, ln_g, ln_b, w_s, b_s_full, tb, name):
    R = z.shape[0]
    AW = ln_g.shape[-1]
    AH = w_s.shape[0]

    def body(zu_ref, zv_ref, dy_ref, g_ref, b_ref, ws_ref, bs_ref, dzu_ref, dzv_ref, dg_ref, db_ref, dws_ref, dbs_ref, cs_u_ref, cs_v_ref,
             dvn_scr):
        i = pl.program_id(0)
        first = i == 0
        zu = zu_ref[...].astype(F32)
        zv = zv_ref[...].astype(F32)
        u, thu = _gelu(zu)
        v, thv = _gelu(zv)
        xh, r = _ln_stats(v)
        g = g_ref[...]
        vn = (xh * g + b_ref[...]).astype(BF16)
        dy = dy_ref[...].astype(F32)

        @pl.when(first)
        def _():
            dws_ref[...] = jnp.zeros_like(dws_ref)
            dbs_ref[...] = jnp.zeros_like(dbs_ref)

        for n in range(tb // CHUNK):
            rs = slice(n * CHUNK, (n + 1) * CHUNK)
            for h in range(AH):
                cs = slice(h * CHUNK, (h + 1) * CHUNK)
                w = ws_ref[h].astype(BF16)
                v2 = jnp.dot(w, vn[rs, cs], preferred_element_type=F32) + bs_ref[h]
                dzu_ref[rs, cs] = (dy[rs, cs] * v2 * _gelu_grad(zu[rs, cs], thu[rs, cs])).astype(BF16)
                dv2 = dy[rs, cs] * u[rs, cs]
                dv2b = dv2.astype(BF16)
                dvn_scr[rs, cs] = lax.dot_general(w, dv2b, _DIMS["tn"], preferred_element_type=F32)
                dws_ref[h] += lax.dot_general(dv2b, vn[rs, cs], _DIMS["nt"], preferred_element_type=F32)
                dbs_ref[h] += jnp.sum(dv2, axis=1, keepdims=True)
        dvn = dvn_scr[...]
        _accum(dg_ref, jnp.sum(dvn * xh, axis=0, keepdims=True), first)
        _accum(db_ref, jnp.sum(dvn, axis=0, keepdims=True), first)
        t = dvn * g
        dv = r * (t - jnp.mean(t, axis=-1, keepdims=True) - xh * jnp.mean(t * xh, axis=-1, keepdims=True))
        dzv = dv * _gelu_grad(zv, thv)
        dzv_ref[...] = dzv.astype(BF16)
        _accum(cs_v_ref, jnp.sum(dzv, axis=0, keepdims=True), first)
        _accum(cs_u_ref, jnp.sum(dzu_ref[...].astype(F32), axis=0, keepdims=True), first)

    full3 = lambda s: BS(s, lambda i: (0, 0, 0))
    vec = BS((1, AW), lambda i: (0, 0))
    row = BS((tb, AW), lambda i: (i, 0))
    outs = _pcs(
        body, name=name,
        out_shape=(jax.ShapeDtypeStruct((R, AW), BF16), jax.ShapeDtypeStruct((R, AW), BF16), jax.ShapeDtypeStruct((1, AW), F32),
                   jax.ShapeDtypeStruct((1, AW), F32), jax.ShapeDtypeStruct((AH, CHUNK, CHUNK), F32),
                   jax.ShapeDtypeStruct((AH, CHUNK, 1), F32), jax.ShapeDtypeStruct((1, AW), F32), jax.ShapeDtypeStruct((1, AW), F32)),
        grid=(R // tb,),
        in_specs=[row, BS((tb, AW), lambda i: (i, 1)), row, vec, vec, full3((AH, CHUNK, CHUNK)), full3((AH, CHUNK, CHUNK))],
        out_specs=(row, row, vec, vec, full3((AH, CHUNK, CHUNK)), full3((AH, CHUNK, 1)), vec, vec),
        scratch_shapes=[pltpu.VMEM((tb, AW), F32)], compiler_params=_cp(("arbitrary",)),
    )(z, z, dya, ln_g.reshape(1, AW), ln_b.reshape(1, AW), w_s, b_s_full)
    return outs


def _segments(R, L):
    return [(0, L)] + ([(L, R - L)] if R > L else [])


def _scr_rows(R, L):
    return R + CONV_PAD * (len(_segments(R, L)) + 1)


def _scr_off(s, start):
    return CONV_PAD * (s + 1) + start


def _zero_pads(scr, R, L):
    segs = _segments(R, L)
    z = jnp.zeros((CONV_PAD, scr.shape[1]), F32)
    for s, (start, n) in enumerate(segs):
        scr[pl.ds(_scr_off(s, start) - CONV_PAD, CONV_PAD), :] = z
    last_s, (last_start, last_n) = len(segs) - 1, segs[-1]
    scr[pl.ds(_scr_off(last_s, last_start) + last_n, CONV_PAD), :] = z


def _for_chunks(R, L, ch, fn):
    for s, (start, n) in enumerate(_segments(R, L)):
        c = min(ch, n)
        off = _scr_off(s, start)

        def step(i, carry, start=start, off=off, c=c):
            r0 = pl.multiple_of(start + i * c, SUBLANES)
            fn(r0, pl.multiple_of(off + i * c, SUBLANES), c)
            return carry

        lax.fori_loop(0, n // c, step, 0)


def _taps(scr, srow, c, w_ref, ntap, flip):
    win = scr[pl.ds(srow - CONV_PAD, c + 2 * CONV_PAD), :]
    n = c + 2 * CONV_PAD
    acc = None
    for k in range(ntap):
        o = k - (ntap - 1) // 2
        if flip:
            o = -o
        sh = win if o == 0 else pltpu.roll(win, (-o) % n, 0)
        term = w_ref[k:k + 1, :] * sh[CONV_PAD:CONV_PAD + c]
        acc = term if acc is None else acc + term
    return acc


def _tap_grads(scr, srow, c, dy, dw_ref, ntap):
    win = scr[pl.ds(srow - CONV_PAD, c + 2 * CONV_PAD), :]
    n = c + 2 * CONV_PAD
    for k in range(ntap):
        o = k - (ntap - 1) // 2
        sh = win if o == 0 else pltpu.roll(win, (-o) % n, 0)
        dw_ref[k:k + 1, :] += jnp.sum(dy * sh[CONV_PAD:CONV_PAD + c], axis=0, keepdims=True)


def _glu_conv_fwd(z, col0, conv_w, conv_b, L, ch, name):
    R = z.shape[0]
    BW = conv_w.shape[1]
    nb, c0 = BW // LANES, col0 // LANES

    def body(a_ref, g_ref, w_ref, b_ref, o_ref, scr):
        _zero_pads(scr, R, L)

        def fill(r0, s0, c):
            a = a_ref[pl.ds(r0, c), :].astype(F32)
            g = g_ref[pl.ds(r0, c), :].astype(F32)
            scr[pl.ds(s0, c), :] = a * _sigmoid(g)

        _for_chunks(R, L, ch, fill)

        def conv(r0, s0, c):
            o_ref[pl.ds(r0, c), :] = _taps(scr, s0, c, w_ref, B_CONV, False) + b_ref[...]

        _for_chunks(R, L, ch, conv)

    return _pcs(
        body, name=name, out_shape=jax.ShapeDtypeStruct((R, BW), F32), grid=(nb,),
        in_specs=[BS((R, LANES), lambda j: (0, c0 + j)), BS((R, LANES), lambda j: (0, c0 + nb + j)),
                  BS((B_CONV, LANES), lambda j: (0, j)), BS((1, LANES), lambda j: (0, j))],
        out_specs=BS((R, LANES), lambda j: (0, j)), scratch_shapes=[pltpu.VMEM((_scr_rows(R, L), LANES), F32)],
        compiler_params=_cp(("parallel",)),
    )(z, z, conv_w, conv_b.reshape(1, BW))


def _glu_conv_bwd(z, col0, dhc, conv_w, L, ch, name):
    R = z.shape[0]
    BW = conv_w.shape[1]
    nb, c0 = BW // LANES, col0 // LANES

    def body(a_ref, g_ref, dy_ref, w_ref, da_ref, dg_ref, dw_ref, db_ref, csa_ref, csg_ref, scr_h, scr_dy):
        _zero_pads(scr_h, R, L)
        _zero_pads(scr_dy, R, L)
        dw_ref[...] = jnp.zeros_like(dw_ref)
        db_ref[...] = jnp.zeros_like(db_ref)
        csa_ref[...] = jnp.zeros_like(csa_ref)
        csg_ref[...] = jnp.zeros_like(csg_ref)

        def fill(r0, s0, c):
            a = a_ref[pl.ds(r0, c), :].astype(F32)
            g = g_ref[pl.ds(r0, c), :].astype(F32)
            scr_h[pl.ds(s0, c), :] = a * _sigmoid(g)
            scr_dy[pl.ds(s0, c), :] = dy_ref[pl.ds(r0, c), :]

        _for_chunks(R, L, ch, fill)

        def back(r0, s0, c):
            dh = _taps(scr_dy, s0, c, w_ref, B_CONV, True)
            a = a_ref[pl.ds(r0, c), :].astype(F32)
            sg = _sigmoid(g_ref[pl.ds(r0, c), :].astype(F32))
            da = dh * sg
            dg = dh * a * sg * (1.0 - sg)
            da_ref[pl.ds(r0, c), :] = da.astype(BF16)
            dg_ref[pl.ds(r0, c), :] = dg.astype(BF16)
            csa_ref[...] += jnp.sum(da, axis=0, keepdims=True)
            csg_ref[...] += jnp.sum(dg, axis=0, keepdims=True)
            dy = dy_ref[pl.ds(r0, c), :]
            db_ref[...] += jnp.sum(dy, axis=0, keepdims=True)
            _tap_grads(scr_h, s0, c, dy, dw_ref, B_CONV)

        _for_chunks(R, L, ch, back)

    col = BS((R, LANES), lambda j: (0, j))
    vec = BS((1, LANES), lambda j: (0, j))
    nrow = _scr_rows(R, L)
    return _pcs(
        body, name=name,
        out_shape=(jax.ShapeDtypeStruct((R, BW), BF16), jax.ShapeDtypeStruct((R, BW), BF16), jax.ShapeDtypeStruct((B_CONV, BW), F32),
                   jax.ShapeDtypeStruct((1, BW), F32), jax.ShapeDtypeStruct((1, BW), F32), jax.ShapeDtypeStruct((1, BW), F32)),
        grid=(nb,),
        in_specs=[BS((R, LANES), lambda j: (0, c0 + j)), BS((R, LANES), lambda j: (0, c0 + nb + j)), col,
                  BS((B_CONV, LANES), lambda j: (0, j))],
        out_specs=(col, col, BS((B_CONV, LANES), lambda j: (0, j)), vec, vec, vec),
        scratch_shapes=[pltpu.VMEM((nrow, LANES), F32), pltpu.VMEM((nrow, LANES), F32)], compiler_params=_cp(("parallel",)),
    )(z, z, dhc, conv_w)


def _ln_silu_fwd(hc, ln_g, ln_b, tb, name):
    R, W = hc.shape

    def body(x_ref, g_ref, b_ref, o_ref):
        xh, _ = _ln_stats(x_ref[...])
        y = xh * g_ref[...] + b_ref[...]
        o_ref[...] = (y * _sigmoid(y)).astype(BF16)

    vec = BS((1, W), lambda i: (0, 0))
    row = BS((tb, W), lambda i: (i, 0))
    return _pcs(
        body, name=name, out_shape=jax.ShapeDtypeStruct((R, W), BF16), grid=(R // tb,), in_specs=[row, vec, vec], out_specs=row,
        compiler_params=_cp(("parallel",)),
    )(hc, ln_g.reshape(1, W), ln_b.reshape(1, W))


def _ln_silu_bwd(hc, dyb, col0, ln_g, ln_b, tb, name):
    R, W = hc.shape
    c0 = col0 // W

    def body(x_ref, dy_ref, g_ref, b_ref, dx_ref, dg_ref, db_ref):
        i = pl.program_id(0)
        xh, r = _ln_stats(x_ref[...])
        g = g_ref[...]
        y = xh * g + b_ref[...]
        s = _sigmoid(y)
        dy = dy_ref[...].astype(F32) * s * (1.0 + y * (1.0 - s))
        _accum(dg_ref, jnp.sum(dy * xh, axis=0, keepdims=True), i == 0)
        _accum(db_ref, jnp.sum(dy, axis=0, keepdims=True), i == 0)
        t = dy * g
        dx_ref[...] = r * (t - jnp.mean(t, axis=-1, keepdims=True) - xh * jnp.mean(t * xh, axis=-1, keepdims=True))

    vec = BS((1, W), lambda i: (0, 0))
    row = BS((tb, W), lambda i: (i, 0))
    return _pcs(
        body, name=name,
        out_shape=(jax.ShapeDtypeStruct((R, W), F32), jax.ShapeDtypeStruct((1, W), F32), jax.ShapeDtypeStruct((1, W), F32)),
        grid=(R // tb,), in_specs=[row, BS((tb, W), lambda i: (i, c0)), vec, vec], out_specs=(row, vec, vec),
        compiler_params=_cp(("arbitrary",)),
    )(hc, dyb, ln_g.reshape(1, W), ln_b.reshape(1, W))


def _ffn_act_fwd(zf, conv_w, conv_b, L, ch, name):
    R = zf.shape[0]
    DFF = conv_w.shape[1]
    nb = DFF // LANES

    def body(g_ref, u_ref, w_ref, b_ref, o_ref, scr):
        _zero_pads(scr, R, L)

        def fill(r0, s0, c):
            scr[pl.ds(s0, c), :] = g_ref[pl.ds(r0, c), :].astype(F32)

        _for_chunks(R, L, ch, fill)

        def act(r0, s0, c):
            gc = _taps(scr, s0, c, w_ref, FFN_CONV, False) + b_ref[...]
            o_ref[pl.ds(r0, c), :] = (gc * _sigmoid(gc) * u_ref[pl.ds(r0, c), :].astype(F32)).astype(BF16)

        _for_chunks(R, L, ch, act)

    return _pcs(
        body, name=name, out_shape=jax.ShapeDtypeStruct((R, DFF), BF16), grid=(nb,),
        in_specs=[BS((R, LANES), lambda j: (0, j)), BS((R, LANES), lambda j: (0, nb + j)), BS((FFN_CONV, LANES), lambda j: (0, j)),
                  BS((1, LANES), lambda j: (0, j))],
        out_specs=BS((R, LANES), lambda j: (0, j)), scratch_shapes=[pltpu.VMEM((_scr_rows(R, L), LANES), F32)],
        compiler_params=_cp(("parallel",)),
    )(zf, zf, conv_w, conv_b.reshape(1, DFF))


def _ffn_act_bwd(zf, df, conv_w, conv_b, L, ch, name):
    R = zf.shape[0]
    DFF = conv_w.shape[1]
    nb = DFF // LANES

    def body(g_ref, u_ref, df_ref, w_ref, b_ref, dg_ref, du_ref, dw_ref, db_ref, scr_g, scr_d):
        _zero_pads(scr_g, R, L)
        _zero_pads(scr_d, R, L)
        dw_ref[...] = jnp.zeros_like(dw_ref)
        db_ref[...] = jnp.zeros_like(db_ref)

        def fill(r0, s0, c):
            scr_g[pl.ds(s0, c), :] = g_ref[pl.ds(r0, c), :].astype(F32)

        _for_chunks(R, L, ch, fill)

        def pre(r0, s0, c):
            gc = _taps(scr_g, s0, c, w_ref, FFN_CONV, False) + b_ref[...]
            s = _sigmoid(gc)
            d = df_ref[pl.ds(r0, c), :].astype(F32)
            du_ref[pl.ds(r0, c), :] = (d * gc * s).astype(BF16)
            dgc = d * u_ref[pl.ds(r0, c), :].astype(F32) * s * (1.0 + gc * (1.0 - s))
            scr_d[pl.ds(s0, c), :] = dgc
            db_ref[...] += jnp.sum(dgc, axis=0, keepdims=True)
            _tap_grads(scr_g, s0, c, dgc, dw_ref, FFN_CONV)

        _for_chunks(R, L, ch, pre)

        def back(r0, s0, c):
            dg_ref[pl.ds(r0, c), :] = _taps(scr_d, s0, c, w_ref, FFN_CONV, True).astype(BF16)

        _for_chunks(R, L, ch, back)

    col = BS((R, LANES), lambda j: (0, j))
    vec = BS((1, LANES), lambda j: (0, j))
    nrow = _scr_rows(R, L)
    return _pcs(
        body, name=name,
        out_shape=(jax.ShapeDtypeStruct((R, DFF), BF16), jax.ShapeDtypeStruct((R, DFF), BF16), jax.ShapeDtypeStruct((FFN_CONV, DFF), F32),
                   jax.ShapeDtypeStruct((1, DFF), F32)),
        grid=(nb,),
        in_specs=[col, BS((R, LANES), lambda j: (0, nb + j)), col, BS((FFN_CONV, LANES), lambda j: (0, j)), vec],
        out_specs=(col, col, BS((FFN_CONV, LANES), lambda j: (0, j)), vec),
        scratch_shapes=[pltpu.VMEM((nrow, LANES), F32), pltpu.VMEM((nrow, LANES), F32)], compiler_params=_cp(("parallel",)),
    )(zf, zf, df, conv_w, conv_b.reshape(1, DFF))


def _rope_tables(L, T):
    rows = L // GRID_W
    row = jnp.repeat(jnp.arange(rows, dtype=F32), GRID_W)
    col = jnp.tile(jnp.arange(GRID_W, dtype=F32), rows)
    n_freq = ROPE // 4
    inv = ROPE_THETA ** (-jnp.arange(n_freq, dtype=F32) / n_freq)
    ang = jnp.concatenate([row[:, None] * inv, col[:, None] * inv], axis=-1)
    cos, sin = jnp.cos(ang), jnp.sin(ang)
    half = ROPE // 2
    zero = jnp.zeros((L, half), F32)
    cos_t = jnp.concatenate([cos, cos, jnp.ones((L, LANES - ROPE), F32)], axis=1)
    sa = jnp.concatenate([zero, sin, zero, zero], axis=1)
    sb = jnp.concatenate([-sin, zero, zero, zero], axis=1)
    pad = T - L
    cos_t = jnp.concatenate([cos_t, jnp.ones((pad, LANES), F32)], axis=0)
    sa = jnp.concatenate([sa, jnp.zeros((pad, LANES), F32)], axis=0)
    sb = jnp.concatenate([sb, jnp.zeros((pad, LANES), F32)], axis=0)
    return cos_t, sa, sb


def _rope(x, cos, sa, sb):
    half = ROPE // 2
    return x * cos + pltpu.roll(x, half, 1) * sa + pltpu.roll(x, LANES - half, 1) * sb


def _rope_t(d, cos, sa, sb):
    half = ROPE // 2
    return d * cos + pltpu.roll(d * sa, LANES - half, 1) + pltpu.roll(d * sb, half, 1)


def _mla_prep_fwd(zm, qg, kvg, tabs, tb, name):
    T, W = zm.shape
    QL, KVL = qg.shape[-1], kvg.shape[-1]

    def body(z_ref, qg_ref, kg_ref, cos_ref, sa_ref, sb_ref, q_ref, k_ref, p_ref):
        cq = z_ref[:, :QL]
        r = lax.rsqrt(jnp.mean(cq * cq, axis=-1, keepdims=True) + EPS)
        q_ref[...] = (cq * r * qg_ref[...]).astype(BF16)
        ck = z_ref[:, QL:QL + KVL]
        r = lax.rsqrt(jnp.mean(ck * ck, axis=-1, keepdims=True) + EPS)
        k_ref[...] = (ck * r * kg_ref[...]).astype(BF16)
        p_ref[...] = _rope(z_ref[:, QL + KVL:], cos_ref[...], sa_ref[...], sb_ref[...]).astype(BF16)

    tab = BS((tb, LANES), lambda i: (i, 0))
    return _pcs(
        body, name=name,
        out_shape=(jax.ShapeDtypeStruct((T, QL), BF16), jax.ShapeDtypeStruct((T, KVL), BF16), jax.ShapeDtypeStruct((T, LANES), BF16)),
        grid=(T // tb,),
        in_specs=[BS((tb, W), lambda i: (i, 0)), BS((1, QL), lambda i: (0, 0)), BS((1, KVL), lambda i: (0, 0)), tab, tab, tab],
        out_specs=(BS((tb, QL), lambda i: (i, 0)), BS((tb, KVL), lambda i: (i, 0)), tab), compiler_params=_cp(("parallel",)),
    )(zm, qg.reshape(1, QL), kvg.reshape(1, KVL), *tabs)


def _mla_prep_bwd(zm, qg, kvg, tabs, dq, dk, dp, tb, name):
    T, W = zm.shape
    QL, KVL = qg.shape[-1], kvg.shape[-1]

    def body(z_ref, qg_ref, kg_ref, cos_ref, sa_ref, sb_ref, dq_ref, dk_ref, dp_ref, dz_ref, dqg_ref, dkg_ref):
        i = pl.program_id(0)

        def rms_bwd(x, g, dy):
            r = lax.rsqrt(jnp.mean(x * x, axis=-1, keepdims=True) + EPS)
            xh = x * r
            t = dy * g
            return r * (t - xh * jnp.mean(t * xh, axis=-1, keepdims=True)), jnp.sum(dy * xh, axis=0, keepdims=True)

        dcq, dg = rms_bwd(z_ref[:, :QL], qg_ref[...], dq_ref[...].astype(F32))
        dz_ref[:, :QL] = dcq.astype(BF16)
        _accum(dqg_ref, dg, i == 0)
        dck, dg = rms_bwd(z_ref[:, QL:QL + KVL], kg_ref[...], dk_ref[...].astype(F32))
        dz_ref[:, QL:QL + KVL] = dck.astype(BF16)
        _accum(dkg_ref, dg, i == 0)
        dz_ref[:, QL + KVL:] = _rope_t(dp_ref[...], cos_ref[...], sa_ref[...], sb_ref[...]).astype(BF16)

    tab = BS((tb, LANES), lambda i: (i, 0))
    return _pcs(
        body, name=name,
        out_shape=(jax.ShapeDtypeStruct((T, W), BF16), jax.ShapeDtypeStruct((1, QL), F32), jax.ShapeDtypeStruct((1, KVL), F32)),
        grid=(T // tb,),
        in_specs=[BS((tb, W), lambda i: (i, 0)), BS((1, QL), lambda i: (0, 0)), BS((1, KVL), lambda i: (0, 0)), tab, tab, tab,
                  BS((tb, QL), lambda i: (i, 0)), BS((tb, KVL), lambda i: (i, 0)), tab],
        out_specs=(BS((tb, W), lambda i: (i, 0)), BS((1, QL), lambda i: (0, 0)), BS((1, KVL), lambda i: (0, 0))),
        compiler_params=_cp(("arbitrary",)),
    )(zm, qg.reshape(1, QL), kvg.reshape(1, KVL), *tabs, dq, dk, dp)


def _attn_fwd(q, kv, kpe, tabs, L, tq, name):
    T = kv.shape[0]
    H = kv.shape[1] // QHEAD
    scale = (HEAD + ROPE) ** -0.5

    def body(q_ref, kv_ref, kpe_ref, cos_ref, sa_ref, sb_ref, o_ref, lse_ref, kcat):
        @pl.when(pl.program_id(1) == 0)
        def _():
            kcat[:, :HEAD] = kv_ref[:, :HEAD]
            kcat[:, HEAD:] = kpe_ref[...]

        sub = tq // 2
        for r in range(2):
            rs = slice(r * sub, (r + 1) * sub)
            qp = _rope(q_ref[rs, HEAD:].astype(F32), cos_ref[rs, :], sa_ref[rs, :], sb_ref[rs, :]).astype(BF16)
            qc = jnp.concatenate([q_ref[rs, :HEAD], qp], axis=1)
            s = lax.dot_general(qc, kcat[...], _DIMS["nt"], preferred_element_type=F32)
            m = jnp.max(s, axis=-1, keepdims=True)
            p = jnp.exp2((s - m) * (scale * math.log2(math.e)))
            l = jnp.sum(p, axis=-1, keepdims=True)
            o = jnp.dot(p.astype(BF16), kv_ref[:, HEAD:], preferred_element_type=F32)
            o_ref[rs, :] = (o / l).astype(BF16)
            lse_ref[0, rs, :] = m * scale + jnp.log(l)

    tab = BS((tq, LANES), lambda h, i: (i, 0))
    return _call(
        body, name=name, out_shape=(jax.ShapeDtypeStruct((L, H * HEAD), BF16), jax.ShapeDtypeStruct((H, L, 1), F32)),
        grid=(H, L // tq),
        in_specs=[BS((tq, QHEAD), lambda h, i: (i, h)), BS((T, QHEAD), lambda h, i: (0, h)), BS((T, LANES), lambda h, i: (0, 0)),
                  tab, tab, tab],
        out_specs=(BS((tq, HEAD), lambda h, i: (i, h)), BS((1, tq, 1), lambda h, i: (h, i, 0))),
        scratch=[pltpu.VMEM((T, QHEAD), BF16)], sem=("parallel", "arbitrary"), operands=(q, kv, kpe, *tabs))


def _attn_bwd(q, kv, kpe, tabs, o, lse, do, L, tq, name):
    T = kv.shape[0]
    H = kv.shape[1] // QHEAD
    scale = (HEAD + ROPE) ** -0.5
    nq = L // tq

    def body(q_ref, kv_ref, kpe_ref, cos_ref, sa_ref, sb_ref, o_ref, lse_ref, do_ref, dq_ref, dkv_ref, dkpe_ref, kcat, dk_acc, dv_acc,
             qc_scr, ds_scr, p_scr):
        h, i = pl.program_id(0), pl.program_id(1)

        @pl.when(i == 0)
        def _():
            kcat[:, :HEAD] = kv_ref[:, :HEAD]
            kcat[:, HEAD:] = kpe_ref[...]
            dk_acc[...] = jnp.zeros_like(dk_acc)
            dv_acc[...] = jnp.zeros_like(dv_acc)

        sub = tq // 2
        log2e = math.log2(math.e)
        for r in range(2):
            rs = slice(r * sub, (r + 1) * sub)
            cos, sa, sb = cos_ref[rs, :], sa_ref[rs, :], sb_ref[rs, :]
            qp = _rope(q_ref[rs, HEAD:].astype(F32), cos, sa, sb).astype(BF16)
            qc_scr[rs, :] = jnp.concatenate([q_ref[rs, :HEAD], qp], axis=1)
            s = lax.dot_general(qc_scr[rs, :], kcat[...], _DIMS["nt"], preferred_element_type=F32)
            p = jnp.exp2(s * (scale * log2e) - lse_ref[0, rs, :] * log2e)
            dov = do_ref[rs, :]
            delta = jnp.sum(dov.astype(F32) * o_ref[rs, :].astype(F32), axis=-1, keepdims=True)
            dp = lax.dot_general(dov, kv_ref[:, HEAD:], _DIMS["nt"], preferred_element_type=F32)
            ds_scr[rs, :] = (p * (dp - delta) * scale).astype(BF16)
            p_scr[rs, :] = p.astype(BF16)
            dqc = jnp.dot(ds_scr[rs, :], kcat[...], preferred_element_type=F32)
            dq_ref[rs, :HEAD] = dqc[:, :HEAD].astype(BF16)
            dq_ref[rs, HEAD:] = _rope_t(dqc[:, HEAD:], cos, sa, sb).astype(BF16)
        dk_acc[...] += lax.dot_general(ds_scr[...], qc_scr[...], _DIMS["tn"], preferred_element_type=F32)
        dv_acc[...] += lax.dot_general(p_scr[...], do_ref[...], _DIMS["tn"], preferred_element_type=F32)

        @pl.when(i == nq - 1)
        def _():
            dkv_ref[:, :HEAD] = dk_acc[:, :HEAD].astype(BF16)
            dkv_ref[:, HEAD:] = dv_acc[...].astype(BF16)

            @pl.when(h == 0)
            def _():
                dkpe_ref[...] = dk_acc[:, HEAD:]

            @pl.when(h > 0)
            def _():
                dkpe_ref[...] += dk_acc[:, HEAD:]

    tab = BS((tq, LANES), lambda h, i: (i, 0))
    return _call(
        body, name=name,
        out_shape=(jax.ShapeDtypeStruct((L, H * QHEAD), BF16), jax.ShapeDtypeStruct((T, H * QHEAD), BF16), jax.ShapeDtypeStruct((T, LANES), F32)),
        grid=(H, nq),
        in_specs=[BS((tq, QHEAD), lambda h, i: (i, h)), BS((T, QHEAD), lambda h, i: (0, h)), BS((T, LANES), lambda h, i: (0, 0)),
                  tab, tab, tab, BS((tq, HEAD), lambda h, i: (i, h)), BS((1, tq, 1), lambda h, i: (h, i, 0)),
                  BS((tq, HEAD), lambda h, i: (i, h))],
        out_specs=(BS((tq, QHEAD), lambda h, i: (i, h)), BS((T, QHEAD), lambda h, i: (0, h)), BS((T, LANES), lambda h, i: (0, 0))),
        scratch=[pltpu.VMEM((T, QHEAD), BF16), pltpu.VMEM((T, QHEAD), F32), pltpu.VMEM((T, HEAD), F32),
                 pltpu.VMEM((tq, QHEAD), BF16), pltpu.VMEM((tq, T), BF16), pltpu.VMEM((tq, T), BF16)],
        sem=("arbitrary", "arbitrary"), operands=(q, kv, kpe, *tabs, o, lse, do))


def _adamw(w, g, m, v, name):
    R, C = w.shape
    tr = _row_tile(R, C)
    c1 = 1.0 / (1.0 - ADAM_B1 ** ADAM_STEP)
    c2 = 1.0 / (1.0 - ADAM_B2 ** ADAM_STEP)

    def body(w_ref, g_ref, m_ref, v_ref, d_ref, nm_ref, nv_ref):
        g_ = g_ref[...]
        nm = ADAM_B1 * m_ref[...] + (1.0 - ADAM_B1) * g_
        nv = ADAM_B2 * v_ref[...] + (1.0 - ADAM_B2) * (g_ * g_)
        nm_ref[...] = nm
        nv_ref[...] = nv
        d_ref[...] = -ADAM_LR * ((nm * c1) / (jnp.sqrt(nv * c2) + ADAM_EPS) + ADAM_WD * w_ref[...])

    blk = BS((tr, C), lambda i: (i, 0))
    sd = jax.ShapeDtypeStruct((R, C), F32)
    return pl.pallas_call(body, name=name, out_shape=(sd, sd, sd), grid=(R // tr,), in_specs=[blk] * 4, out_specs=(blk,) * 3,
                          compiler_params=_cp(("parallel",)))(w, g, m, v)


def _sum_lead(a, out_dtype, name):
    n, R, C = a.shape
    tr = _row_tile(R, C * n, 2 << 20)

    def body(a_ref, o_ref):
        acc = a_ref[0].astype(F32)
        for k in range(1, n):
            acc = acc + a_ref[k].astype(F32)
        o_ref[...] = acc.astype(out_dtype)

    return pl.pallas_call(body, name=name, out_shape=jax.ShapeDtypeStruct((R, C), out_dtype), grid=(R // tr,),
                          in_specs=[BS((n, tr, C), lambda i: (0, i, 0))], out_specs=BS((tr, C), lambda i: (i, 0)),
                          compiler_params=_cp(("parallel",)))(a)


def _cctx_grad(parts, c_ctx, name):
    n, D = parts.shape

    def body(p_ref, c_ref, o_ref):
        d = jnp.sum(p_ref[...], axis=0, keepdims=True)
        c = c_ref[...]
        s = _sigmoid(c)
        o_ref[...] = d * s * (1.0 + c * (1.0 - s))

    return pl.pallas_call(body, name=name, out_shape=jax.ShapeDtypeStruct((1, D), F32))(parts, c_ctx.reshape(1, D))


def _me():
    return lax.axis_index("x"), lax.axis_index("y"), lax.axis_index("c")


def _aligned(v, n):
    return v if isinstance(v, int) else pl.multiple_of(v, n)


def _window(ref, r0, c0, R, C):
    rows = pl.ds(_aligned(r0, SUBLANES), R)
    if C == ref.shape[1]:
        return ref.at[rows, :]
    return ref.at[rows, pl.ds(_aligned(c0, LANES), C)]


def _allgather8(items, name):
    n = len(items)

    def body(*refs):
        srcs, dsts = refs[:n], refs[n:2 * n]
        send_sems, recv_sems, local_sems = refs[2 * n:]
        x, y, c = _me()
        me, sibling = (x, y, c), (x, y, 1 - c)
        chips = [(1 - x, y), (x, 1 - y), (1 - x, 1 - y)]

        def dwin(a, blk):
            (R, C), at = items[a][2], items[a][4]
            return _window(dsts[a], *at(*blk), R, C)

        def swin(a):
            (R, C), at = items[a][2], items[a][1]
            return _window(srcs[a], *at(*me), R, C)

        def copy(a, k, blk, to, src=None):
            return pltpu.make_async_remote_copy(
                src_ref=dwin(a, blk) if src is None else src, dst_ref=dwin(a, blk), send_sem=send_sems.at[7 * a + k],
                recv_sem=recv_sems.at[7 * a + k], device_id=to, device_id_type=MESH)

        mine = [pltpu.make_async_copy(swin(a), dwin(a, me), local_sems.at[a]) for a in range(n)]
        for cp in mine:
            cp.start()
        first = []
        for a in range(n):
            first.append(copy(a, 0, me, sibling, src=swin(a)))
            first += [copy(a, 1 + j, me, (*chip, c), src=swin(a)) for j, chip in enumerate(chips)]
        for cp in first:
            cp.start()
        passed = []
        for j, chip in enumerate(chips):
            for a in range(n):
                copy(a, 1 + j, (*chip, c), me).wait_recv()
                fwd = copy(a, 4 + j, (*chip, c), sibling)
                fwd.start()
                passed.append(fwd)
        for a in range(n):
            copy(a, 0, sibling, me).wait_recv()
            for j, chip in enumerate(chips):
                copy(a, 4 + j, (*chip, 1 - c), me).wait_recv()
        for cp in first + passed:
            cp.wait_send()
        for cp in mine:
            cp.wait()

    outs = pl.pallas_call(
        body, name=name, out_shape=tuple(jax.ShapeDtypeStruct(it[3], it[0].dtype) for it in items),
        in_specs=[ANY] * n, out_specs=tuple([ANY] * n),
        scratch_shapes=[pltpu.SemaphoreType.DMA((7 * n,)), pltpu.SemaphoreType.DMA((7 * n,)), pltpu.SemaphoreType.DMA((n,))],
    )(*[it[0] for it in items])
    return list(outs)


def _cast_place(w, kind, name, lead=None):
    R2, C = w.shape[-2:]
    tr = _row_tile(R2, C)
    nr = R2 // tr
    xi, yi, _ = _me()
    p_arr = (2 * xi + yi).astype(jnp.int32).reshape(1)
    if kind == "col":
        shape, o_spec = (R2, 4 * C), BS((tr, C), lambda i, p: (i, p[0]))
    else:
        shape, o_spec = (4 * R2, C), BS((tr, C), lambda i, p: (p[0] * nr + i, 0))

    def body(p_ref, w_ref, o_ref):
        o_ref[...] = w_ref[...].astype(BF16)

    return pl.pallas_call(
        body, name=name, out_shape=jax.ShapeDtypeStruct(shape, BF16),
        grid_spec=pltpu.PrefetchScalarGridSpec(
            num_scalar_prefetch=1, grid=(nr,), out_specs=o_spec,
            in_specs=[BS((tr, C), lambda i, p: (i, 0)) if lead is None else BS((None, tr, C), lambda i, p: (lead, i, 0))]),
        compiler_params=_cp(("parallel",)),
    )(p_arr, w)


def _remote(src, dst, sends, recvs, k, to):
    return pltpu.make_async_remote_copy(src_ref=src, dst_ref=dst, send_sem=sends.at[k], recv_sem=recvs.at[k], device_id=to,
                                        device_id_type=MESH)


def _gather_side(bufs, pieces):
    def win(b, piece, p, pc):
        bi, (R, C), at, k, i = piece
        r0, c0 = at(p, pc)
        return _window(b[bi], r0 + i * (R // k), c0, R // k, C)

    def chips_of(x, y):
        return [(1 - x, y), (x, 1 - y), (1 - x, 1 - y)]

    def outgoing(b, sends, recvs):
        x, y, c = _me()
        cps = []
        for m, piece in enumerate(pieces):
            mine = win(b, piece, 2 * x + y, c)
            cps += [_remote(mine, mine, sends, recvs, 6 * m + j, (px, py, c)) for j, (px, py) in enumerate(chips_of(x, y))]
        return cps

    def start(b, sends, recvs):
        for cp in outgoing(b, sends, recvs):
            cp.start()

    def finish(b, sends, recvs):
        x, y, c = _me()
        sibling = (x, y, 1 - c)
        passed = []
        for j, (px, py) in enumerate(chips_of(x, y)):
            for m, piece in enumerate(pieces):
                landed = win(b, piece, 2 * px + py, c)
                _remote(landed, landed, sends, recvs, 6 * m + j, (px, py, c)).wait_recv()
                fwd = _remote(landed, landed, sends, recvs, 6 * m + 3 + j, sibling)
                fwd.start()
                passed.append(fwd)
        for j, (px, py) in enumerate(chips_of(x, y)):
            for m, piece in enumerate(pieces):
                theirs = win(b, piece, 2 * px + py, 1 - c)
                _remote(theirs, theirs, sends, recvs, 6 * m + 3 + j, sibling).wait_recv()
        for cp in outgoing(b, sends, recvs) + passed:
            cp.wait_send()

    return _Side(bufs, 6 * len(pieces), start, finish)


_RELS = [(dx, dy, dc) for dx in (0, 1) for dy in (0, 1) for dc in (0, 1)][1:]


def _rs_side(bufs, pieces):
    def flip(v, d):
        return 1 - v if d else v

    def copies(b, sends, recvs, landing):
        x, y, c = _me()
        dev = 4 * x + 2 * y + c
        cps = []
        for m, (gi, ri, (R, C), at, k, i) in enumerate(pieces):
            rows = R // k
            for t, (dx, dy, dc) in enumerate(_RELS):
                tx, ty, tc = flip(x, dx), flip(y, dy), flip(c, dc)
                if landing:
                    theirs = b[ri].at[4 * tx + 2 * ty + tc, pl.ds(i * rows, rows), :]
                    cps.append(_remote(theirs, theirs, sends, recvs, 7 * m + t, (tx, ty, tc)))
                else:
                    r0, c0 = at(2 * tx + ty, tc)
                    src = _window(b[gi], r0 + i * rows, c0, rows, C)
                    cps.append(_remote(src, b[ri].at[dev, pl.ds(i * rows, rows), :], sends, recvs, 7 * m + t, (tx, ty, tc)))
        return cps

    def start(b, sends, recvs):
        for cp in copies(b, sends, recvs, False):
            cp.start()

    def finish(b, sends, recvs):
        for cp in copies(b, sends, recvs, True):
            cp.wait_recv()
        for cp in copies(b, sends, recvs, False):
            cp.wait_send()

    return _Side(bufs, 7 * len(pieces), start, finish)


def _comm_only(side, name):
    n = len(side.bufs)

    def body(*refs):
        bufs, (sends, recvs) = refs[n:2 * n], refs[2 * n:]
        side.start(bufs, sends, recvs)
        side.finish(bufs, sends, recvs)

    outs = pl.pallas_call(
        body, name=name, out_shape=tuple(jax.ShapeDtypeStruct(b.shape, b.dtype) for b in side.bufs),
        in_specs=[ANY] * n, out_specs=tuple([ANY] * n), input_output_aliases={a: a for a in range(n)},
        scratch_shapes=[pltpu.SemaphoreType.DMA((side.nsem,)), pltpu.SemaphoreType.DMA((side.nsem,))],
    )(*side.bufs)
    return list(outs)


def _rs_sum8(grad, recv, win, kind, name):
    R, C = win
    tr = _row_tile(R, 4 * C)
    nr = R // tr
    xi, yi, ci = _me()
    s_arr = jnp.stack([4 * xi + 2 * yi + ci, ci]).astype(jnp.int32)

    if kind == "col":
        g_spec = BS((tr, C), lambda i, s: (s[1] * nr + i, s[0] // 2))
    else:
        g_spec = BS((tr, C), lambda i, s: (s[0] * nr + i, 0))

    def body(s_ref, g_ref, *refs):
        acc = g_ref[...].astype(F32)
        for r_ref in refs[:7]:
            acc = acc + r_ref[0].astype(F32)
        refs[7][...] = acc

    def other(t):
        return BS((1, tr, C), lambda i, s: (jnp.bitwise_xor(s[0], t), i, 0))

    return pl.pallas_call(
        body, name=name, out_shape=jax.ShapeDtypeStruct((2 * R, C), F32),
        grid_spec=pltpu.PrefetchScalarGridSpec(
            num_scalar_prefetch=1, grid=(nr,), in_specs=[g_spec] + [other(t) for t in range(1, 8)],
            out_specs=BS((tr, C), lambda i, s: (s[1] * nr + i, 0))),
        compiler_params=_cp(("parallel",)),
    )(s_arr, grad, *([recv] * 7))


def _rs_share(shards, name):
    n = len(shards)

    def body(*refs):
        bufs = refs[n:2 * n]
        send_sems, recv_sems = refs[2 * n:]
        x, y, c = _me()
        sibling = (x, y, 1 - c)
        remote = []
        for a in range(n):
            R = bufs[a].shape[0] // 2
            mine = bufs[a].at[pl.ds(pl.multiple_of(c * R, SUBLANES), R), :]
            cp = pltpu.make_async_remote_copy(src_ref=mine, dst_ref=mine, send_sem=send_sems.at[a], recv_sem=recv_sems.at[a],
                                              device_id=sibling, device_id_type=MESH)
            cp.start()
            remote.append(cp)
        for a in range(n):
            R = bufs[a].shape[0] // 2
            theirs = bufs[a].at[pl.ds(pl.multiple_of((1 - c) * R, SUBLANES), R), :]
            pltpu.make_async_remote_copy(src_ref=theirs, dst_ref=theirs, send_sem=send_sems.at[a], recv_sem=recv_sems.at[a],
                                         device_id=sibling, device_id_type=MESH).wait_recv()
        for cp in remote:
            cp.wait_send()

    outs = pl.pallas_call(
        body, name=name, out_shape=tuple(jax.ShapeDtypeStruct(h.shape, h.dtype) for h in shards),
        in_specs=[ANY] * n, out_specs=tuple([ANY] * n), input_output_aliases={a: a for a in range(n)},
        scratch_shapes=[pltpu.SemaphoreType.DMA((n,)), pltpu.SemaphoreType.DMA((n,))],
    )(*shards)
    return list(outs)


BLOB_ALIGN = SUBLANES * LANES


def _pack(arrs):
    flat = jnp.concatenate([a.reshape(-1).astype(F32) for a in arrs])
    n = flat.shape[0]
    padded = -(-n // BLOB_ALIGN) * BLOB_ALIGN
    return jnp.pad(flat, (0, padded - n)).reshape(padded // LANES, LANES)


def _unpack(flat, shapes):
    out, off = [], 0
    for s in shapes:
        n = math.prod(s)
        out.append(flat[..., off:off + n].reshape(flat.shape[:-1] + tuple(s)))
        off += n
    return out


def _gather_blob(blob, name):
    r = blob.shape[0]
    at = lambda px, py, pc: ((4 * px + 2 * py + pc) * r, 0)
    (out,) = _allgather8([(blob, lambda px, py, pc: (0, 0), (r, LANES), (8 * r, LANES), at)], name)
    return out.reshape(8, r * LANES)


def _conv_ffn_fwd(mm, full, layer, X, mods, n2g, conv_w, conv_b, L, tb, nlat, ch, tag):
    sh2, sc2, g2 = mods[3], mods[4], mods[5]
    h2 = _norm_mod_fwd(X, n2g, sh2, sc2, tb, nlat, tag + "_norm2")
    zf = mm(h2, full[f"ffn_w_up{layer}"], "nn", BF16, tag + "_up", tm=(544, 512), tn=(1408, 512))
    f = _ffn_act_fwd(zf, conv_w, conv_b, L, ch, tag + "_act")
    yf = mm(f, full[f"ffn_w_down{layer}"], "nn", F32, tag + "_down", tm=(544, 512), tn=(512,))
    Xn = _gate_res_fwd(X, yf, g2, tb, nlat, tag + "_res2")
    return Xn, (X, h2, zf, f, yf)


def _conv_ffn_bwd(mm, full, gbuf, layer, dXn, saved, mods, n2g, conv_w, conv_b, L, tb, nlat, ch, tag):
    X, h2, zf, f, yf = saved
    sc2, g2 = mods[4], mods[5]
    w_up, w_down = full[f"ffn_w_up{layer}"], full[f"ffn_w_down{layer}"]
    dy, dg2 = _gate_bwd(dXn, yf, g2, tb, nlat, tag + "_dres2")
    df = mm(dy, w_down, "nt", BF16, tag + "_ddown_x", tm=(544, 512), tn=(1408, 512))
    gbuf[f"ffn_w_down{layer}"] = mm(f, dy, "tn", BF16, tag + "_ddown_w", tm=(512,), tn=(1024, 512))
    dgp, du, dcw, dcb = _ffn_act_bwd(zf, df, conv_w, conv_b, L, ch, tag + "_dact")
    dzf = jnp.concatenate([dgp, du], axis=1)
    gbuf[f"ffn_w_up{layer}"] = mm(h2, dzf, "tn", BF16, tag + "_dup_w", tm=(512,), tn=(1024, 512))
    dh2 = mm(dzf, w_up, "nt", BF16, tag + "_dup_x", tm=(1088, 1024, 512), tn=(1024,), tk=(2816, 512))
    dX, dn2g, dsh2, dsc2 = _norm_mod_bwd(X, n2g, sc2, dh2, dXn, tb, nlat, tag + "_dnorm2")
    return dX, dict(n2g=dn2g, sh2=dsh2, sc2=dsc2, g2=dg2, cw=dcw, cb=dcb)


def kernel(x, c, ctx, c_ctx, norm1_g, norm2_g, w_ada, b_ada, ab_w_in, ab_b_in, a_ln_g, a_ln_b, a_w_s, a_b_s, b_conv_w, b_conv_b, b_ln_g, b_ln_b, ab_w_out, mla_w_in, mla_q_norm_g, mla_w_uq, mla_kv_norm_g, mla_w_ukv, mla_w_o, ffn_w_up, ffn_conv_w, ffn_conv_b, ffn_w_down, final_norm_g, loss_target, m_c_ctx, m_norm1_g, m_norm2_g, m_w_ada, m_b_ada, m_ab_w_in, m_ab_b_in, m_a_ln_g, m_a_ln_b, m_a_w_s, m_a_b_s, m_b_conv_w, m_b_conv_b, m_b_ln_g, m_b_ln_b, m_ab_w_out, m_mla_w_in, m_mla_q_norm_g, m_mla_w_uq, m_mla_kv_norm_g, m_mla_w_ukv, m_mla_w_o, m_ffn_w_up, m_ffn_conv_w, m_ffn_conv_b, m_ffn_w_down, m_final_norm_g, v_c_ctx, v_norm1_g, v_norm2_g, v_w_ada, v_b_ada, v_ab_w_in, v_ab_b_in, v_a_ln_g, v_a_ln_b, v_a_w_s, v_a_b_s, v_b_conv_w, v_b_conv_b, v_b_ln_g, v_b_ln_b, v_ab_w_out, v_mla_w_in, v_mla_q_norm_g, v_mla_w_uq, v_mla_kv_norm_g, v_mla_w_ukv, v_mla_w_o, v_ffn_w_up, v_ffn_conv_w, v_ffn_conv_b, v_ffn_w_down, v_final_norm_g):
    W = dict(c_ctx=c_ctx, norm1_g=norm1_g, norm2_g=norm2_g, w_ada=w_ada, b_ada=b_ada, ab_w_in=ab_w_in, ab_b_in=ab_b_in, a_ln_g=a_ln_g,
             a_ln_b=a_ln_b, a_w_s=a_w_s, a_b_s=a_b_s, b_conv_w=b_conv_w, b_conv_b=b_conv_b, b_ln_g=b_ln_g, b_ln_b=b_ln_b,
             ab_w_out=ab_w_out, mla_w_in=mla_w_in, mla_q_norm_g=mla_q_norm_g, mla_w_uq=mla_w_uq, mla_kv_norm_g=mla_kv_norm_g,
             mla_w_ukv=mla_w_ukv, mla_w_o=mla_w_o, ffn_w_up=ffn_w_up, ffn_conv_w=ffn_conv_w, ffn_conv_b=ffn_conv_b,
             ffn_w_down=ffn_w_down, final_norm_g=final_norm_g)
    MOM = dict(c_ctx=m_c_ctx, norm1_g=m_norm1_g, norm2_g=m_norm2_g, w_ada=m_w_ada, b_ada=m_b_ada, ab_w_in=m_ab_w_in, ab_b_in=m_ab_b_in,
               a_ln_g=m_a_ln_g, a_ln_b=m_a_ln_b, a_w_s=m_a_w_s, a_b_s=m_a_b_s, b_conv_w=m_b_conv_w, b_conv_b=m_b_conv_b,
               b_ln_g=m_b_ln_g, b_ln_b=m_b_ln_b, ab_w_out=m_ab_w_out, mla_w_in=m_mla_w_in, mla_q_norm_g=m_mla_q_norm_g,
               mla_w_uq=m_mla_w_uq, mla_kv_norm_g=m_mla_kv_norm_g, mla_w_ukv=m_mla_w_ukv, mla_w_o=m_mla_w_o, ffn_w_up=m_ffn_w_up,
               ffn_conv_w=m_ffn_conv_w, ffn_conv_b=m_ffn_conv_b, ffn_w_down=m_ffn_w_down, final_norm_g=m_final_norm_g)
    VAR = dict(c_ctx=v_c_ctx, norm1_g=v_norm1_g, norm2_g=v_norm2_g, w_ada=v_w_ada, b_ada=v_b_ada, ab_w_in=v_ab_w_in, ab_b_in=v_ab_b_in,
               a_ln_g=v_a_ln_g, a_ln_b=v_a_ln_b, a_w_s=v_a_w_s, a_b_s=v_a_b_s, b_conv_w=v_b_conv_w, b_conv_b=v_b_conv_b,
               b_ln_g=v_b_ln_g, b_ln_b=v_b_ln_b, ab_w_out=v_ab_w_out, mla_w_in=v_mla_w_in, mla_q_norm_g=v_mla_q_norm_g,
               mla_w_uq=v_mla_w_uq, mla_kv_norm_g=v_mla_kv_norm_g, mla_w_ukv=v_mla_w_ukv, mla_w_o=v_mla_w_o, ffn_w_up=v_ffn_w_up,
               ffn_conv_w=v_ffn_conv_w, ffn_conv_b=v_ffn_conv_b, ffn_w_down=v_ffn_w_down, final_norm_g=v_final_norm_g)
    ORDER = list(W.keys())

    L, D = x.shape[1], x.shape[2]
    CT = ctx.shape[1]
    T = L + CT
    AW, BW = a_ln_g.shape[-1], b_ln_g.shape[-1]
    AH = a_w_s.shape[1]
    QL, KVL = 4 * mla_q_norm_g.shape[-1], 4 * mla_kv_norm_g.shape[-1]
    H = 4 * mla_w_o.shape[1] // HEAD
    HS = H // 4
    DFF = ffn_conv_b.shape[-1]
    NA = w_ada.shape[-1]
    tb = 256 if (L % 256 == 0 and CT % 256 == 0) else 128
    nlat = L // tb
    ch = tb
    tq = 256 if L >= 512 else 128
    xi, yi, ci = _me()
    p_me = 2 * xi + yi
    dev = 4 * xi + 2 * yi + ci

    shard_small = [c[0], mla_q_norm_g[0], mla_kv_norm_g[0], b_conv_w[0], ffn_conv_w]
    g0 = _gather_blob(_pack(shard_small), "gather_small")
    c_all, qg_s, kvg_s, bcw_s, fcw_s = _unpack(g0, [a.shape for a in shard_small])
    per_chip = lambda a: a[0::2]
    qg = per_chip(qg_s).reshape(QL)
    kvg = per_chip(kvg_s).reshape(KVL)
    bcw = jnp.concatenate(list(per_chip(bcw_s)), axis=-1)
    fcw = jnp.concatenate(list(per_chip(fcw_s)), axis=-1)
    c16 = jnp.concatenate([c_all, c_ctx[None], jnp.zeros((7, D), F32)], axis=0)

    ms = []
    for i in range(2):
        bias = lax.dynamic_slice(b_ada[i], (p_me * NA,), (NA,))
        ms.append(_mm(c16, w_ada, "nn", F32, f"ada{i}", tm=(16,), tn=(512,), bias=bias, silu_a=True, b_lead=i))
    ms = jnp.concatenate(ms, axis=0)
    (mods_all,) = _allgather8(
        [(ms, lambda px, py, pc: (pc * 16, 0), (16, NA), (32, 4 * NA), lambda px, py, pc: (pc * 16, (2 * px + py) * NA))], "gather_mods")
    mods_all = mods_all.reshape(2, 16, N_MOD, D)
    mods = []
    for i in range(2):
        lat = lax.dynamic_index_in_dim(mods_all[i], dev, axis=0, keepdims=False)
        both = jnp.stack([lat, mods_all[i, 8]], axis=0)
        mods.append([both[:, k][:, None, :] for k in range(N_MOD)])

    def pad_uq(w):
        w = w.reshape(w.shape[0], HS, HEAD + ROPE)
        return jnp.pad(w, ((0, 0), (0, 0), (0, QHEAD - HEAD - ROPE))).reshape(w.shape[0], HS * QHEAD)

    MI = QL + KVL + LANES
    big = [
        ("ab_w_in", ab_w_in, 0, "col"), ("ab_w_out", ab_w_out, 0, "row"),
        ("mla_w_in", jnp.pad(mla_w_in[0], ((0, 0), (0, MI - mla_w_in.shape[-1]))), None, "row"),
        ("mla_w_uq", pad_uq(mla_w_uq[0]), None, "col"), ("mla_w_ukv", mla_w_ukv, 0, "col"), ("mla_w_o", mla_w_o, 0, "row"),
        ("ffn_w_up0", ffn_w_up, 0, "col"), ("ffn_w_up1", ffn_w_up, 1, "col"),
        ("ffn_w_down0", ffn_w_down, 0, "row"), ("ffn_w_down1", ffn_w_down, 1, "row"),
    ]
    full, wins = {}, {}
    for nm, w, lead, kind in big:
        R, C = w.shape[-2] // 2, w.shape[-1]
        if kind == "col":
            wins[nm] = ((R, C), lambda p, pc, R=R, C=C: (pc * R, p * C), kind)
        else:
            wins[nm] = ((R, C), lambda p, pc, R=R, C=C: ((2 * p + pc) * R, 0), kind)
        full[nm] = _cast_place(w, kind, "cast_" + nm, lead)
    names = [b[0] for b in big]

    gather_plan = {
        "l0_norm1": [("ab_w_out", 1, 0)], "l0_in": [("ffn_w_up0", 4, 0)], "l0_conv": [("ffn_w_up0", 4, 1)],
        "l0_out": [("ffn_w_up0", 4, 2)], "l0_res1": [("ffn_w_up0", 4, 3)],
        "l0_ffn_up": [("ffn_w_down0", 1, 0), ("mla_w_in", 1, 0), ("mla_w_uq", 1, 0)],
        "l0_ffn_down": [("mla_w_ukv", 1, 0), ("mla_w_o", 1, 0)],
        "l1_attn": [("ffn_w_up1", 1, 0), ("ffn_w_down1", 1, 0)],
    }
    reduce_plan = {
        "l1_ffn_dup_x": [("ffn_w_down1", 1, 0)], "l1_dattn": [("ffn_w_up1", 1, 0), ("mla_w_o", 1, 0)],
        "l0_ffn_ddown_x": [("mla_w_ukv", 1, 0), ("mla_w_uq", 1, 0)], "l0_ffn_ddown_w": [("mla_w_in", 1, 0)],
        "l0_ffn_dact": [("ffn_w_down0", 2, 0)], "l0_ffn_dup_w": [("ffn_w_down0", 2, 1)],
        "l0_ffn_dup_x": [("ffn_w_up0", 8, 0), ("ffn_w_up0", 8, 1), ("ffn_w_up0", 8, 2)], "l0_ffn_dnorm2": [("ffn_w_up0", 8, 3)],
        "l0_dout_w": [("ffn_w_up0", 8, 4)], "l0_dout_x": [("ffn_w_up0", 8, 5)], "l0_dgmlp": [("ffn_w_up0", 8, 6)],
        "l0_dconv": [("ffn_w_up0", 8, 7), ("ab_w_out", 1, 0)],
        "l0_din_x": [("ab_w_in", 2, 0)], "l0_dnorm1": [("ab_w_in", 2, 1)],
    }
    gbuf, rbuf = {}, {}

    def gather_side(req):
        nms = list(dict.fromkeys(nm for nm, _, _ in req))
        side = _gather_side([full[nm] for nm in nms], [(nms.index(nm), wins[nm][0], wins[nm][1], k, i) for nm, k, i in req])

        def commit(bufs):
            full.update(zip(nms, bufs))

        return side, commit

    def reduce_side(req):
        nms = list(dict.fromkeys(nm for nm, _, _ in req))
        for nm in nms:
            if nm not in rbuf:
                rbuf[nm] = lax.empty((8,) + wins[nm][0], BF16)
        n = len(nms)
        side = _rs_side([gbuf[nm] for nm in nms] + [rbuf[nm] for nm in nms],
                        [(nms.index(nm), n + nms.index(nm), wins[nm][0], wins[nm][1], k, i) for nm, k, i in req])

        def commit(bufs):
            gbuf.update(zip(nms, bufs[:n]))
            rbuf.update(zip(nms, bufs[n:]))

        return side, commit

    _carry_plan.clear()
    _carry_plan.update({name: functools.partial(gather_side, req) for name, req in gather_plan.items()})
    _carry_plan.update({name: functools.partial(reduce_side, req) for name, req in reduce_plan.items()})
    mm, attn_fwd, attn_bwd = _mm, _attn_fwd, _attn_bwd

    side, commit = gather_side([("ab_w_in", 1, 0)])
    commit(_comm_only(side, "gather_first"))

    X0 = jnp.concatenate([x[0], ctx[0]], axis=0)
    m0, m1 = mods[0], mods[1]
    h1 = _norm_mod_fwd(X0, norm1_g[0], m0[0], m0[1], tb, nlat, "l0_norm1")
    z = mm(h1, full["ab_w_in"], "nn", BF16, "l0_in", tm=(544, 512), tn=(1024, 512), bias=ab_b_in[0])
    bs_full = jnp.broadcast_to(a_b_s[0][:, :, None], (AH, CHUNK, CHUNK))
    ya = _gmlp_fwd(z, a_ln_g[0], a_ln_b[0], a_w_s[0], bs_full, tb, "l0_gmlp")
    hc = _glu_conv_fwd(z, 2 * AW, bcw, b_conv_b[0], L, ch, "l0_conv")
    yb = _ln_silu_fwd(hc, b_ln_g[0], b_ln_b[0], tb, "l0_lnsilu")
    yab = jnp.concatenate([ya, yb], axis=1)
    y0 = mm(yab, full["ab_w_out"], "nn", F32, "l0_out", tm=(544, 512), tn=(1024, 512))
    X1 = _gate_res_fwd(X0, y0, m0[2], tb, nlat, "l0_res1")
    X2, ffn0 = _conv_ffn_fwd(mm, full, 0, X1, m0, norm2_g[0], fcw[0], ffn_conv_b[0], L, tb, nlat, ch, "l0_ffn")

    tabs = _rope_tables(L, T)
    hm = _norm_mod_fwd(X2, norm1_g[1], m1[0], m1[1], tb, nlat, "l1_norm1")
    zm = mm(hm, full["mla_w_in"], "nn", F32, "l1_in", tm=(544, 512), tn=(MI,))
    cqn, ckvn, kpe = _mla_prep_fwd(zm, qg, kvg, tabs, tb, "l1_prep")
    q = mm(cqn, full["mla_w_uq"], "nn", BF16, "l1_uq", tm=(512,), tn=(1024, 512), rows=L)
    kvh = mm(ckvn, full["mla_w_ukv"], "nn", BF16, "l1_ukv", tm=(544, 512), tn=(1024, 512))
    tq2 = 2 * tq if L % (2 * tq) == 0 and L > 2 * tq else tq
    o, lse = attn_fwd(q, kvh, kpe, tabs, L, tq2, "l1_attn")
    yl = mm(o, full["mla_w_o"], "nn", F32, "l1_o", tm=(512,), tn=(1024, 512))
    m1_lat = [a[:1] for a in m1]
    X3 = _gate_res_fwd(X2, yl, m1_lat[2], tb, nlat, "l1_res1")
    X4, ffn1 = _conv_ffn_fwd(mm, full, 1, X3, m1_lat, norm2_g[1], fcw[1], ffn_conv_b[1], L, tb, nlat, ch, "l1_ffn")
    loss_acc, dX4, dfinal = _final_loss(X4, final_norm_g, loss_target[0], tb, "loss")

    dX3, gf1 = _conv_ffn_bwd(mm, full, gbuf, 1, dX4, ffn1, m1_lat, norm2_g[1], fcw[1], ffn_conv_b[1], L, tb, nlat, ch, "l1_ffn")
    dyl, dg1_1 = _gate_bwd(dX3, yl, m1_lat[2], tb, nlat, "l1_dres1")
    do = mm(dyl, full["mla_w_o"], "nt", BF16, "l1_do_x", tm=(512,), tn=(1024, 512))
    gbuf["mla_w_o"] = mm(o, dyl, "tn", BF16, "l1_do_w", tm=(512,), tn=(1024, 512))
    dq, dkv, dkpe = attn_bwd(q, kvh, kpe, tabs, o, lse, do, L, tq2, "l1_dattn")
    dckvn = mm(dkv, full["mla_w_ukv"], "nt", BF16, "l1_dukv_x", tm=(544, 512), tn=(KVL,), tk=(2048, 512))
    gbuf["mla_w_ukv"] = mm(ckvn, dkv, "tn", BF16, "l1_dukv_w", tm=(512,), tn=(1024, 512))
    dcqn = mm(dq, full["mla_w_uq"], "nt", BF16, "l1_duq_x", tm=(512,), tn=(QL,), tk=(2048, 512))
    gbuf["mla_w_uq"] = mm(cqn, dq, "tn", BF16, "l1_duq_w", tm=(QL,), tn=(1024, 512), rows=L)
    dcqn = jnp.concatenate([dcqn, jnp.zeros((CT, QL), BF16)], axis=0)
    dzm, dqg, dkvg = _mla_prep_bwd(zm, qg, kvg, tabs, dcqn, dckvn, dkpe, tb, "l1_dprep")
    dhm = mm(dzm, full["mla_w_in"], "nt", BF16, "l1_din_x", tm=(544, 512), tn=(1024, 512))
    gbuf["mla_w_in"] = mm(hm, dzm, "tn", BF16, "l1_din_w", tm=(512,), tn=(MI,))
    dX2, dn1g_1, dsh1_1, dsc1_1 = _norm_mod_bwd(X2, norm1_g[1], m1[1], dhm, dX3, tb, nlat, "l1_dnorm1")

    dX1, gf0 = _conv_ffn_bwd(mm, full, gbuf, 0, dX2, ffn0, m0, norm2_g[0], fcw[0], ffn_conv_b[0], L, tb, nlat, ch, "l0_ffn")
    dy0, dg1_0 = _gate_bwd(dX1, y0, m0[2], tb, nlat, "l0_dres1")
    gbuf["ab_w_out"] = mm(yab, dy0, "tn", BF16, "l0_dout_w", tm=(512,), tn=(1024, 512))
    dyab = mm(dy0, full["ab_w_out"], "nt", BF16, "l0_dout_x", tm=(544, 512), tn=(1024, 512))
    dzu, dzv, dlnag, dlnab, dws, dbs, csu, csv = _gmlp_bwd(z, dyab, a_ln_g[0], a_ln_b[0], a_w_s[0], bs_full, tb, "l0_dgmlp")
    dhc, dlnbg, dlnbb = _ln_silu_bwd(hc, dyab, AW, b_ln_g[0], b_ln_b[0], tb, "l0_dlnsilu")
    dza, dzg, dbcw, dbcb, csa, csg = _glu_conv_bwd(z, 2 * AW, dhc, bcw, L, ch, "l0_dconv")
    dz = jnp.concatenate([dzu, dzv, dza, dzg], axis=1)
    dbin = jnp.concatenate([csu, csv, csa, csg], axis=1)
    gbuf["ab_w_in"] = mm(h1, dz, "tn", BF16, "l0_din_w", tm=(512,), tn=(1024, 512))
    dh1 = mm(dz, full["ab_w_in"], "nt", BF16, "l0_din_x", tm=(544, 512), tn=(1024, 512), tk=(2048, 512))
    dX0, dn1g_0, dsh1_0, dsc1_0 = _norm_mod_bwd(X0, norm1_g[0], m0[1], dh1, dX1, tb, nlat, "l0_dnorm1")
    grad_x = dX0[:L][None]

    halves =[_rs_sum8(gbuf[nm], rbuf[nm], wins[nm][0], wins[nm][2], "rs_sum_" + nm) for nm in names]
    gshard = dict(zip(names, _rs_share(halves, "rs_share")))

    def grp6(l, sh1, sc1, g1, f):
        G = sh1.shape[0]
        pad = lambda a: jnp.concatenate([a, jnp.zeros((G - a.shape[0],) + a.shape[1:], F32)], axis=0) if a.shape[0] < G else a
        return jnp.concatenate([pad(a) for a in (sh1, sc1, g1, f["sh2"], f["sc2"], f["g2"])], axis=1)

    dm0 = grp6(0, dsh1_0, dsc1_0, dg1_0, gf0)
    dm1 = grp6(1, dsh1_1, dsc1_1, dg1_1, gf1)
    dmods = jnp.stack([dm0, dm1], axis=0)
    small = [
        jnp.concatenate([dn1g_0, dn1g_1], axis=0), jnp.concatenate([gf0["n2g"], gf1["n2g"]], axis=0), dbin, dlnag, dlnab, dws,
        dbs, dbcw, dbcb, dlnbg, dlnbb, dqg, dkvg, jnp.stack([gf0["cw"], gf1["cw"]], axis=0),
        jnp.concatenate([gf0["cb"], gf1["cb"]], axis=0), dfinal, dmods[:, 1],
    ]
    small_shapes = [a.shape for a in small]
    lat_shape = dmods[:, 0].shape
    blob = _pack(small + [dmods[:, 0], loss_acc[0, :1]])
    gathered = _gather_blob(blob, "gather_grads")
    summed = _sum_lead(gathered.reshape(8, -1, LANES), F32, "sum_grads").reshape(-1)
    (dn1g, dn2g, dbin_s, dlnag_s, dlnab_s, dws_s, dbs_s, dbcw_s, dbcb_s, dlnbg_s, dlnbb_s, dqg_s, dkvg_s, dfcw_s, dfcb_s, dfinal_s,
     dmods_ctx, dmods_lat_sum, loss) = _unpack(summed, small_shapes + [lat_shape, (1,)])
    loss = loss.reshape(())
    n_small = sum(math.prod(s) for s in small_shapes)
    dmods_lat = gathered[:, n_small:n_small + math.prod(lat_shape)].reshape((8,) + lat_shape)

    grad_w_ada, dc_parts = [], []
    for i in range(2):
        dm16 = jnp.concatenate([dmods_lat[:, i].reshape(8, N_MOD * D), dmods_ctx[i].reshape(1, N_MOD * D),
                                jnp.zeros((7, N_MOD * D), F32)], axis=0)
        dm16_s = lax.dynamic_slice(dm16, (0, p_me * NA), (16, NA))
        grad_w_ada.append(_mm(c16, dm16_s, "tn", F32, f"dada{i}_w", tm=(1024, 512), tn=(1024, 512), silu_a=True))
        dc_parts.append(_mm(dm16_s, w_ada, "nt", F32, f"dada{i}_c", tm=(16,), tn=(512,), tk=(1024, 512), b_lead=i))
    grad_w_ada = jnp.stack(grad_w_ada, axis=0)
    grad_b_ada = dmods_lat_sum.reshape(2, N_MOD * D) + dmods_ctx.reshape(2, N_MOD * D)
    dc_blob = _pack([dc_parts[0][8] + dc_parts[1][8]])
    dc_all = _gather_blob(dc_blob, "gather_dc")[0::2, :D]
    grad_c_ctx = _cctx_grad(dc_all, c_ctx, "dcctx").reshape(D)

    def my_cols(a, axis, n):
        return lax.dynamic_slice_in_dim(a, p_me * n, n, axis=axis)

    unpad_uq = lambda g: g.reshape(QL, HS, QHEAD)[:, :, :HEAD + ROPE].reshape(QL, HS * (HEAD + ROPE))
    grads = dict(
        c_ctx=grad_c_ctx, norm1_g=dn1g, norm2_g=dn2g, w_ada=grad_w_ada, b_ada=grad_b_ada, ab_w_in=gshard["ab_w_in"][None],
        ab_b_in=dbin_s, a_ln_g=dlnag_s, a_ln_b=dlnab_s, a_w_s=dws_s[None], a_b_s=dbs_s.reshape(1, AH, CHUNK),
        b_conv_w=my_cols(dbcw_s, 1, BW // 4)[None], b_conv_b=dbcb_s, b_ln_g=dlnbg_s, b_ln_b=dlnbb_s,
        ab_w_out=gshard["ab_w_out"][None], mla_w_in=gshard["mla_w_in"][:, :mla_w_in.shape[-1]][None],
        mla_q_norm_g=my_cols(dqg_s, 1, QL // 4), mla_w_uq=unpad_uq(gshard["mla_w_uq"])[None],
        mla_kv_norm_g=my_cols(dkvg_s, 1, KVL // 4), mla_w_ukv=gshard["mla_w_ukv"][None], mla_w_o=gshard["mla_w_o"][None],
        ffn_w_up=jnp.stack([gshard["ffn_w_up0"], gshard["ffn_w_up1"]], axis=0), ffn_conv_w=my_cols(dfcw_s, 2, DFF // 4),
        ffn_conv_b=dfcb_s, ffn_w_down=jnp.stack([gshard["ffn_w_down0"], gshard["ffn_w_down1"]], axis=0),
        final_norm_g=dfinal_s.reshape(D),
    )
    grads = {k: grads[k].reshape(W[k].shape) for k in ORDER}

    BIG = ("w_ada", "ab_w_in", "ab_w_out", "mla_w_in", "mla_w_uq", "mla_w_ukv", "mla_w_o", "ffn_w_up", "ffn_w_down")
    delta, new_m, new_v = {}, {}, {}
    for k in BIG:
        two = lambda a: a.reshape(-1, a.shape[-1])
        d_, m_, v_ = _adamw(two(W[k]), two(grads[k]), two(MOM[k]), two(VAR[k]), "adamw_" + k)
        delta[k], new_m[k], new_v[k] = (a.reshape(W[k].shape) for a in (d_, m_, v_))
    SMALL = [k for k in ORDER if k not in BIG]
    d_, m_, v_ = _adamw(_pack([W[k] for k in SMALL]), _pack([grads[k] for k in SMALL]), _pack([MOM[k] for k in SMALL]),
                        _pack([VAR[k] for k in SMALL]), "adamw_small")
    shapes = [W[k].shape for k in SMALL]
    for k, dd, mm, vv in zip(SMALL, _unpack(d_.reshape(-1), shapes), _unpack(m_.reshape(-1), shapes), _unpack(v_.reshape(-1), shapes)):
        delta[k], new_m[k], new_v[k] = dd, mm, vv

    return (loss, grad_x, *[grads[k] for k in ORDER], *[delta[k] for k in ORDER], *[new_m[k] for k in ORDER],
            *[new_v[k] for k in ORDER])
```

```python
import functools
import math

import jax
import jax.numpy as jnp
from jax import lax
from jax.experimental import pallas as pl
from jax.experimental.pallas import tpu as pltpu

F32 = jnp.float32
BF16 = jnp.bfloat16
MESH = pl.DeviceIdType.MESH
ANY = pl.BlockSpec(memory_space=pl.ANY)

EPS = 1e-6
N_MOD = 6
CHUNK = 128
HEAD = 128
ROPE = 64
QHEAD = 2 * HEAD
GRID_W = 64
ROPE_THETA = 10000.0
B_CONV = 31
FFN_CONV = 3
ADAM_LR, ADAM_B1, ADAM_B2, ADAM_EPS, ADAM_WD, ADAM_STEP = 0.001, 0.9, 0.999, 1e-08, 0.01, 10

V7X_VMEM_LIMIT = 56 * 1024 * 1024
LANES = 128
SUBLANES = 8
CONV_PAD = 16

BS = pl.BlockSpec


def _cp(sem=None, vmem=V7X_VMEM_LIMIT):
    return pltpu.CompilerParams(dimension_semantics=sem, vmem_limit_bytes=vmem)


class _Side:
    def __init__(self, bufs, nsem, start, finish):
        self.bufs, self.nsem, self.start, self.finish = list(bufs), nsem, start, finish


_carry_plan = {}


def _call(body, *, name, out_shape, grid, in_specs, out_specs, operands, sem, scratch=()):
    if name not in _carry_plan:
        return pl.pallas_call(body, name=name, out_shape=out_shape, grid=grid, in_specs=in_specs, out_specs=out_specs,
                              scratch_shapes=list(scratch), compiler_params=_cp(sem))(*operands)
    side, commit = _carry_plan[name]()
    multi = isinstance(out_shape, (tuple, list))
    outs = tuple(out_shape) if multi else (out_shape,)
    ospecs = tuple(out_specs) if multi else (out_specs,)
    n_in, n_out, n_scr, n_buf = len(in_specs), len(outs), len(scratch), len(side.bufs)

    def body2(*refs):
        o0 = n_in + n_buf
        s0 = o0 + n_out + n_buf
        bufs = refs[o0 + n_out:s0]
        sends, recvs = refs[s0 + n_scr:]
        first = last = None
        for d, g in enumerate(grid):
            pid = pl.program_id(d)
            first = (pid == 0) if first is None else jnp.logical_and(first, pid == 0)
            last = (pid == g - 1) if last is None else jnp.logical_and(last, pid == g - 1)

        @pl.when(first)
        def _():
            side.start(bufs, sends, recvs)

        body(*refs[:n_in], *refs[o0:o0 + n_out], *refs[s0:s0 + n_scr])

        @pl.when(last)
        def _():
            side.finish(bufs, sends, recvs)

    res = pl.pallas_call(
        body2, name=name, out_shape=outs + tuple(jax.ShapeDtypeStruct(b.shape, b.dtype) for b in side.bufs), grid=grid,
        in_specs=list(in_specs) + [ANY] * n_buf, out_specs=ospecs + (ANY,) * n_buf,
        scratch_shapes=list(scratch) + [pltpu.SemaphoreType.DMA((side.nsem,)), pltpu.SemaphoreType.DMA((side.nsem,))],
        input_output_aliases={n_in + i: n_out + i for i in range(n_buf)}, compiler_params=_cp(("arbitrary",) * len(grid)),
    )(*operands, *side.bufs)
    commit(list(res[n_out:]))
    return tuple(res[:n_out]) if multi else res[0]


def _pcs(body, *, name, out_shape, grid, in_specs, out_specs, compiler_params, scratch_shapes=()):
    def run(*operands):
        return _call(body, name=name, out_shape=out_shape, grid=grid, in_specs=in_specs, out_specs=out_specs, operands=operands,
                     sem=compiler_params.dimension_semantics, scratch=scratch_shapes)
    return run


def _pick(n, prefs):
    for p in prefs:
        if p <= n and n % p == 0:
            return p
    return n


def _row_tile(rows, cols, target_bytes=1 << 20):
    best = None
    for d in range(2 * SUBLANES, rows + 1, 2 * SUBLANES):
        if rows % d == 0 and d * cols * 4 <= target_bytes:
            best = d
    return best if best is not None else rows


def _sigmoid(x):
    return 1.0 / (1.0 + jnp.exp(-x))


def _gelu(x):
    c = math.sqrt(2.0 / math.pi)
    th = jnp.tanh(c * (x + 0.044715 * x * x * x))
    return 0.5 * x * (1.0 + th), th


def _gelu_grad(x, th):
    c = math.sqrt(2.0 / math.pi)
    return 0.5 * (1.0 + th) + 0.5 * x * (1.0 - th * th) * c * (1.0 + 3.0 * 0.044715 * x * x)


_DIMS = {"nn": (((1,), (0,)), ((), ())), "nt": (((1,), (1,)), ((), ())), "tn": (((0,), (0,)), ((), ()))}


def _mm(a, b, mode, out_dtype, name, tm=(512,), tn=(512,), tk=(100000,), bias=None, silu_a=False, rows=None, outer="j", b_lead=None,
        halves=False, res=None):
    bshape = b.shape[-2:]
    if mode == "nn":
        (M, K), N = a.shape, bshape[1]
    elif mode == "nt":
        (M, K), N = a.shape[-2:], bshape[0]
        K = 2 * K if halves else K
    else:
        (K, M), N = a.shape, bshape[1]
        N = 2 * N if halves else N
    if rows is not None:
        if mode == "tn":
            K = rows
        else:
            M = rows
    tm, tn, tk = _pick(M, tm), _pick(N, tn), _pick(K, tk)
    gm, gn, gk = M // tm, N // tn, K // tk

    def ij(g0, g1):
        return (g1, g0) if outer == "j" else (g0, g1)

    if mode == "nn":
        a_spec = BS((tm, tk), lambda g0, g1, k: (ij(g0, g1)[0], k))
        b_spec = BS((tk, tn), lambda g0, g1, k: (k, ij(g0, g1)[1]))
    elif mode == "nt":
        a_spec = BS((tm, tk), lambda g0, g1, k: (ij(g0, g1)[0], k))
        b_spec = BS((tn, tk), lambda g0, g1, k: (ij(g0, g1)[1], k))
    else:
        a_spec = BS((tk, tm), lambda g0, g1, k: (k, ij(g0, g1)[0]))
        b_spec = BS((tk, tn), lambda g0, g1, k: (k, ij(g0, g1)[1]))
    if halves and mode == "nt":
        per = K // 2 // tk
        a_spec = BS((None, tm, tk), lambda g0, g1, k: (k // per, ij(g0, g1)[0], k % per))
    if halves and mode == "tn":
        per = N // 2 // tn
        b_spec = BS((None, tk, tn), lambda g0, g1, k: (ij(g0, g1)[1] // per, k, ij(g0, g1)[1] % per))
    if b_lead is not None:
        blk, at = b_spec.block_shape, b_spec.index_map
        b_spec = BS((None,) + tuple(blk), lambda g0, g1, k: (b_lead,) + tuple(at(g0, g1, k)))
    in_specs = [a_spec, b_spec]
    operands = [a, b]
    if bias is not None:
        in_specs.append(BS((1, tn), lambda g0, g1, k: (0, ij(g0, g1)[1])))
        operands.append(bias.reshape(1, N))
    o_spec = BS((tm, tn), lambda g0, g1, k: ij(g0, g1))
    out_shape, out_specs = jax.ShapeDtypeStruct((M, N), out_dtype), o_spec
    if res is not None:
        x_res, gate, lat_rows = res
        G = gate.shape[0]
        in_specs += [o_spec, BS((G, 1, tn), lambda g0, g1, k: (0, 0, ij(g0, g1)[1]))]
        operands += [x_res, gate]
        out_shape, out_specs = (jax.ShapeDtypeStruct((M, N), F32), jax.ShapeDtypeStruct((M, N), BF16)), (o_spec, o_spec)
    n_in = len(in_specs)

    def body(*refs):
        a_ref, b_ref = refs[0], refs[1]
        bias_ref = refs[2] if bias is not None else None
        o_ref = refs[n_in]
        av = a_ref[...]
        if silu_a:
            av = av.astype(F32)
            av = av * _sigmoid(av)
        part = lax.dot_general(av.astype(BF16), b_ref[...].astype(BF16), _DIMS[mode], preferred_element_type=F32)
        i_blk = pl.program_id(1 if outer == "j" else 0)

        def finish(acc):
            if bias_ref is not None:
                acc = acc + bias_ref[...]
            if res is None:
                o_ref[...] = acc.astype(out_dtype)
                return
            x_ref, gate_ref, y_ref = refs[n_in - 2], refs[n_in - 1], refs[n_in + 1]
            gate_ = gate_ref[0]
            if G == 2:
                rows = i_blk * tm + lax.broadcasted_iota(jnp.int32, (tm, 1), 0)
                gate_ = jnp.where(rows < lat_rows, gate_ref[0], gate_ref[1])
            o_ref[...] = x_ref[...] + gate_ * acc
            y_ref[...] = acc.astype(BF16)

        if gk == 1:
            finish(part)
        else:
            acc_ref = refs[-1]
            k = pl.program_id(2)

            @pl.when(k == 0)
            def _():
                acc_ref[...] = part

            @pl.when(k > 0)
            def _():
                acc_ref[...] += part

            @pl.when(k == gk - 1)
            def _():
                finish(acc_ref[...])

    grid = (gn, gm, gk) if outer == "j" else (gm, gn, gk)
    return _call(body, name=name, out_shape=out_shape, grid=grid, in_specs=in_specs, out_specs=out_specs,
                 scratch=[pltpu.VMEM((tm, tn), F32)] if gk > 1 else [], sem=("parallel", "parallel", "arbitrary"),
                 operands=operands)


def _accum(ref, val, first):
    @pl.when(first)
    def _():
        ref[...] = val

    @pl.when(jnp.logical_not(first))
    def _():
        ref[...] += val


def _norm_mod_fwd(X, gain, sh, sc, tb, nlat, name):
    R, D = X.shape

    def body(x_ref, g_ref, sh_ref, sc_ref, o_ref):
        x = x_ref[...]
        r = lax.rsqrt(jnp.mean(x * x, axis=-1, keepdims=True) + EPS)
        o_ref[...] = ((x * r * g_ref[...]) * (1.0 + sc_ref[0]) + sh_ref[0]).astype(BF16)

    grp = BS((1, 1, D), lambda i: (i // nlat, 0, 0))
    return _pcs(
        body, name=name, out_shape=jax.ShapeDtypeStruct((R, D), BF16), grid=(R // tb,),
        in_specs=[BS((tb, D), lambda i: (i, 0)), BS((1, D), lambda i: (0, 0)), grp, grp],
        out_specs=BS((tb, D), lambda i: (i, 0)), compiler_params=_cp(("parallel",)),
    )(X, gain.reshape(1, D), sh, sc)


def _norm_mod_bwd(X, gain, sc, dh, dup, tb, nlat, name):
    R, D = X.shape
    G = sc.shape[0]
    n_up = dup.shape[0] // tb

    def body(x_ref, g_ref, sc_ref, dh_ref, dup_ref, dx_ref, dg_ref, dsh_ref, dsc_ref):
        i = pl.program_id(0)
        x = x_ref[...]
        g = g_ref[...]
        r = lax.rsqrt(jnp.mean(x * x, axis=-1, keepdims=True) + EPS)
        xh = x * r
        dh_ = dh_ref[...].astype(F32)
        t = dh_ * (1.0 + sc_ref[0])
        tg = t * g
        dup_ = dup_ref[...] if n_up == R // tb else jnp.where(i < n_up, dup_ref[...], 0.0)
        dx_ref[...] = dup_ + r * (tg - xh * jnp.mean(tg * xh, axis=-1, keepdims=True))
        _accum(dg_ref, jnp.sum(t * xh, axis=0, keepdims=True), i == 0)
        first = i % nlat == 0
        _accum(dsh_ref, jnp.sum(dh_, axis=0, keepdims=True)[None], first)
        _accum(dsc_ref, jnp.sum(dh_ * xh * g, axis=0, keepdims=True)[None], first)

    row = BS((tb, D), lambda i: (i, 0))
    grp = BS((1, 1, D), lambda i: (i // nlat, 0, 0))
    return _pcs(
        body, name=name,
        out_shape=(jax.ShapeDtypeStruct((R, D), F32), jax.ShapeDtypeStruct((1, D), F32),
                   jax.ShapeDtypeStruct((G, 1, D), F32), jax.ShapeDtypeStruct((G, 1, D), F32)),
        grid=(R // tb,), in_specs=[row, BS((1, D), lambda i: (0, 0)), grp, row, BS((tb, D), lambda i: (jnp.minimum(i, n_up - 1), 0))],
        out_specs=(row, BS((1, D), lambda i: (0, 0)), grp, grp), compiler_params=_cp(("arbitrary",)),
    )(X, gain.reshape(1, D), sc, dh, dup)


def _gate_bwd(dX, y, gate, tb, nlat, name):
    R, D = dX.shape
    G = gate.shape[0]

    def body(dx_ref, y_ref, g_ref, dy_ref, dg_ref):
        i = pl.program_id(0)
        dx = dx_ref[...]
        dy_ref[...] = (g_ref[0] * dx).astype(BF16)
        _accum(dg_ref, jnp.sum(dx * y_ref[...], axis=0, keepdims=True)[None], i % nlat == 0)

    row = BS((tb, D), lambda i: (i, 0))
    grp = BS((1, 1, D), lambda i: (i // nlat, 0, 0))
    return _pcs(
        body, name=name, out_shape=(jax.ShapeDtypeStruct((R, D), BF16), jax.ShapeDtypeStruct((G, 1, D), F32)),
        grid=(R // tb,), in_specs=[row, row, grp], out_specs=(row, grp), compiler_params=_cp(("arbitrary",)),
    )(dX, y, gate)


def _final_loss(X, gain, target, tb, name):
    R, D = X.shape

    def body(x_ref, g_ref, t_ref, loss_ref, dx_ref, dg_ref):
        i = pl.program_id(0)
        x = x_ref[...]
        g = g_ref[...]
        r = lax.rsqrt(jnp.mean(x * x, axis=-1, keepdims=True) + EPS)
        xh = x * r
        e = xh * g - t_ref[...]
        part = jnp.sum(jnp.sum(e * e, axis=1, keepdims=True), axis=0, keepdims=True) * (0.5 / D)
        _accum(loss_ref, jnp.broadcast_to(part, (1, LANES)), i == 0)
        dy = e * (1.0 / D)
        _accum(dg_ref, jnp.sum(dy * xh, axis=0, keepdims=True), i == 0)
        tg = dy * g
        dx_ref[...] = r * (tg - xh * jnp.mean(tg * xh, axis=-1, keepdims=True))

    row = BS((tb, D), lambda i: (i, 0))
    return _pcs(
        body, name=name,
        out_shape=(jax.ShapeDtypeStruct((1, LANES), F32), jax.ShapeDtypeStruct((R, D), F32), jax.ShapeDtypeStruct((1, D), F32)),
        grid=(R // tb,), in_specs=[row, BS((1, D), lambda i: (0, 0)), row],
        out_specs=(BS((1, LANES), lambda i: (0, 0)), row, BS((1, D), lambda i: (0, 0))), compiler_params=_cp(("arbitrary",)),
    )(X, gain.reshape(1, D), target)


def _ln_stats(v):
    mu = jnp.mean(v, axis=-1, keepdims=True)
    d = v - mu
    r = lax.rsqrt(jnp.mean(d * d, axis=-1, keepdims=True) + EPS)
    return d * r, r


def _gmlp_fwd(z, ln_g, ln_b, w_s, b_s_full, tb, name):
    R = z.shape[0]
    AW = ln_g.shape[-1]
    AH = w_s.shape[0]

    def body(zu_ref, zv_ref, g_ref, b_ref, ws_ref, bs_ref, o_ref):
        u, _ = _gelu(zu_ref[...].astype(F32))
        v, _ = _gelu(zv_ref[...].astype(F32))
        xh, _ = _ln_stats(v)
        vn = (xh * g_ref[...] + b_ref[...]).astype(BF16)
        for n in range(tb // CHUNK):
            rs = slice(n * CHUNK, (n + 1) * CHUNK)
            for h in range(AH):
                cs = slice(h * CHUNK, (h + 1) * CHUNK)
                v2 = jnp.dot(ws_ref[h].astype(BF16), vn[rs, cs], preferred_element_type=F32) + bs_ref[h]
                o_ref[rs, cs] = (u[rs, cs] * v2).astype(BF16)

    full3 = lambda s: BS(s, lambda i: (0, 0, 0))
    return _pcs(
        body, name=name, out_shape=jax.ShapeDtypeStruct((R, AW), BF16), grid=(R // tb,),
        in_specs=[BS((tb, AW), lambda i: (i, 0)), BS((tb, AW), lambda i: (i, 1)), BS((1, AW), lambda i: (0, 0)),
                  BS((1, AW), lambda i: (0, 0)), full3((AH, CHUNK, CHUNK)), full3((AH, CHUNK, CHUNK))],
        out_specs=BS((tb, AW), lambda i: (i, 0)), compiler_params=_cp(("parallel",)),
    )(z, z, ln_g.reshape(1, AW), ln_b.reshape(1, AW), w_s, b_s_full)


def _gmlp_bwd(z, dya, ln_g, ln_b, w_s, b_s_full, tb, name):
    R = z.shape[0]
    AW = ln_g.shape[-1]
    AH = w_s.shape[0]

    def body(zu_ref, zv_ref, dy_ref, g_ref, b_ref, ws_ref, bs_ref, dzu_ref, dzv_ref, dg_ref, db_ref, dws_ref, dbs_ref, cs_u_ref, cs_v_ref,
             dvn_scr):
        i = pl.program_id(0)
        first = i == 0
        zu = zu_ref[...].astype(F32)
        zv = zv_ref[...].astype(F32)
        u, thu = _gelu(zu)
        v, thv = _gelu(zv)
        xh, r = _ln_stats(v)
        g = g_ref[...]
        vn = (xh * g + b_ref[...]).astype(BF16)
        dy = dy_ref[...].astype(F32)

        @pl.when(first)
        def _():
            dws_ref[...] = jnp.zeros_like(dws_ref)
            dbs_ref[...] = jnp.zeros_like(dbs_ref)

        for n in range(tb // CHUNK):
            rs = slice(n * CHUNK, (n + 1) * CHUNK)
            for h in range(AH):
                cs = slice(h * CHUNK, (h + 1) * CHUNK)
                w = ws_ref[h].astype(BF16)
                v2 = jnp.dot(w, vn[rs, cs], preferred_element_type=F32) + bs_ref[h]
                dzu_ref[rs, cs] = (dy[rs, cs] * v2 * _gelu_grad(zu[rs, cs], thu[rs, cs])).astype(BF16)
                dv2 = dy[rs, cs] * u[rs, cs]
                dv2b = dv2.astype(BF16)
                dvn_scr[rs, cs] = lax.dot_general(w, dv2b, _DIMS["tn"], preferred_element_type=F32)
                dws_ref[h] += lax.dot_general(dv2b, vn[rs, cs], _DIMS["nt"], preferred_element_type=F32)
                dbs_ref[h] += jnp.sum(dv2, axis=1, keepdims=True)
        dvn = dvn_scr[...]
        _accum(dg_ref, jnp.sum(dvn * xh, axis=0, keepdims=True), first)
        _accum(db_ref, jnp.sum(dvn, axis=0, keepdims=True), first)
        t = dvn * g
        dv = r * (t - jnp.mean(t, axis=-1, keepdims=True) - xh * jnp.mean(t * xh, axis=-1, keepdims=True))
        dzv = dv * _gelu_grad(zv, thv)
        dzv_ref[...] = dzv.astype(BF16)
        _accum(cs_v_ref, jnp.sum(dzv, axis=0, keepdims=True), first)
        _accum(cs_u_ref, jnp.sum(dzu_ref[...].astype(F32), axis=0, keepdims=True), first)

    full3 = lambda s: BS(s, lambda i: (0, 0, 0))
    vec = BS((1, AW), lambda i: (0, 0))
    row = BS((tb, AW), lambda i: (i, 0))
    outs = _pcs(
        body, name=name,
        out_shape=(jax.ShapeDtypeStruct((R, AW), BF16), jax.ShapeDtypeStruct((R, AW), BF16), jax.ShapeDtypeStruct((1, AW), F32),
                   jax.ShapeDtypeStruct((1, AW), F32), jax.ShapeDtypeStruct((AH, CHUNK, CHUNK), F32),
                   jax.ShapeDtypeStruct((AH, CHUNK, 1), F32), jax.ShapeDtypeStruct((1, AW), F32), jax.ShapeDtypeStruct((1, AW), F32)),
        grid=(R // tb,),
        in_specs=[row, BS((tb, AW), lambda i: (i, 1)), row, vec, vec, full3((AH, CHUNK, CHUNK)), full3((AH, CHUNK, CHUNK))],
        out_specs=(row, row, vec, vec, full3((AH, CHUNK, CHUNK)), full3((AH, CHUNK, 1)), vec, vec),
        scratch_shapes=[pltpu.VMEM((tb, AW), F32)], compiler_params=_cp(("arbitrary",)),
    )(z, z, dya, ln_g.reshape(1, AW), ln_b.reshape(1, AW), w_s, b_s_full)
    return outs


def _segments(R, L):
    return [(0, L)] + ([(L, R - L)] if R > L else [])


def _scr_rows(R, L):
    return R + CONV_PAD * (len(_segments(R, L)) + 1)


def _scr_off(s, start):
    return CONV_PAD * (s + 1) + start


def _zero_pads(scr, R, L):
    segs = _segments(R, L)
    z = jnp.zeros((CONV_PAD, scr.shape[1]), F32)
    for s, (start, n) in enumerate(segs):
        scr[pl.ds(_scr_off(s, start) - CONV_PAD, CONV_PAD), :] = z
    last_s, (last_start, last_n) = len(segs) - 1, segs[-1]
    scr[pl.ds(_scr_off(last_s, last_start) + last_n, CONV_PAD), :] = z


def _for_chunks(R, L, ch, fn):
    for s, (start, n) in enumerate(_segments(R, L)):
        c = min(ch, n)
        off = _scr_off(s, start)

        def step(i, carry, start=start, off=off, c=c):
            r0 = pl.multiple_of(start + i * c, SUBLANES)
            fn(r0, pl.multiple_of(off + i * c, SUBLANES), c)
            return carry

        lax.fori_loop(0, n // c, step, 0)


def _taps(scr, srow, c, w_ref, ntap, flip):
    win = scr[pl.ds(srow - CONV_PAD, c + 2 * CONV_PAD), :]
    n = c + 2 * CONV_PAD
    acc = None
    for k in range(ntap):
        o = k - (ntap - 1) // 2
        if flip:
            o = -o
        sh = win if o == 0 else pltpu.roll(win, (-o) % n, 0)
        term = w_ref[k:k + 1, :] * sh[CONV_PAD:CONV_PAD + c]
        acc = term if acc is None else acc + term
    return acc


def _tap_grads(scr, srow, c, dy, dw_ref, ntap):
    win = scr[pl.ds(srow - CONV_PAD, c + 2 * CONV_PAD), :]
    n = c + 2 * CONV_PAD
    for k in range(ntap):
        o = k - (ntap - 1) // 2
        sh = win if o == 0 else pltpu.roll(win, (-o) % n, 0)
        dw_ref[k:k + 1, :] += jnp.sum(dy * sh[CONV_PAD:CONV_PAD + c], axis=0, keepdims=True)


def _glu_conv_fwd(z, col0, conv_w, conv_b, L, ch, name):
    R = z.shape[0]
    BW = conv_w.shape[1]
    nb, c0 = BW // LANES, col0 // LANES

    def body(a_ref, g_ref, w_ref, b_ref, o_ref, scr):
        _zero_pads(scr, R, L)

        def fill(r0, s0, c):
            a = a_ref[pl.ds(r0, c), :].astype(F32)
            g = g_ref[pl.ds(r0, c), :].astype(F32)
            scr[pl.ds(s0, c), :] = a * _sigmoid(g)

        _for_chunks(R, L, ch, fill)

        def conv(r0, s0, c):
            o_ref[pl.ds(r0, c), :] = _taps(scr, s0, c, w_ref, B_CONV, False) + b_ref[...]

        _for_chunks(R, L, ch, conv)

    return _pcs(
        body, name=name, out_shape=jax.ShapeDtypeStruct((R, BW), F32), grid=(nb,),
        in_specs=[BS((R, LANES), lambda j: (0, c0 + j)), BS((R, LANES), lambda j: (0, c0 + nb + j)),
                  BS((B_CONV, LANES), lambda j: (0, j)), BS((1, LANES), lambda j: (0, j))],
        out_specs=BS((R, LANES), lambda j: (0, j)), scratch_shapes=[pltpu.VMEM((_scr_rows(R, L), LANES), F32)],
        compiler_params=_cp(("parallel",)),
    )(z, z, conv_w, conv_b.reshape(1, BW))


def _glu_conv_bwd(z, col0, dhc, conv_w, L, ch, name):
    R = z.shape[0]
    BW = conv_w.shape[1]
    nb, c0 = BW // LANES, col0 // LANES

    def body(a_ref, g_ref, dy_ref, w_ref, da_ref, dg_ref, dw_ref, db_ref, csa_ref, csg_ref, scr_h, scr_dy):
        _zero_pads(scr_h, R, L)
        _zero_pads(scr_dy, R, L)
        dw_ref[...] = jnp.zeros_like(dw_ref)
        db_ref[...] = jnp.zeros_like(db_ref)
        csa_ref[...] = jnp.zeros_like(csa_ref)
        csg_ref[...] = jnp.zeros_like(csg_ref)

        def fill(r0, s0, c):
            a = a_ref[pl.ds(r0, c), :].astype(F32)
            g = g_ref[pl.ds(r0, c), :].astype(F32)
            scr_h[pl.ds(s0, c), :] = a * _sigmoid(g)
            scr_dy[pl.ds(s0, c), :] = dy_ref[pl.ds(r0, c), :]

        _for_chunks(R, L, ch, fill)

        def back(r0, s0, c):
            dh = _taps(scr_dy, s0, c, w_ref, B_CONV, True)
            a = a_ref[pl.ds(r0, c), :].astype(F32)
            sg = _sigmoid(g_ref[pl.ds(r0, c), :].astype(F32))
            da = dh * sg
            dg = dh * a * sg * (1.0 - sg)
            da_ref[pl.ds(r0, c), :] = da.astype(BF16)
            dg_ref[pl.ds(r0, c), :] = dg.astype(BF16)
            csa_ref[...] += jnp.sum(da, axis=0, keepdims=True)
            csg_ref[...] += jnp.sum(dg, axis=0, keepdims=True)
            dy = dy_ref[pl.ds(r0, c), :]
            db_ref[...] += jnp.sum(dy, axis=0, keepdims=True)
            _tap_grads(scr_h, s0, c, dy, dw_ref, B_CONV)

        _for_chunks(R, L, ch, back)

    col = BS((R, LANES), lambda j: (0, j))
    vec = BS((1, LANES), lambda j: (0, j))
    nrow = _scr_rows(R, L)
    return _pcs(
        body, name=name,
        out_shape=(jax.ShapeDtypeStruct((R, BW), BF16), jax.ShapeDtypeStruct((R, BW), BF16), jax.ShapeDtypeStruct((B_CONV, BW), F32),
                   jax.ShapeDtypeStruct((1, BW), F32), jax.ShapeDtypeStruct((1, BW), F32), jax.ShapeDtypeStruct((1, BW), F32)),
        grid=(nb,),
        in_specs=[BS((R, LANES), lambda j: (0, c0 + j)), BS((R, LANES), lambda j: (0, c0 + nb + j)), col,
                  BS((B_CONV, LANES), lambda j: (0, j))],
        out_specs=(col, col, BS((B_CONV, LANES), lambda j: (0, j)), vec, vec, vec),
        scratch_shapes=[pltpu.VMEM((nrow, LANES), F32), pltpu.VMEM((nrow, LANES), F32)], compiler_params=_cp(("parallel",)),
    )(z, z, dhc, conv_w)


def _ln_silu_fwd(hc, ln_g, ln_b, tb, name):
    R, W = hc.shape

    def body(x_ref, g_ref, b_ref, o_ref):
        xh, _ = _ln_stats(x_ref[...])
        y = xh * g_ref[...] + b_ref[...]
        o_ref[...] = (y * _sigmoid(y)).astype(BF16)

    vec = BS((1, W), lambda i: (0, 0))
    row = BS((tb, W), lambda i: (i, 0))
    return _pcs(
        body, name=name, out_shape=jax.ShapeDtypeStruct((R, W), BF16), grid=(R // tb,), in_specs=[row, vec, vec], out_specs=row,
        compiler_params=_cp(("parallel",)),
    )(hc, ln_g.reshape(1, W), ln_b.reshape(1, W))


def _ln_silu_bwd(hc, dyb, col0, ln_g, ln_b, tb, name):
    R, W = hc.shape
    c0 = col0 // W

    def body(x_ref, dy_ref, g_ref, b_ref, dx_ref, dg_ref, db_ref):
        i = pl.program_id(0)
        xh, r = _ln_stats(x_ref[...])
        g = g_ref[...]
        y = xh * g + b_ref[...]
        s = _sigmoid(y)
        dy = dy_ref[...].astype(F32) * s * (1.0 + y * (1.0 - s))
        _accum(dg_ref, jnp.sum(dy * xh, axis=0, keepdims=True), i == 0)
        _accum(db_ref, jnp.sum(dy, axis=0, keepdims=True), i == 0)
        t = dy * g
        dx_ref[...] = r * (t - jnp.mean(t, axis=-1, keepdims=True) - xh * jnp.mean(t * xh, axis=-1, keepdims=True))

    vec = BS((1, W), lambda i: (0, 0))
    row = BS((tb, W), lambda i: (i, 0))
    return _pcs(
        body, name=name,
        out_shape=(jax.ShapeDtypeStruct((R, W), F32), jax.ShapeDtypeStruct((1, W), F32), jax.ShapeDtypeStruct((1, W), F32)),
        grid=(R // tb,), in_specs=[row, BS((tb, W), lambda i: (i, c0)), vec, vec], out_specs=(row, vec, vec),
        compiler_params=_cp(("arbitrary",)),
    )(hc, dyb, ln_g.reshape(1, W), ln_b.reshape(1, W))


def _ffn_act_fwd(zf, conv_w, conv_b, L, ch, name):
    R = zf.shape[0]
    DFF = conv_w.shape[1]
    nb = DFF // LANES

    def body(g_ref, u_ref, w_ref, b_ref, o_ref, scr):
        _zero_pads(scr, R, L)

        def fill(r0, s0, c):
            scr[pl.ds(s0, c), :] = g_ref[pl.ds(r0, c), :].astype(F32)

        _for_chunks(R, L, ch, fill)

        def act(r0, s0, c):
            gc = _taps(scr, s0, c, w_ref, FFN_CONV, False) + b_ref[...]
            o_ref[pl.ds(r0, c), :] = (gc * _sigmoid(gc) * u_ref[pl.ds(r0, c), :].astype(F32)).astype(BF16)

        _for_chunks(R, L, ch, act)

    return _pcs(
        body, name=name, out_shape=jax.ShapeDtypeStruct((R, DFF), BF16), grid=(nb,),
        in_specs=[BS((R, LANES), lambda j: (0, j)), BS((R, LANES), lambda j: (0, nb + j)), BS((FFN_CONV, LANES), lambda j: (0, j)),
                  BS((1, LANES), lambda j: (0, j))],
        out_specs=BS((R, LANES), lambda j: (0, j)), scratch_shapes=[pltpu.VMEM((_scr_rows(R, L), LANES), F32)],
        compiler_params=_cp(("parallel",)),
    )(zf, zf, conv_w, conv_b.reshape(1, DFF))


def _ffn_act_bwd(zf, df, conv_w, conv_b, L, ch, name):
    R = zf.shape[0]
    DFF = conv_w.shape[1]
    nb = DFF // LANES

    def body(g_ref, u_ref, df_ref, w_ref, b_ref, dz_ref, dw_ref, db_ref, scr_g, scr_d):
        _zero_pads(scr_g, R, L)
        _zero_pads(scr_d, R, L)
        dw_ref[...] = jnp.zeros_like(dw_ref)
        db_ref[...] = jnp.zeros_like(db_ref)

        def fill(r0, s0, c):
            scr_g[pl.ds(s0, c), :] = g_ref[pl.ds(r0, c), :].astype(F32)

        _for_chunks(R, L, ch, fill)

        def pre(r0, s0, c):
            gc = _taps(scr_g, s0, c, w_ref, FFN_CONV, False) + b_ref[...]
            s = _sigmoid(gc)
            d = df_ref[pl.ds(r0, c), :].astype(F32)
            dz_ref[1, pl.ds(r0, c), :] = (d * gc * s).astype(BF16)
            dgc = d * u_ref[pl.ds(r0, c), :].astype(F32) * s * (1.0 + gc * (1.0 - s))
            scr_d[pl.ds(s0, c), :] = dgc
            db_ref[...] += jnp.sum(dgc, axis=0, keepdims=True)
            _tap_grads(scr_g, s0, c, dgc, dw_ref, FFN_CONV)

        _for_chunks(R, L, ch, pre)

        def back(r0, s0, c):
            dz_ref[0, pl.ds(r0, c), :] = _taps(scr_d, s0, c, w_ref, FFN_CONV, True).astype(BF16)

        _for_chunks(R, L, ch, back)

    col = BS((R, LANES), lambda j: (0, j))
    vec = BS((1, LANES), lambda j: (0, j))
    nrow = _scr_rows(R, L)
    return _pcs(
        body, name=name,
        out_shape=(jax.ShapeDtypeStruct((2, R, DFF), BF16), jax.ShapeDtypeStruct((FFN_CONV, DFF), F32), jax.ShapeDtypeStruct((1, DFF), F32)),
        grid=(nb,),
        in_specs=[col, BS((R, LANES), lambda j: (0, nb + j)), col, BS((FFN_CONV, LANES), lambda j: (0, j)), vec],
        out_specs=(BS((2, R, LANES), lambda j: (0, 0, j)), BS((FFN_CONV, LANES), lambda j: (0, j)), vec),
        scratch_shapes=[pltpu.VMEM((nrow, LANES), F32), pltpu.VMEM((nrow, LANES), F32)], compiler_params=_cp(("parallel",)),
    )(zf, zf, df, conv_w, conv_b.reshape(1, DFF))


def _rope_tables(L, T):
    rows = L // GRID_W
    row = jnp.repeat(jnp.arange(rows, dtype=F32), GRID_W)
    col = jnp.tile(jnp.arange(GRID_W, dtype=F32), rows)
    n_freq = ROPE // 4
    inv = ROPE_THETA ** (-jnp.arange(n_freq, dtype=F32) / n_freq)
    ang = jnp.concatenate([row[:, None] * inv, col[:, None] * inv], axis=-1)
    cos, sin = jnp.cos(ang), jnp.sin(ang)
    half = ROPE // 2
    zero = jnp.zeros((L, half), F32)
    cos_t = jnp.concatenate([cos, cos, jnp.ones((L, LANES - ROPE), F32)], axis=1)
    sa = jnp.concatenate([zero, sin, zero, zero], axis=1)
    sb = jnp.concatenate([-sin, zero, zero, zero], axis=1)
    pad = T - L
    cos_t = jnp.concatenate([cos_t, jnp.ones((pad, LANES), F32)], axis=0)
    sa = jnp.concatenate([sa, jnp.zeros((pad, LANES), F32)], axis=0)
    sb = jnp.concatenate([sb, jnp.zeros((pad, LANES), F32)], axis=0)
    return cos_t, sa, sb


def _rope(x, cos, sa, sb):
    half = ROPE // 2
    return x * cos + pltpu.roll(x, half, 1) * sa + pltpu.roll(x, LANES - half, 1) * sb


def _rope_t(d, cos, sa, sb):
    half = ROPE // 2
    return d * cos + pltpu.roll(d * sa, LANES - half, 1) + pltpu.roll(d * sb, half, 1)


def _mla_prep_fwd(zm, qg, kvg, tabs, tb, name):
    T, W = zm.shape
    QL, KVL = qg.shape[-1], kvg.shape[-1]

    def body(z_ref, qg_ref, kg_ref, cos_ref, sa_ref, sb_ref, q_ref, k_ref, p_ref):
        cq = z_ref[:, :QL]
        r = lax.rsqrt(jnp.mean(cq * cq, axis=-1, keepdims=True) + EPS)
        q_ref[...] = (cq * r * qg_ref[...]).astype(BF16)
        ck = z_ref[:, QL:QL + KVL]
        r = lax.rsqrt(jnp.mean(ck * ck, axis=-1, keepdims=True) + EPS)
        k_ref[...] = (ck * r * kg_ref[...]).astype(BF16)
        p_ref[...] = _rope(z_ref[:, QL + KVL:], cos_ref[...], sa_ref[...], sb_ref[...]).astype(BF16)

    tab = BS((tb, LANES), lambda i: (i, 0))
    return _pcs(
        body, name=name,
        out_shape=(jax.ShapeDtypeStruct((T, QL), BF16), jax.ShapeDtypeStruct((T, KVL), BF16), jax.ShapeDtypeStruct((T, LANES), BF16)),
        grid=(T // tb,),
        in_specs=[BS((tb, W), lambda i: (i, 0)), BS((1, QL), lambda i: (0, 0)), BS((1, KVL), lambda i: (0, 0)), tab, tab, tab],
        out_specs=(BS((tb, QL), lambda i: (i, 0)), BS((tb, KVL), lambda i: (i, 0)), tab), compiler_params=_cp(("parallel",)),
    )(zm, qg.reshape(1, QL), kvg.reshape(1, KVL), *tabs)


def _mla_prep_bwd(zm, qg, kvg, tabs, dq, dk, dp, tb, name):
    T, W = zm.shape
    QL, KVL = qg.shape[-1], kvg.shape[-1]

    def body(z_ref, qg_ref, kg_ref, cos_ref, sa_ref, sb_ref, dq_ref, dk_ref, dp_ref, dz_ref, dqg_ref, dkg_ref):
        i = pl.program_id(0)

        def rms_bwd(x, g, dy):
            r = lax.rsqrt(jnp.mean(x * x, axis=-1, keepdims=True) + EPS)
            xh = x * r
            t = dy * g
            return r * (t - xh * jnp.mean(t * xh, axis=-1, keepdims=True)), jnp.sum(dy * xh, axis=0, keepdims=True)

        dcq, dg = rms_bwd(z_ref[:, :QL], qg_ref[...], dq_ref[...].astype(F32))
        dz_ref[:, :QL] = dcq.astype(BF16)
        _accum(dqg_ref, dg, i == 0)
        dck, dg = rms_bwd(z_ref[:, QL:QL + KVL], kg_ref[...], dk_ref[...].astype(F32))
        dz_ref[:, QL:QL + KVL] = dck.astype(BF16)
        _accum(dkg_ref, dg, i == 0)
        dz_ref[:, QL + KVL:] = _rope_t(dp_ref[...], cos_ref[...], sa_ref[...], sb_ref[...]).astype(BF16)

    tab = BS((tb, LANES), lambda i: (i, 0))
    return _pcs(
        body, name=name,
        out_shape=(jax.ShapeDtypeStruct((T, W), BF16), jax.ShapeDtypeStruct((1, QL), F32), jax.ShapeDtypeStruct((1, KVL), F32)),
        grid=(T // tb,),
        in_specs=[BS((tb, W), lambda i: (i, 0)), BS((1, QL), lambda i: (0, 0)), BS((1, KVL), lambda i: (0, 0)), tab, tab, tab,
                  BS((tb, QL), lambda i: (i, 0)), BS((tb, KVL), lambda i: (i, 0)), tab],
        out_specs=(BS((tb, W), lambda i: (i, 0)), BS((1, QL), lambda i: (0, 0)), BS((1, KVL), lambda i: (0, 0))),
        compiler_params=_cp(("arbitrary",)),
    )(zm, qg.reshape(1, QL), kvg.reshape(1, KVL), *tabs, dq, dk, dp)


def _attn_fwd(q, kv, kpe, tabs, L, tq, name):
    T = kv.shape[0]
    H = kv.shape[1] // QHEAD
    scale = (HEAD + ROPE) ** -0.5

    def body(q_ref, kv_ref, kpe_ref, cos_ref, sa_ref, sb_ref, o_ref, lse_ref, kcat):
        @pl.when(pl.program_id(1) == 0)
        def _():
            kcat[:, :HEAD] = kv_ref[:, :HEAD]
            kcat[:, HEAD:] = kpe_ref[...]

        sub = tq // 2
        for r in range(2):
            rs = slice(r * sub, (r + 1) * sub)
            qp = _rope(q_ref[rs, HEAD:].astype(F32), cos_ref[rs, :], sa_ref[rs, :], sb_ref[rs, :]).astype(BF16)
            qc = jnp.concatenate([q_ref[rs, :HEAD], qp], axis=1)
            s = lax.dot_general(qc, kcat[...], _DIMS["nt"], preferred_element_type=F32)
            m = jnp.max(s, axis=-1, keepdims=True)
            p = jnp.exp2((s - m) * (scale * math.log2(math.e)))
            l = jnp.sum(p, axis=-1, keepdims=True)
            o = jnp.dot(p.astype(BF16), kv_ref[:, HEAD:], preferred_element_type=F32)
            o_ref[rs, :] = (o / l).astype(BF16)
            lse_ref[0, rs, :] = m * scale + jnp.log(l)

    tab = BS((tq, LANES), lambda h, i: (i, 0))
    return _call(
        body, name=name, out_shape=(jax.ShapeDtypeStruct((L, H * HEAD), BF16), jax.ShapeDtypeStruct((H, L, 1), F32)),
        grid=(H, L // tq),
        in_specs=[BS((tq, QHEAD), lambda h, i: (i, h)), BS((T, QHEAD), lambda h, i: (0, h)), BS((T, LANES), lambda h, i: (0, 0)),
                  tab, tab, tab],
        out_specs=(BS((tq, HEAD), lambda h, i: (i, h)), BS((1, tq, 1), lambda h, i: (h, i, 0))),
        scratch=[pltpu.VMEM((T, QHEAD), BF16)], sem=("parallel", "arbitrary"), operands=(q, kv, kpe, *tabs))


def _attn_bwd(q, kv, kpe, tabs, o, lse, do, L, tq, name):
    T = kv.shape[0]
    H = kv.shape[1] // QHEAD
    scale = (HEAD + ROPE) ** -0.5
    nq = L // tq

    def body(q_ref, kv_ref, kpe_ref, cos_ref, sa_ref, sb_ref, o_ref, lse_ref, do_ref, dq_ref, dkv_ref, dkpe_ref, kcat, dk_acc, dv_acc,
             qc_scr, ds_scr, p_scr):
        h, i = pl.program_id(0), pl.program_id(1)

        @pl.when(i == 0)
        def _():
            kcat[:, :HEAD] = kv_ref[:, :HEAD]
            kcat[:, HEAD:] = kpe_ref[...]
            dk_acc[...] = jnp.zeros_like(dk_acc)
            dv_acc[...] = jnp.zeros_like(dv_acc)

        sub = tq // 2
        log2e = math.log2(math.e)
        for r in range(2):
            rs = slice(r * sub, (r + 1) * sub)
            cos, sa, sb = cos_ref[rs, :], sa_ref[rs, :], sb_ref[rs, :]
            qp = _rope(q_ref[rs, HEAD:].astype(F32), cos, sa, sb).astype(BF16)
            qc_scr[rs, :] = jnp.concatenate([q_ref[rs, :HEAD], qp], axis=1)
            s = lax.dot_general(qc_scr[rs, :], kcat[...], _DIMS["nt"], preferred_element_type=F32)
            p = jnp.exp2(s * (scale * log2e) - lse_ref[0, rs, :] * log2e)
            dov = do_ref[rs, :]
            delta = jnp.sum(dov.astype(F32) * o_ref[rs, :].astype(F32), axis=-1, keepdims=True)
            dp = lax.dot_general(dov, kv_ref[:, HEAD:], _DIMS["nt"], preferred_element_type=F32)
            ds_scr[rs, :] = (p * (dp - delta) * scale).astype(BF16)
            p_scr[rs, :] = p.astype(BF16)
            dqc = jnp.dot(ds_scr[rs, :], kcat[...], preferred_element_type=F32)
            dq_ref[rs, :HEAD] = dqc[:, :HEAD].astype(BF16)
            dq_ref[rs, HEAD:] = _rope_t(dqc[:, HEAD:], cos, sa, sb).astype(BF16)
        dk_acc[...] += lax.dot_general(ds_scr[...], qc_scr[...], _DIMS["tn"], preferred_element_type=F32)
        dv_acc[...] += lax.dot_general(p_scr[...], do_ref[...], _DIMS["tn"], preferred_element_type=F32)

        @pl.when(i == nq - 1)
        def _():
            dkv_ref[:, :HEAD] = dk_acc[:, :HEAD].astype(BF16)
            dkv_ref[:, HEAD:] = dv_acc[...].astype(BF16)

            @pl.when(h == 0)
            def _():
                dkpe_ref[...] = dk_acc[:, HEAD:]

            @pl.when(h > 0)
            def _():
                dkpe_ref[...] += dk_acc[:, HEAD:]

    tab = BS((tq, LANES), lambda h, i: (i, 0))
    return _call(
        body, name=name,
        out_shape=(jax.ShapeDtypeStruct((L, H * QHEAD), BF16), jax.ShapeDtypeStruct((T, H * QHEAD), BF16), jax.ShapeDtypeStruct((T, LANES), F32)),
        grid=(H, nq),
        in_specs=[BS((tq, QHEAD), lambda h, i: (i, h)), BS((T, QHEAD), lambda h, i: (0, h)), BS((T, LANES), lambda h, i: (0, 0)),
                  tab, tab, tab, BS((tq, HEAD), lambda h, i: (i, h)), BS((1, tq, 1), lambda h, i: (h, i, 0)),
                  BS((tq, HEAD), lambda h, i: (i, h))],
        out_specs=(BS((tq, QHEAD), lambda h, i: (i, h)), BS((T, QHEAD), lambda h, i: (0, h)), BS((T, LANES), lambda h, i: (0, 0))),
        scratch=[pltpu.VMEM((T, QHEAD), BF16), pltpu.VMEM((T, QHEAD), F32), pltpu.VMEM((T, HEAD), F32),
                 pltpu.VMEM((tq, QHEAD), BF16), pltpu.VMEM((tq, T), BF16), pltpu.VMEM((tq, T), BF16)],
        sem=("arbitrary", "arbitrary"), operands=(q, kv, kpe, *tabs, o, lse, do))


def _adamw(w, g, m, v, name):
    R, C = w.shape
    tr = _row_tile(R, C)
    c1 = 1.0 / (1.0 - ADAM_B1 ** ADAM_STEP)
    c2 = 1.0 / (1.0 - ADAM_B2 ** ADAM_STEP)

    def body(w_ref, g_ref, m_ref, v_ref, d_ref, nm_ref, nv_ref):
        g_ = g_ref[...]
        nm = ADAM_B1 * m_ref[...] + (1.0 - ADAM_B1) * g_
        nv = ADAM_B2 * v_ref[...] + (1.0 - ADAM_B2) * (g_ * g_)
        nm_ref[...] = nm
        nv_ref[...] = nv
        d_ref[...] = -ADAM_LR * ((nm * c1) / (jnp.sqrt(nv * c2) + ADAM_EPS) + ADAM_WD * w_ref[...])

    blk = BS((tr, C), lambda i: (i, 0))
    sd = jax.ShapeDtypeStruct((R, C), F32)
    return _pcs(body, name=name, out_shape=(sd, sd, sd), grid=(R // tr,), in_specs=[blk] * 4, out_specs=(blk,) * 3,
                compiler_params=_cp(("parallel",)))(w, g, m, v)


def _adamw_layers(w, gs, m, v, name):
    n, R, C = w.shape
    tr = _row_tile(R, C)
    nr = R // tr
    c1 = 1.0 / (1.0 - ADAM_B1 ** ADAM_STEP)
    c2 = 1.0 / (1.0 - ADAM_B2 ** ADAM_STEP)

    def body(w_ref, m_ref, v_ref, *rest):
        g_refs, (go_ref, d_ref, nm_ref, nv_ref) = rest[:n], rest[n:]
        layer = pl.program_id(0)
        g_ = g_refs[0][...]
        for j in range(1, n):
            g_ = jnp.where(layer == j, g_refs[j][...], g_)
        nm = ADAM_B1 * m_ref[...] + (1.0 - ADAM_B1) * g_
        nv = ADAM_B2 * v_ref[...] + (1.0 - ADAM_B2) * (g_ * g_)
        go_ref[...] = g_
        nm_ref[...] = nm
        nv_ref[...] = nv
        d_ref[...] = -ADAM_LR * ((nm * c1) / (jnp.sqrt(nv * c2) + ADAM_EPS) + ADAM_WD * w_ref[...])

    def g_spec(j):
        return BS((tr, C), lambda layer, i: (jnp.where(layer == j, i, jnp.where(layer < j, 0, nr - 1)), 0))

    blk = BS((None, tr, C), lambda layer, i: (layer, i, 0))
    sd = jax.ShapeDtypeStruct((n, R, C), F32)
    return _pcs(body, name=name, out_shape=(sd, sd, sd, sd), grid=(n, nr), in_specs=[blk] * 3 + [g_spec(j) for j in range(n)],
                out_specs=(blk,) * 4, compiler_params=_cp(("arbitrary", "arbitrary")))(w, m, v, *gs)


def _sum_lead(a, out_dtype, name):
    n, R, C = a.shape
    tr = _row_tile(R, C * n, 2 << 20)

    def body(a_ref, o_ref):
        acc = a_ref[0].astype(F32)
        for k in range(1, n):
            acc = acc + a_ref[k].astype(F32)
        o_ref[...] = acc.astype(out_dtype)

    return pl.pallas_call(body, name=name, out_shape=jax.ShapeDtypeStruct((R, C), out_dtype), grid=(R // tr,),
                          in_specs=[BS((n, tr, C), lambda i: (0, i, 0))], out_specs=BS((tr, C), lambda i: (i, 0)),
                          compiler_params=_cp(("parallel",)))(a)


def _cctx_grad(parts, c_ctx, name):
    n, D = parts.shape

    def body(p_ref, c_ref, o_ref):
        d = jnp.sum(p_ref[...], axis=0, keepdims=True)
        c = c_ref[...]
        s = _sigmoid(c)
        o_ref[...] = d * s * (1.0 + c * (1.0 - s))

    return pl.pallas_call(body, name=name, out_shape=jax.ShapeDtypeStruct((1, D), F32))(parts, c_ctx.reshape(1, D))


def _me():
    return lax.axis_index("x"), lax.axis_index("y"), lax.axis_index("c")


def _aligned(v, n):
    return v if isinstance(v, int) else pl.multiple_of(v, n)


def _window(ref, r0, c0, R, C):
    rows = pl.ds(_aligned(r0, SUBLANES), R)
    if C == ref.shape[1]:
        return ref.at[rows, :]
    return ref.at[rows, pl.ds(_aligned(c0, LANES), C)]


def _allgather8(items, name):
    n = len(items)

    def body(*refs):
        srcs, dsts = refs[:n], refs[n:2 * n]
        send_sems, recv_sems, local_sems = refs[2 * n:]
        x, y, c = _me()
        me, sibling = (x, y, c), (x, y, 1 - c)
        chips = [(1 - x, y), (x, 1 - y), (1 - x, 1 - y)]

        def dwin(a, blk):
            (R, C), at = items[a][2], items[a][4]
            return _window(dsts[a], *at(*blk), R, C)

        def swin(a):
            (R, C), at = items[a][2], items[a][1]
            return _window(srcs[a], *at(*me), R, C)

        def copy(a, k, blk, to, src=None):
            return pltpu.make_async_remote_copy(
                src_ref=dwin(a, blk) if src is None else src, dst_ref=dwin(a, blk), send_sem=send_sems.at[7 * a + k],
                recv_sem=recv_sems.at[7 * a + k], device_id=to, device_id_type=MESH)

        mine = [pltpu.make_async_copy(swin(a), dwin(a, me), local_sems.at[a]) for a in range(n)]
        for cp in mine:
            cp.start()
        first = []
        for a in range(n):
            first.append(copy(a, 0, me, sibling, src=swin(a)))
            first += [copy(a, 1 + j, me, (*chip, c), src=swin(a)) for j, chip in enumerate(chips)]
        for cp in first:
            cp.start()
        passed = []
        for j, chip in enumerate(chips):
            for a in range(n):
                copy(a, 1 + j, (*chip, c), me).wait_recv()
                fwd = copy(a, 4 + j, (*chip, c), sibling)
                fwd.start()
                passed.append(fwd)
        for a in range(n):
            copy(a, 0, sibling, me).wait_recv()
            for j, chip in enumerate(chips):
                copy(a, 4 + j, (*chip, 1 - c), me).wait_recv()
        for cp in first + passed:
            cp.wait_send()
        for cp in mine:
            cp.wait()

    outs = pl.pallas_call(
        body, name=name, out_shape=tuple(jax.ShapeDtypeStruct(it[3], it[0].dtype) for it in items),
        in_specs=[ANY] * n, out_specs=tuple([ANY] * n),
        scratch_shapes=[pltpu.SemaphoreType.DMA((7 * n,)), pltpu.SemaphoreType.DMA((7 * n,)), pltpu.SemaphoreType.DMA((n,))],
    )(*[it[0] for it in items])
    return list(outs)


def _cast_place(w, kind, name, lead=None):
    R2, C = w.shape[-2:]
    tr = _row_tile(R2, C)
    nr = R2 // tr
    xi, yi, _ = _me()
    p_arr = (2 * xi + yi).astype(jnp.int32).reshape(1)
    if kind == "col":
        shape, o_spec = (R2, 4 * C), BS((tr, C), lambda i, p: (i, p[0]))
    else:
        shape, o_spec = (4 * R2, C), BS((tr, C), lambda i, p: (p[0] * nr + i, 0))

    def body(p_ref, w_ref, o_ref):
        o_ref[...] = w_ref[...].astype(BF16)

    return pl.pallas_call(
        body, name=name, out_shape=jax.ShapeDtypeStruct(shape, BF16),
        grid_spec=pltpu.PrefetchScalarGridSpec(
            num_scalar_prefetch=1, grid=(nr,), out_specs=o_spec,
            in_specs=[BS((tr, C), lambda i, p: (i, 0)) if lead is None else BS((None, tr, C), lambda i, p: (lead, i, 0))]),
        compiler_params=_cp(("parallel",)),
    )(p_arr, w)


def _remote(src, dst, sends, recvs, k, to):
    return pltpu.make_async_remote_copy(src_ref=src, dst_ref=dst, send_sem=sends.at[k], recv_sem=recvs.at[k], device_id=to,
                                        device_id_type=MESH)


def _gather_side(bufs, pieces):
    def win(b, piece, p, pc):
        bi, (R, C), at, k, i = piece
        r0, c0 = at(p, pc)
        return _window(b[bi], r0 + i * (R // k), c0, R // k, C)

    def chips_of(x, y):
        return [(1 - x, y), (x, 1 - y), (1 - x, 1 - y)]

    def outgoing(b, sends, recvs):
        x, y, c = _me()
        cps = []
        for m, piece in enumerate(pieces):
            mine = win(b, piece, 2 * x + y, c)
            cps += [_remote(mine, mine, sends, recvs, 6 * m + j, (px, py, c)) for j, (px, py) in enumerate(chips_of(x, y))]
        return cps

    def start(b, sends, recvs):
        for cp in outgoing(b, sends, recvs):
            cp.start()

    def finish(b, sends, recvs):
        x, y, c = _me()
        sibling = (x, y, 1 - c)
        passed = []
        for j, (px, py) in enumerate(chips_of(x, y)):
            for m, piece in enumerate(pieces):
                landed = win(b, piece, 2 * px + py, c)
                _remote(landed, landed, sends, recvs, 6 * m + j, (px, py, c)).wait_recv()
                fwd = _remote(landed, landed, sends, recvs, 6 * m + 3 + j, sibling)
                fwd.start()
                passed.append(fwd)
        for j, (px, py) in enumerate(chips_of(x, y)):
            for m, piece in enumerate(pieces):
                theirs = win(b, piece, 2 * px + py, 1 - c)
                _remote(theirs, theirs, sends, recvs, 6 * m + 3 + j, sibling).wait_recv()
        for cp in outgoing(b, sends, recvs) + passed:
            cp.wait_send()

    return _Side(bufs, 6 * len(pieces), start, finish)


_RELS = [(dx, dy, dc) for dx in (0, 1) for dy in (0, 1) for dc in (0, 1)][1:]


def _rs_side(bufs, pieces):
    def flip(v, d):
        return 1 - v if d else v

    def copies(b, sends, recvs, landing):
        x, y, c = _me()
        dev = 4 * x + 2 * y + c
        cps = []
        for m, (gi, ri, (R, C), at, k, i) in enumerate(pieces):
            rows = R // k
            for t, (dx, dy, dc) in enumerate(_RELS):
                tx, ty, tc = flip(x, dx), flip(y, dy), flip(c, dc)
                if landing:
                    theirs = b[ri].at[4 * tx + 2 * ty + tc, pl.ds(i * rows, rows), :]
                    cps.append(_remote(theirs, theirs, sends, recvs, 7 * m + t, (tx, ty, tc)))
                else:
                    r0, c0 = at(2 * tx + ty, tc)
                    src = _window(b[gi], r0 + i * rows, c0, rows, C)
                    cps.append(_remote(src, b[ri].at[dev, pl.ds(i * rows, rows), :], sends, recvs, 7 * m + t, (tx, ty, tc)))
        return cps

    def start(b, sends, recvs):
        for cp in copies(b, sends, recvs, False):
            cp.start()

    def finish(b, sends, recvs):
        for cp in copies(b, sends, recvs, True):
            cp.wait_recv()
        for cp in copies(b, sends, recvs, False):
            cp.wait_send()

    return _Side(bufs, 7 * len(pieces), start, finish)


def _comm_only(side, name):
    n = len(side.bufs)

    def body(*refs):
        bufs, (sends, recvs) = refs[n:2 * n], refs[2 * n:]
        side.start(bufs, sends, recvs)
        side.finish(bufs, sends, recvs)

    outs = pl.pallas_call(
        body, name=name, out_shape=tuple(jax.ShapeDtypeStruct(b.shape, b.dtype) for b in side.bufs),
        in_specs=[ANY] * n, out_specs=tuple([ANY] * n), input_output_aliases={a: a for a in range(n)},
        scratch_shapes=[pltpu.SemaphoreType.DMA((side.nsem,)), pltpu.SemaphoreType.DMA((side.nsem,))],
    )(*side.bufs)
    return list(outs)


def _rs_sum8(grad, recv, win, kind, name):
    R, C = win
    tr = _row_tile(R, C)
    nr = R // tr
    xi, yi, ci = _me()
    s_arr = jnp.stack([4 * xi + 2 * yi + ci, ci]).astype(jnp.int32)

    if kind == "col":
        g_spec = BS((tr, C), lambda i, s: (s[1] * nr + i, s[0] // 2))
    else:
        g_spec = BS((tr, C), lambda i, s: (s[0] * nr + i, 0))

    def body(s_ref, g_ref, *refs):
        acc = g_ref[...].astype(F32)
        for r_ref in refs[:7]:
            acc = acc + r_ref[0].astype(F32)
        refs[7][...] = acc

    def other(t):
        return BS((1, tr, C), lambda i, s: (jnp.bitwise_xor(s[0], t), i, 0))

    return pl.pallas_call(
        body, name=name, out_shape=jax.ShapeDtypeStruct((2 * R, C), F32),
        grid_spec=pltpu.PrefetchScalarGridSpec(
            num_scalar_prefetch=1, grid=(nr,), in_specs=[g_spec] + [other(t) for t in range(1, 8)],
            out_specs=BS((tr, C), lambda i, s: (s[1] * nr + i, 0))),
        compiler_params=_cp(("parallel",)),
    )(s_arr, grad, *([recv] * 7))


def _share_side(shards):
    def half(b, a, pc):
        R = b[a].shape[0] // 2
        return b[a].at[pl.ds(pl.multiple_of(pc * R, SUBLANES), R), :]

    def start(b, sends, recvs):
        x, y, c = _me()
        for a in range(len(shards)):
            _remote(half(b, a, c), half(b, a, c), sends, recvs, a, (x, y, 1 - c)).start()

    def finish(b, sends, recvs):
        x, y, c = _me()
        for a in range(len(shards)):
            _remote(half(b, a, 1 - c), half(b, a, 1 - c), sends, recvs, a, (x, y, 1 - c)).wait_recv()
        for a in range(len(shards)):
            _remote(half(b, a, c), half(b, a, c), sends, recvs, a, (x, y, 1 - c)).wait_send()

    return _Side(shards, len(shards), start, finish)


BLOB_ALIGN = SUBLANES * LANES


def _pack(arrs):
    flat = jnp.concatenate([a.reshape(-1).astype(F32) for a in arrs])
    n = flat.shape[0]
    padded = -(-n // BLOB_ALIGN) * BLOB_ALIGN
    return jnp.pad(flat, (0, padded - n)).reshape(padded // LANES, LANES)


def _unpack(flat, shapes):
    out, off = [], 0
    for s in shapes:
        n = math.prod(s)
        out.append(flat[..., off:off + n].reshape(flat.shape[:-1] + tuple(s)))
        off += n
    return out


def _gather_blob(blob, name):
    r = blob.shape[0]
    at = lambda px, py, pc: ((4 * px + 2 * py + pc) * r, 0)
    (out,) = _allgather8([(blob, lambda px, py, pc: (0, 0), (r, LANES), (8 * r, LANES), at)], name)
    return out.reshape(8, r * LANES)


def _conv_ffn_fwd(mm, full, layer, X, mods, n2g, conv_w, conv_b, L, tb, nlat, ch, tag):
    sh2, sc2, g2 = mods[3], mods[4], mods[5]
    h2 = _norm_mod_fwd(X, n2g, sh2, sc2, tb, nlat, tag + "_norm2")
    zf = mm(h2, full[f"ffn_w_up{layer}"], "nn", BF16, tag + "_up", tm=(544, 512), tn=(2816, 512))
    f = _ffn_act_fwd(zf, conv_w, conv_b, L, ch, tag + "_act")
    Xn, yf = mm(f, full[f"ffn_w_down{layer}"], "nn", F32, tag + "_down", tm=(544, 512), tn=(512,), res=(X, g2, L))
    return Xn, (X, h2, zf, f, yf)


def _conv_ffn_bwd(mm, full, gbuf, layer, dXn, saved, mods, n2g, conv_w, conv_b, L, tb, nlat, ch, tag):
    X, h2, zf, f, yf = saved
    sc2, g2 = mods[4], mods[5]
    w_up, w_down = full[f"ffn_w_up{layer}"], full[f"ffn_w_down{layer}"]
    dy, dg2 = _gate_bwd(dXn, yf, g2, tb, nlat, tag + "_dres2")
    df = mm(dy, w_down, "nt", BF16, tag + "_ddown_x", tm=(544, 512), tn=(2816, 512))
    gbuf[f"ffn_w_down{layer}"] = mm(f, dy, "tn", BF16, tag + "_ddown_w", tm=(512,), tn=(1024, 512))
    dzf, dcw, dcb = _ffn_act_bwd(zf, df, conv_w, conv_b, L, ch, tag + "_dact")
    gbuf[f"ffn_w_up{layer}"] = mm(h2, dzf, "tn", BF16, tag + "_dup_w", tm=(1024, 512), tn=(512,), outer="i", halves=True)
    dh2 = mm(dzf, w_up, "nt", BF16, tag + "_dup_x", tm=(1088, 1024, 512), tn=(1024,), tk=(2816, 512), halves=True)
    dX, dn2g, dsh2, dsc2 = _norm_mod_bwd(X, n2g, sc2, dh2, dXn, tb, nlat, tag + "_dnorm2")
    return dX, dict(n2g=dn2g, sh2=dsh2, sc2=dsc2, g2=dg2, cw=dcw, cb=dcb)


def kernel(x, c, ctx, c_ctx, norm1_g, norm2_g, w_ada, b_ada, ab_w_in, ab_b_in, a_ln_g, a_ln_b, a_w_s, a_b_s, b_conv_w, b_conv_b, b_ln_g, b_ln_b, ab_w_out, mla_w_in, mla_q_norm_g, mla_w_uq, mla_kv_norm_g, mla_w_ukv, mla_w_o, ffn_w_up, ffn_conv_w, ffn_conv_b, ffn_w_down, final_norm_g, loss_target, m_c_ctx, m_norm1_g, m_norm2_g, m_w_ada, m_b_ada, m_ab_w_in, m_ab_b_in, m_a_ln_g, m_a_ln_b, m_a_w_s, m_a_b_s, m_b_conv_w, m_b_conv_b, m_b_ln_g, m_b_ln_b, m_ab_w_out, m_mla_w_in, m_mla_q_norm_g, m_mla_w_uq, m_mla_kv_norm_g, m_mla_w_ukv, m_mla_w_o, m_ffn_w_up, m_ffn_conv_w, m_ffn_conv_b, m_ffn_w_down, m_final_norm_g, v_c_ctx, v_norm1_g, v_norm2_g, v_w_ada, v_b_ada, v_ab_w_in, v_ab_b_in, v_a_ln_g, v_a_ln_b, v_a_w_s, v_a_b_s, v_b_conv_w, v_b_conv_b, v_b_ln_g, v_b_ln_b, v_ab_w_out, v_mla_w_in, v_mla_q_norm_g, v_mla_w_uq, v_mla_kv_norm_g, v_mla_w_ukv, v_mla_w_o, v_ffn_w_up, v_ffn_conv_w, v_ffn_conv_b, v_ffn_w_down, v_final_norm_g):
    W = dict(c_ctx=c_ctx, norm1_g=norm1_g, norm2_g=norm2_g, w_ada=w_ada, b_ada=b_ada, ab_w_in=ab_w_in, ab_b_in=ab_b_in, a_ln_g=a_ln_g,
             a_ln_b=a_ln_b, a_w_s=a_w_s, a_b_s=a_b_s, b_conv_w=b_conv_w, b_conv_b=b_conv_b, b_ln_g=b_ln_g, b_ln_b=b_ln_b,
             ab_w_out=ab_w_out, mla_w_in=mla_w_in, mla_q_norm_g=mla_q_norm_g, mla_w_uq=mla_w_uq, mla_kv_norm_g=mla_kv_norm_g,
             mla_w_ukv=mla_w_ukv, mla_w_o=mla_w_o, ffn_w_up=ffn_w_up, ffn_conv_w=ffn_conv_w, ffn_conv_b=ffn_conv_b,
             ffn_w_down=ffn_w_down, final_norm_g=final_norm_g)
    MOM = dict(c_ctx=m_c_ctx, norm1_g=m_norm1_g, norm2_g=m_norm2_g, w_ada=m_w_ada, b_ada=m_b_ada, ab_w_in=m_ab_w_in, ab_b_in=m_ab_b_in,
               a_ln_g=m_a_ln_g, a_ln_b=m_a_ln_b, a_w_s=m_a_w_s, a_b_s=m_a_b_s, b_conv_w=m_b_conv_w, b_conv_b=m_b_conv_b,
               b_ln_g=m_b_ln_g, b_ln_b=m_b_ln_b, ab_w_out=m_ab_w_out, mla_w_in=m_mla_w_in, mla_q_norm_g=m_mla_q_norm_g,
               mla_w_uq=m_mla_w_uq, mla_kv_norm_g=m_mla_kv_norm_g, mla_w_ukv=m_mla_w_ukv, mla_w_o=m_mla_w_o, ffn_w_up=m_ffn_w_up,
               ffn_conv_w=m_ffn_conv_w, ffn_conv_b=m_ffn_conv_b, ffn_w_down=m_ffn_w_down, final_norm_g=m_final_norm_g)
    VAR = dict(c_ctx=v_c_ctx, norm1_g=v_norm1_g, norm2_g=v_norm2_g, w_ada=v_w_ada, b_ada=v_b_ada, ab_w_in=v_ab_w_in, ab_b_in=v_ab_b_in,
               a_ln_g=v_a_ln_g, a_ln_b=v_a_ln_b, a_w_s=v_a_w_s, a_b_s=v_a_b_s, b_conv_w=v_b_conv_w, b_conv_b=v_b_conv_b,
               b_ln_g=v_b_ln_g, b_ln_b=v_b_ln_b, ab_w_out=v_ab_w_out, mla_w_in=v_mla_w_in, mla_q_norm_g=v_mla_q_norm_g,
               mla_w_uq=v_mla_w_uq, mla_kv_norm_g=v_mla_kv_norm_g, mla_w_ukv=v_mla_w_ukv, mla_w_o=v_mla_w_o, ffn_w_up=v_ffn_w_up,
               ffn_conv_w=v_ffn_conv_w, ffn_conv_b=v_ffn_conv_b, ffn_w_down=v_ffn_w_down, final_norm_g=v_final_norm_g)
    ORDER = list(W.keys())

    L, D = x.shape[1], x.shape[2]
    CT = ctx.shape[1]
    T = L + CT
    AW, BW = a_ln_g.shape[-1], b_ln_g.shape[-1]
    AH = a_w_s.shape[1]
    QL, KVL = 4 * mla_q_norm_g.shape[-1], 4 * mla_kv_norm_g.shape[-1]
    H = 4 * mla_w_o.shape[1] // HEAD
    HS = H // 4
    DFF = ffn_conv_b.shape[-1]
    NA = w_ada.shape[-1]
    tb = 256 if (L % 256 == 0 and CT % 256 == 0) else 128
    nlat = L // tb
    ch = tb
    tq = 256 if L >= 512 else 128
    xi, yi, ci = _me()
    p_me = 2 * xi + yi
    dev = 4 * xi + 2 * yi + ci

    shard_small = [c[0], mla_q_norm_g[0], mla_kv_norm_g[0], b_conv_w[0], ffn_conv_w]
    g0 = _gather_blob(_pack(shard_small), "gather_small")
    c_all, qg_s, kvg_s, bcw_s, fcw_s = _unpack(g0, [a.shape for a in shard_small])
    per_chip = lambda a: a[0::2]
    qg = per_chip(qg_s).reshape(QL)
    kvg = per_chip(kvg_s).reshape(KVL)
    bcw = jnp.concatenate(list(per_chip(bcw_s)), axis=-1)
    fcw = jnp.concatenate(list(per_chip(fcw_s)), axis=-1)
    c16 = jnp.concatenate([c_all, c_ctx[None], jnp.zeros((7, D), F32)], axis=0)

    ms = []
    for i in range(2):
        bias = lax.dynamic_slice(b_ada[i], (p_me * NA,), (NA,))
        ms.append(_mm(c16, w_ada, "nn", F32, f"ada{i}", tm=(16,), tn=(512,), bias=bias, silu_a=True, b_lead=i))
    ms = jnp.concatenate(ms, axis=0)
    (mods_all,) = _allgather8(
        [(ms, lambda px, py, pc: (pc * 16, 0), (16, NA), (32, 4 * NA), lambda px, py, pc: (pc * 16, (2 * px + py) * NA))], "gather_mods")
    mods_all = mods_all.reshape(2, 16, N_MOD, D)
    mods = []
    for i in range(2):
        lat = lax.dynamic_index_in_dim(mods_all[i], dev, axis=0, keepdims=False)
        both = jnp.stack([lat, mods_all[i, 8]], axis=0)
        mods.append([both[:, k][:, None, :] for k in range(N_MOD)])

    def pad_uq(w):
        w = w.reshape(w.shape[0], HS, HEAD + ROPE)
        return jnp.pad(w, ((0, 0), (0, 0), (0, QHEAD - HEAD - ROPE))).reshape(w.shape[0], HS * QHEAD)

    MI = QL + KVL + LANES
    big = [
        ("ab_w_in", ab_w_in, 0, "col"), ("ab_w_out", ab_w_out, 0, "row"),
        ("mla_w_in", jnp.pad(mla_w_in[0], ((0, 0), (0, MI - mla_w_in.shape[-1]))), None, "row"),
        ("mla_w_uq", pad_uq(mla_w_uq[0]), None, "col"), ("mla_w_ukv", mla_w_ukv, 0, "col"), ("mla_w_o", mla_w_o, 0, "row"),
        ("ffn_w_up0", ffn_w_up, 0, "col"), ("ffn_w_up1", ffn_w_up, 1, "col"),
        ("ffn_w_down0", ffn_w_down, 0, "row"), ("ffn_w_down1", ffn_w_down, 1, "row"),
    ]
    full, wins = {}, {}
    for nm, w, lead, kind in big:
        R, C = w.shape[-2] // 2, w.shape[-1]
        if kind == "col":
            wins[nm] = ((R, C), lambda p, pc, R=R, C=C: (pc * R, p * C), kind)
        else:
            wins[nm] = ((R, C), lambda p, pc, R=R, C=C: ((2 * p + pc) * R, 0), kind)
        full[nm] = _cast_place(w, kind, "cast_" + nm, lead)
    names = [b[0] for b in big]

    gather_plan = {
        "l0_norm1": [("ab_w_out", 1, 0)], "l0_in": [("ffn_w_up0", 4, 0)], "l0_conv": [("ffn_w_up0", 4, 1)],
        "l0_out": [("ffn_w_up0", 4, 2)], "l0_ffn_norm2": [("ffn_w_up0", 4, 3)],
        "l0_ffn_up": [("ffn_w_down0", 1, 0), ("mla_w_in", 1, 0), ("mla_w_uq", 1, 0)],
        "l0_ffn_down": [("mla_w_ukv", 1, 0), ("mla_w_o", 1, 0)],
        "l1_attn": [("ffn_w_up1", 1, 0), ("ffn_w_down1", 1, 0)],
    }
    reduce_plan = {
        "l1_ffn_dup_x": [("ffn_w_down1", 1, 0)], "l1_dattn": [("ffn_w_up1", 1, 0), ("mla_w_o", 1, 0)],
        "l0_ffn_ddown_x": [("mla_w_ukv", 1, 0), ("mla_w_uq", 1, 0)], "l0_ffn_ddown_w": [("mla_w_in", 1, 0)],
        "l0_ffn_dact": [("ffn_w_down0", 2, 0)], "l0_ffn_dup_w": [("ffn_w_down0", 2, 1)],
        "l0_ffn_dup_x": [("ffn_w_up0", 8, 0), ("ffn_w_up0", 8, 1), ("ffn_w_up0", 8, 2)], "l0_ffn_dnorm2": [("ffn_w_up0", 8, 3)],
        "l0_dout_w": [("ffn_w_up0", 8, 4)], "l0_dout_x": [("ffn_w_up0", 8, 5)], "l0_dgmlp": [("ffn_w_up0", 8, 6)],
        "l0_dconv": [("ffn_w_up0", 8, 7), ("ab_w_out", 1, 0)],
        "l0_din_x": [("ab_w_in", 2, 0)], "l0_dnorm1": [("ab_w_in", 2, 1)],
    }
    gbuf, rbuf = {}, {}

    def gather_side(req):
        nms = list(dict.fromkeys(nm for nm, _, _ in req))
        side = _gather_side([full[nm] for nm in nms], [(nms.index(nm), wins[nm][0], wins[nm][1], k, i) for nm, k, i in req])

        def commit(bufs):
            full.update(zip(nms, bufs))

        return side, commit

    def reduce_side(req):
        nms = list(dict.fromkeys(nm for nm, _, _ in req))
        for nm in nms:
            if nm not in rbuf:
                rbuf[nm] = lax.empty((8,) + wins[nm][0], BF16)
        n = len(nms)
        side = _rs_side([gbuf[nm] for nm in nms] + [rbuf[nm] for nm in nms],
                        [(nms.index(nm), n + nms.index(nm), wins[nm][0], wins[nm][1], k, i) for nm, k, i in req])

        def commit(bufs):
            gbuf.update(zip(nms, bufs[:n]))
            rbuf.update(zip(nms, bufs[n:]))

        return side, commit

    _carry_plan.clear()
    _carry_plan.update({name: functools.partial(gather_side, req) for name, req in gather_plan.items()})
    _carry_plan.update({name: functools.partial(reduce_side, req) for name, req in reduce_plan.items()})
    mm, attn_fwd, attn_bwd = _mm, _attn_fwd, _attn_bwd

    side, commit = gather_side([("ab_w_in", 1, 0)])
    commit(_comm_only(side, "gather_first"))

    X0 = jnp.concatenate([x[0], ctx[0]], axis=0)
    m0, m1 = mods[0], mods[1]
    h1 = _norm_mod_fwd(X0, norm1_g[0], m0[0], m0[1], tb, nlat, "l0_norm1")
    z = mm(h1, full["ab_w_in"], "nn", BF16, "l0_in", tm=(544, 512), tn=(1024, 512), bias=ab_b_in[0])
    bs_full = jnp.broadcast_to(a_b_s[0][:, :, None], (AH, CHUNK, CHUNK))
    ya = _gmlp_fwd(z, a_ln_g[0], a_ln_b[0], a_w_s[0], bs_full, tb, "l0_gmlp")
    hc = _glu_conv_fwd(z, 2 * AW, bcw, b_conv_b[0], L, ch, "l0_conv")
    yb = _ln_silu_fwd(hc, b_ln_g[0], b_ln_b[0], tb, "l0_lnsilu")
    yab = jnp.concatenate([ya, yb], axis=1)
    X1, y0 = mm(yab, full["ab_w_out"], "nn", F32, "l0_out", tm=(544, 512), tn=(1024, 512), res=(X0, m0[2], L))
    X2, ffn0 = _conv_ffn_fwd(mm, full, 0, X1, m0, norm2_g[0], fcw[0], ffn_conv_b[0], L, tb, nlat, ch, "l0_ffn")

    tabs = _rope_tables(L, T)
    hm = _norm_mod_fwd(X2, norm1_g[1], m1[0], m1[1], tb, nlat, "l1_norm1")
    zm = mm(hm, full["mla_w_in"], "nn", F32, "l1_in", tm=(544, 512), tn=(MI,))
    cqn, ckvn, kpe = _mla_prep_fwd(zm, qg, kvg, tabs, tb, "l1_prep")
    q = mm(cqn, full["mla_w_uq"], "nn", BF16, "l1_uq", tm=(512,), tn=(1024, 512), rows=L)
    kvh = mm(ckvn, full["mla_w_ukv"], "nn", BF16, "l1_ukv", tm=(544, 512), tn=(1024, 512))
    tq2 = 2 * tq if L % (2 * tq) == 0 and L > 2 * tq else tq
    o, lse = attn_fwd(q, kvh, kpe, tabs, L, tq2, "l1_attn")
    m1_lat = [a[:1] for a in m1]
    X3, yl = mm(o, full["mla_w_o"], "nn", F32, "l1_o", tm=(512,), tn=(1024, 512), res=(X2, m1_lat[2], L))
    X4, ffn1 = _conv_ffn_fwd(mm, full, 1, X3, m1_lat, norm2_g[1], fcw[1], ffn_conv_b[1], L, tb, nlat, ch, "l1_ffn")
    loss_acc, dX4, dfinal = _final_loss(X4, final_norm_g, loss_target[0], tb, "loss")

    dX3, gf1 = _conv_ffn_bwd(mm, full, gbuf, 1, dX4, ffn1, m1_lat, norm2_g[1], fcw[1], ffn_conv_b[1], L, tb, nlat, ch, "l1_ffn")
    dyl, dg1_1 = _gate_bwd(dX3, yl, m1_lat[2], tb, nlat, "l1_dres1")
    do = mm(dyl, full["mla_w_o"], "nt", BF16, "l1_do_x", tm=(512,), tn=(1024, 512))
    gbuf["mla_w_o"] = mm(o, dyl, "tn", BF16, "l1_do_w", tm=(512,), tn=(1024, 512))
    dq, dkv, dkpe = attn_bwd(q, kvh, kpe, tabs, o, lse, do, L, tq2, "l1_dattn")
    dckvn = mm(dkv, full["mla_w_ukv"], "nt", BF16, "l1_dukv_x", tm=(544, 512), tn=(KVL,), tk=(2048, 512))
    gbuf["mla_w_ukv"] = mm(ckvn, dkv, "tn", BF16, "l1_dukv_w", tm=(512,), tn=(1024, 512))
    dcqn = mm(dq, full["mla_w_uq"], "nt", BF16, "l1_duq_x", tm=(512,), tn=(QL,), tk=(2048, 512))
    gbuf["mla_w_uq"] = mm(cqn, dq, "tn", BF16, "l1_duq_w", tm=(QL,), tn=(1024, 512), rows=L)
    dcqn = jnp.concatenate([dcqn, jnp.zeros((CT, QL), BF16)], axis=0)
    dzm, dqg, dkvg = _mla_prep_bwd(zm, qg, kvg, tabs, dcqn, dckvn, dkpe, tb, "l1_dprep")
    dhm = mm(dzm, full["mla_w_in"], "nt", BF16, "l1_din_x", tm=(544, 512), tn=(1024, 512))
    gbuf["mla_w_in"] = mm(hm, dzm, "tn", BF16, "l1_din_w", tm=(512,), tn=(MI,))
    dX2, dn1g_1, dsh1_1, dsc1_1 = _norm_mod_bwd(X2, norm1_g[1], m1[1], dhm, dX3, tb, nlat, "l1_dnorm1")

    dX1, gf0 = _conv_ffn_bwd(mm, full, gbuf, 0, dX2, ffn0, m0, norm2_g[0], fcw[0], ffn_conv_b[0], L, tb, nlat, ch, "l0_ffn")
    dy0, dg1_0 = _gate_bwd(dX1, y0, m0[2], tb, nlat, "l0_dres1")
    gbuf["ab_w_out"] = mm(yab, dy0, "tn", BF16, "l0_dout_w", tm=(512,), tn=(1024, 512))
    dyab = mm(dy0, full["ab_w_out"], "nt", BF16, "l0_dout_x", tm=(544, 512), tn=(1024, 512))
    dzu, dzv, dlnag, dlnab, dws, dbs, csu, csv = _gmlp_bwd(z, dyab, a_ln_g[0], a_ln_b[0], a_w_s[0], bs_full, tb, "l0_dgmlp")
    dhc, dlnbg, dlnbb = _ln_silu_bwd(hc, dyab, AW, b_ln_g[0], b_ln_b[0], tb, "l0_dlnsilu")
    dza, dzg, dbcw, dbcb, csa, csg = _glu_conv_bwd(z, 2 * AW, dhc, bcw, L, ch, "l0_dconv")
    dz = jnp.concatenate([dzu, dzv, dza, dzg], axis=1)
    dbin = jnp.concatenate([csu, csv, csa, csg], axis=1)
    gbuf["ab_w_in"] = mm(h1, dz, "tn", BF16, "l0_din_w", tm=(512,), tn=(1024, 512))
    dh1 = mm(dz, full["ab_w_in"], "nt", BF16, "l0_din_x", tm=(544, 512), tn=(1024, 512), tk=(2048, 512))
    dX0, dn1g_0, dsh1_0, dsc1_0 = _norm_mod_bwd(X0, norm1_g[0], m0[1], dh1, dX1, tb, nlat, "l0_dnorm1")
    grad_x = dX0[:L][None]

    halves = [_rs_sum8(gbuf[nm], rbuf[nm], wins[nm][0], wins[nm][2], "rs_sum_" + nm) for nm in names]

    def grp6(l, sh1, sc1, g1, f):
        G = sh1.shape[0]
        pad = lambda a: jnp.concatenate([a, jnp.zeros((G - a.shape[0],) + a.shape[1:], F32)], axis=0) if a.shape[0] < G else a
        return jnp.concatenate([pad(a) for a in (sh1, sc1, g1, f["sh2"], f["sc2"], f["g2"])], axis=1)

    dm0 = grp6(0, dsh1_0, dsc1_0, dg1_0, gf0)
    dm1 = grp6(1, dsh1_1, dsc1_1, dg1_1, gf1)
    dmods = jnp.stack([dm0, dm1], axis=0)
    small = [
        jnp.concatenate([dn1g_0, dn1g_1], axis=0), jnp.concatenate([gf0["n2g"], gf1["n2g"]], axis=0), dbin, dlnag, dlnab, dws,
        dbs, dbcw, dbcb, dlnbg, dlnbb, dqg, dkvg, jnp.stack([gf0["cw"], gf1["cw"]], axis=0),
        jnp.concatenate([gf0["cb"], gf1["cb"]], axis=0), dfinal, dmods[:, 1],
    ]
    small_shapes = [a.shape for a in small]
    lat_shape = dmods[:, 0].shape
    blob = _pack(small + [dmods[:, 0], loss_acc[0, :1]])
    gathered = _gather_blob(blob, "gather_grads")
    summed = _sum_lead(gathered.reshape(8, -1, LANES), F32, "sum_grads").reshape(-1)
    (dn1g, dn2g, dbin_s, dlnag_s, dlnab_s, dws_s, dbs_s, dbcw_s, dbcb_s, dlnbg_s, dlnbb_s, dqg_s, dkvg_s, dfcw_s, dfcb_s, dfinal_s,
     dmods_ctx, dmods_lat_sum, loss) = _unpack(summed, small_shapes + [lat_shape, (1,)])
    loss = loss.reshape(())
    n_small = sum(math.prod(s) for s in small_shapes)
    dmods_lat = gathered[:, n_small:n_small + math.prod(lat_shape)].reshape((8,) + lat_shape)

    grad_w_ada, dc_parts = [], []
    for i in range(2):
        dm16 = jnp.concatenate([dmods_lat[:, i].reshape(8, N_MOD * D), dmods_ctx[i].reshape(1, N_MOD * D),
                                jnp.zeros((7, N_MOD * D), F32)], axis=0)
        dm16_s = lax.dynamic_slice(dm16, (0, p_me * NA), (16, NA))
        grad_w_ada.append(_mm(c16, dm16_s, "tn", F32, f"dada{i}_w", tm=(1024, 512), tn=(1024, 512), silu_a=True))
        dc_parts.append(_mm(dm16_s, w_ada, "nt", F32, f"dada{i}_c", tm=(16,), tn=(512,), tk=(1024, 512), b_lead=i))
    grad_b_ada = dmods_lat_sum.reshape(2, N_MOD * D) + dmods_ctx.reshape(2, N_MOD * D)
    dc_blob = _pack([dc_parts[0][8] + dc_parts[1][8]])
    dc_all = _gather_blob(dc_blob, "gather_dc")[0::2, :D]
    grad_c_ctx = _cctx_grad(dc_all, c_ctx, "dcctx").reshape(D)

    gshard = dict(zip(names, _comm_only(_share_side(halves), "rs_share")))
    two = lambda a: a.reshape(-1, a.shape[-1])
    ada_update = _adamw_layers(w_ada, grad_w_ada, m_w_ada, v_w_ada, "adamw_w_ada")

    def my_cols(a, axis, n):
        return lax.dynamic_slice_in_dim(a, p_me * n, n, axis=axis)

    unpad_uq = lambda g: g.reshape(QL, HS, QHEAD)[:, :, :HEAD + ROPE].reshape(QL, HS * (HEAD + ROPE))
    grads = dict(
        c_ctx=grad_c_ctx, norm1_g=dn1g, norm2_g=dn2g, b_ada=grad_b_ada, ab_w_in=gshard["ab_w_in"][None],
        ab_b_in=dbin_s, a_ln_g=dlnag_s, a_ln_b=dlnab_s, a_w_s=dws_s[None], a_b_s=dbs_s.reshape(1, AH, CHUNK),
        b_conv_w=my_cols(dbcw_s, 1, BW // 4)[None], b_conv_b=dbcb_s, b_ln_g=dlnbg_s, b_ln_b=dlnbb_s,
        ab_w_out=gshard["ab_w_out"][None], mla_w_in=gshard["mla_w_in"][:, :mla_w_in.shape[-1]][None],
        mla_q_norm_g=my_cols(dqg_s, 1, QL // 4), mla_w_uq=unpad_uq(gshard["mla_w_uq"])[None],
        mla_kv_norm_g=my_cols(dkvg_s, 1, KVL // 4), mla_w_ukv=gshard["mla_w_ukv"][None], mla_w_o=gshard["mla_w_o"][None],
        ffn_conv_w=my_cols(dfcw_s, 2, DFF // 4), ffn_conv_b=dfcb_s, final_norm_g=dfinal_s.reshape(D),
    )
    LAYERED = ("w_ada", "ffn_w_up", "ffn_w_down")
    grads = {k: grads[k].reshape(W[k].shape) for k in ORDER if k not in LAYERED}

    BIG = ("ab_w_in", "ab_w_out", "mla_w_in", "mla_w_uq", "mla_w_ukv", "mla_w_o") + LAYERED
    delta, new_m, new_v = {}, {}, {}
    for k in BIG:
        if k == "w_ada":
            grads[k], delta[k], new_m[k], new_v[k] = ada_update
        elif k in LAYERED:
            grads[k], delta[k], new_m[k], new_v[k] = _adamw_layers(W[k], [gshard[k + "0"], gshard[k + "1"]], MOM[k], VAR[k], "adamw_" + k)
        else:
            upd = _adamw(two(W[k]), two(grads[k]), two(MOM[k]), two(VAR[k]), "adamw_" + k)
            delta[k], new_m[k], new_v[k] = (a.reshape(W[k].shape) for a in upd)
    SMALL = [k for k in ORDER if k not in BIG]
    small_upd = _adamw(_pack([W[k] for k in SMALL]), _pack([grads[k] for k in SMALL]), _pack([MOM[k] for k in SMALL]),
                       _pack([VAR[k] for k in SMALL]), "adamw_small")
    shapes = [W[k].shape for k in SMALL]
    for k, d_k, m_k, v_k in zip(SMALL, *[_unpack(a.reshape(-1), shapes) for a in small_upd]):
        delta[k], new_m[k], new_v[k] = d_k, m_k, v_k

    return (loss, grad_x, *[grads[k] for k in ORDER], *[delta[k] for k in ORDER], *[new_m[k] for k in ORDER],
            *[new_v[k] for k in ORDER])
```

```python
import functools
import math

import jax
import jax.numpy as jnp
from jax import lax
from jax.experimental import pallas as pl
from jax.experimental.pallas import tpu as pltpu

F32 = jnp.float32
BF16 = jnp.bfloat16
MESH = pl.DeviceIdType.MESH
ANY = pl.BlockSpec(memory_space=pl.ANY)

EPS = 1e-6
N_MOD = 6
CHUNK = 128
HEAD = 128
ROPE = 64
QHEAD = 2 * HEAD
GRID_W = 64
ROPE_THETA = 10000.0
B_CONV = 31
FFN_CONV = 3
ADAM_LR, ADAM_B1, ADAM_B2, ADAM_EPS, ADAM_WD, ADAM_STEP = 0.001, 0.9, 0.999, 1e-08, 0.01, 10

V7X_VMEM_LIMIT = 56 * 1024 * 1024
LANES = 128
SUBLANES = 8
CONV_PAD = 16

BS = pl.BlockSpec


def _cp(sem=None, vmem=V7X_VMEM_LIMIT):
    return pltpu.CompilerParams(dimension_semantics=sem, vmem_limit_bytes=vmem)


class _Side:
    def __init__(self, bufs, nsem, start, finish):
        self.bufs, self.nsem, self.start, self.finish = list(bufs), nsem, start, finish


_carry_plan = {}


def _call(body, *, name, out_shape, grid, in_specs, out_specs, operands, sem, scratch=()):
    if name not in _carry_plan:
        return pl.pallas_call(body, name=name, out_shape=out_shape, grid=grid, in_specs=in_specs, out_specs=out_specs,
                              scratch_shapes=list(scratch), compiler_params=_cp(sem))(*operands)
    side, commit = _carry_plan[name]()
    multi = isinstance(out_shape, (tuple, list))
    outs = tuple(out_shape) if multi else (out_shape,)
    ospecs = tuple(out_specs) if multi else (out_specs,)
    n_in, n_out, n_scr, n_buf = len(in_specs), len(outs), len(scratch), len(side.bufs)

    def body2(*refs):
        o0 = n_in + n_buf
        s0 = o0 + n_out + n_buf
        bufs = refs[o0 + n_out:s0]
        sends, recvs = refs[s0 + n_scr:]
        first = last = None
        for d, g in enumerate(grid):
            pid = pl.program_id(d)
            first = (pid == 0) if first is None else jnp.logical_and(first, pid == 0)
            last = (pid == g - 1) if last is None else jnp.logical_and(last, pid == g - 1)

        @pl.when(first)
        def _():
            side.start(bufs, sends, recvs)

        body(*refs[:n_in], *refs[o0:o0 + n_out], *refs[s0:s0 + n_scr])

        @pl.when(last)
        def _():
            side.finish(bufs, sends, recvs)

    res = pl.pallas_call(
        body2, name=name, out_shape=outs + tuple(jax.ShapeDtypeStruct(b.shape, b.dtype) for b in side.bufs), grid=grid,
        in_specs=list(in_specs) + [ANY] * n_buf, out_specs=ospecs + (ANY,) * n_buf,
        scratch_shapes=list(scratch) + [pltpu.SemaphoreType.DMA((side.nsem,)), pltpu.SemaphoreType.DMA((side.nsem,))],
        input_output_aliases={n_in + i: n_out + i for i in range(n_buf)}, compiler_params=_cp(("arbitrary",) * len(grid)),
    )(*operands, *side.bufs)
    commit(list(res[n_out:]))
    return tuple(res[:n_out]) if multi else res[0]


def _pcs(body, *, name, out_shape, grid, in_specs, out_specs, compiler_params, scratch_shapes=()):
    def run(*operands):
        return _call(body, name=name, out_shape=out_shape, grid=grid, in_specs=in_specs, out_specs=out_specs, operands=operands,
                     sem=compiler_params.dimension_semantics, scratch=scratch_shapes)
    return run


def _pick(n, prefs):
    for p in prefs:
        if p <= n and n % p == 0:
            return p
    return n


def _row_tile(rows, cols, target_bytes=1 << 20):
    best = None
    for d in range(2 * SUBLANES, rows + 1, 2 * SUBLANES):
        if rows % d == 0 and d * cols * 4 <= target_bytes:
            best = d
    return best if best is not None else rows


def _sigmoid(x):
    return 1.0 / (1.0 + jnp.exp(-x))


def _gelu(x):
    c = math.sqrt(2.0 / math.pi)
    th = jnp.tanh(c * (x + 0.044715 * x * x * x))
    return 0.5 * x * (1.0 + th), th


def _gelu_grad(x, th):
    c = math.sqrt(2.0 / math.pi)
    return 0.5 * (1.0 + th) + 0.5 * x * (1.0 - th * th) * c * (1.0 + 3.0 * 0.044715 * x * x)


_DIMS = {"nn": (((1,), (0,)), ((), ())), "nt": (((1,), (1,)), ((), ())), "tn": (((0,), (0,)), ((), ()))}


def _mm(a, b, mode, out_dtype, name, tm=(512,), tn=(512,), tk=(100000,), bias=None, silu_a=False, rows=None, outer="j", b_lead=None,
        halves=False, res=None):
    bshape = b.shape[-2:]
    if mode == "nn":
        (M, K), N = a.shape, bshape[1]
    elif mode == "nt":
        (M, K), N = a.shape[-2:], bshape[0]
        K = 2 * K if halves else K
    else:
        (K, M), N = a.shape, bshape[1]
        N = 2 * N if halves else N
    if rows is not None:
        if mode == "tn":
            K = rows
        else:
            M = rows
    tm, tn, tk = _pick(M, tm), _pick(N, tn), _pick(K, tk)
    gm, gn, gk = M // tm, N // tn, K // tk

    def ij(g0, g1):
        return (g1, g0) if outer == "j" else (g0, g1)

    if mode == "nn":
        a_spec = BS((tm, tk), lambda g0, g1, k: (ij(g0, g1)[0], k))
        b_spec = BS((tk, tn), lambda g0, g1, k: (k, ij(g0, g1)[1]))
    elif mode == "nt":
        a_spec = BS((tm, tk), lambda g0, g1, k: (ij(g0, g1)[0], k))
        b_spec = BS((tn, tk), lambda g0, g1, k: (ij(g0, g1)[1], k))
    else:
        a_spec = BS((tk, tm), lambda g0, g1, k: (k, ij(g0, g1)[0]))
        b_spec = BS((tk, tn), lambda g0, g1, k: (k, ij(g0, g1)[1]))
    if halves and mode == "nt":
        per = K // 2 // tk
        a_spec = BS((None, tm, tk), lambda g0, g1, k: (k // per, ij(g0, g1)[0], k % per))
    if halves and mode == "tn":
        per = N // 2 // tn
        b_spec = BS((None, tk, tn), lambda g0, g1, k: (ij(g0, g1)[1] // per, k, ij(g0, g1)[1] % per))
    if b_lead is not None:
        blk, at = b_spec.block_shape, b_spec.index_map
        b_spec = BS((None,) + tuple(blk), lambda g0, g1, k: (b_lead,) + tuple(at(g0, g1, k)))
    in_specs = [a_spec, b_spec]
    operands = [a, b]
    if bias is not None:
        in_specs.append(BS((1, tn), lambda g0, g1, k: (0, ij(g0, g1)[1])))
        operands.append(bias.reshape(1, N))
    o_spec = BS((tm, tn), lambda g0, g1, k: ij(g0, g1))
    out_shape, out_specs = jax.ShapeDtypeStruct((M, N), out_dtype), o_spec
    if res is not None:
        x_res, gate, lat_rows = res
        G = gate.shape[0]
        in_specs += [o_spec, BS((G, 1, tn), lambda g0, g1, k: (0, 0, ij(g0, g1)[1]))]
        operands += [x_res, gate]
        out_shape, out_specs = (jax.ShapeDtypeStruct((M, N), F32), jax.ShapeDtypeStruct((M, N), BF16)), (o_spec, o_spec)
    n_in = len(in_specs)

    def body(*refs):
        a_ref, b_ref = refs[0], refs[1]
        bias_ref = refs[2] if bias is not None else None
        o_ref = refs[n_in]
        av = a_ref[...]
        if silu_a:
            av = av.astype(F32)
            av = av * _sigmoid(av)
        part = lax.dot_general(av.astype(BF16), b_ref[...].astype(BF16), _DIMS[mode], preferred_element_type=F32)
        i_blk = pl.program_id(1 if outer == "j" else 0)

        def finish(acc):
            if bias_ref is not None:
                acc = acc + bias_ref[...]
            if res is None:
                o_ref[...] = acc.astype(out_dtype)
                return
            x_ref, gate_ref, y_ref = refs[n_in - 2], refs[n_in - 1], refs[n_in + 1]
            gate_ = gate_ref[0]
            if G == 2:
                rows = i_blk * tm + lax.broadcasted_iota(jnp.int32, (tm, 1), 0)
                gate_ = jnp.where(rows < lat_rows, gate_ref[0], gate_ref[1])
            o_ref[...] = x_ref[...] + gate_ * acc
            y_ref[...] = acc.astype(BF16)

        if gk == 1:
            finish(part)
        else:
            acc_ref = refs[-1]
            k = pl.program_id(2)

            @pl.when(k == 0)
            def _():
                acc_ref[...] = part

            @pl.when(k > 0)
            def _():
                acc_ref[...] += part

            @pl.when(k == gk - 1)
            def _():
                finish(acc_ref[...])

    grid = (gn, gm, gk) if outer == "j" else (gm, gn, gk)
    return _call(body, name=name, out_shape=out_shape, grid=grid, in_specs=in_specs, out_specs=out_specs,
                 scratch=[pltpu.VMEM((tm, tn), F32)] if gk > 1 else [], sem=("parallel", "parallel", "arbitrary"),
                 operands=operands)


def _accum(ref, val, first):
    @pl.when(first)
    def _():
        ref[...] = val

    @pl.when(jnp.logical_not(first))
    def _():
        ref[...] += val


def _norm_mod_fwd(X, gain, sh, sc, tb, nlat, name):
    R, D = X.shape

    def body(x_ref, g_ref, sh_ref, sc_ref, o_ref):
        x = x_ref[...]
        r = lax.rsqrt(jnp.mean(x * x, axis=-1, keepdims=True) + EPS)
        o_ref[...] = ((x * r * g_ref[...]) * (1.0 + sc_ref[0]) + sh_ref[0]).astype(BF16)

    grp = BS((1, 1, D), lambda i: (i // nlat, 0, 0))
    return _pcs(
        body, name=name, out_shape=jax.ShapeDtypeStruct((R, D), BF16), grid=(R // tb,),
        in_specs=[BS((tb, D), lambda i: (i, 0)), BS((1, D), lambda i: (0, 0)), grp, grp],
        out_specs=BS((tb, D), lambda i: (i, 0)), compiler_params=_cp(("parallel",)),
    )(X, gain.reshape(1, D), sh, sc)


def _norm_mod_bwd(X, gain, sc, dh, dup, tb, nlat, name):
    R, D = X.shape
    G = sc.shape[0]
    n_up = dup.shape[0] // tb

    def body(x_ref, g_ref, sc_ref, dh_ref, dup_ref, dx_ref, dg_ref, dsh_ref, dsc_ref):
        i = pl.program_id(0)
        x = x_ref[...]
        g = g_ref[...]
        r = lax.rsqrt(jnp.mean(x * x, axis=-1, keepdims=True) + EPS)
        xh = x * r
        dh_ = dh_ref[...].astype(F32)
        t = dh_ * (1.0 + sc_ref[0])
        tg = t * g
        dup_ = dup_ref[...] if n_up == R // tb else jnp.where(i < n_up, dup_ref[...], 0.0)
        dx_ref[...] = dup_ + r * (tg - xh * jnp.mean(tg * xh, axis=-1, keepdims=True))
        _accum(dg_ref, jnp.sum(t * xh, axis=0, keepdims=True), i == 0)
        first = i % nlat == 0
        _accum(dsh_ref, jnp.sum(dh_, axis=0, keepdims=True)[None], first)
        _accum(dsc_ref, jnp.sum(dh_ * xh * g, axis=0, keepdims=True)[None], first)

    row = BS((tb, D), lambda i: (i, 0))
    grp = BS((1, 1, D), lambda i: (i // nlat, 0, 0))
    return _pcs(
        body, name=name,
        out_shape=(jax.ShapeDtypeStruct((R, D), F32), jax.ShapeDtypeStruct((1, D), F32),
                   jax.ShapeDtypeStruct((G, 1, D), F32), jax.ShapeDtypeStruct((G, 1, D), F32)),
        grid=(R // tb,), in_specs=[row, BS((1, D), lambda i: (0, 0)), grp, row, BS((tb, D), lambda i: (jnp.minimum(i, n_up - 1), 0))],
        out_specs=(row, BS((1, D), lambda i: (0, 0)), grp, grp), compiler_params=_cp(("arbitrary",)),
    )(X, gain.reshape(1, D), sc, dh, dup)


def _gate_bwd(dX, y, gate, tb, nlat, name):
    R, D = dX.shape
    G = gate.shape[0]

    def body(dx_ref, y_ref, g_ref, dy_ref, dg_ref):
        i = pl.program_id(0)
        dx = dx_ref[...]
        dy_ref[...] = (g_ref[0] * dx).astype(BF16)
        _accum(dg_ref, jnp.sum(dx * y_ref[...], axis=0, keepdims=True)[None], i % nlat == 0)

    row = BS((tb, D), lambda i: (i, 0))
    grp = BS((1, 1, D), lambda i: (i // nlat, 0, 0))
    return _pcs(
        body, name=name, out_shape=(jax.ShapeDtypeStruct((R, D), BF16), jax.ShapeDtypeStruct((G, 1, D), F32)),
        grid=(R // tb,), in_specs=[row, row, grp], out_specs=(row, grp), compiler_params=_cp(("arbitrary",)),
    )(dX, y, gate)


def _final_loss(X, gain, target, tb, name):
    R, D = X.shape

    def body(x_ref, g_ref, t_ref, loss_ref, dx_ref, dg_ref):
        i = pl.program_id(0)
        x = x_ref[...]
        g = g_ref[...]
        r = lax.rsqrt(jnp.mean(x * x, axis=-1, keepdims=True) + EPS)
        xh = x * r
        e = xh * g - t_ref[...]
        part = jnp.sum(jnp.sum(e * e, axis=1, keepdims=True), axis=0, keepdims=True) * (0.5 / D)
        _accum(loss_ref, jnp.broadcast_to(part, (1, LANES)), i == 0)
        dy = e * (1.0 / D)
        _accum(dg_ref, jnp.sum(dy * xh, axis=0, keepdims=True), i == 0)
        tg = dy * g
        dx_ref[...] = r * (tg - xh * jnp.mean(tg * xh, axis=-1, keepdims=True))

    row = BS((tb, D), lambda i: (i, 0))
    return _pcs(
        body, name=name,
        out_shape=(jax.ShapeDtypeStruct((1, LANES), F32), jax.ShapeDtypeStruct((R, D), F32), jax.ShapeDtypeStruct((1, D), F32)),
        grid=(R // tb,), in_specs=[row, BS((1, D), lambda i: (0, 0)), row],
        out_specs=(BS((1, LANES), lambda i: (0, 0)), row, BS((1, D), lambda i: (0, 0))), compiler_params=_cp(("arbitrary",)),
    )(X, gain.reshape(1, D), target)


def _ln_stats(v):
    mu = jnp.mean(v, axis=-1, keepdims=True)
    d = v - mu
    r = lax.rsqrt(jnp.mean(d * d, axis=-1, keepdims=True) + EPS)
    return d * r, r


def _gmlp_fwd(z, ln_g, ln_b, w_s, b_s_full, tb, name):
    R = z.shape[0]
    AW = ln_g.shape[-1]
    AH = w_s.shape[0]

    def body(zu_ref, zv_ref, g_ref, b_ref, ws_ref, bs_ref, o_ref):
        u, _ = _gelu(zu_ref[...].astype(F32))
        v, _ = _gelu(zv_ref[...].astype(F32))
        xh, _ = _ln_stats(v)
        vn = (xh * g_ref[...] + b_ref[...]).astype(BF16)
        for n in range(tb // CHUNK):
            rs = slice(n * CHUNK, (n + 1) * CHUNK)
            for h in range(AH):
                cs = slice(h * CHUNK, (h + 1) * CHUNK)
                v2 = jnp.dot(ws_ref[h].astype(BF16), vn[rs, cs], preferred_element_type=F32) + bs_ref[h]
                o_ref[rs, cs] = (u[rs, cs] * v2).astype(BF16)

    full3 = lambda s: BS(s, lambda i: (0, 0, 0))
    return _pcs(
        body, name=name, out_shape=jax.ShapeDtypeStruct((R, AW), BF16), grid=(R // tb,),
        in_specs=[BS((tb, AW), lambda i: (i, 0)), BS((tb, AW), lambda i: (i, 1)), BS((1, AW), lambda i: (0, 0)),
                  BS((1, AW), lambda i: (0, 0)), full3((AH, CHUNK, CHUNK)), full3((AH, CHUNK, CHUNK))],
        out_specs=BS((tb, AW), lambda i: (i, 0)), compiler_params=_cp(("parallel",)),
    )(z, z, ln_g.reshape(1, AW), ln_b.reshape(1, AW), w_s, b_s_full)


def _gmlp_bwd(z, dya, ln_g, ln_b, w_s, b_s_full, tb, name):
    R = z.shape[0]
    AW = ln_g.shape[-1]
    AH = w_s.shape[0]

    def body(zu_ref, zv_ref, dy_ref, g_ref, b_ref, ws_ref, bs_ref, dzu_ref, dzv_ref, dg_ref, db_ref, dws_ref, dbs_ref, cs_u_ref, cs_v_ref,
             dvn_scr):
        i = pl.program_id(0)
        first = i == 0
        zu = zu_ref[...].astype(F32)
        zv = zv_ref[...].astype(F32)
        u, thu = _gelu(zu)
        v, thv = _gelu(zv)
        xh, r = _ln_stats(v)
        g = g_ref[...]
        vn = (xh * g + b_ref[...]).astype(BF16)
        dy = dy_ref[...].astype(F32)

        @pl.when(first)
        def _():
            dws_ref[...] = jnp.zeros_like(dws_ref)
            dbs_ref[...] = jnp.zeros_like(dbs_ref)

        for n in range(tb // CHUNK):
            rs = slice(n * CHUNK, (n + 1) * CHUNK)
            for h in range(AH):
                cs = slice(h * CHUNK, (h + 1) * CHUNK)
                w = ws_ref[h].astype(BF16)
                v2 = jnp.dot(w, vn[rs, cs], preferred_element_type=F32) + bs_ref[h]
                dzu_ref[rs, cs] = (dy[rs, cs] * v2 * _gelu_grad(zu[rs, cs], thu[rs, cs])).astype(BF16)
                dv2 = dy[rs, cs] * u[rs, cs]
                dv2b = dv2.astype(BF16)
                dvn_scr[rs, cs] = lax.dot_general(w, dv2b, _DIMS["tn"], preferred_element_type=F32)
                dws_ref[h] += lax.dot_general(dv2b, vn[rs, cs], _DIMS["nt"], preferred_element_type=F32)
                dbs_ref[h] += jnp.sum(dv2, axis=1, keepdims=True)
        dvn = dvn_scr[...]
        _accum(dg_ref, jnp.sum(dvn * xh, axis=0, keepdims=True), first)
        _accum(db_ref, jnp.sum(dvn, axis=0, keepdims=True), first)
        t = dvn * g
        dv = r * (t - jnp.mean(t, axis=-1, keepdims=True) - xh * jnp.mean(t * xh, axis=-1, keepdims=True))
        dzv = dv * _gelu_grad(zv, thv)
        dzv_ref[...] = dzv.astype(BF16)
        _accum(cs_v_ref, jnp.sum(dzv, axis=0, keepdims=True), first)
        _accum(cs_u_ref, jnp.sum(dzu_ref[...].astype(F32), axis=0, keepdims=True), first)

    full3 = lambda s: BS(s, lambda i: (0, 0, 0))
    vec = BS((1, AW), lambda i: (0, 0))
    row = BS((tb, AW), lambda i: (i, 0))
    outs = _pcs(
        body, name=name,
        out_shape=(jax.ShapeDtypeStruct((R, AW), BF16), jax.ShapeDtypeStruct((R, AW), BF16), jax.ShapeDtypeStruct((1, AW), F32),
                   jax.ShapeDtypeStruct((1, AW), F32), jax.ShapeDtypeStruct((AH, CHUNK, CHUNK), F32),
                   jax.ShapeDtypeStruct((AH, CHUNK, 1), F32), jax.ShapeDtypeStruct((1, AW), F32), jax.ShapeDtypeStruct((1, AW), F32)),
        grid=(R // tb,),
        in_specs=[row, BS((tb, AW), lambda i: (i, 1)), row, vec, vec, full3((AH, CHUNK, CHUNK)), full3((AH, CHUNK, CHUNK))],
        out_specs=(row, row, vec, vec, full3((AH, CHUNK, CHUNK)), full3((AH, CHUNK, 1)), vec, vec),
        scratch_shapes=[pltpu.VMEM((tb, AW), F32)], compiler_params=_cp(("arbitrary",)),
    )(z, z, dya, ln_g.reshape(1, AW), ln_b.reshape(1, AW), w_s, b_s_full)
    return outs


def _segments(R, L):
    return [(0, L)] + ([(L, R - L)] if R > L else [])


def _scr_rows(R, L):
    return R + CONV_PAD * (len(_segments(R, L)) + 1)


def _scr_off(s, start):
    return CONV_PAD * (s + 1) + start


def _zero_pads(scr, R, L):
    segs = _segments(R, L)
    z = jnp.zeros((CONV_PAD, scr.shape[1]), F32)
    for s, (start, n) in enumerate(segs):
        scr[pl.ds(_scr_off(s, start) - CONV_PAD, CONV_PAD), :] = z
    last_s, (last_start, last_n) = len(segs) - 1, segs[-1]
    scr[pl.ds(_scr_off(last_s, last_start) + last_n, CONV_PAD), :] = z


def _for_chunks(R, L, ch, fn):
    for s, (start, n) in enumerate(_segments(R, L)):
        c = min(ch, n)
        off = _scr_off(s, start)

        def step(i, carry, start=start, off=off, c=c):
            r0 = pl.multiple_of(start + i * c, SUBLANES)
            fn(r0, pl.multiple_of(off + i * c, SUBLANES), c)
            return carry

        lax.fori_loop(0, n // c, step, 0)


def _taps(scr, srow, c, w_ref, ntap, flip):
    win = scr[pl.ds(srow - CONV_PAD, c + 2 * CONV_PAD), :]
    n = c + 2 * CONV_PAD
    acc = None
    for k in range(ntap):
        o = k - (ntap - 1) // 2
        if flip:
            o = -o
        sh = win if o == 0 else pltpu.roll(win, (-o) % n, 0)
        term = w_ref[k:k + 1, :] * sh[CONV_PAD:CONV_PAD + c]
        acc = term if acc is None else acc + term
    return acc


def _tap_grads(scr, srow, c, dy, dw_ref, ntap):
    win = scr[pl.ds(srow - CONV_PAD, c + 2 * CONV_PAD), :]
    n = c + 2 * CONV_PAD
    for k in range(ntap):
        o = k - (ntap - 1) // 2
        sh = win if o == 0 else pltpu.roll(win, (-o) % n, 0)
        dw_ref[k:k + 1, :] += jnp.sum(dy * sh[CONV_PAD:CONV_PAD + c], axis=0, keepdims=True)


def _glu_conv_fwd(z, col0, conv_w, conv_b, L, ch, name):
    R = z.shape[0]
    BW = conv_w.shape[1]
    nb, c0 = BW // LANES, col0 // LANES

    def body(a_ref, g_ref, w_ref, b_ref, o_ref, scr):
        _zero_pads(scr, R, L)

        def fill(r0, s0, c):
            a = a_ref[pl.ds(r0, c), :].astype(F32)
            g = g_ref[pl.ds(r0, c), :].astype(F32)
            scr[pl.ds(s0, c), :] = a * _sigmoid(g)

        _for_chunks(R, L, ch, fill)

        def conv(r0, s0, c):
            o_ref[pl.ds(r0, c), :] = _taps(scr, s0, c, w_ref, B_CONV, False) + b_ref[...]

        _for_chunks(R, L, ch, conv)

    return _pcs(
        body, name=name, out_shape=jax.ShapeDtypeStruct((R, BW), F32), grid=(nb,),
        in_specs=[BS((R, LANES), lambda j: (0, c0 + j)), BS((R, LANES), lambda j: (0, c0 + nb + j)),
                  BS((B_CONV, LANES), lambda j: (0, j)), BS((1, LANES), lambda j: (0, j))],
        out_specs=BS((R, LANES), lambda j: (0, j)), scratch_shapes=[pltpu.VMEM((_scr_rows(R, L), LANES), F32)],
        compiler_params=_cp(("parallel",)),
    )(z, z, conv_w, conv_b.reshape(1, BW))


def _glu_conv_bwd(z, col0, dhc, conv_w, L, ch, name):
    R = z.shape[0]
    BW = conv_w.shape[1]
    nb, c0 = BW // LANES, col0 // LANES

    def body(a_ref, g_ref, dy_ref, w_ref, da_ref, dg_ref, dw_ref, db_ref, csa_ref, csg_ref, scr_h, scr_dy):
        _zero_pads(scr_h, R, L)
        _zero_pads(scr_dy, R, L)
        dw_ref[...] = jnp.zeros_like(dw_ref)
        db_ref[...] = jnp.zeros_like(db_ref)
        csa_ref[...] = jnp.zeros_like(csa_ref)
        csg_ref[...] = jnp.zeros_like(csg_ref)

        def fill(r0, s0, c):
            a = a_ref[pl.ds(r0, c), :].astype(F32)
            g = g_ref[pl.ds(r0, c), :].astype(F32)
            scr_h[pl.ds(s0, c), :] = a * _sigmoid(g)
            scr_dy[pl.ds(s0, c), :] = dy_ref[pl.ds(r0, c), :]

        _for_chunks(R, L, ch, fill)

        def back(r0, s0, c):
            dh = _taps(scr_dy, s0, c, w_ref, B_CONV, True)
            a = a_ref[pl.ds(r0, c), :].astype(F32)
            sg = _sigmoid(g_ref[pl.ds(r0, c), :].astype(F32))
            da = dh * sg
            dg = dh * a * sg * (1.0 - sg)
            da_ref[pl.ds(r0, c), :] = da.astype(BF16)
            dg_ref[pl.ds(r0, c), :] = dg.astype(BF16)
            csa_ref[...] += jnp.sum(da, axis=0, keepdims=True)
            csg_ref[...] += jnp.sum(dg, axis=0, keepdims=True)
            dy = dy_ref[pl.ds(r0, c), :]
            db_ref[...] += jnp.sum(dy, axis=0, keepdims=True)
            _tap_grads(scr_h, s0, c, dy, dw_ref, B_CONV)

        _for_chunks(R, L, ch, back)

    col = BS((R, LANES), lambda j: (0, j))
    vec = BS((1, LANES), lambda j: (0, j))
    nrow = _scr_rows(R, L)
    return _pcs(
        body, name=name,
        out_shape=(jax.ShapeDtypeStruct((R, BW), BF16), jax.ShapeDtypeStruct((R, BW), BF16), jax.ShapeDtypeStruct((B_CONV, BW), F32),
                   jax.ShapeDtypeStruct((1, BW), F32), jax.ShapeDtypeStruct((1, BW), F32), jax.ShapeDtypeStruct((1, BW), F32)),
        grid=(nb,),
        in_specs=[BS((R, LANES), lambda j: (0, c0 + j)), BS((R, LANES), lambda j: (0, c0 + nb + j)), col,
                  BS((B_CONV, LANES), lambda j: (0, j))],
        out_specs=(col, col, BS((B_CONV, LANES), lambda j: (0, j)), vec, vec, vec),
        scratch_shapes=[pltpu.VMEM((nrow, LANES), F32), pltpu.VMEM((nrow, LANES), F32)], compiler_params=_cp(("parallel",)),
    )(z, z, dhc, conv_w)


def _ln_silu_fwd(hc, ln_g, ln_b, tb, name):
    R, W = hc.shape

    def body(x_ref, g_ref, b_ref, o_ref):
        xh, _ = _ln_stats(x_ref[...])
        y = xh * g_ref[...] + b_ref[...]
        o_ref[...] = (y * _sigmoid(y)).astype(BF16)

    vec = BS((1, W), lambda i: (0, 0))
    row = BS((tb, W), lambda i: (i, 0))
    return _pcs(
        body, name=name, out_shape=jax.ShapeDtypeStruct((R, W), BF16), grid=(R // tb,), in_specs=[row, vec, vec], out_specs=row,
        compiler_params=_cp(("parallel",)),
    )(hc, ln_g.reshape(1, W), ln_b.reshape(1, W))


def _ln_silu_bwd(hc, dyb, col0, ln_g, ln_b, tb, name):
    R, W = hc.shape
    c0 = col0 // W

    def body(x_ref, dy_ref, g_ref, b_ref, dx_ref, dg_ref, db_ref):
        i = pl.program_id(0)
        xh, r = _ln_stats(x_ref[...])
        g = g_ref[...]
        y = xh * g + b_ref[...]
        s = _sigmoid(y)
        dy = dy_ref[...].astype(F32) * s * (1.0 + y * (1.0 - s))
        _accum(dg_ref, jnp.sum(dy * xh, axis=0, keepdims=True), i == 0)
        _accum(db_ref, jnp.sum(dy, axis=0, keepdims=True), i == 0)
        t = dy * g
        dx_ref[...] = r * (t - jnp.mean(t, axis=-1, keepdims=True) - xh * jnp.mean(t * xh, axis=-1, keepdims=True))

    vec = BS((1, W), lambda i: (0, 0))
    row = BS((tb, W), lambda i: (i, 0))
    return _pcs(
        body, name=name,
        out_shape=(jax.ShapeDtypeStruct((R, W), F32), jax.ShapeDtypeStruct((1, W), F32), jax.ShapeDtypeStruct((1, W), F32)),
        grid=(R // tb,), in_specs=[row, BS((tb, W), lambda i: (i, c0)), vec, vec], out_specs=(row, vec, vec),
        compiler_params=_cp(("arbitrary",)),
    )(hc, dyb, ln_g.reshape(1, W), ln_b.reshape(1, W))


def _ffn_act_fwd(zf, conv_w, conv_b, L, ch, name):
    R = zf.shape[0]
    DFF = conv_w.shape[1]
    nb = DFF // LANES

    def body(g_ref, u_ref, w_ref, b_ref, o_ref, scr):
        _zero_pads(scr, R, L)

        def fill(r0, s0, c):
            scr[pl.ds(s0, c), :] = g_ref[pl.ds(r0, c), :].astype(F32)

        _for_chunks(R, L, ch, fill)

        def act(r0, s0, c):
            gc = _taps(scr, s0, c, w_ref, FFN_CONV, False) + b_ref[...]
            o_ref[pl.ds(r0, c), :] = (gc * _sigmoid(gc) * u_ref[pl.ds(r0, c), :].astype(F32)).astype(BF16)

        _for_chunks(R, L, ch, act)

    return _pcs(
        body, name=name, out_shape=jax.ShapeDtypeStruct((R, DFF), BF16), grid=(nb,),
        in_specs=[BS((R, LANES), lambda j: (0, j)), BS((R, LANES), lambda j: (0, nb + j)), BS((FFN_CONV, LANES), lambda j: (0, j)),
                  BS((1, LANES), lambda j: (0, j))],
        out_specs=BS((R, LANES), lambda j: (0, j)), scratch_shapes=[pltpu.VMEM((_scr_rows(R, L), LANES), F32)],
        compiler_params=_cp(("parallel",)),
    )(zf, zf, conv_w, conv_b.reshape(1, DFF))


def _ffn_act_bwd(zf, df, conv_w, conv_b, L, ch, name):
    R = zf.shape[0]
    DFF = conv_w.shape[1]
    nb = DFF // LANES

    def body(g_ref, u_ref, df_ref, w_ref, b_ref, dz_ref, dw_ref, db_ref, scr_g, scr_d):
        _zero_pads(scr_g, R, L)
        _zero_pads(scr_d, R, L)
        dw_ref[...] = jnp.zeros_like(dw_ref)
        db_ref[...] = jnp.zeros_like(db_ref)

        def fill(r0, s0, c):
            scr_g[pl.ds(s0, c), :] = g_ref[pl.ds(r0, c), :].astype(F32)

        _for_chunks(R, L, ch, fill)

        def pre(r0, s0, c):
            gc = _taps(scr_g, s0, c, w_ref, FFN_CONV, False) + b_ref[...]
            s = _sigmoid(gc)
            d = df_ref[pl.ds(r0, c), :].astype(F32)
            dz_ref[1, pl.ds(r0, c), :] = (d * gc * s).astype(BF16)
            dgc = d * u_ref[pl.ds(r0, c), :].astype(F32) * s * (1.0 + gc * (1.0 - s))
            scr_d[pl.ds(s0, c), :] = dgc
            db_ref[...] += jnp.sum(dgc, axis=0, keepdims=True)
            _tap_grads(scr_g, s0, c, dgc, dw_ref, FFN_CONV)

        _for_chunks(R, L, ch, pre)

        def back(r0, s0, c):
            dz_ref[0, pl.ds(r0, c), :] = _taps(scr_d, s0, c, w_ref, FFN_CONV, True).astype(BF16)

        _for_chunks(R, L, ch, back)

    col = BS((R, LANES), lambda j: (0, j))
    vec = BS((1, LANES), lambda j: (0, j))
    nrow = _scr_rows(R, L)
    return _pcs(
        body, name=name,
        out_shape=(jax.ShapeDtypeStruct((2, R, DFF), BF16), jax.ShapeDtypeStruct((FFN_CONV, DFF), F32), jax.ShapeDtypeStruct((1, DFF), F32)),
        grid=(nb,),
        in_specs=[col, BS((R, LANES), lambda j: (0, nb + j)), col, BS((FFN_CONV, LANES), lambda j: (0, j)), vec],
        out_specs=(BS((2, R, LANES), lambda j: (0, 0, j)), BS((FFN_CONV, LANES), lambda j: (0, j)), vec),
        scratch_shapes=[pltpu.VMEM((nrow, LANES), F32), pltpu.VMEM((nrow, LANES), F32)], compiler_params=_cp(("parallel",)),
    )(zf, zf, df, conv_w, conv_b.reshape(1, DFF))


def _rope_tables(L, T):
    rows = L // GRID_W
    row = jnp.repeat(jnp.arange(rows, dtype=F32), GRID_W)
    col = jnp.tile(jnp.arange(GRID_W, dtype=F32), rows)
    n_freq = ROPE // 4
    inv = ROPE_THETA ** (-jnp.arange(n_freq, dtype=F32) / n_freq)
    ang = jnp.concatenate([row[:, None] * inv, col[:, None] * inv], axis=-1)
    cos, sin = jnp.cos(ang), jnp.sin(ang)
    half = ROPE // 2
    zero = jnp.zeros((L, half), F32)
    cos_t = jnp.concatenate([cos, cos, jnp.ones((L, LANES - ROPE), F32)], axis=1)
    sa = jnp.concatenate([zero, sin, zero, zero], axis=1)
    sb = jnp.concatenate([-sin, zero, zero, zero], axis=1)
    pad = T - L
    cos_t = jnp.concatenate([cos_t, jnp.ones((pad, LANES), F32)], axis=0)
    sa = jnp.concatenate([sa, jnp.zeros((pad, LANES), F32)], axis=0)
    sb = jnp.concatenate([sb, jnp.zeros((pad, LANES), F32)], axis=0)
    return cos_t, sa, sb


def _rope(x, cos, sa, sb):
    half = ROPE // 2
    return x * cos + pltpu.roll(x, half, 1) * sa + pltpu.roll(x, LANES - half, 1) * sb


def _rope_t(d, cos, sa, sb):
    half = ROPE // 2
    return d * cos + pltpu.roll(d * sa, LANES - half, 1) + pltpu.roll(d * sb, half, 1)


def _mla_prep_fwd(zm, qg, kvg, tabs, tb, name):
    T, W = zm.shape
    QL, KVL = qg.shape[-1], kvg.shape[-1]

    def body(z_ref, qg_ref, kg_ref, cos_ref, sa_ref, sb_ref, q_ref, k_ref, p_ref):
        cq = z_ref[:, :QL]
        r = lax.rsqrt(jnp.mean(cq * cq, axis=-1, keepdims=True) + EPS)
        q_ref[...] = (cq * r * qg_ref[...]).astype(BF16)
        ck = z_ref[:, QL:QL + KVL]
        r = lax.rsqrt(jnp.mean(ck * ck, axis=-1, keepdims=True) + EPS)
        k_ref[...] = (ck * r * kg_ref[...]).astype(BF16)
        p_ref[...] = _rope(z_ref[:, QL + KVL:], cos_ref[...], sa_ref[...], sb_ref[...]).astype(BF16)

    tab = BS((tb, LANES), lambda i: (i, 0))
    return _pcs(
        body, name=name,
        out_shape=(jax.ShapeDtypeStruct((T, QL), BF16), jax.ShapeDtypeStruct((T, KVL), BF16), jax.ShapeDtypeStruct((T, LANES), BF16)),
        grid=(T // tb,),
        in_specs=[BS((tb, W), lambda i: (i, 0)), BS((1, QL), lambda i: (0, 0)), BS((1, KVL), lambda i: (0, 0)), tab, tab, tab],
        out_specs=(BS((tb, QL), lambda i: (i, 0)), BS((tb, KVL), lambda i: (i, 0)), tab), compiler_params=_cp(("parallel",)),
    )(zm, qg.reshape(1, QL), kvg.reshape(1, KVL), *tabs)


def _mla_prep_bwd(zm, qg, kvg, tabs, dq, dk, dp, tb, name):
    T, W = zm.shape
    QL, KVL = qg.shape[-1], kvg.shape[-1]

    def body(z_ref, qg_ref, kg_ref, cos_ref, sa_ref, sb_ref, dq_ref, dk_ref, dp_ref, dz_ref, dqg_ref, dkg_ref):
        i = pl.program_id(0)

        def rms_bwd(x, g, dy):
            r = lax.rsqrt(jnp.mean(x * x, axis=-1, keepdims=True) + EPS)
            xh = x * r
            t = dy * g
            return r * (t - xh * jnp.mean(t * xh, axis=-1, keepdims=True)), jnp.sum(dy * xh, axis=0, keepdims=True)

        dcq, dg = rms_bwd(z_ref[:, :QL], qg_ref[...], dq_ref[...].astype(F32))
        dz_ref[:, :QL] = dcq.astype(BF16)
        _accum(dqg_ref, dg, i == 0)
        dck, dg = rms_bwd(z_ref[:, QL:QL + KVL], kg_ref[...], dk_ref[...].astype(F32))
        dz_ref[:, QL:QL + KVL] = dck.astype(BF16)
        _accum(dkg_ref, dg, i == 0)
        dz_ref[:, QL + KVL:] = _rope_t(dp_ref[...], cos_ref[...], sa_ref[...], sb_ref[...]).astype(BF16)

    tab = BS((tb, LANES), lambda i: (i, 0))
    return _pcs(
        body, name=name,
        out_shape=(jax.ShapeDtypeStruct((T, W), BF16), jax.ShapeDtypeStruct((1, QL), F32), jax.ShapeDtypeStruct((1, KVL), F32)),
        grid=(T // tb,),
        in_specs=[BS((tb, W), lambda i: (i, 0)), BS((1, QL), lambda i: (0, 0)), BS((1, KVL), lambda i: (0, 0)), tab, tab, tab,
                  BS((tb, QL), lambda i: (i, 0)), BS((tb, KVL), lambda i: (i, 0)), tab],
        out_specs=(BS((tb, W), lambda i: (i, 0)), BS((1, QL), lambda i: (0, 0)), BS((1, KVL), lambda i: (0, 0))),
        compiler_params=_cp(("arbitrary",)),
    )(zm, qg.reshape(1, QL), kvg.reshape(1, KVL), *tabs, dq, dk, dp)


def _attn_fwd(q, kv, kpe, tabs, L, tq, name):
    T = kv.shape[0]
    H = kv.shape[1] // QHEAD
    scale = (HEAD + ROPE) ** -0.5

    def body(q_ref, kv_ref, kpe_ref, cos_ref, sa_ref, sb_ref, o_ref, lse_ref, kcat):
        @pl.when(pl.program_id(1) == 0)
        def _():
            kcat[:, :HEAD] = kv_ref[:, :HEAD]
            kcat[:, HEAD:] = kpe_ref[...]

        sub = tq // 2
        for r in range(2):
            rs = slice(r * sub, (r + 1) * sub)
            qp = _rope(q_ref[rs, HEAD:].astype(F32), cos_ref[rs, :], sa_ref[rs, :], sb_ref[rs, :]).astype(BF16)
            qc = jnp.concatenate([q_ref[rs, :HEAD], qp], axis=1)
            s = lax.dot_general(qc, kcat[...], _DIMS["nt"], preferred_element_type=F32)
            m = jnp.max(s, axis=-1, keepdims=True)
            p = jnp.exp2((s - m) * (scale * math.log2(math.e)))
            l = jnp.sum(p, axis=-1, keepdims=True)
            o = jnp.dot(p.astype(BF16), kv_ref[:, HEAD:], preferred_element_type=F32)
            o_ref[rs, :] = (o / l).astype(BF16)
            lse_ref[0, rs, :] = m * scale + jnp.log(l)

    tab = BS((tq, LANES), lambda h, i: (i, 0))
    return _call(
        body, name=name, out_shape=(jax.ShapeDtypeStruct((L, H * HEAD), BF16), jax.ShapeDtypeStruct((H, L, 1), F32)),
        grid=(H, L // tq),
        in_specs=[BS((tq, QHEAD), lambda h, i: (i, h)), BS((T, QHEAD), lambda h, i: (0, h)), BS((T, LANES), lambda h, i: (0, 0)),
                  tab, tab, tab],
        out_specs=(BS((tq, HEAD), lambda h, i: (i, h)), BS((1, tq, 1), lambda h, i: (h, i, 0))),
        scratch=[pltpu.VMEM((T, QHEAD), BF16)], sem=("parallel", "arbitrary"), operands=(q, kv, kpe, *tabs))


def _attn_bwd(q, kv, kpe, tabs, o, lse, do, L, tq, name):
    T = kv.shape[0]
    H = kv.shape[1] // QHEAD
    scale = (HEAD + ROPE) ** -0.5
    nq = L // tq

    def body(q_ref, kv_ref, kpe_ref, cos_ref, sa_ref, sb_ref, o_ref, lse_ref, do_ref, dq_ref, dkv_ref, dkpe_ref, kcat, dk_acc, dv_acc,
             qc_scr, ds_scr, p_scr):
        h, i = pl.program_id(0), pl.program_id(1)

        @pl.when(i == 0)
        def _():
            kcat[:, :HEAD] = kv_ref[:, :HEAD]
            kcat[:, HEAD:] = kpe_ref[...]
            dk_acc[...] = jnp.zeros_like(dk_acc)
            dv_acc[...] = jnp.zeros_like(dv_acc)

        sub = tq // 2
        log2e = math.log2(math.e)
        for r in range(2):
            rs = slice(r * sub, (r + 1) * sub)
            cos, sa, sb = cos_ref[rs, :], sa_ref[rs, :], sb_ref[rs, :]
            qp = _rope(q_ref[rs, HEAD:].astype(F32), cos, sa, sb).astype(BF16)
            qc_scr[rs, :] = jnp.concatenate([q_ref[rs, :HEAD], qp], axis=1)
            s = lax.dot_general(qc_scr[rs, :], kcat[...], _DIMS["nt"], preferred_element_type=F32)
            p = jnp.exp2(s * (scale * log2e) - lse_ref[0, rs, :] * log2e)
            dov = do_ref[rs, :]
            delta = jnp.sum(dov.astype(F32) * o_ref[rs, :].astype(F32), axis=-1, keepdims=True)
            dp = lax.dot_general(dov, kv_ref[:, HEAD:], _DIMS["nt"], preferred_element_type=F32)
            ds_scr[rs, :] = (p * (dp - delta) * scale).astype(BF16)
            p_scr[rs, :] = p.astype(BF16)
            dqc = jnp.dot(ds_scr[rs, :], kcat[...], preferred_element_type=F32)
            dq_ref[rs, :HEAD] = dqc[:, :HEAD].astype(BF16)
            dq_ref[rs, HEAD:] = _rope_t(dqc[:, HEAD:], cos, sa, sb).astype(BF16)
        dk_acc[...] += lax.dot_general(ds_scr[...], qc_scr[...], _DIMS["tn"], preferred_element_type=F32)
        dv_acc[...] += lax.dot_general(p_scr[...], do_ref[...], _DIMS["tn"], preferred_element_type=F32)

        @pl.when(i == nq - 1)
        def _():
            dkv_ref[:, :HEAD] = dk_acc[:, :HEAD].astype(BF16)
            dkv_ref[:, HEAD:] = dv_acc[...].astype(BF16)

            @pl.when(h == 0)
            def _():
                dkpe_ref[...] = dk_acc[:, HEAD:]

            @pl.when(h > 0)
            def _():
                dkpe_ref[...] += dk_acc[:, HEAD:]

    tab = BS((tq, LANES), lambda h, i: (i, 0))
    return _call(
        body, name=name,
        out_shape=(jax.ShapeDtypeStruct((L, H * QHEAD), BF16), jax.ShapeDtypeStruct((T, H * QHEAD), BF16), jax.ShapeDtypeStruct((T, LANES), F32)),
        grid=(H, nq),
        in_specs=[BS((tq, QHEAD), lambda h, i: (i, h)), BS((T, QHEAD), lambda h, i: (0, h)), BS((T, LANES), lambda h, i: (0, 0)),
                  tab, tab, tab, BS((tq, HEAD), lambda h, i: (i, h)), BS((1, tq, 1), lambda h, i: (h, i, 0)),
                  BS((tq, HEAD), lambda h, i: (i, h))],
        out_specs=(BS((tq, QHEAD), lambda h, i: (i, h)), BS((T, QHEAD), lambda h, i: (0, h)), BS((T, LANES), lambda h, i: (0, 0))),
        scratch=[pltpu.VMEM((T, QHEAD), BF16), pltpu.VMEM((T, QHEAD), F32), pltpu.VMEM((T, HEAD), F32),
                 pltpu.VMEM((tq, QHEAD), BF16), pltpu.VMEM((tq, T), BF16), pltpu.VMEM((tq, T), BF16)],
        sem=("arbitrary", "arbitrary"), operands=(q, kv, kpe, *tabs, o, lse, do))


def _adamw(w, g, m, v, name):
    R, C = w.shape
    tr = _row_tile(R, C)
    c1 = 1.0 / (1.0 - ADAM_B1 ** ADAM_STEP)
    c2 = 1.0 / (1.0 - ADAM_B2 ** ADAM_STEP)

    def body(w_ref, g_ref, m_ref, v_ref, d_ref, nm_ref, nv_ref):
        g_ = g_ref[...]
        nm = ADAM_B1 * m_ref[...] + (1.0 - ADAM_B1) * g_
        nv = ADAM_B2 * v_ref[...] + (1.0 - ADAM_B2) * (g_ * g_)
        nm_ref[...] = nm
        nv_ref[...] = nv
        d_ref[...] = -ADAM_LR * ((nm * c1) / (jnp.sqrt(nv * c2) + ADAM_EPS) + ADAM_WD * w_ref[...])

    blk = BS((tr, C), lambda i: (i, 0))
    sd = jax.ShapeDtypeStruct((R, C), F32)
    return _pcs(body, name=name, out_shape=(sd, sd, sd), grid=(R // tr,), in_specs=[blk] * 4, out_specs=(blk,) * 3,
                compiler_params=_cp(("parallel",)))(w, g, m, v)


def _adamw_layers(w, gs, m, v, name):
    n, R, C = w.shape
    tr = _row_tile(R, C)
    nr = R // tr
    c1 = 1.0 / (1.0 - ADAM_B1 ** ADAM_STEP)
    c2 = 1.0 / (1.0 - ADAM_B2 ** ADAM_STEP)

    def body(w_ref, m_ref, v_ref, *rest):
        g_refs, (go_ref, d_ref, nm_ref, nv_ref) = rest[:n], rest[n:]
        layer = pl.program_id(0)
        g_ = g_refs[0][...]
        for j in range(1, n):
            g_ = jnp.where(layer == j, g_refs[j][...], g_)
        nm = ADAM_B1 * m_ref[...] + (1.0 - ADAM_B1) * g_
        nv = ADAM_B2 * v_ref[...] + (1.0 - ADAM_B2) * (g_ * g_)
        go_ref[...] = g_
        nm_ref[...] = nm
        nv_ref[...] = nv
        d_ref[...] = -ADAM_LR * ((nm * c1) / (jnp.sqrt(nv * c2) + ADAM_EPS) + ADAM_WD * w_ref[...])

    def g_spec(j):
        return BS((tr, C), lambda layer, i: (jnp.where(layer == j, i, jnp.where(layer < j, 0, nr - 1)), 0))

    blk = BS((None, tr, C), lambda layer, i: (layer, i, 0))
    sd = jax.ShapeDtypeStruct((n, R, C), F32)
    return _pcs(body, name=name, out_shape=(sd, sd, sd, sd), grid=(n, nr), in_specs=[blk] * 3 + [g_spec(j) for j in range(n)],
                out_specs=(blk,) * 4, compiler_params=_cp(("arbitrary", "arbitrary")))(w, m, v, *gs)


def _sum_lead(a, out_dtype, name):
    n, R, C = a.shape
    tr = _row_tile(R, C * n, 2 << 20)

    def body(a_ref, o_ref):
        acc = a_ref[0].astype(F32)
        for k in range(1, n):
            acc = acc + a_ref[k].astype(F32)
        o_ref[...] = acc.astype(out_dtype)

    return pl.pallas_call(body, name=name, out_shape=jax.ShapeDtypeStruct((R, C), out_dtype), grid=(R // tr,),
                          in_specs=[BS((n, tr, C), lambda i: (0, i, 0))], out_specs=BS((tr, C), lambda i: (i, 0)),
                          compiler_params=_cp(("parallel",)))(a)


def _cctx_grad(parts, c_ctx, name):
    n, D = parts.shape

    def body(p_ref, c_ref, o_ref):
        d = jnp.sum(p_ref[...], axis=0, keepdims=True)
        c = c_ref[...]
        s = _sigmoid(c)
        o_ref[...] = d * s * (1.0 + c * (1.0 - s))

    return pl.pallas_call(body, name=name, out_shape=jax.ShapeDtypeStruct((1, D), F32))(parts, c_ctx.reshape(1, D))


def _me():
    return lax.axis_index("x"), lax.axis_index("y"), lax.axis_index("c")


def _aligned(v, n):
    return v if isinstance(v, int) else pl.multiple_of(v, n)


def _window(ref, r0, c0, R, C):
    rows = pl.ds(_aligned(r0, SUBLANES), R)
    if C == ref.shape[1]:
        return ref.at[rows, :]
    return ref.at[rows, pl.ds(_aligned(c0, LANES), C)]


def _allgather8(items, name):
    n = len(items)

    def body(*refs):
        srcs, dsts = refs[:n], refs[n:2 * n]
        send_sems, recv_sems, local_sems = refs[2 * n:]
        x, y, c = _me()
        me, sibling = (x, y, c), (x, y, 1 - c)
        chips = [(1 - x, y), (x, 1 - y), (1 - x, 1 - y)]

        def dwin(a, blk):
            (R, C), at = items[a][2], items[a][4]
            return _window(dsts[a], *at(*blk), R, C)

        def swin(a):
            (R, C), at = items[a][2], items[a][1]
            return _window(srcs[a], *at(*me), R, C)

        def copy(a, k, blk, to, src=None):
            return pltpu.make_async_remote_copy(
                src_ref=dwin(a, blk) if src is None else src, dst_ref=dwin(a, blk), send_sem=send_sems.at[7 * a + k],
                recv_sem=recv_sems.at[7 * a + k], device_id=to, device_id_type=MESH)

        mine = [pltpu.make_async_copy(swin(a), dwin(a, me), local_sems.at[a]) for a in range(n)]
        for cp in mine:
            cp.start()
        first = []
        for a in range(n):
            first.append(copy(a, 0, me, sibling, src=swin(a)))
            first += [copy(a, 1 + j, me, (*chip, c), src=swin(a)) for j, chip in enumerate(chips)]
        for cp in first:
            cp.start()
        passed = []
        for j, chip in enumerate(chips):
            for a in range(n):
                copy(a, 1 + j, (*chip, c), me).wait_recv()
                fwd = copy(a, 4 + j, (*chip, c), sibling)
                fwd.start()
                passed.append(fwd)
        for a in range(n):
            copy(a, 0, sibling, me).wait_recv()
            for j, chip in enumerate(chips):
                copy(a, 4 + j, (*chip, 1 - c), me).wait_recv()
        for cp in first + passed:
            cp.wait_send()
        for cp in mine:
            cp.wait()

    outs = pl.pallas_call(
        body, name=name, out_shape=tuple(jax.ShapeDtypeStruct(it[3], it[0].dtype) for it in items),
        in_specs=[ANY] * n, out_specs=tuple([ANY] * n),
        scratch_shapes=[pltpu.SemaphoreType.DMA((7 * n,)), pltpu.SemaphoreType.DMA((7 * n,)), pltpu.SemaphoreType.DMA((n,))],
    )(*[it[0] for it in items])
    return list(outs)


def _cast_place(w, kind, name, lead=None):
    R2, C = w.shape[-2:]
    tr = _row_tile(R2, C)
    nr = R2 // tr
    xi, yi, _ = _me()
    p_arr = (2 * xi + yi).astype(jnp.int32).reshape(1)
    if kind == "col":
        shape, o_spec = (R2, 4 * C), BS((tr, C), lambda i, p: (i, p[0]))
    else:
        shape, o_spec = (4 * R2, C), BS((tr, C), lambda i, p: (p[0] * nr + i, 0))

    def body(p_ref, w_ref, o_ref):
        o_ref[...] = w_ref[...].astype(BF16)

    return pl.pallas_call(
        body, name=name, out_shape=jax.ShapeDtypeStruct(shape, BF16),
        grid_spec=pltpu.PrefetchScalarGridSpec(
            num_scalar_prefetch=1, grid=(nr,), out_specs=o_spec,
            in_specs=[BS((tr, C), lambda i, p: (i, 0)) if lead is None else BS((None, tr, C), lambda i, p: (lead, i, 0))]),
        compiler_params=_cp(("parallel",)),
    )(p_arr, w)


def _remote(src, dst, sends, recvs, k, to):
    return pltpu.make_async_remote_copy(src_ref=src, dst_ref=dst, send_sem=sends.at[k], recv_sem=recvs.at[k], device_id=to,
                                        device_id_type=MESH)


def _gather_side(bufs, pieces, passed=(), pass_now=False):
    later = list(passed) + (list(pieces) if pass_now else [])
    base = 3 * len(pieces)

    def win(b, piece, p, pc):
        bi, (R, C), at, k, i = piece
        r0, c0 = at(p, pc)
        return _window(b[bi], r0 + i * (R // k), c0, R // k, C)

    def chips_of(x, y):
        return [(1 - x, y), (x, 1 - y), (1 - x, 1 - y)]

    def over_ici(b, sends, recvs, landing):
        x, y, c = _me()
        cps = []
        for m, piece in enumerate(pieces):
            for j, (px, py) in enumerate(chips_of(x, y)):
                w = win(b, piece, 2 * px + py, c) if landing else win(b, piece, 2 * x + y, c)
                cps.append(_remote(w, w, sends, recvs, 3 * m + j, (px, py, c)))
        return cps

    def to_sibling(b, sends, recvs, todo, landing):
        x, y, c = _me()
        cps = []
        for m, piece in enumerate(later):
            if piece in todo:
                for j, (px, py) in enumerate(chips_of(x, y)):
                    w = win(b, piece, 2 * px + py, 1 - c if landing else c)
                    cps.append(_remote(w, w, sends, recvs, base + 3 * m + j, (x, y, 1 - c)))
        return cps

    def start(b, sends, recvs):
        for cp in over_ici(b, sends, recvs, False) + to_sibling(b, sends, recvs, passed, False):
            cp.start()

    def finish(b, sends, recvs):
        for cp in over_ici(b, sends, recvs, True):
            cp.wait_recv()
        if pass_now:
            for cp in to_sibling(b, sends, recvs, pieces, False):
                cp.start()
        for cp in to_sibling(b, sends, recvs, later, True):
            cp.wait_recv()
        for cp in over_ici(b, sends, recvs, False) + to_sibling(b, sends, recvs, later, False):
            cp.wait_send()

    return _Side(bufs, 3 * (len(pieces) + len(later)), start, finish)


_RELS = [(dx, dy, dc) for dx in (0, 1) for dy in (0, 1) for dc in (0, 1)][1:]


def _rs_side(bufs, pieces):
    def flip(v, d):
        return 1 - v if d else v

    def copies(b, sends, recvs, landing):
        x, y, c = _me()
        dev = 4 * x + 2 * y + c
        cps = []
        for m, (gi, ri, (R, C), at, k, i) in enumerate(pieces):
            rows = R // k
            for t, (dx, dy, dc) in enumerate(_RELS):
                tx, ty, tc = flip(x, dx), flip(y, dy), flip(c, dc)
                if landing:
                    theirs = b[ri].at[4 * tx + 2 * ty + tc, pl.ds(i * rows, rows), :]
                    cps.append(_remote(theirs, theirs, sends, recvs, 7 * m + t, (tx, ty, tc)))
                else:
                    r0, c0 = at(2 * tx + ty, tc)
                    src = _window(b[gi], r0 + i * rows, c0, rows, C)
                    cps.append(_remote(src, b[ri].at[dev, pl.ds(i * rows, rows), :], sends, recvs, 7 * m + t, (tx, ty, tc)))
        return cps

    def start(b, sends, recvs):
        for cp in copies(b, sends, recvs, False):
            cp.start()

    def finish(b, sends, recvs):
        for cp in copies(b, sends, recvs, True):
            cp.wait_recv()
        for cp in copies(b, sends, recvs, False):
            cp.wait_send()

    return _Side(bufs, 7 * len(pieces), start, finish)


def _comm_only(side, name):
    n = len(side.bufs)

    def body(*refs):
        bufs, (sends, recvs) = refs[n:2 * n], refs[2 * n:]
        side.start(bufs, sends, recvs)
        side.finish(bufs, sends, recvs)

    outs = pl.pallas_call(
        body, name=name, out_shape=tuple(jax.ShapeDtypeStruct(b.shape, b.dtype) for b in side.bufs),
        in_specs=[ANY] * n, out_specs=tuple([ANY] * n), input_output_aliases={a: a for a in range(n)},
        scratch_shapes=[pltpu.SemaphoreType.DMA((side.nsem,)), pltpu.SemaphoreType.DMA((side.nsem,))],
    )(*side.bufs)
    return list(outs)


def _rs_sum8(grad, recv, win, kind, name):
    R, C = win
    tr = _row_tile(R, C)
    nr = R // tr
    xi, yi, ci = _me()
    s_arr = jnp.stack([4 * xi + 2 * yi + ci, ci]).astype(jnp.int32)

    if kind == "col":
        g_spec = BS((tr, C), lambda i, s: (s[1] * nr + i, s[0] // 2))
    else:
        g_spec = BS((tr, C), lambda i, s: (s[0] * nr + i, 0))

    def body(s_ref, g_ref, *refs):
        acc = g_ref[...].astype(F32)
        for r_ref in refs[:7]:
            acc = acc + r_ref[0].astype(F32)
        refs[7][...] = acc

    def other(t):
        return BS((1, tr, C), lambda i, s: (jnp.bitwise_xor(s[0], t), i, 0))

    return pl.pallas_call(
        body, name=name, out_shape=jax.ShapeDtypeStruct((2 * R, C), F32),
        grid_spec=pltpu.PrefetchScalarGridSpec(
            num_scalar_prefetch=1, grid=(nr,), in_specs=[g_spec] + [other(t) for t in range(1, 8)],
            out_specs=BS((tr, C), lambda i, s: (s[1] * nr + i, 0))),
        compiler_params=_cp(("parallel",)),
    )(s_arr, grad, *([recv] * 7))


def _share_side(shards):
    def half(b, a, pc):
        R = b[a].shape[0] // 2
        return b[a].at[pl.ds(pl.multiple_of(pc * R, SUBLANES), R), :]

    def start(b, sends, recvs):
        x, y, c = _me()
        for a in range(len(shards)):
            _remote(half(b, a, c), half(b, a, c), sends, recvs, a, (x, y, 1 - c)).start()

    def finish(b, sends, recvs):
        x, y, c = _me()
        for a in range(len(shards)):
            _remote(half(b, a, 1 - c), half(b, a, 1 - c), sends, recvs, a, (x, y, 1 - c)).wait_recv()
        for a in range(len(shards)):
            _remote(half(b, a, c), half(b, a, c), sends, recvs, a, (x, y, 1 - c)).wait_send()

    return _Side(shards, len(shards), start, finish)


BLOB_ALIGN = SUBLANES * LANES


def _pack(arrs):
    flat = jnp.concatenate([a.reshape(-1).astype(F32) for a in arrs])
    n = flat.shape[0]
    padded = -(-n // BLOB_ALIGN) * BLOB_ALIGN
    return jnp.pad(flat, (0, padded - n)).reshape(padded // LANES, LANES)


def _unpack(flat, shapes):
    out, off = [], 0
    for s in shapes:
        n = math.prod(s)
        out.append(flat[..., off:off + n].reshape(flat.shape[:-1] + tuple(s)))
        off += n
    return out


def _gather_blob(blob, name):
    r = blob.shape[0]
    at = lambda px, py, pc: ((4 * px + 2 * py + pc) * r, 0)
    (out,) = _allgather8([(blob, lambda px, py, pc: (0, 0), (r, LANES), (8 * r, LANES), at)], name)
    return out.reshape(8, r * LANES)


def _conv_ffn_fwd(mm, full, layer, X, mods, n2g, conv_w, conv_b, L, tb, nlat, ch, tag):
    sh2, sc2, g2 = mods[3], mods[4], mods[5]
    h2 = _norm_mod_fwd(X, n2g, sh2, sc2, tb, nlat, tag + "_norm2")
    zf = mm(h2, full[f"ffn_w_up{layer}"], "nn", BF16, tag + "_up", tm=(544, 512), tn=(2816, 512))
    f = _ffn_act_fwd(zf, conv_w, conv_b, L, ch, tag + "_act")
    Xn, yf = mm(f, full[f"ffn_w_down{layer}"], "nn", F32, tag + "_down", tm=(544, 512), tn=(512,), res=(X, g2, L))
    return Xn, (X, h2, zf, f, yf)


def _conv_ffn_bwd(mm, full, gbuf, layer, dXn, saved, mods, n2g, conv_w, conv_b, L, tb, nlat, ch, tag):
    X, h2, zf, f, yf = saved
    sc2, g2 = mods[4], mods[5]
    w_up, w_down = full[f"ffn_w_up{layer}"], full[f"ffn_w_down{layer}"]
    dy, dg2 = _gate_bwd(dXn, yf, g2, tb, nlat, tag + "_dres2")
    df = mm(dy, w_down, "nt", BF16, tag + "_ddown_x", tm=(544, 512), tn=(2816, 512))
    gbuf[f"ffn_w_down{layer}"] = mm(f, dy, "tn", BF16, tag + "_ddown_w", tm=(512,), tn=(1024, 512))
    dzf, dcw, dcb = _ffn_act_bwd(zf, df, conv_w, conv_b, L, ch, tag + "_dact")
    gbuf[f"ffn_w_up{layer}"] = mm(h2, dzf, "tn", BF16, tag + "_dup_w", tm=(1024, 512), tn=(512,), outer="i", halves=True)
    dh2 = mm(dzf, w_up, "nt", BF16, tag + "_dup_x", tm=(1088, 1024, 512), tn=(1024,), tk=(2816, 512), halves=True)
    dX, dn2g, dsh2, dsc2 = _norm_mod_bwd(X, n2g, sc2, dh2, dXn, tb, nlat, tag + "_dnorm2")
    return dX, dict(n2g=dn2g, sh2=dsh2, sc2=dsc2, g2=dg2, cw=dcw, cb=dcb)


def kernel(x, c, ctx, c_ctx, norm1_g, norm2_g, w_ada, b_ada, ab_w_in, ab_b_in, a_ln_g, a_ln_b, a_w_s, a_b_s, b_conv_w, b_conv_b, b_ln_g, b_ln_b, ab_w_out, mla_w_in, mla_q_norm_g, mla_w_uq, mla_kv_norm_g, mla_w_ukv, mla_w_o, ffn_w_up, ffn_conv_w, ffn_conv_b, ffn_w_down, final_norm_g, loss_target, m_c_ctx, m_norm1_g, m_norm2_g, m_w_ada, m_b_ada, m_ab_w_in, m_ab_b_in, m_a_ln_g, m_a_ln_b, m_a_w_s, m_a_b_s, m_b_conv_w, m_b_conv_b, m_b_ln_g, m_b_ln_b, m_ab_w_out, m_mla_w_in, m_mla_q_norm_g, m_mla_w_uq, m_mla_kv_norm_g, m_mla_w_ukv, m_mla_w_o, m_ffn_w_up, m_ffn_conv_w, m_ffn_conv_b, m_ffn_w_down, m_final_norm_g, v_c_ctx, v_norm1_g, v_norm2_g, v_w_ada, v_b_ada, v_ab_w_in, v_ab_b_in, v_a_ln_g, v_a_ln_b, v_a_w_s, v_a_b_s, v_b_conv_w, v_b_conv_b, v_b_ln_g, v_b_ln_b, v_ab_w_out, v_mla_w_in, v_mla_q_norm_g, v_mla_w_uq, v_mla_kv_norm_g, v_mla_w_ukv, v_mla_w_o, v_ffn_w_up, v_ffn_conv_w, v_ffn_conv_b, v_ffn_w_down, v_final_norm_g):
    W = dict(c_ctx=c_ctx, norm1_g=norm1_g, norm2_g=norm2_g, w_ada=w_ada, b_ada=b_ada, ab_w_in=ab_w_in, ab_b_in=ab_b_in, a_ln_g=a_ln_g,
             a_ln_b=a_ln_b, a_w_s=a_w_s, a_b_s=a_b_s, b_conv_w=b_conv_w, b_conv_b=b_conv_b, b_ln_g=b_ln_g, b_ln_b=b_ln_b,
             ab_w_out=ab_w_out, mla_w_in=mla_w_in, mla_q_norm_g=mla_q_norm_g, mla_w_uq=mla_w_uq, mla_kv_norm_g=mla_kv_norm_g,
             mla_w_ukv=mla_w_ukv, mla_w_o=mla_w_o, ffn_w_up=ffn_w_up, ffn_conv_w=ffn_conv_w, ffn_conv_b=ffn_conv_b,
             ffn_w_down=ffn_w_down, final_norm_g=final_norm_g)
    MOM = dict(c_ctx=m_c_ctx, norm1_g=m_norm1_g, norm2_g=m_norm2_g, w_ada=m_w_ada, b_ada=m_b_ada, ab_w_in=m_ab_w_in, ab_b_in=m_ab_b_in,
               a_ln_g=m_a_ln_g, a_ln_b=m_a_ln_b, a_w_s=m_a_w_s, a_b_s=m_a_b_s, b_conv_w=m_b_conv_w, b_conv_b=m_b_conv_b,
               b_ln_g=m_b_ln_g, b_ln_b=m_b_ln_b, ab_w_out=m_ab_w_out, mla_w_in=m_mla_w_in, mla_q_norm_g=m_mla_q_norm_g,
               mla_w_uq=m_mla_w_uq, mla_kv_norm_g=m_mla_kv_norm_g, mla_w_ukv=m_mla_w_ukv, mla_w_o=m_mla_w_o, ffn_w_up=m_ffn_w_up,
               ffn_conv_w=m_ffn_conv_w, ffn_conv_b=m_ffn_conv_b, ffn_w_down=m_ffn_w_down, final_norm_g=m_final_norm_g)
    VAR = dict(c_ctx=v_c_ctx, norm1_g=v_norm1_g, norm2_g=v_norm2_g, w_ada=v_w_ada, b_ada=v_b_ada, ab_w_in=v_ab_w_in, ab_b_in=v_ab_b_in,
               a_ln_g=v_a_ln_g, a_ln_b=v_a_ln_b, a_w_s=v_a_w_s, a_b_s=v_a_b_s, b_conv_w=v_b_conv_w, b_conv_b=v_b_conv_b,
               b_ln_g=v_b_ln_g, b_ln_b=v_b_ln_b, ab_w_out=v_ab_w_out, mla_w_in=v_mla_w_in, mla_q_norm_g=v_mla_q_norm_g,
               mla_w_uq=v_mla_w_uq, mla_kv_norm_g=v_mla_kv_norm_g, mla_w_ukv=v_mla_w_ukv, mla_w_o=v_mla_w_o, ffn_w_up=v_ffn_w_up,
               ffn_conv_w=v_ffn_conv_w, ffn_conv_b=v_ffn_conv_b, ffn_w_down=v_ffn_w_down, final_norm_g=v_final_norm_g)
    ORDER = list(W.keys())

    L, D = x.shape[1], x.shape[2]
    CT = ctx.shape[1]
    T = L + CT
    AW, BW = a_ln_g.shape[-1], b_ln_g.shape[-1]
    AH = a_w_s.shape[1]
    QL, KVL = 4 * mla_q_norm_g.shape[-1], 4 * mla_kv_norm_g.shape[-1]
    H = 4 * mla_w_o.shape[1] // HEAD
    HS = H // 4
    DFF = ffn_conv_b.shape[-1]
    NA = w_ada.shape[-1]
    tb = 256 if (L % 256 == 0 and CT % 256 == 0) else 128
    nlat = L // tb
    ch = tb
    tq = 256 if L >= 512 else 128
    xi, yi, ci = _me()
    p_me = 2 * xi + yi
    dev = 4 * xi + 2 * yi + ci

    shard_small = [c[0], mla_q_norm_g[0], mla_kv_norm_g[0], b_conv_w[0], ffn_conv_w]
    g0 = _gather_blob(_pack(shard_small), "gather_small")
    c_all, qg_s, kvg_s, bcw_s, fcw_s = _unpack(g0, [a.shape for a in shard_small])
    per_chip = lambda a: a[0::2]
    qg = per_chip(qg_s).reshape(QL)
    kvg = per_chip(kvg_s).reshape(KVL)
    bcw = jnp.concatenate(list(per_chip(bcw_s)), axis=-1)
    fcw = jnp.concatenate(list(per_chip(fcw_s)), axis=-1)
    c16 = jnp.concatenate([c_all, c_ctx[None], jnp.zeros((7, D), F32)], axis=0)

    ms = []
    for i in range(2):
        bias = lax.dynamic_slice(b_ada[i], (p_me * NA,), (NA,))
        ms.append(_mm(c16, w_ada, "nn", F32, f"ada{i}", tm=(16,), tn=(512,), bias=bias, silu_a=True, b_lead=i))
    ms = jnp.concatenate(ms, axis=0)
    (mods_all,) = _allgather8(
        [(ms, lambda px, py, pc: (pc * 16, 0), (16, NA), (32, 4 * NA), lambda px, py, pc: (pc * 16, (2 * px + py) * NA))], "gather_mods")
    mods_all = mods_all.reshape(2, 16, N_MOD, D)
    mods = []
    for i in range(2):
        lat = lax.dynamic_index_in_dim(mods_all[i], dev, axis=0, keepdims=False)
        both = jnp.stack([lat, mods_all[i, 8]], axis=0)
        mods.append([both[:, k][:, None, :] for k in range(N_MOD)])

    def pad_uq(w):
        w = w.reshape(w.shape[0], HS, HEAD + ROPE)
        return jnp.pad(w, ((0, 0), (0, 0), (0, QHEAD - HEAD - ROPE))).reshape(w.shape[0], HS * QHEAD)

    MI = QL + KVL + LANES
    big = [
        ("ab_w_in", ab_w_in, 0, "col"), ("ab_w_out", ab_w_out, 0, "row"),
        ("mla_w_in", jnp.pad(mla_w_in[0], ((0, 0), (0, MI - mla_w_in.shape[-1]))), None, "row"),
        ("mla_w_uq", pad_uq(mla_w_uq[0]), None, "col"), ("mla_w_ukv", mla_w_ukv, 0, "col"), ("mla_w_o", mla_w_o, 0, "row"),
        ("ffn_w_up0", ffn_w_up, 0, "col"), ("ffn_w_up1", ffn_w_up, 1, "col"),
        ("ffn_w_down0", ffn_w_down, 0, "row"), ("ffn_w_down1", ffn_w_down, 1, "row"),
    ]
    full, wins = {}, {}
    for nm, w, lead, kind in big:
        R, C = w.shape[-2] // 2, w.shape[-1]
        if kind == "col":
            wins[nm] = ((R, C), lambda p, pc, R=R, C=C: (pc * R, p * C), kind)
        else:
            wins[nm] = ((R, C), lambda p, pc, R=R, C=C: ((2 * p + pc) * R, 0), kind)
        full[nm] = _cast_place(w, kind, "cast_" + nm, lead)
    names = [b[0] for b in big]

    up0 = lambda *parts: [("ffn_w_up0", 8, i) for i in parts]
    mla_a = [("ffn_w_down0", 1, 0), ("mla_w_in", 1, 0), ("mla_w_uq", 1, 0)]
    mla_b = [("mla_w_ukv", 1, 0), ("mla_w_o", 1, 0)]
    gather_plan = {
        "l0_norm1": ([("ab_w_out", 1, 0)], []), "l0_in": (up0(0, 1), [("ab_w_out", 1, 0)]), "l0_gmlp": (up0(2), up0(0, 1)),
        "l0_conv": (up0(3, 4), up0(2)), "l0_lnsilu": (up0(5), up0(3, 4)), "l0_out": (up0(6, 7), up0(5)),
        "l0_ffn_norm2": ([], up0(6, 7)), "l0_ffn_up": (mla_a, []), "l0_ffn_act": ([], mla_a), "l0_ffn_down": (mla_b, []),
        "l1_norm1": ([], mla_b),
    }
    u0 = lambda *parts: [("ffn_w_up0", 16, i) for i in parts]
    reduce_plan = {
        "l1_ffn_dact": [("ffn_w_down1", 2, 0)], "l1_ffn_dup_w": [("ffn_w_down1", 2, 1)], "l1_ffn_dup_x": [("ffn_w_up1", 4, 0)],
        "l1_dattn": [("ffn_w_up1", 4, 1), ("ffn_w_up1", 4, 2), ("ffn_w_up1", 4, 3), ("mla_w_o", 1, 0)],
        "l0_ffn_ddown_x": [("mla_w_ukv", 1, 0), ("mla_w_uq", 1, 0)], "l0_ffn_ddown_w": [("mla_w_in", 1, 0)],
        "l0_ffn_dact": [("ffn_w_down0", 2, 0)], "l0_ffn_dup_w": [("ffn_w_down0", 2, 1)],
        "l0_ffn_dup_x": u0(0, 1, 2, 3, 4, 5, 6), "l0_ffn_dnorm2": u0(7), "l0_dres1": u0(8), "l0_dout_w": u0(9),
        "l0_dout_x": [("ab_w_out", 2, 0)], "l0_dgmlp": u0(10, 11), "l0_dlnsilu": u0(12),
        "l0_dconv": u0(13, 14) + [("ab_w_out", 2, 1)], "l0_din_w": u0(15),
        "l0_din_x": [("ab_w_in", 4, 0), ("ab_w_in", 4, 1)], "l0_dnorm1": [("ab_w_in", 4, 2), ("ab_w_in", 4, 3)],
    }
    gbuf, rbuf = {}, {}

    def gather_side(req, passed=(), pass_now=False):
        nms = list(dict.fromkeys(nm for nm, _, _ in list(req) + list(passed)))
        piece = lambda nm, k, i: (nms.index(nm), wins[nm][0], wins[nm][1], k, i)
        side = _gather_side([full[nm] for nm in nms], [piece(*r) for r in req], [piece(*r) for r in passed], pass_now)

        def commit(bufs):
            full.update(zip(nms, bufs))

        return side, commit

    def reduce_side(req):
        nms = list(dict.fromkeys(nm for nm, _, _ in req))
        for nm in nms:
            if nm not in rbuf:
                rbuf[nm] = lax.empty((8,) + wins[nm][0], BF16)
        n = len(nms)
        side = _rs_side([gbuf[nm] for nm in nms] + [rbuf[nm] for nm in nms],
                        [(nms.index(nm), n + nms.index(nm), wins[nm][0], wins[nm][1], k, i) for nm, k, i in req])

        def commit(bufs):
            gbuf.update(zip(nms, bufs[:n]))
            rbuf.update(zip(nms, bufs[n:]))

        return side, commit

    _carry_plan.clear()
    _carry_plan.update({name: functools.partial(gather_side, req, passed) for name, (req, passed) in gather_plan.items()})
    _carry_plan["l1_attn"] = functools.partial(gather_side, [("ffn_w_up1", 1, 0), ("ffn_w_down1", 1, 0)], (), True)
    _carry_plan.update({name: functools.partial(reduce_side, req) for name, req in reduce_plan.items()})
    mm, attn_fwd, attn_bwd = _mm, _attn_fwd, _attn_bwd

    side, commit = gather_side([("ab_w_in", 1, 0)], (), True)
    commit(_comm_only(side, "gather_first"))

    X0 = jnp.concatenate([x[0], ctx[0]], axis=0)
    m0, m1 = mods[0], mods[1]
    h1 = _norm_mod_fwd(X0, norm1_g[0], m0[0], m0[1], tb, nlat, "l0_norm1")
    z = mm(h1, full["ab_w_in"], "nn", BF16, "l0_in", tm=(544, 512), tn=(1024, 512), bias=ab_b_in[0])
    bs_full = jnp.broadcast_to(a_b_s[0][:, :, None], (AH, CHUNK, CHUNK))
    ya = _gmlp_fwd(z, a_ln_g[0], a_ln_b[0], a_w_s[0], bs_full, tb, "l0_gmlp")
    hc = _glu_conv_fwd(z, 2 * AW, bcw, b_conv_b[0], L, ch, "l0_conv")
    yb = _ln_silu_fwd(hc, b_ln_g[0], b_ln_b[0], tb, "l0_lnsilu")
    yab = jnp.concatenate([ya, yb], axis=1)
    X1, y0 = mm(yab, full["ab_w_out"], "nn", F32, "l0_out", tm=(544, 512), tn=(1024, 512), res=(X0, m0[2], L))
    X2, ffn0 = _conv_ffn_fwd(mm, full, 0, X1, m0, norm2_g[0], fcw[0], ffn_conv_b[0], L, tb, nlat, ch, "l0_ffn")

    tabs = _rope_tables(L, T)
    hm = _norm_mod_fwd(X2, norm1_g[1], m1[0], m1[1], tb, nlat, "l1_norm1")
    zm = mm(hm, full["mla_w_in"], "nn", F32, "l1_in", tm=(544, 512), tn=(MI,))
    cqn, ckvn, kpe = _mla_prep_fwd(zm, qg, kvg, tabs, tb, "l1_prep")
    q = mm(cqn, full["mla_w_uq"], "nn", BF16, "l1_uq", tm=(512,), tn=(1024, 512), rows=L)
    kvh = mm(ckvn, full["mla_w_ukv"], "nn", BF16, "l1_ukv", tm=(544, 512), tn=(1024, 512))
    tq2 = 2 * tq if L % (2 * tq) == 0 and L > 2 * tq else tq
    o, lse = attn_fwd(q, kvh, kpe, tabs, L, tq2, "l1_attn")
    m1_lat = [a[:1] for a in m1]
    X3, yl = mm(o, full["mla_w_o"], "nn", F32, "l1_o", tm=(512,), tn=(1024, 512), res=(X2, m1_lat[2], L))
    X4, ffn1 = _conv_ffn_fwd(mm, full, 1, X3, m1_lat, norm2_g[1], fcw[1], ffn_conv_b[1], L, tb, nlat, ch, "l1_ffn")
    loss_acc, dX4, dfinal = _final_loss(X4, final_norm_g, loss_target[0], tb, "loss")

    dX3, gf1 = _conv_ffn_bwd(mm, full, gbuf, 1, dX4, ffn1, m1_lat, norm2_g[1], fcw[1], ffn_conv_b[1], L, tb, nlat, ch, "l1_ffn")
    dyl, dg1_1 = _gate_bwd(dX3, yl, m1_lat[2], tb, nlat, "l1_dres1")
    do = mm(dyl, full["mla_w_o"], "nt", BF16, "l1_do_x", tm=(512,), tn=(1024, 512))
    gbuf["mla_w_o"] = mm(o, dyl, "tn", BF16, "l1_do_w", tm=(512,), tn=(1024, 512))
    dq, dkv, dkpe = attn_bwd(q, kvh, kpe, tabs, o, lse, do, L, tq2, "l1_dattn")
    dckvn = mm(dkv, full["mla_w_ukv"], "nt", BF16, "l1_dukv_x", tm=(544, 512), tn=(KVL,), tk=(2048, 512))
    gbuf["mla_w_ukv"] = mm(ckvn, dkv, "tn", BF16, "l1_dukv_w", tm=(512,), tn=(1024, 512))
    dcqn = mm(dq, full["mla_w_uq"], "nt", BF16, "l1_duq_x", tm=(512,), tn=(QL,), tk=(2048, 512))
    gbuf["mla_w_uq"] = mm(cqn, dq, "tn", BF16, "l1_duq_w", tm=(QL,), tn=(1024, 512), rows=L)
    dcqn = jnp.concatenate([dcqn, jnp.zeros((CT, QL), BF16)], axis=0)
    dzm, dqg, dkvg = _mla_prep_bwd(zm, qg, kvg, tabs, dcqn, dckvn, dkpe, tb, "l1_dprep")
    dhm = mm(dzm, full["mla_w_in"], "nt", BF16, "l1_din_x", tm=(544, 512), tn=(1024, 512))
    gbuf["mla_w_in"] = mm(hm, dzm, "tn", BF16, "l1_din_w", tm=(512,), tn=(MI,))
    dX2, dn1g_1, dsh1_1, dsc1_1 = _norm_mod_bwd(X2, norm1_g[1], m1[1], dhm, dX3, tb, nlat, "l1_dnorm1")

    dX1, gf0 = _conv_ffn_bwd(mm, full, gbuf, 0, dX2, ffn0, m0, norm2_g[0], fcw[0], ffn_conv_b[0], L, tb, nlat, ch, "l0_ffn")
    dy0, dg1_0 = _gate_bwd(dX1, y0, m0[2], tb, nlat, "l0_dres1")
    gbuf["ab_w_out"] = mm(yab, dy0, "tn", BF16, "l0_dout_w", tm=(512,), tn=(1024, 512))
    dyab = mm(dy0, full["ab_w_out"], "nt", BF16, "l0_dout_x", tm=(544, 512), tn=(1024, 512))
    dzu, dzv, dlnag, dlnab, dws, dbs, csu, csv = _gmlp_bwd(z, dyab, a_ln_g[0], a_ln_b[0], a_w_s[0], bs_full, tb, "l0_dgmlp")
    dhc, dlnbg, dlnbb = _ln_silu_bwd(hc, dyab, AW, b_ln_g[0], b_ln_b[0], tb, "l0_dlnsilu")
    dza, dzg, dbcw, dbcb, csa, csg = _glu_conv_bwd(z, 2 * AW, dhc, bcw, L, ch, "l0_dconv")
    dz = jnp.concatenate([dzu, dzv, dza, dzg], axis=1)
    dbin = jnp.concatenate([csu, csv, csa, csg], axis=1)
    gbuf["ab_w_in"] = mm(h1, dz, "tn", BF16, "l0_din_w", tm=(512,), tn=(1024, 512))
    dh1 = mm(dz, full["ab_w_in"], "nt", BF16, "l0_din_x", tm=(544, 512), tn=(1024, 512), tk=(2048, 512))
    dX0, dn1g_0, dsh1_0, dsc1_0 = _norm_mod_bwd(X0, norm1_g[0], m0[1], dh1, dX1, tb, nlat, "l0_dnorm1")
    grad_x = dX0[:L][None]

    halves = [_rs_sum8(gbuf[nm], rbuf[nm], wins[nm][0], wins[nm][2], "rs_sum_" + nm) for nm in names]

    def grp6(l, sh1, sc1, g1, f):
        G = sh1.shape[0]
        pad = lambda a: jnp.concatenate([a, jnp.zeros((G - a.shape[0],) + a.shape[1:], F32)], axis=0) if a.shape[0] < G else a
        return jnp.concatenate([pad(a) for a in (sh1, sc1, g1, f["sh2"], f["sc2"], f["g2"])], axis=1)

    dm0 = grp6(0, dsh1_0, dsc1_0, dg1_0, gf0)
    dm1 = grp6(1, dsh1_1, dsc1_1, dg1_1, gf1)
    dmods = jnp.stack([dm0, dm1], axis=0)
    small = [
        jnp.concatenate([dn1g_0, dn1g_1], axis=0), jnp.concatenate([gf0["n2g"], gf1["n2g"]], axis=0), dbin, dlnag, dlnab, dws,
        dbs, dbcw, dbcb, dlnbg, dlnbb, dqg, dkvg, jnp.stack([gf0["cw"], gf1["cw"]], axis=0),
        jnp.concatenate([gf0["cb"], gf1["cb"]], axis=0), dfinal, dmods[:, 1],
    ]
    small_shapes = [a.shape for a in small]
    lat_shape = dmods[:, 0].shape
    blob = _pack(small + [dmods[:, 0], loss_acc[0, :1]])
    gathered = _gather_blob(blob, "gather_grads")
    summed = _sum_lead(gathered.reshape(8, -1, LANES), F32, "sum_grads").reshape(-1)
    (dn1g, dn2g, dbin_s, dlnag_s, dlnab_s, dws_s, dbs_s, dbcw_s, dbcb_s, dlnbg_s, dlnbb_s, dqg_s, dkvg_s, dfcw_s, dfcb_s, dfinal_s,
     dmods_ctx, dmods_lat_sum, loss) = _unpack(summed, small_shapes + [lat_shape, (1,)])
    loss = loss.reshape(())
    n_small = sum(math.prod(s) for s in small_shapes)
    dmods_lat = gathered[:, n_small:n_small + math.prod(lat_shape)].reshape((8,) + lat_shape)

    grad_w_ada, dc_parts = [], []
    for i in range(2):
        dm16 = jnp.concatenate([dmods_lat[:, i].reshape(8, N_MOD * D), dmods_ctx[i].reshape(1, N_MOD * D),
                                jnp.zeros((7, N_MOD * D), F32)], axis=0)
        dm16_s = lax.dynamic_slice(dm16, (0, p_me * NA), (16, NA))
        grad_w_ada.append(_mm(c16, dm16_s, "tn", F32, f"dada{i}_w", tm=(1024, 512), tn=(1024, 512), silu_a=True))
        dc_parts.append(_mm(dm16_s, w_ada, "nt", F32, f"dada{i}_c", tm=(16,), tn=(512,), tk=(1024, 512), b_lead=i))
    grad_b_ada = dmods_lat_sum.reshape(2, N_MOD * D) + dmods_ctx.reshape(2, N_MOD * D)
    dc_blob = _pack([dc_parts[0][8] + dc_parts[1][8]])
    dc_all = _gather_blob(dc_blob, "gather_dc")[0::2, :D]
    grad_c_ctx = _cctx_grad(dc_all, c_ctx, "dcctx").reshape(D)

    gshard = dict(zip(names, _comm_only(_share_side(halves), "rs_share")))
    two = lambda a: a.reshape(-1, a.shape[-1])
    ada_update = _adamw_layers(w_ada, grad_w_ada, m_w_ada, v_w_ada, "adamw_w_ada")

    def my_cols(a, axis, n):
        return lax.dynamic_slice_in_dim(a, p_me * n, n, axis=axis)

    unpad_uq = lambda g: g.reshape(QL, HS, QHEAD)[:, :, :HEAD + ROPE].reshape(QL, HS * (HEAD + ROPE))
    grads = dict(
        c_ctx=grad_c_ctx, norm1_g=dn1g, norm2_g=dn2g, b_ada=grad_b_ada, ab_w_in=gshard["ab_w_in"][None],
        ab_b_in=dbin_s, a_ln_g=dlnag_s, a_ln_b=dlnab_s, a_w_s=dws_s[None], a_b_s=dbs_s.reshape(1, AH, CHUNK),
        b_conv_w=my_cols(dbcw_s, 1, BW // 4)[None], b_conv_b=dbcb_s, b_ln_g=dlnbg_s, b_ln_b=dlnbb_s,
        ab_w_out=gshard["ab_w_out"][None], mla_w_in=gshard["mla_w_in"][:, :mla_w_in.shape[-1]][None],
        mla_q_norm_g=my_cols(dqg_s, 1, QL // 4), mla_w_uq=unpad_uq(gshard["mla_w_uq"])[None],
        mla_kv_norm_g=my_cols(dkvg_s, 1, KVL // 4), mla_w_ukv=gshard["mla_w_ukv"][None], mla_w_o=gshard["mla_w_o"][None],
        ffn_conv_w=my_cols(dfcw_s, 2, DFF // 4), ffn_conv_b=dfcb_s, final_norm_g=dfinal_s.reshape(D),
    )
    LAYERED = ("w_ada", "ffn_w_up", "ffn_w_down")
    grads = {k: grads[k].reshape(W[k].shape) for k in ORDER if k not in LAYERED}

    BIG = ("ab_w_in", "ab_w_out", "mla_w_in", "mla_w_uq", "mla_w_ukv", "mla_w_o") + LAYERED
    delta, new_m, new_v = {}, {}, {}
    for k in BIG:
        if k == "w_ada":
            grads[k], delta[k], new_m[k], new_v[k] = ada_update
        elif k in LAYERED:
            grads[k], delta[k], new_m[k], new_v[k] = _adamw_layers(W[k], [gshard[k + "0"], gshard[k + "1"]], MOM[k], VAR[k], "adamw_" + k)
        else:
            upd = _adamw(two(W[k]), two(grads[k]), two(MOM[k]), two(VAR[k]), "adamw_" + k)
            delta[k], new_m[k], new_v[k] = (a.reshape(W[k].shape) for a in upd)
    SMALL = [k for k in ORDER if k not in BIG]
    small_upd = _adamw(_pack([W[k] for k in SMALL]), _pack([grads[k] for k in SMALL]), _pack([MOM[k] for k in SMALL]),
                       _pack([VAR[k] for k in SMALL]), "adamw_small")
    shapes = [W[k].shape for k in SMALL]
    for k, d_k, m_k, v_k in zip(SMALL, *[_unpack(a.reshape(-1), shapes) for a in small_upd]):
        delta[k], new_m[k], new_v[k] = d_k, m_k, v_k

    return (loss, grad_x, *[grads[k] for k in ORDER], *[delta[k] for k in ORDER], *[new_m[k] for k in ORDER],
            *[new_v[k] for k in ORDER])
```

```python
import functools
import math

import jax
import jax.numpy as jnp
from jax import lax
from jax.experimental import pallas as pl
from jax.experimental.pallas import tpu as pltpu

F32 = jnp.float32
BF16 = jnp.bfloat16
MESH = pl.DeviceIdType.MESH
ANY = pl.BlockSpec(memory_space=pl.ANY)

EPS = 1e-6
N_MOD = 6
CHUNK = 128
HEAD = 128
ROPE = 64
QHEAD = 2 * HEAD
GRID_W = 64
ROPE_THETA = 10000.0
B_CONV = 31
FFN_CONV = 3
ADAM_LR, ADAM_B1, ADAM_B2, ADAM_EPS, ADAM_WD, ADAM_STEP = 0.001, 0.9, 0.999, 1e-08, 0.01, 10

V7X_VMEM_LIMIT = 56 * 1024 * 1024
LANES = 128
SUBLANES = 8
CONV_PAD = 16

BS = pl.BlockSpec


def _cp(sem=None, vmem=V7X_VMEM_LIMIT):
    return pltpu.CompilerParams(dimension_semantics=sem, vmem_limit_bytes=vmem)


class _Side:
    def __init__(self, bufs, nsem, start, finish):
        self.bufs, self.nsem, self.start, self.finish = list(bufs), nsem, start, finish


_carry_plan = {}


def _call(body, *, name, out_shape, grid, in_specs, out_specs, operands, sem, scratch=()):
    if name not in _carry_plan:
        return pl.pallas_call(body, name=name, out_shape=out_shape, grid=grid, in_specs=in_specs, out_specs=out_specs,
                              scratch_shapes=list(scratch), compiler_params=_cp(sem))(*operands)
    side, commit = _carry_plan[name]()
    multi = isinstance(out_shape, (tuple, list))
    outs = tuple(out_shape) if multi else (out_shape,)
    ospecs = tuple(out_specs) if multi else (out_specs,)
    n_in, n_out, n_scr, n_buf = len(in_specs), len(outs), len(scratch), len(side.bufs)

    def body2(*refs):
        o0 = n_in + n_buf
        s0 = o0 + n_out + n_buf
        bufs = refs[o0 + n_out:s0]
        sends, recvs = refs[s0 + n_scr:]
        first = last = None
        for d, g in enumerate(grid):
            pid = pl.program_id(d)
            first = (pid == 0) if first is None else jnp.logical_and(first, pid == 0)
            last = (pid == g - 1) if last is None else jnp.logical_and(last, pid == g - 1)

        @pl.when(first)
        def _():
            side.start(bufs, sends, recvs)

        body(*refs[:n_in], *refs[o0:o0 + n_out], *refs[s0:s0 + n_scr])

        @pl.when(last)
        def _():
            side.finish(bufs, sends, recvs)

    res = pl.pallas_call(
        body2, name=name, out_shape=outs + tuple(jax.ShapeDtypeStruct(b.shape, b.dtype) for b in side.bufs), grid=grid,
        in_specs=list(in_specs) + [ANY] * n_buf, out_specs=ospecs + (ANY,) * n_buf,
        scratch_shapes=list(scratch) + [pltpu.SemaphoreType.DMA((side.nsem,)), pltpu.SemaphoreType.DMA((side.nsem,))],
        input_output_aliases={n_in + i: n_out + i for i in range(n_buf)}, compiler_params=_cp(("arbitrary",) * len(grid)),
    )(*operands, *side.bufs)
    commit(list(res[n_out:]))
    return tuple(res[:n_out]) if multi else res[0]


def _pcs(body, *, name, out_shape, grid, in_specs, out_specs, compiler_params, scratch_shapes=()):
    def run(*operands):
        return _call(body, name=name, out_shape=out_shape, grid=grid, in_specs=in_specs, out_specs=out_specs, operands=operands,
                     sem=compiler_params.dimension_semantics, scratch=scratch_shapes)
    return run


def _pick(n, prefs):
    for p in prefs:
        if p <= n and n % p == 0:
            return p
    return n


def _row_tile(rows, cols, target_bytes=1 << 20):
    best = None
    for d in range(2 * SUBLANES, rows + 1, 2 * SUBLANES):
        if rows % d == 0 and d * cols * 4 <= target_bytes:
            best = d
    return best if best is not None else rows


def _sigmoid(x):
    return 1.0 / (1.0 + jnp.exp(-x))


def _gelu(x):
    c = math.sqrt(2.0 / math.pi)
    th = jnp.tanh(c * (x + 0.044715 * x * x * x))
    return 0.5 * x * (1.0 + th), th


def _gelu_grad(x, th):
    c = math.sqrt(2.0 / math.pi)
    return 0.5 * (1.0 + th) + 0.5 * x * (1.0 - th * th) * c * (1.0 + 3.0 * 0.044715 * x * x)


_DIMS = {"nn": (((1,), (0,)), ((), ())), "nt": (((1,), (1,)), ((), ())), "tn": (((0,), (0,)), ((), ()))}


def _mm(a, b, mode, out_dtype, name, tm=(512,), tn=(512,), tk=(100000,), bias=None, silu_a=False, rows=None, outer="j", b_lead=None,
        halves=False, res=None):
    bshape = b.shape[-2:]
    if mode == "nn":
        (M, K), N = a.shape, bshape[1]
    elif mode == "nt":
        (M, K), N = a.shape[-2:], bshape[0]
        K = 2 * K if halves else K
    else:
        (K, M), N = a.shape, bshape[1]
        N = 2 * N if halves else N
    if rows is not None:
        if mode == "tn":
            K = rows
        else:
            M = rows
    tm, tn, tk = _pick(M, tm), _pick(N, tn), _pick(K, tk)
    gm, gn, gk = M // tm, N // tn, K // tk

    def ij(g0, g1):
        return (g1, g0) if outer == "j" else (g0, g1)

    if mode == "nn":
        a_spec = BS((tm, tk), lambda g0, g1, k: (ij(g0, g1)[0], k))
        b_spec = BS((tk, tn), lambda g0, g1, k: (k, ij(g0, g1)[1]))
    elif mode == "nt":
        a_spec = BS((tm, tk), lambda g0, g1, k: (ij(g0, g1)[0], k))
        b_spec = BS((tn, tk), lambda g0, g1, k: (ij(g0, g1)[1], k))
    else:
        a_spec = BS((tk, tm), lambda g0, g1, k: (k, ij(g0, g1)[0]))
        b_spec = BS((tk, tn), lambda g0, g1, k: (k, ij(g0, g1)[1]))
    if halves and mode == "nt":
        per = K // 2 // tk
        a_spec = BS((None, tm, tk), lambda g0, g1, k: (k // per, ij(g0, g1)[0], k % per))
    if halves and mode == "tn":
        per = N // 2 // tn
        b_spec = BS((None, tk, tn), lambda g0, g1, k: (ij(g0, g1)[1] // per, k, ij(g0, g1)[1] % per))
    if b_lead is not None:
        blk, at = b_spec.block_shape, b_spec.index_map
        b_spec = BS((None,) + tuple(blk), lambda g0, g1, k: (b_lead,) + tuple(at(g0, g1, k)))
    in_specs = [a_spec, b_spec]
    operands = [a, b]
    if bias is not None:
        in_specs.append(BS((1, tn), lambda g0, g1, k: (0, ij(g0, g1)[1])))
        operands.append(bias.reshape(1, N))
    o_spec = BS((tm, tn), lambda g0, g1, k: ij(g0, g1))
    out_shape, out_specs = jax.ShapeDtypeStruct((M, N), out_dtype), o_spec
    if res is not None:
        x_res, gate, lat_rows = res
        G = gate.shape[0]
        in_specs += [o_spec, BS((G, 1, tn), lambda g0, g1, k: (0, 0, ij(g0, g1)[1]))]
        operands += [x_res, gate]
        out_shape, out_specs = (jax.ShapeDtypeStruct((M, N), F32), jax.ShapeDtypeStruct((M, N), BF16)), (o_spec, o_spec)
    n_in = len(in_specs)

    def body(*refs):
        a_ref, b_ref = refs[0], refs[1]
        bias_ref = refs[2] if bias is not None else None
        o_ref = refs[n_in]
        av = a_ref[...]
        if silu_a:
            av = av.astype(F32)
            av = av * _sigmoid(av)
        part = lax.dot_general(av.astype(BF16), b_ref[...].astype(BF16), _DIMS[mode], preferred_element_type=F32)
        i_blk = pl.program_id(1 if outer == "j" else 0)

        def finish(acc):
            if bias_ref is not None:
                acc = acc + bias_ref[...]
            if res is None:
                o_ref[...] = acc.astype(out_dtype)
                return
            x_ref, gate_ref, y_ref = refs[n_in - 2], refs[n_in - 1], refs[n_in + 1]
            gate_ = gate_ref[0]
            if G == 2:
                rows = i_blk * tm + lax.broadcasted_iota(jnp.int32, (tm, 1), 0)
                gate_ = jnp.where(rows < lat_rows, gate_ref[0], gate_ref[1])
            o_ref[...] = x_ref[...] + gate_ * acc
            y_ref[...] = acc.astype(BF16)

        if gk == 1:
            finish(part)
        else:
            acc_ref = refs[-1]
            k = pl.program_id(2)

            @pl.when(k == 0)
            def _():
                acc_ref[...] = part

            @pl.when(k > 0)
            def _():
                acc_ref[...] += part

            @pl.when(k == gk - 1)
            def _():
                finish(acc_ref[...])

    grid = (gn, gm, gk) if outer == "j" else (gm, gn, gk)
    return _call(body, name=name, out_shape=out_shape, grid=grid, in_specs=in_specs, out_specs=out_specs,
                 scratch=[pltpu.VMEM((tm, tn), F32)] if gk > 1 else [], sem=("parallel", "parallel", "arbitrary"),
                 operands=operands)


def _accum(ref, val, first):
    @pl.when(first)
    def _():
        ref[...] = val

    @pl.when(jnp.logical_not(first))
    def _():
        ref[...] += val


def _norm_mod_fwd(X, gain, sh, sc, tb, nlat, name):
    R, D = X.shape

    def body(x_ref, g_ref, sh_ref, sc_ref, o_ref):
        x = x_ref[...]
        r = lax.rsqrt(jnp.mean(x * x, axis=-1, keepdims=True) + EPS)
        o_ref[...] = ((x * r * g_ref[...]) * (1.0 + sc_ref[0]) + sh_ref[0]).astype(BF16)

    grp = BS((1, 1, D), lambda i: (i // nlat, 0, 0))
    return _pcs(
        body, name=name, out_shape=jax.ShapeDtypeStruct((R, D), BF16), grid=(R // tb,),
        in_specs=[BS((tb, D), lambda i: (i, 0)), BS((1, D), lambda i: (0, 0)), grp, grp],
        out_specs=BS((tb, D), lambda i: (i, 0)), compiler_params=_cp(("parallel",)),
    )(X, gain.reshape(1, D), sh, sc)


def _norm_mod_bwd(X, gain, sc, dh, dup, tb, nlat, name):
    R, D = X.shape
    G = sc.shape[0]
    n_up = dup.shape[0] // tb

    def body(x_ref, g_ref, sc_ref, dh_ref, dup_ref, dx_ref, dg_ref, dsh_ref, dsc_ref):
        i = pl.program_id(0)
        x = x_ref[...]
        g = g_ref[...]
        r = lax.rsqrt(jnp.mean(x * x, axis=-1, keepdims=True) + EPS)
        xh = x * r
        dh_ = dh_ref[...].astype(F32)
        t = dh_ * (1.0 + sc_ref[0])
        tg = t * g
        dup_ = dup_ref[...] if n_up == R // tb else jnp.where(i < n_up, dup_ref[...], 0.0)
        dx_ref[...] = dup_ + r * (tg - xh * jnp.mean(tg * xh, axis=-1, keepdims=True))
        _accum(dg_ref, jnp.sum(t * xh, axis=0, keepdims=True), i == 0)
        first = i % nlat == 0
        _accum(dsh_ref, jnp.sum(dh_, axis=0, keepdims=True)[None], first)
        _accum(dsc_ref, jnp.sum(dh_ * xh * g, axis=0, keepdims=True)[None], first)

    row = BS((tb, D), lambda i: (i, 0))
    grp = BS((1, 1, D), lambda i: (i // nlat, 0, 0))
    return _pcs(
        body, name=name,
        out_shape=(jax.ShapeDtypeStruct((R, D), F32), jax.ShapeDtypeStruct((1, D), F32),
                   jax.ShapeDtypeStruct((G, 1, D), F32), jax.ShapeDtypeStruct((G, 1, D), F32)),
        grid=(R // tb,), in_specs=[row, BS((1, D), lambda i: (0, 0)), grp, row, BS((tb, D), lambda i: (jnp.minimum(i, n_up - 1), 0))],
        out_specs=(row, BS((1, D), lambda i: (0, 0)), grp, grp), compiler_params=_cp(("arbitrary",)),
    )(X, gain.reshape(1, D), sc, dh, dup)


def _gate_bwd(dX, y, gate, tb, nlat, name):
    R, D = dX.shape
    G = gate.shape[0]

    def body(dx_ref, y_ref, g_ref, dy_ref, dg_ref):
        i = pl.program_id(0)
        dx = dx_ref[...]
        dy_ref[...] = (g_ref[0] * dx).astype(BF16)
        _accum(dg_ref, jnp.sum(dx * y_ref[...], axis=0, keepdims=True)[None], i % nlat == 0)

    row = BS((tb, D), lambda i: (i, 0))
    grp = BS((1, 1, D), lambda i: (i // nlat, 0, 0))
    return _pcs(
        body, name=name, out_shape=(jax.ShapeDtypeStruct((R, D), BF16), jax.ShapeDtypeStruct((G, 1, D), F32)),
        grid=(R // tb,), in_specs=[row, row, grp], out_specs=(row, grp), compiler_params=_cp(("arbitrary",)),
    )(dX, y, gate)


def _final_loss(X, gain, target, tb, name):
    R, D = X.shape

    def body(x_ref, g_ref, t_ref, loss_ref, dx_ref, dg_ref):
        i = pl.program_id(0)
        x = x_ref[...]
        g = g_ref[...]
        r = lax.rsqrt(jnp.mean(x * x, axis=-1, keepdims=True) + EPS)
        xh = x * r
        e = xh * g - t_ref[...]
        part = jnp.sum(jnp.sum(e * e, axis=1, keepdims=True), axis=0, keepdims=True) * (0.5 / D)
        _accum(loss_ref, jnp.broadcast_to(part, (1, LANES)), i == 0)
        dy = e * (1.0 / D)
        _accum(dg_ref, jnp.sum(dy * xh, axis=0, keepdims=True), i == 0)
        tg = dy * g
        dx_ref[...] = r * (tg - xh * jnp.mean(tg * xh, axis=-1, keepdims=True))

    row = BS((tb, D), lambda i: (i, 0))
    return _pcs(
        body, name=name,
        out_shape=(jax.ShapeDtypeStruct((1, LANES), F32), jax.ShapeDtypeStruct((R, D), F32), jax.ShapeDtypeStruct((1, D), F32)),
        grid=(R // tb,), in_specs=[row, BS((1, D), lambda i: (0, 0)), row],
        out_specs=(BS((1, LANES), lambda i: (0, 0)), row, BS((1, D), lambda i: (0, 0))), compiler_params=_cp(("arbitrary",)),
    )(X, gain.reshape(1, D), target)


def _ln_stats(v):
    mu = jnp.mean(v, axis=-1, keepdims=True)
    d = v - mu
    r = lax.rsqrt(jnp.mean(d * d, axis=-1, keepdims=True) + EPS)
    return d * r, r


def _gmlp_fwd(z, ln_g, ln_b, w_s, b_s_full, tb, name):
    R = z.shape[0]
    AW = ln_g.shape[-1]
    AH = w_s.shape[0]

    def body(zu_ref, zv_ref, g_ref, b_ref, ws_ref, bs_ref, o_ref):
        u, _ = _gelu(zu_ref[...].astype(F32))
        v, _ = _gelu(zv_ref[...].astype(F32))
        xh, _ = _ln_stats(v)
        vn = (xh * g_ref[...] + b_ref[...]).astype(BF16)
        for n in range(tb // CHUNK):
            rs = slice(n * CHUNK, (n + 1) * CHUNK)
            for h in range(AH):
                cs = slice(h * CHUNK, (h + 1) * CHUNK)
                v2 = jnp.dot(ws_ref[h].astype(BF16), vn[rs, cs], preferred_element_type=F32) + bs_ref[h]
                o_ref[rs, cs] = (u[rs, cs] * v2).astype(BF16)

    full3 = lambda s: BS(s, lambda i: (0, 0, 0))
    return _pcs(
        body, name=name, out_shape=jax.ShapeDtypeStruct((R, AW), BF16), grid=(R // tb,),
        in_specs=[BS((tb, AW), lambda i: (i, 0)), BS((tb, AW), lambda i: (i, 1)), BS((1, AW), lambda i: (0, 0)),
                  BS((1, AW), lambda i: (0, 0)), full3((AH, CHUNK, CHUNK)), full3((AH, CHUNK, CHUNK))],
        out_specs=BS((tb, AW), lambda i: (i, 0)), compiler_params=_cp(("parallel",)),
    )(z, z, ln_g.reshape(1, AW), ln_b.reshape(1, AW), w_s, b_s_full)


def _gmlp_bwd(z, dya, ln_g, ln_b, w_s, b_s_full, tb, name):
    R = z.shape[0]
    AW = ln_g.shape[-1]
    AH = w_s.shape[0]

    def body(zu_ref, zv_ref, dy_ref, g_ref, b_ref, ws_ref, bs_ref, dzu_ref, dzv_ref, dg_ref, db_ref, dws_ref, dbs_ref, cs_u_ref, cs_v_ref,
             dvn_scr):
        i = pl.program_id(0)
        first = i == 0
        zu = zu_ref[...].astype(F32)
        zv = zv_ref[...].astype(F32)
        u, thu = _gelu(zu)
        v, thv = _gelu(zv)
        xh, r = _ln_stats(v)
        g = g_ref[...]
        vn = (xh * g + b_ref[...]).astype(BF16)
        dy = dy_ref[...].astype(F32)

        @pl.when(first)
        def _():
            dws_ref[...] = jnp.zeros_like(dws_ref)
            dbs_ref[...] = jnp.zeros_like(dbs_ref)

        for n in range(tb // CHUNK):
            rs = slice(n * CHUNK, (n + 1) * CHUNK)
            for h in range(AH):
                cs = slice(h * CHUNK, (h + 1) * CHUNK)
                w = ws_ref[h].astype(BF16)
                v2 = jnp.dot(w, vn[rs, cs], preferred_element_type=F32) + bs_ref[h]
                dzu_ref[rs, cs] = (dy[rs, cs] * v2 * _gelu_grad(zu[rs, cs], thu[rs, cs])).astype(BF16)
                dv2 = dy[rs, cs] * u[rs, cs]
                dv2b = dv2.astype(BF16)
                dvn_scr[rs, cs] = lax.dot_general(w, dv2b, _DIMS["tn"], preferred_element_type=F32)
                dws_ref[h] += lax.dot_general(dv2b, vn[rs, cs], _DIMS["nt"], preferred_element_type=F32)
                dbs_ref[h] += jnp.sum(dv2, axis=1, keepdims=True)
        dvn = dvn_scr[...]
        _accum(dg_ref, jnp.sum(dvn * xh, axis=0, keepdims=True), first)
        _accum(db_ref, jnp.sum(dvn, axis=0, keepdims=True), first)
        t = dvn * g
        dv = r * (t - jnp.mean(t, axis=-1, keepdims=True) - xh * jnp.mean(t * xh, axis=-1, keepdims=True))
        dzv = dv * _gelu_grad(zv, thv)
        dzv_ref[...] = dzv.astype(BF16)
        _accum(cs_v_ref, jnp.sum(dzv, axis=0, keepdims=True), first)
        _accum(cs_u_ref, jnp.sum(dzu_ref[...].astype(F32), axis=0, keepdims=True), first)

    full3 = lambda s: BS(s, lambda i: (0, 0, 0))
    vec = BS((1, AW), lambda i: (0, 0))
    row = BS((tb, AW), lambda i: (i, 0))
    outs = _pcs(
        body, name=name,
        out_shape=(jax.ShapeDtypeStruct((R, AW), BF16), jax.ShapeDtypeStruct((R, AW), BF16), jax.ShapeDtypeStruct((1, AW), F32),
                   jax.ShapeDtypeStruct((1, AW), F32), jax.ShapeDtypeStruct((AH, CHUNK, CHUNK), F32),
                   jax.ShapeDtypeStruct((AH, CHUNK, 1), F32), jax.ShapeDtypeStruct((1, AW), F32), jax.ShapeDtypeStruct((1, AW), F32)),
        grid=(R // tb,),
        in_specs=[row, BS((tb, AW), lambda i: (i, 1)), row, vec, vec, full3((AH, CHUNK, CHUNK)), full3((AH, CHUNK, CHUNK))],
        out_specs=(row, row, vec, vec, full3((AH, CHUNK, CHUNK)), full3((AH, CHUNK, 1)), vec, vec),
        scratch_shapes=[pltpu.VMEM((tb, AW), F32)], compiler_params=_cp(("arbitrary",)),
    )(z, z, dya, ln_g.reshape(1, AW), ln_b.reshape(1, AW), w_s, b_s_full)
    return outs


def _segments(R, L):
    return [(0, L)] + ([(L, R - L)] if R > L else [])


def _scr_rows(R, L):
    return R + CONV_PAD * (len(_segments(R, L)) + 1)


def _scr_off(s, start):
    return CONV_PAD * (s + 1) + start


def _zero_pads(scr, R, L):
    segs = _segments(R, L)
    z = jnp.zeros((CONV_PAD, scr.shape[1]), F32)
    for s, (start, n) in enumerate(segs):
        scr[pl.ds(_scr_off(s, start) - CONV_PAD, CONV_PAD), :] = z
    last_s, (last_start, last_n) = len(segs) - 1, segs[-1]
    scr[pl.ds(_scr_off(last_s, last_start) + last_n, CONV_PAD), :] = z


def _for_chunks(R, L, ch, fn):
    for s, (start, n) in enumerate(_segments(R, L)):
        c = min(ch, n)
        off = _scr_off(s, start)

        def step(i, carry, start=start, off=off, c=c):
            r0 = pl.multiple_of(start + i * c, SUBLANES)
            fn(r0, pl.multiple_of(off + i * c, SUBLANES), c)
            return carry

        lax.fori_loop(0, n // c, step, 0)


def _taps(scr, srow, c, w_ref, ntap, flip):
    acc = None
    for k in range(ntap):
        o = k - (ntap - 1) // 2
        if flip:
            o = -o
        term = w_ref[k:k + 1, :] * scr[pl.ds(srow + o, c), :]
        acc = term if acc is None else acc + term
    return acc


def _tap_grads(scr, srow, c, dy, dw_ref, ntap):
    for k in range(ntap):
        o = k - (ntap - 1) // 2
        dw_ref[k:k + 1, :] += jnp.sum(dy * scr[pl.ds(srow + o, c), :], axis=0, keepdims=True)


def _glu_conv_fwd(z, col0, conv_w, conv_b, L, ch, name):
    R = z.shape[0]
    BW = conv_w.shape[1]
    nb, c0 = BW // LANES, col0 // LANES

    def body(a_ref, g_ref, w_ref, b_ref, o_ref, scr):
        _zero_pads(scr, R, L)

        def fill(r0, s0, c):
            a = a_ref[pl.ds(r0, c), :].astype(F32)
            g = g_ref[pl.ds(r0, c), :].astype(F32)
            scr[pl.ds(s0, c), :] = a * _sigmoid(g)

        _for_chunks(R, L, ch, fill)

        def conv(r0, s0, c):
            o_ref[pl.ds(r0, c), :] = _taps(scr, s0, c, w_ref, B_CONV, False) + b_ref[...]

        _for_chunks(R, L, ch, conv)

    return _pcs(
        body, name=name, out_shape=jax.ShapeDtypeStruct((R, BW), F32), grid=(nb,),
        in_specs=[BS((R, LANES), lambda j: (0, c0 + j)), BS((R, LANES), lambda j: (0, c0 + nb + j)),
                  BS((B_CONV, LANES), lambda j: (0, j)), BS((1, LANES), lambda j: (0, j))],
        out_specs=BS((R, LANES), lambda j: (0, j)), scratch_shapes=[pltpu.VMEM((_scr_rows(R, L), LANES), F32)],
        compiler_params=_cp(("parallel",)),
    )(z, z, conv_w, conv_b.reshape(1, BW))


def _glu_conv_bwd(z, col0, dhc, conv_w, L, ch, name):
    R = z.shape[0]
    BW = conv_w.shape[1]
    nb, c0 = BW // LANES, col0 // LANES

    def body(a_ref, g_ref, dy_ref, w_ref, da_ref, dg_ref, dw_ref, db_ref, csa_ref, csg_ref, scr_h, scr_dy):
        _zero_pads(scr_h, R, L)
        _zero_pads(scr_dy, R, L)
        dw_ref[...] = jnp.zeros_like(dw_ref)
        db_ref[...] = jnp.zeros_like(db_ref)
        csa_ref[...] = jnp.zeros_like(csa_ref)
        csg_ref[...] = jnp.zeros_like(csg_ref)

        def fill(r0, s0, c):
            a = a_ref[pl.ds(r0, c), :].astype(F32)
            g = g_ref[pl.ds(r0, c), :].astype(F32)
            scr_h[pl.ds(s0, c), :] = a * _sigmoid(g)
            scr_dy[pl.ds(s0, c), :] = dy_ref[pl.ds(r0, c), :]

        _for_chunks(R, L, ch, fill)

        def back(r0, s0, c):
            dh = _taps(scr_dy, s0, c, w_ref, B_CONV, True)
            a = a_ref[pl.ds(r0, c), :].astype(F32)
            sg = _sigmoid(g_ref[pl.ds(r0, c), :].astype(F32))
            da = dh * sg
            dg = dh * a * sg * (1.0 - sg)
            da_ref[pl.ds(r0, c), :] = da.astype(BF16)
            dg_ref[pl.ds(r0, c), :] = dg.astype(BF16)
            csa_ref[...] += jnp.sum(da, axis=0, keepdims=True)
            csg_ref[...] += jnp.sum(dg, axis=0, keepdims=True)
            dy = dy_ref[pl.ds(r0, c), :]
            db_ref[...] += jnp.sum(dy, axis=0, keepdims=True)
            _tap_grads(scr_h, s0, c, dy, dw_ref, B_CONV)

        _for_chunks(R, L, ch, back)

    col = BS((R, LANES), lambda j: (0, j))
    vec = BS((1, LANES), lambda j: (0, j))
    nrow = _scr_rows(R, L)
    return _pcs(
        body, name=name,
        out_shape=(jax.ShapeDtypeStruct((R, BW), BF16), jax.ShapeDtypeStruct((R, BW), BF16), jax.ShapeDtypeStruct((B_CONV, BW), F32),
                   jax.ShapeDtypeStruct((1, BW), F32), jax.ShapeDtypeStruct((1, BW), F32), jax.ShapeDtypeStruct((1, BW), F32)),
        grid=(nb,),
        in_specs=[BS((R, LANES), lambda j: (0, c0 + j)), BS((R, LANES), lambda j: (0, c0 + nb + j)), col,
                  BS((B_CONV, LANES), lambda j: (0, j))],
        out_specs=(col, col, BS((B_CONV, LANES), lambda j: (0, j)), vec, vec, vec),
        scratch_shapes=[pltpu.VMEM((nrow, LANES), F32), pltpu.VMEM((nrow, LANES), F32)], compiler_params=_cp(("parallel",)),
    )(z, z, dhc, conv_w)


def _ln_silu_fwd(hc, ln_g, ln_b, tb, name):
    R, W = hc.shape

    def body(x_ref, g_ref, b_ref, o_ref):
        xh, _ = _ln_stats(x_ref[...])
        y = xh * g_ref[...] + b_ref[...]
        o_ref[...] = (y * _sigmoid(y)).astype(BF16)

    vec = BS((1, W), lambda i: (0, 0))
    row = BS((tb, W), lambda i: (i, 0))
    return _pcs(
        body, name=name, out_shape=jax.ShapeDtypeStruct((R, W), BF16), grid=(R // tb,), in_specs=[row, vec, vec], out_specs=row,
        compiler_params=_cp(("parallel",)),
    )(hc, ln_g.reshape(1, W), ln_b.reshape(1, W))


def _ln_silu_bwd(hc, dyb, col0, ln_g, ln_b, tb, name):
    R, W = hc.shape
    c0 = col0 // W

    def body(x_ref, dy_ref, g_ref, b_ref, dx_ref, dg_ref, db_ref):
        i = pl.program_id(0)
        xh, r = _ln_stats(x_ref[...])
        g = g_ref[...]
        y = xh * g + b_ref[...]
        s = _sigmoid(y)
        dy = dy_ref[...].astype(F32) * s * (1.0 + y * (1.0 - s))
        _accum(dg_ref, jnp.sum(dy * xh, axis=0, keepdims=True), i == 0)
        _accum(db_ref, jnp.sum(dy, axis=0, keepdims=True), i == 0)
        t = dy * g
        dx_ref[...] = r * (t - jnp.mean(t, axis=-1, keepdims=True) - xh * jnp.mean(t * xh, axis=-1, keepdims=True))

    vec = BS((1, W), lambda i: (0, 0))
    row = BS((tb, W), lambda i: (i, 0))
    return _pcs(
        body, name=name,
        out_shape=(jax.ShapeDtypeStruct((R, W), F32), jax.ShapeDtypeStruct((1, W), F32), jax.ShapeDtypeStruct((1, W), F32)),
        grid=(R // tb,), in_specs=[row, BS((tb, W), lambda i: (i, c0)), vec, vec], out_specs=(row, vec, vec),
        compiler_params=_cp(("arbitrary",)),
    )(hc, dyb, ln_g.reshape(1, W), ln_b.reshape(1, W))


def _ffn_act_fwd(zf, conv_w, conv_b, L, ch, name):
    R = zf.shape[0]
    DFF = conv_w.shape[1]
    nb = DFF // LANES

    def body(g_ref, u_ref, w_ref, b_ref, o_ref, scr):
        _zero_pads(scr, R, L)

        def fill(r0, s0, c):
            scr[pl.ds(s0, c), :] = g_ref[pl.ds(r0, c), :].astype(F32)

        _for_chunks(R, L, ch, fill)

        def act(r0, s0, c):
            gc = _taps(scr, s0, c, w_ref, FFN_CONV, False) + b_ref[...]
            o_ref[pl.ds(r0, c), :] = (gc * _sigmoid(gc) * u_ref[pl.ds(r0, c), :].astype(F32)).astype(BF16)

        _for_chunks(R, L, ch, act)

    return _pcs(
        body, name=name, out_shape=jax.ShapeDtypeStruct((R, DFF), BF16), grid=(nb,),
        in_specs=[BS((R, LANES), lambda j: (0, j)), BS((R, LANES), lambda j: (0, nb + j)), BS((FFN_CONV, LANES), lambda j: (0, j)),
                  BS((1, LANES), lambda j: (0, j))],
        out_specs=BS((R, LANES), lambda j: (0, j)), scratch_shapes=[pltpu.VMEM((_scr_rows(R, L), LANES), F32)],
        compiler_params=_cp(("parallel",)),
    )(zf, zf, conv_w, conv_b.reshape(1, DFF))


def _ffn_act_bwd(zf, df, conv_w, conv_b, L, ch, name):
    R = zf.shape[0]
    DFF = conv_w.shape[1]
    nb = DFF // LANES

    def body(g_ref, u_ref, df_ref, w_ref, b_ref, dz_ref, dw_ref, db_ref, scr_g, scr_d):
        _zero_pads(scr_g, R, L)
        _zero_pads(scr_d, R, L)
        dw_ref[...] = jnp.zeros_like(dw_ref)
        db_ref[...] = jnp.zeros_like(db_ref)

        def fill(r0, s0, c):
            scr_g[pl.ds(s0, c), :] = g_ref[pl.ds(r0, c), :].astype(F32)

        _for_chunks(R, L, ch, fill)

        def pre(r0, s0, c):
            gc = _taps(scr_g, s0, c, w_ref, FFN_CONV, False) + b_ref[...]
            s = _sigmoid(gc)
            d = df_ref[pl.ds(r0, c), :].astype(F32)
            dz_ref[1, pl.ds(r0, c), :] = (d * gc * s).astype(BF16)
            dgc = d * u_ref[pl.ds(r0, c), :].astype(F32) * s * (1.0 + gc * (1.0 - s))
            scr_d[pl.ds(s0, c), :] = dgc
            db_ref[...] += jnp.sum(dgc, axis=0, keepdims=True)
            _tap_grads(scr_g, s0, c, dgc, dw_ref, FFN_CONV)

        _for_chunks(R, L, ch, pre)

        def back(r0, s0, c):
            dz_ref[0, pl.ds(r0, c), :] = _taps(scr_d, s0, c, w_ref, FFN_CONV, True).astype(BF16)

        _for_chunks(R, L, ch, back)

    col = BS((R, LANES), lambda j: (0, j))
    vec = BS((1, LANES), lambda j: (0, j))
    nrow = _scr_rows(R, L)
    return _pcs(
        body, name=name,
        out_shape=(jax.ShapeDtypeStruct((2, R, DFF), BF16), jax.ShapeDtypeStruct((FFN_CONV, DFF), F32), jax.ShapeDtypeStruct((1, DFF), F32)),
        grid=(nb,),
        in_specs=[col, BS((R, LANES), lambda j: (0, nb + j)), col, BS((FFN_CONV, LANES), lambda j: (0, j)), vec],
        out_specs=(BS((2, R, LANES), lambda j: (0, 0, j)), BS((FFN_CONV, LANES), lambda j: (0, j)), vec),
        scratch_shapes=[pltpu.VMEM((nrow, LANES), F32), pltpu.VMEM((nrow, LANES), F32)], compiler_params=_cp(("parallel",)),
    )(zf, zf, df, conv_w, conv_b.reshape(1, DFF))


def _rope_tables(L, T):
    rows = L // GRID_W
    row = jnp.repeat(jnp.arange(rows, dtype=F32), GRID_W)
    col = jnp.tile(jnp.arange(GRID_W, dtype=F32), rows)
    n_freq = ROPE // 4
    inv = ROPE_THETA ** (-jnp.arange(n_freq, dtype=F32) / n_freq)
    ang = jnp.concatenate([row[:, None] * inv, col[:, None] * inv], axis=-1)
    cos, sin = jnp.cos(ang), jnp.sin(ang)
    half = ROPE // 2
    zero = jnp.zeros((L, half), F32)
    cos_t = jnp.concatenate([cos, cos, jnp.ones((L, LANES - ROPE), F32)], axis=1)
    sa = jnp.concatenate([zero, sin, zero, zero], axis=1)
    sb = jnp.concatenate([-sin, zero, zero, zero], axis=1)
    pad = T - L
    cos_t = jnp.concatenate([cos_t, jnp.ones((pad, LANES), F32)], axis=0)
    sa = jnp.concatenate([sa, jnp.zeros((pad, LANES), F32)], axis=0)
    sb = jnp.concatenate([sb, jnp.zeros((pad, LANES), F32)], axis=0)
    return cos_t, sa, sb


def _rope(x, cos, sa, sb):
    half = ROPE // 2
    return x * cos + pltpu.roll(x, half, 1) * sa + pltpu.roll(x, LANES - half, 1) * sb


def _rope_t(d, cos, sa, sb):
    half = ROPE // 2
    return d * cos + pltpu.roll(d * sa, LANES - half, 1) + pltpu.roll(d * sb, half, 1)


def _mla_prep_fwd(zm, qg, kvg, tabs, tb, name):
    T, W = zm.shape
    QL, KVL = qg.shape[-1], kvg.shape[-1]

    def body(z_ref, qg_ref, kg_ref, cos_ref, sa_ref, sb_ref, q_ref, k_ref, p_ref):
        cq = z_ref[:, :QL]
        r = lax.rsqrt(jnp.mean(cq * cq, axis=-1, keepdims=True) + EPS)
        q_ref[...] = (cq * r * qg_ref[...]).astype(BF16)
        ck = z_ref[:, QL:QL + KVL]
        r = lax.rsqrt(jnp.mean(ck * ck, axis=-1, keepdims=True) + EPS)
        k_ref[...] = (ck * r * kg_ref[...]).astype(BF16)
        p_ref[...] = _rope(z_ref[:, QL + KVL:], cos_ref[...], sa_ref[...], sb_ref[...]).astype(BF16)

    tab = BS((tb, LANES), lambda i: (i, 0))
    return _pcs(
        body, name=name,
        out_shape=(jax.ShapeDtypeStruct((T, QL), BF16), jax.ShapeDtypeStruct((T, KVL), BF16), jax.ShapeDtypeStruct((T, LANES), BF16)),
        grid=(T // tb,),
        in_specs=[BS((tb, W), lambda i: (i, 0)), BS((1, QL), lambda i: (0, 0)), BS((1, KVL), lambda i: (0, 0)), tab, tab, tab],
        out_specs=(BS((tb, QL), lambda i: (i, 0)), BS((tb, KVL), lambda i: (i, 0)), tab), compiler_params=_cp(("parallel",)),
    )(zm, qg.reshape(1, QL), kvg.reshape(1, KVL), *tabs)


def _mla_prep_bwd(zm, qg, kvg, tabs, dq, dk, dp, tb, name):
    T, W = zm.shape
    QL, KVL = qg.shape[-1], kvg.shape[-1]

    def body(z_ref, qg_ref, kg_ref, cos_ref, sa_ref, sb_ref, dq_ref, dk_ref, dp_ref, dz_ref, dqg_ref, dkg_ref):
        i = pl.program_id(0)

        def rms_bwd(x, g, dy):
            r = lax.rsqrt(jnp.mean(x * x, axis=-1, keepdims=True) + EPS)
            xh = x * r
            t = dy * g
            return r * (t - xh * jnp.mean(t * xh, axis=-1, keepdims=True)), jnp.sum(dy * xh, axis=0, keepdims=True)

        dcq, dg = rms_bwd(z_ref[:, :QL], qg_ref[...], dq_ref[...].astype(F32))
        dz_ref[:, :QL] = dcq.astype(BF16)
        _accum(dqg_ref, dg, i == 0)
        dck, dg = rms_bwd(z_ref[:, QL:QL + KVL], kg_ref[...], dk_ref[...].astype(F32))
        dz_ref[:, QL:QL + KVL] = dck.astype(BF16)
        _accum(dkg_ref, dg, i == 0)
        dz_ref[:, QL + KVL:] = _rope_t(dp_ref[...], cos_ref[...], sa_ref[...], sb_ref[...]).astype(BF16)

    tab = BS((tb, LANES), lambda i: (i, 0))
    return _pcs(
        body, name=name,
        out_shape=(jax.ShapeDtypeStruct((T, W), BF16), jax.ShapeDtypeStruct((1, QL), F32), jax.ShapeDtypeStruct((1, KVL), F32)),
        grid=(T // tb,),
        in_specs=[BS((tb, W), lambda i: (i, 0)), BS((1, QL), lambda i: (0, 0)), BS((1, KVL), lambda i: (0, 0)), tab, tab, tab,
                  BS((tb, QL), lambda i: (i, 0)), BS((tb, KVL), lambda i: (i, 0)), tab],
        out_specs=(BS((tb, W), lambda i: (i, 0)), BS((1, QL), lambda i: (0, 0)), BS((1, KVL), lambda i: (0, 0))),
        compiler_params=_cp(("arbitrary",)),
    )(zm, qg.reshape(1, QL), kvg.reshape(1, KVL), *tabs, dq, dk, dp)


def _attn_fwd(q, kv, kpe, tabs, L, tq, nsub, name):
    T = kv.shape[0]
    H = kv.shape[1] // QHEAD
    scale = (HEAD + ROPE) ** -0.5

    def body(q_ref, kv_ref, kpe_ref, cos_ref, sa_ref, sb_ref, o_ref, lse_ref, kcat):
        @pl.when(pl.program_id(1) == 0)
        def _():
            kcat[:, :HEAD] = kv_ref[:, :HEAD]
            kcat[:, HEAD:] = kpe_ref[...]

        sub = tq // nsub
        for r in range(nsub):
            rs = slice(r * sub, (r + 1) * sub)
            qp = _rope(q_ref[rs, HEAD:].astype(F32), cos_ref[rs, :], sa_ref[rs, :], sb_ref[rs, :]).astype(BF16)
            qc = jnp.concatenate([q_ref[rs, :HEAD], qp], axis=1)
            s = lax.dot_general(qc, kcat[...], _DIMS["nt"], preferred_element_type=F32)
            m = jnp.max(s, axis=-1, keepdims=True)
            p = jnp.exp2((s - m) * (scale * math.log2(math.e)))
            l = jnp.sum(p, axis=-1, keepdims=True)
            o = jnp.dot(p.astype(BF16), kv_ref[:, HEAD:], preferred_element_type=F32)
            o_ref[rs, :] = (o / l).astype(BF16)
            lse_ref[0, rs, :] = m * scale + jnp.log(l)

    tab = BS((tq, LANES), lambda h, i: (i, 0))
    return _call(
        body, name=name, out_shape=(jax.ShapeDtypeStruct((L, H * HEAD), BF16), jax.ShapeDtypeStruct((H, L, 1), F32)),
        grid=(H, L // tq),
        in_specs=[BS((tq, QHEAD), lambda h, i: (i, h)), BS((T, QHEAD), lambda h, i: (0, h)), BS((T, LANES), lambda h, i: (0, 0)),
                  tab, tab, tab],
        out_specs=(BS((tq, HEAD), lambda h, i: (i, h)), BS((1, tq, 1), lambda h, i: (h, i, 0))),
        scratch=[pltpu.VMEM((T, QHEAD), BF16)], sem=("parallel", "arbitrary"), operands=(q, kv, kpe, *tabs))


def _attn_bwd(q, kv, kpe, tabs, o, lse, do, L, tq, name):
    T = kv.shape[0]
    H = kv.shape[1] // QHEAD
    scale = (HEAD + ROPE) ** -0.5
    nq = L // tq

    def body(q_ref, kv_ref, kpe_ref, cos_ref, sa_ref, sb_ref, o_ref, lse_ref, do_ref, dq_ref, dkv_ref, dkpe_ref, kcat, dk_acc, dv_acc,
             qc_scr, ds_scr, p_scr):
        h, i = pl.program_id(0), pl.program_id(1)

        @pl.when(i == 0)
        def _():
            kcat[:, :HEAD] = kv_ref[:, :HEAD]
            kcat[:, HEAD:] = kpe_ref[...]
            dk_acc[...] = jnp.zeros_like(dk_acc)
            dv_acc[...] = jnp.zeros_like(dv_acc)

        sub = tq // 2
        log2e = math.log2(math.e)
        for r in range(2):
            rs = slice(r * sub, (r + 1) * sub)
            cos, sa, sb = cos_ref[rs, :], sa_ref[rs, :], sb_ref[rs, :]
            qp = _rope(q_ref[rs, HEAD:].astype(F32), cos, sa, sb).astype(BF16)
            qc_scr[rs, :] = jnp.concatenate([q_ref[rs, :HEAD], qp], axis=1)
            s = lax.dot_general(qc_scr[rs, :], kcat[...], _DIMS["nt"], preferred_element_type=F32)
            p = jnp.exp2(s * (scale * log2e) - lse_ref[0, rs, :] * log2e)
            dov = do_ref[rs, :]
            delta = jnp.sum(dov.astype(F32) * o_ref[rs, :].astype(F32), axis=-1, keepdims=True)
            dp = lax.dot_general(dov, kv_ref[:, HEAD:], _DIMS["nt"], preferred_element_type=F32)
            ds_scr[rs, :] = (p * (dp - delta) * scale).astype(BF16)
            p_scr[rs, :] = p.astype(BF16)
            dqc = jnp.dot(ds_scr[rs, :], kcat[...], preferred_element_type=F32)
            dq_ref[rs, :HEAD] = dqc[:, :HEAD].astype(BF16)
            dq_ref[rs, HEAD:] = _rope_t(dqc[:, HEAD:], cos, sa, sb).astype(BF16)
        dk_acc[...] += lax.dot_general(ds_scr[...], qc_scr[...], _DIMS["tn"], preferred_element_type=F32)
        dv_acc[...] += lax.dot_general(p_scr[...], do_ref[...], _DIMS["tn"], preferred_element_type=F32)

        @pl.when(i == nq - 1)
        def _():
            dkv_ref[:, :HEAD] = dk_acc[:, :HEAD].astype(BF16)
            dkv_ref[:, HEAD:] = dv_acc[...].astype(BF16)

            @pl.when(h == 0)
            def _():
                dkpe_ref[...] = dk_acc[:, HEAD:]

            @pl.when(h > 0)
            def _():
                dkpe_ref[...] += dk_acc[:, HEAD:]

    tab = BS((tq, LANES), lambda h, i: (i, 0))
    return _call(
        body, name=name,
        out_shape=(jax.ShapeDtypeStruct((L, H * QHEAD), BF16), jax.ShapeDtypeStruct((T, H * QHEAD), BF16), jax.ShapeDtypeStruct((T, LANES), F32)),
        grid=(H, nq),
        in_specs=[BS((tq, QHEAD), lambda h, i: (i, h)), BS((T, QHEAD), lambda h, i: (0, h)), BS((T, LANES), lambda h, i: (0, 0)),
                  tab, tab, tab, BS((tq, HEAD), lambda h, i: (i, h)), BS((1, tq, 1), lambda h, i: (h, i, 0)),
                  BS((tq, HEAD), lambda h, i: (i, h))],
        out_specs=(BS((tq, QHEAD), lambda h, i: (i, h)), BS((T, QHEAD), lambda h, i: (0, h)), BS((T, LANES), lambda h, i: (0, 0))),
        scratch=[pltpu.VMEM((T, QHEAD), BF16), pltpu.VMEM((T, QHEAD), F32), pltpu.VMEM((T, HEAD), F32),
                 pltpu.VMEM((tq, QHEAD), BF16), pltpu.VMEM((tq, T), BF16), pltpu.VMEM((tq, T), BF16)],
        sem=("arbitrary", "arbitrary"), operands=(q, kv, kpe, *tabs, o, lse, do))


def _adamw(w, g, m, v, name):
    R, C = w.shape
    tr = _row_tile(R, C)
    c1 = 1.0 / (1.0 - ADAM_B1 ** ADAM_STEP)
    c2 = 1.0 / (1.0 - ADAM_B2 ** ADAM_STEP)

    def body(w_ref, g_ref, m_ref, v_ref, d_ref, nm_ref, nv_ref):
        g_ = g_ref[...]
        nm = ADAM_B1 * m_ref[...] + (1.0 - ADAM_B1) * g_
        nv = ADAM_B2 * v_ref[...] + (1.0 - ADAM_B2) * (g_ * g_)
        nm_ref[...] = nm
        nv_ref[...] = nv
        d_ref[...] = -ADAM_LR * ((nm * c1) / (jnp.sqrt(nv * c2) + ADAM_EPS) + ADAM_WD * w_ref[...])

    blk = BS((tr, C), lambda i: (i, 0))
    sd = jax.ShapeDtypeStruct((R, C), F32)
    return _pcs(body, name=name, out_shape=(sd, sd, sd), grid=(R // tr,), in_specs=[blk] * 4, out_specs=(blk,) * 3,
                compiler_params=_cp(("parallel",)))(w, g, m, v)


def _adamw_layers(w, gs, m, v, name):
    n, R, C = w.shape
    tr = _row_tile(R, C)
    nr = R // tr
    c1 = 1.0 / (1.0 - ADAM_B1 ** ADAM_STEP)
    c2 = 1.0 / (1.0 - ADAM_B2 ** ADAM_STEP)

    def body(w_ref, m_ref, v_ref, *rest):
        g_refs, (go_ref, d_ref, nm_ref, nv_ref) = rest[:n], rest[n:]
        layer = pl.program_id(0)
        g_ = g_refs[0][...]
        for j in range(1, n):
            g_ = jnp.where(layer == j, g_refs[j][...], g_)
        nm = ADAM_B1 * m_ref[...] + (1.0 - ADAM_B1) * g_
        nv = ADAM_B2 * v_ref[...] + (1.0 - ADAM_B2) * (g_ * g_)
        go_ref[...] = g_
        nm_ref[...] = nm
        nv_ref[...] = nv
        d_ref[...] = -ADAM_LR * ((nm * c1) / (jnp.sqrt(nv * c2) + ADAM_EPS) + ADAM_WD * w_ref[...])

    def g_spec(j):
        return BS((tr, C), lambda layer, i: (jnp.where(layer == j, i, jnp.where(layer < j, 0, nr - 1)), 0))

    blk = BS((None, tr, C), lambda layer, i: (layer, i, 0))
    sd = jax.ShapeDtypeStruct((n, R, C), F32)
    return _pcs(body, name=name, out_shape=(sd, sd, sd, sd), grid=(n, nr), in_specs=[blk] * 3 + [g_spec(j) for j in range(n)],
                out_specs=(blk,) * 4, compiler_params=_cp(("arbitrary", "arbitrary")))(w, m, v, *gs)


def _sum_lead(a, out_dtype, name):
    n, R, C = a.shape
    tr = _row_tile(R, C * n, 2 << 20)

    def body(a_ref, o_ref):
        acc = a_ref[0].astype(F32)
        for k in range(1, n):
            acc = acc + a_ref[k].astype(F32)
        o_ref[...] = acc.astype(out_dtype)

    return pl.pallas_call(body, name=name, out_shape=jax.ShapeDtypeStruct((R, C), out_dtype), grid=(R // tr,),
                          in_specs=[BS((n, tr, C), lambda i: (0, i, 0))], out_specs=BS((tr, C), lambda i: (i, 0)),
                          compiler_params=_cp(("parallel",)))(a)


def _cctx_grad(parts, c_ctx, name):
    n, D = parts.shape

    def body(p_ref, c_ref, o_ref):
        d = jnp.sum(p_ref[...], axis=0, keepdims=True)
        c = c_ref[...]
        s = _sigmoid(c)
        o_ref[...] = d * s * (1.0 + c * (1.0 - s))

    return pl.pallas_call(body, name=name, out_shape=jax.ShapeDtypeStruct((1, D), F32))(parts, c_ctx.reshape(1, D))


def _me():
    return lax.axis_index("x"), lax.axis_index("y"), lax.axis_index("c")


def _aligned(v, n):
    return v if isinstance(v, int) else pl.multiple_of(v, n)


def _window(ref, r0, c0, R, C):
    rows = pl.ds(_aligned(r0, SUBLANES), R)
    if C == ref.shape[1]:
        return ref.at[rows, :]
    return ref.at[rows, pl.ds(_aligned(c0, LANES), C)]


def _allgather8(items, name):
    n = len(items)

    def body(*refs):
        srcs, dsts = refs[:n], refs[n:2 * n]
        send_sems, recv_sems, local_sems = refs[2 * n:]
        x, y, c = _me()
        me, sibling = (x, y, c), (x, y, 1 - c)
        chips = [(1 - x, y), (x, 1 - y), (1 - x, 1 - y)]

        def dwin(a, blk):
            (R, C), at = items[a][2], items[a][4]
            return _window(dsts[a], *at(*blk), R, C)

        def swin(a):
            (R, C), at = items[a][2], items[a][1]
            return _window(srcs[a], *at(*me), R, C)

        def copy(a, k, blk, to, src=None):
            return pltpu.make_async_remote_copy(
                src_ref=dwin(a, blk) if src is None else src, dst_ref=dwin(a, blk), send_sem=send_sems.at[7 * a + k],
                recv_sem=recv_sems.at[7 * a + k], device_id=to, device_id_type=MESH)

        mine = [pltpu.make_async_copy(swin(a), dwin(a, me), local_sems.at[a]) for a in range(n)]
        for cp in mine:
            cp.start()
        first = []
        for a in range(n):
            first.append(copy(a, 0, me, sibling, src=swin(a)))
            first += [copy(a, 1 + j, me, (*chip, c), src=swin(a)) for j, chip in enumerate(chips)]
        for cp in first:
            cp.start()
        passed = []
        for j, chip in enumerate(chips):
            for a in range(n):
                copy(a, 1 + j, (*chip, c), me).wait_recv()
                fwd = copy(a, 4 + j, (*chip, c), sibling)
                fwd.start()
                passed.append(fwd)
        for a in range(n):
            copy(a, 0, sibling, me).wait_recv()
            for j, chip in enumerate(chips):
                copy(a, 4 + j, (*chip, 1 - c), me).wait_recv()
        for cp in first + passed:
            cp.wait_send()
        for cp in mine:
            cp.wait()

    outs = pl.pallas_call(
        body, name=name, out_shape=tuple(jax.ShapeDtypeStruct(it[3], it[0].dtype) for it in items),
        in_specs=[ANY] * n, out_specs=tuple([ANY] * n),
        scratch_shapes=[pltpu.SemaphoreType.DMA((7 * n,)), pltpu.SemaphoreType.DMA((7 * n,)), pltpu.SemaphoreType.DMA((n,))],
    )(*[it[0] for it in items])
    return list(outs)


def _cast_place(w, kind, name, lead=None):
    R2, C = w.shape[-2:]
    tr = _row_tile(R2, C)
    nr = R2 // tr
    xi, yi, _ = _me()
    p_arr = (2 * xi + yi).astype(jnp.int32).reshape(1)
    if kind == "col":
        shape, o_spec = (R2, 4 * C), BS((tr, C), lambda i, p: (i, p[0]))
    else:
        shape, o_spec = (4 * R2, C), BS((tr, C), lambda i, p: (p[0] * nr + i, 0))

    def body(p_ref, w_ref, o_ref):
        o_ref[...] = w_ref[...].astype(BF16)

    return pl.pallas_call(
        body, name=name, out_shape=jax.ShapeDtypeStruct(shape, BF16),
        grid_spec=pltpu.PrefetchScalarGridSpec(
            num_scalar_prefetch=1, grid=(nr,), out_specs=o_spec,
            in_specs=[BS((tr, C), lambda i, p: (i, 0)) if lead is None else BS((None, tr, C), lambda i, p: (lead, i, 0))]),
        compiler_params=_cp(("parallel",)),
    )(p_arr, w)


def _remote(src, dst, sends, recvs, k, to):
    return pltpu.make_async_remote_copy(src_ref=src, dst_ref=dst, send_sem=sends.at[k], recv_sem=recvs.at[k], device_id=to,
                                        device_id_type=MESH)


def _gather_side(bufs, pieces, passed=(), pass_now=False):
    later = list(passed) + (list(pieces) if pass_now else [])
    base = 3 * len(pieces)

    def win(b, piece, p, pc):
        bi, (R, C), at, k, i = piece
        r0, c0 = at(p, pc)
        return _window(b[bi], r0 + i * (R // k), c0, R // k, C)

    def chips_of(x, y):
        return [(1 - x, y), (x, 1 - y), (1 - x, 1 - y)]

    def over_ici(b, sends, recvs, landing):
        x, y, c = _me()
        cps = []
        for m, piece in enumerate(pieces):
            for j, (px, py) in enumerate(chips_of(x, y)):
                w = win(b, piece, 2 * px + py, c) if landing else win(b, piece, 2 * x + y, c)
                cps.append(_remote(w, w, sends, recvs, 3 * m + j, (px, py, c)))
        return cps

    def to_sibling(b, sends, recvs, todo, landing):
        x, y, c = _me()
        cps = []
        for m, piece in enumerate(later):
            if piece in todo:
                for j, (px, py) in enumerate(chips_of(x, y)):
                    w = win(b, piece, 2 * px + py, 1 - c if landing else c)
                    cps.append(_remote(w, w, sends, recvs, base + 3 * m + j, (x, y, 1 - c)))
        return cps

    def start(b, sends, recvs):
        for cp in over_ici(b, sends, recvs, False) + to_sibling(b, sends, recvs, passed, False):
            cp.start()

    def finish(b, sends, recvs):
        for cp in over_ici(b, sends, recvs, True):
            cp.wait_recv()
        if pass_now:
            for cp in to_sibling(b, sends, recvs, pieces, False):
                cp.start()
        for cp in to_sibling(b, sends, recvs, later, True):
            cp.wait_recv()
        for cp in over_ici(b, sends, recvs, False) + to_sibling(b, sends, recvs, later, False):
            cp.wait_send()

    return _Side(bufs, 3 * (len(pieces) + len(later)), start, finish)


_RELS = [(dx, dy, dc) for dx in (0, 1) for dy in (0, 1) for dc in (0, 1)][1:]


def _rs_side(bufs, pieces):
    def flip(v, d):
        return 1 - v if d else v

    def copies(b, sends, recvs, landing):
        x, y, c = _me()
        dev = 4 * x + 2 * y + c
        cps = []
        for m, (gi, ri, (R, C), at, k, i) in enumerate(pieces):
            rows = R // k
            for t, (dx, dy, dc) in enumerate(_RELS):
                tx, ty, tc = flip(x, dx), flip(y, dy), flip(c, dc)
                if landing:
                    theirs = b[ri].at[4 * tx + 2 * ty + tc, pl.ds(i * rows, rows), :]
                    cps.append(_remote(theirs, theirs, sends, recvs, 7 * m + t, (tx, ty, tc)))
                else:
                    r0, c0 = at(2 * tx + ty, tc)
                    src = _window(b[gi], r0 + i * rows, c0, rows, C)
                    cps.append(_remote(src, b[ri].at[dev, pl.ds(i * rows, rows), :], sends, recvs, 7 * m + t, (tx, ty, tc)))
        return cps

    def start(b, sends, recvs):
        for cp in copies(b, sends, recvs, False):
            cp.start()

    def finish(b, sends, recvs):
        for cp in copies(b, sends, recvs, True):
            cp.wait_recv()
        for cp in copies(b, sends, recvs, False):
            cp.wait_send()

    return _Side(bufs, 7 * len(pieces), start, finish)


def _comm_only(side, name):
    n = len(side.bufs)

    def body(*refs):
        bufs, (sends, recvs) = refs[n:2 * n], refs[2 * n:]
        side.start(bufs, sends, recvs)
        side.finish(bufs, sends, recvs)

    outs = pl.pallas_call(
        body, name=name, out_shape=tuple(jax.ShapeDtypeStruct(b.shape, b.dtype) for b in side.bufs),
        in_specs=[ANY] * n, out_specs=tuple([ANY] * n), input_output_aliases={a: a for a in range(n)},
        scratch_shapes=[pltpu.SemaphoreType.DMA((side.nsem,)), pltpu.SemaphoreType.DMA((side.nsem,))],
    )(*side.bufs)
    return list(outs)


def _rs_sum8(grad, recv, win, kind, name):
    R, C = win
    tr = _row_tile(R, C)
    nr = R // tr
    xi, yi, ci = _me()
    s_arr = jnp.stack([4 * xi + 2 * yi + ci, ci]).astype(jnp.int32)

    if kind == "col":
        g_spec = BS((tr, C), lambda i, s: (s[1] * nr + i, s[0] // 2))
    else:
        g_spec = BS((tr, C), lambda i, s: (s[0] * nr + i, 0))

    def body(s_ref, g_ref, *refs):
        acc = g_ref[...].astype(F32)
        for r_ref in refs[:7]:
            acc = acc + r_ref[0].astype(F32)
        refs[7][...] = acc

    def other(t):
        return BS((1, tr, C), lambda i, s: (jnp.bitwise_xor(s[0], t), i, 0))

    return pl.pallas_call(
        body, name=name, out_shape=jax.ShapeDtypeStruct((2 * R, C), F32),
        grid_spec=pltpu.PrefetchScalarGridSpec(
            num_scalar_prefetch=1, grid=(nr,), in_specs=[g_spec] + [other(t) for t in range(1, 8)],
            out_specs=BS((tr, C), lambda i, s: (s[1] * nr + i, 0))),
        compiler_params=_cp(("parallel",)),
    )(s_arr, grad, *([recv] * 7))


def _share_side(shards):
    def half(b, a, pc):
        R = b[a].shape[0] // 2
        return b[a].at[pl.ds(pl.multiple_of(pc * R, SUBLANES), R), :]

    def start(b, sends, recvs):
        x, y, c = _me()
        for a in range(len(shards)):
            _remote(half(b, a, c), half(b, a, c), sends, recvs, a, (x, y, 1 - c)).start()

    def finish(b, sends, recvs):
        x, y, c = _me()
        for a in range(len(shards)):
            _remote(half(b, a, 1 - c), half(b, a, 1 - c), sends, recvs, a, (x, y, 1 - c)).wait_recv()
        for a in range(len(shards)):
            _remote(half(b, a, c), half(b, a, c), sends, recvs, a, (x, y, 1 - c)).wait_send()

    return _Side(shards, len(shards), start, finish)


BLOB_ALIGN = SUBLANES * LANES


def _pack(arrs):
    flat = jnp.concatenate([a.reshape(-1).astype(F32) for a in arrs])
    n = flat.shape[0]
    padded = -(-n // BLOB_ALIGN) * BLOB_ALIGN
    return jnp.pad(flat, (0, padded - n)).reshape(padded // LANES, LANES)


def _unpack(flat, shapes):
    out, off = [], 0
    for s in shapes:
        n = math.prod(s)
        out.append(flat[..., off:off + n].reshape(flat.shape[:-1] + tuple(s)))
        off += n
    return out


def _gather_blob(blob, name):
    r = blob.shape[0]
    at = lambda px, py, pc: ((4 * px + 2 * py + pc) * r, 0)
    (out,) = _allgather8([(blob, lambda px, py, pc: (0, 0), (r, LANES), (8 * r, LANES), at)], name)
    return out.reshape(8, r * LANES)


def _conv_ffn_fwd(mm, full, layer, X, mods, n2g, conv_w, conv_b, L, tb, nlat, ch, tag):
    sh2, sc2, g2 = mods[3], mods[4], mods[5]
    h2 = _norm_mod_fwd(X, n2g, sh2, sc2, tb, nlat, tag + "_norm2")
    zf = mm(h2, full[f"ffn_w_up{layer}"], "nn", BF16, tag + "_up", tm=(544, 512), tn=(2816, 512))
    f = _ffn_act_fwd(zf, conv_w, conv_b, L, ch, tag + "_act")
    Xn, yf = mm(f, full[f"ffn_w_down{layer}"], "nn", F32, tag + "_down", tm=(544, 512), tn=(512,), res=(X, g2, L))
    return Xn, (X, h2, zf, f, yf)


def _conv_ffn_bwd(mm, full, gbuf, layer, dXn, saved, mods, n2g, conv_w, conv_b, L, tb, nlat, ch, tag):
    X, h2, zf, f, yf = saved
    sc2, g2 = mods[4], mods[5]
    w_up, w_down = full[f"ffn_w_up{layer}"], full[f"ffn_w_down{layer}"]
    dy, dg2 = _gate_bwd(dXn, yf, g2, tb, nlat, tag + "_dres2")
    df = mm(dy, w_down, "nt", BF16, tag + "_ddown_x", tm=(544, 512), tn=(2816, 512))
    gbuf[f"ffn_w_down{layer}"] = mm(f, dy, "tn", BF16, tag + "_ddown_w", tm=(512,), tn=(1024, 512))
    dzf, dcw, dcb = _ffn_act_bwd(zf, df, conv_w, conv_b, L, ch, tag + "_dact")
    gbuf[f"ffn_w_up{layer}"] = mm(h2, dzf, "tn", BF16, tag + "_dup_w", tm=(1024, 512), tn=(512,), outer="i", halves=True)
    dh2 = mm(dzf, w_up, "nt", BF16, tag + "_dup_x", tm=(1088, 1024, 512), tn=(1024,), tk=(2816, 512), halves=True)
    dX, dn2g, dsh2, dsc2 = _norm_mod_bwd(X, n2g, sc2, dh2, dXn, tb, nlat, tag + "_dnorm2")
    return dX, dict(n2g=dn2g, sh2=dsh2, sc2=dsc2, g2=dg2, cw=dcw, cb=dcb)


def kernel(x, c, ctx, c_ctx, norm1_g, norm2_g, w_ada, b_ada, ab_w_in, ab_b_in, a_ln_g, a_ln_b, a_w_s, a_b_s, b_conv_w, b_conv_b, b_ln_g, b_ln_b, ab_w_out, mla_w_in, mla_q_norm_g, mla_w_uq, mla_kv_norm_g, mla_w_ukv, mla_w_o, ffn_w_up, ffn_conv_w, ffn_conv_b, ffn_w_down, final_norm_g, loss_target, m_c_ctx, m_norm1_g, m_norm2_g, m_w_ada, m_b_ada, m_ab_w_in, m_ab_b_in, m_a_ln_g, m_a_ln_b, m_a_w_s, m_a_b_s, m_b_conv_w, m_b_conv_b, m_b_ln_g, m_b_ln_b, m_ab_w_out, m_mla_w_in, m_mla_q_norm_g, m_mla_w_uq, m_mla_kv_norm_g, m_mla_w_ukv, m_mla_w_o, m_ffn_w_up, m_ffn_conv_w, m_ffn_conv_b, m_ffn_w_down, m_final_norm_g, v_c_ctx, v_norm1_g, v_norm2_g, v_w_ada, v_b_ada, v_ab_w_in, v_ab_b_in, v_a_ln_g, v_a_ln_b, v_a_w_s, v_a_b_s, v_b_conv_w, v_b_conv_b, v_b_ln_g, v_b_ln_b, v_ab_w_out, v_mla_w_in, v_mla_q_norm_g, v_mla_w_uq, v_mla_kv_norm_g, v_mla_w_ukv, v_mla_w_o, v_ffn_w_up, v_ffn_conv_w, v_ffn_conv_b, v_ffn_w_down, v_final_norm_g):
    W = dict(c_ctx=c_ctx, norm1_g=norm1_g, norm2_g=norm2_g, w_ada=w_ada, b_ada=b_ada, ab_w_in=ab_w_in, ab_b_in=ab_b_in, a_ln_g=a_ln_g,
             a_ln_b=a_ln_b, a_w_s=a_w_s, a_b_s=a_b_s, b_conv_w=b_conv_w, b_conv_b=b_conv_b, b_ln_g=b_ln_g, b_ln_b=b_ln_b,
             ab_w_out=ab_w_out, mla_w_in=mla_w_in, mla_q_norm_g=mla_q_norm_g, mla_w_uq=mla_w_uq, mla_kv_norm_g=mla_kv_norm_g,
             mla_w_ukv=mla_w_ukv, mla_w_o=mla_w_o, ffn_w_up=ffn_w_up, ffn_conv_w=ffn_conv_w, ffn_conv_b=ffn_conv_b,
             ffn_w_down=ffn_w_down, final_norm_g=final_norm_g)
    MOM = dict(c_ctx=m_c_ctx, norm1_g=m_norm1_g, norm2_g=m_norm2_g, w_ada=m_w_ada, b_ada=m_b_ada, ab_w_in=m_ab_w_in, ab_b_in=m_ab_b_in,
               a_ln_g=m_a_ln_g, a_ln_b=m_a_ln_b, a_w_s=m_a_w_s, a_b_s=m_a_b_s, b_conv_w=m_b_conv_w, b_conv_b=m_b_conv_b,
               b_ln_g=m_b_ln_g, b_ln_b=m_b_ln_b, ab_w_out=m_ab_w_out, mla_w_in=m_mla_w_in, mla_q_norm_g=m_mla_q_norm_g,
               mla_w_uq=m_mla_w_uq, mla_kv_norm_g=m_mla_kv_norm_g, mla_w_ukv=m_mla_w_ukv, mla_w_o=m_mla_w_o, ffn_w_up=m_ffn_w_up,
               ffn_conv_w=m_ffn_conv_w, ffn_conv_b=m_ffn_conv_b, ffn_w_down=m_ffn_w_down, final_norm_g=m_final_norm_g)
    VAR = dict(c_ctx=v_c_ctx, norm1_g=v_norm1_g, norm2_g=v_norm2_g, w_ada=v_w_ada, b_ada=v_b_ada, ab_w_in=v_ab_w_in, ab_b_in=v_ab_b_in,
               a_ln_g=v_a_ln_g, a_ln_b=v_a_ln_b, a_w_s=v_a_w_s, a_b_s=v_a_b_s, b_conv_w=v_b_conv_w, b_conv_b=v_b_conv_b,
               b_ln_g=v_b_ln_g, b_ln_b=v_b_ln_b, ab_w_out=v_ab_w_out, mla_w_in=v_mla_w_in, mla_q_norm_g=v_mla_q_norm_g,
               mla_w_uq=v_mla_w_uq, mla_kv_norm_g=v_mla_kv_norm_g, mla_w_ukv=v_mla_w_ukv, mla_w_o=v_mla_w_o, ffn_w_up=v_ffn_w_up,
               ffn_conv_w=v_ffn_conv_w, ffn_conv_b=v_ffn_conv_b, ffn_w_down=v_ffn_w_down, final_norm_g=v_final_norm_g)
    ORDER = list(W.keys())

    L, D = x.shape[1], x.shape[2]
    CT = ctx.shape[1]
    T = L + CT
    AW, BW = a_ln_g.shape[-1], b_ln_g.shape[-1]
    AH = a_w_s.shape[1]
    QL, KVL = 4 * mla_q_norm_g.shape[-1], 4 * mla_kv_norm_g.shape[-1]
    H = 4 * mla_w_o.shape[1] // HEAD
    HS = H // 4
    DFF = ffn_conv_b.shape[-1]
    NA = w_ada.shape[-1]
    tb = 256 if (L % 256 == 0 and CT % 256 == 0) else 128
    nlat = L // tb
    ch = tb
    tq = 256 if L >= 512 else 128
    xi, yi, ci = _me()
    p_me = 2 * xi + yi
    dev = 4 * xi + 2 * yi + ci

    shard_small = [c[0], mla_q_norm_g[0], mla_kv_norm_g[0], b_conv_w[0], ffn_conv_w]
    g0 = _gather_blob(_pack(shard_small), "gather_small")
    c_all, qg_s, kvg_s, bcw_s, fcw_s = _unpack(g0, [a.shape for a in shard_small])
    per_chip = lambda a: a[0::2]
    qg = per_chip(qg_s).reshape(QL)
    kvg = per_chip(kvg_s).reshape(KVL)
    bcw = jnp.concatenate(list(per_chip(bcw_s)), axis=-1)
    fcw = jnp.concatenate(list(per_chip(fcw_s)), axis=-1)
    c16 = jnp.concatenate([c_all, c_ctx[None], jnp.zeros((7, D), F32)], axis=0)

    ms = []
    for i in range(2):
        bias = lax.dynamic_slice(b_ada[i], (p_me * NA,), (NA,))
        ms.append(_mm(c16, w_ada, "nn", F32, f"ada{i}", tm=(16,), tn=(512,), bias=bias, silu_a=True, b_lead=i))
    ms = jnp.concatenate(ms, axis=0)
    (mods_all,) = _allgather8(
        [(ms, lambda px, py, pc: (pc * 16, 0), (16, NA), (32, 4 * NA), lambda px, py, pc: (pc * 16, (2 * px + py) * NA))], "gather_mods")
    mods_all = mods_all.reshape(2, 16, N_MOD, D)
    mods = []
    for i in range(2):
        lat = lax.dynamic_index_in_dim(mods_all[i], dev, axis=0, keepdims=False)
        both = jnp.stack([lat, mods_all[i, 8]], axis=0)
        mods.append([both[:, k][:, None, :] for k in range(N_MOD)])

    def pad_uq(w):
        w = w.reshape(w.shape[0], HS, HEAD + ROPE)
        return jnp.pad(w, ((0, 0), (0, 0), (0, QHEAD - HEAD - ROPE))).reshape(w.shape[0], HS * QHEAD)

    MI = QL + KVL + LANES
    big = [
        ("ab_w_in", ab_w_in, 0, "col"), ("ab_w_out", ab_w_out, 0, "row"),
        ("mla_w_in", jnp.pad(mla_w_in[0], ((0, 0), (0, MI - mla_w_in.shape[-1]))), None, "row"),
        ("mla_w_uq", pad_uq(mla_w_uq[0]), None, "col"), ("mla_w_ukv", mla_w_ukv, 0, "col"), ("mla_w_o", mla_w_o, 0, "row"),
        ("ffn_w_up0", ffn_w_up, 0, "col"), ("ffn_w_up1", ffn_w_up, 1, "col"),
        ("ffn_w_down0", ffn_w_down, 0, "row"), ("ffn_w_down1", ffn_w_down, 1, "row"),
    ]
    full, wins = {}, {}
    for nm, w, lead, kind in big:
        R, C = w.shape[-2] // 2, w.shape[-1]
        if kind == "col":
            wins[nm] = ((R, C), lambda p, pc, R=R, C=C: (pc * R, p * C), kind)
        else:
            wins[nm] = ((R, C), lambda p, pc, R=R, C=C: ((2 * p + pc) * R, 0), kind)
        full[nm] = _cast_place(w, kind, "cast_" + nm, lead)
    names = [b[0] for b in big]

    up0 = lambda *parts: [("ffn_w_up0", 8, i) for i in parts]
    mla_a = [("ffn_w_down0", 1, 0), ("mla_w_in", 1, 0), ("mla_w_uq", 1, 0)]
    mla_b = [("mla_w_ukv", 1, 0), ("mla_w_o", 1, 0)]
    gather_plan = {
        "l0_norm1": ([("ab_w_out", 1, 0)], []), "l0_in": (up0(0, 1), [("ab_w_out", 1, 0)]), "l0_gmlp": (up0(2), up0(0, 1)),
        "l0_conv": (up0(3, 4), up0(2)), "l0_lnsilu": (up0(5), up0(3, 4)), "l0_out": (up0(6, 7), up0(5)),
        "l0_ffn_norm2": ([], up0(6, 7)), "l0_ffn_up": (mla_a, []), "l0_ffn_act": ([], mla_a), "l0_ffn_down": (mla_b, []),
        "l1_norm1": ([], mla_b),
    }
    u0 = lambda *parts: [("ffn_w_up0", 16, i) for i in parts]
    reduce_plan = {
        "l1_ffn_dact": [("ffn_w_down1", 2, 0)], "l1_ffn_dup_w": [("ffn_w_down1", 2, 1)], "l1_ffn_dup_x": [("ffn_w_up1", 4, 0)],
        "l1_dattn": [("ffn_w_up1", 4, 1), ("ffn_w_up1", 4, 2), ("ffn_w_up1", 4, 3), ("mla_w_o", 1, 0)],
        "l0_ffn_ddown_x": [("mla_w_ukv", 1, 0), ("mla_w_uq", 1, 0)], "l0_ffn_ddown_w": [("mla_w_in", 1, 0)],
        "l0_ffn_dact": [("ffn_w_down0", 2, 0)], "l0_ffn_dup_w": [("ffn_w_down0", 2, 1)],
        "l0_ffn_dup_x": u0(0, 1, 2, 3, 4, 5, 6), "l0_ffn_dnorm2": u0(7), "l0_dres1": u0(8), "l0_dout_w": u0(9),
        "l0_dout_x": [("ab_w_out", 2, 0)], "l0_dgmlp": u0(10, 11), "l0_dlnsilu": u0(12),
        "l0_dconv": u0(13, 14) + [("ab_w_out", 2, 1)], "l0_din_w": u0(15),
        "l0_din_x": [("ab_w_in", 4, 0), ("ab_w_in", 4, 1)], "l0_dnorm1": [("ab_w_in", 4, 2), ("ab_w_in", 4, 3)],
    }
    gbuf, rbuf = {}, {}

    def gather_side(req, passed=(), pass_now=False):
        nms = list(dict.fromkeys(nm for nm, _, _ in list(req) + list(passed)))
        piece = lambda nm, k, i: (nms.index(nm), wins[nm][0], wins[nm][1], k, i)
        side = _gather_side([full[nm] for nm in nms], [piece(*r) for r in req], [piece(*r) for r in passed], pass_now)

        def commit(bufs):
            full.update(zip(nms, bufs))

        return side, commit

    def reduce_side(req):
        nms = list(dict.fromkeys(nm for nm, _, _ in req))
        for nm in nms:
            if nm not in rbuf:
                rbuf[nm] = lax.empty((8,) + wins[nm][0], BF16)
        n = len(nms)
        side = _rs_side([gbuf[nm] for nm in nms] + [rbuf[nm] for nm in nms],
                        [(nms.index(nm), n + nms.index(nm), wins[nm][0], wins[nm][1], k, i) for nm, k, i in req])

        def commit(bufs):
            gbuf.update(zip(nms, bufs[:n]))
            rbuf.update(zip(nms, bufs[n:]))

        return side, commit

    _carry_plan.clear()
    _carry_plan.update({name: functools.partial(gather_side, req, passed) for name, (req, passed) in gather_plan.items()})
    _carry_plan["l1_attn"] = functools.partial(gather_side, [("ffn_w_up1", 1, 0), ("ffn_w_down1", 1, 0)], (), True)
    _carry_plan.update({name: functools.partial(reduce_side, req) for name, req in reduce_plan.items()})
    mm, attn_fwd, attn_bwd = _mm, _attn_fwd, _attn_bwd

    side, commit = gather_side([("ab_w_in", 1, 0)], (), True)
    commit(_comm_only(side, "gather_first"))

    X0 = jnp.concatenate([x[0], ctx[0]], axis=0)
    m0, m1 = mods[0], mods[1]
    h1 = _norm_mod_fwd(X0, norm1_g[0], m0[0], m0[1], tb, nlat, "l0_norm1")
    z = mm(h1, full["ab_w_in"], "nn", BF16, "l0_in", tm=(544, 512), tn=(1024, 512), bias=ab_b_in[0])
    bs_full = jnp.broadcast_to(a_b_s[0][:, :, None], (AH, CHUNK, CHUNK))
    ya = _gmlp_fwd(z, a_ln_g[0], a_ln_b[0], a_w_s[0], bs_full, tb, "l0_gmlp")
    hc = _glu_conv_fwd(z, 2 * AW, bcw, b_conv_b[0], L, ch, "l0_conv")
    yb = _ln_silu_fwd(hc, b_ln_g[0], b_ln_b[0], tb, "l0_lnsilu")
    yab = jnp.concatenate([ya, yb], axis=1)
    X1, y0 = mm(yab, full["ab_w_out"], "nn", F32, "l0_out", tm=(544, 512), tn=(1024, 512), res=(X0, m0[2], L))
    X2, ffn0 = _conv_ffn_fwd(mm, full, 0, X1, m0, norm2_g[0], fcw[0], ffn_conv_b[0], L, tb, nlat, ch, "l0_ffn")

    tabs = _rope_tables(L, T)
    hm = _norm_mod_fwd(X2, norm1_g[1], m1[0], m1[1], tb, nlat, "l1_norm1")
    zm = mm(hm, full["mla_w_in"], "nn", F32, "l1_in", tm=(544, 512), tn=(MI,))
    cqn, ckvn, kpe = _mla_prep_fwd(zm, qg, kvg, tabs, tb, "l1_prep")
    q = mm(cqn, full["mla_w_uq"], "nn", BF16, "l1_uq", tm=(512,), tn=(1024, 512), rows=L)
    kvh = mm(ckvn, full["mla_w_ukv"], "nn", BF16, "l1_ukv", tm=(544, 512), tn=(1024, 512))
    tq2 = 2 * tq if L % (2 * tq) == 0 and L > 2 * tq else tq
    wide = L % (4 * tq) == 0 and L > 4 * tq
    o, lse = attn_fwd(q, kvh, kpe, tabs, L, 4 * tq if wide else tq2, 4 if wide else 2, "l1_attn")
    m1_lat = [a[:1] for a in m1]
    X3, yl = mm(o, full["mla_w_o"], "nn", F32, "l1_o", tm=(512,), tn=(1024, 512), res=(X2, m1_lat[2], L))
    X4, ffn1 = _conv_ffn_fwd(mm, full, 1, X3, m1_lat, norm2_g[1], fcw[1], ffn_conv_b[1], L, tb, nlat, ch, "l1_ffn")
    loss_acc, dX4, dfinal = _final_loss(X4, final_norm_g, loss_target[0], tb, "loss")

    dX3, gf1 = _conv_ffn_bwd(mm, full, gbuf, 1, dX4, ffn1, m1_lat, norm2_g[1], fcw[1], ffn_conv_b[1], L, tb, nlat, ch, "l1_ffn")
    dyl, dg1_1 = _gate_bwd(dX3, yl, m1_lat[2], tb, nlat, "l1_dres1")
    do = mm(dyl, full["mla_w_o"], "nt", BF16, "l1_do_x", tm=(512,), tn=(1024, 512))
    gbuf["mla_w_o"] = mm(o, dyl, "tn", BF16, "l1_do_w", tm=(512,), tn=(1024, 512))
    dq, dkv, dkpe = attn_bwd(q, kvh, kpe, tabs, o, lse, do, L, tq2, "l1_dattn")
    dckvn = mm(dkv, full["mla_w_ukv"], "nt", BF16, "l1_dukv_x", tm=(544, 512), tn=(KVL,), tk=(2048, 512))
    gbuf["mla_w_ukv"] = mm(ckvn, dkv, "tn", BF16, "l1_dukv_w", tm=(512,), tn=(1024, 512))
    dcqn = mm(dq, full["mla_w_uq"], "nt", BF16, "l1_duq_x", tm=(512,), tn=(QL,), tk=(2048, 512))
    gbuf["mla_w_uq"] = mm(cqn, dq, "tn", BF16, "l1_duq_w", tm=(QL,), tn=(1024, 512), rows=L)
    dcqn = jnp.concatenate([dcqn, jnp.zeros((CT, QL), BF16)], axis=0)
    dzm, dqg, dkvg = _mla_prep_bwd(zm, qg, kvg, tabs, dcqn, dckvn, dkpe, tb, "l1_dprep")
    dhm = mm(dzm, full["mla_w_in"], "nt", BF16, "l1_din_x", tm=(544, 512), tn=(1024, 512))
    gbuf["mla_w_in"] = mm(hm, dzm, "tn", BF16, "l1_din_w", tm=(512,), tn=(MI,))
    dX2, dn1g_1, dsh1_1, dsc1_1 = _norm_mod_bwd(X2, norm1_g[1], m1[1], dhm, dX3, tb, nlat, "l1_dnorm1")

    dX1, gf0 = _conv_ffn_bwd(mm, full, gbuf, 0, dX2, ffn0, m0, norm2_g[0], fcw[0], ffn_conv_b[0], L, tb, nlat, ch, "l0_ffn")
    dy0, dg1_0 = _gate_bwd(dX1, y0, m0[2], tb, nlat, "l0_dres1")
    gbuf["ab_w_out"] = mm(yab, dy0, "tn", BF16, "l0_dout_w", tm=(512,), tn=(1024, 512))
    dyab = mm(dy0, full["ab_w_out"], "nt", BF16, "l0_dout_x", tm=(544, 512), tn=(1024, 512))
    dzu, dzv, dlnag, dlnab, dws, dbs, csu, csv = _gmlp_bwd(z, dyab, a_ln_g[0], a_ln_b[0], a_w_s[0], bs_full, tb, "l0_dgmlp")
    dhc, dlnbg, dlnbb = _ln_silu_bwd(hc, dyab, AW, b_ln_g[0], b_ln_b[0], tb, "l0_dlnsilu")
    dza, dzg, dbcw, dbcb, csa, csg = _glu_conv_bwd(z, 2 * AW, dhc, bcw, L, ch, "l0_dconv")
    dz = jnp.concatenate([dzu, dzv, dza, dzg], axis=1)
    dbin = jnp.concatenate([csu, csv, csa, csg], axis=1)
    gbuf["ab_w_in"] = mm(h1, dz, "tn", BF16, "l0_din_w", tm=(512,), tn=(1024, 512))
    dh1 = mm(dz, full["ab_w_in"], "nt", BF16, "l0_din_x", tm=(544, 512), tn=(1024, 512), tk=(2048, 512))
    dX0, dn1g_0, dsh1_0, dsc1_0 = _norm_mod_bwd(X0, norm1_g[0], m0[1], dh1, dX1, tb, nlat, "l0_dnorm1")
    grad_x = dX0[:L][None]

    halves = [_rs_sum8(gbuf[nm], rbuf[nm], wins[nm][0], wins[nm][2], "rs_sum_" + nm) for nm in names]

    def grp6(l, sh1, sc1, g1, f):
        G = sh1.shape[0]
        pad = lambda a: jnp.concatenate([a, jnp.zeros((G - a.shape[0],) + a.shape[1:], F32)], axis=0) if a.shape[0] < G else a
        return jnp.concatenate([pad(a) for a in (sh1, sc1, g1, f["sh2"], f["sc2"], f["g2"])], axis=1)

    dm0 = grp6(0, dsh1_0, dsc1_0, dg1_0, gf0)
    dm1 = grp6(1, dsh1_1, dsc1_1, dg1_1, gf1)
    dmods = jnp.stack([dm0, dm1], axis=0)
    small = [
        jnp.concatenate([dn1g_0, dn1g_1], axis=0), jnp.concatenate([gf0["n2g"], gf1["n2g"]], axis=0), dbin, dlnag, dlnab, dws,
        dbs, dbcw, dbcb, dlnbg, dlnbb, dqg, dkvg, jnp.stack([gf0["cw"], gf1["cw"]], axis=0),
        jnp.concatenate([gf0["cb"], gf1["cb"]], axis=0), dfinal, dmods[:, 1],
    ]
    small_shapes = [a.shape for a in small]
    lat_shape = dmods[:, 0].shape
    blob = _pack(small + [dmods[:, 0], loss_acc[0, :1]])
    gathered = _gather_blob(blob, "gather_grads")
    summed = _sum_lead(gathered.reshape(8, -1, LANES), F32, "sum_grads").reshape(-1)
    (dn1g, dn2g, dbin_s, dlnag_s, dlnab_s, dws_s, dbs_s, dbcw_s, dbcb_s, dlnbg_s, dlnbb_s, dqg_s, dkvg_s, dfcw_s, dfcb_s, dfinal_s,
     dmods_ctx, dmods_lat_sum, loss) = _unpack(summed, small_shapes + [lat_shape, (1,)])
    loss = loss.reshape(())
    n_small = sum(math.prod(s) for s in small_shapes)
    dmods_lat = gathered[:, n_small:n_small + math.prod(lat_shape)].reshape((8,) + lat_shape)

    grad_w_ada, dc_parts = [], []
    for i in range(2):
        dm16 = jnp.concatenate([dmods_lat[:, i].reshape(8, N_MOD * D), dmods_ctx[i].reshape(1, N_MOD * D),
                                jnp.zeros((7, N_MOD * D), F32)], axis=0)
        dm16_s = lax.dynamic_slice(dm16, (0, p_me * NA), (16, NA))
        grad_w_ada.append(_mm(c16, dm16_s, "tn", F32, f"dada{i}_w", tm=(1024, 512), tn=(1024, 512), silu_a=True))
        dc_parts.append(_mm(dm16_s, w_ada, "nt", F32, f"dada{i}_c", tm=(16,), tn=(512,), tk=(1024, 512), b_lead=i))
    grad_b_ada = dmods_lat_sum.reshape(2, N_MOD * D) + dmods_ctx.reshape(2, N_MOD * D)
    dc_blob = _pack([dc_parts[0][8] + dc_parts[1][8]])
    dc_all = _gather_blob(dc_blob, "gather_dc")[0::2, :D]
    grad_c_ctx = _cctx_grad(dc_all, c_ctx, "dcctx").reshape(D)

    gshard = dict(zip(names, _comm_only(_share_side(halves), "rs_share")))
    two = lambda a: a.reshape(-1, a.shape[-1])
    ada_update = _adamw_layers(w_ada, grad_w_ada, m_w_ada, v_w_ada, "adamw_w_ada")

    def my_cols(a, axis, n):
        return lax.dynamic_slice_in_dim(a, p_me * n, n, axis=axis)

    unpad_uq = lambda g: g.reshape(QL, HS, QHEAD)[:, :, :HEAD + ROPE].reshape(QL, HS * (HEAD + ROPE))
    grads = dict(
        c_ctx=grad_c_ctx, norm1_g=dn1g, norm2_g=dn2g, b_ada=grad_b_ada, ab_w_in=gshard["ab_w_in"][None],
        ab_b_in=dbin_s, a_ln_g=dlnag_s, a_ln_b=dlnab_s, a_w_s=dws_s[None], a_b_s=dbs_s.reshape(1, AH, CHUNK),
        b_conv_w=my_cols(dbcw_s, 1, BW // 4)[None], b_conv_b=dbcb_s, b_ln_g=dlnbg_s, b_ln_b=dlnbb_s,
        ab_w_out=gshard["ab_w_out"][None], mla_w_in=gshard["mla_w_in"][:, :mla_w_in.shape[-1]][None],
        mla_q_norm_g=my_cols(dqg_s, 1, QL // 4), mla_w_uq=unpad_uq(gshard["mla_w_uq"])[None],
        mla_kv_norm_g=my_cols(dkvg_s, 1, KVL // 4), mla_w_ukv=gshard["mla_w_ukv"][None], mla_w_o=gshard["mla_w_o"][None],
        ffn_conv_w=my_cols(dfcw_s, 2, DFF // 4), ffn_conv_b=dfcb_s, final_norm_g=dfinal_s.reshape(D),
    )
    LAYERED = ("w_ada", "ffn_w_up", "ffn_w_down")
    grads = {k: grads[k].reshape(W[k].shape) for k in ORDER if k not in LAYERED}

    BIG = ("ab_w_in", "ab_w_out", "mla_w_in", "mla_w_uq", "mla_w_ukv", "mla_w_o") + LAYERED
    delta, new_m, new_v = {}, {}, {}
    for k in BIG:
        if k == "w_ada":
            grads[k], delta[k], new_m[k], new_v[k] = ada_update
        elif k in LAYERED:
            grads[k], delta[k], new_m[k], new_v[k] = _adamw_layers(W[k], [gshard[k + "0"], gshard[k + "1"]], MOM[k], VAR[k], "adamw_" + k)
        else:
            upd = _adamw(two(W[k]), two(grads[k]), two(MOM[k]), two(VAR[k]), "adamw_" + k)
            delta[k], new_m[k], new_v[k] = (a.reshape(W[k].shape) for a in upd)
    SMALL = [k for k in ORDER if k not in BIG]
    small_upd = _adamw(_pack([W[k] for k in SMALL]), _pack([grads[k] for k in SMALL]), _pack([MOM[k] for k in SMALL]),
                       _pack([VAR[k] for k in SMALL]), "adamw_small")
    shapes = [W[k].shape for k in SMALL]
    for k, d_k, m_k, v_k in zip(SMALL, *[_unpack(a.reshape(-1), shapes) for a in small_upd]):
        delta[k], new_m[k], new_v[k] = d_k, m_k, v_k

    return (loss, grad_x, *[grads[k] for k in ORDER], *[delta[k] for k in ORDER], *[new_m[k] for k in ORDER],
            *[new_v[k] for k in ORDER])
```

```python
import functools
import math

import jax
import jax.numpy as jnp
from jax import lax
from jax.experimental import pallas as pl
from jax.experimental.pallas import tpu as pltpu

F32 = jnp.float32
BF16 = jnp.bfloat16
MESH = pl.DeviceIdType.MESH
ANY = pl.BlockSpec(memory_space=pl.ANY)

EPS = 1e-6
N_MOD = 6
CHUNK = 128
HEAD = 128
ROPE = 64
QHEAD = 2 * HEAD
GRID_W = 64
ROPE_THETA = 10000.0
B_CONV = 31
FFN_CONV = 3
ADAM_LR, ADAM_B1, ADAM_B2, ADAM_EPS, ADAM_WD, ADAM_STEP = 0.001, 0.9, 0.999, 1e-08, 0.01, 10

V7X_VMEM_LIMIT = 56 * 1024 * 1024
LANES = 128
SUBLANES = 8
CONV_PAD = 16

BS = pl.BlockSpec


def _cp(sem=None, vmem=V7X_VMEM_LIMIT):
    return pltpu.CompilerParams(dimension_semantics=sem, vmem_limit_bytes=vmem)


class _Side:
    def __init__(self, bufs, nsem, start, finish):
        self.bufs, self.nsem, self.start, self.finish = list(bufs), nsem, start, finish


_carry_plan = {}


def _call(body, *, name, out_shape, grid, in_specs, out_specs, operands, sem, scratch=()):
    if name not in _carry_plan:
        return pl.pallas_call(body, name=name, out_shape=out_shape, grid=grid, in_specs=in_specs, out_specs=out_specs,
                              scratch_shapes=list(scratch), compiler_params=_cp(sem))(*operands)
    side, commit = _carry_plan[name]()
    multi = isinstance(out_shape, (tuple, list))
    outs = tuple(out_shape) if multi else (out_shape,)
    ospecs = tuple(out_specs) if multi else (out_specs,)
    n_in, n_out, n_scr, n_buf = len(in_specs), len(outs), len(scratch), len(side.bufs)

    def body2(*refs):
        o0 = n_in + n_buf
        s0 = o0 + n_out + n_buf
        bufs = refs[o0 + n_out:s0]
        sends, recvs = refs[s0 + n_scr:]
        first = last = None
        for d, g in enumerate(grid):
            pid = pl.program_id(d)
            first = (pid == 0) if first is None else jnp.logical_and(first, pid == 0)
            last = (pid == g - 1) if last is None else jnp.logical_and(last, pid == g - 1)

        @pl.when(first)
        def _():
            side.start(bufs, sends, recvs)

        body(*refs[:n_in], *refs[o0:o0 + n_out], *refs[s0:s0 + n_scr])

        @pl.when(last)
        def _():
            side.finish(bufs, sends, recvs)

    res = pl.pallas_call(
        body2, name=name, out_shape=outs + tuple(jax.ShapeDtypeStruct(b.shape, b.dtype) for b in side.bufs), grid=grid,
        in_specs=list(in_specs) + [ANY] * n_buf, out_specs=ospecs + (ANY,) * n_buf,
        scratch_shapes=list(scratch) + [pltpu.SemaphoreType.DMA((side.nsem,)), pltpu.SemaphoreType.DMA((side.nsem,))],
        input_output_aliases={n_in + i: n_out + i for i in range(n_buf)}, compiler_params=_cp(("arbitrary",) * len(grid)),
    )(*operands, *side.bufs)
    commit(list(res[n_out:]))
    return tuple(res[:n_out]) if multi else res[0]


def _pcs(body, *, name, out_shape, grid, in_specs, out_specs, compiler_params, scratch_shapes=()):
    def run(*operands):
        return _call(body, name=name, out_shape=out_shape, grid=grid, in_specs=in_specs, out_specs=out_specs, operands=operands,
                     sem=compiler_params.dimension_semantics, scratch=scratch_shapes)
    return run


def _pick(n, prefs):
    for p in prefs:
        if p <= n and n % p == 0:
            return p
    return n


def _row_tile(rows, cols, target_bytes=1 << 20):
    best = None
    for d in range(2 * SUBLANES, rows + 1, 2 * SUBLANES):
        if rows % d == 0 and d * cols * 4 <= target_bytes:
            best = d
    return best if best is not None else rows


def _sigmoid(x):
    return 1.0 / (1.0 + jnp.exp(-x))


def _gelu(x):
    c = math.sqrt(2.0 / math.pi)
    th = jnp.tanh(c * (x + 0.044715 * x * x * x))
    return 0.5 * x * (1.0 + th), th


def _gelu_grad(x, th):
    c = math.sqrt(2.0 / math.pi)
    return 0.5 * (1.0 + th) + 0.5 * x * (1.0 - th * th) * c * (1.0 + 3.0 * 0.044715 * x * x)


_DIMS = {"nn": (((1,), (0,)), ((), ())), "nt": (((1,), (1,)), ((), ())), "tn": (((0,), (0,)), ((), ()))}


def _mm(a, b, mode, out_dtype, name, tm=(512,), tn=(512,), tk=(100000,), bias=None, silu_a=False, rows=None, outer="j", b_lead=None,
        halves=False, res=None):
    bshape = b.shape[-2:]
    if mode == "nn":
        (M, K), N = a.shape, bshape[1]
    elif mode == "nt":
        (M, K), N = a.shape[-2:], bshape[0]
        K = 2 * K if halves else K
    else:
        (K, M), N = a.shape, bshape[1]
        N = 2 * N if halves else N
    if rows is not None:
        if mode == "tn":
            K = rows
        else:
            M = rows
    tm, tn, tk = _pick(M, tm), _pick(N, tn), _pick(K, tk)
    gm, gn, gk = M // tm, N // tn, K // tk

    def ij(g0, g1):
        return (g1, g0) if outer == "j" else (g0, g1)

    if mode == "nn":
        a_spec = BS((tm, tk), lambda g0, g1, k: (ij(g0, g1)[0], k))
        b_spec = BS((tk, tn), lambda g0, g1, k: (k, ij(g0, g1)[1]))
    elif mode == "nt":
        a_spec = BS((tm, tk), lambda g0, g1, k: (ij(g0, g1)[0], k))
        b_spec = BS((tn, tk), lambda g0, g1, k: (ij(g0, g1)[1], k))
    else:
        a_spec = BS((tk, tm), lambda g0, g1, k: (k, ij(g0, g1)[0]))
        b_spec = BS((tk, tn), lambda g0, g1, k: (k, ij(g0, g1)[1]))
    if halves and mode == "nt":
        per = K // 2 // tk
        a_spec = BS((None, tm, tk), lambda g0, g1, k: (k // per, ij(g0, g1)[0], k % per))
    if halves and mode == "tn":
        per = N // 2 // tn
        b_spec = BS((None, tk, tn), lambda g0, g1, k: (ij(g0, g1)[1] // per, k, ij(g0, g1)[1] % per))
    if b_lead is not None:
        blk, at = b_spec.block_shape, b_spec.index_map
        b_spec = BS((None,) + tuple(blk), lambda g0, g1, k: (b_lead,) + tuple(at(g0, g1, k)))
    in_specs = [a_spec, b_spec]
    operands = [a, b]
    if bias is not None:
        in_specs.append(BS((1, tn), lambda g0, g1, k: (0, ij(g0, g1)[1])))
        operands.append(bias.reshape(1, N))
    o_spec = BS((tm, tn), lambda g0, g1, k: ij(g0, g1))
    out_shape, out_specs = jax.ShapeDtypeStruct((M, N), out_dtype), o_spec
    if res is not None:
        x_res, gate, lat_rows = res
        G = gate.shape[0]
        in_specs += [o_spec, BS((G, 1, tn), lambda g0, g1, k: (0, 0, ij(g0, g1)[1]))]
        operands += [x_res, gate]
        out_shape, out_specs = (jax.ShapeDtypeStruct((M, N), F32), jax.ShapeDtypeStruct((M, N), BF16)), (o_spec, o_spec)
    n_in = len(in_specs)

    def body(*refs):
        a_ref, b_ref = refs[0], refs[1]
        bias_ref = refs[2] if bias is not None else None
        o_ref = refs[n_in]
        av = a_ref[...]
        if silu_a:
            av = av.astype(F32)
            av = av * _sigmoid(av)
        part = lax.dot_general(av.astype(BF16), b_ref[...].astype(BF16), _DIMS[mode], preferred_element_type=F32)
        i_blk = pl.program_id(1 if outer == "j" else 0)

        def finish(acc):
            if bias_ref is not None:
                acc = acc + bias_ref[...]
            if res is None:
                o_ref[...] = acc.astype(out_dtype)
                return
            x_ref, gate_ref, y_ref = refs[n_in - 2], refs[n_in - 1], refs[n_in + 1]
            gate_ = gate_ref[0]
            if G == 2:
                rows = i_blk * tm + lax.broadcasted_iota(jnp.int32, (tm, 1), 0)
                gate_ = jnp.where(rows < lat_rows, gate_ref[0], gate_ref[1])
            o_ref[...] = x_ref[...] + gate_ * acc
            y_ref[...] = acc.astype(BF16)

        if gk == 1:
            finish(part)
        else:
            acc_ref = refs[-1]
            k = pl.program_id(2)

            @pl.when(k == 0)
            def _():
                acc_ref[...] = part

            @pl.when(k > 0)
            def _():
                acc_ref[...] += part

            @pl.when(k == gk - 1)
            def _():
                finish(acc_ref[...])

    grid = (gn, gm, gk) if outer == "j" else (gm, gn, gk)
    return _call(body, name=name, out_shape=out_shape, grid=grid, in_specs=in_specs, out_specs=out_specs,
                 scratch=[pltpu.VMEM((tm, tn), F32)] if gk > 1 else [], sem=("parallel", "parallel", "arbitrary"),
                 operands=operands)


def _accum(ref, val, first):
    @pl.when(first)
    def _():
        ref[...] = val

    @pl.when(jnp.logical_not(first))
    def _():
        ref[...] += val


def _norm_mod_fwd(X, gain, sh, sc, tb, nlat, name):
    R, D = X.shape

    def body(x_ref, g_ref, sh_ref, sc_ref, o_ref):
        x = x_ref[...]
        r = lax.rsqrt(jnp.mean(x * x, axis=-1, keepdims=True) + EPS)
        o_ref[...] = ((x * r * g_ref[...]) * (1.0 + sc_ref[0]) + sh_ref[0]).astype(BF16)

    grp = BS((1, 1, D), lambda i: (i // nlat, 0, 0))
    return _pcs(
        body, name=name, out_shape=jax.ShapeDtypeStruct((R, D), BF16), grid=(R // tb,),
        in_specs=[BS((tb, D), lambda i: (i, 0)), BS((1, D), lambda i: (0, 0)), grp, grp],
        out_specs=BS((tb, D), lambda i: (i, 0)), compiler_params=_cp(("parallel",)),
    )(X, gain.reshape(1, D), sh, sc)


def _gate_back(dx, y_ref, gate_ref, dy_ref, dgate_ref, first):
    dy_ref[...] = (gate_ref[0] * dx).astype(BF16)
    _accum(dgate_ref, jnp.sum(dx * y_ref[...], axis=0, keepdims=True)[None], first)


def _norm_mod_bwd(X, gain, sc, dh, dup, tb, nlat, name, before=None):
    R, D = X.shape
    G = sc.shape[0]
    n_up = dup.shape[0] // tb

    def body(x_ref, g_ref, sc_ref, dh_ref, dup_ref, *rest):
        dx_ref, dg_ref, dsh_ref, dsc_ref = rest[2:6] if before else rest[:4]
        i = pl.program_id(0)
        x = x_ref[...]
        g = g_ref[...]
        r = lax.rsqrt(jnp.mean(x * x, axis=-1, keepdims=True) + EPS)
        xh = x * r
        dh_ = dh_ref[...].astype(F32)
        t = dh_ * (1.0 + sc_ref[0])
        tg = t * g
        dup_ = dup_ref[...] if n_up == R // tb else jnp.where(i < n_up, dup_ref[...], 0.0)
        dx = dup_ + r * (tg - xh * jnp.mean(tg * xh, axis=-1, keepdims=True))
        dx_ref[...] = dx
        _accum(dg_ref, jnp.sum(t * xh, axis=0, keepdims=True), i == 0)
        first = i % nlat == 0
        _accum(dsh_ref, jnp.sum(dh_, axis=0, keepdims=True)[None], first)
        _accum(dsc_ref, jnp.sum(dh_ * xh * g, axis=0, keepdims=True)[None], first)
        if before:
            _gate_back(dx, rest[0], rest[1], rest[6], rest[7], first)

    row = BS((tb, D), lambda i: (i, 0))
    grp = BS((1, 1, D), lambda i: (i // nlat, 0, 0))
    vec = BS((1, D), lambda i: (0, 0))
    sds = jax.ShapeDtypeStruct
    out_shape = (sds((R, D), F32), sds((1, D), F32), sds((G, 1, D), F32), sds((G, 1, D), F32))
    in_specs, out_specs = [row, vec, grp, row, BS((tb, D), lambda i: (jnp.minimum(i, n_up - 1), 0))], (row, vec, grp, grp)
    operands = (X, gain.reshape(1, D), sc, dh, dup)
    if before:
        in_specs, operands = in_specs + [row, grp], operands + tuple(before)
        out_shape, out_specs = out_shape + (sds((R, D), BF16), sds((G, 1, D), F32)), out_specs + (row, grp)
    return _pcs(body, name=name, out_shape=out_shape, grid=(R // tb,), in_specs=in_specs, out_specs=out_specs,
                compiler_params=_cp(("arbitrary",)))(*operands)


def _final_loss(X, gain, target, before, tb, name):
    R, D = X.shape

    def body(x_ref, g_ref, t_ref, y_ref, gate_ref, loss_ref, dx_ref, dg_ref, dy_ref, dgate_ref):
        i = pl.program_id(0)
        x = x_ref[...]
        g = g_ref[...]
        r = lax.rsqrt(jnp.mean(x * x, axis=-1, keepdims=True) + EPS)
        xh = x * r
        e = xh * g - t_ref[...]
        part = jnp.sum(jnp.sum(e * e, axis=1, keepdims=True), axis=0, keepdims=True) * (0.5 / D)
        _accum(loss_ref, jnp.broadcast_to(part, (1, LANES)), i == 0)
        dy = e * (1.0 / D)
        _accum(dg_ref, jnp.sum(dy * xh, axis=0, keepdims=True), i == 0)
        tg = dy * g
        dx = r * (tg - xh * jnp.mean(tg * xh, axis=-1, keepdims=True))
        dx_ref[...] = dx
        _gate_back(dx, y_ref, gate_ref, dy_ref, dgate_ref, i == 0)

    row = BS((tb, D), lambda i: (i, 0))
    vec = BS((1, D), lambda i: (0, 0))
    grp = BS((1, 1, D), lambda i: (0, 0, 0))
    sds = jax.ShapeDtypeStruct
    return _pcs(
        body, name=name,
        out_shape=(sds((1, LANES), F32), sds((R, D), F32), sds((1, D), F32), sds((R, D), BF16), sds((1, 1, D), F32)),
        grid=(R // tb,), in_specs=[row, vec, row, row, grp],
        out_specs=(BS((1, LANES), lambda i: (0, 0)), row, vec, row, grp), compiler_params=_cp(("arbitrary",)),
    )(X, gain.reshape(1, D), target, *before)


def _ln_stats(v):
    mu = jnp.mean(v, axis=-1, keepdims=True)
    d = v - mu
    r = lax.rsqrt(jnp.mean(d * d, axis=-1, keepdims=True) + EPS)
    return d * r, r


def _gmlp_fwd(z, ln_g, ln_b, w_s, b_s_full, tb, name):
    R = z.shape[0]
    AW = ln_g.shape[-1]
    AH = w_s.shape[0]

    def body(zu_ref, zv_ref, g_ref, b_ref, ws_ref, bs_ref, o_ref):
        u, _ = _gelu(zu_ref[...].astype(F32))
        v, _ = _gelu(zv_ref[...].astype(F32))
        xh, _ = _ln_stats(v)
        vn = (xh * g_ref[...] + b_ref[...]).astype(BF16)
        for n in range(tb // CHUNK):
            rs = slice(n * CHUNK, (n + 1) * CHUNK)
            for h in range(AH):
                cs = slice(h * CHUNK, (h + 1) * CHUNK)
                v2 = jnp.dot(ws_ref[h].astype(BF16), vn[rs, cs], preferred_element_type=F32) + bs_ref[h]
                o_ref[rs, cs] = (u[rs, cs] * v2).astype(BF16)

    full3 = lambda s: BS(s, lambda i: (0, 0, 0))
    return _pcs(
        body, name=name, out_shape=jax.ShapeDtypeStruct((R, AW), BF16), grid=(R // tb,),
        in_specs=[BS((tb, AW), lambda i: (i, 0)), BS((tb, AW), lambda i: (i, 1)), BS((1, AW), lambda i: (0, 0)),
                  BS((1, AW), lambda i: (0, 0)), full3((AH, CHUNK, CHUNK)), full3((AH, CHUNK, CHUNK))],
        out_specs=BS((tb, AW), lambda i: (i, 0)), compiler_params=_cp(("parallel",)),
    )(z, z, ln_g.reshape(1, AW), ln_b.reshape(1, AW), w_s, b_s_full)


def _gmlp_bwd(z, dya, ln_g, ln_b, w_s, b_s_full, tb, name):
    R = z.shape[0]
    AW = ln_g.shape[-1]
    AH = w_s.shape[0]

    def body(zu_ref, zv_ref, dy_ref, g_ref, b_ref, ws_ref, bs_ref, dzu_ref, dzv_ref, dg_ref, db_ref, dws_ref, dbs_ref, cs_u_ref, cs_v_ref,
             dvn_scr):
        i = pl.program_id(0)
        first = i == 0
        zu = zu_ref[...].astype(F32)
        zv = zv_ref[...].astype(F32)
        u, thu = _gelu(zu)
        v, thv = _gelu(zv)
        xh, r = _ln_stats(v)
        g = g_ref[...]
        vn = (xh * g + b_ref[...]).astype(BF16)
        dy = dy_ref[...].astype(F32)

        @pl.when(first)
        def _():
            dws_ref[...] = jnp.zeros_like(dws_ref)
            dbs_ref[...] = jnp.zeros_like(dbs_ref)

        for n in range(tb // CHUNK):
            rs = slice(n * CHUNK, (n + 1) * CHUNK)
            for h in range(AH):
                cs = slice(h * CHUNK, (h + 1) * CHUNK)
                w = ws_ref[h].astype(BF16)
                v2 = jnp.dot(w, vn[rs, cs], preferred_element_type=F32) + bs_ref[h]
                dzu_ref[rs, cs] = (dy[rs, cs] * v2 * _gelu_grad(zu[rs, cs], thu[rs, cs])).astype(BF16)
                dv2 = dy[rs, cs] * u[rs, cs]
                dv2b = dv2.astype(BF16)
                dvn_scr[rs, cs] = lax.dot_general(w, dv2b, _DIMS["tn"], preferred_element_type=F32)
                dws_ref[h] += lax.dot_general(dv2b, vn[rs, cs], _DIMS["nt"], preferred_element_type=F32)
                dbs_ref[h] += jnp.sum(dv2, axis=1, keepdims=True)
        dvn = dvn_scr[...]
        _accum(dg_ref, jnp.sum(dvn * xh, axis=0, keepdims=True), first)
        _accum(db_ref, jnp.sum(dvn, axis=0, keepdims=True), first)
        t = dvn * g
        dv = r * (t - jnp.mean(t, axis=-1, keepdims=True) - xh * jnp.mean(t * xh, axis=-1, keepdims=True))
        dzv = dv * _gelu_grad(zv, thv)
        dzv_ref[...] = dzv.astype(BF16)
        _accum(cs_v_ref, jnp.sum(dzv, axis=0, keepdims=True), first)
        _accum(cs_u_ref, jnp.sum(dzu_ref[...].astype(F32), axis=0, keepdims=True), first)

    full3 = lambda s: BS(s, lambda i: (0, 0, 0))
    vec = BS((1, AW), lambda i: (0, 0))
    row = BS((tb, AW), lambda i: (i, 0))
    outs = _pcs(
        body, name=name,
        out_shape=(jax.ShapeDtypeStruct((R, AW), BF16), jax.ShapeDtypeStruct((R, AW), BF16), jax.ShapeDtypeStruct((1, AW), F32),
                   jax.ShapeDtypeStruct((1, AW), F32), jax.ShapeDtypeStruct((AH, CHUNK, CHUNK), F32),
                   jax.ShapeDtypeStruct((AH, CHUNK, 1), F32), jax.ShapeDtypeStruct((1, AW), F32), jax.ShapeDtypeStruct((1, AW), F32)),
        grid=(R // tb,),
        in_specs=[row, BS((tb, AW), lambda i: (i, 1)), row, vec, vec, full3((AH, CHUNK, CHUNK)), full3((AH, CHUNK, CHUNK))],
        out_specs=(row, row, vec, vec, full3((AH, CHUNK, CHUNK)), full3((AH, CHUNK, 1)), vec, vec),
        scratch_shapes=[pltpu.VMEM((tb, AW), F32)], compiler_params=_cp(("arbitrary",)),
    )(z, z, dya, ln_g.reshape(1, AW), ln_b.reshape(1, AW), w_s, b_s_full)
    return outs


def _segments(R, L):
    return [(0, L)] + ([(L, R - L)] if R > L else [])


def _scr_rows(R, L):
    return R + CONV_PAD * (len(_segments(R, L)) + 1)


def _scr_off(s, start):
    return CONV_PAD * (s + 1) + start


def _zero_pads(scr, R, L):
    segs = _segments(R, L)
    z = jnp.zeros((CONV_PAD, scr.shape[1]), F32)
    for s, (start, n) in enumerate(segs):
        scr[pl.ds(_scr_off(s, start) - CONV_PAD, CONV_PAD), :] = z
    last_s, (last_start, last_n) = len(segs) - 1, segs[-1]
    scr[pl.ds(_scr_off(last_s, last_start) + last_n, CONV_PAD), :] = z


def _for_chunks(R, L, ch, fn):
    for s, (start, n) in enumerate(_segments(R, L)):
        c = min(ch, n)
        off = _scr_off(s, start)

        def step(i, carry, start=start, off=off, c=c):
            r0 = pl.multiple_of(start + i * c, SUBLANES)
            fn(r0, pl.multiple_of(off + i * c, SUBLANES), c)
            return carry

        lax.fori_loop(0, n // c, step, 0)


def _taps(scr, srow, c, w_ref, ntap, flip):
    acc = None
    for k in range(ntap):
        o = k - (ntap - 1) // 2
        if flip:
            o = -o
        term = w_ref[k:k + 1, :] * scr[pl.ds(srow + o, c), :]
        acc = term if acc is None else acc + term
    return acc


def _tap_grads(scr, srow, c, dy, dw_ref, ntap):
    for k in range(ntap):
        o = k - (ntap - 1) // 2
        dw_ref[k:k + 1, :] += jnp.sum(dy * scr[pl.ds(srow + o, c), :], axis=0, keepdims=True)


def _glu_conv_fwd(z, col0, conv_w, conv_b, L, ch, name):
    R = z.shape[0]
    BW = conv_w.shape[1]
    nb, c0 = BW // LANES, col0 // LANES

    def body(a_ref, g_ref, w_ref, b_ref, o_ref, scr):
        _zero_pads(scr, R, L)

        def fill(r0, s0, c):
            a = a_ref[pl.ds(r0, c), :].astype(F32)
            g = g_ref[pl.ds(r0, c), :].astype(F32)
            scr[pl.ds(s0, c), :] = a * _sigmoid(g)

        _for_chunks(R, L, ch, fill)

        def conv(r0, s0, c):
            o_ref[pl.ds(r0, c), :] = _taps(scr, s0, c, w_ref, B_CONV, False) + b_ref[...]

        _for_chunks(R, L, ch, conv)

    return _pcs(
        body, name=name, out_shape=jax.ShapeDtypeStruct((R, BW), F32), grid=(nb,),
        in_specs=[BS((R, LANES), lambda j: (0, c0 + j)), BS((R, LANES), lambda j: (0, c0 + nb + j)),
                  BS((B_CONV, LANES), lambda j: (0, j)), BS((1, LANES), lambda j: (0, j))],
        out_specs=BS((R, LANES), lambda j: (0, j)), scratch_shapes=[pltpu.VMEM((_scr_rows(R, L), LANES), F32)],
        compiler_params=_cp(("parallel",)),
    )(z, z, conv_w, conv_b.reshape(1, BW))


def _glu_conv_bwd(z, col0, dhc, conv_w, L, ch, name):
    R = z.shape[0]
    BW = conv_w.shape[1]
    nb, c0 = BW // LANES, col0 // LANES

    def body(a_ref, g_ref, dy_ref, w_ref, da_ref, dg_ref, dw_ref, db_ref, csa_ref, csg_ref, scr_h, scr_dy):
        _zero_pads(scr_h, R, L)
        _zero_pads(scr_dy, R, L)
        dw_ref[...] = jnp.zeros_like(dw_ref)
        db_ref[...] = jnp.zeros_like(db_ref)
        csa_ref[...] = jnp.zeros_like(csa_ref)
        csg_ref[...] = jnp.zeros_like(csg_ref)

        def fill(r0, s0, c):
            a = a_ref[pl.ds(r0, c), :].astype(F32)
            g = g_ref[pl.ds(r0, c), :].astype(F32)
            scr_h[pl.ds(s0, c), :] = a * _sigmoid(g)
            scr_dy[pl.ds(s0, c), :] = dy_ref[pl.ds(r0, c), :]

        _for_chunks(R, L, ch, fill)

        def back(r0, s0, c):
            dh = _taps(scr_dy, s0, c, w_ref, B_CONV, True)
            a = a_ref[pl.ds(r0, c), :].astype(F32)
            sg = _sigmoid(g_ref[pl.ds(r0, c), :].astype(F32))
            da = dh * sg
            dg = dh * a * sg * (1.0 - sg)
            da_ref[pl.ds(r0, c), :] = da.astype(BF16)
            dg_ref[pl.ds(r0, c), :] = dg.astype(BF16)
            csa_ref[...] += jnp.sum(da, axis=0, keepdims=True)
            csg_ref[...] += jnp.sum(dg, axis=0, keepdims=True)
            dy = dy_ref[pl.ds(r0, c), :]
            db_ref[...] += jnp.sum(dy, axis=0, keepdims=True)
            _tap_grads(scr_h, s0, c, dy, dw_ref, B_CONV)

        _for_chunks(R, L, ch, back)

    col = BS((R, LANES), lambda j: (0, j))
    vec = BS((1, LANES), lambda j: (0, j))
    nrow = _scr_rows(R, L)
    return _pcs(
        body, name=name,
        out_shape=(jax.ShapeDtypeStruct((R, BW), BF16), jax.ShapeDtypeStruct((R, BW), BF16), jax.ShapeDtypeStruct((B_CONV, BW), F32),
                   jax.ShapeDtypeStruct((1, BW), F32), jax.ShapeDtypeStruct((1, BW), F32), jax.ShapeDtypeStruct((1, BW), F32)),
        grid=(nb,),
        in_specs=[BS((R, LANES), lambda j: (0, c0 + j)), BS((R, LANES), lambda j: (0, c0 + nb + j)), col,
                  BS((B_CONV, LANES), lambda j: (0, j))],
        out_specs=(col, col, BS((B_CONV, LANES), lambda j: (0, j)), vec, vec, vec),
        scratch_shapes=[pltpu.VMEM((nrow, LANES), F32), pltpu.VMEM((nrow, LANES), F32)], compiler_params=_cp(("parallel",)),
    )(z, z, dhc, conv_w)


def _ln_silu_fwd(hc, ln_g, ln_b, tb, name):
    R, W = hc.shape

    def body(x_ref, g_ref, b_ref, o_ref):
        xh, _ = _ln_stats(x_ref[...])
        y = xh * g_ref[...] + b_ref[...]
        o_ref[...] = (y * _sigmoid(y)).astype(BF16)

    vec = BS((1, W), lambda i: (0, 0))
    row = BS((tb, W), lambda i: (i, 0))
    return _pcs(
        body, name=name, out_shape=jax.ShapeDtypeStruct((R, W), BF16), grid=(R // tb,), in_specs=[row, vec, vec], out_specs=row,
        compiler_params=_cp(("parallel",)),
    )(hc, ln_g.reshape(1, W), ln_b.reshape(1, W))


def _ln_silu_bwd(hc, dyb, col0, ln_g, ln_b, tb, name):
    R, W = hc.shape
    c0 = col0 // W

    def body(x_ref, dy_ref, g_ref, b_ref, dx_ref, dg_ref, db_ref):
        i = pl.program_id(0)
        xh, r = _ln_stats(x_ref[...])
        g = g_ref[...]
        y = xh * g + b_ref[...]
        s = _sigmoid(y)
        dy = dy_ref[...].astype(F32) * s * (1.0 + y * (1.0 - s))
        _accum(dg_ref, jnp.sum(dy * xh, axis=0, keepdims=True), i == 0)
        _accum(db_ref, jnp.sum(dy, axis=0, keepdims=True), i == 0)
        t = dy * g
        dx_ref[...] = r * (t - jnp.mean(t, axis=-1, keepdims=True) - xh * jnp.mean(t * xh, axis=-1, keepdims=True))

    vec = BS((1, W), lambda i: (0, 0))
    row = BS((tb, W), lambda i: (i, 0))
    return _pcs(
        body, name=name,
        out_shape=(jax.ShapeDtypeStruct((R, W), F32), jax.ShapeDtypeStruct((1, W), F32), jax.ShapeDtypeStruct((1, W), F32)),
        grid=(R // tb,), in_specs=[row, BS((tb, W), lambda i: (i, c0)), vec, vec], out_specs=(row, vec, vec),
        compiler_params=_cp(("arbitrary",)),
    )(hc, dyb, ln_g.reshape(1, W), ln_b.reshape(1, W))


def _ffn_act_fwd(zf, conv_w, conv_b, L, ch, name):
    R = zf.shape[0]
    DFF = conv_w.shape[1]
    nb = DFF // LANES

    def body(g_ref, u_ref, w_ref, b_ref, o_ref, scr):
        _zero_pads(scr, R, L)

        def fill(r0, s0, c):
            scr[pl.ds(s0, c), :] = g_ref[pl.ds(r0, c), :].astype(F32)

        _for_chunks(R, L, ch, fill)

        def act(r0, s0, c):
            gc = _taps(scr, s0, c, w_ref, FFN_CONV, False) + b_ref[...]
            o_ref[pl.ds(r0, c), :] = (gc * _sigmoid(gc) * u_ref[pl.ds(r0, c), :].astype(F32)).astype(BF16)

        _for_chunks(R, L, ch, act)

    return _pcs(
        body, name=name, out_shape=jax.ShapeDtypeStruct((R, DFF), BF16), grid=(nb,),
        in_specs=[BS((R, LANES), lambda j: (0, j)), BS((R, LANES), lambda j: (0, nb + j)), BS((FFN_CONV, LANES), lambda j: (0, j)),
                  BS((1, LANES), lambda j: (0, j))],
        out_specs=BS((R, LANES), lambda j: (0, j)), scratch_shapes=[pltpu.VMEM((_scr_rows(R, L), LANES), F32)],
        compiler_params=_cp(("parallel",)),
    )(zf, zf, conv_w, conv_b.reshape(1, DFF))


def _ffn_act_bwd(zf, df, conv_w, conv_b, L, ch, name):
    R = zf.shape[0]
    DFF = conv_w.shape[1]
    nb = DFF // LANES

    def body(g_ref, u_ref, df_ref, w_ref, b_ref, dz_ref, dw_ref, db_ref, scr_g, scr_d):
        _zero_pads(scr_g, R, L)
        _zero_pads(scr_d, R, L)
        dw_ref[...] = jnp.zeros_like(dw_ref)
        db_ref[...] = jnp.zeros_like(db_ref)

        def fill(r0, s0, c):
            scr_g[pl.ds(s0, c), :] = g_ref[pl.ds(r0, c), :].astype(F32)

        _for_chunks(R, L, ch, fill)

        def pre(r0, s0, c):
            gc = _taps(scr_g, s0, c, w_ref, FFN_CONV, False) + b_ref[...]
            s = _sigmoid(gc)
            d = df_ref[pl.ds(r0, c), :].astype(F32)
            dz_ref[1, pl.ds(r0, c), :] = (d * gc * s).astype(BF16)
            dgc = d * u_ref[pl.ds(r0, c), :].astype(F32) * s * (1.0 + gc * (1.0 - s))
            scr_d[pl.ds(s0, c), :] = dgc
            db_ref[...] += jnp.sum(dgc, axis=0, keepdims=True)
            _tap_grads(scr_g, s0, c, dgc, dw_ref, FFN_CONV)

        _for_chunks(R, L, ch, pre)

        def back(r0, s0, c):
            dz_ref[0, pl.ds(r0, c), :] = _taps(scr_d, s0, c, w_ref, FFN_CONV, True).astype(BF16)

        _for_chunks(R, L, ch, back)

    col = BS((R, LANES), lambda j: (0, j))
    vec = BS((1, LANES), lambda j: (0, j))
    nrow = _scr_rows(R, L)
    return _pcs(
        body, name=name,
        out_shape=(jax.ShapeDtypeStruct((2, R, DFF), BF16), jax.ShapeDtypeStruct((FFN_CONV, DFF), F32), jax.ShapeDtypeStruct((1, DFF), F32)),
        grid=(nb,),
        in_specs=[col, BS((R, LANES), lambda j: (0, nb + j)), col, BS((FFN_CONV, LANES), lambda j: (0, j)), vec],
        out_specs=(BS((2, R, LANES), lambda j: (0, 0, j)), BS((FFN_CONV, LANES), lambda j: (0, j)), vec),
        scratch_shapes=[pltpu.VMEM((nrow, LANES), F32), pltpu.VMEM((nrow, LANES), F32)], compiler_params=_cp(("parallel",)),
    )(zf, zf, df, conv_w, conv_b.reshape(1, DFF))


def _rope_tables(L, T):
    rows = L // GRID_W
    row = jnp.repeat(jnp.arange(rows, dtype=F32), GRID_W)
    col = jnp.tile(jnp.arange(GRID_W, dtype=F32), rows)
    n_freq = ROPE // 4
    inv = ROPE_THETA ** (-jnp.arange(n_freq, dtype=F32) / n_freq)
    ang = jnp.concatenate([row[:, None] * inv, col[:, None] * inv], axis=-1)
    cos, sin = jnp.cos(ang), jnp.sin(ang)
    half = ROPE // 2
    zero = jnp.zeros((L, half), F32)
    cos_t = jnp.concatenate([cos, cos, jnp.ones((L, LANES - ROPE), F32)], axis=1)
    sa = jnp.concatenate([zero, sin, zero, zero], axis=1)
    sb = jnp.concatenate([-sin, zero, zero, zero], axis=1)
    pad = T - L
    cos_t = jnp.concatenate([cos_t, jnp.ones((pad, LANES), F32)], axis=0)
    sa = jnp.concatenate([sa, jnp.zeros((pad, LANES), F32)], axis=0)
    sb = jnp.concatenate([sb, jnp.zeros((pad, LANES), F32)], axis=0)
    return cos_t, sa, sb


def _rope(x, cos, sa, sb):
    half = ROPE // 2
    return x * cos + pltpu.roll(x, half, 1) * sa + pltpu.roll(x, LANES - half, 1) * sb


def _rope_t(d, cos, sa, sb):
    half = ROPE // 2
    return d * cos + pltpu.roll(d * sa, LANES - half, 1) + pltpu.roll(d * sb, half, 1)


def _mla_prep_fwd(zm, qg, kvg, tabs, tb, name):
    T, W = zm.shape
    QL, KVL = qg.shape[-1], kvg.shape[-1]

    def body(z_ref, qg_ref, kg_ref, cos_ref, sa_ref, sb_ref, q_ref, k_ref, p_ref):
        cq = z_ref[:, :QL]
        r = lax.rsqrt(jnp.mean(cq * cq, axis=-1, keepdims=True) + EPS)
        q_ref[...] = (cq * r * qg_ref[...]).astype(BF16)
        ck = z_ref[:, QL:QL + KVL]
        r = lax.rsqrt(jnp.mean(ck * ck, axis=-1, keepdims=True) + EPS)
        k_ref[...] = (ck * r * kg_ref[...]).astype(BF16)
        p_ref[...] = _rope(z_ref[:, QL + KVL:], cos_ref[...], sa_ref[...], sb_ref[...]).astype(BF16)

    tab = BS((tb, LANES), lambda i: (i, 0))
    return _pcs(
        body, name=name,
        out_shape=(jax.ShapeDtypeStruct((T, QL), BF16), jax.ShapeDtypeStruct((T, KVL), BF16), jax.ShapeDtypeStruct((T, LANES), BF16)),
        grid=(T // tb,),
        in_specs=[BS((tb, W), lambda i: (i, 0)), BS((1, QL), lambda i: (0, 0)), BS((1, KVL), lambda i: (0, 0)), tab, tab, tab],
        out_specs=(BS((tb, QL), lambda i: (i, 0)), BS((tb, KVL), lambda i: (i, 0)), tab), compiler_params=_cp(("parallel",)),
    )(zm, qg.reshape(1, QL), kvg.reshape(1, KVL), *tabs)


def _mla_prep_bwd(zm, qg, kvg, tabs, dq, dk, dp, tb, name):
    T, W = zm.shape
    QL, KVL = qg.shape[-1], kvg.shape[-1]

    def body(z_ref, qg_ref, kg_ref, cos_ref, sa_ref, sb_ref, dq_ref, dk_ref, dp_ref, dz_ref, dqg_ref, dkg_ref):
        i = pl.program_id(0)

        def rms_bwd(x, g, dy):
            r = lax.rsqrt(jnp.mean(x * x, axis=-1, keepdims=True) + EPS)
            xh = x * r
            t = dy * g
            return r * (t - xh * jnp.mean(t * xh, axis=-1, keepdims=True)), jnp.sum(dy * xh, axis=0, keepdims=True)

        dcq, dg = rms_bwd(z_ref[:, :QL], qg_ref[...], dq_ref[...].astype(F32))
        dz_ref[:, :QL] = dcq.astype(BF16)
        _accum(dqg_ref, dg, i == 0)
        dck, dg = rms_bwd(z_ref[:, QL:QL + KVL], kg_ref[...], dk_ref[...].astype(F32))
        dz_ref[:, QL:QL + KVL] = dck.astype(BF16)
        _accum(dkg_ref, dg, i == 0)
        dz_ref[:, QL + KVL:] = _rope_t(dp_ref[...], cos_ref[...], sa_ref[...], sb_ref[...]).astype(BF16)

    tab = BS((tb, LANES), lambda i: (i, 0))
    return _pcs(
        body, name=name,
        out_shape=(jax.ShapeDtypeStruct((T, W), BF16), jax.ShapeDtypeStruct((1, QL), F32), jax.ShapeDtypeStruct((1, KVL), F32)),
        grid=(T // tb,),
        in_specs=[BS((tb, W), lambda i: (i, 0)), BS((1, QL), lambda i: (0, 0)), BS((1, KVL), lambda i: (0, 0)), tab, tab, tab,
                  BS((tb, QL), lambda i: (i, 0)), BS((tb, KVL), lambda i: (i, 0)), tab],
        out_specs=(BS((tb, W), lambda i: (i, 0)), BS((1, QL), lambda i: (0, 0)), BS((1, KVL), lambda i: (0, 0))),
        compiler_params=_cp(("arbitrary",)),
    )(zm, qg.reshape(1, QL), kvg.reshape(1, KVL), *tabs, dq, dk, dp)


def _attn_fwd(q, kv, kpe, tabs, L, tq, nsub, name):
    T = kv.shape[0]
    H = kv.shape[1] // QHEAD
    scale = (HEAD + ROPE) ** -0.5

    def body(q_ref, kv_ref, kpe_ref, cos_ref, sa_ref, sb_ref, o_ref, lse_ref, kcat):
        @pl.when(pl.program_id(1) == 0)
        def _():
            kcat[:, :HEAD] = kv_ref[:, :HEAD]
            kcat[:, HEAD:] = kpe_ref[...]

        sub = tq // nsub
        for r in range(nsub):
            rs = slice(r * sub, (r + 1) * sub)
            qp = _rope(q_ref[rs, HEAD:].astype(F32), cos_ref[rs, :], sa_ref[rs, :], sb_ref[rs, :]).astype(BF16)
            qc = jnp.concatenate([q_ref[rs, :HEAD], qp], axis=1)
            s = lax.dot_general(qc, kcat[...], _DIMS["nt"], preferred_element_type=F32)
            m = jnp.max(s, axis=-1, keepdims=True)
            p = jnp.exp2((s - m) * (scale * math.log2(math.e)))
            l = jnp.sum(p, axis=-1, keepdims=True)
            o = jnp.dot(p.astype(BF16), kv_ref[:, HEAD:], preferred_element_type=F32)
            o_ref[rs, :] = (o / l).astype(BF16)
            lse_ref[0, rs, :] = m * scale + jnp.log(l)

    tab = BS((tq, LANES), lambda h, i: (i, 0))
    return _call(
        body, name=name, out_shape=(jax.ShapeDtypeStruct((L, H * HEAD), BF16), jax.ShapeDtypeStruct((H, L, 1), F32)),
        grid=(H, L // tq),
        in_specs=[BS((tq, QHEAD), lambda h, i: (i, h)), BS((T, QHEAD), lambda h, i: (0, h)), BS((T, LANES), lambda h, i: (0, 0)),
                  tab, tab, tab],
        out_specs=(BS((tq, HEAD), lambda h, i: (i, h)), BS((1, tq, 1), lambda h, i: (h, i, 0))),
        scratch=[pltpu.VMEM((T, QHEAD), BF16)], sem=("parallel", "arbitrary"), operands=(q, kv, kpe, *tabs))


def _attn_bwd(q, kv, kpe, tabs, o, lse, do, L, tq, name):
    T = kv.shape[0]
    H = kv.shape[1] // QHEAD
    scale = (HEAD + ROPE) ** -0.5
    nq = L // tq

    def body(q_ref, kv_ref, kpe_ref, cos_ref, sa_ref, sb_ref, o_ref, lse_ref, do_ref, dq_ref, dkv_ref, dkpe_ref, kcat, dk_acc, dv_acc,
             qc_scr, ds_scr, p_scr):
        h, i = pl.program_id(0), pl.program_id(1)

        @pl.when(i == 0)
        def _():
            kcat[:, :HEAD] = kv_ref[:, :HEAD]
            kcat[:, HEAD:] = kpe_ref[...]
            dk_acc[...] = jnp.zeros_like(dk_acc)
            dv_acc[...] = jnp.zeros_like(dv_acc)

        sub = tq // 2
        log2e = math.log2(math.e)
        for r in range(2):
            rs = slice(r * sub, (r + 1) * sub)
            cos, sa, sb = cos_ref[rs, :], sa_ref[rs, :], sb_ref[rs, :]
            qp = _rope(q_ref[rs, HEAD:].astype(F32), cos, sa, sb).astype(BF16)
            qc_scr[rs, :] = jnp.concatenate([q_ref[rs, :HEAD], qp], axis=1)
            s = lax.dot_general(qc_scr[rs, :], kcat[...], _DIMS["nt"], preferred_element_type=F32)
            p = jnp.exp2(s * (scale * log2e) - lse_ref[0, rs, :] * log2e)
            dov = do_ref[rs, :]
            delta = jnp.sum(dov.astype(F32) * o_ref[rs, :].astype(F32), axis=-1, keepdims=True)
            dp = lax.dot_general(dov, kv_ref[:, HEAD:], _DIMS["nt"], preferred_element_type=F32)
            ds_scr[rs, :] = (p * (dp - delta) * scale).astype(BF16)
            p_scr[rs, :] = p.astype(BF16)
            dqc = jnp.dot(ds_scr[rs, :], kcat[...], preferred_element_type=F32)
            dq_ref[rs, :HEAD] = dqc[:, :HEAD].astype(BF16)
            dq_ref[rs, HEAD:] = _rope_t(dqc[:, HEAD:], cos, sa, sb).astype(BF16)
        dk_acc[...] += lax.dot_general(ds_scr[...], qc_scr[...], _DIMS["tn"], preferred_element_type=F32)
        dv_acc[...] += lax.dot_general(p_scr[...], do_ref[...], _DIMS["tn"], preferred_element_type=F32)

        @pl.when(i == nq - 1)
        def _():
            dkv_ref[:, :HEAD] = dk_acc[:, :HEAD].astype(BF16)
            dkv_ref[:, HEAD:] = dv_acc[...].astype(BF16)

            @pl.when(h == 0)
            def _():
                dkpe_ref[...] = dk_acc[:, HEAD:]

            @pl.when(h > 0)
            def _():
                dkpe_ref[...] += dk_acc[:, HEAD:]

    tab = BS((tq, LANES), lambda h, i: (i, 0))
    return _call(
        body, name=name,
        out_shape=(jax.ShapeDtypeStruct((L, H * QHEAD), BF16), jax.ShapeDtypeStruct((T, H * QHEAD), BF16), jax.ShapeDtypeStruct((T, LANES), F32)),
        grid=(H, nq),
        in_specs=[BS((tq, QHEAD), lambda h, i: (i, h)), BS((T, QHEAD), lambda h, i: (0, h)), BS((T, LANES), lambda h, i: (0, 0)),
                  tab, tab, tab, BS((tq, HEAD), lambda h, i: (i, h)), BS((1, tq, 1), lambda h, i: (h, i, 0)),
                  BS((tq, HEAD), lambda h, i: (i, h))],
        out_specs=(BS((tq, QHEAD), lambda h, i: (i, h)), BS((T, QHEAD), lambda h, i: (0, h)), BS((T, LANES), lambda h, i: (0, 0))),
        scratch=[pltpu.VMEM((T, QHEAD), BF16), pltpu.VMEM((T, QHEAD), F32), pltpu.VMEM((T, HEAD), F32),
                 pltpu.VMEM((tq, QHEAD), BF16), pltpu.VMEM((tq, T), BF16), pltpu.VMEM((tq, T), BF16)],
        sem=("arbitrary", "arbitrary"), operands=(q, kv, kpe, *tabs, o, lse, do))


def _adamw(w, g, m, v, name):
    R, C = w.shape
    tr = _row_tile(R, C)
    c1 = 1.0 / (1.0 - ADAM_B1 ** ADAM_STEP)
    c2 = 1.0 / (1.0 - ADAM_B2 ** ADAM_STEP)

    def body(w_ref, g_ref, m_ref, v_ref, d_ref, nm_ref, nv_ref):
        g_ = g_ref[...]
        nm = ADAM_B1 * m_ref[...] + (1.0 - ADAM_B1) * g_
        nv = ADAM_B2 * v_ref[...] + (1.0 - ADAM_B2) * (g_ * g_)
        nm_ref[...] = nm
        nv_ref[...] = nv
        d_ref[...] = -ADAM_LR * ((nm * c1) / (jnp.sqrt(nv * c2) + ADAM_EPS) + ADAM_WD * w_ref[...])

    blk = BS((tr, C), lambda i: (i, 0))
    sd = jax.ShapeDtypeStruct((R, C), F32)
    return _pcs(body, name=name, out_shape=(sd, sd, sd), grid=(R // tr,), in_specs=[blk] * 4, out_specs=(blk,) * 3,
                compiler_params=_cp(("parallel",)))(w, g, m, v)


def _adamw_layers(w, gs, m, v, name):
    n, R, C = w.shape
    tr = _row_tile(R, C)
    nr = R // tr
    c1 = 1.0 / (1.0 - ADAM_B1 ** ADAM_STEP)
    c2 = 1.0 / (1.0 - ADAM_B2 ** ADAM_STEP)

    def body(w_ref, m_ref, v_ref, *rest):
        g_refs, (go_ref, d_ref, nm_ref, nv_ref) = rest[:n], rest[n:]
        layer = pl.program_id(0)
        g_ = g_refs[0][...]
        for j in range(1, n):
            g_ = jnp.where(layer == j, g_refs[j][...], g_)
        nm = ADAM_B1 * m_ref[...] + (1.0 - ADAM_B1) * g_
        nv = ADAM_B2 * v_ref[...] + (1.0 - ADAM_B2) * (g_ * g_)
        go_ref[...] = g_
        nm_ref[...] = nm
        nv_ref[...] = nv
        d_ref[...] = -ADAM_LR * ((nm * c1) / (jnp.sqrt(nv * c2) + ADAM_EPS) + ADAM_WD * w_ref[...])

    def g_spec(j):
        return BS((tr, C), lambda layer, i: (jnp.where(layer == j, i, jnp.where(layer < j, 0, nr - 1)), 0))

    blk = BS((None, tr, C), lambda layer, i: (layer, i, 0))
    sd = jax.ShapeDtypeStruct((n, R, C), F32)
    return _pcs(body, name=name, out_shape=(sd, sd, sd, sd), grid=(n, nr), in_specs=[blk] * 3 + [g_spec(j) for j in range(n)],
                out_specs=(blk,) * 4, compiler_params=_cp(("arbitrary", "arbitrary")))(w, m, v, *gs)


def _sum_lead(a, out_dtype, name):
    n, R, C = a.shape
    tr = _row_tile(R, C * n, 2 << 20)

    def body(a_ref, o_ref):
        acc = a_ref[0].astype(F32)
        for k in range(1, n):
            acc = acc + a_ref[k].astype(F32)
        o_ref[...] = acc.astype(out_dtype)

    return pl.pallas_call(body, name=name, out_shape=jax.ShapeDtypeStruct((R, C), out_dtype), grid=(R // tr,),
                          in_specs=[BS((n, tr, C), lambda i: (0, i, 0))], out_specs=BS((tr, C), lambda i: (i, 0)),
                          compiler_params=_cp(("parallel",)))(a)


def _cctx_grad(parts, c_ctx, name):
    n, D = parts.shape

    def body(p_ref, c_ref, o_ref):
        d = jnp.sum(p_ref[...], axis=0, keepdims=True)
        c = c_ref[...]
        s = _sigmoid(c)
        o_ref[...] = d * s * (1.0 + c * (1.0 - s))

    return pl.pallas_call(body, name=name, out_shape=jax.ShapeDtypeStruct((1, D), F32))(parts, c_ctx.reshape(1, D))


def _me():
    return lax.axis_index("x"), lax.axis_index("y"), lax.axis_index("c")


def _aligned(v, n):
    return v if isinstance(v, int) else pl.multiple_of(v, n)


def _window(ref, r0, c0, R, C):
    rows = pl.ds(_aligned(r0, SUBLANES), R)
    if C == ref.shape[1]:
        return ref.at[rows, :]
    return ref.at[rows, pl.ds(_aligned(c0, LANES), C)]


def _allgather8(items, name):
    n = len(items)

    def body(*refs):
        srcs, dsts = refs[:n], refs[n:2 * n]
        send_sems, recv_sems, local_sems = refs[2 * n:]
        x, y, c = _me()
        me, sibling = (x, y, c), (x, y, 1 - c)
        chips = [(1 - x, y), (x, 1 - y), (1 - x, 1 - y)]

        def dwin(a, blk):
            (R, C), at = items[a][2], items[a][4]
            return _window(dsts[a], *at(*blk), R, C)

        def swin(a):
            (R, C), at = items[a][2], items[a][1]
            return _window(srcs[a], *at(*me), R, C)

        def copy(a, k, blk, to, src=None):
            return pltpu.make_async_remote_copy(
                src_ref=dwin(a, blk) if src is None else src, dst_ref=dwin(a, blk), send_sem=send_sems.at[7 * a + k],
                recv_sem=recv_sems.at[7 * a + k], device_id=to, device_id_type=MESH)

        mine = [pltpu.make_async_copy(swin(a), dwin(a, me), local_sems.at[a]) for a in range(n)]
        for cp in mine:
            cp.start()
        first = []
        for a in range(n):
            first.append(copy(a, 0, me, sibling, src=swin(a)))
            first += [copy(a, 1 + j, me, (*chip, c), src=swin(a)) for j, chip in enumerate(chips)]
        for cp in first:
            cp.start()
        passed = []
        for j, chip in enumerate(chips):
            for a in range(n):
                copy(a, 1 + j, (*chip, c), me).wait_recv()
                fwd = copy(a, 4 + j, (*chip, c), sibling)
                fwd.start()
                passed.append(fwd)
        for a in range(n):
            copy(a, 0, sibling, me).wait_recv()
            for j, chip in enumerate(chips):
                copy(a, 4 + j, (*chip, 1 - c), me).wait_recv()
        for cp in first + passed:
            cp.wait_send()
        for cp in mine:
            cp.wait()

    outs = pl.pallas_call(
        body, name=name, out_shape=tuple(jax.ShapeDtypeStruct(it[3], it[0].dtype) for it in items),
        in_specs=[ANY] * n, out_specs=tuple([ANY] * n),
        scratch_shapes=[pltpu.SemaphoreType.DMA((7 * n,)), pltpu.SemaphoreType.DMA((7 * n,)), pltpu.SemaphoreType.DMA((n,))],
    )(*[it[0] for it in items])
    return list(outs)


def _cast_place(w, kind, name, lead=None):
    R2, C = w.shape[-2:]
    tr = _row_tile(R2, C)
    nr = R2 // tr
    xi, yi, _ = _me()
    p_arr = (2 * xi + yi).astype(jnp.int32).reshape(1)
    if kind == "col":
        shape, o_spec = (R2, 4 * C), BS((tr, C), lambda i, p: (i, p[0]))
    else:
        shape, o_spec = (4 * R2, C), BS((tr, C), lambda i, p: (p[0] * nr + i, 0))

    def body(p_ref, w_ref, o_ref):
        o_ref[...] = w_ref[...].astype(BF16)

    return pl.pallas_call(
        body, name=name, out_shape=jax.ShapeDtypeStruct(shape, BF16),
        grid_spec=pltpu.PrefetchScalarGridSpec(
            num_scalar_prefetch=1, grid=(nr,), out_specs=o_spec,
            in_specs=[BS((tr, C), lambda i, p: (i, 0)) if lead is None else BS((None, tr, C), lambda i, p: (lead, i, 0))]),
        compiler_params=_cp(("parallel",)),
    )(p_arr, w)


def _remote(src, dst, sends, recvs, k, to):
    return pltpu.make_async_remote_copy(src_ref=src, dst_ref=dst, send_sem=sends.at[k], recv_sem=recvs.at[k], device_id=to,
                                        device_id_type=MESH)


def _gather_side(bufs, pieces, passed=(), pass_now=False):
    later = list(passed) + (list(pieces) if pass_now else [])
    base = 3 * len(pieces)

    def win(b, piece, p, pc):
        bi, (R, C), at, k, i = piece
        r0, c0 = at(p, pc)
        return _window(b[bi], r0 + i * (R // k), c0, R // k, C)

    def chips_of(x, y):
        return [(1 - x, y), (x, 1 - y), (1 - x, 1 - y)]

    def over_ici(b, sends, recvs, landing):
        x, y, c = _me()
        cps = []
        for m, piece in enumerate(pieces):
            for j, (px, py) in enumerate(chips_of(x, y)):
                w = win(b, piece, 2 * px + py, c) if landing else win(b, piece, 2 * x + y, c)
                cps.append(_remote(w, w, sends, recvs, 3 * m + j, (px, py, c)))
        return cps

    def to_sibling(b, sends, recvs, todo, landing):
        x, y, c = _me()
        cps = []
        for m, piece in enumerate(later):
            if piece in todo:
                for j, (px, py) in enumerate(chips_of(x, y)):
                    w = win(b, piece, 2 * px + py, 1 - c if landing else c)
                    cps.append(_remote(w, w, sends, recvs, base + 3 * m + j, (x, y, 1 - c)))
        return cps

    def start(b, sends, recvs):
        for cp in over_ici(b, sends, recvs, False) + to_sibling(b, sends, recvs, passed, False):
            cp.start()

    def finish(b, sends, recvs):
        for cp in over_ici(b, sends, recvs, True):
            cp.wait_recv()
        if pass_now:
            for cp in to_sibling(b, sends, recvs, pieces, False):
                cp.start()
        for cp in to_sibling(b, sends, recvs, later, True):
            cp.wait_recv()
        for cp in over_ici(b, sends, recvs, False) + to_sibling(b, sends, recvs, later, False):
            cp.wait_send()

    return _Side(bufs, 3 * (len(pieces) + len(later)), start, finish)


_RELS = [(dx, dy, dc) for dx in (0, 1) for dy in (0, 1) for dc in (0, 1)][1:]


def _rs_side(bufs, pieces):
    def flip(v, d):
        return 1 - v if d else v

    def copies(b, sends, recvs, landing):
        x, y, c = _me()
        dev = 4 * x + 2 * y + c
        cps = []
        for m, (gi, ri, (R, C), at, k, i) in enumerate(pieces):
            rows = R // k
            for t, (dx, dy, dc) in enumerate(_RELS):
                tx, ty, tc = flip(x, dx), flip(y, dy), flip(c, dc)
                if landing:
                    theirs = b[ri].at[4 * tx + 2 * ty + tc, pl.ds(i * rows, rows), :]
                    cps.append(_remote(theirs, theirs, sends, recvs, 7 * m + t, (tx, ty, tc)))
                else:
                    r0, c0 = at(2 * tx + ty, tc)
                    src = _window(b[gi], r0 + i * rows, c0, rows, C)
                    cps.append(_remote(src, b[ri].at[dev, pl.ds(i * rows, rows), :], sends, recvs, 7 * m + t, (tx, ty, tc)))
        return cps

    def start(b, sends, recvs):
        for cp in copies(b, sends, recvs, False):
            cp.start()

    def finish(b, sends, recvs):
        for cp in copies(b, sends, recvs, True):
            cp.wait_recv()
        for cp in copies(b, sends, recvs, False):
            cp.wait_send()

    return _Side(bufs, 7 * len(pieces), start, finish)


def _comm_only(side, name):
    n = len(side.bufs)

    def body(*refs):
        bufs, (sends, recvs) = refs[n:2 * n], refs[2 * n:]
        side.start(bufs, sends, recvs)
        side.finish(bufs, sends, recvs)

    outs = pl.pallas_call(
        body, name=name, out_shape=tuple(jax.ShapeDtypeStruct(b.shape, b.dtype) for b in side.bufs),
        in_specs=[ANY] * n, out_specs=tuple([ANY] * n), input_output_aliases={a: a for a in range(n)},
        scratch_shapes=[pltpu.SemaphoreType.DMA((side.nsem,)), pltpu.SemaphoreType.DMA((side.nsem,))],
    )(*side.bufs)
    return list(outs)


def _rs_sum8(grad, recv, win, kind, name):
    R, C = win
    tr = _row_tile(R, C)
    nr = R // tr
    xi, yi, ci = _me()
    s_arr = jnp.stack([4 * xi + 2 * yi + ci, ci]).astype(jnp.int32)

    if kind == "col":
        g_spec = BS((tr, C), lambda i, s: (s[1] * nr + i, s[0] // 2))
    else:
        g_spec = BS((tr, C), lambda i, s: (s[0] * nr + i, 0))

    def body(s_ref, g_ref, *refs):
        acc = g_ref[...].astype(F32)
        for r_ref in refs[:7]:
            acc = acc + r_ref[0].astype(F32)
        refs[7][...] = acc

    def other(t):
        return BS((1, tr, C), lambda i, s: (jnp.bitwise_xor(s[0], t), i, 0))

    return pl.pallas_call(
        body, name=name, out_shape=jax.ShapeDtypeStruct((2 * R, C), F32),
        grid_spec=pltpu.PrefetchScalarGridSpec(
            num_scalar_prefetch=1, grid=(nr,), in_specs=[g_spec] + [other(t) for t in range(1, 8)],
            out_specs=BS((tr, C), lambda i, s: (s[1] * nr + i, 0))),
        compiler_params=_cp(("parallel",)),
    )(s_arr, grad, *([recv] * 7))


def _share_side(shards):
    def half(b, a, pc):
        R = b[a].shape[0] // 2
        return b[a].at[pl.ds(pl.multiple_of(pc * R, SUBLANES), R), :]

    def start(b, sends, recvs):
        x, y, c = _me()
        for a in range(len(shards)):
            _remote(half(b, a, c), half(b, a, c), sends, recvs, a, (x, y, 1 - c)).start()

    def finish(b, sends, recvs):
        x, y, c = _me()
        for a in range(len(shards)):
            _remote(half(b, a, 1 - c), half(b, a, 1 - c), sends, recvs, a, (x, y, 1 - c)).wait_recv()
        for a in range(len(shards)):
            _remote(half(b, a, c), half(b, a, c), sends, recvs, a, (x, y, 1 - c)).wait_send()

    return _Side(shards, len(shards), start, finish)


BLOB_ALIGN = SUBLANES * LANES


def _pack(arrs):
    flat = jnp.concatenate([a.reshape(-1).astype(F32) for a in arrs])
    n = flat.shape[0]
    padded = -(-n // BLOB_ALIGN) * BLOB_ALIGN
    return jnp.pad(flat, (0, padded - n)).reshape(padded // LANES, LANES)


def _unpack(flat, shapes):
    out, off = [], 0
    for s in shapes:
        n = math.prod(s)
        out.append(flat[..., off:off + n].reshape(flat.shape[:-1] + tuple(s)))
        off += n
    return out


def _gather_blob(blob, name):
    r = blob.shape[0]
    at = lambda px, py, pc: ((4 * px + 2 * py + pc) * r, 0)
    (out,) = _allgather8([(blob, lambda px, py, pc: (0, 0), (r, LANES), (8 * r, LANES), at)], name)
    return out.reshape(8, r * LANES)


def _conv_ffn_fwd(mm, full, layer, X, mods, n2g, conv_w, conv_b, L, tb, nlat, ch, tag):
    sh2, sc2, g2 = mods[3], mods[4], mods[5]
    h2 = _norm_mod_fwd(X, n2g, sh2, sc2, tb, nlat, tag + "_norm2")
    zf = mm(h2, full[f"ffn_w_up{layer}"], "nn", BF16, tag + "_up", tm=(544, 512), tn=(2816, 512))
    f = _ffn_act_fwd(zf, conv_w, conv_b, L, ch, tag + "_act")
    Xn, yf = mm(f, full[f"ffn_w_down{layer}"], "nn", F32, tag + "_down", tm=(544, 512), tn=(512,), res=(X, g2, L))
    return Xn, (X, h2, zf, f, yf)


def _conv_ffn_bwd(mm, full, gbuf, layer, dXn, dy, dg2, before, saved, mods, n2g, conv_w, conv_b, L, tb, nlat, ch, tag):
    X, h2, zf, f, _ = saved
    sc2 = mods[4]
    w_up, w_down = full[f"ffn_w_up{layer}"], full[f"ffn_w_down{layer}"]
    df = mm(dy, w_down, "nt", BF16, tag + "_ddown_x", tm=(544, 512), tn=(2816, 512))
    gbuf[f"ffn_w_down{layer}"] = mm(f, dy, "tn", BF16, tag + "_ddown_w", tm=(512,), tn=(1024, 512))
    dzf, dcw, dcb = _ffn_act_bwd(zf, df, conv_w, conv_b, L, ch, tag + "_dact")
    gbuf[f"ffn_w_up{layer}"] = mm(h2, dzf, "tn", BF16, tag + "_dup_w", tm=(1024, 512), tn=(512,), outer="i", halves=True)
    dh2 = mm(dzf, w_up, "nt", BF16, tag + "_dup_x", tm=(1088, 1024, 512), tn=(1024,), tk=(2816, 512), halves=True)
    dX, dn2g, dsh2, dsc2, dy_before, dg_before = _norm_mod_bwd(X, n2g, sc2, dh2, dXn, tb, nlat, tag + "_dnorm2", before)
    return dX, dict(n2g=dn2g, sh2=dsh2, sc2=dsc2, g2=dg2, cw=dcw, cb=dcb), (dy_before, dg_before)


def kernel(x, c, ctx, c_ctx, norm1_g, norm2_g, w_ada, b_ada, ab_w_in, ab_b_in, a_ln_g, a_ln_b, a_w_s, a_b_s, b_conv_w, b_conv_b, b_ln_g, b_ln_b, ab_w_out, mla_w_in, mla_q_norm_g, mla_w_uq, mla_kv_norm_g, mla_w_ukv, mla_w_o, ffn_w_up, ffn_conv_w, ffn_conv_b, ffn_w_down, final_norm_g, loss_target, m_c_ctx, m_norm1_g, m_norm2_g, m_w_ada, m_b_ada, m_ab_w_in, m_ab_b_in, m_a_ln_g, m_a_ln_b, m_a_w_s, m_a_b_s, m_b_conv_w, m_b_conv_b, m_b_ln_g, m_b_ln_b, m_ab_w_out, m_mla_w_in, m_mla_q_norm_g, m_mla_w_uq, m_mla_kv_norm_g, m_mla_w_ukv, m_mla_w_o, m_ffn_w_up, m_ffn_conv_w, m_ffn_conv_b, m_ffn_w_down, m_final_norm_g, v_c_ctx, v_norm1_g, v_norm2_g, v_w_ada, v_b_ada, v_ab_w_in, v_ab_b_in, v_a_ln_g, v_a_ln_b, v_a_w_s, v_a_b_s, v_b_conv_w, v_b_conv_b, v_b_ln_g, v_b_ln_b, v_ab_w_out, v_mla_w_in, v_mla_q_norm_g, v_mla_w_uq, v_mla_kv_norm_g, v_mla_w_ukv, v_mla_w_o, v_ffn_w_up, v_ffn_conv_w, v_ffn_conv_b, v_ffn_w_down, v_final_norm_g):
    W = dict(c_ctx=c_ctx, norm1_g=norm1_g, norm2_g=norm2_g, w_ada=w_ada, b_ada=b_ada, ab_w_in=ab_w_in, ab_b_in=ab_b_in, a_ln_g=a_ln_g,
             a_ln_b=a_ln_b, a_w_s=a_w_s, a_b_s=a_b_s, b_conv_w=b_conv_w, b_conv_b=b_conv_b, b_ln_g=b_ln_g, b_ln_b=b_ln_b,
             ab_w_out=ab_w_out, mla_w_in=mla_w_in, mla_q_norm_g=mla_q_norm_g, mla_w_uq=mla_w_uq, mla_kv_norm_g=mla_kv_norm_g,
             mla_w_ukv=mla_w_ukv, mla_w_o=mla_w_o, ffn_w_up=ffn_w_up, ffn_conv_w=ffn_conv_w, ffn_conv_b=ffn_conv_b,
             ffn_w_down=ffn_w_down, final_norm_g=final_norm_g)
    MOM = dict(c_ctx=m_c_ctx, norm1_g=m_norm1_g, norm2_g=m_norm2_g, w_ada=m_w_ada, b_ada=m_b_ada, ab_w_in=m_ab_w_in, ab_b_in=m_ab_b_in,
               a_ln_g=m_a_ln_g, a_ln_b=m_a_ln_b, a_w_s=m_a_w_s, a_b_s=m_a_b_s, b_conv_w=m_b_conv_w, b_conv_b=m_b_conv_b,
               b_ln_g=m_b_ln_g, b_ln_b=m_b_ln_b, ab_w_out=m_ab_w_out, mla_w_in=m_mla_w_in, mla_q_norm_g=m_mla_q_norm_g,
               mla_w_uq=m_mla_w_uq, mla_kv_norm_g=m_mla_kv_norm_g, mla_w_ukv=m_mla_w_ukv, mla_w_o=m_mla_w_o, ffn_w_up=m_ffn_w_up,
               ffn_conv_w=m_ffn_conv_w, ffn_conv_b=m_ffn_conv_b, ffn_w_down=m_ffn_w_down, final_norm_g=m_final_norm_g)
    VAR = dict(c_ctx=v_c_ctx, norm1_g=v_norm1_g, norm2_g=v_norm2_g, w_ada=v_w_ada, b_ada=v_b_ada, ab_w_in=v_ab_w_in, ab_b_in=v_ab_b_in,
               a_ln_g=v_a_ln_g, a_ln_b=v_a_ln_b, a_w_s=v_a_w_s, a_b_s=v_a_b_s, b_conv_w=v_b_conv_w, b_conv_b=v_b_conv_b,
               b_ln_g=v_b_ln_g, b_ln_b=v_b_ln_b, ab_w_out=v_ab_w_out, mla_w_in=v_mla_w_in, mla_q_norm_g=v_mla_q_norm_g,
               mla_w_uq=v_mla_w_uq, mla_kv_norm_g=v_mla_kv_norm_g, mla_w_ukv=v_mla_w_ukv, mla_w_o=v_mla_w_o, ffn_w_up=v_ffn_w_up,
               ffn_conv_w=v_ffn_conv_w, ffn_conv_b=v_ffn_conv_b, ffn_w_down=v_ffn_w_down, final_norm_g=v_final_norm_g)
    ORDER = list(W.keys())

    L, D = x.shape[1], x.shape[2]
    CT = ctx.shape[1]
    T = L + CT
    AW, BW = a_ln_g.shape[-1], b_ln_g.shape[-1]
    AH = a_w_s.shape[1]
    QL, KVL = 4 * mla_q_norm_g.shape[-1], 4 * mla_kv_norm_g.shape[-1]
    H = 4 * mla_w_o.shape[1] // HEAD
    HS = H // 4
    DFF = ffn_conv_b.shape[-1]
    NA = w_ada.shape[-1]
    tb = 256 if (L % 256 == 0 and CT % 256 == 0) else 128
    nlat = L // tb
    ch = tb
    tq = 256 if L >= 512 else 128
    xi, yi, ci = _me()
    p_me = 2 * xi + yi
    dev = 4 * xi + 2 * yi + ci

    shard_small = [c[0], mla_q_norm_g[0], mla_kv_norm_g[0], b_conv_w[0], ffn_conv_w]
    g0 = _gather_blob(_pack(shard_small), "gather_small")
    c_all, qg_s, kvg_s, bcw_s, fcw_s = _unpack(g0, [a.shape for a in shard_small])
    per_chip = lambda a: a[0::2]
    qg = per_chip(qg_s).reshape(QL)
    kvg = per_chip(kvg_s).reshape(KVL)
    bcw = jnp.concatenate(list(per_chip(bcw_s)), axis=-1)
    fcw = jnp.concatenate(list(per_chip(fcw_s)), axis=-1)
    c16 = jnp.concatenate([c_all, c_ctx[None], jnp.zeros((7, D), F32)], axis=0)

    ms = []
    for i in range(2):
        bias = lax.dynamic_slice(b_ada[i], (p_me * NA,), (NA,))
        ms.append(_mm(c16, w_ada, "nn", F32, f"ada{i}", tm=(16,), tn=(512,), bias=bias, silu_a=True, b_lead=i))
    ms = jnp.concatenate(ms, axis=0)
    (mods_all,) = _allgather8(
        [(ms, lambda px, py, pc: (pc * 16, 0), (16, NA), (32, 4 * NA), lambda px, py, pc: (pc * 16, (2 * px + py) * NA))], "gather_mods")
    mods_all = mods_all.reshape(2, 16, N_MOD, D)
    mods = []
    for i in range(2):
        lat = lax.dynamic_index_in_dim(mods_all[i], dev, axis=0, keepdims=False)
        both = jnp.stack([lat, mods_all[i, 8]], axis=0)
        mods.append([both[:, k][:, None, :] for k in range(N_MOD)])

    def pad_uq(w):
        w = w.reshape(w.shape[0], HS, HEAD + ROPE)
        return jnp.pad(w, ((0, 0), (0, 0), (0, QHEAD - HEAD - ROPE))).reshape(w.shape[0], HS * QHEAD)

    MI = QL + KVL + LANES
    big = [
        ("ab_w_in", ab_w_in, 0, "col"), ("ab_w_out", ab_w_out, 0, "row"),
        ("mla_w_in", jnp.pad(mla_w_in[0], ((0, 0), (0, MI - mla_w_in.shape[-1]))), None, "row"),
        ("mla_w_uq", pad_uq(mla_w_uq[0]), None, "col"), ("mla_w_ukv", mla_w_ukv, 0, "col"), ("mla_w_o", mla_w_o, 0, "row"),
        ("ffn_w_up0", ffn_w_up, 0, "col"), ("ffn_w_up1", ffn_w_up, 1, "col"),
        ("ffn_w_down0", ffn_w_down, 0, "row"), ("ffn_w_down1", ffn_w_down, 1, "row"),
    ]
    full, wins = {}, {}
    for nm, w, lead, kind in big:
        R, C = w.shape[-2] // 2, w.shape[-1]
        if kind == "col":
            wins[nm] = ((R, C), lambda p, pc, R=R, C=C: (pc * R, p * C), kind)
        else:
            wins[nm] = ((R, C), lambda p, pc, R=R, C=C: ((2 * p + pc) * R, 0), kind)
        full[nm] = _cast_place(w, kind, "cast_" + nm, lead)
    names = [b[0] for b in big]

    up0 = lambda *parts: [("ffn_w_up0", 8, i) for i in parts]
    mla_a = [("ffn_w_down0", 1, 0), ("mla_w_in", 1, 0), ("mla_w_uq", 1, 0)]
    mla_b = [("mla_w_ukv", 1, 0), ("mla_w_o", 1, 0)]
    gather_plan = {
        "l0_norm1": ([("ab_w_out", 1, 0)], []), "l0_in": (up0(0, 1), [("ab_w_out", 1, 0)]), "l0_gmlp": (up0(2), up0(0, 1)),
        "l0_conv": (up0(3, 4), up0(2)), "l0_lnsilu": (up0(5), up0(3, 4)), "l0_out": (up0(6, 7), up0(5)),
        "l0_ffn_norm2": ([], up0(6, 7)), "l0_ffn_up": (mla_a, []), "l0_ffn_act": ([], mla_a), "l0_ffn_down": (mla_b, []),
        "l1_norm1": ([], mla_b),
    }
    u0 = lambda *parts: [("ffn_w_up0", 16, i) for i in parts]
    reduce_plan = {
        "l1_ffn_dact": [("ffn_w_down1", 2, 0)], "l1_ffn_dup_w": [("ffn_w_down1", 2, 1)], "l1_ffn_dup_x": [("ffn_w_up1", 4, 0)],
        "l1_dattn": [("ffn_w_up1", 4, 1), ("ffn_w_up1", 4, 2), ("ffn_w_up1", 4, 3), ("mla_w_o", 1, 0)],
        "l0_ffn_ddown_x": [("mla_w_ukv", 1, 0), ("mla_w_uq", 1, 0)], "l0_ffn_ddown_w": [("mla_w_in", 1, 0)],
        "l0_ffn_dact": [("ffn_w_down0", 2, 0)], "l0_ffn_dup_w": [("ffn_w_down0", 2, 1)],
        "l0_ffn_dup_x": u0(0, 1, 2, 3, 4, 5, 6), "l0_ffn_dnorm2": u0(7, 8), "l0_dout_w": u0(9),
        "l0_dout_x": [("ab_w_out", 2, 0)], "l0_dgmlp": u0(10, 11), "l0_dlnsilu": u0(12),
        "l0_dconv": u0(13, 14) + [("ab_w_out", 2, 1)], "l0_din_w": u0(15),
        "l0_din_x": [("ab_w_in", 4, 0), ("ab_w_in", 4, 1)], "l0_dnorm1": [("ab_w_in", 4, 2), ("ab_w_in", 4, 3)],
    }
    gbuf, rbuf = {}, {}

    def gather_side(req, passed=(), pass_now=False):
        nms = list(dict.fromkeys(nm for nm, _, _ in list(req) + list(passed)))
        piece = lambda nm, k, i: (nms.index(nm), wins[nm][0], wins[nm][1], k, i)
        side = _gather_side([full[nm] for nm in nms], [piece(*r) for r in req], [piece(*r) for r in passed], pass_now)

        def commit(bufs):
            full.update(zip(nms, bufs))

        return side, commit

    def reduce_side(req):
        nms = list(dict.fromkeys(nm for nm, _, _ in req))
        for nm in nms:
            if nm not in rbuf:
                rbuf[nm] = lax.empty((8,) + wins[nm][0], BF16)
        n = len(nms)
        side = _rs_side([gbuf[nm] for nm in nms] + [rbuf[nm] for nm in nms],
                        [(nms.index(nm), n + nms.index(nm), wins[nm][0], wins[nm][1], k, i) for nm, k, i in req])

        def commit(bufs):
            gbuf.update(zip(nms, bufs[:n]))
            rbuf.update(zip(nms, bufs[n:]))

        return side, commit

    _carry_plan.clear()
    _carry_plan.update({name: functools.partial(gather_side, req, passed) for name, (req, passed) in gather_plan.items()})
    _carry_plan["l1_attn"] = functools.partial(gather_side, [("ffn_w_up1", 1, 0), ("ffn_w_down1", 1, 0)], (), True)
    _carry_plan.update({name: functools.partial(reduce_side, req) for name, req in reduce_plan.items()})
    mm, attn_fwd, attn_bwd = _mm, _attn_fwd, _attn_bwd

    side, commit = gather_side([("ab_w_in", 1, 0)], (), True)
    commit(_comm_only(side, "gather_first"))

    X0 = jnp.concatenate([x[0], ctx[0]], axis=0)
    m0, m1 = mods[0], mods[1]
    h1 = _norm_mod_fwd(X0, norm1_g[0], m0[0], m0[1], tb, nlat, "l0_norm1")
    z = mm(h1, full["ab_w_in"], "nn", BF16, "l0_in", tm=(544, 512), tn=(1024, 512), bias=ab_b_in[0])
    bs_full = jnp.broadcast_to(a_b_s[0][:, :, None], (AH, CHUNK, CHUNK))
    ya = _gmlp_fwd(z, a_ln_g[0], a_ln_b[0], a_w_s[0], bs_full, tb, "l0_gmlp")
    hc = _glu_conv_fwd(z, 2 * AW, bcw, b_conv_b[0], L, ch, "l0_conv")
    yb = _ln_silu_fwd(hc, b_ln_g[0], b_ln_b[0], tb, "l0_lnsilu")
    yab = jnp.concatenate([ya, yb], axis=1)
    X1, y0 = mm(yab, full["ab_w_out"], "nn", F32, "l0_out", tm=(544, 512), tn=(1024, 512), res=(X0, m0[2], L))
    X2, ffn0 = _conv_ffn_fwd(mm, full, 0, X1, m0, norm2_g[0], fcw[0], ffn_conv_b[0], L, tb, nlat, ch, "l0_ffn")

    tabs = _rope_tables(L, T)
    hm = _norm_mod_fwd(X2, norm1_g[1], m1[0], m1[1], tb, nlat, "l1_norm1")
    zm = mm(hm, full["mla_w_in"], "nn", F32, "l1_in", tm=(544, 512), tn=(MI,))
    cqn, ckvn, kpe = _mla_prep_fwd(zm, qg, kvg, tabs, tb, "l1_prep")
    q = mm(cqn, full["mla_w_uq"], "nn", BF16, "l1_uq", tm=(512,), tn=(1024, 512), rows=L)
    kvh = mm(ckvn, full["mla_w_ukv"], "nn", BF16, "l1_ukv", tm=(544, 512), tn=(1024, 512))
    tq2 = 2 * tq if L % (2 * tq) == 0 and L > 2 * tq else tq
    wide = L % (4 * tq) == 0 and L > 4 * tq
    o, lse = attn_fwd(q, kvh, kpe, tabs, L, 4 * tq if wide else tq2, 4 if wide else 2, "l1_attn")
    m1_lat = [a[:1] for a in m1]
    X3, yl = mm(o, full["mla_w_o"], "nn", F32, "l1_o", tm=(512,), tn=(1024, 512), res=(X2, m1_lat[2], L))
    X4, ffn1 = _conv_ffn_fwd(mm, full, 1, X3, m1_lat, norm2_g[1], fcw[1], ffn_conv_b[1], L, tb, nlat, ch, "l1_ffn")
    loss_acc, dX4, dfinal, dyf1, dg2_1 = _final_loss(X4, final_norm_g, loss_target[0], (ffn1[4], m1_lat[5]), tb, "loss")

    dX3, gf1, (dyl, dg1_1) = _conv_ffn_bwd(mm, full, gbuf, 1, dX4, dyf1, dg2_1, (yl, m1_lat[2]), ffn1, m1_lat, norm2_g[1], fcw[1],
                                           ffn_conv_b[1], L, tb, nlat, ch, "l1_ffn")
    do = mm(dyl, full["mla_w_o"], "nt", BF16, "l1_do_x", tm=(512,), tn=(1024, 512))
    gbuf["mla_w_o"] = mm(o, dyl, "tn", BF16, "l1_do_w", tm=(512,), tn=(1024, 512))
    dq, dkv, dkpe = attn_bwd(q, kvh, kpe, tabs, o, lse, do, L, tq2, "l1_dattn")
    dckvn = mm(dkv, full["mla_w_ukv"], "nt", BF16, "l1_dukv_x", tm=(544, 512), tn=(KVL,), tk=(2048, 512))
    gbuf["mla_w_ukv"] = mm(ckvn, dkv, "tn", BF16, "l1_dukv_w", tm=(512,), tn=(1024, 512))
    dcqn = mm(dq, full["mla_w_uq"], "nt", BF16, "l1_duq_x", tm=(512,), tn=(QL,), tk=(2048, 512))
    gbuf["mla_w_uq"] = mm(cqn, dq, "tn", BF16, "l1_duq_w", tm=(QL,), tn=(1024, 512), rows=L)
    dcqn = jnp.concatenate([dcqn, jnp.zeros((CT, QL), BF16)], axis=0)
    dzm, dqg, dkvg = _mla_prep_bwd(zm, qg, kvg, tabs, dcqn, dckvn, dkpe, tb, "l1_dprep")
    dhm = mm(dzm, full["mla_w_in"], "nt", BF16, "l1_din_x", tm=(544, 512), tn=(1024, 512))
    gbuf["mla_w_in"] = mm(hm, dzm, "tn", BF16, "l1_din_w", tm=(512,), tn=(MI,))
    dX2, dn1g_1, dsh1_1, dsc1_1, dyf0, dg2_0 = _norm_mod_bwd(X2, norm1_g[1], m1[1], dhm, dX3, tb, nlat, "l1_dnorm1", (ffn0[4], m0[5]))

    dX1, gf0, (dy0, dg1_0) = _conv_ffn_bwd(mm, full, gbuf, 0, dX2, dyf0, dg2_0, (y0, m0[2]), ffn0, m0, norm2_g[0], fcw[0],
                                           ffn_conv_b[0], L, tb, nlat, ch, "l0_ffn")
    gbuf["ab_w_out"] = mm(yab, dy0, "tn", BF16, "l0_dout_w", tm=(512,), tn=(1024, 512))
    dyab = mm(dy0, full["ab_w_out"], "nt", BF16, "l0_dout_x", tm=(544, 512), tn=(1024, 512))
    dzu, dzv, dlnag, dlnab, dws, dbs, csu, csv = _gmlp_bwd(z, dyab, a_ln_g[0], a_ln_b[0], a_w_s[0], bs_full, tb, "l0_dgmlp")
    dhc, dlnbg, dlnbb = _ln_silu_bwd(hc, dyab, AW, b_ln_g[0], b_ln_b[0], tb, "l0_dlnsilu")
    dza, dzg, dbcw, dbcb, csa, csg = _glu_conv_bwd(z, 2 * AW, dhc, bcw, L, ch, "l0_dconv")
    dz = jnp.concatenate([dzu, dzv, dza, dzg], axis=1)
    dbin = jnp.concatenate([csu, csv, csa, csg], axis=1)
    gbuf["ab_w_in"] = mm(h1, dz, "tn", BF16, "l0_din_w", tm=(512,), tn=(1024, 512))
    dh1 = mm(dz, full["ab_w_in"], "nt", BF16, "l0_din_x", tm=(544, 512), tn=(1024, 512), tk=(2048, 512))
    dX0, dn1g_0, dsh1_0, dsc1_0 = _norm_mod_bwd(X0, norm1_g[0], m0[1], dh1, dX1, tb, nlat, "l0_dnorm1")
    grad_x = dX0[:L][None]

    halves = [_rs_sum8(gbuf[nm], rbuf[nm], wins[nm][0], wins[nm][2], "rs_sum_" + nm) for nm in names]

    def grp6(l, sh1, sc1, g1, f):
        G = sh1.shape[0]
        pad = lambda a: jnp.concatenate([a, jnp.zeros((G - a.shape[0],) + a.shape[1:], F32)], axis=0) if a.shape[0] < G else a
        return jnp.concatenate([pad(a) for a in (sh1, sc1, g1, f["sh2"], f["sc2"], f["g2"])], axis=1)

    dm0 = grp6(0, dsh1_0, dsc1_0, dg1_0, gf0)
    dm1 = grp6(1, dsh1_1, dsc1_1, dg1_1, gf1)
    dmods = jnp.stack([dm0, dm1], axis=0)
    small = [
        jnp.concatenate([dn1g_0, dn1g_1], axis=0), jnp.concatenate([gf0["n2g"], gf1["n2g"]], axis=0), dbin, dlnag, dlnab, dws,
        dbs, dbcw, dbcb, dlnbg, dlnbb, dqg, dkvg, jnp.stack([gf0["cw"], gf1["cw"]], axis=0),
        jnp.concatenate([gf0["cb"], gf1["cb"]], axis=0), dfinal, dmods[:, 1],
    ]
    small_shapes = [a.shape for a in small]
    lat_shape = dmods[:, 0].shape
    blob = _pack(small + [dmods[:, 0], loss_acc[0, :1]])
    gathered = _gather_blob(blob, "gather_grads")
    summed = _sum_lead(gathered.reshape(8, -1, LANES), F32, "sum_grads").reshape(-1)
    (dn1g, dn2g, dbin_s, dlnag_s, dlnab_s, dws_s, dbs_s, dbcw_s, dbcb_s, dlnbg_s, dlnbb_s, dqg_s, dkvg_s, dfcw_s, dfcb_s, dfinal_s,
     dmods_ctx, dmods_lat_sum, loss) = _unpack(summed, small_shapes + [lat_shape, (1,)])
    loss = loss.reshape(())
    n_small = sum(math.prod(s) for s in small_shapes)
    dmods_lat = gathered[:, n_small:n_small + math.prod(lat_shape)].reshape((8,) + lat_shape)

    grad_w_ada, dc_parts = [], []
    for i in range(2):
        dm16 = jnp.concatenate([dmods_lat[:, i].reshape(8, N_MOD * D), dmods_ctx[i].reshape(1, N_MOD * D),
                                jnp.zeros((7, N_MOD * D), F32)], axis=0)
        dm16_s = lax.dynamic_slice(dm16, (0, p_me * NA), (16, NA))
        grad_w_ada.append(_mm(c16, dm16_s, "tn", F32, f"dada{i}_w", tm=(1024, 512), tn=(1024, 512), silu_a=True))
        dc_parts.append(_mm(dm16_s, w_ada, "nt", F32, f"dada{i}_c", tm=(16,), tn=(512,), tk=(1024, 512), b_lead=i))
    grad_b_ada = dmods_lat_sum.reshape(2, N_MOD * D) + dmods_ctx.reshape(2, N_MOD * D)
    dc_blob = _pack([dc_parts[0][8] + dc_parts[1][8]])
    dc_all = _gather_blob(dc_blob, "gather_dc")[0::2, :D]
    grad_c_ctx = _cctx_grad(dc_all, c_ctx, "dcctx").reshape(D)

    gshard = dict(zip(names, _comm_only(_share_side(halves), "rs_share")))
    two = lambda a: a.reshape(-1, a.shape[-1])
    ada_update = _adamw_layers(w_ada, grad_w_ada, m_w_ada, v_w_ada, "adamw_w_ada")

    def my_cols(a, axis, n):
        return lax.dynamic_slice_in_dim(a, p_me * n, n, axis=axis)

    unpad_uq = lambda g: g.reshape(QL, HS, QHEAD)[:, :, :HEAD + ROPE].reshape(QL, HS * (HEAD + ROPE))
    grads = dict(
        c_ctx=grad_c_ctx, norm1_g=dn1g, norm2_g=dn2g, b_ada=grad_b_ada, ab_w_in=gshard["ab_w_in"][None],
        ab_b_in=dbin_s, a_ln_g=dlnag_s, a_ln_b=dlnab_s, a_w_s=dws_s[None], a_b_s=dbs_s.reshape(1, AH, CHUNK),
        b_conv_w=my_cols(dbcw_s, 1, BW // 4)[None], b_conv_b=dbcb_s, b_ln_g=dlnbg_s, b_ln_b=dlnbb_s,
        ab_w_out=gshard["ab_w_out"][None], mla_w_in=gshard["mla_w_in"][:, :mla_w_in.shape[-1]][None],
        mla_q_norm_g=my_cols(dqg_s, 1, QL // 4), mla_w_uq=unpad_uq(gshard["mla_w_uq"])[None],
        mla_kv_norm_g=my_cols(dkvg_s, 1, KVL // 4), mla_w_ukv=gshard["mla_w_ukv"][None], mla_w_o=gshard["mla_w_o"][None],
        ffn_conv_w=my_cols(dfcw_s, 2, DFF // 4), ffn_conv_b=dfcb_s, final_norm_g=dfinal_s.reshape(D),
    )
    LAYERED = ("w_ada", "ffn_w_up", "ffn_w_down")
    grads = {k: grads[k].reshape(W[k].shape) for k in ORDER if k not in LAYERED}

    BIG = ("ab_w_in", "ab_w_out", "mla_w_in", "mla_w_uq", "mla_w_ukv", "mla_w_o") + LAYERED
    delta, new_m, new_v = {}, {}, {}
    for k in BIG:
        if k == "w_ada":
            grads[k], delta[k], new_m[k], new_v[k] = ada_update
        elif k in LAYERED:
            grads[k], delta[k], new_m[k], new_v[k] = _adamw_layers(W[k], [gshard[k + "0"], gshard[k + "1"]], MOM[k], VAR[k], "adamw_" + k)
        else:
            upd = _adamw(two(W[k]), two(grads[k]), two(MOM[k]), two(VAR[k]), "adamw_" + k)
            delta[k], new_m[k], new_v[k] = (a.reshape(W[k].shape) for a in upd)
    SMALL = [k for k in ORDER if k not in BIG]
    small_upd = _adamw(_pack([W[k] for k in SMALL]), _pack([grads[k] for k in SMALL]), _pack([MOM[k] for k in SMALL]),
                       _pack([VAR[k] for k in SMALL]), "adamw_small")
    shapes = [W[k].shape for k in SMALL]
    for k, d_k, m_k, v_k in zip(SMALL, *[_unpack(a.reshape(-1), shapes) for a in small_upd]):
        delta[k], new_m[k], new_v[k] = d_k, m_k, v_k

    return (loss, grad_x, *[grads[k] for k in ORDER], *[delta[k] for k in ORDER], *[new_m[k] for k in ORDER],
            *[new_v[k] for k in ORDER])
```

```python
import functools
import math

import jax
import jax.numpy as jnp
from jax import lax
from jax.experimental import pallas as pl
from jax.experimental.pallas import tpu as pltpu

F32 = jnp.float32
BF16 = jnp.bfloat16
MESH = pl.DeviceIdType.MESH
ANY = pl.BlockSpec(memory_space=pl.ANY)

EPS = 1e-6
N_MOD = 6
CHUNK = 128
HEAD = 128
ROPE = 64
QHEAD = 2 * HEAD
GRID_W = 64
ROPE_THETA = 10000.0
B_CONV = 31
FFN_CONV = 3
ADAM_LR, ADAM_B1, ADAM_B2, ADAM_EPS, ADAM_WD, ADAM_STEP = 0.001, 0.9, 0.999, 1e-08, 0.01, 10

V7X_VMEM_LIMIT = 56 * 1024 * 1024
LANES = 128
SUBLANES = 8
CONV_PAD = 16

BS = pl.BlockSpec


def _cp(sem=None, vmem=V7X_VMEM_LIMIT):
    return pltpu.CompilerParams(dimension_semantics=sem, vmem_limit_bytes=vmem)


class _Side:
    def __init__(self, bufs, nsem, start, finish):
        self.bufs, self.nsem, self.start, self.finish = list(bufs), nsem, start, finish


_carry_plan = {}


def _call(body, *, name, out_shape, grid, in_specs, out_specs, operands, sem, scratch=()):
    if name not in _carry_plan:
        return pl.pallas_call(body, name=name, out_shape=out_shape, grid=grid, in_specs=in_specs, out_specs=out_specs,
                              scratch_shapes=list(scratch), compiler_params=_cp(sem))(*operands)
    side, commit = _carry_plan[name]()
    multi = isinstance(out_shape, (tuple, list))
    outs = tuple(out_shape) if multi else (out_shape,)
    ospecs = tuple(out_specs) if multi else (out_specs,)
    n_in, n_out, n_scr, n_buf = len(in_specs), len(outs), len(scratch), len(side.bufs)

    def body2(*refs):
        o0 = n_in + n_buf
        s0 = o0 + n_out + n_buf
        bufs = refs[o0 + n_out:s0]
        sends, recvs = refs[s0 + n_scr:]
        first = last = None
        for d, g in enumerate(grid):
            pid = pl.program_id(d)
            first = (pid == 0) if first is None else jnp.logical_and(first, pid == 0)
            last = (pid == g - 1) if last is None else jnp.logical_and(last, pid == g - 1)

        @pl.when(first)
        def _():
            side.start(bufs, sends, recvs)

        body(*refs[:n_in], *refs[o0:o0 + n_out], *refs[s0:s0 + n_scr])

        @pl.when(last)
        def _():
            side.finish(bufs, sends, recvs)

    res = pl.pallas_call(
        body2, name=name, out_shape=outs + tuple(jax.ShapeDtypeStruct(b.shape, b.dtype) for b in side.bufs), grid=grid,
        in_specs=list(in_specs) + [ANY] * n_buf, out_specs=ospecs + (ANY,) * n_buf,
        scratch_shapes=list(scratch) + [pltpu.SemaphoreType.DMA((side.nsem,)), pltpu.SemaphoreType.DMA((side.nsem,))],
        input_output_aliases={n_in + i: n_out + i for i in range(n_buf)}, compiler_params=_cp(("arbitrary",) * len(grid)),
    )(*operands, *side.bufs)
    commit(list(res[n_out:]))
    return tuple(res[:n_out]) if multi else res[0]


def _pcs(body, *, name, out_shape, grid, in_specs, out_specs, compiler_params, scratch_shapes=()):
    def run(*operands):
        return _call(body, name=name, out_shape=out_shape, grid=grid, in_specs=in_specs, out_specs=out_specs, operands=operands,
                     sem=compiler_params.dimension_semantics, scratch=scratch_shapes)
    return run


def _pick(n, prefs):
    for p in prefs:
        if p <= n and n % p == 0:
            return p
    return n


def _row_tile(rows, cols, target_bytes=1 << 20):
    best = None
    for d in range(2 * SUBLANES, rows + 1, 2 * SUBLANES):
        if rows % d == 0 and d * cols * 4 <= target_bytes:
            best = d
    return best if best is not None else rows


def _sigmoid(x):
    return 1.0 / (1.0 + jnp.exp(-x))


def _gelu(x):
    c = math.sqrt(2.0 / math.pi)
    th = jnp.tanh(c * (x + 0.044715 * x * x * x))
    return 0.5 * x * (1.0 + th), th


def _gelu_grad(x, th):
    c = math.sqrt(2.0 / math.pi)
    return 0.5 * (1.0 + th) + 0.5 * x * (1.0 - th * th) * c * (1.0 + 3.0 * 0.044715 * x * x)


_DIMS = {"nn": (((1,), (0,)), ((), ())), "nt": (((1,), (1,)), ((), ())), "tn": (((0,), (0,)), ((), ()))}


def _mm(a, b, mode, out_dtype, name, tm=(512,), tn=(512,), tk=(100000,), bias=None, silu_a=False, rows=None, outer="j", b_lead=None,
        halves=False, res=None):
    bshape = b.shape[-2:]
    if mode == "nn":
        (M, K), N = a.shape, bshape[1]
    elif mode == "nt":
        (M, K), N = a.shape[-2:], bshape[0]
        K = 2 * K if halves else K
    else:
        (K, M), N = a.shape, bshape[1]
        N = 2 * N if halves else N
    if rows is not None:
        if mode == "tn":
            K = rows
        else:
            M = rows
    tm, tn, tk = _pick(M, tm), _pick(N, tn), _pick(K, tk)
    gm, gn, gk = M // tm, N // tn, K // tk

    def ij(g0, g1):
        return (g1, g0) if outer == "j" else (g0, g1)

    if mode == "nn":
        a_spec = BS((tm, tk), lambda g0, g1, k: (ij(g0, g1)[0], k))
        b_spec = BS((tk, tn), lambda g0, g1, k: (k, ij(g0, g1)[1]))
    elif mode == "nt":
        a_spec = BS((tm, tk), lambda g0, g1, k: (ij(g0, g1)[0], k))
        b_spec = BS((tn, tk), lambda g0, g1, k: (ij(g0, g1)[1], k))
    else:
        a_spec = BS((tk, tm), lambda g0, g1, k: (k, ij(g0, g1)[0]))
        b_spec = BS((tk, tn), lambda g0, g1, k: (k, ij(g0, g1)[1]))
    if halves and mode == "nt":
        per = K // 2 // tk
        a_spec = BS((None, tm, tk), lambda g0, g1, k: (k // per, ij(g0, g1)[0], k % per))
    if halves and mode == "tn":
        per = N // 2 // tn
        b_spec = BS((None, tk, tn), lambda g0, g1, k: (ij(g0, g1)[1] // per, k, ij(g0, g1)[1] % per))
    if b_lead is not None:
        blk, at = b_spec.block_shape, b_spec.index_map
        b_spec = BS((None,) + tuple(blk), lambda g0, g1, k: (b_lead,) + tuple(at(g0, g1, k)))
    in_specs = [a_spec, b_spec]
    operands = [a, b]
    if bias is not None:
        in_specs.append(BS((1, tn), lambda g0, g1, k: (0, ij(g0, g1)[1])))
        operands.append(bias.reshape(1, N))
    o_spec = BS((tm, tn), lambda g0, g1, k: ij(g0, g1))
    out_shape, out_specs = jax.ShapeDtypeStruct((M, N), out_dtype), o_spec
    if res is not None:
        x_res, gate, lat_rows = res
        G = gate.shape[0]
        in_specs += [o_spec, BS((G, 1, tn), lambda g0, g1, k: (0, 0, ij(g0, g1)[1]))]
        operands += [x_res, gate]
        out_shape, out_specs = (jax.ShapeDtypeStruct((M, N), F32), jax.ShapeDtypeStruct((M, N), BF16)), (o_spec, o_spec)
    n_in = len(in_specs)

    def body(*refs):
        a_ref, b_ref = refs[0], refs[1]
        bias_ref = refs[2] if bias is not None else None
        o_ref = refs[n_in]
        av = a_ref[...]
        if silu_a:
            av = av.astype(F32)
            av = av * _sigmoid(av)
        part = lax.dot_general(av.astype(BF16), b_ref[...].astype(BF16), _DIMS[mode], preferred_element_type=F32)
        i_blk = pl.program_id(1 if outer == "j" else 0)

        def finish(acc):
            if bias_ref is not None:
                acc = acc + bias_ref[...]
            if res is None:
                o_ref[...] = acc.astype(out_dtype)
                return
            x_ref, gate_ref, y_ref = refs[n_in - 2], refs[n_in - 1], refs[n_in + 1]
            gate_ = gate_ref[0]
            if G == 2:
                rows = i_blk * tm + lax.broadcasted_iota(jnp.int32, (tm, 1), 0)
                gate_ = jnp.where(rows < lat_rows, gate_ref[0], gate_ref[1])
            o_ref[...] = x_ref[...] + gate_ * acc
            y_ref[...] = acc.astype(BF16)

        if gk == 1:
            finish(part)
        else:
            acc_ref = refs[-1]
            k = pl.program_id(2)

            @pl.when(k == 0)
            def _():
                acc_ref[...] = part

            @pl.when(k > 0)
            def _():
                acc_ref[...] += part

            @pl.when(k == gk - 1)
            def _():
                finish(acc_ref[...])

    grid = (gn, gm, gk) if outer == "j" else (gm, gn, gk)
    return _call(body, name=name, out_shape=out_shape, grid=grid, in_specs=in_specs, out_specs=out_specs,
                 scratch=[pltpu.VMEM((tm, tn), F32)] if gk > 1 else [], sem=("parallel", "parallel", "arbitrary"),
                 operands=operands)


def _accum(ref, val, first):
    @pl.when(first)
    def _():
        ref[...] = val

    @pl.when(jnp.logical_not(first))
    def _():
        ref[...] += val


def _norm_mod_fwd(X, gain, sh, sc, tb, nlat, name):
    R, D = X.shape

    def body(x_ref, g_ref, sh_ref, sc_ref, o_ref):
        x = x_ref[...]
        r = lax.rsqrt(jnp.mean(x * x, axis=-1, keepdims=True) + EPS)
        o_ref[...] = ((x * r * g_ref[...]) * (1.0 + sc_ref[0]) + sh_ref[0]).astype(BF16)

    grp = BS((1, 1, D), lambda i: (i // nlat, 0, 0))
    return _pcs(
        body, name=name, out_shape=jax.ShapeDtypeStruct((R, D), BF16), grid=(R // tb,),
        in_specs=[BS((tb, D), lambda i: (i, 0)), BS((1, D), lambda i: (0, 0)), grp, grp],
        out_specs=BS((tb, D), lambda i: (i, 0)), compiler_params=_cp(("parallel",)),
    )(X, gain.reshape(1, D), sh, sc)


def _gate_back(dx, y_ref, gate_ref, dy_ref, dgate_ref, first):
    dy_ref[...] = (gate_ref[0] * dx).astype(BF16)
    _accum(dgate_ref, jnp.sum(dx * y_ref[...], axis=0, keepdims=True)[None], first)


def _norm_mod_bwd(X, gain, sc, dh, dup, tb, nlat, name, before=None):
    R, D = X.shape
    G = sc.shape[0]
    n_up = dup.shape[0] // tb

    def body(x_ref, g_ref, sc_ref, dh_ref, dup_ref, *rest):
        dx_ref, dg_ref, dsh_ref, dsc_ref = rest[2:6] if before else rest[:4]
        i = pl.program_id(0)
        x = x_ref[...]
        g = g_ref[...]
        r = lax.rsqrt(jnp.mean(x * x, axis=-1, keepdims=True) + EPS)
        xh = x * r
        dh_ = dh_ref[...].astype(F32)
        t = dh_ * (1.0 + sc_ref[0])
        tg = t * g
        dup_ = dup_ref[...] if n_up == R // tb else jnp.where(i < n_up, dup_ref[...], 0.0)
        dx = dup_ + r * (tg - xh * jnp.mean(tg * xh, axis=-1, keepdims=True))
        dx_ref[...] = dx
        _accum(dg_ref, jnp.sum(t * xh, axis=0, keepdims=True), i == 0)
        first = i % nlat == 0
        _accum(dsh_ref, jnp.sum(dh_, axis=0, keepdims=True)[None], first)
        _accum(dsc_ref, jnp.sum(dh_ * xh * g, axis=0, keepdims=True)[None], first)
        if before:
            _gate_back(dx, rest[0], rest[1], rest[6], rest[7], first)

    row = BS((tb, D), lambda i: (i, 0))
    grp = BS((1, 1, D), lambda i: (i // nlat, 0, 0))
    vec = BS((1, D), lambda i: (0, 0))
    sds = jax.ShapeDtypeStruct
    out_shape = (sds((R, D), F32), sds((1, D), F32), sds((G, 1, D), F32), sds((G, 1, D), F32))
    in_specs, out_specs = [row, vec, grp, row, BS((tb, D), lambda i: (jnp.minimum(i, n_up - 1), 0))], (row, vec, grp, grp)
    operands = (X, gain.reshape(1, D), sc, dh, dup)
    if before:
        in_specs, operands = in_specs + [row, grp], operands + tuple(before)
        out_shape, out_specs = out_shape + (sds((R, D), BF16), sds((G, 1, D), F32)), out_specs + (row, grp)
    return _pcs(body, name=name, out_shape=out_shape, grid=(R // tb,), in_specs=in_specs, out_specs=out_specs,
                compiler_params=_cp(("arbitrary",)))(*operands)


def _final_loss(X, gain, target, before, tb, name):
    R, D = X.shape

    def body(x_ref, g_ref, t_ref, y_ref, gate_ref, loss_ref, dx_ref, dg_ref, dy_ref, dgate_ref):
        i = pl.program_id(0)
        x = x_ref[...]
        g = g_ref[...]
        r = lax.rsqrt(jnp.mean(x * x, axis=-1, keepdims=True) + EPS)
        xh = x * r
        e = xh * g - t_ref[...]
        part = jnp.sum(jnp.sum(e * e, axis=1, keepdims=True), axis=0, keepdims=True) * (0.5 / D)
        _accum(loss_ref, jnp.broadcast_to(part, (1, LANES)), i == 0)
        dy = e * (1.0 / D)
        _accum(dg_ref, jnp.sum(dy * xh, axis=0, keepdims=True), i == 0)
        tg = dy * g
        dx = r * (tg - xh * jnp.mean(tg * xh, axis=-1, keepdims=True))
        dx_ref[...] = dx
        _gate_back(dx, y_ref, gate_ref, dy_ref, dgate_ref, i == 0)

    row = BS((tb, D), lambda i: (i, 0))
    vec = BS((1, D), lambda i: (0, 0))
    grp = BS((1, 1, D), lambda i: (0, 0, 0))
    sds = jax.ShapeDtypeStruct
    return _pcs(
        body, name=name,
        out_shape=(sds((1, LANES), F32), sds((R, D), F32), sds((1, D), F32), sds((R, D), BF16), sds((1, 1, D), F32)),
        grid=(R // tb,), in_specs=[row, vec, row, row, grp],
        out_specs=(BS((1, LANES), lambda i: (0, 0)), row, vec, row, grp), compiler_params=_cp(("arbitrary",)),
    )(X, gain.reshape(1, D), target, *before)


def _ln_stats(v):
    mu = jnp.mean(v, axis=-1, keepdims=True)
    d = v - mu
    r = lax.rsqrt(jnp.mean(d * d, axis=-1, keepdims=True) + EPS)
    return d * r, r


def _gmlp_fwd(z, ln_g, ln_b, w_s, b_s_full, tb, name):
    R = z.shape[0]
    AW = ln_g.shape[-1]
    AH = w_s.shape[0]

    def body(zu_ref, zv_ref, g_ref, b_ref, ws_ref, bs_ref, o_ref):
        u, _ = _gelu(zu_ref[...].astype(F32))
        v, _ = _gelu(zv_ref[...].astype(F32))
        xh, _ = _ln_stats(v)
        vn = (xh * g_ref[...] + b_ref[...]).astype(BF16)
        for n in range(tb // CHUNK):
            rs = slice(n * CHUNK, (n + 1) * CHUNK)
            for h in range(AH):
                cs = slice(h * CHUNK, (h + 1) * CHUNK)
                v2 = jnp.dot(ws_ref[h].astype(BF16), vn[rs, cs], preferred_element_type=F32) + bs_ref[h]
                o_ref[rs, cs] = (u[rs, cs] * v2).astype(BF16)

    full3 = lambda s: BS(s, lambda i: (0, 0, 0))
    return _pcs(
        body, name=name, out_shape=jax.ShapeDtypeStruct((R, AW), BF16), grid=(R // tb,),
        in_specs=[BS((tb, AW), lambda i: (i, 0)), BS((tb, AW), lambda i: (i, 1)), BS((1, AW), lambda i: (0, 0)),
                  BS((1, AW), lambda i: (0, 0)), full3((AH, CHUNK, CHUNK)), full3((AH, CHUNK, CHUNK))],
        out_specs=BS((tb, AW), lambda i: (i, 0)), compiler_params=_cp(("parallel",)),
    )(z, z, ln_g.reshape(1, AW), ln_b.reshape(1, AW), w_s, b_s_full)


def _gmlp_bwd(z, dya, ln_g, ln_b, w_s, b_s_full, tb, name):
    R = z.shape[0]
    AW = ln_g.shape[-1]
    AH = w_s.shape[0]

    def body(zu_ref, zv_ref, dy_ref, g_ref, b_ref, ws_ref, bs_ref, dzu_ref, dzv_ref, dg_ref, db_ref, dws_ref, dbs_ref, cs_u_ref, cs_v_ref,
             dvn_scr):
        i = pl.program_id(0)
        first = i == 0
        zu = zu_ref[...].astype(F32)
        zv = zv_ref[...].astype(F32)
        u, thu = _gelu(zu)
        v, thv = _gelu(zv)
        xh, r = _ln_stats(v)
        g = g_ref[...]
        vn = (xh * g + b_ref[...]).astype(BF16)
        dy = dy_ref[...].astype(F32)

        @pl.when(first)
        def _():
            dws_ref[...] = jnp.zeros_like(dws_ref)
            dbs_ref[...] = jnp.zeros_like(dbs_ref)

        for n in range(tb // CHUNK):
            rs = slice(n * CHUNK, (n + 1) * CHUNK)
            for h in range(AH):
                cs = slice(h * CHUNK, (h + 1) * CHUNK)
                w = ws_ref[h].astype(BF16)
                v2 = jnp.dot(w, vn[rs, cs], preferred_element_type=F32) + bs_ref[h]
                dzu_ref[rs, cs] = (dy[rs, cs] * v2 * _gelu_grad(zu[rs, cs], thu[rs, cs])).astype(BF16)
                dv2 = dy[rs, cs] * u[rs, cs]
                dv2b = dv2.astype(BF16)
                dvn_scr[rs, cs] = lax.dot_general(w, dv2b, _DIMS["tn"], preferred_element_type=F32)
                dws_ref[h] += lax.dot_general(dv2b, vn[rs, cs], _DIMS["nt"], preferred_element_type=F32)
                dbs_ref[h] += jnp.sum(dv2, axis=1, keepdims=True)
        dvn = dvn_scr[...]
        _accum(dg_ref, jnp.sum(dvn * xh, axis=0, keepdims=True), first)
        _accum(db_ref, jnp.sum(dvn, axis=0, keepdims=True), first)
        t = dvn * g
        dv = r * (t - jnp.mean(t, axis=-1, keepdims=True) - xh * jnp.mean(t * xh, axis=-1, keepdims=True))
        dzv = dv * _gelu_grad(zv, thv)
        dzv_ref[...] = dzv.astype(BF16)
        _accum(cs_v_ref, jnp.sum(dzv, axis=0, keepdims=True), first)
        _accum(cs_u_ref, jnp.sum(dzu_ref[...].astype(F32), axis=0, keepdims=True), first)

    full3 = lambda s: BS(s, lambda i: (0, 0, 0))
    vec = BS((1, AW), lambda i: (0, 0))
    row = BS((tb, AW), lambda i: (i, 0))
    outs = _pcs(
        body, name=name,
        out_shape=(jax.ShapeDtypeStruct((R, AW), BF16), jax.ShapeDtypeStruct((R, AW), BF16), jax.ShapeDtypeStruct((1, AW), F32),
                   jax.ShapeDtypeStruct((1, AW), F32), jax.ShapeDtypeStruct((AH, CHUNK, CHUNK), F32),
                   jax.ShapeDtypeStruct((AH, CHUNK, 1), F32), jax.ShapeDtypeStruct((1, AW), F32), jax.ShapeDtypeStruct((1, AW), F32)),
        grid=(R // tb,),
        in_specs=[row, BS((tb, AW), lambda i: (i, 1)), row, vec, vec, full3((AH, CHUNK, CHUNK)), full3((AH, CHUNK, CHUNK))],
        out_specs=(row, row, vec, vec, full3((AH, CHUNK, CHUNK)), full3((AH, CHUNK, 1)), vec, vec),
        scratch_shapes=[pltpu.VMEM((tb, AW), F32)], compiler_params=_cp(("arbitrary",)),
    )(z, z, dya, ln_g.reshape(1, AW), ln_b.reshape(1, AW), w_s, b_s_full)
    return outs


def _segments(R, L):
    return [(0, L)] + ([(L, R - L)] if R > L else [])


def _scr_rows(R, L):
    return R + CONV_PAD * (len(_segments(R, L)) + 1)


def _scr_off(s, start):
    return CONV_PAD * (s + 1) + start


def _zero_pads(scr, R, L):
    segs = _segments(R, L)
    z = jnp.zeros((CONV_PAD, scr.shape[1]), F32)
    for s, (start, n) in enumerate(segs):
        scr[pl.ds(_scr_off(s, start) - CONV_PAD, CONV_PAD), :] = z
    last_s, (last_start, last_n) = len(segs) - 1, segs[-1]
    scr[pl.ds(_scr_off(last_s, last_start) + last_n, CONV_PAD), :] = z


def _for_chunks(R, L, ch, fn):
    for s, (start, n) in enumerate(_segments(R, L)):
        c = min(ch, n)
        off = _scr_off(s, start)

        def step(i, carry, start=start, off=off, c=c):
            r0 = pl.multiple_of(start + i * c, SUBLANES)
            fn(r0, pl.multiple_of(off + i * c, SUBLANES), c)
            return carry

        lax.fori_loop(0, n // c, step, 0)


def _taps(scr, srow, c, w_ref, ntap, flip):
    acc = None
    for k in range(ntap):
        o = k - (ntap - 1) // 2
        if flip:
            o = -o
        term = w_ref[k:k + 1, :] * scr[pl.ds(srow + o, c), :]
        acc = term if acc is None else acc + term
    return acc


def _tap_grads(scr, srow, c, dy, dw_ref, ntap):
    for k in range(ntap):
        o = k - (ntap - 1) // 2
        dw_ref[k:k + 1, :] += jnp.sum(dy * scr[pl.ds(srow + o, c), :], axis=0, keepdims=True)


def _glu_conv_fwd(z, col0, conv_w, conv_b, L, ch, name):
    R = z.shape[0]
    BW = conv_w.shape[1]
    nb, c0 = BW // LANES, col0 // LANES

    def body(a_ref, g_ref, w_ref, b_ref, o_ref, scr):
        _zero_pads(scr, R, L)

        def fill(r0, s0, c):
            a = a_ref[pl.ds(r0, c), :].astype(F32)
            g = g_ref[pl.ds(r0, c), :].astype(F32)
            scr[pl.ds(s0, c), :] = a * _sigmoid(g)

        _for_chunks(R, L, ch, fill)

        def conv(r0, s0, c):
            o_ref[pl.ds(r0, c), :] = _taps(scr, s0, c, w_ref, B_CONV, False) + b_ref[...]

        _for_chunks(R, L, ch, conv)

    return _pcs(
        body, name=name, out_shape=jax.ShapeDtypeStruct((R, BW), F32), grid=(nb,),
        in_specs=[BS((R, LANES), lambda j: (0, c0 + j)), BS((R, LANES), lambda j: (0, c0 + nb + j)),
                  BS((B_CONV, LANES), lambda j: (0, j)), BS((1, LANES), lambda j: (0, j))],
        out_specs=BS((R, LANES), lambda j: (0, j)), scratch_shapes=[pltpu.VMEM((_scr_rows(R, L), LANES), F32)],
        compiler_params=_cp(("parallel",)),
    )(z, z, conv_w, conv_b.reshape(1, BW))


def _glu_conv_bwd(z, col0, dhc, conv_w, L, ch, name):
    R = z.shape[0]
    BW = conv_w.shape[1]
    nb, c0 = BW // LANES, col0 // LANES

    def body(a_ref, g_ref, dy_ref, w_ref, da_ref, dg_ref, dw_ref, db_ref, csa_ref, csg_ref, scr_h, scr_dy):
        _zero_pads(scr_h, R, L)
        _zero_pads(scr_dy, R, L)
        dw_ref[...] = jnp.zeros_like(dw_ref)
        db_ref[...] = jnp.zeros_like(db_ref)
        csa_ref[...] = jnp.zeros_like(csa_ref)
        csg_ref[...] = jnp.zeros_like(csg_ref)

        def fill(r0, s0, c):
            a = a_ref[pl.ds(r0, c), :].astype(F32)
            g = g_ref[pl.ds(r0, c), :].astype(F32)
            scr_h[pl.ds(s0, c), :] = a * _sigmoid(g)
            scr_dy[pl.ds(s0, c), :] = dy_ref[pl.ds(r0, c), :]

        _for_chunks(R, L, ch, fill)

        def back(r0, s0, c):
            dh = _taps(scr_dy, s0, c, w_ref, B_CONV, True)
            a = a_ref[pl.ds(r0, c), :].astype(F32)
            sg = _sigmoid(g_ref[pl.ds(r0, c), :].astype(F32))
            da = dh * sg
            dg = dh * a * sg * (1.0 - sg)
            da_ref[pl.ds(r0, c), :] = da.astype(BF16)
            dg_ref[pl.ds(r0, c), :] = dg.astype(BF16)
            csa_ref[...] += jnp.sum(da, axis=0, keepdims=True)
            csg_ref[...] += jnp.sum(dg, axis=0, keepdims=True)
            dy = dy_ref[pl.ds(r0, c), :]
            db_ref[...] += jnp.sum(dy, axis=0, keepdims=True)
            _tap_grads(scr_h, s0, c, dy, dw_ref, B_CONV)

        _for_chunks(R, L, ch, back)

    col = BS((R, LANES), lambda j: (0, j))
    vec = BS((1, LANES), lambda j: (0, j))
    nrow = _scr_rows(R, L)
    return _pcs(
        body, name=name,
        out_shape=(jax.ShapeDtypeStruct((R, BW), BF16), jax.ShapeDtypeStruct((R, BW), BF16), jax.ShapeDtypeStruct((B_CONV, BW), F32),
                   jax.ShapeDtypeStruct((1, BW), F32), jax.ShapeDtypeStruct((1, BW), F32), jax.ShapeDtypeStruct((1, BW), F32)),
        grid=(nb,),
        in_specs=[BS((R, LANES), lambda j: (0, c0 + j)), BS((R, LANES), lambda j: (0, c0 + nb + j)), col,
                  BS((B_CONV, LANES), lambda j: (0, j))],
        out_specs=(col, col, BS((B_CONV, LANES), lambda j: (0, j)), vec, vec, vec),
        scratch_shapes=[pltpu.VMEM((nrow, LANES), F32), pltpu.VMEM((nrow, LANES), F32)], compiler_params=_cp(("parallel",)),
    )(z, z, dhc, conv_w)


def _ln_silu_fwd(hc, ln_g, ln_b, tb, name):
    R, W = hc.shape

    def body(x_ref, g_ref, b_ref, o_ref):
        xh, _ = _ln_stats(x_ref[...])
        y = xh * g_ref[...] + b_ref[...]
        o_ref[...] = (y * _sigmoid(y)).astype(BF16)

    vec = BS((1, W), lambda i: (0, 0))
    row = BS((tb, W), lambda i: (i, 0))
    return _pcs(
        body, name=name, out_shape=jax.ShapeDtypeStruct((R, W), BF16), grid=(R // tb,), in_specs=[row, vec, vec], out_specs=row,
        compiler_params=_cp(("parallel",)),
    )(hc, ln_g.reshape(1, W), ln_b.reshape(1, W))


def _ln_silu_bwd(hc, dyb, col0, ln_g, ln_b, tb, name):
    R, W = hc.shape
    c0 = col0 // W

    def body(x_ref, dy_ref, g_ref, b_ref, dx_ref, dg_ref, db_ref):
        i = pl.program_id(0)
        xh, r = _ln_stats(x_ref[...])
        g = g_ref[...]
        y = xh * g + b_ref[...]
        s = _sigmoid(y)
        dy = dy_ref[...].astype(F32) * s * (1.0 + y * (1.0 - s))
        _accum(dg_ref, jnp.sum(dy * xh, axis=0, keepdims=True), i == 0)
        _accum(db_ref, jnp.sum(dy, axis=0, keepdims=True), i == 0)
        t = dy * g
        dx_ref[...] = r * (t - jnp.mean(t, axis=-1, keepdims=True) - xh * jnp.mean(t * xh, axis=-1, keepdims=True))

    vec = BS((1, W), lambda i: (0, 0))
    row = BS((tb, W), lambda i: (i, 0))
    return _pcs(
        body, name=name,
        out_shape=(jax.ShapeDtypeStruct((R, W), F32), jax.ShapeDtypeStruct((1, W), F32), jax.ShapeDtypeStruct((1, W), F32)),
        grid=(R // tb,), in_specs=[row, BS((tb, W), lambda i: (i, c0)), vec, vec], out_specs=(row, vec, vec),
        compiler_params=_cp(("arbitrary",)),
    )(hc, dyb, ln_g.reshape(1, W), ln_b.reshape(1, W))


def _ffn_act_fwd(zf, conv_w, conv_b, L, ch, name):
    R = zf.shape[0]
    DFF = conv_w.shape[1]
    nb = DFF // LANES

    def body(g_ref, u_ref, w_ref, b_ref, o_ref, scr):
        _zero_pads(scr, R, L)

        def fill(r0, s0, c):
            scr[pl.ds(s0, c), :] = g_ref[pl.ds(r0, c), :].astype(F32)

        _for_chunks(R, L, ch, fill)

        def act(r0, s0, c):
            gc = _taps(scr, s0, c, w_ref, FFN_CONV, False) + b_ref[...]
            o_ref[pl.ds(r0, c), :] = (gc * _sigmoid(gc) * u_ref[pl.ds(r0, c), :].astype(F32)).astype(BF16)

        _for_chunks(R, L, ch, act)

    return _pcs(
        body, name=name, out_shape=jax.ShapeDtypeStruct((R, DFF), BF16), grid=(nb,),
        in_specs=[BS((R, LANES), lambda j: (0, j)), BS((R, LANES), lambda j: (0, nb + j)), BS((FFN_CONV, LANES), lambda j: (0, j)),
                  BS((1, LANES), lambda j: (0, j))],
        out_specs=BS((R, LANES), lambda j: (0, j)), scratch_shapes=[pltpu.VMEM((_scr_rows(R, L), LANES), F32)],
        compiler_params=_cp(("parallel",)),
    )(zf, zf, conv_w, conv_b.reshape(1, DFF))


def _ffn_act_bwd(zf, df, conv_w, conv_b, L, ch, name):
    R = zf.shape[0]
    DFF = conv_w.shape[1]
    nb = DFF // LANES

    def body(g_ref, u_ref, df_ref, w_ref, b_ref, dz_ref, dw_ref, db_ref, scr_g, scr_d):
        _zero_pads(scr_g, R, L)
        _zero_pads(scr_d, R, L)
        dw_ref[...] = jnp.zeros_like(dw_ref)
        db_ref[...] = jnp.zeros_like(db_ref)

        def fill(r0, s0, c):
            scr_g[pl.ds(s0, c), :] = g_ref[pl.ds(r0, c), :].astype(F32)

        _for_chunks(R, L, ch, fill)

        def pre(r0, s0, c):
            gc = _taps(scr_g, s0, c, w_ref, FFN_CONV, False) + b_ref[...]
            s = _sigmoid(gc)
            d = df_ref[pl.ds(r0, c), :].astype(F32)
            dz_ref[1, pl.ds(r0, c), :] = (d * gc * s).astype(BF16)
            dgc = d * u_ref[pl.ds(r0, c), :].astype(F32) * s * (1.0 + gc * (1.0 - s))
            scr_d[pl.ds(s0, c), :] = dgc
            db_ref[...] += jnp.sum(dgc, axis=0, keepdims=True)
            _tap_grads(scr_g, s0, c, dgc, dw_ref, FFN_CONV)

        _for_chunks(R, L, ch, pre)

        def back(r0, s0, c):
            dz_ref[0, pl.ds(r0, c), :] = _taps(scr_d, s0, c, w_ref, FFN_CONV, True).astype(BF16)

        _for_chunks(R, L, ch, back)

    col = BS((R, LANES), lambda j: (0, j))
    vec = BS((1, LANES), lambda j: (0, j))
    nrow = _scr_rows(R, L)
    return _pcs(
        body, name=name,
        out_shape=(jax.ShapeDtypeStruct((2, R, DFF), BF16), jax.ShapeDtypeStruct((FFN_CONV, DFF), F32), jax.ShapeDtypeStruct((1, DFF), F32)),
        grid=(nb,),
        in_specs=[col, BS((R, LANES), lambda j: (0, nb + j)), col, BS((FFN_CONV, LANES), lambda j: (0, j)), vec],
        out_specs=(BS((2, R, LANES), lambda j: (0, 0, j)), BS((FFN_CONV, LANES), lambda j: (0, j)), vec),
        scratch_shapes=[pltpu.VMEM((nrow, LANES), F32), pltpu.VMEM((nrow, LANES), F32)], compiler_params=_cp(("parallel",)),
    )(zf, zf, df, conv_w, conv_b.reshape(1, DFF))


def _rope_tables(L, T):
    rows = L // GRID_W
    row = jnp.repeat(jnp.arange(rows, dtype=F32), GRID_W)
    col = jnp.tile(jnp.arange(GRID_W, dtype=F32), rows)
    n_freq = ROPE // 4
    inv = ROPE_THETA ** (-jnp.arange(n_freq, dtype=F32) / n_freq)
    ang = jnp.concatenate([row[:, None] * inv, col[:, None] * inv], axis=-1)
    cos, sin = jnp.cos(ang), jnp.sin(ang)
    half = ROPE // 2
    zero = jnp.zeros((L, half), F32)
    cos_t = jnp.concatenate([cos, cos, jnp.ones((L, LANES - ROPE), F32)], axis=1)
    sa = jnp.concatenate([zero, sin, zero, zero], axis=1)
    sb = jnp.concatenate([-sin, zero, zero, zero], axis=1)
    pad = T - L
    cos_t = jnp.concatenate([cos_t, jnp.ones((pad, LANES), F32)], axis=0)
    sa = jnp.concatenate([sa, jnp.zeros((pad, LANES), F32)], axis=0)
    sb = jnp.concatenate([sb, jnp.zeros((pad, LANES), F32)], axis=0)
    return cos_t, sa, sb


def _rope(x, cos, sa, sb):
    half = ROPE // 2
    return x * cos + pltpu.roll(x, half, 1) * sa + pltpu.roll(x, LANES - half, 1) * sb


def _rope_t(d, cos, sa, sb):
    half = ROPE // 2
    return d * cos + pltpu.roll(d * sa, LANES - half, 1) + pltpu.roll(d * sb, half, 1)


def _mla_prep_fwd(zm, qg, kvg, tabs, tb, name):
    T, W = zm.shape
    QL, KVL = qg.shape[-1], kvg.shape[-1]

    def body(z_ref, qg_ref, kg_ref, cos_ref, sa_ref, sb_ref, q_ref, k_ref, p_ref):
        cq = z_ref[:, :QL]
        r = lax.rsqrt(jnp.mean(cq * cq, axis=-1, keepdims=True) + EPS)
        q_ref[...] = (cq * r * qg_ref[...]).astype(BF16)
        ck = z_ref[:, QL:QL + KVL]
        r = lax.rsqrt(jnp.mean(ck * ck, axis=-1, keepdims=True) + EPS)
        k_ref[...] = (ck * r * kg_ref[...]).astype(BF16)
        p_ref[...] = _rope(z_ref[:, QL + KVL:], cos_ref[...], sa_ref[...], sb_ref[...]).astype(BF16)

    tab = BS((tb, LANES), lambda i: (i, 0))
    return _pcs(
        body, name=name,
        out_shape=(jax.ShapeDtypeStruct((T, QL), BF16), jax.ShapeDtypeStruct((T, KVL), BF16), jax.ShapeDtypeStruct((T, LANES), BF16)),
        grid=(T // tb,),
        in_specs=[BS((tb, W), lambda i: (i, 0)), BS((1, QL), lambda i: (0, 0)), BS((1, KVL), lambda i: (0, 0)), tab, tab, tab],
        out_specs=(BS((tb, QL), lambda i: (i, 0)), BS((tb, KVL), lambda i: (i, 0)), tab), compiler_params=_cp(("parallel",)),
    )(zm, qg.reshape(1, QL), kvg.reshape(1, KVL), *tabs)


def _mla_prep_bwd(zm, qg, kvg, tabs, dq, dk, dp, tb, name):
    T, W = zm.shape
    QL, KVL = qg.shape[-1], kvg.shape[-1]

    def body(z_ref, qg_ref, kg_ref, cos_ref, sa_ref, sb_ref, dq_ref, dk_ref, dp_ref, dz_ref, dqg_ref, dkg_ref):
        i = pl.program_id(0)

        def rms_bwd(x, g, dy):
            r = lax.rsqrt(jnp.mean(x * x, axis=-1, keepdims=True) + EPS)
            xh = x * r
            t = dy * g
            return r * (t - xh * jnp.mean(t * xh, axis=-1, keepdims=True)), jnp.sum(dy * xh, axis=0, keepdims=True)

        dcq, dg = rms_bwd(z_ref[:, :QL], qg_ref[...], dq_ref[...].astype(F32))
        dz_ref[:, :QL] = dcq.astype(BF16)
        _accum(dqg_ref, dg, i == 0)
        dck, dg = rms_bwd(z_ref[:, QL:QL + KVL], kg_ref[...], dk_ref[...].astype(F32))
        dz_ref[:, QL:QL + KVL] = dck.astype(BF16)
        _accum(dkg_ref, dg, i == 0)
        dz_ref[:, QL + KVL:] = _rope_t(dp_ref[...], cos_ref[...], sa_ref[...], sb_ref[...]).astype(BF16)

    tab = BS((tb, LANES), lambda i: (i, 0))
    return _pcs(
        body, name=name,
        out_shape=(jax.ShapeDtypeStruct((T, W), BF16), jax.ShapeDtypeStruct((1, QL), F32), jax.ShapeDtypeStruct((1, KVL), F32)),
        grid=(T // tb,),
        in_specs=[BS((tb, W), lambda i: (i, 0)), BS((1, QL), lambda i: (0, 0)), BS((1, KVL), lambda i: (0, 0)), tab, tab, tab,
                  BS((tb, QL), lambda i: (i, 0)), BS((tb, KVL), lambda i: (i, 0)), tab],
        out_specs=(BS((tb, W), lambda i: (i, 0)), BS((1, QL), lambda i: (0, 0)), BS((1, KVL), lambda i: (0, 0))),
        compiler_params=_cp(("arbitrary",)),
    )(zm, qg.reshape(1, QL), kvg.reshape(1, KVL), *tabs, dq, dk, dp)


def _attn_fwd(q, kv, kpe, tabs, L, tq, nsub, name):
    T = kv.shape[0]
    H = kv.shape[1] // QHEAD
    scale = (HEAD + ROPE) ** -0.5

    def body(q_ref, kv_ref, kpe_ref, cos_ref, sa_ref, sb_ref, o_ref, lse_ref, kcat):
        @pl.when(pl.program_id(1) == 0)
        def _():
            kcat[:, :HEAD] = kv_ref[:, :HEAD]
            kcat[:, HEAD:] = kpe_ref[...]

        sub = tq // nsub
        for r in range(nsub):
            rs = slice(r * sub, (r + 1) * sub)
            qp = _rope(q_ref[rs, HEAD:].astype(F32), cos_ref[rs, :], sa_ref[rs, :], sb_ref[rs, :]).astype(BF16)
            qc = jnp.concatenate([q_ref[rs, :HEAD], qp], axis=1)
            s = lax.dot_general(qc, kcat[...], _DIMS["nt"], preferred_element_type=F32)
            m = jnp.max(s, axis=-1, keepdims=True)
            p = jnp.exp2((s - m) * (scale * math.log2(math.e)))
            l = jnp.sum(p, axis=-1, keepdims=True)
            o = jnp.dot(p.astype(BF16), kv_ref[:, HEAD:], preferred_element_type=F32)
            o_ref[rs, :] = (o / l).astype(BF16)
            lse_ref[0, rs, :] = m * scale + jnp.log(l)

    tab = BS((tq, LANES), lambda h, i: (i, 0))
    return _call(
        body, name=name, out_shape=(jax.ShapeDtypeStruct((L, H * HEAD), BF16), jax.ShapeDtypeStruct((H, L, 1), F32)),
        grid=(H, L // tq),
        in_specs=[BS((tq, QHEAD), lambda h, i: (i, h)), BS((T, QHEAD), lambda h, i: (0, h)), BS((T, LANES), lambda h, i: (0, 0)),
                  tab, tab, tab],
        out_specs=(BS((tq, HEAD), lambda h, i: (i, h)), BS((1, tq, 1), lambda h, i: (h, i, 0))),
        scratch=[pltpu.VMEM((T, QHEAD), BF16)], sem=("parallel", "arbitrary"), operands=(q, kv, kpe, *tabs))


def _attn_bwd(q, kv, kpe, tabs, o, lse, do, L, tq, name):
    T = kv.shape[0]
    H = kv.shape[1] // QHEAD
    scale = (HEAD + ROPE) ** -0.5
    nq = L // tq

    def body(q_ref, kv_ref, kpe_ref, cos_ref, sa_ref, sb_ref, o_ref, lse_ref, do_ref, dq_ref, dkv_ref, dkpe_ref, kcat, dk_acc, dv_acc,
             qc_scr, ds_scr, p_scr):
        h, i = pl.program_id(0), pl.program_id(1)

        @pl.when(i == 0)
        def _():
            kcat[:, :HEAD] = kv_ref[:, :HEAD]
            kcat[:, HEAD:] = kpe_ref[...]
            dk_acc[...] = jnp.zeros_like(dk_acc)
            dv_acc[...] = jnp.zeros_like(dv_acc)

        sub = tq // 2
        log2e = math.log2(math.e)
        for r in range(2):
            rs = slice(r * sub, (r + 1) * sub)
            cos, sa, sb = cos_ref[rs, :], sa_ref[rs, :], sb_ref[rs, :]
            qp = _rope(q_ref[rs, HEAD:].astype(F32), cos, sa, sb).astype(BF16)
            qc_scr[rs, :] = jnp.concatenate([q_ref[rs, :HEAD], qp], axis=1)
            s = lax.dot_general(qc_scr[rs, :], kcat[...], _DIMS["nt"], preferred_element_type=F32)
            p = jnp.exp2(s * (scale * log2e) - lse_ref[0, rs, :] * log2e)
            dov = do_ref[rs, :]
            delta = jnp.sum(dov.astype(F32) * o_ref[rs, :].astype(F32), axis=-1, keepdims=True)
            dp = lax.dot_general(dov, kv_ref[:, HEAD:], _DIMS["nt"], preferred_element_type=F32)
            ds_scr[rs, :] = (p * (dp - delta) * scale).astype(BF16)
            p_scr[rs, :] = p.astype(BF16)
            dqc = jnp.dot(ds_scr[rs, :], kcat[...], preferred_element_type=F32)
            dq_ref[rs, :HEAD] = dqc[:, :HEAD].astype(BF16)
            dq_ref[rs, HEAD:] = _rope_t(dqc[:, HEAD:], cos, sa, sb).astype(BF16)
        dk_acc[...] += lax.dot_general(ds_scr[...], qc_scr[...], _DIMS["tn"], preferred_element_type=F32)
        dv_acc[...] += lax.dot_general(p_scr[...], do_ref[...], _DIMS["tn"], preferred_element_type=F32)

        @pl.when(i == nq - 1)
        def _():
            dkv_ref[:, :HEAD] = dk_acc[:, :HEAD].astype(BF16)
            dkv_ref[:, HEAD:] = dv_acc[...].astype(BF16)

            @pl.when(h == 0)
            def _():
                dkpe_ref[...] = dk_acc[:, HEAD:]

            @pl.when(h > 0)
            def _():
                dkpe_ref[...] += dk_acc[:, HEAD:]

    tab = BS((tq, LANES), lambda h, i: (i, 0))
    return _call(
        body, name=name,
        out_shape=(jax.ShapeDtypeStruct((L, H * QHEAD), BF16), jax.ShapeDtypeStruct((T, H * QHEAD), BF16), jax.ShapeDtypeStruct((T, LANES), F32)),
        grid=(H, nq),
        in_specs=[BS((tq, QHEAD), lambda h, i: (i, h)), BS((T, QHEAD), lambda h, i: (0, h)), BS((T, LANES), lambda h, i: (0, 0)),
                  tab, tab, tab, BS((tq, HEAD), lambda h, i: (i, h)), BS((1, tq, 1), lambda h, i: (h, i, 0)),
                  BS((tq, HEAD), lambda h, i: (i, h))],
        out_specs=(BS((tq, QHEAD), lambda h, i: (i, h)), BS((T, QHEAD), lambda h, i: (0, h)), BS((T, LANES), lambda h, i: (0, 0))),
        scratch=[pltpu.VMEM((T, QHEAD), BF16), pltpu.VMEM((T, QHEAD), F32), pltpu.VMEM((T, HEAD), F32),
                 pltpu.VMEM((tq, QHEAD), BF16), pltpu.VMEM((tq, T), BF16), pltpu.VMEM((tq, T), BF16)],
        sem=("arbitrary", "arbitrary"), operands=(q, kv, kpe, *tabs, o, lse, do))


def _adamw(w, g, m, v, name):
    R, C = w.shape
    tr = _row_tile(R, C)
    c1 = 1.0 / (1.0 - ADAM_B1 ** ADAM_STEP)
    c2 = 1.0 / (1.0 - ADAM_B2 ** ADAM_STEP)

    def body(w_ref, g_ref, m_ref, v_ref, d_ref, nm_ref, nv_ref):
        g_ = g_ref[...]
        nm = ADAM_B1 * m_ref[...] + (1.0 - ADAM_B1) * g_
        nv = ADAM_B2 * v_ref[...] + (1.0 - ADAM_B2) * (g_ * g_)
        nm_ref[...] = nm
        nv_ref[...] = nv
        d_ref[...] = -ADAM_LR * ((nm * c1) / (jnp.sqrt(nv * c2) + ADAM_EPS) + ADAM_WD * w_ref[...])

    blk = BS((tr, C), lambda i: (i, 0))
    sd = jax.ShapeDtypeStruct((R, C), F32)
    return _pcs(body, name=name, out_shape=(sd, sd, sd), grid=(R // tr,), in_specs=[blk] * 4, out_specs=(blk,) * 3,
                compiler_params=_cp(("parallel",)))(w, g, m, v)


def _adamw_layers(w, gs, m, v, name):
    n, R, C = w.shape
    tr = _row_tile(R, C)
    nr = R // tr
    c1 = 1.0 / (1.0 - ADAM_B1 ** ADAM_STEP)
    c2 = 1.0 / (1.0 - ADAM_B2 ** ADAM_STEP)

    def body(w_ref, m_ref, v_ref, *rest):
        g_refs, (go_ref, d_ref, nm_ref, nv_ref) = rest[:n], rest[n:]
        layer = pl.program_id(0)
        g_ = g_refs[0][...]
        for j in range(1, n):
            g_ = jnp.where(layer == j, g_refs[j][...], g_)
        nm = ADAM_B1 * m_ref[...] + (1.0 - ADAM_B1) * g_
        nv = ADAM_B2 * v_ref[...] + (1.0 - ADAM_B2) * (g_ * g_)
        go_ref[...] = g_
        nm_ref[...] = nm
        nv_ref[...] = nv
        d_ref[...] = -ADAM_LR * ((nm * c1) / (jnp.sqrt(nv * c2) + ADAM_EPS) + ADAM_WD * w_ref[...])

    def g_spec(j):
        return BS((tr, C), lambda layer, i: (jnp.where(layer == j, i, jnp.where(layer < j, 0, nr - 1)), 0))

    blk = BS((None, tr, C), lambda layer, i: (layer, i, 0))
    sd = jax.ShapeDtypeStruct((n, R, C), F32)
    return _pcs(body, name=name, out_shape=(sd, sd, sd, sd), grid=(n, nr), in_specs=[blk] * 3 + [g_spec(j) for j in range(n)],
                out_specs=(blk,) * 4, compiler_params=_cp(("arbitrary", "arbitrary")))(w, m, v, *gs)


def _sum_lead(a, out_dtype, name):
    n, R, C = a.shape
    tr = _row_tile(R, C * n, 2 << 20)

    def body(a_ref, o_ref):
        acc = a_ref[0].astype(F32)
        for k in range(1, n):
            acc = acc + a_ref[k].astype(F32)
        o_ref[...] = acc.astype(out_dtype)

    return pl.pallas_call(body, name=name, out_shape=jax.ShapeDtypeStruct((R, C), out_dtype), grid=(R // tr,),
                          in_specs=[BS((n, tr, C), lambda i: (0, i, 0))], out_specs=BS((tr, C), lambda i: (i, 0)),
                          compiler_params=_cp(("parallel",)))(a)


def _cctx_grad(parts, c_ctx, name):
    n, D = parts.shape

    def body(p_ref, c_ref, o_ref):
        d = jnp.sum(p_ref[...], axis=0, keepdims=True)
        c = c_ref[...]
        s = _sigmoid(c)
        o_ref[...] = d * s * (1.0 + c * (1.0 - s))

    return pl.pallas_call(body, name=name, out_shape=jax.ShapeDtypeStruct((1, D), F32))(parts, c_ctx.reshape(1, D))


def _me():
    return lax.axis_index("x"), lax.axis_index("y"), lax.axis_index("c")


def _aligned(v, n):
    return v if isinstance(v, int) else pl.multiple_of(v, n)


def _window(ref, r0, c0, R, C):
    rows = pl.ds(_aligned(r0, SUBLANES), R)
    if C == ref.shape[1]:
        return ref.at[rows, :]
    return ref.at[rows, pl.ds(_aligned(c0, LANES), C)]


def _allgather8(items, name):
    n = len(items)

    def body(*refs):
        srcs, dsts = refs[:n], refs[n:2 * n]
        send_sems, recv_sems, local_sems = refs[2 * n:]
        x, y, c = _me()
        me, sibling = (x, y, c), (x, y, 1 - c)
        chips = [(1 - x, y), (x, 1 - y), (1 - x, 1 - y)]

        def dwin(a, blk):
            (R, C), at = items[a][2], items[a][4]
            return _window(dsts[a], *at(*blk), R, C)

        def swin(a):
            (R, C), at = items[a][2], items[a][1]
            return _window(srcs[a], *at(*me), R, C)

        def copy(a, k, blk, to, src=None):
            return pltpu.make_async_remote_copy(
                src_ref=dwin(a, blk) if src is None else src, dst_ref=dwin(a, blk), send_sem=send_sems.at[7 * a + k],
                recv_sem=recv_sems.at[7 * a + k], device_id=to, device_id_type=MESH)

        mine = [pltpu.make_async_copy(swin(a), dwin(a, me), local_sems.at[a]) for a in range(n)]
        for cp in mine:
            cp.start()
        first = []
        for a in range(n):
            first.append(copy(a, 0, me, sibling, src=swin(a)))
            first += [copy(a, 1 + j, me, (*chip, c), src=swin(a)) for j, chip in enumerate(chips)]
        for cp in first:
            cp.start()
        passed = []
        for j, chip in enumerate(chips):
            for a in range(n):
                copy(a, 1 + j, (*chip, c), me).wait_recv()
                fwd = copy(a, 4 + j, (*chip, c), sibling)
                fwd.start()
                passed.append(fwd)
        for a in range(n):
            copy(a, 0, sibling, me).wait_recv()
            for j, chip in enumerate(chips):
                copy(a, 4 + j, (*chip, 1 - c), me).wait_recv()
        for cp in first + passed:
            cp.wait_send()
        for cp in mine:
            cp.wait()

    outs = pl.pallas_call(
        body, name=name, out_shape=tuple(jax.ShapeDtypeStruct(it[3], it[0].dtype) for it in items),
        in_specs=[ANY] * n, out_specs=tuple([ANY] * n),
        scratch_shapes=[pltpu.SemaphoreType.DMA((7 * n,)), pltpu.SemaphoreType.DMA((7 * n,)), pltpu.SemaphoreType.DMA((n,))],
    )(*[it[0] for it in items])
    return list(outs)


def _cast_place(w, kind, name, lead=None):
    R2, C = w.shape[-2:]
    tr = _row_tile(R2, C)
    nr = R2 // tr
    xi, yi, _ = _me()
    p_arr = (2 * xi + yi).astype(jnp.int32).reshape(1)
    if kind == "col":
        shape, o_spec = (R2, 4 * C), BS((tr, C), lambda i, p: (i, p[0]))
    else:
        shape, o_spec = (4 * R2, C), BS((tr, C), lambda i, p: (p[0] * nr + i, 0))

    def body(p_ref, w_ref, o_ref):
        o_ref[...] = w_ref[...].astype(BF16)

    return pl.pallas_call(
        body, name=name, out_shape=jax.ShapeDtypeStruct(shape, BF16),
        grid_spec=pltpu.PrefetchScalarGridSpec(
            num_scalar_prefetch=1, grid=(nr,), out_specs=o_spec,
            in_specs=[BS((tr, C), lambda i, p: (i, 0)) if lead is None else BS((None, tr, C), lambda i, p: (lead, i, 0))]),
        compiler_params=_cp(("parallel",)),
    )(p_arr, w)


def _remote(src, dst, sends, recvs, k, to):
    return pltpu.make_async_remote_copy(src_ref=src, dst_ref=dst, send_sem=sends.at[k], recv_sem=recvs.at[k], device_id=to,
                                        device_id_type=MESH)


def _gather_side(bufs, pieces, passed=(), pass_now=False):
    later = list(passed) + (list(pieces) if pass_now else [])
    base = 3 * len(pieces)

    def win(b, piece, p, pc):
        bi, (R, C), at, k, i = piece
        r0, c0 = at(p, pc)
        return _window(b[bi], r0 + i * (R // k), c0, R // k, C)

    def chips_of(x, y):
        return [(1 - x, y), (x, 1 - y), (1 - x, 1 - y)]

    def over_ici(b, sends, recvs, landing):
        x, y, c = _me()
        cps = []
        for m, piece in enumerate(pieces):
            for j, (px, py) in enumerate(chips_of(x, y)):
                w = win(b, piece, 2 * px + py, c) if landing else win(b, piece, 2 * x + y, c)
                cps.append(_remote(w, w, sends, recvs, 3 * m + j, (px, py, c)))
        return cps

    def to_sibling(b, sends, recvs, todo, landing):
        x, y, c = _me()
        cps = []
        for m, piece in enumerate(later):
            if piece in todo:
                for j, (px, py) in enumerate(chips_of(x, y)):
                    w = win(b, piece, 2 * px + py, 1 - c if landing else c)
                    cps.append(_remote(w, w, sends, recvs, base + 3 * m + j, (x, y, 1 - c)))
        return cps

    def start(b, sends, recvs):
        for cp in over_ici(b, sends, recvs, False) + to_sibling(b, sends, recvs, passed, False):
            cp.start()

    def finish(b, sends, recvs):
        for cp in over_ici(b, sends, recvs, True):
            cp.wait_recv()
        if pass_now:
            for cp in to_sibling(b, sends, recvs, pieces, False):
                cp.start()
        for cp in to_sibling(b, sends, recvs, later, True):
            cp.wait_recv()
        for cp in over_ici(b, sends, recvs, False) + to_sibling(b, sends, recvs, later, False):
            cp.wait_send()

    return _Side(bufs, 3 * (len(pieces) + len(later)), start, finish)


_RELS = [(dx, dy, dc) for dx in (0, 1) for dy in (0, 1) for dc in (0, 1)][1:]


def _rs_side(bufs, pieces):
    def flip(v, d):
        return 1 - v if d else v

    def copies(b, sends, recvs, landing):
        x, y, c = _me()
        dev = 4 * x + 2 * y + c
        cps = []
        for m, (gi, ri, (R, C), at, k, i) in enumerate(pieces):
            rows = R // k
            for t, (dx, dy, dc) in enumerate(_RELS):
                tx, ty, tc = flip(x, dx), flip(y, dy), flip(c, dc)
                if landing:
                    theirs = b[ri].at[4 * tx + 2 * ty + tc, pl.ds(i * rows, rows), :]
                    cps.append(_remote(theirs, theirs, sends, recvs, 7 * m + t, (tx, ty, tc)))
                else:
                    r0, c0 = at(2 * tx + ty, tc)
                    src = _window(b[gi], r0 + i * rows, c0, rows, C)
                    cps.append(_remote(src, b[ri].at[dev, pl.ds(i * rows, rows), :], sends, recvs, 7 * m + t, (tx, ty, tc)))
        return cps

    def start(b, sends, recvs):
        for cp in copies(b, sends, recvs, False):
            cp.start()

    def finish(b, sends, recvs):
        for cp in copies(b, sends, recvs, True):
            cp.wait_recv()
        for cp in copies(b, sends, recvs, False):
            cp.wait_send()

    return _Side(bufs, 7 * len(pieces), start, finish)


def _comm_only(side, name):
    n = len(side.bufs)

    def body(*refs):
        bufs, (sends, recvs) = refs[n:2 * n], refs[2 * n:]
        side.start(bufs, sends, recvs)
        side.finish(bufs, sends, recvs)

    outs = pl.pallas_call(
        body, name=name, out_shape=tuple(jax.ShapeDtypeStruct(b.shape, b.dtype) for b in side.bufs),
        in_specs=[ANY] * n, out_specs=tuple([ANY] * n), input_output_aliases={a: a for a in range(n)},
        scratch_shapes=[pltpu.SemaphoreType.DMA((side.nsem,)), pltpu.SemaphoreType.DMA((side.nsem,))],
    )(*side.bufs)
    return list(outs)


def _rs_sum8(grad, recv, win, kind, name):
    R, C = win
    tr = _row_tile(R, C)
    nr = R // tr
    xi, yi, ci = _me()
    s_arr = jnp.stack([4 * xi + 2 * yi + ci, ci]).astype(jnp.int32)

    if kind == "col":
        g_spec = BS((tr, C), lambda i, s: (s[1] * nr + i, s[0] // 2))
    else:
        g_spec = BS((tr, C), lambda i, s: (s[0] * nr + i, 0))

    def body(s_ref, g_ref, *refs):
        acc = g_ref[...].astype(F32)
        for r_ref in refs[:7]:
            acc = acc + r_ref[0].astype(F32)
        refs[7][...] = acc

    def other(t):
        return BS((1, tr, C), lambda i, s: (jnp.bitwise_xor(s[0], t), i, 0))

    return pl.pallas_call(
        body, name=name, out_shape=jax.ShapeDtypeStruct((2 * R, C), F32),
        grid_spec=pltpu.PrefetchScalarGridSpec(
            num_scalar_prefetch=1, grid=(nr,), in_specs=[g_spec] + [other(t) for t in range(1, 8)],
            out_specs=BS((tr, C), lambda i, s: (s[1] * nr + i, 0))),
        compiler_params=_cp(("parallel",)),
    )(s_arr, grad, *([recv] * 7))


def _share_side(shards):
    def half(b, a, pc):
        R = b[a].shape[0] // 2
        return b[a].at[pl.ds(pl.multiple_of(pc * R, SUBLANES), R), :]

    def start(b, sends, recvs):
        x, y, c = _me()
        for a in range(len(shards)):
            _remote(half(b, a, c), half(b, a, c), sends, recvs, a, (x, y, 1 - c)).start()

    def finish(b, sends, recvs):
        x, y, c = _me()
        for a in range(len(shards)):
            _remote(half(b, a, 1 - c), half(b, a, 1 - c), sends, recvs, a, (x, y, 1 - c)).wait_recv()
        for a in range(len(shards)):
            _remote(half(b, a, c), half(b, a, c), sends, recvs, a, (x, y, 1 - c)).wait_send()

    return _Side(shards, len(shards), start, finish)


BLOB_ALIGN = SUBLANES * LANES


def _pack(arrs):
    flat = jnp.concatenate([a.reshape(-1).astype(F32) for a in arrs])
    n = flat.shape[0]
    padded = -(-n // BLOB_ALIGN) * BLOB_ALIGN
    return jnp.pad(flat, (0, padded - n)).reshape(padded // LANES, LANES)


def _unpack(flat, shapes):
    out, off = [], 0
    for s in shapes:
        n = math.prod(s)
        out.append(flat[..., off:off + n].reshape(flat.shape[:-1] + tuple(s)))
        off += n
    return out


def _gather_blob(blob, name):
    r = blob.shape[0]
    at = lambda px, py, pc: ((4 * px + 2 * py + pc) * r, 0)
    (out,) = _allgather8([(blob, lambda px, py, pc: (0, 0), (r, LANES), (8 * r, LANES), at)], name)
    return out.reshape(8, r * LANES)


def _conv_ffn_fwd(mm, full, layer, X, mods, n2g, conv_w, conv_b, L, tb, nlat, ch, tag):
    sh2, sc2, g2 = mods[3], mods[4], mods[5]
    h2 = _norm_mod_fwd(X, n2g, sh2, sc2, tb, nlat, tag + "_norm2")
    zf = mm(h2, full[f"ffn_w_up{layer}"], "nn", BF16, tag + "_up", tm=(544, 512), tn=(2816, 512))
    f = _ffn_act_fwd(zf, conv_w, conv_b, L, ch, tag + "_act")
    Xn, yf = mm(f, full[f"ffn_w_down{layer}"], "nn", F32, tag + "_down", tm=(544, 512), tn=(512,), res=(X, g2, L))
    return Xn, (X, h2, zf, f, yf)


def _conv_ffn_bwd(mm, full, gbuf, layer, dXn, dy, dg2, before, saved, mods, n2g, conv_w, conv_b, L, tb, nlat, ch, tag):
    X, h2, zf, f, _ = saved
    sc2 = mods[4]
    w_up, w_down = full[f"ffn_w_up{layer}"], full[f"ffn_w_down{layer}"]
    df = mm(dy, w_down, "nt", BF16, tag + "_ddown_x", tm=(544, 512), tn=(2816, 512))
    gbuf[f"ffn_w_down{layer}"] = mm(f, dy, "tn", BF16, tag + "_ddown_w", tm=(512,), tn=(1024, 512))
    dzf, dcw, dcb = _ffn_act_bwd(zf, df, conv_w, conv_b, L, ch, tag + "_dact")
    gbuf[f"ffn_w_up{layer}"] = mm(h2, dzf, "tn", BF16, tag + "_dup_w", tm=(1024, 512), tn=(512,), outer="i", halves=True)
    dh2 = mm(dzf, w_up, "nt", BF16, tag + "_dup_x", tm=(1088, 1024, 512), tn=(1024,), tk=(2816, 512), halves=True)
    dX, dn2g, dsh2, dsc2, dy_before, dg_before = _norm_mod_bwd(X, n2g, sc2, dh2, dXn, tb, nlat, tag + "_dnorm2", before)
    return dX, dict(n2g=dn2g, sh2=dsh2, sc2=dsc2, g2=dg2, cw=dcw, cb=dcb), (dy_before, dg_before)


def kernel(x, c, ctx, c_ctx, norm1_g, norm2_g, w_ada, b_ada, ab_w_in, ab_b_in, a_ln_g, a_ln_b, a_w_s, a_b_s, b_conv_w, b_conv_b, b_ln_g, b_ln_b, ab_w_out, mla_w_in, mla_q_norm_g, mla_w_uq, mla_kv_norm_g, mla_w_ukv, mla_w_o, ffn_w_up, ffn_conv_w, ffn_conv_b, ffn_w_down, final_norm_g, loss_target, m_c_ctx, m_norm1_g, m_norm2_g, m_w_ada, m_b_ada, m_ab_w_in, m_ab_b_in, m_a_ln_g, m_a_ln_b, m_a_w_s, m_a_b_s, m_b_conv_w, m_b_conv_b, m_b_ln_g, m_b_ln_b, m_ab_w_out, m_mla_w_in, m_mla_q_norm_g, m_mla_w_uq, m_mla_kv_norm_g, m_mla_w_ukv, m_mla_w_o, m_ffn_w_up, m_ffn_conv_w, m_ffn_conv_b, m_ffn_w_down, m_final_norm_g, v_c_ctx, v_norm1_g, v_norm2_g, v_w_ada, v_b_ada, v_ab_w_in, v_ab_b_in, v_a_ln_g, v_a_ln_b, v_a_w_s, v_a_b_s, v_b_conv_w, v_b_conv_b, v_b_ln_g, v_b_ln_b, v_ab_w_out, v_mla_w_in, v_mla_q_norm_g, v_mla_w_uq, v_mla_kv_norm_g, v_mla_w_ukv, v_mla_w_o, v_ffn_w_up, v_ffn_conv_w, v_ffn_conv_b, v_ffn_w_down, v_final_norm_g):
    W = dict(c_ctx=c_ctx, norm1_g=norm1_g, norm2_g=norm2_g, w_ada=w_ada, b_ada=b_ada, ab_w_in=ab_w_in, ab_b_in=ab_b_in, a_ln_g=a_ln_g,
             a_ln_b=a_ln_b, a_w_s=a_w_s, a_b_s=a_b_s, b_conv_w=b_conv_w, b_conv_b=b_conv_b, b_ln_g=b_ln_g, b_ln_b=b_ln_b,
             ab_w_out=ab_w_out, mla_w_in=mla_w_in, mla_q_norm_g=mla_q_norm_g, mla_w_uq=mla_w_uq, mla_kv_norm_g=mla_kv_norm_g,
             mla_w_ukv=mla_w_ukv, mla_w_o=mla_w_o, ffn_w_up=ffn_w_up, ffn_conv_w=ffn_conv_w, ffn_conv_b=ffn_conv_b,
             ffn_w_down=ffn_w_down, final_norm_g=final_norm_g)
    MOM = dict(c_ctx=m_c_ctx, norm1_g=m_norm1_g, norm2_g=m_norm2_g, w_ada=m_w_ada, b_ada=m_b_ada, ab_w_in=m_ab_w_in, ab_b_in=m_ab_b_in,
               a_ln_g=m_a_ln_g, a_ln_b=m_a_ln_b, a_w_s=m_a_w_s, a_b_s=m_a_b_s, b_conv_w=m_b_conv_w, b_conv_b=m_b_conv_b,
               b_ln_g=m_b_ln_g, b_ln_b=m_b_ln_b, ab_w_out=m_ab_w_out, mla_w_in=m_mla_w_in, mla_q_norm_g=m_mla_q_norm_g,
               mla_w_uq=m_mla_w_uq, mla_kv_norm_g=m_mla_kv_norm_g, mla_w_ukv=m_mla_w_ukv, mla_w_o=m_mla_w_o, ffn_w_up=m_ffn_w_up,
               ffn_conv_w=m_ffn_conv_w, ffn_conv_b=m_ffn_conv_b, ffn_w_down=m_ffn_w_down, final_norm_g=m_final_norm_g)
    VAR = dict(c_ctx=v_c_ctx, norm1_g=v_norm1_g, norm2_g=v_norm2_g, w_ada=v_w_ada, b_ada=v_b_ada, ab_w_in=v_ab_w_in, ab_b_in=v_ab_b_in,
               a_ln_g=v_a_ln_g, a_ln_b=v_a_ln_b, a_w_s=v_a_w_s, a_b_s=v_a_b_s, b_conv_w=v_b_conv_w, b_conv_b=v_b_conv_b,
               b_ln_g=v_b_ln_g, b_ln_b=v_b_ln_b, ab_w_out=v_ab_w_out, mla_w_in=v_mla_w_in, mla_q_norm_g=v_mla_q_norm_g,
               mla_w_uq=v_mla_w_uq, mla_kv_norm_g=v_mla_kv_norm_g, mla_w_ukv=v_mla_w_ukv, mla_w_o=v_mla_w_o, ffn_w_up=v_ffn_w_up,
               ffn_conv_w=v_ffn_conv_w, ffn_conv_b=v_ffn_conv_b, ffn_w_down=v_ffn_w_down, final_norm_g=v_final_norm_g)
    ORDER = list(W.keys())

    L, D = x.shape[1], x.shape[2]
    CT = ctx.shape[1]
    T = L + CT
    AW, BW = a_ln_g.shape[-1], b_ln_g.shape[-1]
    AH = a_w_s.shape[1]
    QL, KVL = 4 * mla_q_norm_g.shape[-1], 4 * mla_kv_norm_g.shape[-1]
    H = 4 * mla_w_o.shape[1] // HEAD
    HS = H // 4
    DFF = ffn_conv_b.shape[-1]
    NA = w_ada.shape[-1]
    tb = 256 if (L % 256 == 0 and CT % 256 == 0) else 128
    nlat = L // tb
    ch = tb
    tq = 256 if L >= 512 else 128
    xi, yi, ci = _me()
    p_me = 2 * xi + yi
    dev = 4 * xi + 2 * yi + ci

    shard_small = [c[0], mla_q_norm_g[0], mla_kv_norm_g[0], b_conv_w[0], ffn_conv_w]
    g0 = _gather_blob(_pack(shard_small), "gather_small")
    c_all, qg_s, kvg_s, bcw_s, fcw_s = _unpack(g0, [a.shape for a in shard_small])
    per_chip = lambda a: a[0::2]
    qg = per_chip(qg_s).reshape(QL)
    kvg = per_chip(kvg_s).reshape(KVL)
    bcw = jnp.concatenate(list(per_chip(bcw_s)), axis=-1)
    fcw = jnp.concatenate(list(per_chip(fcw_s)), axis=-1)
    c16 = jnp.concatenate([c_all, c_ctx[None], jnp.zeros((7, D), F32)], axis=0)

    ms = []
    for i in range(2):
        bias = lax.dynamic_slice(b_ada[i], (p_me * NA,), (NA,))
        ms.append(_mm(c16, w_ada, "nn", F32, f"ada{i}", tm=(16,), tn=(512,), bias=bias, silu_a=True, b_lead=i))
    ms = jnp.concatenate(ms, axis=0)
    (mods_all,) = _allgather8(
        [(ms, lambda px, py, pc: (pc * 16, 0), (16, NA), (32, 4 * NA), lambda px, py, pc: (pc * 16, (2 * px + py) * NA))], "gather_mods")
    mods_all = mods_all.reshape(2, 16, N_MOD, D)
    mods = []
    for i in range(2):
        lat = lax.dynamic_index_in_dim(mods_all[i], dev, axis=0, keepdims=False)
        both = jnp.stack([lat, mods_all[i, 8]], axis=0)
        mods.append([both[:, k][:, None, :] for k in range(N_MOD)])

    def pad_uq(w):
        w = w.reshape(w.shape[0], HS, HEAD + ROPE)
        return jnp.pad(w, ((0, 0), (0, 0), (0, QHEAD - HEAD - ROPE))).reshape(w.shape[0], HS * QHEAD)

    MI = QL + KVL + LANES
    big = [
        ("ab_w_in", ab_w_in, 0, "col"), ("ab_w_out", ab_w_out, 0, "row"),
        ("mla_w_in", jnp.pad(mla_w_in[0], ((0, 0), (0, MI - mla_w_in.shape[-1]))), None, "row"),
        ("mla_w_uq", pad_uq(mla_w_uq[0]), None, "col"), ("mla_w_ukv", mla_w_ukv, 0, "col"), ("mla_w_o", mla_w_o, 0, "row"),
        ("ffn_w_up0", ffn_w_up, 0, "col"), ("ffn_w_up1", ffn_w_up, 1, "col"),
        ("ffn_w_down0", ffn_w_down, 0, "row"), ("ffn_w_down1", ffn_w_down, 1, "row"),
    ]
    full, wins = {}, {}
    for nm, w, lead, kind in big:
        R, C = w.shape[-2] // 2, w.shape[-1]
        if kind == "col":
            wins[nm] = ((R, C), lambda p, pc, R=R, C=C: (pc * R, p * C), kind)
        else:
            wins[nm] = ((R, C), lambda p, pc, R=R, C=C: ((2 * p + pc) * R, 0), kind)
        full[nm] = _cast_place(w, kind, "cast_" + nm, lead)
    names = [b[0] for b in big]

    up0 = lambda *parts: [("ffn_w_up0", 8, i) for i in parts]
    mla_a = [("ffn_w_down0", 1, 0), ("mla_w_in", 1, 0), ("mla_w_uq", 1, 0)]
    mla_b = [("mla_w_ukv", 1, 0), ("mla_w_o", 1, 0)]
    ffn1_w = [("ffn_w_up1", 1, 0), ("ffn_w_down1", 1, 0)]
    gather_plan = {
        "l0_norm1": ([("ab_w_out", 1, 0)], []), "l0_in": (up0(0, 1), [("ab_w_out", 1, 0)]), "l0_gmlp": (up0(2), up0(0, 1)),
        "l0_conv": (up0(3, 4), up0(2)), "l0_lnsilu": (up0(5), up0(3, 4)), "l0_out": (up0(6, 7), up0(5)),
        "l0_ffn_norm2": ([], up0(6, 7)), "l0_ffn_up": (mla_a, []), "l0_ffn_act": ([], mla_a), "l0_ffn_down": (mla_b, []),
        "l1_norm1": ([], mla_b), "l1_attn": (ffn1_w, []), "l1_o": ([], ffn1_w),
    }
    u0 = lambda *parts: [("ffn_w_up0", 16, i) for i in parts]
    reduce_plan = {
        "l1_ffn_dact": [("ffn_w_down1", 2, 0)], "l1_ffn_dup_w": [("ffn_w_down1", 2, 1)], "l1_ffn_dup_x": [("ffn_w_up1", 4, 0)],
        "l1_dattn": [("ffn_w_up1", 4, 1), ("ffn_w_up1", 4, 2), ("ffn_w_up1", 4, 3), ("mla_w_o", 1, 0)],
        "l0_ffn_ddown_x": [("mla_w_ukv", 1, 0), ("mla_w_uq", 1, 0)], "l0_ffn_ddown_w": [("mla_w_in", 1, 0)],
        "l0_ffn_dact": [("ffn_w_down0", 2, 0)], "l0_ffn_dup_w": [("ffn_w_down0", 2, 1)],
        "l0_ffn_dup_x": u0(0, 1, 2, 3, 4, 5, 6), "l0_ffn_dnorm2": u0(7, 8), "l0_dout_w": u0(9),
        "l0_dout_x": [("ab_w_out", 2, 0)], "l0_dgmlp": u0(10, 11), "l0_dlnsilu": u0(12),
        "l0_dconv": u0(13, 14) + [("ab_w_out", 2, 1)], "l0_din_w": u0(15),
        "l0_din_x": [("ab_w_in", 4, 0), ("ab_w_in", 4, 1), ("ab_w_in", 4, 2)], "l0_dnorm1": [("ab_w_in", 4, 3)],
    }
    gbuf, rbuf = {}, {}

    def gather_side(req, passed=(), pass_now=False):
        nms = list(dict.fromkeys(nm for nm, _, _ in list(req) + list(passed)))
        piece = lambda nm, k, i: (nms.index(nm), wins[nm][0], wins[nm][1], k, i)
        side = _gather_side([full[nm] for nm in nms], [piece(*r) for r in req], [piece(*r) for r in passed], pass_now)

        def commit(bufs):
            full.update(zip(nms, bufs))

        return side, commit

    def reduce_side(req):
        nms = list(dict.fromkeys(nm for nm, _, _ in req))
        for nm in nms:
            if nm not in rbuf:
                rbuf[nm] = lax.empty((8,) + wins[nm][0], BF16)
        n = len(nms)
        side = _rs_side([gbuf[nm] for nm in nms] + [rbuf[nm] for nm in nms],
                        [(nms.index(nm), n + nms.index(nm), wins[nm][0], wins[nm][1], k, i) for nm, k, i in req])

        def commit(bufs):
            gbuf.update(zip(nms, bufs[:n]))
            rbuf.update(zip(nms, bufs[n:]))

        return side, commit

    _carry_plan.clear()
    _carry_plan.update({name: functools.partial(gather_side, req, passed) for name, (req, passed) in gather_plan.items()})
    _carry_plan.update({name: functools.partial(reduce_side, req) for name, req in reduce_plan.items()})
    mm, attn_fwd, attn_bwd = _mm, _attn_fwd, _attn_bwd

    side, commit = gather_side([("ab_w_in", 1, 0)], (), True)
    commit(_comm_only(side, "gather_first"))

    X0 = jnp.concatenate([x[0], ctx[0]], axis=0)
    m0, m1 = mods[0], mods[1]
    h1 = _norm_mod_fwd(X0, norm1_g[0], m0[0], m0[1], tb, nlat, "l0_norm1")
    z = mm(h1, full["ab_w_in"], "nn", BF16, "l0_in", tm=(544, 512), tn=(1024, 512), bias=ab_b_in[0])
    bs_full = jnp.broadcast_to(a_b_s[0][:, :, None], (AH, CHUNK, CHUNK))
    ya = _gmlp_fwd(z, a_ln_g[0], a_ln_b[0], a_w_s[0], bs_full, tb, "l0_gmlp")
    hc = _glu_conv_fwd(z, 2 * AW, bcw, b_conv_b[0], L, ch, "l0_conv")
    yb = _ln_silu_fwd(hc, b_ln_g[0], b_ln_b[0], tb, "l0_lnsilu")
    yab = jnp.concatenate([ya, yb], axis=1)
    X1, y0 = mm(yab, full["ab_w_out"], "nn", F32, "l0_out", tm=(544, 512), tn=(1024, 512), res=(X0, m0[2], L))
    X2, ffn0 = _conv_ffn_fwd(mm, full, 0, X1, m0, norm2_g[0], fcw[0], ffn_conv_b[0], L, tb, nlat, ch, "l0_ffn")

    tabs = _rope_tables(L, T)
    hm = _norm_mod_fwd(X2, norm1_g[1], m1[0], m1[1], tb, nlat, "l1_norm1")
    zm = mm(hm, full["mla_w_in"], "nn", F32, "l1_in", tm=(544, 512), tn=(MI,))
    cqn, ckvn, kpe = _mla_prep_fwd(zm, qg, kvg, tabs, tb, "l1_prep")
    q = mm(cqn, full["mla_w_uq"], "nn", BF16, "l1_uq", tm=(512,), tn=(1024, 512), rows=L)
    kvh = mm(ckvn, full["mla_w_ukv"], "nn", BF16, "l1_ukv", tm=(544, 512), tn=(1024, 512))
    tq2 = 2 * tq if L % (2 * tq) == 0 and L > 2 * tq else tq
    wide = L % (4 * tq) == 0 and L > 4 * tq
    o, lse = attn_fwd(q, kvh, kpe, tabs, L, 4 * tq if wide else tq2, 4 if wide else 2, "l1_attn")
    m1_lat = [a[:1] for a in m1]
    X3, yl = mm(o, full["mla_w_o"], "nn", F32, "l1_o", tm=(512,), tn=(1024, 512), res=(X2, m1_lat[2], L))
    X4, ffn1 = _conv_ffn_fwd(mm, full, 1, X3, m1_lat, norm2_g[1], fcw[1], ffn_conv_b[1], L, tb, nlat, ch, "l1_ffn")
    loss_acc, dX4, dfinal, dyf1, dg2_1 = _final_loss(X4, final_norm_g, loss_target[0], (ffn1[4], m1_lat[5]), tb, "loss")

    dX3, gf1, (dyl, dg1_1) = _conv_ffn_bwd(mm, full, gbuf, 1, dX4, dyf1, dg2_1, (yl, m1_lat[2]), ffn1, m1_lat, norm2_g[1], fcw[1],
                                           ffn_conv_b[1], L, tb, nlat, ch, "l1_ffn")
    do = mm(dyl, full["mla_w_o"], "nt", BF16, "l1_do_x", tm=(512,), tn=(1024, 512))
    gbuf["mla_w_o"] = mm(o, dyl, "tn", BF16, "l1_do_w", tm=(512,), tn=(1024, 512))
    dq, dkv, dkpe = attn_bwd(q, kvh, kpe, tabs, o, lse, do, L, tq2, "l1_dattn")
    dckvn = mm(dkv, full["mla_w_ukv"], "nt", BF16, "l1_dukv_x", tm=(544, 512), tn=(KVL,), tk=(2048, 512))
    gbuf["mla_w_ukv"] = mm(ckvn, dkv, "tn", BF16, "l1_dukv_w", tm=(512,), tn=(1024, 512))
    dcqn = mm(dq, full["mla_w_uq"], "nt", BF16, "l1_duq_x", tm=(512,), tn=(QL,), tk=(2048, 512))
    gbuf["mla_w_uq"] = mm(cqn, dq, "tn", BF16, "l1_duq_w", tm=(QL,), tn=(1024, 512), rows=L)
    dcqn = jnp.concatenate([dcqn, jnp.zeros((CT, QL), BF16)], axis=0)
    dzm, dqg, dkvg = _mla_prep_bwd(zm, qg, kvg, tabs, dcqn, dckvn, dkpe, tb, "l1_dprep")
    dhm = mm(dzm, full["mla_w_in"], "nt", BF16, "l1_din_x", tm=(544, 512), tn=(1024, 512))
    gbuf["mla_w_in"] = mm(hm, dzm, "tn", BF16, "l1_din_w", tm=(512,), tn=(MI,))
    dX2, dn1g_1, dsh1_1, dsc1_1, dyf0, dg2_0 = _norm_mod_bwd(X2, norm1_g[1], m1[1], dhm, dX3, tb, nlat, "l1_dnorm1", (ffn0[4], m0[5]))

    dX1, gf0, (dy0, dg1_0) = _conv_ffn_bwd(mm, full, gbuf, 0, dX2, dyf0, dg2_0, (y0, m0[2]), ffn0, m0, norm2_g[0], fcw[0],
                                           ffn_conv_b[0], L, tb, nlat, ch, "l0_ffn")
    gbuf["ab_w_out"] = mm(yab, dy0, "tn", BF16, "l0_dout_w", tm=(512,), tn=(1024, 512))
    dyab = mm(dy0, full["ab_w_out"], "nt", BF16, "l0_dout_x", tm=(544, 512), tn=(1024, 512))
    dzu, dzv, dlnag, dlnab, dws, dbs, csu, csv = _gmlp_bwd(z, dyab, a_ln_g[0], a_ln_b[0], a_w_s[0], bs_full, tb, "l0_dgmlp")
    dhc, dlnbg, dlnbb = _ln_silu_bwd(hc, dyab, AW, b_ln_g[0], b_ln_b[0], tb, "l0_dlnsilu")
    dza, dzg, dbcw, dbcb, csa, csg = _glu_conv_bwd(z, 2 * AW, dhc, bcw, L, ch, "l0_dconv")
    dz = jnp.concatenate([dzu, dzv, dza, dzg], axis=1)
    dbin = jnp.concatenate([csu, csv, csa, csg], axis=1)
    gbuf["ab_w_in"] = mm(h1, dz, "tn", BF16, "l0_din_w", tm=(512,), tn=(1024, 512))
    dh1 = mm(dz, full["ab_w_in"], "nt", BF16, "l0_din_x", tm=(544, 512), tn=(1024, 512), tk=(2048, 512))
    dX0, dn1g_0, dsh1_0, dsc1_0 = _norm_mod_bwd(X0, norm1_g[0], m0[1], dh1, dX1, tb, nlat, "l0_dnorm1")
    grad_x = dX0[:L][None]

    halves = [_rs_sum8(gbuf[nm], rbuf[nm], wins[nm][0], wins[nm][2], "rs_sum_" + nm) for nm in names]

    def grp6(l, sh1, sc1, g1, f):
        G = sh1.shape[0]
        pad = lambda a: jnp.concatenate([a, jnp.zeros((G - a.shape[0],) + a.shape[1:], F32)], axis=0) if a.shape[0] < G else a
        return jnp.concatenate([pad(a) for a in (sh1, sc1, g1, f["sh2"], f["sc2"], f["g2"])], axis=1)

    dm0 = grp6(0, dsh1_0, dsc1_0, dg1_0, gf0)
    dm1 = grp6(1, dsh1_1, dsc1_1, dg1_1, gf1)
    dmods = jnp.stack([dm0, dm1], axis=0)
    small = [
        jnp.concatenate([dn1g_0, dn1g_1], axis=0), jnp.concatenate([gf0["n2g"], gf1["n2g"]], axis=0), dbin, dlnag, dlnab, dws,
        dbs, dbcw, dbcb, dlnbg, dlnbb, dqg, dkvg, jnp.stack([gf0["cw"], gf1["cw"]], axis=0),
        jnp.concatenate([gf0["cb"], gf1["cb"]], axis=0), dfinal, dmods[:, 1],
    ]
    small_shapes = [a.shape for a in small]
    lat_shape = dmods[:, 0].shape
    blob = _pack(small + [dmods[:, 0], loss_acc[0, :1]])
    gathered = _gather_blob(blob, "gather_grads")
    summed = _sum_lead(gathered.reshape(8, -1, LANES), F32, "sum_grads").reshape(-1)
    (dn1g, dn2g, dbin_s, dlnag_s, dlnab_s, dws_s, dbs_s, dbcw_s, dbcb_s, dlnbg_s, dlnbb_s, dqg_s, dkvg_s, dfcw_s, dfcb_s, dfinal_s,
     dmods_ctx, dmods_lat_sum, loss) = _unpack(summed, small_shapes + [lat_shape, (1,)])
    loss = loss.reshape(())
    n_small = sum(math.prod(s) for s in small_shapes)
    dmods_lat = gathered[:, n_small:n_small + math.prod(lat_shape)].reshape((8,) + lat_shape)

    grad_w_ada, dc_parts = [], []
    for i in range(2):
        dm16 = jnp.concatenate([dmods_lat[:, i].reshape(8, N_MOD * D), dmods_ctx[i].reshape(1, N_MOD * D),
                                jnp.zeros((7, N_MOD * D), F32)], axis=0)
        dm16_s = lax.dynamic_slice(dm16, (0, p_me * NA), (16, NA))
        grad_w_ada.append(_mm(c16, dm16_s, "tn", F32, f"dada{i}_w", tm=(1024, 512), tn=(1024, 512), silu_a=True))
        dc_parts.append(_mm(dm16_s, w_ada, "nt", F32, f"dada{i}_c", tm=(16,), tn=(512,), tk=(1024, 512), b_lead=i))
    grad_b_ada = dmods_lat_sum.reshape(2, N_MOD * D) + dmods_ctx.reshape(2, N_MOD * D)
    dc_blob = _pack([dc_parts[0][8] + dc_parts[1][8]])
    dc_all = _gather_blob(dc_blob, "gather_dc")[0::2, :D]
    grad_c_ctx = _cctx_grad(dc_all, c_ctx, "dcctx").reshape(D)

    gshard = dict(zip(names, _comm_only(_share_side(halves), "rs_share")))
    two = lambda a: a.reshape(-1, a.shape[-1])
    ada_update = _adamw_layers(w_ada, grad_w_ada, m_w_ada, v_w_ada, "adamw_w_ada")

    def my_cols(a, axis, n):
        return lax.dynamic_slice_in_dim(a, p_me * n, n, axis=axis)

    unpad_uq = lambda g: g.reshape(QL, HS, QHEAD)[:, :, :HEAD + ROPE].reshape(QL, HS * (HEAD + ROPE))
    grads = dict(
        c_ctx=grad_c_ctx, norm1_g=dn1g, norm2_g=dn2g, b_ada=grad_b_ada, ab_w_in=gshard["ab_w_in"][None],
        ab_b_in=dbin_s, a_ln_g=dlnag_s, a_ln_b=dlnab_s, a_w_s=dws_s[None], a_b_s=dbs_s.reshape(1, AH, CHUNK),
        b_conv_w=my_cols(dbcw_s, 1, BW // 4)[None], b_conv_b=dbcb_s, b_ln_g=dlnbg_s, b_ln_b=dlnbb_s,
        ab_w_out=gshard["ab_w_out"][None], mla_w_in=gshard["mla_w_in"][:, :mla_w_in.shape[-1]][None],
        mla_q_norm_g=my_cols(dqg_s, 1, QL // 4), mla_w_uq=unpad_uq(gshard["mla_w_uq"])[None],
        mla_kv_norm_g=my_cols(dkvg_s, 1, KVL // 4), mla_w_ukv=gshard["mla_w_ukv"][None], mla_w_o=gshard["mla_w_o"][None],
        ffn_conv_w=my_cols(dfcw_s, 2, DFF // 4), ffn_conv_b=dfcb_s, final_norm_g=dfinal_s.reshape(D),
    )
    LAYERED = ("w_ada", "ffn_w_up", "ffn_w_down")
    grads = {k: grads[k].reshape(W[k].shape) for k in ORDER if k not in LAYERED}

    BIG = ("ab_w_in", "ab_w_out", "mla_w_in", "mla_w_uq", "mla_w_ukv", "mla_w_o") + LAYERED
    delta, new_m, new_v = {}, {}, {}
    for k in BIG:
        if k == "w_ada":
            grads[k], delta[k], new_m[k], new_v[k] = ada_update
        elif k in LAYERED:
            grads[k], delta[k], new_m[k], new_v[k] = _adamw_layers(W[k], [gshard[k + "0"], gshard[k + "1"]], MOM[k], VAR[k], "adamw_" + k)
        else:
            upd = _adamw(two(W[k]), two(grads[k]), two(MOM[k]), two(VAR[k]), "adamw_" + k)
            delta[k], new_m[k], new_v[k] = (a.reshape(W[k].shape) for a in upd)
    SMALL = [k for k in ORDER if k not in BIG]
    small_upd = _adamw(_pack([W[k] for k in SMALL]), _pack([grads[k] for k in SMALL]), _pack([MOM[k] for k in SMALL]),
                       _pack([VAR[k] for k in SMALL]), "adamw_small")
    shapes = [W[k].shape for k in SMALL]
    for k, d_k, m_k, v_k in zip(SMALL, *[_unpack(a.reshape(-1), shapes) for a in small_upd]):
        delta[k], new_m[k], new_v[k] = d_k, m_k, v_k

    return (loss, grad_x, *[grads[k] for k in ORDER], *[delta[k] for k in ORDER], *[new_m[k] for k in ORDER],
            *[new_v[k] for k in ORDER])
```

```python
import functools
import math

import jax
import jax.numpy as jnp
from jax import lax
from jax.experimental import pallas as pl
from jax.experimental.pallas import tpu as pltpu

F32 = jnp.float32
BF16 = jnp.bfloat16
MESH = pl.DeviceIdType.MESH
ANY = pl.BlockSpec(memory_space=pl.ANY)

EPS = 1e-6
N_MOD = 6
CHUNK = 128
HEAD = 128
ROPE = 64
QHEAD = 2 * HEAD
GRID_W = 64
ROPE_THETA = 10000.0
B_CONV = 31
FFN_CONV = 3
ADAM_LR, ADAM_B1, ADAM_B2, ADAM_EPS, ADAM_WD, ADAM_STEP = 0.001, 0.9, 0.999, 1e-08, 0.01, 10

V7X_VMEM_LIMIT = 56 * 1024 * 1024
LANES = 128
SUBLANES = 8
CONV_PAD = 16

BS = pl.BlockSpec


def _cp(sem=None, vmem=V7X_VMEM_LIMIT):
    return pltpu.CompilerParams(dimension_semantics=sem, vmem_limit_bytes=vmem)


class _Side:
    def __init__(self, bufs, nsem, start, finish):
        self.bufs, self.nsem, self.start, self.finish = list(bufs), nsem, start, finish


_carry_plan = {}


def _call(body, *, name, out_shape, grid, in_specs, out_specs, operands, sem, scratch=()):
    if name not in _carry_plan:
        return pl.pallas_call(body, name=name, out_shape=out_shape, grid=grid, in_specs=in_specs, out_specs=out_specs,
                              scratch_shapes=list(scratch), compiler_params=_cp(sem))(*operands)
    side, commit = _carry_plan[name]()
    multi = isinstance(out_shape, (tuple, list))
    outs = tuple(out_shape) if multi else (out_shape,)
    ospecs = tuple(out_specs) if multi else (out_specs,)
    n_in, n_out, n_scr, n_buf = len(in_specs), len(outs), len(scratch), len(side.bufs)

    def body2(*refs):
        o0 = n_in + n_buf
        s0 = o0 + n_out + n_buf
        bufs = refs[o0 + n_out:s0]
        sends, recvs = refs[s0 + n_scr:]
        first = last = None
        for d, g in enumerate(grid):
            pid = pl.program_id(d)
            first = (pid == 0) if first is None else jnp.logical_and(first, pid == 0)
            last = (pid == g - 1) if last is None else jnp.logical_and(last, pid == g - 1)

        @pl.when(first)
        def _():
            side.start(bufs, sends, recvs)

        body(*refs[:n_in], *refs[o0:o0 + n_out], *refs[s0:s0 + n_scr])

        @pl.when(last)
        def _():
            side.finish(bufs, sends, recvs)

    res = pl.pallas_call(
        body2, name=name, out_shape=outs + tuple(jax.ShapeDtypeStruct(b.shape, b.dtype) for b in side.bufs), grid=grid,
        in_specs=list(in_specs) + [ANY] * n_buf, out_specs=ospecs + (ANY,) * n_buf,
        scratch_shapes=list(scratch) + [pltpu.SemaphoreType.DMA((side.nsem,)), pltpu.SemaphoreType.DMA((side.nsem,))],
        input_output_aliases={n_in + i: n_out + i for i in range(n_buf)}, compiler_params=_cp(("arbitrary",) * len(grid)),
    )(*operands, *side.bufs)
    commit(list(res[n_out:]))
    return tuple(res[:n_out]) if multi else res[0]


def _pcs(body, *, name, out_shape, grid, in_specs, out_specs, compiler_params, scratch_shapes=()):
    def run(*operands):
        return _call(body, name=name, out_shape=out_shape, grid=grid, in_specs=in_specs, out_specs=out_specs, operands=operands,
                     sem=compiler_params.dimension_semantics, scratch=scratch_shapes)
    return run


def _pick(n, prefs):
    for p in prefs:
        if p <= n and n % p == 0:
            return p
    return n


def _row_tile(rows, cols, target_bytes=1 << 20):
    best = None
    for d in range(2 * SUBLANES, rows + 1, 2 * SUBLANES):
        if rows % d == 0 and d * cols * 4 <= target_bytes:
            best = d
    return best if best is not None else rows


def _sigmoid(x):
    return 1.0 / (1.0 + jnp.exp(-x))


def _gelu(x):
    c = math.sqrt(2.0 / math.pi)
    th = jnp.tanh(c * (x + 0.044715 * x * x * x))
    return 0.5 * x * (1.0 + th), th


def _gelu_grad(x, th):
    c = math.sqrt(2.0 / math.pi)
    return 0.5 * (1.0 + th) + 0.5 * x * (1.0 - th * th) * c * (1.0 + 3.0 * 0.044715 * x * x)


_DIMS = {"nn": (((1,), (0,)), ((), ())), "nt": (((1,), (1,)), ((), ())), "tn": (((0,), (0,)), ((), ()))}


def _mm(a, b, mode, out_dtype, name, tm=(512,), tn=(512,), tk=(100000,), bias=None, silu_a=False, rows=None, outer="j", b_lead=None,
        halves=False, res=None):
    bshape = b.shape[-2:]
    if mode == "nn":
        (M, K), N = a.shape, bshape[1]
    elif mode == "nt":
        (M, K), N = a.shape[-2:], bshape[0]
        K = 2 * K if halves else K
    else:
        (K, M), N = a.shape, bshape[1]
        N = 2 * N if halves else N
    if rows is not None:
        if mode == "tn":
            K = rows
        else:
            M = rows
    tm, tn, tk = _pick(M, tm), _pick(N, tn), _pick(K, tk)
    gm, gn, gk = M // tm, N // tn, K // tk

    def ij(g0, g1):
        return (g1, g0) if outer == "j" else (g0, g1)

    if mode == "nn":
        a_spec = BS((tm, tk), lambda g0, g1, k: (ij(g0, g1)[0], k))
        b_spec = BS((tk, tn), lambda g0, g1, k: (k, ij(g0, g1)[1]))
    elif mode == "nt":
        a_spec = BS((tm, tk), lambda g0, g1, k: (ij(g0, g1)[0], k))
        b_spec = BS((tn, tk), lambda g0, g1, k: (ij(g0, g1)[1], k))
    else:
        a_spec = BS((tk, tm), lambda g0, g1, k: (k, ij(g0, g1)[0]))
        b_spec = BS((tk, tn), lambda g0, g1, k: (k, ij(g0, g1)[1]))
    if halves and mode == "nt":
        per = K // 2 // tk
        a_spec = BS((None, tm, tk), lambda g0, g1, k: (k // per, ij(g0, g1)[0], k % per))
    if halves and mode == "tn":
        per = N // 2 // tn
        b_spec = BS((None, tk, tn), lambda g0, g1, k: (ij(g0, g1)[1] // per, k, ij(g0, g1)[1] % per))
    if b_lead is not None:
        blk, at = b_spec.block_shape, b_spec.index_map
        b_spec = BS((None,) + tuple(blk), lambda g0, g1, k: (b_lead,) + tuple(at(g0, g1, k)))
    in_specs = [a_spec, b_spec]
    operands = [a, b]
    if bias is not None:
        in_specs.append(BS((1, tn), lambda g0, g1, k: (0, ij(g0, g1)[1])))
        operands.append(bias.reshape(1, N))
    o_spec = BS((tm, tn), lambda g0, g1, k: ij(g0, g1))
    out_shape, out_specs = jax.ShapeDtypeStruct((M, N), out_dtype), o_spec
    if res is not None:
        x_res, gate, lat_rows = res
        G = gate.shape[0]
        in_specs += [o_spec, BS((G, 1, tn), lambda g0, g1, k: (0, 0, ij(g0, g1)[1]))]
        operands += [x_res, gate]
        out_shape, out_specs = (jax.ShapeDtypeStruct((M, N), F32), jax.ShapeDtypeStruct((M, N), BF16)), (o_spec, o_spec)
    n_in = len(in_specs)

    def body(*refs):
        a_ref, b_ref = refs[0], refs[1]
        bias_ref = refs[2] if bias is not None else None
        o_ref = refs[n_in]
        av = a_ref[...]
        if silu_a:
            av = av.astype(F32)
            av = av * _sigmoid(av)
        part = lax.dot_general(av.astype(BF16), b_ref[...].astype(BF16), _DIMS[mode], preferred_element_type=F32)
        i_blk = pl.program_id(1 if outer == "j" else 0)

        def finish(acc):
            if bias_ref is not None:
                acc = acc + bias_ref[...]
            if res is None:
                o_ref[...] = acc.astype(out_dtype)
                return
            x_ref, gate_ref, y_ref = refs[n_in - 2], refs[n_in - 1], refs[n_in + 1]
            gate_ = gate_ref[0]
            if G == 2:
                rows = i_blk * tm + lax.broadcasted_iota(jnp.int32, (tm, 1), 0)
                gate_ = jnp.where(rows < lat_rows, gate_ref[0], gate_ref[1])
            o_ref[...] = x_ref[...] + gate_ * acc
            y_ref[...] = acc.astype(BF16)

        if gk == 1:
            finish(part)
        else:
            acc_ref = refs[-1]
            k = pl.program_id(2)

            @pl.when(k == 0)
            def _():
                acc_ref[...] = part

            @pl.when(k > 0)
            def _():
                acc_ref[...] += part

            @pl.when(k == gk - 1)
            def _():
                finish(acc_ref[...])

    grid = (gn, gm, gk) if outer == "j" else (gm, gn, gk)
    return _call(body, name=name, out_shape=out_shape, grid=grid, in_specs=in_specs, out_specs=out_specs,
                 scratch=[pltpu.VMEM((tm, tn), F32)] if gk > 1 else [], sem=("parallel", "parallel", "arbitrary"),
                 operands=operands)


def _accum(ref, val, first):
    @pl.when(first)
    def _():
        ref[...] = val

    @pl.when(jnp.logical_not(first))
    def _():
        ref[...] += val


def _norm_mod_fwd(X, gain, sh, sc, tb, nlat, name):
    R, D = X.shape

    def body(x_ref, g_ref, sh_ref, sc_ref, o_ref):
        x = x_ref[...]
        r = lax.rsqrt(jnp.mean(x * x, axis=-1, keepdims=True) + EPS)
        o_ref[...] = ((x * r * g_ref[...]) * (1.0 + sc_ref[0]) + sh_ref[0]).astype(BF16)

    grp = BS((1, 1, D), lambda i: (i // nlat, 0, 0))
    return _pcs(
        body, name=name, out_shape=jax.ShapeDtypeStruct((R, D), BF16), grid=(R // tb,),
        in_specs=[BS((tb, D), lambda i: (i, 0)), BS((1, D), lambda i: (0, 0)), grp, grp],
        out_specs=BS((tb, D), lambda i: (i, 0)), compiler_params=_cp(("parallel",)),
    )(X, gain.reshape(1, D), sh, sc)


def _gate_back(dx, y_ref, gate_ref, dy_ref, dgate_ref, first):
    dy_ref[...] = (gate_ref[0] * dx).astype(BF16)
    _accum(dgate_ref, jnp.sum(dx * y_ref[...], axis=0, keepdims=True)[None], first)


def _norm_mod_bwd(X, gain, sc, dh, dup, tb, nlat, name, before=None):
    R, D = X.shape
    G = sc.shape[0]
    n_up = dup.shape[0] // tb

    def body(x_ref, g_ref, sc_ref, dh_ref, dup_ref, *rest):
        dx_ref, dg_ref, dsh_ref, dsc_ref = rest[2:6] if before else rest[:4]
        i = pl.program_id(0)
        x = x_ref[...]
        g = g_ref[...]
        r = lax.rsqrt(jnp.mean(x * x, axis=-1, keepdims=True) + EPS)
        xh = x * r
        dh_ = dh_ref[...].astype(F32)
        t = dh_ * (1.0 + sc_ref[0])
        tg = t * g
        dup_ = dup_ref[...] if n_up == R // tb else jnp.where(i < n_up, dup_ref[...], 0.0)
        dx = dup_ + r * (tg - xh * jnp.mean(tg * xh, axis=-1, keepdims=True))
        dx_ref[...] = dx
        _accum(dg_ref, jnp.sum(t * xh, axis=0, keepdims=True), i == 0)
        first = i % nlat == 0
        _accum(dsh_ref, jnp.sum(dh_, axis=0, keepdims=True)[None], first)
        _accum(dsc_ref, jnp.sum(dh_ * xh * g, axis=0, keepdims=True)[None], first)
        if before:
            _gate_back(dx, rest[0], rest[1], rest[6], rest[7], first)

    row = BS((tb, D), lambda i: (i, 0))
    grp = BS((1, 1, D), lambda i: (i // nlat, 0, 0))
    vec = BS((1, D), lambda i: (0, 0))
    sds = jax.ShapeDtypeStruct
    out_shape = (sds((R, D), F32), sds((1, D), F32), sds((G, 1, D), F32), sds((G, 1, D), F32))
    in_specs, out_specs = [row, vec, grp, row, BS((tb, D), lambda i: (jnp.minimum(i, n_up - 1), 0))], (row, vec, grp, grp)
    operands = (X, gain.reshape(1, D), sc, dh, dup)
    if before:
        in_specs, operands = in_specs + [row, grp], operands + tuple(before)
        out_shape, out_specs = out_shape + (sds((R, D), BF16), sds((G, 1, D), F32)), out_specs + (row, grp)
    return _pcs(body, name=name, out_shape=out_shape, grid=(R // tb,), in_specs=in_specs, out_specs=out_specs,
                compiler_params=_cp(("arbitrary",)))(*operands)


def _final_loss(X, gain, target, before, tb, name):
    R, D = X.shape

    def body(x_ref, g_ref, t_ref, y_ref, gate_ref, loss_ref, dx_ref, dg_ref, dy_ref, dgate_ref):
        i = pl.program_id(0)
        x = x_ref[...]
        g = g_ref[...]
        r = lax.rsqrt(jnp.mean(x * x, axis=-1, keepdims=True) + EPS)
        xh = x * r
        e = xh * g - t_ref[...]
        part = jnp.sum(jnp.sum(e * e, axis=1, keepdims=True), axis=0, keepdims=True) * (0.5 / D)
        _accum(loss_ref, jnp.broadcast_to(part, (1, LANES)), i == 0)
        dy = e * (1.0 / D)
        _accum(dg_ref, jnp.sum(dy * xh, axis=0, keepdims=True), i == 0)
        tg = dy * g
        dx = r * (tg - xh * jnp.mean(tg * xh, axis=-1, keepdims=True))
        dx_ref[...] = dx
        _gate_back(dx, y_ref, gate_ref, dy_ref, dgate_ref, i == 0)

    row = BS((tb, D), lambda i: (i, 0))
    vec = BS((1, D), lambda i: (0, 0))
    grp = BS((1, 1, D), lambda i: (0, 0, 0))
    sds = jax.ShapeDtypeStruct
    return _pcs(
        body, name=name,
        out_shape=(sds((1, LANES), F32), sds((R, D), F32), sds((1, D), F32), sds((R, D), BF16), sds((1, 1, D), F32)),
        grid=(R // tb,), in_specs=[row, vec, row, row, grp],
        out_specs=(BS((1, LANES), lambda i: (0, 0)), row, vec, row, grp), compiler_params=_cp(("arbitrary",)),
    )(X, gain.reshape(1, D), target, *before)


def _ln_stats(v):
    mu = jnp.mean(v, axis=-1, keepdims=True)
    d = v - mu
    r = lax.rsqrt(jnp.mean(d * d, axis=-1, keepdims=True) + EPS)
    return d * r, r


def _gmlp_fwd(z, ln_g, ln_b, w_s, b_s_full, tb, name):
    R = z.shape[0]
    AW = ln_g.shape[-1]
    AH = w_s.shape[0]

    def body(zu_ref, zv_ref, g_ref, b_ref, ws_ref, bs_ref, o_ref):
        u, _ = _gelu(zu_ref[...].astype(F32))
        v, _ = _gelu(zv_ref[...].astype(F32))
        xh, _ = _ln_stats(v)
        vn = (xh * g_ref[...] + b_ref[...]).astype(BF16)
        for n in range(tb // CHUNK):
            rs = slice(n * CHUNK, (n + 1) * CHUNK)
            for h in range(AH):
                cs = slice(h * CHUNK, (h + 1) * CHUNK)
                v2 = jnp.dot(ws_ref[h].astype(BF16), vn[rs, cs], preferred_element_type=F32) + bs_ref[h]
                o_ref[rs, cs] = (u[rs, cs] * v2).astype(BF16)

    full3 = lambda s: BS(s, lambda i: (0, 0, 0))
    return _pcs(
        body, name=name, out_shape=jax.ShapeDtypeStruct((R, AW), BF16), grid=(R // tb,),
        in_specs=[BS((tb, AW), lambda i: (i, 0)), BS((tb, AW), lambda i: (i, 1)), BS((1, AW), lambda i: (0, 0)),
                  BS((1, AW), lambda i: (0, 0)), full3((AH, CHUNK, CHUNK)), full3((AH, CHUNK, CHUNK))],
        out_specs=BS((tb, AW), lambda i: (i, 0)), compiler_params=_cp(("parallel",)),
    )(z, z, ln_g.reshape(1, AW), ln_b.reshape(1, AW), w_s, b_s_full)


def _gmlp_bwd(z, dya, ln_g, ln_b, w_s, b_s_full, tb, name):
    R = z.shape[0]
    AW = ln_g.shape[-1]
    AH = w_s.shape[0]

    def body(zu_ref, zv_ref, dy_ref, g_ref, b_ref, ws_ref, bs_ref, dzu_ref, dzv_ref, dg_ref, db_ref, dws_ref, dbs_ref, cs_u_ref, cs_v_ref,
             dvn_scr):
        i = pl.program_id(0)
        first = i == 0
        zu = zu_ref[...].astype(F32)
        zv = zv_ref[...].astype(F32)
        u, thu = _gelu(zu)
        v, thv = _gelu(zv)
        xh, r = _ln_stats(v)
        g = g_ref[...]
        vn = (xh * g + b_ref[...]).astype(BF16)
        dy = dy_ref[...].astype(F32)

        @pl.when(first)
        def _():
            dws_ref[...] = jnp.zeros_like(dws_ref)
            dbs_ref[...] = jnp.zeros_like(dbs_ref)

        for n in range(tb // CHUNK):
            rs = slice(n * CHUNK, (n + 1) * CHUNK)
            for h in range(AH):
                cs = slice(h * CHUNK, (h + 1) * CHUNK)
                w = ws_ref[h].astype(BF16)
                v2 = jnp.dot(w, vn[rs, cs], preferred_element_type=F32) + bs_ref[h]
                dzu_ref[rs, cs] = (dy[rs, cs] * v2 * _gelu_grad(zu[rs, cs], thu[rs, cs])).astype(BF16)
                dv2 = dy[rs, cs] * u[rs, cs]
                dv2b = dv2.astype(BF16)
                dvn_scr[rs, cs] = lax.dot_general(w, dv2b, _DIMS["tn"], preferred_element_type=F32)
                dws_ref[h] += lax.dot_general(dv2b, vn[rs, cs], _DIMS["nt"], preferred_element_type=F32)
                dbs_ref[h] += jnp.sum(dv2, axis=1, keepdims=True)
        dvn = dvn_scr[...]
        _accum(dg_ref, jnp.sum(dvn * xh, axis=0, keepdims=True), first)
        _accum(db_ref, jnp.sum(dvn, axis=0, keepdims=True), first)
        t = dvn * g
        dv = r * (t - jnp.mean(t, axis=-1, keepdims=True) - xh * jnp.mean(t * xh, axis=-1, keepdims=True))
        dzv = dv * _gelu_grad(zv, thv)
        dzv_ref[...] = dzv.astype(BF16)
        _accum(cs_v_ref, jnp.sum(dzv, axis=0, keepdims=True), first)
        _accum(cs_u_ref, jnp.sum(dzu_ref[...].astype(F32), axis=0, keepdims=True), first)

    full3 = lambda s: BS(s, lambda i: (0, 0, 0))
    vec = BS((1, AW), lambda i: (0, 0))
    row = BS((tb, AW), lambda i: (i, 0))
    outs = _pcs(
        body, name=name,
        out_shape=(jax.ShapeDtypeStruct((R, AW), BF16), jax.ShapeDtypeStruct((R, AW), BF16), jax.ShapeDtypeStruct((1, AW), F32),
                   jax.ShapeDtypeStruct((1, AW), F32), jax.ShapeDtypeStruct((AH, CHUNK, CHUNK), F32),
                   jax.ShapeDtypeStruct((AH, CHUNK, 1), F32), jax.ShapeDtypeStruct((1, AW), F32), jax.ShapeDtypeStruct((1, AW), F32)),
        grid=(R // tb,),
        in_specs=[row, BS((tb, AW), lambda i: (i, 1)), row, vec, vec, full3((AH, CHUNK, CHUNK)), full3((AH, CHUNK, CHUNK))],
        out_specs=(row, row, vec, vec, full3((AH, CHUNK, CHUNK)), full3((AH, CHUNK, 1)), vec, vec),
        scratch_shapes=[pltpu.VMEM((tb, AW), F32)], compiler_params=_cp(("arbitrary",)),
    )(z, z, dya, ln_g.reshape(1, AW), ln_b.reshape(1, AW), w_s, b_s_full)
    return outs


def _segments(R, L):
    return [(0, L)] + ([(L, R - L)] if R > L else [])


def _scr_rows(R, L):
    return R + CONV_PAD * (len(_segments(R, L)) + 1)


def _scr_off(s, start):
    return CONV_PAD * (s + 1) + start


def _zero_pads(scr, R, L):
    segs = _segments(R, L)
    z = jnp.zeros((CONV_PAD, scr.shape[1]), F32)
    for s, (start, n) in enumerate(segs):
        scr[pl.ds(_scr_off(s, start) - CONV_PAD, CONV_PAD), :] = z
    last_s, (last_start, last_n) = len(segs) - 1, segs[-1]
    scr[pl.ds(_scr_off(last_s, last_start) + last_n, CONV_PAD), :] = z


def _for_chunks(R, L, ch, fn):
    for s, (start, n) in enumerate(_segments(R, L)):
        c = min(ch, n)
        off = _scr_off(s, start)

        def step(i, carry, start=start, off=off, c=c):
            r0 = pl.multiple_of(start + i * c, SUBLANES)
            fn(r0, pl.multiple_of(off + i * c, SUBLANES), c)
            return carry

        lax.fori_loop(0, n // c, step, 0)


def _taps(scr, srow, c, w_ref, ntap, flip):
    acc = None
    for k in range(ntap):
        o = k - (ntap - 1) // 2
        if flip:
            o = -o
        term = w_ref[k:k + 1, :] * scr[pl.ds(srow + o, c), :]
        acc = term if acc is None else acc + term
    return acc


def _tap_grads(scr, srow, c, dy, dw_ref, ntap):
    for k in range(ntap):
        o = k - (ntap - 1) // 2
        dw_ref[k:k + 1, :] += jnp.sum(dy * scr[pl.ds(srow + o, c), :], axis=0, keepdims=True)


def _glu_conv_fwd(z, col0, conv_w, conv_b, L, ch, name):
    R = z.shape[0]
    BW = conv_w.shape[1]
    nb, c0 = BW // LANES, col0 // LANES

    def body(a_ref, g_ref, w_ref, b_ref, o_ref, scr):
        _zero_pads(scr, R, L)

        def fill(r0, s0, c):
            a = a_ref[pl.ds(r0, c), :].astype(F32)
            g = g_ref[pl.ds(r0, c), :].astype(F32)
            scr[pl.ds(s0, c), :] = a * _sigmoid(g)

        _for_chunks(R, L, ch, fill)

        def conv(r0, s0, c):
            o_ref[pl.ds(r0, c), :] = _taps(scr, s0, c, w_ref, B_CONV, False) + b_ref[...]

        _for_chunks(R, L, ch, conv)

    return _pcs(
        body, name=name, out_shape=jax.ShapeDtypeStruct((R, BW), F32), grid=(nb,),
        in_specs=[BS((R, LANES), lambda j: (0, c0 + j)), BS((R, LANES), lambda j: (0, c0 + nb + j)),
                  BS((B_CONV, LANES), lambda j: (0, j)), BS((1, LANES), lambda j: (0, j))],
        out_specs=BS((R, LANES), lambda j: (0, j)), scratch_shapes=[pltpu.VMEM((_scr_rows(R, L), LANES), F32)],
        compiler_params=_cp(("parallel",)),
    )(z, z, conv_w, conv_b.reshape(1, BW))


def _glu_conv_bwd(z, col0, dhc, conv_w, L, ch, name):
    R = z.shape[0]
    BW = conv_w.shape[1]
    nb, c0 = BW // LANES, col0 // LANES

    def body(a_ref, g_ref, dy_ref, w_ref, da_ref, dg_ref, dw_ref, db_ref, csa_ref, csg_ref, scr_h, scr_dy):
        _zero_pads(scr_h, R, L)
        _zero_pads(scr_dy, R, L)
        dw_ref[...] = jnp.zeros_like(dw_ref)
        db_ref[...] = jnp.zeros_like(db_ref)
        csa_ref[...] = jnp.zeros_like(csa_ref)
        csg_ref[...] = jnp.zeros_like(csg_ref)

        def fill(r0, s0, c):
            a = a_ref[pl.ds(r0, c), :].astype(F32)
            g = g_ref[pl.ds(r0, c), :].astype(F32)
            scr_h[pl.ds(s0, c), :] = a * _sigmoid(g)
            scr_dy[pl.ds(s0, c), :] = dy_ref[pl.ds(r0, c), :]

        _for_chunks(R, L, ch, fill)

        def back(r0, s0, c):
            dh = _taps(scr_dy, s0, c, w_ref, B_CONV, True)
            a = a_ref[pl.ds(r0, c), :].astype(F32)
            sg = _sigmoid(g_ref[pl.ds(r0, c), :].astype(F32))
            da = dh * sg
            dg = dh * a * sg * (1.0 - sg)
            da_ref[pl.ds(r0, c), :] = da.astype(BF16)
            dg_ref[pl.ds(r0, c), :] = dg.astype(BF16)
            csa_ref[...] += jnp.sum(da, axis=0, keepdims=True)
            csg_ref[...] += jnp.sum(dg, axis=0, keepdims=True)
            dy = dy_ref[pl.ds(r0, c), :]
            db_ref[...] += jnp.sum(dy, axis=0, keepdims=True)
            _tap_grads(scr_h, s0, c, dy, dw_ref, B_CONV)

        _for_chunks(R, L, ch, back)

    col = BS((R, LANES), lambda j: (0, j))
    vec = BS((1, LANES), lambda j: (0, j))
    nrow = _scr_rows(R, L)
    return _pcs(
        body, name=name,
        out_shape=(jax.ShapeDtypeStruct((R, BW), BF16), jax.ShapeDtypeStruct((R, BW), BF16), jax.ShapeDtypeStruct((B_CONV, BW), F32),
                   jax.ShapeDtypeStruct((1, BW), F32), jax.ShapeDtypeStruct((1, BW), F32), jax.ShapeDtypeStruct((1, BW), F32)),
        grid=(nb,),
        in_specs=[BS((R, LANES), lambda j: (0, c0 + j)), BS((R, LANES), lambda j: (0, c0 + nb + j)), col,
                  BS((B_CONV, LANES), lambda j: (0, j))],
        out_specs=(col, col, BS((B_CONV, LANES), lambda j: (0, j)), vec, vec, vec),
        scratch_shapes=[pltpu.VMEM((nrow, LANES), F32), pltpu.VMEM((nrow, LANES), F32)], compiler_params=_cp(("parallel",)),
    )(z, z, dhc, conv_w)


def _ln_silu_fwd(hc, ln_g, ln_b, tb, name):
    R, W = hc.shape

    def body(x_ref, g_ref, b_ref, o_ref):
        xh, _ = _ln_stats(x_ref[...])
        y = xh * g_ref[...] + b_ref[...]
        o_ref[...] = (y * _sigmoid(y)).astype(BF16)

    vec = BS((1, W), lambda i: (0, 0))
    row = BS((tb, W), lambda i: (i, 0))
    return _pcs(
        body, name=name, out_shape=jax.ShapeDtypeStruct((R, W), BF16), grid=(R // tb,), in_specs=[row, vec, vec], out_specs=row,
        compiler_params=_cp(("parallel",)),
    )(hc, ln_g.reshape(1, W), ln_b.reshape(1, W))


def _ln_silu_bwd(hc, dyb, col0, ln_g, ln_b, tb, name):
    R, W = hc.shape
    c0 = col0 // W

    def body(x_ref, dy_ref, g_ref, b_ref, dx_ref, dg_ref, db_ref):
        i = pl.program_id(0)
        xh, r = _ln_stats(x_ref[...])
        g = g_ref[...]
        y = xh * g + b_ref[...]
        s = _sigmoid(y)
        dy = dy_ref[...].astype(F32) * s * (1.0 + y * (1.0 - s))
        _accum(dg_ref, jnp.sum(dy * xh, axis=0, keepdims=True), i == 0)
        _accum(db_ref, jnp.sum(dy, axis=0, keepdims=True), i == 0)
        t = dy * g
        dx_ref[...] = r * (t - jnp.mean(t, axis=-1, keepdims=True) - xh * jnp.mean(t * xh, axis=-1, keepdims=True))

    vec = BS((1, W), lambda i: (0, 0))
    row = BS((tb, W), lambda i: (i, 0))
    return _pcs(
        body, name=name,
        out_shape=(jax.ShapeDtypeStruct((R, W), F32), jax.ShapeDtypeStruct((1, W), F32), jax.ShapeDtypeStruct((1, W), F32)),
        grid=(R // tb,), in_specs=[row, BS((tb, W), lambda i: (i, c0)), vec, vec], out_specs=(row, vec, vec),
        compiler_params=_cp(("arbitrary",)),
    )(hc, dyb, ln_g.reshape(1, W), ln_b.reshape(1, W))


def _ffn_act_fwd(zf, conv_w, conv_b, L, ch, name):
    R = zf.shape[0]
    DFF = conv_w.shape[1]
    nb = DFF // LANES

    def body(g_ref, u_ref, w_ref, b_ref, o_ref, scr):
        _zero_pads(scr, R, L)

        def fill(r0, s0, c):
            scr[pl.ds(s0, c), :] = g_ref[pl.ds(r0, c), :].astype(F32)

        _for_chunks(R, L, ch, fill)

        def act(r0, s0, c):
            gc = _taps(scr, s0, c, w_ref, FFN_CONV, False) + b_ref[...]
            o_ref[pl.ds(r0, c), :] = (gc * _sigmoid(gc) * u_ref[pl.ds(r0, c), :].astype(F32)).astype(BF16)

        _for_chunks(R, L, ch, act)

    return _pcs(
        body, name=name, out_shape=jax.ShapeDtypeStruct((R, DFF), BF16), grid=(nb,),
        in_specs=[BS((R, LANES), lambda j: (0, j)), BS((R, LANES), lambda j: (0, nb + j)), BS((FFN_CONV, LANES), lambda j: (0, j)),
                  BS((1, LANES), lambda j: (0, j))],
        out_specs=BS((R, LANES), lambda j: (0, j)), scratch_shapes=[pltpu.VMEM((_scr_rows(R, L), LANES), F32)],
        compiler_params=_cp(("parallel",)),
    )(zf, zf, conv_w, conv_b.reshape(1, DFF))


def _ffn_act_bwd(zf, df, conv_w, conv_b, L, ch, name):
    R = zf.shape[0]
    DFF = conv_w.shape[1]
    nb = DFF // LANES

    def body(g_ref, u_ref, df_ref, w_ref, b_ref, dz_ref, dw_ref, db_ref, scr_g, scr_d):
        _zero_pads(scr_g, R, L)
        _zero_pads(scr_d, R, L)
        dw_ref[...] = jnp.zeros_like(dw_ref)
        db_ref[...] = jnp.zeros_like(db_ref)

        def fill(r0, s0, c):
            scr_g[pl.ds(s0, c), :] = g_ref[pl.ds(r0, c), :].astype(F32)

        _for_chunks(R, L, ch, fill)

        def pre(r0, s0, c):
            gc = _taps(scr_g, s0, c, w_ref, FFN_CONV, False) + b_ref[...]
            s = _sigmoid(gc)
            d = df_ref[pl.ds(r0, c), :].astype(F32)
            dz_ref[1, pl.ds(r0, c), :] = (d * gc * s).astype(BF16)
            dgc = d * u_ref[pl.ds(r0, c), :].astype(F32) * s * (1.0 + gc * (1.0 - s))
            scr_d[pl.ds(s0, c), :] = dgc
            db_ref[...] += jnp.sum(dgc, axis=0, keepdims=True)
            _tap_grads(scr_g, s0, c, dgc, dw_ref, FFN_CONV)

        _for_chunks(R, L, ch, pre)

        def back(r0, s0, c):
            dz_ref[0, pl.ds(r0, c), :] = _taps(scr_d, s0, c, w_ref, FFN_CONV, True).astype(BF16)

        _for_chunks(R, L, ch, back)

    col = BS((R, LANES), lambda j: (0, j))
    vec = BS((1, LANES), lambda j: (0, j))
    nrow = _scr_rows(R, L)
    return _pcs(
        body, name=name,
        out_shape=(jax.ShapeDtypeStruct((2, R, DFF), BF16), jax.ShapeDtypeStruct((FFN_CONV, DFF), F32), jax.ShapeDtypeStruct((1, DFF), F32)),
        grid=(nb,),
        in_specs=[col, BS((R, LANES), lambda j: (0, nb + j)), col, BS((FFN_CONV, LANES), lambda j: (0, j)), vec],
        out_specs=(BS((2, R, LANES), lambda j: (0, 0, j)), BS((FFN_CONV, LANES), lambda j: (0, j)), vec),
        scratch_shapes=[pltpu.VMEM((nrow, LANES), F32), pltpu.VMEM((nrow, LANES), F32)], compiler_params=_cp(("parallel",)),
    )(zf, zf, df, conv_w, conv_b.reshape(1, DFF))


def _rope_tables(L, T):
    rows = L // GRID_W
    row = jnp.repeat(jnp.arange(rows, dtype=F32), GRID_W)
    col = jnp.tile(jnp.arange(GRID_W, dtype=F32), rows)
    n_freq = ROPE // 4
    inv = ROPE_THETA ** (-jnp.arange(n_freq, dtype=F32) / n_freq)
    ang = jnp.concatenate([row[:, None] * inv, col[:, None] * inv], axis=-1)
    cos, sin = jnp.cos(ang), jnp.sin(ang)
    half = ROPE // 2
    zero = jnp.zeros((L, half), F32)
    cos_t = jnp.concatenate([cos, cos, jnp.ones((L, LANES - ROPE), F32)], axis=1)
    sa = jnp.concatenate([zero, sin, zero, zero], axis=1)
    sb = jnp.concatenate([-sin, zero, zero, zero], axis=1)
    pad = T - L
    cos_t = jnp.concatenate([cos_t, jnp.ones((pad, LANES), F32)], axis=0)
    sa = jnp.concatenate([sa, jnp.zeros((pad, LANES), F32)], axis=0)
    sb = jnp.concatenate([sb, jnp.zeros((pad, LANES), F32)], axis=0)
    return cos_t, sa, sb


def _rope(x, cos, sa, sb):
    half = ROPE // 2
    return x * cos + pltpu.roll(x, half, 1) * sa + pltpu.roll(x, LANES - half, 1) * sb


def _rope_t(d, cos, sa, sb):
    half = ROPE // 2
    return d * cos + pltpu.roll(d * sa, LANES - half, 1) + pltpu.roll(d * sb, half, 1)


def _mla_prep_fwd(zm, qg, kvg, tabs, tb, name):
    T, W = zm.shape
    QL, KVL = qg.shape[-1], kvg.shape[-1]

    def body(z_ref, qg_ref, kg_ref, cos_ref, sa_ref, sb_ref, q_ref, k_ref, p_ref):
        cq = z_ref[:, :QL]
        r = lax.rsqrt(jnp.mean(cq * cq, axis=-1, keepdims=True) + EPS)
        q_ref[...] = (cq * r * qg_ref[...]).astype(BF16)
        ck = z_ref[:, QL:QL + KVL]
        r = lax.rsqrt(jnp.mean(ck * ck, axis=-1, keepdims=True) + EPS)
        k_ref[...] = (ck * r * kg_ref[...]).astype(BF16)
        p_ref[...] = _rope(z_ref[:, QL + KVL:], cos_ref[...], sa_ref[...], sb_ref[...]).astype(BF16)

    tab = BS((tb, LANES), lambda i: (i, 0))
    return _pcs(
        body, name=name,
        out_shape=(jax.ShapeDtypeStruct((T, QL), BF16), jax.ShapeDtypeStruct((T, KVL), BF16), jax.ShapeDtypeStruct((T, LANES), BF16)),
        grid=(T // tb,),
        in_specs=[BS((tb, W), lambda i: (i, 0)), BS((1, QL), lambda i: (0, 0)), BS((1, KVL), lambda i: (0, 0)), tab, tab, tab],
        out_specs=(BS((tb, QL), lambda i: (i, 0)), BS((tb, KVL), lambda i: (i, 0)), tab), compiler_params=_cp(("parallel",)),
    )(zm, qg.reshape(1, QL), kvg.reshape(1, KVL), *tabs)


def _mla_prep_bwd(zm, qg, kvg, tabs, dq, dk, dp, tb, name):
    T, W = zm.shape
    QL, KVL = qg.shape[-1], kvg.shape[-1]

    def body(z_ref, qg_ref, kg_ref, cos_ref, sa_ref, sb_ref, dq_ref, dk_ref, dp_ref, dz_ref, dqg_ref, dkg_ref):
        i = pl.program_id(0)

        def rms_bwd(x, g, dy):
            r = lax.rsqrt(jnp.mean(x * x, axis=-1, keepdims=True) + EPS)
            xh = x * r
            t = dy * g
            return r * (t - xh * jnp.mean(t * xh, axis=-1, keepdims=True)), jnp.sum(dy * xh, axis=0, keepdims=True)

        dcq, dg = rms_bwd(z_ref[:, :QL], qg_ref[...], dq_ref[...].astype(F32))
        dz_ref[:, :QL] = dcq.astype(BF16)
        _accum(dqg_ref, dg, i == 0)
        dck, dg = rms_bwd(z_ref[:, QL:QL + KVL], kg_ref[...], dk_ref[...].astype(F32))
        dz_ref[:, QL:QL + KVL] = dck.astype(BF16)
        _accum(dkg_ref, dg, i == 0)
        dz_ref[:, QL + KVL:] = _rope_t(dp_ref[...], cos_ref[...], sa_ref[...], sb_ref[...]).astype(BF16)

    tab = BS((tb, LANES), lambda i: (i, 0))
    return _pcs(
        body, name=name,
        out_shape=(jax.ShapeDtypeStruct((T, W), BF16), jax.ShapeDtypeStruct((1, QL), F32), jax.ShapeDtypeStruct((1, KVL), F32)),
        grid=(T // tb,),
        in_specs=[BS((tb, W), lambda i: (i, 0)), BS((1, QL), lambda i: (0, 0)), BS((1, KVL), lambda i: (0, 0)), tab, tab, tab,
                  BS((tb, QL), lambda i: (i, 0)), BS((tb, KVL), lambda i: (i, 0)), tab],
        out_specs=(BS((tb, W), lambda i: (i, 0)), BS((1, QL), lambda i: (0, 0)), BS((1, KVL), lambda i: (0, 0))),
        compiler_params=_cp(("arbitrary",)),
    )(zm, qg.reshape(1, QL), kvg.reshape(1, KVL), *tabs, dq, dk, dp)


def _attn_fwd(q, kv, kpe, tabs, L, tq, nsub, name):
    T = kv.shape[0]
    H = kv.shape[1] // QHEAD
    scale = (HEAD + ROPE) ** -0.5

    def body(q_ref, kv_ref, kpe_ref, cos_ref, sa_ref, sb_ref, o_ref, lse_ref, kcat):
        @pl.when(pl.program_id(1) == 0)
        def _():
            kcat[:, :HEAD] = kv_ref[:, :HEAD]
            kcat[:, HEAD:] = kpe_ref[...]

        sub = tq // nsub
        for r in range(nsub):
            rs = slice(r * sub, (r + 1) * sub)
            qp = _rope(q_ref[rs, HEAD:].astype(F32), cos_ref[rs, :], sa_ref[rs, :], sb_ref[rs, :]).astype(BF16)
            qc = jnp.concatenate([q_ref[rs, :HEAD], qp], axis=1)
            s = lax.dot_general(qc, kcat[...], _DIMS["nt"], preferred_element_type=F32)
            m = jnp.max(s, axis=-1, keepdims=True)
            p = jnp.exp2((s - m) * (scale * math.log2(math.e)))
            l = jnp.sum(p, axis=-1, keepdims=True)
            o = jnp.dot(p.astype(BF16), kv_ref[:, HEAD:], preferred_element_type=F32)
            o_ref[rs, :] = (o / l).astype(BF16)
            lse_ref[0, rs, :] = m * scale + jnp.log(l)

    tab = BS((tq, LANES), lambda h, i: (i, 0))
    return _call(
        body, name=name, out_shape=(jax.ShapeDtypeStruct((L, H * HEAD), BF16), jax.ShapeDtypeStruct((H, L, 1), F32)),
        grid=(H, L // tq),
        in_specs=[BS((tq, QHEAD), lambda h, i: (i, h)), BS((T, QHEAD), lambda h, i: (0, h)), BS((T, LANES), lambda h, i: (0, 0)),
                  tab, tab, tab],
        out_specs=(BS((tq, HEAD), lambda h, i: (i, h)), BS((1, tq, 1), lambda h, i: (h, i, 0))),
        scratch=[pltpu.VMEM((T, QHEAD), BF16)], sem=("parallel", "arbitrary"), operands=(q, kv, kpe, *tabs))


def _attn_bwd(q, kv, kpe, tabs, o, lse, do, L, tq, name):
    T = kv.shape[0]
    H = kv.shape[1] // QHEAD
    scale = (HEAD + ROPE) ** -0.5
    nq = L // tq

    def body(q_ref, kv_ref, kpe_ref, cos_ref, sa_ref, sb_ref, o_ref, lse_ref, do_ref, dq_ref, dkv_ref, dkpe_ref, kcat, dk_acc, dv_acc,
             qc_scr, ds_scr, p_scr):
        h, i = pl.program_id(0), pl.program_id(1)

        @pl.when(i == 0)
        def _():
            kcat[:, :HEAD] = kv_ref[:, :HEAD]
            kcat[:, HEAD:] = kpe_ref[...]
            dk_acc[...] = jnp.zeros_like(dk_acc)
            dv_acc[...] = jnp.zeros_like(dv_acc)

        sub = tq // 2
        log2e = math.log2(math.e)
        for r in range(2):
            rs = slice(r * sub, (r + 1) * sub)
            cos, sa, sb = cos_ref[rs, :], sa_ref[rs, :], sb_ref[rs, :]
            qp = _rope(q_ref[rs, HEAD:].astype(F32), cos, sa, sb).astype(BF16)
            qc_scr[rs, :] = jnp.concatenate([q_ref[rs, :HEAD], qp], axis=1)
            s = lax.dot_general(qc_scr[rs, :], kcat[...], _DIMS["nt"], preferred_element_type=F32)
            p = jnp.exp2(s * (scale * log2e) - lse_ref[0, rs, :] * log2e)
            dov = do_ref[rs, :]
            delta = jnp.sum(dov.astype(F32) * o_ref[rs, :].astype(F32), axis=-1, keepdims=True)
            dp = lax.dot_general(dov, kv_ref[:, HEAD:], _DIMS["nt"], preferred_element_type=F32)
            ds_scr[rs, :] = (p * (dp - delta) * scale).astype(BF16)
            p_scr[rs, :] = p.astype(BF16)
            dqc = jnp.dot(ds_scr[rs, :], kcat[...], preferred_element_type=F32)
            dq_ref[rs, :HEAD] = dqc[:, :HEAD].astype(BF16)
            dq_ref[rs, HEAD:] = _rope_t(dqc[:, HEAD:], cos, sa, sb).astype(BF16)
        dk_acc[...] += lax.dot_general(ds_scr[...], qc_scr[...], _DIMS["tn"], preferred_element_type=F32)
        dv_acc[...] += lax.dot_general(p_scr[...], do_ref[...], _DIMS["tn"], preferred_element_type=F32)

        @pl.when(i == nq - 1)
        def _():
            dkv_ref[:, :HEAD] = dk_acc[:, :HEAD].astype(BF16)
            dkv_ref[:, HEAD:] = dv_acc[...].astype(BF16)

            @pl.when(h == 0)
            def _():
                dkpe_ref[...] = dk_acc[:, HEAD:]

            @pl.when(h > 0)
            def _():
                dkpe_ref[...] += dk_acc[:, HEAD:]

    tab = BS((tq, LANES), lambda h, i: (i, 0))
    return _call(
        body, name=name,
        out_shape=(jax.ShapeDtypeStruct((L, H * QHEAD), BF16), jax.ShapeDtypeStruct((T, H * QHEAD), BF16), jax.ShapeDtypeStruct((T, LANES), F32)),
        grid=(H, nq),
        in_specs=[BS((tq, QHEAD), lambda h, i: (i, h)), BS((T, QHEAD), lambda h, i: (0, h)), BS((T, LANES), lambda h, i: (0, 0)),
                  tab, tab, tab, BS((tq, HEAD), lambda h, i: (i, h)), BS((1, tq, 1), lambda h, i: (h, i, 0)),
                  BS((tq, HEAD), lambda h, i: (i, h))],
        out_specs=(BS((tq, QHEAD), lambda h, i: (i, h)), BS((T, QHEAD), lambda h, i: (0, h)), BS((T, LANES), lambda h, i: (0, 0))),
        scratch=[pltpu.VMEM((T, QHEAD), BF16), pltpu.VMEM((T, QHEAD), F32), pltpu.VMEM((T, HEAD), F32),
                 pltpu.VMEM((tq, QHEAD), BF16), pltpu.VMEM((tq, T), BF16), pltpu.VMEM((tq, T), BF16)],
        sem=("arbitrary", "arbitrary"), operands=(q, kv, kpe, *tabs, o, lse, do))


def _adamw(w, g, m, v, name):
    R, C = w.shape
    tr = _row_tile(R, C)
    c1 = 1.0 / (1.0 - ADAM_B1 ** ADAM_STEP)
    c2 = 1.0 / (1.0 - ADAM_B2 ** ADAM_STEP)

    def body(w_ref, g_ref, m_ref, v_ref, d_ref, nm_ref, nv_ref):
        g_ = g_ref[...]
        nm = ADAM_B1 * m_ref[...] + (1.0 - ADAM_B1) * g_
        nv = ADAM_B2 * v_ref[...] + (1.0 - ADAM_B2) * (g_ * g_)
        nm_ref[...] = nm
        nv_ref[...] = nv
        d_ref[...] = -ADAM_LR * ((nm * c1) / (jnp.sqrt(nv * c2) + ADAM_EPS) + ADAM_WD * w_ref[...])

    blk = BS((tr, C), lambda i: (i, 0))
    sd = jax.ShapeDtypeStruct((R, C), F32)
    return _pcs(body, name=name, out_shape=(sd, sd, sd), grid=(R // tr,), in_specs=[blk] * 4, out_specs=(blk,) * 3,
                compiler_params=_cp(("parallel",)))(w, g, m, v)


def _adamw_layers(w, gs, m, v, name):
    n, R, C = w.shape
    tr = _row_tile(R, C)
    nr = R // tr
    c1 = 1.0 / (1.0 - ADAM_B1 ** ADAM_STEP)
    c2 = 1.0 / (1.0 - ADAM_B2 ** ADAM_STEP)

    def body(w_ref, m_ref, v_ref, *rest):
        g_refs, (go_ref, d_ref, nm_ref, nv_ref) = rest[:n], rest[n:]
        layer = pl.program_id(0)
        g_ = g_refs[0][...]
        for j in range(1, n):
            g_ = jnp.where(layer == j, g_refs[j][...], g_)
        nm = ADAM_B1 * m_ref[...] + (1.0 - ADAM_B1) * g_
        nv = ADAM_B2 * v_ref[...] + (1.0 - ADAM_B2) * (g_ * g_)
        go_ref[...] = g_
        nm_ref[...] = nm
        nv_ref[...] = nv
        d_ref[...] = -ADAM_LR * ((nm * c1) / (jnp.sqrt(nv * c2) + ADAM_EPS) + ADAM_WD * w_ref[...])

    def g_spec(j):
        return BS((tr, C), lambda layer, i: (jnp.where(layer == j, i, jnp.where(layer < j, 0, nr - 1)), 0))

    blk = BS((None, tr, C), lambda layer, i: (layer, i, 0))
    sd = jax.ShapeDtypeStruct((n, R, C), F32)
    return _pcs(body, name=name, out_shape=(sd, sd, sd, sd), grid=(n, nr), in_specs=[blk] * 3 + [g_spec(j) for j in range(n)],
                out_specs=(blk,) * 4, compiler_params=_cp(("arbitrary", "arbitrary")))(w, m, v, *gs)


def _sum_lead(a, out_dtype, name):
    n, R, C = a.shape
    tr = _row_tile(R, C * n, 2 << 20)

    def body(a_ref, o_ref):
        acc = a_ref[0].astype(F32)
        for k in range(1, n):
            acc = acc + a_ref[k].astype(F32)
        o_ref[...] = acc.astype(out_dtype)

    return pl.pallas_call(body, name=name, out_shape=jax.ShapeDtypeStruct((R, C), out_dtype), grid=(R // tr,),
                          in_specs=[BS((n, tr, C), lambda i: (0, i, 0))], out_specs=BS((tr, C), lambda i: (i, 0)),
                          compiler_params=_cp(("parallel",)))(a)


def _cctx_grad(parts, c_ctx, name):
    n, D = parts.shape

    def body(p_ref, c_ref, o_ref):
        d = jnp.sum(p_ref[...], axis=0, keepdims=True)
        c = c_ref[...]
        s = _sigmoid(c)
        o_ref[...] = d * s * (1.0 + c * (1.0 - s))

    return pl.pallas_call(body, name=name, out_shape=jax.ShapeDtypeStruct((1, D), F32))(parts, c_ctx.reshape(1, D))


def _me():
    return lax.axis_index("x"), lax.axis_index("y"), lax.axis_index("c")


def _aligned(v, n):
    return v if isinstance(v, int) else pl.multiple_of(v, n)


def _window(ref, r0, c0, R, C):
    rows = pl.ds(_aligned(r0, SUBLANES), R)
    if C == ref.shape[1]:
        return ref.at[rows, :]
    return ref.at[rows, pl.ds(_aligned(c0, LANES), C)]


def _allgather8(items, name):
    n = len(items)

    def body(*refs):
        srcs, dsts = refs[:n], refs[n:2 * n]
        send_sems, recv_sems, local_sems = refs[2 * n:]
        x, y, c = _me()
        me, sibling = (x, y, c), (x, y, 1 - c)
        chips = [(1 - x, y), (x, 1 - y), (1 - x, 1 - y)]

        def dwin(a, blk):
            (R, C), at = items[a][2], items[a][4]
            return _window(dsts[a], *at(*blk), R, C)

        def swin(a):
            (R, C), at = items[a][2], items[a][1]
            return _window(srcs[a], *at(*me), R, C)

        def copy(a, k, blk, to, src=None):
            return pltpu.make_async_remote_copy(
                src_ref=dwin(a, blk) if src is None else src, dst_ref=dwin(a, blk), send_sem=send_sems.at[7 * a + k],
                recv_sem=recv_sems.at[7 * a + k], device_id=to, device_id_type=MESH)

        mine = [pltpu.make_async_copy(swin(a), dwin(a, me), local_sems.at[a]) for a in range(n)]
        for cp in mine:
            cp.start()
        first = []
        for a in range(n):
            first.append(copy(a, 0, me, sibling, src=swin(a)))
            first += [copy(a, 1 + j, me, (*chip, c), src=swin(a)) for j, chip in enumerate(chips)]
        for cp in first:
            cp.start()
        passed = []
        for j, chip in enumerate(chips):
            for a in range(n):
                copy(a, 1 + j, (*chip, c), me).wait_recv()
                fwd = copy(a, 4 + j, (*chip, c), sibling)
                fwd.start()
                passed.append(fwd)
        for a in range(n):
            copy(a, 0, sibling, me).wait_recv()
            for j, chip in enumerate(chips):
                copy(a, 4 + j, (*chip, 1 - c), me).wait_recv()
        for cp in first + passed:
            cp.wait_send()
        for cp in mine:
            cp.wait()

    outs = pl.pallas_call(
        body, name=name, out_shape=tuple(jax.ShapeDtypeStruct(it[3], it[0].dtype) for it in items),
        in_specs=[ANY] * n, out_specs=tuple([ANY] * n),
        scratch_shapes=[pltpu.SemaphoreType.DMA((7 * n,)), pltpu.SemaphoreType.DMA((7 * n,)), pltpu.SemaphoreType.DMA((n,))],
    )(*[it[0] for it in items])
    return list(outs)


def _cast_place(w, kind, name, lead=None):
    R2, C = w.shape[-2:]
    tr = _row_tile(R2, C)
    nr = R2 // tr
    xi, yi, _ = _me()
    p_arr = (2 * xi + yi).astype(jnp.int32).reshape(1)
    if kind == "col":
        shape, o_spec = (R2, 4 * C), BS((tr, C), lambda i, p: (i, p[0]))
    else:
        shape, o_spec = (4 * R2, C), BS((tr, C), lambda i, p: (p[0] * nr + i, 0))

    def body(p_ref, w_ref, o_ref):
        o_ref[...] = w_ref[...].astype(BF16)

    return pl.pallas_call(
        body, name=name, out_shape=jax.ShapeDtypeStruct(shape, BF16),
        grid_spec=pltpu.PrefetchScalarGridSpec(
            num_scalar_prefetch=1, grid=(nr,), out_specs=o_spec,
            in_specs=[BS((tr, C), lambda i, p: (i, 0)) if lead is None else BS((None, tr, C), lambda i, p: (lead, i, 0))]),
        compiler_params=_cp(("parallel",)),
    )(p_arr, w)


def _remote(src, dst, sends, recvs, k, to):
    return pltpu.make_async_remote_copy(src_ref=src, dst_ref=dst, send_sem=sends.at[k], recv_sem=recvs.at[k], device_id=to,
                                        device_id_type=MESH)


def _gather_side(bufs, pieces, passed=(), pass_now=False):
    later = list(passed) + (list(pieces) if pass_now else [])
    base = 3 * len(pieces)

    def win(b, piece, p, pc):
        bi, (R, C), at, k, i = piece
        r0, c0 = at(p, pc)
        return _window(b[bi], r0 + i * (R // k), c0, R // k, C)

    def chips_of(x, y):
        return [(1 - x, y), (x, 1 - y), (1 - x, 1 - y)]

    def over_ici(b, sends, recvs, landing):
        x, y, c = _me()
        cps = []
        for m, piece in enumerate(pieces):
            for j, (px, py) in enumerate(chips_of(x, y)):
                w = win(b, piece, 2 * px + py, c) if landing else win(b, piece, 2 * x + y, c)
                cps.append(_remote(w, w, sends, recvs, 3 * m + j, (px, py, c)))
        return cps

    def to_sibling(b, sends, recvs, todo, landing):
        x, y, c = _me()
        cps = []
        for m, piece in enumerate(later):
            if piece in todo:
                for j, (px, py) in enumerate(chips_of(x, y)):
                    w = win(b, piece, 2 * px + py, 1 - c if landing else c)
                    cps.append(_remote(w, w, sends, recvs, base + 3 * m + j, (x, y, 1 - c)))
        return cps

    def start(b, sends, recvs):
        for cp in over_ici(b, sends, recvs, False) + to_sibling(b, sends, recvs, passed, False):
            cp.start()

    def finish(b, sends, recvs):
        for cp in over_ici(b, sends, recvs, True):
            cp.wait_recv()
        if pass_now:
            for cp in to_sibling(b, sends, recvs, pieces, False):
                cp.start()
        for cp in to_sibling(b, sends, recvs, later, True):
            cp.wait_recv()
        for cp in over_ici(b, sends, recvs, False) + to_sibling(b, sends, recvs, later, False):
            cp.wait_send()

    return _Side(bufs, 3 * (len(pieces) + len(later)), start, finish)


_RELS = [(dx, dy, dc) for dx in (0, 1) for dy in (0, 1) for dc in (0, 1)][1:]


def _rs_side(bufs, pieces):
    def flip(v, d):
        return 1 - v if d else v

    def copies(b, sends, recvs, landing):
        x, y, c = _me()
        dev = 4 * x + 2 * y + c
        cps = []
        for m, (gi, ri, (R, C), at, k, i) in enumerate(pieces):
            rows = R // k
            for t, (dx, dy, dc) in enumerate(_RELS):
                tx, ty, tc = flip(x, dx), flip(y, dy), flip(c, dc)
                if landing:
                    theirs = b[ri].at[4 * tx + 2 * ty + tc, pl.ds(i * rows, rows), :]
                    cps.append(_remote(theirs, theirs, sends, recvs, 7 * m + t, (tx, ty, tc)))
                else:
                    r0, c0 = at(2 * tx + ty, tc)
                    src = _window(b[gi], r0 + i * rows, c0, rows, C)
                    cps.append(_remote(src, b[ri].at[dev, pl.ds(i * rows, rows), :], sends, recvs, 7 * m + t, (tx, ty, tc)))
        return cps

    def start(b, sends, recvs):
        for cp in copies(b, sends, recvs, False):
            cp.start()

    def finish(b, sends, recvs):
        for cp in copies(b, sends, recvs, True):
            cp.wait_recv()
        for cp in copies(b, sends, recvs, False):
            cp.wait_send()

    return _Side(bufs, 7 * len(pieces), start, finish)


def _comm_only(side, name):
    n = len(side.bufs)

    def body(*refs):
        bufs, (sends, recvs) = refs[n:2 * n], refs[2 * n:]
        side.start(bufs, sends, recvs)
        side.finish(bufs, sends, recvs)

    outs = pl.pallas_call(
        body, name=name, out_shape=tuple(jax.ShapeDtypeStruct(b.shape, b.dtype) for b in side.bufs),
        in_specs=[ANY] * n, out_specs=tuple([ANY] * n), input_output_aliases={a: a for a in range(n)},
        scratch_shapes=[pltpu.SemaphoreType.DMA((side.nsem,)), pltpu.SemaphoreType.DMA((side.nsem,))],
    )(*side.bufs)
    return list(outs)


def _rs_sum8(grad, recv, win, kind, name):
    R, C = win
    tr = _row_tile(R, C)
    nr = R // tr
    xi, yi, ci = _me()
    s_arr = jnp.stack([4 * xi + 2 * yi + ci, ci]).astype(jnp.int32)

    if kind == "col":
        g_spec = BS((tr, C), lambda i, s: (s[1] * nr + i, s[0] // 2))
    else:
        g_spec = BS((tr, C), lambda i, s: (s[0] * nr + i, 0))

    def body(s_ref, g_ref, *refs):
        acc = g_ref[...].astype(F32)
        for r_ref in refs[:7]:
            acc = acc + r_ref[0].astype(F32)
        refs[7][...] = acc

    def other(t):
        return BS((1, tr, C), lambda i, s: (jnp.bitwise_xor(s[0], t), i, 0))

    return pl.pallas_call(
        body, name=name, out_shape=jax.ShapeDtypeStruct((2 * R, C), F32),
        grid_spec=pltpu.PrefetchScalarGridSpec(
            num_scalar_prefetch=1, grid=(nr,), in_specs=[g_spec] + [other(t) for t in range(1, 8)],
            out_specs=BS((tr, C), lambda i, s: (s[1] * nr + i, 0))),
        compiler_params=_cp(("parallel",)),
    )(s_arr, grad, *([recv] * 7))


def _share_side(shards):
    def half(b, a, pc):
        R = b[a].shape[0] // 2
        return b[a].at[pl.ds(pl.multiple_of(pc * R, SUBLANES), R), :]

    def start(b, sends, recvs):
        x, y, c = _me()
        for a in range(len(shards)):
            _remote(half(b, a, c), half(b, a, c), sends, recvs, a, (x, y, 1 - c)).start()

    def finish(b, sends, recvs):
        x, y, c = _me()
        for a in range(len(shards)):
            _remote(half(b, a, 1 - c), half(b, a, 1 - c), sends, recvs, a, (x, y, 1 - c)).wait_recv()
        for a in range(len(shards)):
            _remote(half(b, a, c), half(b, a, c), sends, recvs, a, (x, y, 1 - c)).wait_send()

    return _Side(shards, len(shards), start, finish)


BLOB_ALIGN = SUBLANES * LANES


def _pack(arrs):
    flat = jnp.concatenate([a.reshape(-1).astype(F32) for a in arrs])
    n = flat.shape[0]
    padded = -(-n // BLOB_ALIGN) * BLOB_ALIGN
    return jnp.pad(flat, (0, padded - n)).reshape(padded // LANES, LANES)


def _unpack(flat, shapes):
    out, off = [], 0
    for s in shapes:
        n = math.prod(s)
        out.append(flat[..., off:off + n].reshape(flat.shape[:-1] + tuple(s)))
        off += n
    return out


def _gather_blob(blob, name):
    r = blob.shape[0]
    at = lambda px, py, pc: ((4 * px + 2 * py + pc) * r, 0)
    (out,) = _allgather8([(blob, lambda px, py, pc: (0, 0), (r, LANES), (8 * r, LANES), at)], name)
    return out.reshape(8, r * LANES)


def _conv_ffn_fwd(mm, full, layer, X, mods, n2g, conv_w, conv_b, L, tb, nlat, ch, tag):
    sh2, sc2, g2 = mods[3], mods[4], mods[5]
    h2 = _norm_mod_fwd(X, n2g, sh2, sc2, tb, nlat, tag + "_norm2")
    zf = mm(h2, full[f"ffn_w_up{layer}"], "nn", BF16, tag + "_up", tm=(544, 512), tn=(2816, 512))
    f = _ffn_act_fwd(zf, conv_w, conv_b, L, ch, tag + "_act")
    Xn, yf = mm(f, full[f"ffn_w_down{layer}"], "nn", F32, tag + "_down", tm=(544, 512), tn=(512,), res=(X, g2, L))
    return Xn, (X, h2, zf, f, yf)


def _conv_ffn_bwd(mm, full, gbuf, layer, dXn, dy, dg2, before, saved, mods, n2g, conv_w, conv_b, L, tb, nlat, ch, tag):
    X, h2, zf, f, _ = saved
    sc2 = mods[4]
    w_up, w_down = full[f"ffn_w_up{layer}"], full[f"ffn_w_down{layer}"]
    df = mm(dy, w_down, "nt", BF16, tag + "_ddown_x", tm=(544, 512), tn=(2816, 512))
    gbuf[f"ffn_w_down{layer}"] = mm(f, dy, "tn", BF16, tag + "_ddown_w", tm=(512,), tn=(1024, 512))
    dzf, dcw, dcb = _ffn_act_bwd(zf, df, conv_w, conv_b, L, ch, tag + "_dact")
    gbuf[f"ffn_w_up{layer}"] = mm(h2, dzf, "tn", BF16, tag + "_dup_w", tm=(1024, 512), tn=(512,), outer="i", halves=True)
    dh2 = mm(dzf, w_up, "nt", BF16, tag + "_dup_x", tm=(1088, 1024, 512), tn=(1024,), tk=(2816, 512), halves=True)
    dX, dn2g, dsh2, dsc2, dy_before, dg_before = _norm_mod_bwd(X, n2g, sc2, dh2, dXn, tb, nlat, tag + "_dnorm2", before)
    return dX, dict(n2g=dn2g, sh2=dsh2, sc2=dsc2, g2=dg2, cw=dcw, cb=dcb), (dy_before, dg_before)


def kernel(x, c, ctx, c_ctx, norm1_g, norm2_g, w_ada, b_ada, ab_w_in, ab_b_in, a_ln_g, a_ln_b, a_w_s, a_b_s, b_conv_w, b_conv_b, b_ln_g, b_ln_b, ab_w_out, mla_w_in, mla_q_norm_g, mla_w_uq, mla_kv_norm_g, mla_w_ukv, mla_w_o, ffn_w_up, ffn_conv_w, ffn_conv_b, ffn_w_down, final_norm_g, loss_target, m_c_ctx, m_norm1_g, m_norm2_g, m_w_ada, m_b_ada, m_ab_w_in, m_ab_b_in, m_a_ln_g, m_a_ln_b, m_a_w_s, m_a_b_s, m_b_conv_w, m_b_conv_b, m_b_ln_g, m_b_ln_b, m_ab_w_out, m_mla_w_in, m_mla_q_norm_g, m_mla_w_uq, m_mla_kv_norm_g, m_mla_w_ukv, m_mla_w_o, m_ffn_w_up, m_ffn_conv_w, m_ffn_conv_b, m_ffn_w_down, m_final_norm_g, v_c_ctx, v_norm1_g, v_norm2_g, v_w_ada, v_b_ada, v_ab_w_in, v_ab_b_in, v_a_ln_g, v_a_ln_b, v_a_w_s, v_a_b_s, v_b_conv_w, v_b_conv_b, v_b_ln_g, v_b_ln_b, v_ab_w_out, v_mla_w_in, v_mla_q_norm_g, v_mla_w_uq, v_mla_kv_norm_g, v_mla_w_ukv, v_mla_w_o, v_ffn_w_up, v_ffn_conv_w, v_ffn_conv_b, v_ffn_w_down, v_final_norm_g):
    W = dict(c_ctx=c_ctx, norm1_g=norm1_g, norm2_g=norm2_g, w_ada=w_ada, b_ada=b_ada, ab_w_in=ab_w_in, ab_b_in=ab_b_in, a_ln_g=a_ln_g,
             a_ln_b=a_ln_b, a_w_s=a_w_s, a_b_s=a_b_s, b_conv_w=b_conv_w, b_conv_b=b_conv_b, b_ln_g=b_ln_g, b_ln_b=b_ln_b,
             ab_w_out=ab_w_out, mla_w_in=mla_w_in, mla_q_norm_g=mla_q_norm_g, mla_w_uq=mla_w_uq, mla_kv_norm_g=mla_kv_norm_g,
             mla_w_ukv=mla_w_ukv, mla_w_o=mla_w_o, ffn_w_up=ffn_w_up, ffn_conv_w=ffn_conv_w, ffn_conv_b=ffn_conv_b,
             ffn_w_down=ffn_w_down, final_norm_g=final_norm_g)
    MOM = dict(c_ctx=m_c_ctx, norm1_g=m_norm1_g, norm2_g=m_norm2_g, w_ada=m_w_ada, b_ada=m_b_ada, ab_w_in=m_ab_w_in, ab_b_in=m_ab_b_in,
               a_ln_g=m_a_ln_g, a_ln_b=m_a_ln_b, a_w_s=m_a_w_s, a_b_s=m_a_b_s, b_conv_w=m_b_conv_w, b_conv_b=m_b_conv_b,
               b_ln_g=m_b_ln_g, b_ln_b=m_b_ln_b, ab_w_out=m_ab_w_out, mla_w_in=m_mla_w_in, mla_q_norm_g=m_mla_q_norm_g,
               mla_w_uq=m_mla_w_uq, mla_kv_norm_g=m_mla_kv_norm_g, mla_w_ukv=m_mla_w_ukv, mla_w_o=m_mla_w_o, ffn_w_up=m_ffn_w_up,
               ffn_conv_w=m_ffn_conv_w, ffn_conv_b=m_ffn_conv_b, ffn_w_down=m_ffn_w_down, final_norm_g=m_final_norm_g)
    VAR = dict(c_ctx=v_c_ctx, norm1_g=v_norm1_g, norm2_g=v_norm2_g, w_ada=v_w_ada, b_ada=v_b_ada, ab_w_in=v_ab_w_in, ab_b_in=v_ab_b_in,
               a_ln_g=v_a_ln_g, a_ln_b=v_a_ln_b, a_w_s=v_a_w_s, a_b_s=v_a_b_s, b_conv_w=v_b_conv_w, b_conv_b=v_b_conv_b,
               b_ln_g=v_b_ln_g, b_ln_b=v_b_ln_b, ab_w_out=v_ab_w_out, mla_w_in=v_mla_w_in, mla_q_norm_g=v_mla_q_norm_g,
               mla_w_uq=v_mla_w_uq, mla_kv_norm_g=v_mla_kv_norm_g, mla_w_ukv=v_mla_w_ukv, mla_w_o=v_mla_w_o, ffn_w_up=v_ffn_w_up,
               ffn_conv_w=v_ffn_conv_w, ffn_conv_b=v_ffn_conv_b, ffn_w_down=v_ffn_w_down, final_norm_g=v_final_norm_g)
    ORDER = list(W.keys())

    L, D = x.shape[1], x.shape[2]
    CT = ctx.shape[1]
    T = L + CT
    AW, BW = a_ln_g.shape[-1], b_ln_g.shape[-1]
    AH = a_w_s.shape[1]
    QL, KVL = 4 * mla_q_norm_g.shape[-1], 4 * mla_kv_norm_g.shape[-1]
    H = 4 * mla_w_o.shape[1] // HEAD
    HS = H // 4
    DFF = ffn_conv_b.shape[-1]
    NA = w_ada.shape[-1]
    tb = 256 if (L % 256 == 0 and CT % 256 == 0) else 128
    nlat = L // tb
    ch = tb
    tq = 256 if L >= 512 else 128
    xi, yi, ci = _me()
    p_me = 2 * xi + yi
    dev = 4 * xi + 2 * yi + ci

    shard_small = [c[0], mla_q_norm_g[0], mla_kv_norm_g[0], b_conv_w[0], ffn_conv_w]
    g0 = _gather_blob(_pack(shard_small), "gather_small")
    c_all, qg_s, kvg_s, bcw_s, fcw_s = _unpack(g0, [a.shape for a in shard_small])
    per_chip = lambda a: a[0::2]
    qg = per_chip(qg_s).reshape(QL)
    kvg = per_chip(kvg_s).reshape(KVL)
    bcw = jnp.concatenate(list(per_chip(bcw_s)), axis=-1)
    fcw = jnp.concatenate(list(per_chip(fcw_s)), axis=-1)
    c16 = jnp.concatenate([c_all, c_ctx[None], jnp.zeros((7, D), F32)], axis=0)

    ms = []
    for i in range(2):
        bias = lax.dynamic_slice(b_ada[i], (p_me * NA,), (NA,))
        ms.append(_mm(c16, w_ada, "nn", F32, f"ada{i}", tm=(16,), tn=(512,), bias=bias, silu_a=True, b_lead=i))
    ms = jnp.concatenate(ms, axis=0)
    (mods_all,) = _allgather8(
        [(ms, lambda px, py, pc: (pc * 16, 0), (16, NA), (32, 4 * NA), lambda px, py, pc: (pc * 16, (2 * px + py) * NA))], "gather_mods")
    mods_all = mods_all.reshape(2, 16, N_MOD, D)
    mods = []
    for i in range(2):
        lat = lax.dynamic_index_in_dim(mods_all[i], dev, axis=0, keepdims=False)
        both = jnp.stack([lat, mods_all[i, 8]], axis=0)
        mods.append([both[:, k][:, None, :] for k in range(N_MOD)])

    def pad_uq(w):
        w = w.reshape(w.shape[0], HS, HEAD + ROPE)
        return jnp.pad(w, ((0, 0), (0, 0), (0, QHEAD - HEAD - ROPE))).reshape(w.shape[0], HS * QHEAD)

    MI = QL + KVL + LANES
    big = [
        ("ab_w_in", ab_w_in, 0, "col"), ("ab_w_out", ab_w_out, 0, "row"),
        ("mla_w_in", jnp.pad(mla_w_in[0], ((0, 0), (0, MI - mla_w_in.shape[-1]))), None, "row"),
        ("mla_w_uq", pad_uq(mla_w_uq[0]), None, "col"), ("mla_w_ukv", mla_w_ukv, 0, "col"), ("mla_w_o", mla_w_o, 0, "row"),
        ("ffn_w_up0", ffn_w_up, 0, "col"), ("ffn_w_up1", ffn_w_up, 1, "col"),
        ("ffn_w_down0", ffn_w_down, 0, "row"), ("ffn_w_down1", ffn_w_down, 1, "row"),
    ]
    full, wins = {}, {}
    for nm, w, lead, kind in big:
        R, C = w.shape[-2] // 2, w.shape[-1]
        if kind == "col":
            wins[nm] = ((R, C), lambda p, pc, R=R, C=C: (pc * R, p * C), kind)
        else:
            wins[nm] = ((R, C), lambda p, pc, R=R, C=C: ((2 * p + pc) * R, 0), kind)
        full[nm] = _cast_place(w, kind, "cast_" + nm, lead)
    names = [b[0] for b in big]

    up0 = lambda *parts: [("ffn_w_up0", 8, i) for i in parts]
    mla_a = [("ffn_w_down0", 1, 0), ("mla_w_in", 1, 0), ("mla_w_uq", 1, 0)]
    mla_b = [("mla_w_ukv", 1, 0), ("mla_w_o", 1, 0)]
    ffn1_w = [("ffn_w_up1", 1, 0), ("ffn_w_down1", 1, 0)]
    gather_plan = {
        "l0_norm1": ([("ab_w_out", 1, 0)], []), "l0_in": (up0(0, 1), [("ab_w_out", 1, 0)]), "l0_gmlp": (up0(2), up0(0, 1)),
        "l0_conv": (up0(3, 4), up0(2)), "l0_lnsilu": (up0(5), up0(3, 4)), "l0_out": (up0(6, 7), up0(5)),
        "l0_ffn_norm2": ([], up0(6, 7)), "l0_ffn_up": (mla_a, []), "l0_ffn_act": ([], mla_a), "l0_ffn_down": (mla_b, []),
        "l1_norm1": ([], mla_b), "l1_attn": (ffn1_w, []), "l1_o": ([], ffn1_w),
    }
    u0 = lambda *parts: [("ffn_w_up0", 16, i) for i in parts]
    reduce_plan = {
        "l1_dattn": [("ffn_w_down1", 1, 0), ("ffn_w_up1", 1, 0), ("mla_w_o", 1, 0)],
        "l0_ffn_ddown_x": [("mla_w_ukv", 1, 0), ("mla_w_uq", 1, 0)], "l0_ffn_ddown_w": [("mla_w_in", 1, 0)],
        "l0_ffn_dact": [("ffn_w_down0", 2, 0)], "l0_ffn_dup_w": [("ffn_w_down0", 2, 1)],
        "l0_ffn_dup_x": u0(0, 1, 2, 3, 4, 5, 6), "l0_ffn_dnorm2": u0(7, 8), "l0_dout_w": u0(9),
        "l0_dout_x": [("ab_w_out", 2, 0)], "l0_dgmlp": u0(10, 11), "l0_dlnsilu": u0(12),
        "l0_dconv": u0(13, 14) + [("ab_w_out", 2, 1)], "l0_din_w": u0(15),
        "l0_din_x": [("ab_w_in", 4, 0), ("ab_w_in", 4, 1), ("ab_w_in", 4, 2)], "l0_dnorm1": [("ab_w_in", 4, 3)],
    }
    gbuf, rbuf = {}, {}

    def gather_side(req, passed=(), pass_now=False):
        nms = list(dict.fromkeys(nm for nm, _, _ in list(req) + list(passed)))
        piece = lambda nm, k, i: (nms.index(nm), wins[nm][0], wins[nm][1], k, i)
        side = _gather_side([full[nm] for nm in nms], [piece(*r) for r in req], [piece(*r) for r in passed], pass_now)

        def commit(bufs):
            full.update(zip(nms, bufs))

        return side, commit

    def reduce_side(req):
        nms = list(dict.fromkeys(nm for nm, _, _ in req))
        for nm in nms:
            if nm not in rbuf:
                rbuf[nm] = lax.empty((8,) + wins[nm][0], BF16)
        n = len(nms)
        side = _rs_side([gbuf[nm] for nm in nms] + [rbuf[nm] for nm in nms],
                        [(nms.index(nm), n + nms.index(nm), wins[nm][0], wins[nm][1], k, i) for nm, k, i in req])

        def commit(bufs):
            gbuf.update(zip(nms, bufs[:n]))
            rbuf.update(zip(nms, bufs[n:]))

        return side, commit

    _carry_plan.clear()
    _carry_plan.update({name: functools.partial(gather_side, req, passed) for name, (req, passed) in gather_plan.items()})
    _carry_plan.update({name: functools.partial(reduce_side, req) for name, req in reduce_plan.items()})
    mm, attn_fwd, attn_bwd = _mm, _attn_fwd, _attn_bwd

    side, commit = gather_side([("ab_w_in", 1, 0)], (), True)
    commit(_comm_only(side, "gather_first"))

    X0 = jnp.concatenate([x[0], ctx[0]], axis=0)
    m0, m1 = mods[0], mods[1]
    h1 = _norm_mod_fwd(X0, norm1_g[0], m0[0], m0[1], tb, nlat, "l0_norm1")
    z = mm(h1, full["ab_w_in"], "nn", BF16, "l0_in", tm=(544, 512), tn=(1024, 512), bias=ab_b_in[0])
    bs_full = jnp.broadcast_to(a_b_s[0][:, :, None], (AH, CHUNK, CHUNK))
    ya = _gmlp_fwd(z, a_ln_g[0], a_ln_b[0], a_w_s[0], bs_full, tb, "l0_gmlp")
    hc = _glu_conv_fwd(z, 2 * AW, bcw, b_conv_b[0], L, ch, "l0_conv")
    yb = _ln_silu_fwd(hc, b_ln_g[0], b_ln_b[0], tb, "l0_lnsilu")
    yab = jnp.concatenate([ya, yb], axis=1)
    X1, y0 = mm(yab, full["ab_w_out"], "nn", F32, "l0_out", tm=(544, 512), tn=(1024, 512), res=(X0, m0[2], L))
    X2, ffn0 = _conv_ffn_fwd(mm, full, 0, X1, m0, norm2_g[0], fcw[0], ffn_conv_b[0], L, tb, nlat, ch, "l0_ffn")

    tabs = _rope_tables(L, T)
    hm = _norm_mod_fwd(X2, norm1_g[1], m1[0], m1[1], tb, nlat, "l1_norm1")
    zm = mm(hm, full["mla_w_in"], "nn", F32, "l1_in", tm=(544, 512), tn=(MI,))
    cqn, ckvn, kpe = _mla_prep_fwd(zm, qg, kvg, tabs, tb, "l1_prep")
    q = mm(cqn, full["mla_w_uq"], "nn", BF16, "l1_uq", tm=(512,), tn=(1024, 512), rows=L)
    kvh = mm(ckvn, full["mla_w_ukv"], "nn", BF16, "l1_ukv", tm=(544, 512), tn=(1024, 512))
    tq2 = 2 * tq if L % (2 * tq) == 0 and L > 2 * tq else tq
    wide = L % (4 * tq) == 0 and L > 4 * tq
    o, lse = attn_fwd(q, kvh, kpe, tabs, L, 4 * tq if wide else tq2, 4 if wide else 2, "l1_attn")
    m1_lat = [a[:1] for a in m1]
    X3, yl = mm(o, full["mla_w_o"], "nn", F32, "l1_o", tm=(512,), tn=(1024, 512), res=(X2, m1_lat[2], L))
    X4, ffn1 = _conv_ffn_fwd(mm, full, 1, X3, m1_lat, norm2_g[1], fcw[1], ffn_conv_b[1], L, tb, nlat, ch, "l1_ffn")
    loss_acc, dX4, dfinal, dyf1, dg2_1 = _final_loss(X4, final_norm_g, loss_target[0], (ffn1[4], m1_lat[5]), tb, "loss")

    dX3, gf1, (dyl, dg1_1) = _conv_ffn_bwd(mm, full, gbuf, 1, dX4, dyf1, dg2_1, (yl, m1_lat[2]), ffn1, m1_lat, norm2_g[1], fcw[1],
                                           ffn_conv_b[1], L, tb, nlat, ch, "l1_ffn")
    do = mm(dyl, full["mla_w_o"], "nt", BF16, "l1_do_x", tm=(512,), tn=(1024, 512))
    gbuf["mla_w_o"] = mm(o, dyl, "tn", BF16, "l1_do_w", tm=(512,), tn=(1024, 512))
    dq, dkv, dkpe = attn_bwd(q, kvh, kpe, tabs, o, lse, do, L, tq2, "l1_dattn")
    dckvn = mm(dkv, full["mla_w_ukv"], "nt", BF16, "l1_dukv_x", tm=(544, 512), tn=(KVL,), tk=(2048, 512))
    gbuf["mla_w_ukv"] = mm(ckvn, dkv, "tn", BF16, "l1_dukv_w", tm=(512,), tn=(1024, 512))
    dcqn = mm(dq, full["mla_w_uq"], "nt", BF16, "l1_duq_x", tm=(512,), tn=(QL,), tk=(2048, 512))
    gbuf["mla_w_uq"] = mm(cqn, dq, "tn", BF16, "l1_duq_w", tm=(QL,), tn=(1024, 512), rows=L)
    dcqn = jnp.concatenate([dcqn, jnp.zeros((CT, QL), BF16)], axis=0)
    dzm, dqg, dkvg = _mla_prep_bwd(zm, qg, kvg, tabs, dcqn, dckvn, dkpe, tb, "l1_dprep")
    dhm = mm(dzm, full["mla_w_in"], "nt", BF16, "l1_din_x", tm=(544, 512), tn=(1024, 512))
    gbuf["mla_w_in"] = mm(hm, dzm, "tn", BF16, "l1_din_w", tm=(512,), tn=(MI,))
    dX2, dn1g_1, dsh1_1, dsc1_1, dyf0, dg2_0 = _norm_mod_bwd(X2, norm1_g[1], m1[1], dhm, dX3, tb, nlat, "l1_dnorm1", (ffn0[4], m0[5]))

    dX1, gf0, (dy0, dg1_0) = _conv_ffn_bwd(mm, full, gbuf, 0, dX2, dyf0, dg2_0, (y0, m0[2]), ffn0, m0, norm2_g[0], fcw[0],
                                           ffn_conv_b[0], L, tb, nlat, ch, "l0_ffn")
    gbuf["ab_w_out"] = mm(yab, dy0, "tn", BF16, "l0_dout_w", tm=(512,), tn=(1024, 512))
    dyab = mm(dy0, full["ab_w_out"], "nt", BF16, "l0_dout_x", tm=(544, 512), tn=(1024, 512))
    dzu, dzv, dlnag, dlnab, dws, dbs, csu, csv = _gmlp_bwd(z, dyab, a_ln_g[0], a_ln_b[0], a_w_s[0], bs_full, tb, "l0_dgmlp")
    dhc, dlnbg, dlnbb = _ln_silu_bwd(hc, dyab, AW, b_ln_g[0], b_ln_b[0], tb, "l0_dlnsilu")
    dza, dzg, dbcw, dbcb, csa, csg = _glu_conv_bwd(z, 2 * AW, dhc, bcw, L, ch, "l0_dconv")
    dz = jnp.concatenate([dzu, dzv, dza, dzg], axis=1)
    dbin = jnp.concatenate([csu, csv, csa, csg], axis=1)
    gbuf["ab_w_in"] = mm(h1, dz, "tn", BF16, "l0_din_w", tm=(512,), tn=(1024, 512))
    dh1 = mm(dz, full["ab_w_in"], "nt", BF16, "l0_din_x", tm=(544, 512), tn=(1024, 512), tk=(2048, 512))
    dX0, dn1g_0, dsh1_0, dsc1_0 = _norm_mod_bwd(X0, norm1_g[0], m0[1], dh1, dX1, tb, nlat, "l0_dnorm1")
    grad_x = dX0[:L][None]

    halves = [_rs_sum8(gbuf[nm], rbuf[nm], wins[nm][0], wins[nm][2], "rs_sum_" + nm) for nm in names]

    def grp6(l, sh1, sc1, g1, f):
        G = sh1.shape[0]
        pad = lambda a: jnp.concatenate([a, jnp.zeros((G - a.shape[0],) + a.shape[1:], F32)], axis=0) if a.shape[0] < G else a
        return jnp.concatenate([pad(a) for a in (sh1, sc1, g1, f["sh2"], f["sc2"], f["g2"])], axis=1)

    dm0 = grp6(0, dsh1_0, dsc1_0, dg1_0, gf0)
    dm1 = grp6(1, dsh1_1, dsc1_1, dg1_1, gf1)
    dmods = jnp.stack([dm0, dm1], axis=0)
    small = [
        jnp.concatenate([dn1g_0, dn1g_1], axis=0), jnp.concatenate([gf0["n2g"], gf1["n2g"]], axis=0), dbin, dlnag, dlnab, dws,
        dbs, dbcw, dbcb, dlnbg, dlnbb, dqg, dkvg, jnp.stack([gf0["cw"], gf1["cw"]], axis=0),
        jnp.concatenate([gf0["cb"], gf1["cb"]], axis=0), dfinal, dmods[:, 1],
    ]
    small_shapes = [a.shape for a in small]
    lat_shape = dmods[:, 0].shape
    blob = _pack(small + [dmods[:, 0], loss_acc[0, :1]])
    gathered = _gather_blob(blob, "gather_grads")
    summed = _sum_lead(gathered.reshape(8, -1, LANES), F32, "sum_grads").reshape(-1)
    (dn1g, dn2g, dbin_s, dlnag_s, dlnab_s, dws_s, dbs_s, dbcw_s, dbcb_s, dlnbg_s, dlnbb_s, dqg_s, dkvg_s, dfcw_s, dfcb_s, dfinal_s,
     dmods_ctx, dmods_lat_sum, loss) = _unpack(summed, small_shapes + [lat_shape, (1,)])
    loss = loss.reshape(())
    n_small = sum(math.prod(s) for s in small_shapes)
    dmods_lat = gathered[:, n_small:n_small + math.prod(lat_shape)].reshape((8,) + lat_shape)

    grad_w_ada, dc_parts = [], []
    for i in range(2):
        dm16 = jnp.concatenate([dmods_lat[:, i].reshape(8, N_MOD * D), dmods_ctx[i].reshape(1, N_MOD * D),
                                jnp.zeros((7, N_MOD * D), F32)], axis=0)
        dm16_s = lax.dynamic_slice(dm16, (0, p_me * NA), (16, NA))
        grad_w_ada.append(_mm(c16, dm16_s, "tn", F32, f"dada{i}_w", tm=(1024, 512), tn=(1024, 512), silu_a=True))
        dc_parts.append(_mm(dm16_s, w_ada, "nt", F32, f"dada{i}_c", tm=(16,), tn=(512,), tk=(1024, 512), b_lead=i))
    grad_b_ada = dmods_lat_sum.reshape(2, N_MOD * D) + dmods_ctx.reshape(2, N_MOD * D)
    dc_blob = _pack([dc_parts[0][8] + dc_parts[1][8]])
    dc_all = _gather_blob(dc_blob, "gather_dc")[0::2, :D]
    grad_c_ctx = _cctx_grad(dc_all, c_ctx, "dcctx").reshape(D)

    gshard = dict(zip(names, _comm_only(_share_side(halves), "rs_share")))
    two = lambda a: a.reshape(-1, a.shape[-1])
    ada_update = _adamw_layers(w_ada, grad_w_ada, m_w_ada, v_w_ada, "adamw_w_ada")

    def my_cols(a, axis, n):
        return lax.dynamic_slice_in_dim(a, p_me * n, n, axis=axis)

    unpad_uq = lambda g: g.reshape(QL, HS, QHEAD)[:, :, :HEAD + ROPE].reshape(QL, HS * (HEAD + ROPE))
    grads = dict(
        c_ctx=grad_c_ctx, norm1_g=dn1g, norm2_g=dn2g, b_ada=grad_b_ada, ab_w_in=gshard["ab_w_in"][None],
        ab_b_in=dbin_s, a_ln_g=dlnag_s, a_ln_b=dlnab_s, a_w_s=dws_s[None], a_b_s=dbs_s.reshape(1, AH, CHUNK),
        b_conv_w=my_cols(dbcw_s, 1, BW // 4)[None], b_conv_b=dbcb_s, b_ln_g=dlnbg_s, b_ln_b=dlnbb_s,
        ab_w_out=gshard["ab_w_out"][None], mla_w_in=gshard["mla_w_in"][:, :mla_w_in.shape[-1]][None],
        mla_q_norm_g=my_cols(dqg_s, 1, QL // 4), mla_w_uq=unpad_uq(gshard["mla_w_uq"])[None],
        mla_kv_norm_g=my_cols(dkvg_s, 1, KVL // 4), mla_w_ukv=gshard["mla_w_ukv"][None], mla_w_o=gshard["mla_w_o"][None],
        ffn_conv_w=my_cols(dfcw_s, 2, DFF // 4), ffn_conv_b=dfcb_s, final_norm_g=dfinal_s.reshape(D),
    )
    LAYERED = ("w_ada", "ffn_w_up", "ffn_w_down")
    grads = {k: grads[k].reshape(W[k].shape) for k in ORDER if k not in LAYERED}

    BIG = ("ab_w_in", "ab_w_out", "mla_w_in", "mla_w_uq", "mla_w_ukv", "mla_w_o") + LAYERED
    delta, new_m, new_v = {}, {}, {}
    for k in BIG:
        if k == "w_ada":
            grads[k], delta[k], new_m[k], new_v[k] = ada_update
        elif k in LAYERED:
            grads[k], delta[k], new_m[k], new_v[k] = _adamw_layers(W[k], [gshard[k + "0"], gshard[k + "1"]], MOM[k], VAR[k], "adamw_" + k)
        else:
            upd = _adamw(two(W[k]), two(grads[k]), two(MOM[k]), two(VAR[k]), "adamw_" + k)
            delta[k], new_m[k], new_v[k] = (a.reshape(W[k].shape) for a in upd)
    SMALL = [k for k in ORDER if k not in BIG]
    small_upd = _adamw(_pack([W[k] for k in SMALL]), _pack([grads[k] for k in SMALL]), _pack([MOM[k] for k in SMALL]),
                       _pack([VAR[k] for k in SMALL]), "adamw_small")
    shapes = [W[k].shape for k in SMALL]
    for k, d_k, m_k, v_k in zip(SMALL, *[_unpack(a.reshape(-1), shapes) for a in small_upd]):
        delta[k], new_m[k], new_v[k] = d_k, m_k, v_k

    return (loss, grad_x, *[grads[k] for k in ORDER], *[delta[k] for k in ORDER], *[new_m[k] for k in ORDER],
            *[new_v[k] for k in ORDER])
```
